```python
import math
import jax
import jax.numpy as jnp
from jax import lax
import numpy as np

D_MODEL = 1024
BATCH = 2
SEQ = 8192
DEPTH = 2

SSM_WIDTH = D_MODEL // 2
SSM_GROUP = 16
SSM_GROUPS = SSM_WIDTH // SSM_GROUP
SSM_STATE = 64
GDN_HEADS = 4
GDN_DK = 128
GDN_DV = 128
GDN_CONV = 4
GDN_CHUNK = 64
NSA_HEADS = 8
NSA_KV_GROUPS = 2
NSA_HPG = NSA_HEADS // NSA_KV_GROUPS
NSA_DH = 64
NSA_CMP_LEN = 32
NSA_CMP_STRIDE = 16
NSA_CMP_HIDDEN = 128
NSA_SEL_LEN = 64
NSA_SEL_TOPK = 16
NSA_WINDOW = 512
NSA_QBLOCK = 128
SGU_WIDTH = D_MODEL // 2
SGU_GROUPS = 4
SGU_CHUNK = 128
REL_BUCKETS = 32
REL_MAX_DIST = 128
FFN_DENSE = 2816
N_EXPERTS = 8
TOP_K = 2
FFN_EXPERT = 3584
N_BRANCH = 4
N_DENSE_LAYERS = (DEPTH + 1) // 2
N_MOE_LAYERS = DEPTH // 2
RMS_EPS = 1e-6
GDN_QKV = GDN_HEADS * (2 * GDN_DK + GDN_DV)
NSA_Q = NSA_HEADS * NSA_DH
NSA_KV = NSA_KV_GROUPS * NSA_DH
IN_COLS = (SSM_WIDTH + GDN_QKV + 2 * GDN_HEADS + GDN_HEADS * GDN_DV + NSA_Q + 6 * NSA_KV
           + 3 * NSA_HEADS + 2 * SGU_WIDTH + N_BRANCH * D_MODEL)

kernel_name = 'hybrid_parallel_mixers_moe'


def rms_norm(x, g):
    xf = x.astype(jnp.float32)
    y = xf * lax.rsqrt(jnp.mean(xf * xf, axis=-1, keepdims=True) + RMS_EPS)
    return (y * g.astype(jnp.float32)).astype(x.dtype)


def l2_norm(t):
    return t * lax.rsqrt(jnp.sum(t * t, axis=-1, keepdims=True) + 1e-6)


def masked_softmax(logits, valid):
    logits = jnp.where(valid, logits.astype(jnp.float32), -1e30)
    return jax.nn.softmax(logits, axis=-1) * valid


def rel_bucket(dist):
    n = jnp.maximum(dist, 0)
    max_exact = REL_BUCKETS // 2
    nf = jnp.maximum(n, 1).astype(jnp.float32)
    large = max_exact + (jnp.log(nf / max_exact) / math.log(REL_MAX_DIST / max_exact)
                         * (REL_BUCKETS - max_exact)).astype(jnp.int32)
    large = jnp.minimum(large, REL_BUCKETS - 1)
    return jnp.where(n < max_exact, n, large)


def split_columns(proj):
    sizes = ([SSM_WIDTH, GDN_QKV, GDN_HEADS, GDN_HEADS, GDN_HEADS * GDN_DV, NSA_Q]
             + [NSA_KV] * 6 + [3 * NSA_HEADS, 2 * SGU_WIDTH, N_BRANCH * D_MODEL])
    cuts = [int(c) for c in np.cumsum(sizes)[:-1]]
    return jnp.split(proj, cuts, axis=-1)


def swiglu(h, w_gate, w_up, w_down):
    return (jax.nn.silu(h @ w_gate) * (h @ w_up)) @ w_down


def ssm_mixer(u, a_re, a_im, log_dt, b_re, b_im, c_re, c_im, d_skip, glu_w):
    f32 = jnp.float32
    bs, s, _ = u.shape
    uf = u.astype(f32).reshape(bs, s, SSM_GROUPS, SSM_GROUP)
    a_re, a_im = a_re.astype(f32), a_im.astype(f32)
    dt = jnp.exp(log_dt.astype(f32))[:, None]
    mag = jnp.exp(a_re * dt)
    abar_r, abar_i = mag * jnp.cos(a_im * dt), mag * jnp.sin(a_im * dt)
    nr, ni = abar_r - 1.0, abar_i
    den = a_re * a_re + a_im * a_im
    sr, si = (nr * a_re + ni * a_im) / den, (ni * a_re - nr * a_im) / den
    b_re, b_im = b_re.astype(f32), b_im.astype(f32)
    bbar_r = sr[..., None] * b_re - si[..., None] * b_im
    bbar_i = sr[..., None] * b_im + si[..., None] * b_re
    bu_r = jnp.einsum('bsgp,gnp->bsgn', uf, bbar_r)
    bu_i = jnp.einsum('bsgp,gnp->bsgn', uf, bbar_i)
    ar = jnp.broadcast_to(abar_r, bu_r.shape)
    ai = jnp.broadcast_to(abar_i, bu_r.shape)

    def combine(e1, e2):
        a1r, a1i, b1r, b1i = e1
        a2r, a2i, b2r, b2i = e2
        return (a2r * a1r - a2i * a1i, a2r * a1i + a2i * a1r,
                a2r * b1r - a2i * b1i + b2r, a2r * b1i + a2i * b1r + b2i)

    _, _, st_r, st_i = lax.associative_scan(combine, (ar, ai, bu_r, bu_i), axis=1)
    y = (jnp.einsum('gpn,bsgn->bsgp', c_re.astype(f32), st_r)
         - jnp.einsum('gpn,bsgn->bsgp', c_im.astype(f32), st_i)
         + d_skip.astype(f32).reshape(SSM_GROUPS, SSM_GROUP) * uf)
    y = jax.nn.gelu(y.reshape(bs, s, SSM_WIDTH))
    return (y * jax.nn.sigmoid(y @ glu_w.astype(f32))).astype(u.dtype)


def gdn_mixer(qkv, a, b, z, conv_w, a_log, dt_bias, norm_g):
    f32 = jnp.float32
    out_dtype = z.dtype
    bs, s, c = qkv.shape
    H, DK, DV, CH = GDN_HEADS, GDN_DK, GDN_DV, GDN_CHUNK
    qkv = lax.conv_general_dilated(qkv, conv_w.astype(qkv.dtype)[:, None, :], window_strides=(1,),
                                   padding=[(GDN_CONV - 1, 0)],
                                   dimension_numbers=('NWC', 'WIO', 'NWC'), feature_group_count=c)
    qkv = jax.nn.silu(qkv.astype(f32))
    q, k, v = jnp.split(qkv, [H * DK, 2 * H * DK], axis=-1)
    q = l2_norm(q.reshape(bs, s, H, DK)) * (DK ** -0.5)
    k = l2_norm(k.reshape(bs, s, H, DK))
    v = v.reshape(bs, s, H, DV)
    beta = jax.nn.sigmoid(b.astype(f32))
    log_alpha = -jnp.exp(a_log.astype(f32)) * jax.nn.softplus(a.astype(f32) + dt_bias.astype(f32))
    nc = s // CH

    def chunks(t):
        return t.reshape(bs, nc, CH, H, -1).transpose(0, 3, 1, 2, 4)

    q, k, v = chunks(q), chunks(k), chunks(v)
    beta = chunks(beta[..., None])
    g = jnp.cumsum(chunks(log_alpha[..., None]), axis=3)
    gv = g[..., 0]
    causal = jnp.tril(jnp.ones((CH, CH), bool))
    strict = jnp.tril(jnp.ones((CH, CH), bool), -1)
    decay = jnp.exp(jnp.where(causal, gv[..., :, None] - gv[..., None, :], -jnp.inf))
    kb = k * beta
    a_mat = (jnp.where(strict, jnp.einsum('bhncd,bhnkd->bhnck', kb, k) * decay, 0.0)
             + jnp.eye(CH, dtype=f32))
    rhs = jnp.concatenate([v * beta, kb * jnp.exp(g)], axis=-1)
    sol = lax.linalg.triangular_solve(a_mat, rhs, left_side=True, lower=True, unit_diagonal=True)
    u_val, w = sol[..., :DV], sol[..., DV:]
    attn = jnp.einsum('bhncd,bhnkd->bhnck', q, k) * decay
    g_last = g[..., -1:, :]
    k_st = k * jnp.exp(g_last - g)
    q_st = q * jnp.exp(g)

    def step(state, xs):
        q_c, k_c, u_c, w_c, a_c, gl_c = xs
        v_new = u_c - jnp.einsum('bhcd,bhde->bhce', w_c, state)
        o = jnp.einsum('bhcd,bhde->bhce', q_c, state) + jnp.einsum('bhck,bhke->bhce', a_c, v_new)
        state = state * jnp.exp(gl_c) + jnp.einsum('bhcd,bhce->bhde', k_c, v_new)
        return state, o

    xs = tuple(jnp.moveaxis(t, 2, 0) for t in (q_st, k_st, u_val, w, attn, g_last))
    _, o = lax.scan(step, jnp.zeros((bs, H, DK, DV), f32), xs)
    o = o.transpose(1, 0, 3, 2, 4).reshape(bs, s, H, DV)
    o = o * lax.rsqrt(jnp.mean(o * o, axis=-1, keepdims=True) + RMS_EPS) * norm_g.astype(f32)
    o = o * jax.nn.silu(z.astype(f32).reshape(bs, s, H, DV))
    return o.reshape(bs, s, H * DV).astype(out_dtype)


def nsa_mixer(q, k_c, v_c, k_s, v_s, k_w, v_w, gate_logits, positions,
              ck_pe, ck_w1, ck_w2, cv_pe, cv_w1, cv_w2, rel_bias):
    f32 = jnp.float32
    out_dtype = q.dtype
    bs, s, _ = q.shape
    G, HG, DH, QB, W = NSA_KV_GROUPS, NSA_HPG, NSA_DH, NSA_QBLOCK, NSA_WINDOW
    q = q.astype(f32).reshape(bs, s, G, HG, DH) * (DH ** -0.5)

    def kv(t):
        return t.astype(f32).reshape(bs, s, G, DH)

    k_c, v_c, k_s, v_s, k_w, v_w = (kv(t) for t in (k_c, v_c, k_s, v_s, k_w, v_w))
    gates = jax.nn.sigmoid(gate_logits.astype(f32)).reshape(bs, s, G, HG, 3)
    n_cmp = (s - NSA_CMP_LEN) // NSA_CMP_STRIDE + 1
    cmp_start = jnp.arange(n_cmp) * NSA_CMP_STRIDE
    cmp_end = cmp_start + NSA_CMP_LEN - 1
    cmp_tok = cmp_start[:, None] + jnp.arange(NSA_CMP_LEN)[None, :]

    def compress(t, pe, w1, w2):
        blk = t[:, cmp_tok] + pe.astype(f32)[None, None, :, None, :]
        blk = blk.transpose(0, 1, 3, 2, 4).reshape(bs, n_cmp, G, NSA_CMP_LEN * DH)
        return jax.nn.gelu(blk @ w1.astype(f32)) @ w2.astype(f32)

    kc = compress(k_c, ck_pe, ck_w1, ck_w2)
    vc = compress(v_c, cv_pe, cv_w1, cv_w2)
    cmp_pos = positions[cmp_end]
    n_sel = s // NSA_SEL_LEN
    sel_k = min(NSA_SEL_TOPK, n_sel)
    sel_start = jnp.arange(n_sel) * NSA_SEL_LEN
    agg = ((cmp_start[:, None] <= sel_start[None, :] + NSA_SEL_LEN - 1)
           & (cmp_end[:, None] >= sel_start[None, :])).astype(f32)
    ks_blk = k_s.reshape(bs, n_sel, NSA_SEL_LEN, G, DH).transpose(0, 3, 1, 2, 4)
    vs_blk = v_s.reshape(bs, n_sel, NSA_SEL_LEN, G, DH).transpose(0, 3, 1, 2, 4)
    pos_blk = positions.reshape(n_sel, NSA_SEL_LEN)
    kw_pad = jnp.pad(k_w, ((0, 0), (W, 0), (0, 0), (0, 0)))
    vw_pad = jnp.pad(v_w, ((0, 0), (W, 0), (0, 0), (0, 0)))
    pos_pad = jnp.pad(positions, (W, 0))
    tbl = rel_bias.astype(f32).reshape(REL_BUCKETS, G, HG)
    b_ix = jnp.arange(bs)[:, None, None, None]
    g_ix = jnp.arange(G)[None, :, None, None]
    sel_offs = jnp.arange(NSA_SEL_LEN)
    win_offs = jnp.arange(QB + W)
    blk_j = jnp.arange(n_sel)

    def one_block(qi):
        s0 = qi * QB
        qb = lax.dynamic_slice_in_dim(q, s0, QB, axis=1)
        t = s0 + jnp.arange(QB)
        tp = lax.dynamic_slice_in_dim(positions, s0, QB)
        bias_c = tbl[rel_bucket(tp[:, None] - cmp_pos[None, :])].transpose(2, 3, 0, 1)
        lg_c = jnp.einsum('bqghd,bcgd->bghqc', qb, kc) + bias_c
        p_c = masked_softmax(lg_c, cmp_end[None, :] <= t[:, None])
        o_c = jnp.einsum('bghqc,bcgd->bqghd', p_c, vc)
        imp = jnp.einsum('bghqc,cj->bgqj', p_c, agg)
        tb = t // NSA_SEL_LEN
        forced = ((blk_j[None, :] == 0) | (blk_j[None, :] == tb[:, None])
                  | (blk_j[None, :] == tb[:, None] - 1))
        eligible = sel_start[None, :] <= t[:, None]
        score = jnp.where(forced, 1e9, jnp.where(eligible, imp, -1e9))
        _, sel = lax.top_k(score, sel_k)
        k_sel = ks_blk[b_ix, g_ix, sel]
        v_sel = vs_blk[b_ix, g_ix, sel]
        tok_s = sel[..., None] * NSA_SEL_LEN + sel_offs
        bias_s = tbl[rel_bucket(tp[:, None, None] - pos_blk[sel]), g_ix[..., None]]
        lg_s = jnp.einsum('bqghd,bgqnkd->bghqnk', qb, k_sel) + bias_s.transpose(0, 1, 5, 2, 3, 4)
        valid_s = (tok_s <= t[:, None, None])[:, :, None].reshape(bs, G, 1, QB, -1)
        p_s = masked_softmax(lg_s.reshape(bs, G, HG, QB, -1), valid_s)
        p_s = p_s.reshape(bs, G, HG, QB, sel_k, NSA_SEL_LEN)
        o_s = jnp.einsum('bghqnk,bgqnkd->bqghd', p_s, v_sel)
        kwb = lax.dynamic_slice_in_dim(kw_pad, s0, QB + W, axis=1)
        vwb = lax.dynamic_slice_in_dim(vw_pad, s0, QB + W, axis=1)
        tok_w = s0 - W + win_offs
        pos_w = lax.dynamic_slice_in_dim(pos_pad, s0, QB + W)
        d_tok = t[:, None] - tok_w[None, :]
        valid_w = (d_tok >= 0) & (d_tok < W) & (tok_w[None, :] >= 0)
        bias_w = tbl[rel_bucket(tp[:, None] - pos_w[None, :])].transpose(2, 3, 0, 1)
        lg_w = jnp.einsum('bqghd,bkgd->bghqk', qb, kwb) + bias_w
        p_w = masked_softmax(lg_w, valid_w)
        o_w = jnp.einsum('bghqk,bkgd->bqghd', p_w, vwb)
        gb = lax.dynamic_slice_in_dim(gates, s0, QB, axis=1)
        return gb[..., 0:1] * o_c + gb[..., 1:2] * o_s + gb[..., 2:3] * o_w

    out = lax.map(one_block, jnp.arange(s // QB))
    return out.transpose(1, 0, 2, 3, 4, 5).reshape(bs, s, G * HG * DH).astype(out_dtype)


def sgu_mixer(uv, ln_g, ln_b, w_s, b_s):
    f32 = jnp.float32
    bs, s, _ = uv.shape
    uv = jax.nn.gelu(uv.astype(f32))
    u, v = uv[..., :SGU_WIDTH], uv[..., SGU_WIDTH:]
    mu = jnp.mean(v, axis=-1, keepdims=True)
    var = jnp.mean(jnp.square(v - mu), axis=-1, keepdims=True)
    v = (v - mu) * lax.rsqrt(var + RMS_EPS) * ln_g.astype(f32) + ln_b.astype(f32)
    nc = s // SGU_CHUNK
    v = v.reshape(bs, nc, SGU_CHUNK, SGU_GROUPS, SGU_WIDTH // SGU_GROUPS)
    mix = (jnp.einsum('gij,bcjgd->bcigd', jnp.tril(w_s.astype(f32)), v)
           + b_s.astype(f32).T[None, None, :, :, None])
    return (u * mix.reshape(bs, s, SGU_WIDTH)).astype(uv.dtype)


def moe_ffn(h, router, w_gate, w_up, w_down):
    logits = (h @ router).astype(jnp.float32)
    top_val, top_idx = lax.top_k(logits, TOP_K)
    weights = jax.nn.softmax(top_val, axis=-1)
    gate = jnp.sum(jax.nn.one_hot(top_idx, N_EXPERTS, dtype=jnp.float32) * weights[..., None], axis=-2)
    out = jnp.zeros_like(h)
    for e in range(N_EXPERTS):
        out = out + gate[..., e:e + 1].astype(h.dtype) * swiglu(h, w_gate[e], w_up[e], w_down[e])
    return out


def setup_inputs(seed: int = 0) -> dict:
    key = jax.random.key(seed)
    keys = iter(jax.random.split(key, 64))
    f32 = jnp.float32

    def nrm(shape, scale):
        return jax.random.normal(next(keys), shape, f32) * scale

    def gain(shape):
        return 1.0 + nrm(shape, 0.02)

    L = DEPTH
    x = nrm((BATCH, SEQ, D_MODEL), 1.0)
    positions = jnp.arange(SEQ, dtype=jnp.int32) + jax.random.randint(next(keys), (), 0, 1024, dtype=jnp.int32)
    ssm_a_re = -0.5 + nrm((L, SSM_GROUPS, SSM_STATE), 1e-3)
    ssm_a_im = math.pi * jnp.arange(SSM_STATE, dtype=f32) + nrm((L, SSM_GROUPS, SSM_STATE), 1e-3)
    ssm_log_dt = jax.random.uniform(next(keys), (L, SSM_GROUPS), f32, math.log(1e-3), math.log(1e-1))
    dt = jnp.exp(jax.random.uniform(next(keys), (L, GDN_HEADS), f32, math.log(1e-3), math.log(1e-1)))
    return {
        'x': x,
        'positions': positions,
        'norm_mix': gain((L, D_MODEL)),
        'norm_ffn': gain((L, D_MODEL)),
        'norm_final': gain((D_MODEL,)),
        'w_in': nrm((L, D_MODEL, IN_COLS), D_MODEL ** -0.5),
        'ssm_a_re': ssm_a_re,
        'ssm_a_im': ssm_a_im,
        'ssm_log_dt': ssm_log_dt,
        'ssm_b_re': nrm((L, SSM_GROUPS, SSM_STATE, SSM_GROUP), (2 * SSM_GROUP) ** -0.5),
        'ssm_b_im': nrm((L, SSM_GROUPS, SSM_STATE, SSM_GROUP), (2 * SSM_GROUP) ** -0.5),
        'ssm_c_re': nrm((L, SSM_GROUPS, SSM_GROUP, SSM_STATE), (2 * SSM_STATE) ** -0.5),
        'ssm_c_im': nrm((L, SSM_GROUPS, SSM_GROUP, SSM_STATE), (2 * SSM_STATE) ** -0.5),
        'ssm_d': nrm((L, SSM_WIDTH), 1.0),
        'ssm_glu_w': nrm((L, SSM_WIDTH, SSM_WIDTH), SSM_WIDTH ** -0.5),
        'gdn_conv_w': nrm((L, GDN_CONV, GDN_QKV), GDN_CONV ** -0.5),
        'gdn_a_log': jnp.log(jax.random.uniform(next(keys), (L, GDN_HEADS), f32, 1.0, 16.0)),
        'gdn_dt_bias': dt + jnp.log(-jnp.expm1(-dt)),
        'gdn_norm': gain((L, GDN_DV)),
        'nsa_cmp_k_pe': nrm((L, NSA_CMP_LEN, NSA_DH), 0.02),
        'nsa_cmp_k_w1': nrm((L, NSA_CMP_LEN * NSA_DH, NSA_CMP_HIDDEN), (NSA_CMP_LEN * NSA_DH) ** -0.5),
        'nsa_cmp_k_w2': nrm((L, NSA_CMP_HIDDEN, NSA_DH), NSA_CMP_HIDDEN ** -0.5),
        'nsa_cmp_v_pe': nrm((L, NSA_CMP_LEN, NSA_DH), 0.02),
        'nsa_cmp_v_w1': nrm((L, NSA_CMP_LEN * NSA_DH, NSA_CMP_HIDDEN), (NSA_CMP_LEN * NSA_DH) ** -0.5),
        'nsa_cmp_v_w2': nrm((L, NSA_CMP_HIDDEN, NSA_DH), NSA_CMP_HIDDEN ** -0.5),
        'rel_bias': nrm((REL_BUCKETS, NSA_HEADS), 0.5),
        'sgu_ln_g': gain((L, SGU_WIDTH)),
        'sgu_ln_b': nrm((L, SGU_WIDTH), 0.02),
        'sgu_w': nrm((L, SGU_GROUPS, SGU_CHUNK, SGU_CHUNK), SGU_CHUNK ** -0.5),
        'sgu_b': 1.0 + nrm((L, SGU_GROUPS, SGU_CHUNK), 0.02),
        'w_br_ssm': nrm((L, SSM_WIDTH, D_MODEL), SSM_WIDTH ** -0.5),
        'w_br_gdn': nrm((L, GDN_HEADS * GDN_DV, D_MODEL), (GDN_HEADS * GDN_DV) ** -0.5),
        'w_br_nsa': nrm((L, NSA_Q, D_MODEL), NSA_Q ** -0.5),
        'w_br_sgu': nrm((L, SGU_WIDTH, D_MODEL), SGU_WIDTH ** -0.5),
        'w_out': nrm((L, D_MODEL, D_MODEL), D_MODEL ** -0.5),
        'ffn_w_gate': nrm((N_DENSE_LAYERS, D_MODEL, FFN_DENSE), D_MODEL ** -0.5),
        'ffn_w_up': nrm((N_DENSE_LAYERS, D_MODEL, FFN_DENSE), D_MODEL ** -0.5),
        'ffn_w_down': nrm((N_DENSE_LAYERS, FFN_DENSE, D_MODEL), FFN_DENSE ** -0.5),
        'moe_router': nrm((N_MOE_LAYERS, D_MODEL, N_EXPERTS), D_MODEL ** -0.5),
        'moe_w_gate': nrm((N_MOE_LAYERS, N_EXPERTS, D_MODEL, FFN_EXPERT), D_MODEL ** -0.5),
        'moe_w_up': nrm((N_MOE_LAYERS, N_EXPERTS, D_MODEL, FFN_EXPERT), D_MODEL ** -0.5),
        'moe_w_down': nrm((N_MOE_LAYERS, N_EXPERTS, FFN_EXPERT, D_MODEL), FFN_EXPERT ** -0.5),
    }


def reference(x, positions, norm_mix, norm_ffn, norm_final, w_in,
              ssm_a_re, ssm_a_im, ssm_log_dt, ssm_b_re, ssm_b_im, ssm_c_re, ssm_c_im, ssm_d, ssm_glu_w,
              gdn_conv_w, gdn_a_log, gdn_dt_bias, gdn_norm,
              nsa_cmp_k_pe, nsa_cmp_k_w1, nsa_cmp_k_w2, nsa_cmp_v_pe, nsa_cmp_v_w1, nsa_cmp_v_w2, rel_bias,
              sgu_ln_g, sgu_ln_b, sgu_w, sgu_b,
              w_br_ssm, w_br_gdn, w_br_nsa, w_br_sgu, w_out,
              ffn_w_gate, ffn_w_up, ffn_w_down,
              moe_router, moe_w_gate, moe_w_up, moe_w_down):
    bs, s, _ = x.shape
    for layer in range(DEPTH):
        h = rms_norm(x, norm_mix[layer])
        (p_ssm, p_qkv, p_a, p_b, p_z, p_nq, p_kc, p_vc, p_ks, p_vs, p_kw, p_vw,
         p_ng, p_sgu, p_gate) = split_columns(h @ w_in[layer])
        y_ssm = ssm_mixer(p_ssm, ssm_a_re[layer], ssm_a_im[layer], ssm_log_dt[layer], ssm_b_re[layer],
                          ssm_b_im[layer], ssm_c_re[layer], ssm_c_im[layer], ssm_d[layer], ssm_glu_w[layer])
        y_gdn = gdn_mixer(p_qkv, p_a, p_b, p_z, gdn_conv_w[layer], gdn_a_log[layer], gdn_dt_bias[layer],
                          gdn_norm[layer])
        y_nsa = nsa_mixer(p_nq, p_kc, p_vc, p_ks, p_vs, p_kw, p_vw, p_ng, positions,
                          nsa_cmp_k_pe[layer], nsa_cmp_k_w1[layer], nsa_cmp_k_w2[layer],
                          nsa_cmp_v_pe[layer], nsa_cmp_v_w1[layer], nsa_cmp_v_w2[layer], rel_bias)
        y_sgu = sgu_mixer(p_sgu, sgu_ln_g[layer], sgu_ln_b[layer], sgu_w[layer], sgu_b[layer])
        gates = jax.nn.sigmoid(p_gate).reshape(bs, s, N_BRANCH, D_MODEL)
        merged = (gates[:, :, 0] * (y_ssm @ w_br_ssm[layer])
                  + gates[:, :, 1] * (y_gdn @ w_br_gdn[layer])
                  + gates[:, :, 2] * (y_nsa @ w_br_nsa[layer])
                  + gates[:, :, 3] * (y_sgu @ w_br_sgu[layer]))
        x = x + merged @ w_out[layer]
        h = rms_norm(x, norm_ffn[layer])
        if layer % 2 == 0:
            i = layer // 2
            x = x + swiglu(h, ffn_w_gate[i], ffn_w_up[i], ffn_w_down[i])
        else:
            i = layer // 2
            x = x + moe_ffn(h, moe_router[i], moe_w_gate[i], moe_w_up[i], moe_w_down[i])
    return rms_norm(x, norm_final)
```

```python
import functools
import math

import jax
import jax.numpy as jnp
from jax import lax
import numpy as np
from jax.experimental import pallas as pl
from jax.experimental.pallas import tpu as pltpu

F32 = jnp.float32
BF16 = jnp.bfloat16

D_MODEL = 1024
DEPTH = 2
SSM_WIDTH = D_MODEL // 2
SSM_GROUP = 16
SSM_GROUPS = SSM_WIDTH // SSM_GROUP
SSM_STATE = 64
GDN_HEADS = 4
GDN_DK = 128
GDN_DV = 128
GDN_CONV = 4
GDN_CHUNK = 64
NSA_HEADS = 8
NSA_KV_GROUPS = 2
NSA_HPG = NSA_HEADS // NSA_KV_GROUPS
NSA_DH = 64
NSA_CMP_LEN = 32
NSA_CMP_STRIDE = 16
NSA_CMP_HIDDEN = 128
NSA_SEL_LEN = 64
NSA_SEL_TOPK = 16
NSA_WINDOW = 512
NSA_QBLOCK = 128
SGU_WIDTH = D_MODEL // 2
SGU_GROUPS = 4
SGU_CHUNK = 128
REL_BUCKETS = 32
REL_MAX_DIST = 128
FFN_DENSE = 2816
N_EXPERTS = 8
TOP_K = 2
FFN_EXPERT = 3584
N_BRANCH = 4
RMS_EPS = 1e-6
GDN_QKV = GDN_HEADS * (2 * GDN_DK + GDN_DV)
NSA_Q = NSA_HEADS * NSA_DH
NSA_KV = NSA_KV_GROUPS * NSA_DH

LANES = 128
VMEM_LIMIT = 48 * 1024 * 1024

OFF_QKV = 0
OFF_SSM = OFF_QKV + GDN_QKV
OFF_Z = OFF_SSM + SSM_WIDTH
OFF_NQ = OFF_Z + GDN_HEADS * GDN_DV
OFF_SGU = OFF_NQ + NSA_Q
OFF_GATE = OFF_SGU + 2 * SGU_WIDTH
OFF_KV = OFF_GATE + N_BRANCH * D_MODEL
OFF_SMALL = OFF_KV + 6 * NSA_KV
SMALL_USED = 2 * GDN_HEADS + 3 * NSA_HEADS
PROJ_COLS = 9216


def _params(*sem):
    return pltpu.CompilerParams(dimension_semantics=sem, vmem_limit_bytes=VMEM_LIMIT)


def _rms(x, g):
    return x * lax.rsqrt(jnp.mean(x * x, axis=-1, keepdims=True) + RMS_EPS) * g


def _in_proj_body(x_ref, g_ref, w_ref, o_ref, h_ref):
    @pl.when(pl.program_id(1) == 0)
    def _():
        h_ref[...] = _rms(x_ref[...], g_ref[...]).astype(BF16)

    o_ref[...] = jnp.dot(h_ref[...], w_ref[...], preferred_element_type=F32)


def in_proj(x, g, w, tm=1024, tn=512):
    t, d = x.shape
    n = w.shape[1]
    return pl.pallas_call(
        _in_proj_body,
        grid=(t // tm, n // tn),
        in_specs=[pl.BlockSpec((tm, d), lambda i, j: (i, 0)),
                  pl.BlockSpec((1, d), lambda i, j: (0, 0)),
                  pl.BlockSpec((d, tn), lambda i, j: (0, j))],
        out_specs=pl.BlockSpec((tm, tn), lambda i, j: (i, j)),
        out_shape=jax.ShapeDtypeStruct((t, n), F32),
        scratch_shapes=[pltpu.VMEM((tm, d), BF16)],
        compiler_params=_params("parallel", "arbitrary"),
        name="in_proj",
    )(x, g.reshape(1, d), w)


def _merge_body(x_ref, gate_ref, ya_ref, yb_ref, yc_ref, yd_ref,
                wa_ref, wb_ref, wc_ref, wd_ref, wo_ref, o_ref):
    def branch(k, y_ref, w_ref):
        p = jnp.dot(y_ref[...].astype(BF16), w_ref[...], preferred_element_type=F32)
        return jax.nn.sigmoid(gate_ref[:, k * D_MODEL:(k + 1) * D_MODEL]) * p

    merged = (branch(0, ya_ref, wa_ref) + branch(1, yb_ref, wb_ref)
              + branch(2, yc_ref, wc_ref) + branch(3, yd_ref, wd_ref))
    o_ref[...] = x_ref[...] + jnp.dot(merged.astype(BF16), wo_ref[...],
                                      preferred_element_type=F32)


def merge(x, proj, ys, ws, w_out, tm=256):
    t, d = x.shape
    gate_blk = OFF_GATE // (N_BRANCH * D_MODEL)
    row = lambda i: (i, 0)
    fixed = lambda i: (0, 0)
    in_specs = [pl.BlockSpec((tm, d), row),
                pl.BlockSpec((tm, N_BRANCH * D_MODEL), lambda i: (i, gate_blk))]
    in_specs += [pl.BlockSpec((tm, y.shape[1]), row) for y in ys]
    in_specs += [pl.BlockSpec(w.shape, fixed) for w in ws]
    in_specs += [pl.BlockSpec(w_out.shape, fixed)]
    return pl.pallas_call(
        _merge_body,
        grid=(t // tm,),
        in_specs=in_specs,
        out_specs=pl.BlockSpec((tm, d), row),
        out_shape=jax.ShapeDtypeStruct((t, d), F32),
        compiler_params=_params("parallel"),
        name="merge",
    )(x, proj, *ys, *ws, w_out)


def _ffn_body(x_ref, g_ref, wg_ref, wu_ref, wd_ref, o_ref, h_ref, acc_ref):
    f = pl.program_id(1)

    @pl.when(f == 0)
    def _():
        h_ref[...] = _rms(x_ref[...], g_ref[...]).astype(BF16)
        acc_ref[...] = x_ref[...]

    h = h_ref[...]
    a = jnp.dot(h, wg_ref[...], preferred_element_type=F32)
    u = jnp.dot(h, wu_ref[...], preferred_element_type=F32)
    act = (a * jax.nn.sigmoid(a) * u).astype(BF16)
    acc_ref[...] += jnp.dot(act, wd_ref[...], preferred_element_type=F32)

    @pl.when(f == pl.num_programs(1) - 1)
    def _():
        o_ref[...] = acc_ref[...]


def dense_ffn(x, g, wg, wu, wd, tm=1024, tf=256):
    t, d = x.shape
    nf = wg.shape[1]
    return pl.pallas_call(
        _ffn_body,
        grid=(t // tm, nf // tf),
        in_specs=[pl.BlockSpec((tm, d), lambda i, f: (i, 0)),
                  pl.BlockSpec((1, d), lambda i, f: (0, 0)),
                  pl.BlockSpec((d, tf), lambda i, f: (0, f)),
                  pl.BlockSpec((d, tf), lambda i, f: (0, f)),
                  pl.BlockSpec((tf, d), lambda i, f: (f, 0))],
        out_specs=pl.BlockSpec((tm, d), lambda i, f: (i, 0)),
        out_shape=jax.ShapeDtypeStruct((t, d), F32),
        scratch_shapes=[pltpu.VMEM((tm, d), BF16), pltpu.VMEM((tm, d), F32)],
        compiler_params=_params("parallel", "arbitrary"),
        name="dense_ffn",
    )(x, g.reshape(1, d), wg, wu, wd)


def _route_body(x_ref, g_ref, rhi_ref, rlo_ref, h_ref, idx_ref, wt_ref):
    h = _rms(x_ref[...], g_ref[...])
    hi = h.astype(BF16)
    lo = (h - hi.astype(F32)).astype(BF16)
    h_ref[...] = hi
    logits = (jnp.dot(hi, rhi_ref[...], preferred_element_type=F32)
              + jnp.dot(hi, rlo_ref[...], preferred_element_type=F32)
              + jnp.dot(lo, rhi_ref[...], preferred_element_type=F32))
    col = lax.broadcasted_iota(jnp.int32, logits.shape, 1).astype(F32)
    neg = jnp.float32(-jnp.inf)
    lg = jnp.where(col < N_EXPERTS, logits, neg)
    m1 = jnp.max(lg, axis=-1, keepdims=True)
    i1 = jnp.min(jnp.where(lg == m1, col, float(LANES)), axis=-1, keepdims=True)
    lg2 = jnp.where(col == i1, neg, lg)
    m2 = jnp.max(lg2, axis=-1, keepdims=True)
    i2 = jnp.min(jnp.where(lg2 == m2, col, float(LANES)), axis=-1, keepdims=True)
    e = jnp.exp(m2 - m1)
    w1 = 1.0 / (1.0 + e)
    w2 = e / (1.0 + e)
    idx_ref[...] = jnp.where(col == 0, i1, jnp.where(col == 1, i2, 0.0)).astype(jnp.int32)
    wt_ref[...] = jnp.where(col == 0, w1, jnp.where(col == 1, w2, 0.0))


def moe_route(x, g, r_hi, r_lo, tm=512):
    t, d = x.shape
    row = lambda i: (i, 0)
    fixed = lambda i: (0, 0)
    return pl.pallas_call(
        _route_body,
        grid=(t // tm,),
        in_specs=[pl.BlockSpec((tm, d), row), pl.BlockSpec((1, d), fixed),
                  pl.BlockSpec((d, LANES), fixed), pl.BlockSpec((d, LANES), fixed)],
        out_specs=[pl.BlockSpec((tm, d), row), pl.BlockSpec((tm, LANES), row),
                   pl.BlockSpec((tm, LANES), row)],
        out_shape=[jax.ShapeDtypeStruct((t, d), BF16),
                   jax.ShapeDtypeStruct((t, LANES), jnp.int32),
                   jax.ShapeDtypeStruct((t, LANES), F32)],
        compiler_params=_params("parallel"),
        name="moe_route",
    )(x, g.reshape(1, d), r_hi, r_lo)


def _experts_body(te_ref, nu_ref, xs_ref, rw_ref, wg_ref, wu_ref, wd_ref, o_ref, acc_ref):
    i = pl.program_id(0)
    f = pl.program_id(1)
    used = i < nu_ref[0]

    @pl.when(f == 0)
    def _():
        acc_ref[...] = jnp.zeros_like(acc_ref)

    @pl.when(used)
    def _():
        h = xs_ref[...]
        a = jnp.dot(h, wg_ref[0], preferred_element_type=F32)
        u = jnp.dot(h, wu_ref[0], preferred_element_type=F32)
        act = (a * jax.nn.sigmoid(a) * u).astype(BF16)
        acc_ref[...] += jnp.dot(act, wd_ref[0], preferred_element_type=F32)

    @pl.when(f == pl.num_programs(1) - 1)
    def _():
        o_ref[...] = acc_ref[...] * rw_ref[...]


def moe_experts(tile_expert, n_used, xs, row_w, wg, wu, wd, tm, tf=512):
    p, d = xs.shape
    nf = wg.shape[2]
    grid_spec = pltpu.PrefetchScalarGridSpec(
        num_scalar_prefetch=2,
        grid=(p // tm, nf // tf),
        in_specs=[pl.BlockSpec((tm, d), lambda i, f, te, nu: (i, 0)),
                  pl.BlockSpec((tm, 1), lambda i, f, te, nu: (i, 0)),
                  pl.BlockSpec((1, d, tf), lambda i, f, te, nu: (te[i], 0, f)),
                  pl.BlockSpec((1, d, tf), lambda i, f, te, nu: (te[i], 0, f)),
                  pl.BlockSpec((1, tf, d), lambda i, f, te, nu: (te[i], f, 0))],
        out_specs=pl.BlockSpec((tm, d), lambda i, f, te, nu: (i, 0)),
        scratch_shapes=[pltpu.VMEM((tm, d), F32)],
    )
    return pl.pallas_call(
        _experts_body,
        grid_spec=grid_spec,
        out_shape=jax.ShapeDtypeStruct((p, d), F32),
        compiler_params=_params("parallel", "arbitrary"),
        name="moe_experts",
    )(tile_expert, n_used, xs, row_w, wg, wu, wd)


def _combine_norm_body(x_ref, ya_ref, yb_ref, g_ref, o_ref):
    o_ref[...] = _rms(x_ref[...] + ya_ref[...] + yb_ref[...], g_ref[...])


def combine_norm(x, ya, yb, g, tm=1024):
    t, d = x.shape
    row = lambda i: (i, 0)
    return pl.pallas_call(
        _combine_norm_body,
        grid=(t // tm,),
        in_specs=[pl.BlockSpec((tm, d), row)] * 3 + [pl.BlockSpec((1, d), lambda i: (0, 0))],
        out_specs=pl.BlockSpec((tm, d), row),
        out_shape=jax.ShapeDtypeStruct((t, d), F32),
        compiler_params=_params("parallel"),
        name="combine_norm",
    )(x, ya, yb, g.reshape(1, d))


def moe_layer(x, g_norm, router, wg, wu, wd, g_final, tm=512):
    t, d = x.shape
    r = jnp.zeros((d, LANES), F32).at[:, :N_EXPERTS].set(router)
    r_hi = r.astype(BF16)
    r_lo = (r - r_hi.astype(F32)).astype(BF16)
    h, idx, wts = moe_route(x, g_norm, r_hi, r_lo)
    e_flat = jnp.concatenate([idx[:, 0], idx[:, 1]])
    w_flat = jnp.concatenate([wts[:, 0], wts[:, 1]])
    onehot = (e_flat[:, None] == jnp.arange(N_EXPERTS)[None, :]).astype(jnp.int32)
    csum = jnp.cumsum(onehot, axis=0)
    rank = jnp.sum((csum - onehot) * onehot, axis=1)
    counts = csum[-1]
    padded = ((counts + tm - 1) // tm) * tm
    ends = jnp.cumsum(padded)
    starts = ends - padded
    dest = starts[e_flat] + rank
    n_tiles = (TOP_K * t) // tm + N_EXPERTS
    p = n_tiles * tm
    tok = jnp.concatenate([jnp.arange(t, dtype=jnp.int32)] * TOP_K)
    src = jnp.zeros((p,), jnp.int32).at[dest].set(tok)
    row_w = jnp.zeros((p,), F32).at[dest].set(w_flat)
    tile_expert = jnp.minimum(
        jnp.searchsorted(ends, jnp.arange(n_tiles, dtype=jnp.int32) * tm, side="right"),
        N_EXPERTS - 1).astype(jnp.int32)
    n_used = (ends[-1] // tm).astype(jnp.int32).reshape(1)
    xs = jnp.take(h, src, axis=0)
    ys = moe_experts(tile_expert, n_used, xs, row_w.reshape(p, 1), wg, wu, wd, tm)
    ya = jnp.take(ys, dest[:t], axis=0)
    yb = jnp.take(ys, dest[t:], axis=0)
    return combine_norm(x, ya, yb, g_final)


def _l2_norm(t):
    return t * lax.rsqrt(jnp.sum(t * t, axis=-1, keepdims=True) + 1e-6)


def _masked_softmax(logits, valid):
    logits = jnp.where(valid, logits.astype(F32), -1e30)
    return jax.nn.softmax(logits, axis=-1) * valid


def _rel_bucket(dist):
    n = jnp.maximum(dist, 0)
    max_exact = REL_BUCKETS // 2
    nf = jnp.maximum(n, 1).astype(F32)
    large = max_exact + (jnp.log(nf / max_exact) / math.log(REL_MAX_DIST / max_exact)
                         * (REL_BUCKETS - max_exact)).astype(jnp.int32)
    large = jnp.minimum(large, REL_BUCKETS - 1)
    return jnp.where(n < max_exact, n, large)


def ssm_mixer(u, a_re, a_im, log_dt, b_re, b_im, c_re, c_im, d_skip, glu_w):
    bs, s, _ = u.shape
    uf = u.reshape(bs, s, SSM_GROUPS, SSM_GROUP)
    dt = jnp.exp(log_dt)[:, None]
    mag = jnp.exp(a_re * dt)
    abar_r, abar_i = mag * jnp.cos(a_im * dt), mag * jnp.sin(a_im * dt)
    nr, ni = abar_r - 1.0, abar_i
    den = a_re * a_re + a_im * a_im
    sr, si = (nr * a_re + ni * a_im) / den, (ni * a_re - nr * a_im) / den
    bbar_r = sr[..., None] * b_re - si[..., None] * b_im
    bbar_i = sr[..., None] * b_im + si[..., None] * b_re
    bu_r = jnp.einsum('bsgp,gnp->bsgn', uf, bbar_r)
    bu_i = jnp.einsum('bsgp,gnp->bsgn', uf, bbar_i)
    ar = jnp.broadcast_to(abar_r, bu_r.shape)
    ai = jnp.broadcast_to(abar_i, bu_r.shape)

    def combine(e1, e2):
        a1r, a1i, b1r, b1i = e1
        a2r, a2i, b2r, b2i = e2
        return (a2r * a1r - a2i * a1i, a2r * a1i + a2i * a1r,
                a2r * b1r - a2i * b1i + b2r, a2r * b1i + a2i * b1r + b2i)

    _, _, st_r, st_i = lax.associative_scan(combine, (ar, ai, bu_r, bu_i), axis=1)
    y = (jnp.einsum('gpn,bsgn->bsgp', c_re, st_r) - jnp.einsum('gpn,bsgn->bsgp', c_im, st_i)
         + d_skip.reshape(SSM_GROUPS, SSM_GROUP) * uf)
    y = jax.nn.gelu(y.reshape(bs, s, SSM_WIDTH))
    return y * jax.nn.sigmoid(y @ glu_w)


def gdn_mixer(qkv, a, b, z, conv_w, a_log, dt_bias, norm_g):
    bs, s, c = qkv.shape
    H, DK, DV, CH = GDN_HEADS, GDN_DK, GDN_DV, GDN_CHUNK
    qkv = lax.conv_general_dilated(qkv, conv_w[:, None, :], window_strides=(1,),
                                   padding=[(GDN_CONV - 1, 0)],
                                   dimension_numbers=('NWC', 'WIO', 'NWC'), feature_group_count=c)
    qkv = jax.nn.silu(qkv)
    q, k, v = jnp.split(qkv, [H * DK, 2 * H * DK], axis=-1)
    q = _l2_norm(q.reshape(bs, s, H, DK)) * (DK ** -0.5)
    k = _l2_norm(k.reshape(bs, s, H, DK))
    v = v.reshape(bs, s, H, DV)
    beta = jax.nn.sigmoid(b)
    log_alpha = -jnp.exp(a_log) * jax.nn.softplus(a + dt_bias)
    nc = s // CH

    def chunks(t):
        return t.reshape(bs, nc, CH, H, -1).transpose(0, 3, 1, 2, 4)

    q, k, v = chunks(q), chunks(k), chunks(v)
    beta = chunks(beta[..., None])
    g = jnp.cumsum(chunks(log_alpha[..., None]), axis=3)
    gv = g[..., 0]
    causal = jnp.tril(jnp.ones((CH, CH), bool))
    strict = jnp.tril(jnp.ones((CH, CH), bool), -1)
    decay = jnp.exp(jnp.where(causal, gv[..., :, None] - gv[..., None, :], -jnp.inf))
    kb = k * beta
    a_mat = (jnp.where(strict, jnp.einsum('bhncd,bhnkd->bhnck', kb, k) * decay, 0.0)
             + jnp.eye(CH, dtype=F32))
    rhs = jnp.concatenate([v * beta, kb * jnp.exp(g)], axis=-1)
    sol = lax.linalg.triangular_solve(a_mat, rhs, left_side=True, lower=True, unit_diagonal=True)
    u_val, w = sol[..., :DV], sol[..., DV:]
    attn = jnp.einsum('bhncd,bhnkd->bhnck', q, k) * decay
    g_last = g[..., -1:, :]
    k_st = k * jnp.exp(g_last - g)
    q_st = q * jnp.exp(g)

    def step(state, xs):
        q_c, k_c, u_c, w_c, a_c, gl_c = xs
        v_new = u_c - jnp.einsum('bhcd,bhde->bhce', w_c, state)
        o = jnp.einsum('bhcd,bhde->bhce', q_c, state) + jnp.einsum('bhck,bhke->bhce', a_c, v_new)
        state = state * jnp.exp(gl_c) + jnp.einsum('bhcd,bhce->bhde', k_c, v_new)
        return state, o

    xs = tuple(jnp.moveaxis(t, 2, 0) for t in (q_st, k_st, u_val, w, attn, g_last))
    _, o = lax.scan(step, jnp.zeros((bs, H, DK, DV), F32), xs)
    o = o.transpose(1, 0, 3, 2, 4).reshape(bs, s, H, DV)
    o = o * lax.rsqrt(jnp.mean(o * o, axis=-1, keepdims=True) + RMS_EPS) * norm_g
    o = o * jax.nn.silu(z.reshape(bs, s, H, DV))
    return o.reshape(bs, s, H * DV)


def nsa_mixer(q, k_c, v_c, k_s, v_s, k_w, v_w, gate_logits, positions,
              ck_pe, ck_w1, ck_w2, cv_pe, cv_w1, cv_w2, rel_bias):
    bs, s, _ = q.shape
    G, HG, DH, QB, W = NSA_KV_GROUPS, NSA_HPG, NSA_DH, NSA_QBLOCK, NSA_WINDOW
    q = q.reshape(bs, s, G, HG, DH) * (DH ** -0.5)

    def kv(t):
        return t.reshape(bs, s, G, DH)

    k_c, v_c, k_s, v_s, k_w, v_w = (kv(t) for t in (k_c, v_c, k_s, v_s, k_w, v_w))
    gates = jax.nn.sigmoid(gate_logits).reshape(bs, s, G, HG, 3)
    n_cmp = (s - NSA_CMP_LEN) // NSA_CMP_STRIDE + 1
    cmp_start = jnp.arange(n_cmp) * NSA_CMP_STRIDE
    cmp_end = cmp_start + NSA_CMP_LEN - 1
    cmp_tok = cmp_start[:, None] + jnp.arange(NSA_CMP_LEN)[None, :]

    def compress(t, pe, w1, w2):
        blk = t[:, cmp_tok] + pe[None, None, :, None, :]
        blk = blk.transpose(0, 1, 3, 2, 4).reshape(bs, n_cmp, G, NSA_CMP_LEN * DH)
        return jax.nn.gelu(blk @ w1) @ w2

    kc = compress(k_c, ck_pe, ck_w1, ck_w2)
    vc = compress(v_c, cv_pe, cv_w1, cv_w2)
    cmp_pos = positions[cmp_end]
    n_sel = s // NSA_SEL_LEN
    sel_k = min(NSA_SEL_TOPK, n_sel)
    sel_start = jnp.arange(n_sel) * NSA_SEL_LEN
    agg = ((cmp_start[:, None] <= sel_start[None, :] + NSA_SEL_LEN - 1)
           & (cmp_end[:, None] >= sel_start[None, :])).astype(F32)
    ks_blk = k_s.reshape(bs, n_sel, NSA_SEL_LEN, G, DH).transpose(0, 3, 1, 2, 4)
    vs_blk = v_s.reshape(bs, n_sel, NSA_SEL_LEN, G, DH).transpose(0, 3, 1, 2, 4)
    pos_blk = positions.reshape(n_sel, NSA_SEL_LEN)
    kw_pad = jnp.pad(k_w, ((0, 0), (W, 0), (0, 0), (0, 0)))
    vw_pad = jnp.pad(v_w, ((0, 0), (W, 0), (0, 0), (0, 0)))
    pos_pad = jnp.pad(positions, (W, 0))
    tbl = rel_bias.reshape(REL_BUCKETS, G, HG)
    b_ix = jnp.arange(bs)[:, None, None, None]
    g_ix = jnp.arange(G)[None, :, None, None]
    sel_offs = jnp.arange(NSA_SEL_LEN)
    win_offs = jnp.arange(QB + W)
    blk_j = jnp.arange(n_sel)

    def one_block(qi):
        s0 = qi * QB
        qb = lax.dynamic_slice_in_dim(q, s0, QB, axis=1)
        t = s0 + jnp.arange(QB)
        tp = lax.dynamic_slice_in_dim(positions, s0, QB)
        bias_c = tbl[_rel_bucket(tp[:, None] - cmp_pos[None, :])].transpose(2, 3, 0, 1)
        lg_c = jnp.einsum('bqghd,bcgd->bghqc', qb, kc) + bias_c
        p_c = _masked_softmax(lg_c, cmp_end[None, :] <= t[:, None])
        o_c = jnp.einsum('bghqc,bcgd->bqghd', p_c, vc)
        imp = jnp.einsum('bghqc,cj->bgqj', p_c, agg)
        tb = t // NSA_SEL_LEN
        forced = ((blk_j[None, :] == 0) | (blk_j[None, :] == tb[:, None])
                  | (blk_j[None, :] == tb[:, None] - 1))
        eligible = sel_start[None, :] <= t[:, None]
        score = jnp.where(forced, 1e9, jnp.where(eligible, imp, -1e9))
        _, sel = lax.top_k(score, sel_k)
        k_sel = ks_blk[b_ix, g_ix, sel]
        v_sel = vs_blk[b_ix, g_ix, sel]
        tok_s = sel[..., None] * NSA_SEL_LEN + sel_offs
        bias_s = tbl[_rel_bucket(tp[:, None, None] - pos_blk[sel]), g_ix[..., None]]
        lg_s = jnp.einsum('bqghd,bgqnkd->bghqnk', qb, k_sel) + bias_s.transpose(0, 1, 5, 2, 3, 4)
        valid_s = (tok_s <= t[:, None, None])[:, :, None].reshape(bs, G, 1, QB, -1)
        p_s = _masked_softmax(lg_s.reshape(bs, G, HG, QB, -1), valid_s)
        p_s = p_s.reshape(bs, G, HG, QB, sel_k, NSA_SEL_LEN)
        o_s = jnp.einsum('bghqnk,bgqnkd->bqghd', p_s, v_sel)
        kwb = lax.dynamic_slice_in_dim(kw_pad, s0, QB + W, axis=1)
        vwb = lax.dynamic_slice_in_dim(vw_pad, s0, QB + W, axis=1)
        tok_w = s0 - W + win_offs
        pos_w = lax.dynamic_slice_in_dim(pos_pad, s0, QB + W)
        d_tok = t[:, None] - tok_w[None, :]
        valid_w = (d_tok >= 0) & (d_tok < W) & (tok_w[None, :] >= 0)
        bias_w = tbl[_rel_bucket(tp[:, None] - pos_w[None, :])].transpose(2, 3, 0, 1)
        lg_w = jnp.einsum('bqghd,bkgd->bghqk', qb, kwb) + bias_w
        p_w = _masked_softmax(lg_w, valid_w)
        o_w = jnp.einsum('bghqk,bkgd->bqghd', p_w, vwb)
        gb = lax.dynamic_slice_in_dim(gates, s0, QB, axis=1)
        return gb[..., 0:1] * o_c + gb[..., 1:2] * o_s + gb[..., 2:3] * o_w

    out = lax.map(one_block, jnp.arange(s // QB))
    return out.transpose(1, 0, 2, 3, 4, 5).reshape(bs, s, G * HG * DH)


def sgu_mixer(uv, ln_g, ln_b, w_s, b_s):
    bs, s, _ = uv.shape
    uv = jax.nn.gelu(uv)
    u, v = uv[..., :SGU_WIDTH], uv[..., SGU_WIDTH:]
    mu = jnp.mean(v, axis=-1, keepdims=True)
    var = jnp.mean(jnp.square(v - mu), axis=-1, keepdims=True)
    v = (v - mu) * lax.rsqrt(var + RMS_EPS) * ln_g + ln_b
    nc = s // SGU_CHUNK
    v = v.reshape(bs, nc, SGU_CHUNK, SGU_GROUPS, SGU_WIDTH // SGU_GROUPS)
    mix = (jnp.einsum('gij,bcjgd->bcigd', jnp.tril(w_s), v) + b_s.T[None, None, :, :, None])
    return u * mix.reshape(bs, s, SGU_WIDTH)


def _permute_w_in(w):
    sizes = ([SSM_WIDTH, GDN_QKV, GDN_HEADS, GDN_HEADS, GDN_HEADS * GDN_DV, NSA_Q]
             + [NSA_KV] * 6 + [3 * NSA_HEADS, 2 * SGU_WIDTH, N_BRANCH * D_MODEL])
    cuts = [int(c) for c in np.cumsum(sizes)[:-1]]
    (w_ssm, w_qkv, w_a, w_b, w_z, w_nq, w_kc, w_vc, w_ks, w_vs, w_kw, w_vw,
     w_ng, w_sgu, w_gate) = jnp.split(w, cuts, axis=-1)
    pad = jnp.zeros((w.shape[0], PROJ_COLS - OFF_SMALL - SMALL_USED), w.dtype)
    return jnp.concatenate([w_qkv, w_ssm, w_z, w_nq, w_sgu, w_gate, w_kc, w_vc, w_ks, w_vs,
                            w_kw, w_vw, w_a, w_b, w_ng, pad], axis=-1).astype(BF16)


def kernel(x, positions, norm_mix, norm_ffn, norm_final, w_in, ssm_a_re, ssm_a_im, ssm_log_dt, ssm_b_re, ssm_b_im, ssm_c_re, ssm_c_im, ssm_d, ssm_glu_w, gdn_conv_w, gdn_a_log, gdn_dt_bias, gdn_norm, nsa_cmp_k_pe, nsa_cmp_k_w1, nsa_cmp_k_w2, nsa_cmp_v_pe, nsa_cmp_v_w1, nsa_cmp_v_w2, rel_bias, sgu_ln_g, sgu_ln_b, sgu_w, sgu_b, w_br_ssm, w_br_gdn, w_br_nsa, w_br_sgu, w_out, ffn_w_gate, ffn_w_up, ffn_w_down, moe_router, moe_w_gate, moe_w_up, moe_w_down):
    bs, s, d = x.shape
    t = bs * s
    xf = x.reshape(t, d)
    for layer in range(DEPTH):
        proj = in_proj(xf, norm_mix[layer], _permute_w_in(w_in[layer]))
        p3 = proj.reshape(bs, s, PROJ_COLS)

        def seg(off, width):
            return p3[..., off:off + width]

        y_ssm = ssm_mixer(seg(OFF_SSM, SSM_WIDTH), ssm_a_re[layer], ssm_a_im[layer],
                          ssm_log_dt[layer], ssm_b_re[layer], ssm_b_im[layer], ssm_c_re[layer],
                          ssm_c_im[layer], ssm_d[layer], ssm_glu_w[layer])
        y_gdn = gdn_mixer(seg(OFF_QKV, GDN_QKV), seg(OFF_SMALL, GDN_HEADS),
                          seg(OFF_SMALL + GDN_HEADS, GDN_HEADS), seg(OFF_Z, GDN_HEADS * GDN_DV),
                          gdn_conv_w[layer], gdn_a_log[layer], gdn_dt_bias[layer], gdn_norm[layer])
        kvs = [seg(OFF_KV + i * NSA_KV, NSA_KV) for i in range(6)]
        y_nsa = nsa_mixer(seg(OFF_NQ, NSA_Q), *kvs, seg(OFF_SMALL + 2 * GDN_HEADS, 3 * NSA_HEADS),
                          positions, nsa_cmp_k_pe[layer], nsa_cmp_k_w1[layer], nsa_cmp_k_w2[layer],
                          nsa_cmp_v_pe[layer], nsa_cmp_v_w1[layer], nsa_cmp_v_w2[layer], rel_bias)
        y_sgu = sgu_mixer(seg(OFF_SGU, 2 * SGU_WIDTH), sgu_ln_g[layer], sgu_ln_b[layer],
                          sgu_w[layer], sgu_b[layer])
        ys = [y.reshape(t, -1) for y in (y_ssm, y_gdn, y_nsa, y_sgu)]
        ws = [w[layer].astype(BF16) for w in (w_br_ssm, w_br_gdn, w_br_nsa, w_br_sgu)]
        xf = merge(xf, proj, ys, ws, w_out[layer].astype(BF16))
        i = layer // 2
        if layer % 2 == 0:
            xf = dense_ffn(xf, norm_ffn[layer], ffn_w_gate[i].astype(BF16),
                           ffn_w_up[i].astype(BF16), ffn_w_down[i].astype(BF16))
        else:
            xf = moe_layer(xf, norm_ffn[layer], moe_router[i], moe_w_gate[i].astype(BF16),
                           moe_w_up[i].astype(BF16), moe_w_down[i].astype(BF16), norm_final)
    return xf.reshape(bs, s, d)
```

```python
import functools
import math

import jax
import jax.numpy as jnp
from jax import lax
import numpy as np
from jax.experimental import pallas as pl
from jax.experimental.pallas import tpu as pltpu

F32 = jnp.float32
BF16 = jnp.bfloat16

D_MODEL = 1024
DEPTH = 2
SSM_WIDTH = D_MODEL // 2
SSM_GROUP = 16
SSM_GROUPS = SSM_WIDTH // SSM_GROUP
SSM_STATE = 64
GDN_HEADS = 4
GDN_DK = 128
GDN_DV = 128
GDN_CONV = 4
GDN_CHUNK = 64
NSA_HEADS = 8
NSA_KV_GROUPS = 2
NSA_HPG = NSA_HEADS // NSA_KV_GROUPS
NSA_DH = 64
NSA_CMP_LEN = 32
NSA_CMP_STRIDE = 16
NSA_CMP_HIDDEN = 128
NSA_SEL_LEN = 64
NSA_SEL_TOPK = 16
NSA_WINDOW = 512
NSA_QBLOCK = 128
SGU_WIDTH = D_MODEL // 2
SGU_GROUPS = 4
SGU_CHUNK = 128
REL_BUCKETS = 32
REL_MAX_DIST = 128
FFN_DENSE = 2816
N_EXPERTS = 8
TOP_K = 2
FFN_EXPERT = 3584
N_BRANCH = 4
RMS_EPS = 1e-6
GDN_QKV = GDN_HEADS * (2 * GDN_DK + GDN_DV)
NSA_Q = NSA_HEADS * NSA_DH
NSA_KV = NSA_KV_GROUPS * NSA_DH

LANES = 128
VMEM_LIMIT = 48 * 1024 * 1024

OFF_QKV = 0
OFF_SSM = OFF_QKV + GDN_QKV
OFF_Z = OFF_SSM + SSM_WIDTH
OFF_NQ = OFF_Z + GDN_HEADS * GDN_DV
OFF_SGU = OFF_NQ + NSA_Q
OFF_GATE = OFF_SGU + 2 * SGU_WIDTH
OFF_KV = OFF_GATE + N_BRANCH * D_MODEL
OFF_SMALL = OFF_KV + 6 * NSA_KV
SMALL_USED = 2 * GDN_HEADS + 3 * NSA_HEADS
PROJ_COLS = 9216


def _params(*sem):
    return pltpu.CompilerParams(dimension_semantics=sem, vmem_limit_bytes=VMEM_LIMIT)


def _rms(x, g):
    return x * lax.rsqrt(jnp.mean(x * x, axis=-1, keepdims=True) + RMS_EPS) * g


def _in_proj_body(x_ref, g_ref, w_ref, o_ref, h_ref):
    @pl.when(pl.program_id(1) == 0)
    def _():
        h_ref[...] = _rms(x_ref[...], g_ref[...]).astype(BF16)

    o_ref[...] = jnp.dot(h_ref[...], w_ref[...], preferred_element_type=F32)


def in_proj(x, g, w, tm=1024, tn=512):
    t, d = x.shape
    n = w.shape[1]
    return pl.pallas_call(
        _in_proj_body,
        grid=(t // tm, n // tn),
        in_specs=[pl.BlockSpec((tm, d), lambda i, j: (i, 0)),
                  pl.BlockSpec((1, d), lambda i, j: (0, 0)),
                  pl.BlockSpec((d, tn), lambda i, j: (0, j))],
        out_specs=pl.BlockSpec((tm, tn), lambda i, j: (i, j)),
        out_shape=jax.ShapeDtypeStruct((t, n), F32),
        scratch_shapes=[pltpu.VMEM((tm, d), BF16)],
        compiler_params=_params("parallel", "arbitrary"),
        name="in_proj",
    )(x, g.reshape(1, d), w)


def _merge_body(x_ref, gate_ref, ya_ref, yb_ref, yc_ref, yd_ref,
                wa_ref, wb_ref, wc_ref, wd_ref, wo_ref, o_ref):
    def branch(k, y_ref, w_ref):
        p = jnp.dot(y_ref[...].astype(BF16), w_ref[...], preferred_element_type=F32)
        return jax.nn.sigmoid(gate_ref[:, k * D_MODEL:(k + 1) * D_MODEL]) * p

    merged = (branch(0, ya_ref, wa_ref) + branch(1, yb_ref, wb_ref)
              + branch(2, yc_ref, wc_ref) + branch(3, yd_ref, wd_ref))
    o_ref[...] = x_ref[...] + jnp.dot(merged.astype(BF16), wo_ref[...],
                                      preferred_element_type=F32)


def merge(x, proj, ys, ws, w_out, tm=256):
    t, d = x.shape
    gate_blk = OFF_GATE // (N_BRANCH * D_MODEL)
    row = lambda i: (i, 0)
    fixed = lambda i: (0, 0)
    in_specs = [pl.BlockSpec((tm, d), row),
                pl.BlockSpec((tm, N_BRANCH * D_MODEL), lambda i: (i, gate_blk))]
    in_specs += [pl.BlockSpec((tm, y.shape[1]), row) for y in ys]
    in_specs += [pl.BlockSpec(w.shape, fixed) for w in ws]
    in_specs += [pl.BlockSpec(w_out.shape, fixed)]
    return pl.pallas_call(
        _merge_body,
        grid=(t // tm,),
        in_specs=in_specs,
        out_specs=pl.BlockSpec((tm, d), row),
        out_shape=jax.ShapeDtypeStruct((t, d), F32),
        compiler_params=_params("parallel"),
        name="merge",
    )(x, proj, *ys, *ws, w_out)


def _ffn_body(x_ref, g_ref, wg_ref, wu_ref, wd_ref, o_ref, h_ref, acc_ref):
    f = pl.program_id(1)

    @pl.when(f == 0)
    def _():
        h_ref[...] = _rms(x_ref[...], g_ref[...]).astype(BF16)
        acc_ref[...] = x_ref[...]

    h = h_ref[...]
    a = jnp.dot(h, wg_ref[...], preferred_element_type=F32)
    u = jnp.dot(h, wu_ref[...], preferred_element_type=F32)
    act = (a * jax.nn.sigmoid(a) * u).astype(BF16)
    acc_ref[...] += jnp.dot(act, wd_ref[...], preferred_element_type=F32)

    @pl.when(f == pl.num_programs(1) - 1)
    def _():
        o_ref[...] = acc_ref[...]


def dense_ffn(x, g, wg, wu, wd, tm=1024, tf=256):
    t, d = x.shape
    nf = wg.shape[1]
    return pl.pallas_call(
        _ffn_body,
        grid=(t // tm, nf // tf),
        in_specs=[pl.BlockSpec((tm, d), lambda i, f: (i, 0)),
                  pl.BlockSpec((1, d), lambda i, f: (0, 0)),
                  pl.BlockSpec((d, tf), lambda i, f: (0, f)),
                  pl.BlockSpec((d, tf), lambda i, f: (0, f)),
                  pl.BlockSpec((tf, d), lambda i, f: (f, 0))],
        out_specs=pl.BlockSpec((tm, d), lambda i, f: (i, 0)),
        out_shape=jax.ShapeDtypeStruct((t, d), F32),
        scratch_shapes=[pltpu.VMEM((tm, d), BF16), pltpu.VMEM((tm, d), F32)],
        compiler_params=_params("parallel", "arbitrary"),
        name="dense_ffn",
    )(x, g.reshape(1, d), wg, wu, wd)


def _route_body(x_ref, g_ref, rhi_ref, rlo_ref, h_ref, idx_ref, wt_ref):
    h = _rms(x_ref[...], g_ref[...])
    hi = h.astype(BF16)
    lo = (h - hi.astype(F32)).astype(BF16)
    h_ref[...] = hi
    logits = (jnp.dot(hi, rhi_ref[...], preferred_element_type=F32)
              + jnp.dot(hi, rlo_ref[...], preferred_element_type=F32)
              + jnp.dot(lo, rhi_ref[...], preferred_element_type=F32))
    col = lax.broadcasted_iota(jnp.int32, logits.shape, 1).astype(F32)
    neg = jnp.float32(-jnp.inf)
    lg = jnp.where(col < N_EXPERTS, logits, neg)
    m1 = jnp.max(lg, axis=-1, keepdims=True)
    i1 = jnp.min(jnp.where(lg == m1, col, float(LANES)), axis=-1, keepdims=True)
    lg2 = jnp.where(col == i1, neg, lg)
    m2 = jnp.max(lg2, axis=-1, keepdims=True)
    i2 = jnp.min(jnp.where(lg2 == m2, col, float(LANES)), axis=-1, keepdims=True)
    e = jnp.exp(m2 - m1)
    w1 = 1.0 / (1.0 + e)
    w2 = e / (1.0 + e)
    idx_ref[...] = jnp.where(col == 0, i1, jnp.where(col == 1, i2, 0.0)).astype(jnp.int32)
    wt_ref[...] = jnp.where(col == 0, w1, jnp.where(col == 1, w2, 0.0))


def moe_route(x, g, r_hi, r_lo, tm=512):
    t, d = x.shape
    row = lambda i: (i, 0)
    fixed = lambda i: (0, 0)
    return pl.pallas_call(
        _route_body,
        grid=(t // tm,),
        in_specs=[pl.BlockSpec((tm, d), row), pl.BlockSpec((1, d), fixed),
                  pl.BlockSpec((d, LANES), fixed), pl.BlockSpec((d, LANES), fixed)],
        out_specs=[pl.BlockSpec((tm, d), row), pl.BlockSpec((tm, LANES), row),
                   pl.BlockSpec((tm, LANES), row)],
        out_shape=[jax.ShapeDtypeStruct((t, d), BF16),
                   jax.ShapeDtypeStruct((t, LANES), jnp.int32),
                   jax.ShapeDtypeStruct((t, LANES), F32)],
        compiler_params=_params("parallel"),
        name="moe_route",
    )(x, g.reshape(1, d), r_hi, r_lo)


def _experts_body(te_ref, nu_ref, xs_ref, rw_ref, wg_ref, wu_ref, wd_ref, o_ref, acc_ref):
    i = pl.program_id(0)
    f = pl.program_id(1)
    used = i < nu_ref[0]

    @pl.when(f == 0)
    def _():
        acc_ref[...] = jnp.zeros_like(acc_ref)

    @pl.when(used)
    def _():
        h = xs_ref[...]
        a = jnp.dot(h, wg_ref[0], preferred_element_type=F32)
        u = jnp.dot(h, wu_ref[0], preferred_element_type=F32)
        act = (a * jax.nn.sigmoid(a) * u).astype(BF16)
        acc_ref[...] += jnp.dot(act, wd_ref[0], preferred_element_type=F32)

    @pl.when(f == pl.num_programs(1) - 1)
    def _():
        o_ref[...] = acc_ref[...] * rw_ref[...]


def moe_experts(tile_expert, n_used, xs, row_w, wg, wu, wd, tm, tf=512):
    p, d = xs.shape
    nf = wg.shape[2]
    grid_spec = pltpu.PrefetchScalarGridSpec(
        num_scalar_prefetch=2,
        grid=(p // tm, nf // tf),
        in_specs=[pl.BlockSpec((tm, d), lambda i, f, te, nu: (i, 0)),
                  pl.BlockSpec((tm, 1), lambda i, f, te, nu: (i, 0)),
                  pl.BlockSpec((1, d, tf), lambda i, f, te, nu: (te[i], 0, f)),
                  pl.BlockSpec((1, d, tf), lambda i, f, te, nu: (te[i], 0, f)),
                  pl.BlockSpec((1, tf, d), lambda i, f, te, nu: (te[i], f, 0))],
        out_specs=pl.BlockSpec((tm, d), lambda i, f, te, nu: (i, 0)),
        scratch_shapes=[pltpu.VMEM((tm, d), F32)],
    )
    return pl.pallas_call(
        _experts_body,
        grid_spec=grid_spec,
        out_shape=jax.ShapeDtypeStruct((p, d), F32),
        compiler_params=_params("parallel", "arbitrary"),
        name="moe_experts",
    )(tile_expert, n_used, xs, row_w, wg, wu, wd)


def _combine_norm_body(x_ref, ya_ref, yb_ref, g_ref, o_ref):
    o_ref[...] = _rms(x_ref[...] + ya_ref[...] + yb_ref[...], g_ref[...])


def combine_norm(x, ya, yb, g, tm=1024):
    t, d = x.shape
    row = lambda i: (i, 0)
    return pl.pallas_call(
        _combine_norm_body,
        grid=(t // tm,),
        in_specs=[pl.BlockSpec((tm, d), row)] * 3 + [pl.BlockSpec((1, d), lambda i: (0, 0))],
        out_specs=pl.BlockSpec((tm, d), row),
        out_shape=jax.ShapeDtypeStruct((t, d), F32),
        compiler_params=_params("parallel"),
        name="combine_norm",
    )(x, ya, yb, g.reshape(1, d))


def moe_layer(x, g_norm, router, wg, wu, wd, g_final, tm=512):
    t, d = x.shape
    r = jnp.zeros((d, LANES), F32).at[:, :N_EXPERTS].set(router)
    r_hi = r.astype(BF16)
    r_lo = (r - r_hi.astype(F32)).astype(BF16)
    h, idx, wts = moe_route(x, g_norm, r_hi, r_lo)
    e_flat = jnp.concatenate([idx[:, 0], idx[:, 1]])
    w_flat = jnp.concatenate([wts[:, 0], wts[:, 1]])
    onehot = (e_flat[:, None] == jnp.arange(N_EXPERTS)[None, :]).astype(jnp.int32)
    csum = jnp.cumsum(onehot, axis=0)
    rank = jnp.sum((csum - onehot) * onehot, axis=1)
    counts = csum[-1]
    padded = ((counts + tm - 1) // tm) * tm
    ends = jnp.cumsum(padded)
    starts = ends - padded
    dest = starts[e_flat] + rank
    n_tiles = (TOP_K * t) // tm + N_EXPERTS
    p = n_tiles * tm
    tok = jnp.concatenate([jnp.arange(t, dtype=jnp.int32)] * TOP_K)
    src = jnp.zeros((p,), jnp.int32).at[dest].set(tok)
    row_w = jnp.zeros((p,), F32).at[dest].set(w_flat)
    tile_expert = jnp.minimum(
        jnp.searchsorted(ends, jnp.arange(n_tiles, dtype=jnp.int32) * tm, side="right"),
        N_EXPERTS - 1).astype(jnp.int32)
    n_used = (ends[-1] // tm).astype(jnp.int32).reshape(1)
    xs = jnp.take(h, src, axis=0)
    ys = moe_experts(tile_expert, n_used, xs, row_w.reshape(p, 1), wg, wu, wd, tm)
    ya = jnp.take(ys, dest[:t], axis=0)
    yb = jnp.take(ys, dest[t:], axis=0)
    return combine_norm(x, ya, yb, g_final)


def _l2_norm(t):
    return t * lax.rsqrt(jnp.sum(t * t, axis=-1, keepdims=True) + 1e-6)


def _masked_softmax(logits, valid):
    logits = jnp.where(valid, logits.astype(F32), -1e30)
    return jax.nn.softmax(logits, axis=-1) * valid


def _rel_bucket(dist):
    n = jnp.maximum(dist, 0)
    max_exact = REL_BUCKETS // 2
    nf = jnp.maximum(n, 1).astype(F32)
    large = max_exact + (jnp.log(nf / max_exact) / math.log(REL_MAX_DIST / max_exact)
                         * (REL_BUCKETS - max_exact)).astype(jnp.int32)
    large = jnp.minimum(large, REL_BUCKETS - 1)
    return jnp.where(n < max_exact, n, large)


def ssm_mixer(u, a_re, a_im, log_dt, b_re, b_im, c_re, c_im, d_skip, glu_w):
    bs, s, _ = u.shape
    uf = u.reshape(bs, s, SSM_GROUPS, SSM_GROUP)
    dt = jnp.exp(log_dt)[:, None]
    mag = jnp.exp(a_re * dt)
    abar_r, abar_i = mag * jnp.cos(a_im * dt), mag * jnp.sin(a_im * dt)
    nr, ni = abar_r - 1.0, abar_i
    den = a_re * a_re + a_im * a_im
    sr, si = (nr * a_re + ni * a_im) / den, (ni * a_re - nr * a_im) / den
    bbar_r = sr[..., None] * b_re - si[..., None] * b_im
    bbar_i = sr[..., None] * b_im + si[..., None] * b_re
    bu_r = jnp.einsum('bsgp,gnp->bsgn', uf, bbar_r)
    bu_i = jnp.einsum('bsgp,gnp->bsgn', uf, bbar_i)
    ar = jnp.broadcast_to(abar_r, bu_r.shape)
    ai = jnp.broadcast_to(abar_i, bu_r.shape)

    def combine(e1, e2):
        a1r, a1i, b1r, b1i = e1
        a2r, a2i, b2r, b2i = e2
        return (a2r * a1r - a2i * a1i, a2r * a1i + a2i * a1r,
                a2r * b1r - a2i * b1i + b2r, a2r * b1i + a2i * b1r + b2i)

    _, _, st_r, st_i = lax.associative_scan(combine, (ar, ai, bu_r, bu_i), axis=1)
    y = (jnp.einsum('gpn,bsgn->bsgp', c_re, st_r) - jnp.einsum('gpn,bsgn->bsgp', c_im, st_i)
         + d_skip.reshape(SSM_GROUPS, SSM_GROUP) * uf)
    y = jax.nn.gelu(y.reshape(bs, s, SSM_WIDTH))
    return y * jax.nn.sigmoid(y @ glu_w)


def gdn_mixer(qkv, a, b, z, conv_w, a_log, dt_bias, norm_g):
    bs, s, c = qkv.shape
    H, DK, DV, CH = GDN_HEADS, GDN_DK, GDN_DV, GDN_CHUNK
    qkv = lax.conv_general_dilated(qkv, conv_w[:, None, :], window_strides=(1,),
                                   padding=[(GDN_CONV - 1, 0)],
                                   dimension_numbers=('NWC', 'WIO', 'NWC'), feature_group_count=c)
    qkv = jax.nn.silu(qkv)
    q, k, v = jnp.split(qkv, [H * DK, 2 * H * DK], axis=-1)
    q = _l2_norm(q.reshape(bs, s, H, DK)) * (DK ** -0.5)
    k = _l2_norm(k.reshape(bs, s, H, DK))
    v = v.reshape(bs, s, H, DV)
    beta = jax.nn.sigmoid(b)
    log_alpha = -jnp.exp(a_log) * jax.nn.softplus(a + dt_bias)
    nc = s // CH

    def chunks(t):
        return t.reshape(bs, nc, CH, H, -1).transpose(0, 3, 1, 2, 4)

    q, k, v = chunks(q), chunks(k), chunks(v)
    beta = chunks(beta[..., None])
    g = jnp.cumsum(chunks(log_alpha[..., None]), axis=3)
    gv = g[..., 0]
    causal = jnp.tril(jnp.ones((CH, CH), bool))
    strict = jnp.tril(jnp.ones((CH, CH), bool), -1)
    decay = jnp.exp(jnp.where(causal, gv[..., :, None] - gv[..., None, :], -jnp.inf))
    kb = k * beta
    a_mat = (jnp.where(strict, jnp.einsum('bhncd,bhnkd->bhnck', kb, k) * decay, 0.0)
             + jnp.eye(CH, dtype=F32))
    rhs = jnp.concatenate([v * beta, kb * jnp.exp(g)], axis=-1)
    sol = lax.linalg.triangular_solve(a_mat, rhs, left_side=True, lower=True, unit_diagonal=True)
    u_val, w = sol[..., :DV], sol[..., DV:]
    attn = jnp.einsum('bhncd,bhnkd->bhnck', q, k) * decay
    g_last = g[..., -1:, :]
    k_st = k * jnp.exp(g_last - g)
    q_st = q * jnp.exp(g)

    def step(state, xs):
        q_c, k_c, u_c, w_c, a_c, gl_c = xs
        v_new = u_c - jnp.einsum('bhcd,bhde->bhce', w_c, state)
        o = jnp.einsum('bhcd,bhde->bhce', q_c, state) + jnp.einsum('bhck,bhke->bhce', a_c, v_new)
        state = state * jnp.exp(gl_c) + jnp.einsum('bhcd,bhce->bhde', k_c, v_new)
        return state, o

    xs = tuple(jnp.moveaxis(t, 2, 0) for t in (q_st, k_st, u_val, w, attn, g_last))
    _, o = lax.scan(step, jnp.zeros((bs, H, DK, DV), F32), xs)
    o = o.transpose(1, 0, 3, 2, 4).reshape(bs, s, H, DV)
    o = o * lax.rsqrt(jnp.mean(o * o, axis=-1, keepdims=True) + RMS_EPS) * norm_g
    o = o * jax.nn.silu(z.reshape(bs, s, H, DV))
    return o.reshape(bs, s, H * DV)


NSA_KT = 256
NSA_NEG = -1e30
NSA_WTILES = NSA_WINDOW // LANES + 1


def _dot_nt(a, b):
    return lax.dot_general(a, b, (((1,), (1,)), ((), ())), preferred_element_type=F32)


def _bucket_table():
    n = np.arange(LANES)
    max_exact = REL_BUCKETS // 2
    nf = np.maximum(n, 1).astype(np.float32)
    large = max_exact + (np.log(nf / np.float32(max_exact)) / np.float32(math.log(REL_MAX_DIST / max_exact))
                         * np.float32(REL_BUCKETS - max_exact)).astype(np.int32)
    tbl = np.where(n < max_exact, n, np.minimum(large, REL_BUCKETS - 1))
    assert tbl[-1] == REL_BUCKETS - 1 and REL_MAX_DIST <= LANES
    return tbl


def _compress_body(x_ref, pe_ref, w1_ref, w2_ref, o_ref):
    x = x_ref[0, 0, 0]
    w1 = w1_ref[0]
    half = NSA_CMP_STRIDE * NSA_DH
    nc = x.shape[0]
    a = jnp.dot(x, w1[:half], preferred_element_type=F32)
    b = jnp.dot(x, w1[half:], preferred_element_type=F32)
    pe_h = jnp.dot(pe_ref[0], w1, preferred_element_type=F32)
    hid = a + pltpu.roll(b, nc - 1, 0) + pe_h[0:1, :]
    o_ref[0, 0, 0] = jnp.dot(jax.nn.gelu(hid).astype(BF16), w2_ref[0], preferred_element_type=F32)


def nsa_compress(x, pe, w1, w2):
    _, bs, g, nc, width = x.shape
    return pl.pallas_call(
        _compress_body,
        grid=(2, bs, g),
        in_specs=[pl.BlockSpec((1, 1, 1, nc, width), lambda i, b, j: (i, b, j, 0, 0)),
                  pl.BlockSpec((1,) + pe.shape[1:], lambda i, b, j: (i, 0, 0)),
                  pl.BlockSpec((1,) + w1.shape[1:], lambda i, b, j: (i, 0, 0)),
                  pl.BlockSpec((1,) + w2.shape[1:], lambda i, b, j: (i, 0, 0))],
        out_specs=pl.BlockSpec((1, 1, 1, nc, NSA_DH), lambda i, b, j: (i, b, j, 0, 0)),
        out_shape=jax.ShapeDtypeStruct((2, bs, g, nc, NSA_DH), F32),
        compiler_params=_params("parallel", "parallel", "parallel"),
        name="nsa_compress",
    )(x, pe, w1, w2)


def _nsa_body(near_s, near_w, near_c, q_ref, gl_ref, posq_ref, posk_ref, cpos_ref,
              ks_ref, vs_ref, kw_ref, vw_ref, kc_ref, vc_ref, agg_ref, td_ref, o_ref,
              s_ref, m_ref, l_ref, acc_ref, *, n_sel, sel_k):
    QB, HG = NSA_QBLOCK, NSA_HPG
    g = pl.program_id(1)
    qi = pl.program_id(2)
    s0 = qi * QB
    nc = kc_ref.shape[2]
    n_ct = nc // LANES
    n_kt = ks_ref.shape[2] // NSA_KT
    q = q_ref[0, 0, 0]
    tp = posq_ref[...]
    row = lax.broadcasted_iota(jnp.int32, (QB, LANES), 0)
    lane = lax.broadcasted_iota(jnp.int32, (QB, LANES), 1)
    t_tok = s0 + row

    def add_delta(col0, pk_row):
        idx = jnp.clip(tp - pk_row, 0, LANES - 1)
        for hg in range(HG):
            tb = jnp.broadcast_to(td_ref[pl.ds(g * HG + hg, 1), :], (QB, LANES))
            s_ref[hg * QB:(hg + 1) * QB, col0:col0 + LANES] += jnp.take_along_axis(tb, idx, axis=1)

    def heads(x):
        return jnp.concatenate([x] * HG, axis=0)

    gate = jax.nn.sigmoid(gl_ref[0, 0, 0])

    s_ref[:, :nc] = _dot_nt(q, kc_ref[0, 0])
    for ct in range(n_ct):
        @pl.when(near_c[qi * n_ct + ct] != 0)
        def _():
            add_delta(ct * LANES, cpos_ref[ct:ct + 1, :])
    c_id = lax.broadcasted_iota(jnp.int32, (QB, nc), 1)
    c_row = lax.broadcasted_iota(jnp.int32, (QB, nc), 0)
    ok_c = heads(jnp.where(c_id * NSA_CMP_STRIDE + (NSA_CMP_LEN - 1) <= s0 + c_row, 1.0, 0.0))
    sc = jnp.where(ok_c > 0.5, s_ref[:, :nc], NSA_NEG)
    e = jnp.exp(sc - jnp.max(sc, axis=-1, keepdims=True)) * ok_c
    p = e * (1.0 / jnp.maximum(jnp.sum(e, axis=-1, keepdims=True), 1e-30))
    o_c = jnp.dot(p.astype(BF16), vc_ref[0, 0], preferred_element_type=F32)
    o_ref[0, 0, 0] = gate[:, 0:1] * o_c
    ps = p[0:QB]
    for hg in range(1, HG):
        ps = ps + p[hg * QB:(hg + 1) * QB]
    ps_hi = ps.astype(BF16)
    ps_lo = (ps - ps_hi.astype(F32)).astype(BF16)
    imp = (jnp.dot(ps_hi, agg_ref[...], preferred_element_type=F32)
           + jnp.dot(ps_lo, agg_ref[...], preferred_element_type=F32))
    tb_blk = t_tok // NSA_SEL_LEN
    forced = jnp.where(lane == 0, 1.0, 0.0) + jnp.where(lane == tb_blk, 1.0, 0.0) \
        + jnp.where(lane == tb_blk - 1, 1.0, 0.0)
    score = jnp.where(forced > 0.5, 1e9, jnp.where(lane * NSA_SEL_LEN <= t_tok, imp, -1e9))
    score = jnp.where(lane < n_sel, score, -jnp.inf)
    lanef = lane.astype(F32)
    sel = jnp.zeros((QB, LANES), F32)
    for _ in range(sel_k):
        mx = jnp.max(score, axis=-1, keepdims=True)
        first = jnp.min(jnp.where(score == mx, lanef, float(LANES)), axis=-1, keepdims=True)
        hit = lanef == first
        sel = jnp.where(hit, 1.0, sel)
        score = jnp.where(hit, -jnp.inf, score)
    sel_b = sel.astype(BF16)

    def reset():
        m_ref[...] = jnp.full_like(m_ref, NSA_NEG)
        l_ref[...] = jnp.zeros_like(l_ref)
        acc_ref[...] = jnp.zeros_like(acc_ref)

    def tile_step(width, mask_add, v):
        s = s_ref[:, :width]
        if mask_add is not None:
            s = s + heads(mask_add)
        m_old = m_ref[...]
        m_new = jnp.maximum(m_old, jnp.max(s, axis=-1, keepdims=True))
        alpha = jnp.exp(m_old - m_new)
        p = jnp.exp(s - m_new)
        psum = p[:, :LANES]
        for c in range(1, width // LANES):
            psum = psum + p[:, c * LANES:(c + 1) * LANES]
        l_ref[...] = l_ref[...] * alpha + psum
        acc_ref[...] = acc_ref[...] * alpha + jnp.dot(p.astype(BF16), v, preferred_element_type=F32)
        m_ref[...] = m_new

    def finish():
        return acc_ref[...] * (1.0 / jnp.sum(l_ref[...], axis=-1, keepdims=True))

    blk_i = lax.broadcasted_iota(jnp.int32, (LANES, NSA_KT), 0)
    key_i = lax.broadcasted_iota(jnp.int32, (LANES, NSA_KT), 1)
    key_q = lax.broadcasted_iota(jnp.int32, (QB, NSA_KT), 1)
    row_q = lax.broadcasted_iota(jnp.int32, (QB, NSA_KT), 0)
    per_tile = NSA_KT // NSA_SEL_LEN

    def sel_tile(kt, diag):
        k0 = pl.multiple_of(kt * NSA_KT, NSA_KT)
        s_ref[:, :NSA_KT] = _dot_nt(q, ks_ref[0, 0, pl.ds(k0, NSA_KT), :])

        @pl.when(near_s[qi * n_kt + kt] != 0)
        def _():
            for c in range(NSA_KT // LANES):
                add_delta(c * LANES, posk_ref[pl.ds(kt * (NSA_KT // LANES) + c, 1), :])

        expand = jnp.where(blk_i == kt * per_tile + key_i // NSA_SEL_LEN, 1.0, 0.0).astype(BF16)
        picked = jnp.dot(sel_b, expand, preferred_element_type=F32)
        if diag:
            picked = jnp.where(k0 + key_q <= s0 + row_q, picked, 0.0)
        tile_step(NSA_KT, jnp.where(picked > 0.5, 0.0, NSA_NEG), vs_ref[0, 0, pl.ds(k0, NSA_KT), :])

    reset()
    n_past = (s0 + QB - 1) // NSA_KT

    def past_body(kt, carry):
        sel_tile(kt, False)
        return carry

    lax.fori_loop(0, n_past, past_body, 0)
    sel_tile(n_past, True)
    o_ref[0, 0, 0] += gate[:, 1:2] * finish()

    reset()
    for j in range(NSA_WTILES):
        start = s0 - NSA_WINDOW + j * LANES

        @pl.when(start >= 0)
        def _():
            st = pl.multiple_of(start, LANES)
            s_ref[:, :LANES] = _dot_nt(q, kw_ref[0, 0, pl.ds(st, LANES), :])

            @pl.when(near_w[qi * NSA_WTILES + j] != 0)
            def _():
                add_delta(0, posk_ref[pl.ds(qi - (NSA_WTILES - 1) + j, 1), :])

            if j == 0:
                mask_add = jnp.where(lane > row, 0.0, NSA_NEG)
            elif j == NSA_WTILES - 1:
                mask_add = jnp.where(lane <= row, 0.0, NSA_NEG)
            else:
                mask_add = None
            tile_step(LANES, mask_add, vw_ref[0, 0, pl.ds(st, LANES), :])

    o_ref[0, 0, 0] += gate[:, 2:3] * finish()


def nsa_mixer(q, k_c, v_c, k_s, v_s, k_w, v_w, gate_logits, positions,
              ck_pe, ck_w1, ck_w2, cv_pe, cv_w1, cv_w2, rel_bias):
    bs, s, _ = q.shape
    G, HG, DH, QB = NSA_KV_GROUPS, NSA_HPG, NSA_DH, NSA_QBLOCK
    nqt = s // QB
    nc = s // NSA_CMP_STRIDE
    n_cmp = (s - NSA_CMP_LEN) // NSA_CMP_STRIDE + 1
    n_sel = s // NSA_SEL_LEN
    sel_k = min(NSA_SEL_TOPK, n_sel)
    n_kt = s // NSA_KT
    n_ct = nc // LANES
    assert s % (LANES * NSA_CMP_STRIDE) == 0 and n_sel <= LANES and NSA_KT == 2 * QB

    def by_group(t):
        return t.reshape(bs, s, G, DH).transpose(0, 2, 1, 3)

    ones2 = jnp.zeros((LANES - DH,), F32).at[:2].set(1.0)

    def keys_aug(t):
        return jnp.concatenate([t, jnp.broadcast_to(ones2, t.shape[:-1] + (LANES - DH,))],
                               axis=-1).astype(BF16)

    chunks = jnp.stack([by_group(k_c), by_group(v_c)]).reshape(2, bs, G, nc, NSA_CMP_STRIDE * DH)
    pe = jnp.stack([ck_pe, cv_pe]).reshape(2, 1, NSA_CMP_LEN * DH)
    cmp = nsa_compress(chunks.astype(BF16), jnp.broadcast_to(pe, (2, 8, NSA_CMP_LEN * DH)).astype(BF16),
                       jnp.stack([ck_w1, cv_w1]).astype(BF16), jnp.stack([ck_w2, cv_w2]).astype(BF16))
    kc, vc = keys_aug(cmp[0]), cmp[1].astype(BF16)

    far = rel_bias[REL_BUCKETS - 1].astype(F32)
    far_hi = far.astype(BF16).astype(F32)
    extra = jnp.zeros((NSA_HEADS, LANES - DH), F32).at[:, 0].set(far_hi).at[:, 1].set(far - far_hi)
    qh = (q * (DH ** -0.5)).reshape(bs, nqt, QB, G, HG, DH).transpose(0, 3, 1, 4, 2, 5)
    ex = jnp.broadcast_to(extra.reshape(1, G, 1, HG, 1, LANES - DH), (bs, G, nqt, HG, QB, LANES - DH))
    qs = jnp.concatenate([qh, ex], axis=-1).astype(BF16).reshape(bs, G, nqt, HG * QB, LANES)
    gl = gate_logits.reshape(bs, nqt, QB, G, HG, 3).transpose(0, 3, 1, 4, 2, 5)
    gl = jnp.pad(gl, ((0, 0),) * 5 + ((0, 5),)).reshape(bs, G, nqt, HG * QB, 8)

    td = (rel_bias[_bucket_table()] - rel_bias[REL_BUCKETS - 1][None, :]).T.astype(F32)
    cmp_end = jnp.minimum(jnp.arange(nc) * NSA_CMP_STRIDE + NSA_CMP_LEN - 1, s - 1)
    cpos = positions[cmp_end]
    qmin = positions.reshape(nqt, QB).min(axis=1)
    kmax = positions.reshape(nqt, QB).max(axis=1)
    near_s = (qmin[:, None] - positions.reshape(n_kt, NSA_KT).max(axis=1)[None, :]) < REL_MAX_DIST
    wt = jnp.clip(jnp.arange(nqt)[:, None] - (NSA_WTILES - 1) + jnp.arange(NSA_WTILES)[None, :], 0, nqt - 1)
    near_w = (qmin[:, None] - kmax[wt]) < REL_MAX_DIST
    near_c = (qmin[:, None] - cpos.reshape(n_ct, LANES).max(axis=1)[None, :]) < REL_MAX_DIST

    cmp_start = np.arange(nc) * NSA_CMP_STRIDE
    sel_start = np.arange(LANES) * NSA_SEL_LEN
    agg = ((cmp_start[:, None] <= sel_start[None, :] + NSA_SEL_LEN - 1)
           & (cmp_start[:, None] + NSA_CMP_LEN - 1 >= sel_start[None, :])
           & (np.arange(nc)[:, None] < n_cmp) & (np.arange(LANES)[None, :] < n_sel))

    rows = HG * QB
    full = lambda shape: pl.BlockSpec(shape, lambda b, g, i, *_: (0,) * len(shape))
    per_bg = lambda n, w: pl.BlockSpec((1, 1, n, w), lambda b, g, i, *_: (b, g, 0, 0))
    grid_spec = pltpu.PrefetchScalarGridSpec(
        num_scalar_prefetch=3,
        grid=(bs, G, nqt),
        in_specs=[pl.BlockSpec((1, 1, 1, rows, LANES), lambda b, g, i, *_: (b, g, i, 0, 0)),
                  pl.BlockSpec((1, 1, 1, rows, 8), lambda b, g, i, *_: (b, g, i, 0, 0)),
                  pl.BlockSpec((QB, 1), lambda b, g, i, *_: (i, 0)),
                  full((nqt, LANES)), full((n_ct, LANES)),
                  per_bg(s, LANES), per_bg(s, DH), per_bg(s, LANES), per_bg(s, DH),
                  per_bg(nc, LANES), per_bg(nc, DH),
                  full((nc, LANES)), full((NSA_HEADS, LANES))],
        out_specs=pl.BlockSpec((1, 1, 1, rows, DH), lambda b, g, i, *_: (b, g, i, 0, 0)),
        scratch_shapes=[pltpu.VMEM((rows, max(nc, NSA_KT)), F32), pltpu.VMEM((rows, 1), F32),
                        pltpu.VMEM((rows, LANES), F32), pltpu.VMEM((rows, DH), F32)],
    )
    out = pl.pallas_call(
        functools.partial(_nsa_body, n_sel=n_sel, sel_k=sel_k),
        grid_spec=grid_spec,
        out_shape=jax.ShapeDtypeStruct((bs, G, nqt, rows, DH), F32),
        compiler_params=_params("parallel", "parallel", "arbitrary"),
        name="nsa_attention",
    )(near_s.reshape(-1).astype(jnp.int32), near_w.reshape(-1).astype(jnp.int32),
      near_c.reshape(-1).astype(jnp.int32),
      qs, gl, positions.reshape(s, 1), positions.reshape(nqt, LANES), cpos.reshape(n_ct, LANES),
      keys_aug(by_group(k_s)), by_group(v_s).astype(BF16), keys_aug(by_group(k_w)),
      by_group(v_w).astype(BF16), kc, vc, jnp.asarray(agg, BF16), td)
    out = out.reshape(bs, G, nqt, HG, QB, DH).transpose(0, 2, 4, 1, 3, 5)
    return out.reshape(bs, s, G * HG * DH)


def _nsa_mixer_jax(q, k_c, v_c, k_s, v_s, k_w, v_w, gate_logits, positions,
                   ck_pe, ck_w1, ck_w2, cv_pe, cv_w1, cv_w2, rel_bias):
    bs, s, _ = q.shape
    G, HG, DH, QB, W = NSA_KV_GROUPS, NSA_HPG, NSA_DH, NSA_QBLOCK, NSA_WINDOW
    q = q.reshape(bs, s, G, HG, DH) * (DH ** -0.5)

    def kv(t):
        return t.reshape(bs, s, G, DH)

    k_c, v_c, k_s, v_s, k_w, v_w = (kv(t) for t in (k_c, v_c, k_s, v_s, k_w, v_w))
    gates = jax.nn.sigmoid(gate_logits).reshape(bs, s, G, HG, 3)
    n_cmp = (s - NSA_CMP_LEN) // NSA_CMP_STRIDE + 1
    cmp_start = jnp.arange(n_cmp) * NSA_CMP_STRIDE
    cmp_end = cmp_start + NSA_CMP_LEN - 1
    cmp_tok = cmp_start[:, None] + jnp.arange(NSA_CMP_LEN)[None, :]

    def compress(t, pe, w1, w2):
        blk = t[:, cmp_tok] + pe[None, None, :, None, :]
        blk = blk.transpose(0, 1, 3, 2, 4).reshape(bs, n_cmp, G, NSA_CMP_LEN * DH)
        return jax.nn.gelu(blk @ w1) @ w2

    kc = compress(k_c, ck_pe, ck_w1, ck_w2)
    vc = compress(v_c, cv_pe, cv_w1, cv_w2)
    cmp_pos = positions[cmp_end]
    n_sel = s // NSA_SEL_LEN
    sel_k = min(NSA_SEL_TOPK, n_sel)
    sel_start = jnp.arange(n_sel) * NSA_SEL_LEN
    agg = ((cmp_start[:, None] <= sel_start[None, :] + NSA_SEL_LEN - 1)
           & (cmp_end[:, None] >= sel_start[None, :])).astype(F32)
    ks_blk = k_s.reshape(bs, n_sel, NSA_SEL_LEN, G, DH).transpose(0, 3, 1, 2, 4)
    vs_blk = v_s.reshape(bs, n_sel, NSA_SEL_LEN, G, DH).transpose(0, 3, 1, 2, 4)
    pos_blk = positions.reshape(n_sel, NSA_SEL_LEN)
    kw_pad = jnp.pad(k_w, ((0, 0), (W, 0), (0, 0), (0, 0)))
    vw_pad = jnp.pad(v_w, ((0, 0), (W, 0), (0, 0), (0, 0)))
    pos_pad = jnp.pad(positions, (W, 0))
    tbl = rel_bias.reshape(REL_BUCKETS, G, HG)
    b_ix = jnp.arange(bs)[:, None, None, None]
    g_ix = jnp.arange(G)[None, :, None, None]
    sel_offs = jnp.arange(NSA_SEL_LEN)
    win_offs = jnp.arange(QB + W)
    blk_j = jnp.arange(n_sel)

    def one_block(qi):
        s0 = qi * QB
        qb = lax.dynamic_slice_in_dim(q, s0, QB, axis=1)
        t = s0 + jnp.arange(QB)
        tp = lax.dynamic_slice_in_dim(positions, s0, QB)
        bias_c = tbl[_rel_bucket(tp[:, None] - cmp_pos[None, :])].transpose(2, 3, 0, 1)
        lg_c = jnp.einsum('bqghd,bcgd->bghqc', qb, kc) + bias_c
        p_c = _masked_softmax(lg_c, cmp_end[None, :] <= t[:, None])
        o_c = jnp.einsum('bghqc,bcgd->bqghd', p_c, vc)
        imp = jnp.einsum('bghqc,cj->bgqj', p_c, agg)
        tb = t // NSA_SEL_LEN
        forced = ((blk_j[None, :] == 0) | (blk_j[None, :] == tb[:, None])
                  | (blk_j[None, :] == tb[:, None] - 1))
        eligible = sel_start[None, :] <= t[:, None]
        score = jnp.where(forced, 1e9, jnp.where(eligible, imp, -1e9))
        _, sel = lax.top_k(score, sel_k)
        k_sel = ks_blk[b_ix, g_ix, sel]
        v_sel = vs_blk[b_ix, g_ix, sel]
        tok_s = sel[..., None] * NSA_SEL_LEN + sel_offs
        bias_s = tbl[_rel_bucket(tp[:, None, None] - pos_blk[sel]), g_ix[..., None]]
        lg_s = jnp.einsum('bqghd,bgqnkd->bghqnk', qb, k_sel) + bias_s.transpose(0, 1, 5, 2, 3, 4)
        valid_s = (tok_s <= t[:, None, None])[:, :, None].reshape(bs, G, 1, QB, -1)
        p_s = _masked_softmax(lg_s.reshape(bs, G, HG, QB, -1), valid_s)
        p_s = p_s.reshape(bs, G, HG, QB, sel_k, NSA_SEL_LEN)
        o_s = jnp.einsum('bghqnk,bgqnkd->bqghd', p_s, v_sel)
        kwb = lax.dynamic_slice_in_dim(kw_pad, s0, QB + W, axis=1)
        vwb = lax.dynamic_slice_in_dim(vw_pad, s0, QB + W, axis=1)
        tok_w = s0 - W + win_offs
        pos_w = lax.dynamic_slice_in_dim(pos_pad, s0, QB + W)
        d_tok = t[:, None] - tok_w[None, :]
        valid_w = (d_tok >= 0) & (d_tok < W) & (tok_w[None, :] >= 0)
        bias_w = tbl[_rel_bucket(tp[:, None] - pos_w[None, :])].transpose(2, 3, 0, 1)
        lg_w = jnp.einsum('bqghd,bkgd->bghqk', qb, kwb) + bias_w
        p_w = _masked_softmax(lg_w, valid_w)
        o_w = jnp.einsum('bghqk,bkgd->bqghd', p_w, vwb)
        gb = lax.dynamic_slice_in_dim(gates, s0, QB, axis=1)
        return gb[..., 0:1] * o_c + gb[..., 1:2] * o_s + gb[..., 2:3] * o_w

    out = lax.map(one_block, jnp.arange(s // QB))
    return out.transpose(1, 0, 2, 3, 4, 5).reshape(bs, s, G * HG * DH)


def sgu_mixer(uv, ln_g, ln_b, w_s, b_s):
    bs, s, _ = uv.shape
    uv = jax.nn.gelu(uv)
    u, v = uv[..., :SGU_WIDTH], uv[..., SGU_WIDTH:]
    mu = jnp.mean(v, axis=-1, keepdims=True)
    var = jnp.mean(jnp.square(v - mu), axis=-1, keepdims=True)
    v = (v - mu) * lax.rsqrt(var + RMS_EPS) * ln_g + ln_b
    nc = s // SGU_CHUNK
    v = v.reshape(bs, nc, SGU_CHUNK, SGU_GROUPS, SGU_WIDTH // SGU_GROUPS)
    mix = (jnp.einsum('gij,bcjgd->bcigd', jnp.tril(w_s), v) + b_s.T[None, None, :, :, None])
    return u * mix.reshape(bs, s, SGU_WIDTH)


def _permute_w_in(w):
    sizes = ([SSM_WIDTH, GDN_QKV, GDN_HEADS, GDN_HEADS, GDN_HEADS * GDN_DV, NSA_Q]
             + [NSA_KV] * 6 + [3 * NSA_HEADS, 2 * SGU_WIDTH, N_BRANCH * D_MODEL])
    cuts = [int(c) for c in np.cumsum(sizes)[:-1]]
    (w_ssm, w_qkv, w_a, w_b, w_z, w_nq, w_kc, w_vc, w_ks, w_vs, w_kw, w_vw,
     w_ng, w_sgu, w_gate) = jnp.split(w, cuts, axis=-1)
    pad = jnp.zeros((w.shape[0], PROJ_COLS - OFF_SMALL - SMALL_USED), w.dtype)
    return jnp.concatenate([w_qkv, w_ssm, w_z, w_nq, w_sgu, w_gate, w_kc, w_vc, w_ks, w_vs,
                            w_kw, w_vw, w_a, w_b, w_ng, pad], axis=-1).astype(BF16)


def kernel(x, positions, norm_mix, norm_ffn, norm_final, w_in, ssm_a_re, ssm_a_im, ssm_log_dt, ssm_b_re, ssm_b_im, ssm_c_re, ssm_c_im, ssm_d, ssm_glu_w, gdn_conv_w, gdn_a_log, gdn_dt_bias, gdn_norm, nsa_cmp_k_pe, nsa_cmp_k_w1, nsa_cmp_k_w2, nsa_cmp_v_pe, nsa_cmp_v_w1, nsa_cmp_v_w2, rel_bias, sgu_ln_g, sgu_ln_b, sgu_w, sgu_b, w_br_ssm, w_br_gdn, w_br_nsa, w_br_sgu, w_out, ffn_w_gate, ffn_w_up, ffn_w_down, moe_router, moe_w_gate, moe_w_up, moe_w_down):
    bs, s, d = x.shape
    t = bs * s
    xf = x.reshape(t, d)
    for layer in range(DEPTH):
        proj = in_proj(xf, norm_mix[layer], _permute_w_in(w_in[layer]))
        p3 = proj.reshape(bs, s, PROJ_COLS)

        def seg(off, width):
            return p3[..., off:off + width]

        y_ssm = ssm_mixer(seg(OFF_SSM, SSM_WIDTH), ssm_a_re[layer], ssm_a_im[layer],
                          ssm_log_dt[layer], ssm_b_re[layer], ssm_b_im[layer], ssm_c_re[layer],
                          ssm_c_im[layer], ssm_d[layer], ssm_glu_w[layer])
        y_gdn = gdn_mixer(seg(OFF_QKV, GDN_QKV), seg(OFF_SMALL, GDN_HEADS),
                          seg(OFF_SMALL + GDN_HEADS, GDN_HEADS), seg(OFF_Z, GDN_HEADS * GDN_DV),
                          gdn_conv_w[layer], gdn_a_log[layer], gdn_dt_bias[layer], gdn_norm[layer])
        kvs = [seg(OFF_KV + i * NSA_KV, NSA_KV) for i in range(6)]
        y_nsa = nsa_mixer(seg(OFF_NQ, NSA_Q), *kvs, seg(OFF_SMALL + 2 * GDN_HEADS, 3 * NSA_HEADS),
                          positions, nsa_cmp_k_pe[layer], nsa_cmp_k_w1[layer], nsa_cmp_k_w2[layer],
                          nsa_cmp_v_pe[layer], nsa_cmp_v_w1[layer], nsa_cmp_v_w2[layer], rel_bias)
        y_sgu = sgu_mixer(seg(OFF_SGU, 2 * SGU_WIDTH), sgu_ln_g[layer], sgu_ln_b[layer],
                          sgu_w[layer], sgu_b[layer])
        ys = [y.reshape(t, -1) for y in (y_ssm, y_gdn, y_nsa, y_sgu)]
        ws = [w[layer].astype(BF16) for w in (w_br_ssm, w_br_gdn, w_br_nsa, w_br_sgu)]
        xf = merge(xf, proj, ys, ws, w_out[layer].astype(BF16))
        i = layer // 2
        if layer % 2 == 0:
            xf = dense_ffn(xf, norm_ffn[layer], ffn_w_gate[i].astype(BF16),
                           ffn_w_up[i].astype(BF16), ffn_w_down[i].astype(BF16))
        else:
            xf = moe_layer(xf, norm_ffn[layer], moe_router[i], moe_w_gate[i].astype(BF16),
                           moe_w_up[i].astype(BF16), moe_w_down[i].astype(BF16), norm_final)
    return xf.reshape(bs, s, d)
```

```python
import functools
import math

import jax
import jax.numpy as jnp
from jax import lax
import numpy as np
from jax.experimental import pallas as pl
from jax.experimental.pallas import tpu as pltpu

F32 = jnp.float32
BF16 = jnp.bfloat16

D_MODEL = 1024
DEPTH = 2
SSM_WIDTH = D_MODEL // 2
SSM_GROUP = 16
SSM_GROUPS = SSM_WIDTH // SSM_GROUP
SSM_STATE = 64
GDN_HEADS = 4
GDN_DK = 128
GDN_DV = 128
GDN_CONV = 4
GDN_CHUNK = 64
NSA_HEADS = 8
NSA_KV_GROUPS = 2
NSA_HPG = NSA_HEADS // NSA_KV_GROUPS
NSA_DH = 64
NSA_CMP_LEN = 32
NSA_CMP_STRIDE = 16
NSA_CMP_HIDDEN = 128
NSA_SEL_LEN = 64
NSA_SEL_TOPK = 16
NSA_WINDOW = 512
NSA_QBLOCK = 128
SGU_WIDTH = D_MODEL // 2
SGU_GROUPS = 4
SGU_CHUNK = 128
REL_BUCKETS = 32
REL_MAX_DIST = 128
FFN_DENSE = 2816
N_EXPERTS = 8
TOP_K = 2
FFN_EXPERT = 3584
N_BRANCH = 4
RMS_EPS = 1e-6
GDN_QKV = GDN_HEADS * (2 * GDN_DK + GDN_DV)
NSA_Q = NSA_HEADS * NSA_DH
NSA_KV = NSA_KV_GROUPS * NSA_DH

LANES = 128
VMEM_LIMIT = 48 * 1024 * 1024

OFF_QKV = 0
OFF_SSM = OFF_QKV + GDN_QKV
OFF_Z = OFF_SSM + SSM_WIDTH
OFF_NQ = OFF_Z + GDN_HEADS * GDN_DV
OFF_SGU = OFF_NQ + NSA_Q
OFF_GATE = OFF_SGU + 2 * SGU_WIDTH
OFF_KV = OFF_GATE + N_BRANCH * D_MODEL
OFF_SMALL = OFF_KV + 6 * NSA_KV
SMALL_USED = 2 * GDN_HEADS + 3 * NSA_HEADS
PROJ_COLS = 9216


def _params(*sem):
    return pltpu.CompilerParams(dimension_semantics=sem, vmem_limit_bytes=VMEM_LIMIT)


def _rms(x, g):
    return x * lax.rsqrt(jnp.mean(x * x, axis=-1, keepdims=True) + RMS_EPS) * g


def _in_proj_body(x_ref, g_ref, w_ref, o_ref, h_ref):
    @pl.when(pl.program_id(1) == 0)
    def _():
        h_ref[...] = _rms(x_ref[...], g_ref[...]).astype(BF16)

    o_ref[...] = jnp.dot(h_ref[...], w_ref[...], preferred_element_type=F32)


def in_proj(x, g, w, tm=1024, tn=512):
    t, d = x.shape
    n = w.shape[1]
    return pl.pallas_call(
        _in_proj_body,
        grid=(t // tm, n // tn),
        in_specs=[pl.BlockSpec((tm, d), lambda i, j: (i, 0)),
                  pl.BlockSpec((1, d), lambda i, j: (0, 0)),
                  pl.BlockSpec((d, tn), lambda i, j: (0, j))],
        out_specs=pl.BlockSpec((tm, tn), lambda i, j: (i, j)),
        out_shape=jax.ShapeDtypeStruct((t, n), F32),
        scratch_shapes=[pltpu.VMEM((tm, d), BF16)],
        compiler_params=_params("parallel", "arbitrary"),
        name="in_proj",
    )(x, g.reshape(1, d), w)


def _merge_body(x_ref, gate_ref, ya_ref, yb_ref, yc_ref, yd_ref,
                wa_ref, wb_ref, wc_ref, wd_ref, wo_ref, o_ref):
    def branch(k, y_ref, w_ref):
        p = jnp.dot(y_ref[...].astype(BF16), w_ref[...], preferred_element_type=F32)
        return jax.nn.sigmoid(gate_ref[:, k * D_MODEL:(k + 1) * D_MODEL]) * p

    merged = (branch(0, ya_ref, wa_ref) + branch(1, yb_ref, wb_ref)
              + branch(2, yc_ref, wc_ref) + branch(3, yd_ref, wd_ref))
    o_ref[...] = x_ref[...] + jnp.dot(merged.astype(BF16), wo_ref[...],
                                      preferred_element_type=F32)


def merge(x, proj, ys, ws, w_out, tm=256):
    t, d = x.shape
    gate_blk = OFF_GATE // (N_BRANCH * D_MODEL)
    row = lambda i: (i, 0)
    fixed = lambda i: (0, 0)
    in_specs = [pl.BlockSpec((tm, d), row),
                pl.BlockSpec((tm, N_BRANCH * D_MODEL), lambda i: (i, gate_blk))]
    in_specs += [pl.BlockSpec((tm, y.shape[1]), row) for y in ys]
    in_specs += [pl.BlockSpec(w.shape, fixed) for w in ws]
    in_specs += [pl.BlockSpec(w_out.shape, fixed)]
    return pl.pallas_call(
        _merge_body,
        grid=(t // tm,),
        in_specs=in_specs,
        out_specs=pl.BlockSpec((tm, d), row),
        out_shape=jax.ShapeDtypeStruct((t, d), F32),
        compiler_params=_params("parallel"),
        name="merge",
    )(x, proj, *ys, *ws, w_out)


def _ffn_body(x_ref, g_ref, wg_ref, wu_ref, wd_ref, o_ref, h_ref, acc_ref):
    f = pl.program_id(1)

    @pl.when(f == 0)
    def _():
        h_ref[...] = _rms(x_ref[...], g_ref[...]).astype(BF16)
        acc_ref[...] = x_ref[...]

    h = h_ref[...]
    a = jnp.dot(h, wg_ref[...], preferred_element_type=F32)
    u = jnp.dot(h, wu_ref[...], preferred_element_type=F32)
    act = (a * jax.nn.sigmoid(a) * u).astype(BF16)
    acc_ref[...] += jnp.dot(act, wd_ref[...], preferred_element_type=F32)

    @pl.when(f == pl.num_programs(1) - 1)
    def _():
        o_ref[...] = acc_ref[...]


def dense_ffn(x, g, wg, wu, wd, tm=1024, tf=256):
    t, d = x.shape
    nf = wg.shape[1]
    return pl.pallas_call(
        _ffn_body,
        grid=(t // tm, nf // tf),
        in_specs=[pl.BlockSpec((tm, d), lambda i, f: (i, 0)),
                  pl.BlockSpec((1, d), lambda i, f: (0, 0)),
                  pl.BlockSpec((d, tf), lambda i, f: (0, f)),
                  pl.BlockSpec((d, tf), lambda i, f: (0, f)),
                  pl.BlockSpec((tf, d), lambda i, f: (f, 0))],
        out_specs=pl.BlockSpec((tm, d), lambda i, f: (i, 0)),
        out_shape=jax.ShapeDtypeStruct((t, d), F32),
        scratch_shapes=[pltpu.VMEM((tm, d), BF16), pltpu.VMEM((tm, d), F32)],
        compiler_params=_params("parallel", "arbitrary"),
        name="dense_ffn",
    )(x, g.reshape(1, d), wg, wu, wd)


def _route_body(x_ref, g_ref, rhi_ref, rlo_ref, h_ref, idx_ref, wt_ref):
    h = _rms(x_ref[...], g_ref[...])
    hi = h.astype(BF16)
    lo = (h - hi.astype(F32)).astype(BF16)
    h_ref[...] = hi
    logits = (jnp.dot(hi, rhi_ref[...], preferred_element_type=F32)
              + jnp.dot(hi, rlo_ref[...], preferred_element_type=F32)
              + jnp.dot(lo, rhi_ref[...], preferred_element_type=F32))
    col = lax.broadcasted_iota(jnp.int32, logits.shape, 1).astype(F32)
    neg = jnp.float32(-jnp.inf)
    lg = jnp.where(col < N_EXPERTS, logits, neg)
    m1 = jnp.max(lg, axis=-1, keepdims=True)
    i1 = jnp.min(jnp.where(lg == m1, col, float(LANES)), axis=-1, keepdims=True)
    lg2 = jnp.where(col == i1, neg, lg)
    m2 = jnp.max(lg2, axis=-1, keepdims=True)
    i2 = jnp.min(jnp.where(lg2 == m2, col, float(LANES)), axis=-1, keepdims=True)
    e = jnp.exp(m2 - m1)
    w1 = 1.0 / (1.0 + e)
    w2 = e / (1.0 + e)
    idx_ref[...] = jnp.where(col == 0, i1, jnp.where(col == 1, i2, 0.0)).astype(jnp.int32)
    wt_ref[...] = jnp.where(col == 0, w1, jnp.where(col == 1, w2, 0.0))


def moe_route(x, g, r_hi, r_lo, tm=512):
    t, d = x.shape
    row = lambda i: (i, 0)
    fixed = lambda i: (0, 0)
    return pl.pallas_call(
        _route_body,
        grid=(t // tm,),
        in_specs=[pl.BlockSpec((tm, d), row), pl.BlockSpec((1, d), fixed),
                  pl.BlockSpec((d, LANES), fixed), pl.BlockSpec((d, LANES), fixed)],
        out_specs=[pl.BlockSpec((tm, d), row), pl.BlockSpec((tm, LANES), row),
                   pl.BlockSpec((tm, LANES), row)],
        out_shape=[jax.ShapeDtypeStruct((t, d), BF16),
                   jax.ShapeDtypeStruct((t, LANES), jnp.int32),
                   jax.ShapeDtypeStruct((t, LANES), F32)],
        compiler_params=_params("parallel"),
        name="moe_route",
    )(x, g.reshape(1, d), r_hi, r_lo)


def _experts_body(te_ref, nu_ref, xs_ref, rw_ref, wg_ref, wu_ref, wd_ref, o_ref, acc_ref):
    i = pl.program_id(0)
    f = pl.program_id(1)
    used = i < nu_ref[0]

    @pl.when(f == 0)
    def _():
        acc_ref[...] = jnp.zeros_like(acc_ref)

    @pl.when(used)
    def _():
        h = xs_ref[...]
        a = jnp.dot(h, wg_ref[0], preferred_element_type=F32)
        u = jnp.dot(h, wu_ref[0], preferred_element_type=F32)
        act = (a * jax.nn.sigmoid(a) * u).astype(BF16)
        acc_ref[...] += jnp.dot(act, wd_ref[0], preferred_element_type=F32)

    @pl.when(f == pl.num_programs(1) - 1)
    def _():
        o_ref[...] = acc_ref[...] * rw_ref[...]


def moe_experts(tile_expert, n_used, xs, row_w, wg, wu, wd, tm, tf=512):
    p, d = xs.shape
    nf = wg.shape[2]
    grid_spec = pltpu.PrefetchScalarGridSpec(
        num_scalar_prefetch=2,
        grid=(p // tm, nf // tf),
        in_specs=[pl.BlockSpec((tm, d), lambda i, f, te, nu: (i, 0)),
                  pl.BlockSpec((tm, 1), lambda i, f, te, nu: (i, 0)),
                  pl.BlockSpec((1, d, tf), lambda i, f, te, nu: (te[i], 0, f)),
                  pl.BlockSpec((1, d, tf), lambda i, f, te, nu: (te[i], 0, f)),
                  pl.BlockSpec((1, tf, d), lambda i, f, te, nu: (te[i], f, 0))],
        out_specs=pl.BlockSpec((tm, d), lambda i, f, te, nu: (i, 0)),
        scratch_shapes=[pltpu.VMEM((tm, d), F32)],
    )
    return pl.pallas_call(
        _experts_body,
        grid_spec=grid_spec,
        out_shape=jax.ShapeDtypeStruct((p, d), F32),
        compiler_params=_params("parallel", "arbitrary"),
        name="moe_experts",
    )(tile_expert, n_used, xs, row_w, wg, wu, wd)


def _combine_norm_body(x_ref, ya_ref, yb_ref, g_ref, o_ref):
    o_ref[...] = _rms(x_ref[...] + ya_ref[...] + yb_ref[...], g_ref[...])


def combine_norm(x, ya, yb, g, tm=1024):
    t, d = x.shape
    row = lambda i: (i, 0)
    return pl.pallas_call(
        _combine_norm_body,
        grid=(t // tm,),
        in_specs=[pl.BlockSpec((tm, d), row)] * 3 + [pl.BlockSpec((1, d), lambda i: (0, 0))],
        out_specs=pl.BlockSpec((tm, d), row),
        out_shape=jax.ShapeDtypeStruct((t, d), F32),
        compiler_params=_params("parallel"),
        name="combine_norm",
    )(x, ya, yb, g.reshape(1, d))


def moe_layer(x, g_norm, router, wg, wu, wd, g_final, tm=512):
    t, d = x.shape
    r = jnp.zeros((d, LANES), F32).at[:, :N_EXPERTS].set(router)
    r_hi = r.astype(BF16)
    r_lo = (r - r_hi.astype(F32)).astype(BF16)
    h, idx, wts = moe_route(x, g_norm, r_hi, r_lo)
    e_flat = jnp.concatenate([idx[:, 0], idx[:, 1]])
    w_flat = jnp.concatenate([wts[:, 0], wts[:, 1]])
    onehot = (e_flat[:, None] == jnp.arange(N_EXPERTS)[None, :]).astype(jnp.int32)
    csum = jnp.cumsum(onehot, axis=0)
    rank = jnp.sum((csum - onehot) * onehot, axis=1)
    counts = csum[-1]
    padded = ((counts + tm - 1) // tm) * tm
    ends = jnp.cumsum(padded)
    starts = ends - padded
    dest = starts[e_flat] + rank
    n_tiles = (TOP_K * t) // tm + N_EXPERTS
    p = n_tiles * tm
    tok = jnp.concatenate([jnp.arange(t, dtype=jnp.int32)] * TOP_K)
    src = jnp.zeros((p,), jnp.int32).at[dest].set(tok)
    row_w = jnp.zeros((p,), F32).at[dest].set(w_flat)
    tile_expert = jnp.minimum(
        jnp.searchsorted(ends, jnp.arange(n_tiles, dtype=jnp.int32) * tm, side="right"),
        N_EXPERTS - 1).astype(jnp.int32)
    n_used = (ends[-1] // tm).astype(jnp.int32).reshape(1)
    xs = jnp.take(h, src, axis=0)
    ys = moe_experts(tile_expert, n_used, xs, row_w.reshape(p, 1), wg, wu, wd, tm)
    ya = jnp.take(ys, dest[:t], axis=0)
    yb = jnp.take(ys, dest[t:], axis=0)
    return combine_norm(x, ya, yb, g_final)


def _l2_norm(t):
    return t * lax.rsqrt(jnp.sum(t * t, axis=-1, keepdims=True) + 1e-6)


def _masked_softmax(logits, valid):
    logits = jnp.where(valid, logits.astype(F32), -1e30)
    return jax.nn.softmax(logits, axis=-1) * valid


def _rel_bucket(dist):
    n = jnp.maximum(dist, 0)
    max_exact = REL_BUCKETS // 2
    nf = jnp.maximum(n, 1).astype(F32)
    large = max_exact + (jnp.log(nf / max_exact) / math.log(REL_MAX_DIST / max_exact)
                         * (REL_BUCKETS - max_exact)).astype(jnp.int32)
    large = jnp.minimum(large, REL_BUCKETS - 1)
    return jnp.where(n < max_exact, n, large)


SSM_TC = 512
SSM_SUB = 128
SSM_HALF = 256
SSM_NSTATE = SSM_GROUPS * SSM_STATE


def _ssm_body(u_ref, bw_ref, cw_ref, d_ref, glu_ref, ar_ref, ai_ref, pr_ref, pi_ref, o_ref,
              xr_ref, xi_ref, cr_ref, ci_ref):
    @pl.when(pl.program_id(1) == 0)
    def _():
        cr_ref[...] = jnp.zeros_like(cr_ref)
        ci_ref[...] = jnp.zeros_like(ci_ref)

    u = u_ref[0]
    ub = u.astype(BF16)
    nblk = SSM_WIDTH // SSM_HALF
    sblk = SSM_NSTATE // nblk
    for k in range(nblk):
        bu = jnp.dot(ub[:, k * SSM_HALF:(k + 1) * SSM_HALF], bw_ref[k], preferred_element_type=F32)
        xr_ref[:, k * sblk:(k + 1) * sblk] = bu[:, :sblk]
        xi_ref[:, k * sblk:(k + 1) * sblk] = bu[:, sblk:]

    row = lax.broadcasted_iota(jnp.int32, (SSM_SUB, LANES), 0)
    n_steps = SSM_SUB.bit_length() - 1

    def shift(x, d):
        if d % 8 == 0:
            return jnp.concatenate([jnp.zeros((d, LANES), F32), x[:SSM_SUB - d]], axis=0)
        return jnp.where(row >= d, pltpu.roll(x, d, 0), 0.0)

    def strip(c, carry):
        col = pl.ds(pl.multiple_of(c * LANES, LANES), LANES)
        c_r = cr_ref[:, col]
        c_i = ci_ref[:, col]
        for sub in range(SSM_TC // SSM_SUB):
            rows = slice(sub * SSM_SUB, (sub + 1) * SSM_SUB)
            xr = xr_ref[rows, col]
            xi = xi_ref[rows, col]
            for i in range(n_steps):
                a_r = ar_ref[i:i + 1, col]
                a_i = ai_ref[i:i + 1, col]
                sr, si = shift(xr, 1 << i), shift(xi, 1 << i)
                xr, xi = xr + a_r * sr - a_i * si, xi + a_r * si + a_i * sr
            p_r = pr_ref[:, col]
            p_i = pi_ref[:, col]
            xr, xi = xr + p_r * c_r - p_i * c_i, xi + p_r * c_i + p_i * c_r
            xr_ref[rows, col] = xr
            xi_ref[rows, col] = xi
            c_r = xr[SSM_SUB - 1:SSM_SUB, :]
            c_i = xi[SSM_SUB - 1:SSM_SUB, :]
        cr_ref[:, col] = c_r
        ci_ref[:, col] = c_i
        return carry

    lax.fori_loop(0, SSM_NSTATE // LANES, strip, 0)

    ys = []
    for k in range(nblk):
        st = jnp.concatenate([xr_ref[:, k * sblk:(k + 1) * sblk].astype(BF16),
                              xi_ref[:, k * sblk:(k + 1) * sblk].astype(BF16)], axis=1)
        ys.append(jnp.dot(st, cw_ref[k], preferred_element_type=F32))
    y = jax.nn.gelu(jnp.concatenate(ys, axis=1) + d_ref[...] * u)
    o_ref[0] = y * jax.nn.sigmoid(jnp.dot(y.astype(BF16), glu_ref[...], preferred_element_type=F32))


def ssm_mixer(u, a_re, a_im, log_dt, b_re, b_im, c_re, c_im, d_skip, glu_w):
    bs, s, _ = u.shape
    assert s % SSM_TC == 0
    nblk = SSM_WIDTH // SSM_HALF
    gpb = SSM_GROUPS // nblk
    dt = jnp.exp(log_dt)[:, None]
    mag = jnp.exp(a_re * dt)
    abar_r, abar_i = mag * jnp.cos(a_im * dt), mag * jnp.sin(a_im * dt)
    nr, ni = abar_r - 1.0, abar_i
    den = a_re * a_re + a_im * a_im
    sr, si = (nr * a_re + ni * a_im) / den, (ni * a_re - nr * a_im) / den
    bbar_r = sr[..., None] * b_re - si[..., None] * b_im
    bbar_i = sr[..., None] * b_im + si[..., None] * b_re

    def powers(k):
        kk = k.astype(F32)[:, None, None]
        m = jnp.exp(kk * (a_re * dt))
        return ((m * jnp.cos(kk * (a_im * dt))).reshape(-1, SSM_NSTATE),
                (m * jnp.sin(kk * (a_im * dt))).reshape(-1, SSM_NSTATE))

    ar, ai = powers(2 ** jnp.arange(8))
    pr, pi = powers(jnp.arange(1, SSM_SUB + 1))
    eye = jnp.eye(gpb, dtype=F32)

    def in_block(w):
        return jnp.einsum('gnp,gh->gphn', w, eye).reshape(gpb * SSM_GROUP, gpb * SSM_STATE)

    def out_block(w):
        return jnp.einsum('gpn,gh->gnhp', w, eye).reshape(gpb * SSM_STATE, gpb * SSM_GROUP)

    bw = jnp.stack([jnp.concatenate([in_block(bbar_r[k * gpb:(k + 1) * gpb]),
                                     in_block(bbar_i[k * gpb:(k + 1) * gpb])], axis=1)
                    for k in range(nblk)]).astype(BF16)
    cw = jnp.stack([jnp.concatenate([out_block(c_re[k * gpb:(k + 1) * gpb]),
                                     -out_block(c_im[k * gpb:(k + 1) * gpb])], axis=0)
                    for k in range(nblk)]).astype(BF16)
    fixed2 = lambda b, i: (0, 0)
    fixed3 = lambda b, i: (0, 0, 0)
    return pl.pallas_call(
        _ssm_body,
        grid=(bs, s // SSM_TC),
        in_specs=[pl.BlockSpec((1, SSM_TC, SSM_WIDTH), lambda b, i: (b, i, 0)),
                  pl.BlockSpec(bw.shape, fixed3), pl.BlockSpec(cw.shape, fixed3),
                  pl.BlockSpec((1, SSM_WIDTH), fixed2), pl.BlockSpec((SSM_WIDTH, SSM_WIDTH), fixed2),
                  pl.BlockSpec((8, SSM_NSTATE), fixed2), pl.BlockSpec((8, SSM_NSTATE), fixed2),
                  pl.BlockSpec((SSM_SUB, SSM_NSTATE), fixed2), pl.BlockSpec((SSM_SUB, SSM_NSTATE), fixed2)],
        out_specs=pl.BlockSpec((1, SSM_TC, SSM_WIDTH), lambda b, i: (b, i, 0)),
        out_shape=jax.ShapeDtypeStruct((bs, s, SSM_WIDTH), F32),
        scratch_shapes=[pltpu.VMEM((SSM_TC, SSM_NSTATE), F32), pltpu.VMEM((SSM_TC, SSM_NSTATE), F32),
                        pltpu.VMEM((1, SSM_NSTATE), F32), pltpu.VMEM((1, SSM_NSTATE), F32)],
        compiler_params=_params("parallel", "arbitrary"),
        name="ssm_scan",
    )(u, bw, cw, d_skip.reshape(1, SSM_WIDTH), glu_w.astype(BF16), ar, ai, pr, pi)


def _ssm_mixer_jax(u, a_re, a_im, log_dt, b_re, b_im, c_re, c_im, d_skip, glu_w):
    bs, s, _ = u.shape
    uf = u.reshape(bs, s, SSM_GROUPS, SSM_GROUP)
    dt = jnp.exp(log_dt)[:, None]
    mag = jnp.exp(a_re * dt)
    abar_r, abar_i = mag * jnp.cos(a_im * dt), mag * jnp.sin(a_im * dt)
    nr, ni = abar_r - 1.0, abar_i
    den = a_re * a_re + a_im * a_im
    sr, si = (nr * a_re + ni * a_im) / den, (ni * a_re - nr * a_im) / den
    bbar_r = sr[..., None] * b_re - si[..., None] * b_im
    bbar_i = sr[..., None] * b_im + si[..., None] * b_re
    bu_r = jnp.einsum('bsgp,gnp->bsgn', uf, bbar_r)
    bu_i = jnp.einsum('bsgp,gnp->bsgn', uf, bbar_i)
    ar = jnp.broadcast_to(abar_r, bu_r.shape)
    ai = jnp.broadcast_to(abar_i, bu_r.shape)

    def combine(e1, e2):
        a1r, a1i, b1r, b1i = e1
        a2r, a2i, b2r, b2i = e2
        return (a2r * a1r - a2i * a1i, a2r * a1i + a2i * a1r,
                a2r * b1r - a2i * b1i + b2r, a2r * b1i + a2i * b1r + b2i)

    _, _, st_r, st_i = lax.associative_scan(combine, (ar, ai, bu_r, bu_i), axis=1)
    y = (jnp.einsum('gpn,bsgn->bsgp', c_re, st_r) - jnp.einsum('gpn,bsgn->bsgp', c_im, st_i)
         + d_skip.reshape(SSM_GROUPS, SSM_GROUP) * uf)
    y = jax.nn.gelu(y.reshape(bs, s, SSM_WIDTH))
    return y * jax.nn.sigmoid(y @ glu_w)


def gdn_mixer(qkv, a, b, z, conv_w, a_log, dt_bias, norm_g):
    bs, s, c = qkv.shape
    H, DK, DV, CH = GDN_HEADS, GDN_DK, GDN_DV, GDN_CHUNK
    qkv = lax.conv_general_dilated(qkv, conv_w[:, None, :], window_strides=(1,),
                                   padding=[(GDN_CONV - 1, 0)],
                                   dimension_numbers=('NWC', 'WIO', 'NWC'), feature_group_count=c)
    qkv = jax.nn.silu(qkv)
    q, k, v = jnp.split(qkv, [H * DK, 2 * H * DK], axis=-1)
    q = _l2_norm(q.reshape(bs, s, H, DK)) * (DK ** -0.5)
    k = _l2_norm(k.reshape(bs, s, H, DK))
    v = v.reshape(bs, s, H, DV)
    beta = jax.nn.sigmoid(b)
    log_alpha = -jnp.exp(a_log) * jax.nn.softplus(a + dt_bias)
    nc = s // CH

    def chunks(t):
        return t.reshape(bs, nc, CH, H, -1).transpose(0, 3, 1, 2, 4)

    q, k, v = chunks(q), chunks(k), chunks(v)
    beta = chunks(beta[..., None])
    g = jnp.cumsum(chunks(log_alpha[..., None]), axis=3)
    gv = g[..., 0]
    causal = jnp.tril(jnp.ones((CH, CH), bool))
    strict = jnp.tril(jnp.ones((CH, CH), bool), -1)
    decay = jnp.exp(jnp.where(causal, gv[..., :, None] - gv[..., None, :], -jnp.inf))
    kb = k * beta
    a_mat = (jnp.where(strict, jnp.einsum('bhncd,bhnkd->bhnck', kb, k) * decay, 0.0)
             + jnp.eye(CH, dtype=F32))
    rhs = jnp.concatenate([v * beta, kb * jnp.exp(g)], axis=-1)
    sol = lax.linalg.triangular_solve(a_mat, rhs, left_side=True, lower=True, unit_diagonal=True)
    u_val, w = sol[..., :DV], sol[..., DV:]
    attn = jnp.einsum('bhncd,bhnkd->bhnck', q, k) * decay
    g_last = g[..., -1:, :]
    k_st = k * jnp.exp(g_last - g)
    q_st = q * jnp.exp(g)

    def step(state, xs):
        q_c, k_c, u_c, w_c, a_c, gl_c = xs
        v_new = u_c - jnp.einsum('bhcd,bhde->bhce', w_c, state)
        o = jnp.einsum('bhcd,bhde->bhce', q_c, state) + jnp.einsum('bhck,bhke->bhce', a_c, v_new)
        state = state * jnp.exp(gl_c) + jnp.einsum('bhcd,bhce->bhde', k_c, v_new)
        return state, o

    xs = tuple(jnp.moveaxis(t, 2, 0) for t in (q_st, k_st, u_val, w, attn, g_last))
    _, o = lax.scan(step, jnp.zeros((bs, H, DK, DV), F32), xs)
    o = o.transpose(1, 0, 3, 2, 4).reshape(bs, s, H, DV)
    o = o * lax.rsqrt(jnp.mean(o * o, axis=-1, keepdims=True) + RMS_EPS) * norm_g
    o = o * jax.nn.silu(z.reshape(bs, s, H, DV))
    return o.reshape(bs, s, H * DV)


NSA_KT = 256
NSA_NEG = -1e30
NSA_WTILES = NSA_WINDOW // LANES + 1


def _dot_nt(a, b):
    return lax.dot_general(a, b, (((1,), (1,)), ((), ())), preferred_element_type=F32)


def _bucket_table():
    n = np.arange(LANES)
    max_exact = REL_BUCKETS // 2
    nf = np.maximum(n, 1).astype(np.float32)
    large = max_exact + (np.log(nf / np.float32(max_exact)) / np.float32(math.log(REL_MAX_DIST / max_exact))
                         * np.float32(REL_BUCKETS - max_exact)).astype(np.int32)
    tbl = np.where(n < max_exact, n, np.minimum(large, REL_BUCKETS - 1))
    assert tbl[-1] == REL_BUCKETS - 1 and REL_MAX_DIST <= LANES
    return tbl


def _compress_body(x_ref, pe_ref, w1_ref, w2_ref, o_ref):
    x = x_ref[0, 0, 0]
    w1 = w1_ref[0]
    half = NSA_CMP_STRIDE * NSA_DH
    nc = x.shape[0]
    a = jnp.dot(x, w1[:half], preferred_element_type=F32)
    b = jnp.dot(x, w1[half:], preferred_element_type=F32)
    pe_h = jnp.dot(pe_ref[0], w1, preferred_element_type=F32)
    hid = a + pltpu.roll(b, nc - 1, 0) + pe_h[0:1, :]
    o_ref[0, 0, 0] = jnp.dot(jax.nn.gelu(hid).astype(BF16), w2_ref[0], preferred_element_type=F32)


def nsa_compress(x, pe, w1, w2):
    _, bs, g, nc, width = x.shape
    return pl.pallas_call(
        _compress_body,
        grid=(2, bs, g),
        in_specs=[pl.BlockSpec((1, 1, 1, nc, width), lambda i, b, j: (i, b, j, 0, 0)),
                  pl.BlockSpec((1,) + pe.shape[1:], lambda i, b, j: (i, 0, 0)),
                  pl.BlockSpec((1,) + w1.shape[1:], lambda i, b, j: (i, 0, 0)),
                  pl.BlockSpec((1,) + w2.shape[1:], lambda i, b, j: (i, 0, 0))],
        out_specs=pl.BlockSpec((1, 1, 1, nc, NSA_DH), lambda i, b, j: (i, b, j, 0, 0)),
        out_shape=jax.ShapeDtypeStruct((2, bs, g, nc, NSA_DH), F32),
        compiler_params=_params("parallel", "parallel", "parallel"),
        name="nsa_compress",
    )(x, pe, w1, w2)


def _nsa_body(near_s, near_w, near_c, q_ref, gl_ref, posq_ref, posk_ref, cpos_ref,
              ks_ref, vs_ref, kw_ref, vw_ref, kc_ref, vc_ref, agg_ref, td_ref, o_ref,
              s_ref, m_ref, l_ref, acc_ref, *, n_sel, sel_k):
    QB, HG = NSA_QBLOCK, NSA_HPG
    g = pl.program_id(1)
    qi = pl.program_id(2)
    s0 = qi * QB
    nc = kc_ref.shape[2]
    n_ct = nc // LANES
    n_kt = ks_ref.shape[2] // NSA_KT
    q = q_ref[0, 0, 0]
    tp = posq_ref[...]
    row = lax.broadcasted_iota(jnp.int32, (QB, LANES), 0)
    lane = lax.broadcasted_iota(jnp.int32, (QB, LANES), 1)
    t_tok = s0 + row

    def add_delta(col0, pk_row):
        idx = jnp.clip(tp - pk_row, 0, LANES - 1)
        for hg in range(HG):
            tb = jnp.broadcast_to(td_ref[pl.ds(g * HG + hg, 1), :], (QB, LANES))
            s_ref[hg * QB:(hg + 1) * QB, col0:col0 + LANES] += jnp.take_along_axis(tb, idx, axis=1)

    def heads(x):
        return jnp.concatenate([x] * HG, axis=0)

    gate = jax.nn.sigmoid(gl_ref[0, 0, 0])

    s_ref[:, :nc] = _dot_nt(q, kc_ref[0, 0])
    for ct in range(n_ct):
        @pl.when(near_c[qi * n_ct + ct] != 0)
        def _():
            add_delta(ct * LANES, cpos_ref[ct:ct + 1, :])
    c_id = lax.broadcasted_iota(jnp.int32, (QB, nc), 1)
    c_row = lax.broadcasted_iota(jnp.int32, (QB, nc), 0)
    ok_c = heads(jnp.where(c_id * NSA_CMP_STRIDE + (NSA_CMP_LEN - 1) <= s0 + c_row, 1.0, 0.0))
    sc = jnp.where(ok_c > 0.5, s_ref[:, :nc], NSA_NEG)
    e = jnp.exp(sc - jnp.max(sc, axis=-1, keepdims=True)) * ok_c
    p = e * (1.0 / jnp.maximum(jnp.sum(e, axis=-1, keepdims=True), 1e-30))
    o_c = jnp.dot(p.astype(BF16), vc_ref[0, 0], preferred_element_type=F32)
    o_ref[0, 0, 0] = gate[:, 0:1] * o_c
    ps = p[0:QB]
    for hg in range(1, HG):
        ps = ps + p[hg * QB:(hg + 1) * QB]
    ps_hi = ps.astype(BF16)
    ps_lo = (ps - ps_hi.astype(F32)).astype(BF16)
    imp = (jnp.dot(ps_hi, agg_ref[...], preferred_element_type=F32)
           + jnp.dot(ps_lo, agg_ref[...], preferred_element_type=F32))
    tb_blk = t_tok // NSA_SEL_LEN
    forced = jnp.where(lane == 0, 1.0, 0.0) + jnp.where(lane == tb_blk, 1.0, 0.0) \
        + jnp.where(lane == tb_blk - 1, 1.0, 0.0)
    score = jnp.where(forced > 0.5, 1e9, jnp.where(lane * NSA_SEL_LEN <= t_tok, imp, -1e9))
    score = jnp.where(lane < n_sel, score, -jnp.inf)
    lanef = lane.astype(F32)
    sel = jnp.zeros((QB, LANES), F32)
    for _ in range(sel_k):
        mx = jnp.max(score, axis=-1, keepdims=True)
        first = jnp.min(jnp.where(score == mx, lanef, float(LANES)), axis=-1, keepdims=True)
        hit = lanef == first
        sel = jnp.where(hit, 1.0, sel)
        score = jnp.where(hit, -jnp.inf, score)
    sel_b = sel.astype(BF16)

    def reset():
        m_ref[...] = jnp.full_like(m_ref, NSA_NEG)
        l_ref[...] = jnp.zeros_like(l_ref)
        acc_ref[...] = jnp.zeros_like(acc_ref)

    def tile_step(width, mask_add, v):
        s = s_ref[:, :width]
        if mask_add is not None:
            s = s + heads(mask_add)
        m_old = m_ref[...]
        m_new = jnp.maximum(m_old, jnp.max(s, axis=-1, keepdims=True))
        alpha = jnp.exp(m_old - m_new)
        p = jnp.exp(s - m_new)
        psum = p[:, :LANES]
        for c in range(1, width // LANES):
            psum = psum + p[:, c * LANES:(c + 1) * LANES]
        l_ref[...] = l_ref[...] * alpha + psum
        acc_ref[...] = acc_ref[...] * alpha + jnp.dot(p.astype(BF16), v, preferred_element_type=F32)
        m_ref[...] = m_new

    def finish():
        return acc_ref[...] * (1.0 / jnp.sum(l_ref[...], axis=-1, keepdims=True))

    blk_i = lax.broadcasted_iota(jnp.int32, (LANES, NSA_KT), 0)
    key_i = lax.broadcasted_iota(jnp.int32, (LANES, NSA_KT), 1)
    key_q = lax.broadcasted_iota(jnp.int32, (QB, NSA_KT), 1)
    row_q = lax.broadcasted_iota(jnp.int32, (QB, NSA_KT), 0)
    per_tile = NSA_KT // NSA_SEL_LEN

    def sel_tile(kt, diag):
        k0 = pl.multiple_of(kt * NSA_KT, NSA_KT)
        s_ref[:, :NSA_KT] = _dot_nt(q, ks_ref[0, 0, pl.ds(k0, NSA_KT), :])

        @pl.when(near_s[qi * n_kt + kt] != 0)
        def _():
            for c in range(NSA_KT // LANES):
                add_delta(c * LANES, posk_ref[pl.ds(kt * (NSA_KT // LANES) + c, 1), :])

        expand = jnp.where(blk_i == kt * per_tile + key_i // NSA_SEL_LEN, 1.0, 0.0).astype(BF16)
        picked = jnp.dot(sel_b, expand, preferred_element_type=F32)
        if diag:
            picked = jnp.where(k0 + key_q <= s0 + row_q, picked, 0.0)
        tile_step(NSA_KT, jnp.where(picked > 0.5, 0.0, NSA_NEG), vs_ref[0, 0, pl.ds(k0, NSA_KT), :])

    reset()
    n_past = (s0 + QB - 1) // NSA_KT

    def past_body(kt, carry):
        sel_tile(kt, False)
        return carry

    lax.fori_loop(0, n_past, past_body, 0)
    sel_tile(n_past, True)
    o_ref[0, 0, 0] += gate[:, 1:2] * finish()

    reset()
    for j in range(NSA_WTILES):
        start = s0 - NSA_WINDOW + j * LANES

        @pl.when(start >= 0)
        def _():
            st = pl.multiple_of(start, LANES)
            s_ref[:, :LANES] = _dot_nt(q, kw_ref[0, 0, pl.ds(st, LANES), :])

            @pl.when(near_w[qi * NSA_WTILES + j] != 0)
            def _():
                add_delta(0, posk_ref[pl.ds(qi - (NSA_WTILES - 1) + j, 1), :])

            if j == 0:
                mask_add = jnp.where(lane > row, 0.0, NSA_NEG)
            elif j == NSA_WTILES - 1:
                mask_add = jnp.where(lane <= row, 0.0, NSA_NEG)
            else:
                mask_add = None
            tile_step(LANES, mask_add, vw_ref[0, 0, pl.ds(st, LANES), :])

    o_ref[0, 0, 0] += gate[:, 2:3] * finish()


def nsa_mixer(q, k_c, v_c, k_s, v_s, k_w, v_w, gate_logits, positions,
              ck_pe, ck_w1, ck_w2, cv_pe, cv_w1, cv_w2, rel_bias):
    bs, s, _ = q.shape
    G, HG, DH, QB = NSA_KV_GROUPS, NSA_HPG, NSA_DH, NSA_QBLOCK
    nqt = s // QB
    nc = s // NSA_CMP_STRIDE
    n_cmp = (s - NSA_CMP_LEN) // NSA_CMP_STRIDE + 1
    n_sel = s // NSA_SEL_LEN
    sel_k = min(NSA_SEL_TOPK, n_sel)
    n_kt = s // NSA_KT
    n_ct = nc // LANES
    assert s % (LANES * NSA_CMP_STRIDE) == 0 and n_sel <= LANES and NSA_KT == 2 * QB

    def by_group(t):
        return t.reshape(bs, s, G, DH).transpose(0, 2, 1, 3)

    ones2 = jnp.zeros((LANES - DH,), F32).at[:2].set(1.0)

    def keys_aug(t):
        return jnp.concatenate([t, jnp.broadcast_to(ones2, t.shape[:-1] + (LANES - DH,))],
                               axis=-1).astype(BF16)

    chunks = jnp.stack([by_group(k_c), by_group(v_c)]).reshape(2, bs, G, nc, NSA_CMP_STRIDE * DH)
    pe = jnp.stack([ck_pe, cv_pe]).reshape(2, 1, NSA_CMP_LEN * DH)
    cmp = nsa_compress(chunks.astype(BF16), jnp.broadcast_to(pe, (2, 8, NSA_CMP_LEN * DH)).astype(BF16),
                       jnp.stack([ck_w1, cv_w1]).astype(BF16), jnp.stack([ck_w2, cv_w2]).astype(BF16))
    kc, vc = keys_aug(cmp[0]), cmp[1].astype(BF16)

    far = rel_bias[REL_BUCKETS - 1].astype(F32)
    far_hi = far.astype(BF16).astype(F32)
    extra = jnp.zeros((NSA_HEADS, LANES - DH), F32).at[:, 0].set(far_hi).at[:, 1].set(far - far_hi)
    qh = (q * (DH ** -0.5)).reshape(bs, nqt, QB, G, HG, DH).transpose(0, 3, 1, 4, 2, 5)
    ex = jnp.broadcast_to(extra.reshape(1, G, 1, HG, 1, LANES - DH), (bs, G, nqt, HG, QB, LANES - DH))
    qs = jnp.concatenate([qh, ex], axis=-1).astype(BF16).reshape(bs, G, nqt, HG * QB, LANES)
    gl = gate_logits.reshape(bs, nqt, QB, G, HG, 3).transpose(0, 3, 1, 4, 2, 5)
    gl = jnp.pad(gl, ((0, 0),) * 5 + ((0, 5),)).reshape(bs, G, nqt, HG * QB, 8)

    td = (rel_bias[_bucket_table()] - rel_bias[REL_BUCKETS - 1][None, :]).T.astype(F32)
    cmp_end = jnp.minimum(jnp.arange(nc) * NSA_CMP_STRIDE + NSA_CMP_LEN - 1, s - 1)
    cpos = positions[cmp_end]
    qmin = positions.reshape(nqt, QB).min(axis=1)
    kmax = positions.reshape(nqt, QB).max(axis=1)
    near_s = (qmin[:, None] - positions.reshape(n_kt, NSA_KT).max(axis=1)[None, :]) < REL_MAX_DIST
    wt = jnp.clip(jnp.arange(nqt)[:, None] - (NSA_WTILES - 1) + jnp.arange(NSA_WTILES)[None, :], 0, nqt - 1)
    near_w = (qmin[:, None] - kmax[wt]) < REL_MAX_DIST
    near_c = (qmin[:, None] - cpos.reshape(n_ct, LANES).max(axis=1)[None, :]) < REL_MAX_DIST

    cmp_start = np.arange(nc) * NSA_CMP_STRIDE
    sel_start = np.arange(LANES) * NSA_SEL_LEN
    agg = ((cmp_start[:, None] <= sel_start[None, :] + NSA_SEL_LEN - 1)
           & (cmp_start[:, None] + NSA_CMP_LEN - 1 >= sel_start[None, :])
           & (np.arange(nc)[:, None] < n_cmp) & (np.arange(LANES)[None, :] < n_sel))

    rows = HG * QB
    full = lambda shape: pl.BlockSpec(shape, lambda b, g, i, *_: (0,) * len(shape))
    per_bg = lambda n, w: pl.BlockSpec((1, 1, n, w), lambda b, g, i, *_: (b, g, 0, 0))
    grid_spec = pltpu.PrefetchScalarGridSpec(
        num_scalar_prefetch=3,
        grid=(bs, G, nqt),
        in_specs=[pl.BlockSpec((1, 1, 1, rows, LANES), lambda b, g, i, *_: (b, g, i, 0, 0)),
                  pl.BlockSpec((1, 1, 1, rows, 8), lambda b, g, i, *_: (b, g, i, 0, 0)),
                  pl.BlockSpec((QB, 1), lambda b, g, i, *_: (i, 0)),
                  full((nqt, LANES)), full((n_ct, LANES)),
                  per_bg(s, LANES), per_bg(s, DH), per_bg(s, LANES), per_bg(s, DH),
                  per_bg(nc, LANES), per_bg(nc, DH),
                  full((nc, LANES)), full((NSA_HEADS, LANES))],
        out_specs=pl.BlockSpec((1, 1, 1, rows, DH), lambda b, g, i, *_: (b, g, i, 0, 0)),
        scratch_shapes=[pltpu.VMEM((rows, max(nc, NSA_KT)), F32), pltpu.VMEM((rows, 1), F32),
                        pltpu.VMEM((rows, LANES), F32), pltpu.VMEM((rows, DH), F32)],
    )
    out = pl.pallas_call(
        functools.partial(_nsa_body, n_sel=n_sel, sel_k=sel_k),
        grid_spec=grid_spec,
        out_shape=jax.ShapeDtypeStruct((bs, G, nqt, rows, DH), F32),
        compiler_params=_params("parallel", "parallel", "arbitrary"),
        name="nsa_attention",
    )(near_s.reshape(-1).astype(jnp.int32), near_w.reshape(-1).astype(jnp.int32),
      near_c.reshape(-1).astype(jnp.int32),
      qs, gl, positions.reshape(s, 1), positions.reshape(nqt, LANES), cpos.reshape(n_ct, LANES),
      keys_aug(by_group(k_s)), by_group(v_s).astype(BF16), keys_aug(by_group(k_w)),
      by_group(v_w).astype(BF16), kc, vc, jnp.asarray(agg, BF16), td)
    out = out.reshape(bs, G, nqt, HG, QB, DH).transpose(0, 2, 4, 1, 3, 5)
    return out.reshape(bs, s, G * HG * DH)


def _nsa_mixer_jax(q, k_c, v_c, k_s, v_s, k_w, v_w, gate_logits, positions,
                   ck_pe, ck_w1, ck_w2, cv_pe, cv_w1, cv_w2, rel_bias):
    bs, s, _ = q.shape
    G, HG, DH, QB, W = NSA_KV_GROUPS, NSA_HPG, NSA_DH, NSA_QBLOCK, NSA_WINDOW
    q = q.reshape(bs, s, G, HG, DH) * (DH ** -0.5)

    def kv(t):
        return t.reshape(bs, s, G, DH)

    k_c, v_c, k_s, v_s, k_w, v_w = (kv(t) for t in (k_c, v_c, k_s, v_s, k_w, v_w))
    gates = jax.nn.sigmoid(gate_logits).reshape(bs, s, G, HG, 3)
    n_cmp = (s - NSA_CMP_LEN) // NSA_CMP_STRIDE + 1
    cmp_start = jnp.arange(n_cmp) * NSA_CMP_STRIDE
    cmp_end = cmp_start + NSA_CMP_LEN - 1
    cmp_tok = cmp_start[:, None] + jnp.arange(NSA_CMP_LEN)[None, :]

    def compress(t, pe, w1, w2):
        blk = t[:, cmp_tok] + pe[None, None, :, None, :]
        blk = blk.transpose(0, 1, 3, 2, 4).reshape(bs, n_cmp, G, NSA_CMP_LEN * DH)
        return jax.nn.gelu(blk @ w1) @ w2

    kc = compress(k_c, ck_pe, ck_w1, ck_w2)
    vc = compress(v_c, cv_pe, cv_w1, cv_w2)
    cmp_pos = positions[cmp_end]
    n_sel = s // NSA_SEL_LEN
    sel_k = min(NSA_SEL_TOPK, n_sel)
    sel_start = jnp.arange(n_sel) * NSA_SEL_LEN
    agg = ((cmp_start[:, None] <= sel_start[None, :] + NSA_SEL_LEN - 1)
           & (cmp_end[:, None] >= sel_start[None, :])).astype(F32)
    ks_blk = k_s.reshape(bs, n_sel, NSA_SEL_LEN, G, DH).transpose(0, 3, 1, 2, 4)
    vs_blk = v_s.reshape(bs, n_sel, NSA_SEL_LEN, G, DH).transpose(0, 3, 1, 2, 4)
    pos_blk = positions.reshape(n_sel, NSA_SEL_LEN)
    kw_pad = jnp.pad(k_w, ((0, 0), (W, 0), (0, 0), (0, 0)))
    vw_pad = jnp.pad(v_w, ((0, 0), (W, 0), (0, 0), (0, 0)))
    pos_pad = jnp.pad(positions, (W, 0))
    tbl = rel_bias.reshape(REL_BUCKETS, G, HG)
    b_ix = jnp.arange(bs)[:, None, None, None]
    g_ix = jnp.arange(G)[None, :, None, None]
    sel_offs = jnp.arange(NSA_SEL_LEN)
    win_offs = jnp.arange(QB + W)
    blk_j = jnp.arange(n_sel)

    def one_block(qi):
        s0 = qi * QB
        qb = lax.dynamic_slice_in_dim(q, s0, QB, axis=1)
        t = s0 + jnp.arange(QB)
        tp = lax.dynamic_slice_in_dim(positions, s0, QB)
        bias_c = tbl[_rel_bucket(tp[:, None] - cmp_pos[None, :])].transpose(2, 3, 0, 1)
        lg_c = jnp.einsum('bqghd,bcgd->bghqc', qb, kc) + bias_c
        p_c = _masked_softmax(lg_c, cmp_end[None, :] <= t[:, None])
        o_c = jnp.einsum('bghqc,bcgd->bqghd', p_c, vc)
        imp = jnp.einsum('bghqc,cj->bgqj', p_c, agg)
        tb = t // NSA_SEL_LEN
        forced = ((blk_j[None, :] == 0) | (blk_j[None, :] == tb[:, None])
                  | (blk_j[None, :] == tb[:, None] - 1))
        eligible = sel_start[None, :] <= t[:, None]
        score = jnp.where(forced, 1e9, jnp.where(eligible, imp, -1e9))
        _, sel = lax.top_k(score, sel_k)
        k_sel = ks_blk[b_ix, g_ix, sel]
        v_sel = vs_blk[b_ix, g_ix, sel]
        tok_s = sel[..., None] * NSA_SEL_LEN + sel_offs
        bias_s = tbl[_rel_bucket(tp[:, None, None] - pos_blk[sel]), g_ix[..., None]]
        lg_s = jnp.einsum('bqghd,bgqnkd->bghqnk', qb, k_sel) + bias_s.transpose(0, 1, 5, 2, 3, 4)
        valid_s = (tok_s <= t[:, None, None])[:, :, None].reshape(bs, G, 1, QB, -1)
        p_s = _masked_softmax(lg_s.reshape(bs, G, HG, QB, -1), valid_s)
        p_s = p_s.reshape(bs, G, HG, QB, sel_k, NSA_SEL_LEN)
        o_s = jnp.einsum('bghqnk,bgqnkd->bqghd', p_s, v_sel)
        kwb = lax.dynamic_slice_in_dim(kw_pad, s0, QB + W, axis=1)
        vwb = lax.dynamic_slice_in_dim(vw_pad, s0, QB + W, axis=1)
        tok_w = s0 - W + win_offs
        pos_w = lax.dynamic_slice_in_dim(pos_pad, s0, QB + W)
        d_tok = t[:, None] - tok_w[None, :]
        valid_w = (d_tok >= 0) & (d_tok < W) & (tok_w[None, :] >= 0)
        bias_w = tbl[_rel_bucket(tp[:, None] - pos_w[None, :])].transpose(2, 3, 0, 1)
        lg_w = jnp.einsum('bqghd,bkgd->bghqk', qb, kwb) + bias_w
        p_w = _masked_softmax(lg_w, valid_w)
        o_w = jnp.einsum('bghqk,bkgd->bqghd', p_w, vwb)
        gb = lax.dynamic_slice_in_dim(gates, s0, QB, axis=1)
        return gb[..., 0:1] * o_c + gb[..., 1:2] * o_s + gb[..., 2:3] * o_w

    out = lax.map(one_block, jnp.arange(s // QB))
    return out.transpose(1, 0, 2, 3, 4, 5).reshape(bs, s, G * HG * DH)


def sgu_mixer(uv, ln_g, ln_b, w_s, b_s):
    bs, s, _ = uv.shape
    uv = jax.nn.gelu(uv)
    u, v = uv[..., :SGU_WIDTH], uv[..., SGU_WIDTH:]
    mu = jnp.mean(v, axis=-1, keepdims=True)
    var = jnp.mean(jnp.square(v - mu), axis=-1, keepdims=True)
    v = (v - mu) * lax.rsqrt(var + RMS_EPS) * ln_g + ln_b
    nc = s // SGU_CHUNK
    v = v.reshape(bs, nc, SGU_CHUNK, SGU_GROUPS, SGU_WIDTH // SGU_GROUPS)
    mix = (jnp.einsum('gij,bcjgd->bcigd', jnp.tril(w_s), v) + b_s.T[None, None, :, :, None])
    return u * mix.reshape(bs, s, SGU_WIDTH)


def _permute_w_in(w):
    sizes = ([SSM_WIDTH, GDN_QKV, GDN_HEADS, GDN_HEADS, GDN_HEADS * GDN_DV, NSA_Q]
             + [NSA_KV] * 6 + [3 * NSA_HEADS, 2 * SGU_WIDTH, N_BRANCH * D_MODEL])
    cuts = [int(c) for c in np.cumsum(sizes)[:-1]]
    (w_ssm, w_qkv, w_a, w_b, w_z, w_nq, w_kc, w_vc, w_ks, w_vs, w_kw, w_vw,
     w_ng, w_sgu, w_gate) = jnp.split(w, cuts, axis=-1)
    pad = jnp.zeros((w.shape[0], PROJ_COLS - OFF_SMALL - SMALL_USED), w.dtype)
    return jnp.concatenate([w_qkv, w_ssm, w_z, w_nq, w_sgu, w_gate, w_kc, w_vc, w_ks, w_vs,
                            w_kw, w_vw, w_a, w_b, w_ng, pad], axis=-1).astype(BF16)


def kernel(x, positions, norm_mix, norm_ffn, norm_final, w_in, ssm_a_re, ssm_a_im, ssm_log_dt, ssm_b_re, ssm_b_im, ssm_c_re, ssm_c_im, ssm_d, ssm_glu_w, gdn_conv_w, gdn_a_log, gdn_dt_bias, gdn_norm, nsa_cmp_k_pe, nsa_cmp_k_w1, nsa_cmp_k_w2, nsa_cmp_v_pe, nsa_cmp_v_w1, nsa_cmp_v_w2, rel_bias, sgu_ln_g, sgu_ln_b, sgu_w, sgu_b, w_br_ssm, w_br_gdn, w_br_nsa, w_br_sgu, w_out, ffn_w_gate, ffn_w_up, ffn_w_down, moe_router, moe_w_gate, moe_w_up, moe_w_down):
    bs, s, d = x.shape
    t = bs * s
    xf = x.reshape(t, d)
    for layer in range(DEPTH):
        proj = in_proj(xf, norm_mix[layer], _permute_w_in(w_in[layer]))
        p3 = proj.reshape(bs, s, PROJ_COLS)

        def seg(off, width):
            return p3[..., off:off + width]

        y_ssm = ssm_mixer(seg(OFF_SSM, SSM_WIDTH), ssm_a_re[layer], ssm_a_im[layer],
                          ssm_log_dt[layer], ssm_b_re[layer], ssm_b_im[layer], ssm_c_re[layer],
                          ssm_c_im[layer], ssm_d[layer], ssm_glu_w[layer])
        y_gdn = gdn_mixer(seg(OFF_QKV, GDN_QKV), seg(OFF_SMALL, GDN_HEADS),
                          seg(OFF_SMALL + GDN_HEADS, GDN_HEADS), seg(OFF_Z, GDN_HEADS * GDN_DV),
                          gdn_conv_w[layer], gdn_a_log[layer], gdn_dt_bias[layer], gdn_norm[layer])
        kvs = [seg(OFF_KV + i * NSA_KV, NSA_KV) for i in range(6)]
        y_nsa = nsa_mixer(seg(OFF_NQ, NSA_Q), *kvs, seg(OFF_SMALL + 2 * GDN_HEADS, 3 * NSA_HEADS),
                          positions, nsa_cmp_k_pe[layer], nsa_cmp_k_w1[layer], nsa_cmp_k_w2[layer],
                          nsa_cmp_v_pe[layer], nsa_cmp_v_w1[layer], nsa_cmp_v_w2[layer], rel_bias)
        y_sgu = sgu_mixer(seg(OFF_SGU, 2 * SGU_WIDTH), sgu_ln_g[layer], sgu_ln_b[layer],
                          sgu_w[layer], sgu_b[layer])
        ys = [y.reshape(t, -1) for y in (y_ssm, y_gdn, y_nsa, y_sgu)]
        ws = [w[layer].astype(BF16) for w in (w_br_ssm, w_br_gdn, w_br_nsa, w_br_sgu)]
        xf = merge(xf, proj, ys, ws, w_out[layer].astype(BF16))
        i = layer // 2
        if layer % 2 == 0:
            xf = dense_ffn(xf, norm_ffn[layer], ffn_w_gate[i].astype(BF16),
                           ffn_w_up[i].astype(BF16), ffn_w_down[i].astype(BF16))
        else:
            xf = moe_layer(xf, norm_ffn[layer], moe_router[i], moe_w_gate[i].astype(BF16),
                           moe_w_up[i].astype(BF16), moe_w_down[i].astype(BF16), norm_final)
    return xf.reshape(bs, s, d)
```

```python
import functools
import math

import jax
import jax.numpy as jnp
from jax import lax
import numpy as np
from jax.experimental import pallas as pl
from jax.experimental.pallas import tpu as pltpu

F32 = jnp.float32
BF16 = jnp.bfloat16

D_MODEL = 1024
DEPTH = 2
SSM_WIDTH = D_MODEL // 2
SSM_GROUP = 16
SSM_GROUPS = SSM_WIDTH // SSM_GROUP
SSM_STATE = 64
GDN_HEADS = 4
GDN_DK = 128
GDN_DV = 128
GDN_CONV = 4
GDN_CHUNK = 64
NSA_HEADS = 8
NSA_KV_GROUPS = 2
NSA_HPG = NSA_HEADS // NSA_KV_GROUPS
NSA_DH = 64
NSA_CMP_LEN = 32
NSA_CMP_STRIDE = 16
NSA_CMP_HIDDEN = 128
NSA_SEL_LEN = 64
NSA_SEL_TOPK = 16
NSA_WINDOW = 512
NSA_QBLOCK = 128
SGU_WIDTH = D_MODEL // 2
SGU_GROUPS = 4
SGU_CHUNK = 128
REL_BUCKETS = 32
REL_MAX_DIST = 128
FFN_DENSE = 2816
N_EXPERTS = 8
TOP_K = 2
FFN_EXPERT = 3584
N_BRANCH = 4
RMS_EPS = 1e-6
GDN_QKV = GDN_HEADS * (2 * GDN_DK + GDN_DV)
NSA_Q = NSA_HEADS * NSA_DH
NSA_KV = NSA_KV_GROUPS * NSA_DH

LANES = 128
VMEM_LIMIT = 48 * 1024 * 1024

OFF_QKV = 0
OFF_SSM = OFF_QKV + GDN_QKV
OFF_Z = OFF_SSM + SSM_WIDTH
OFF_NQ = OFF_Z + GDN_HEADS * GDN_DV
OFF_SGU = OFF_NQ + NSA_Q
OFF_GATE = OFF_SGU + 2 * SGU_WIDTH
OFF_KV = OFF_GATE + N_BRANCH * D_MODEL
OFF_SMALL = OFF_KV + 6 * NSA_KV
SMALL_USED = 2 * GDN_HEADS + 3 * NSA_HEADS
PROJ_COLS = 9216


def _params(*sem):
    return pltpu.CompilerParams(dimension_semantics=sem, vmem_limit_bytes=VMEM_LIMIT)


def _rms(x, g):
    return x * lax.rsqrt(jnp.mean(x * x, axis=-1, keepdims=True) + RMS_EPS) * g


def _in_proj_body(x_ref, g_ref, w_ref, o_ref, h_ref):
    @pl.when(pl.program_id(1) == 0)
    def _():
        h_ref[...] = _rms(x_ref[...], g_ref[...]).astype(BF16)

    o_ref[...] = jnp.dot(h_ref[...], w_ref[...], preferred_element_type=F32)


def in_proj(x, g, w, tm=1024, tn=512):
    t, d = x.shape
    n = w.shape[1]
    return pl.pallas_call(
        _in_proj_body,
        grid=(t // tm, n // tn),
        in_specs=[pl.BlockSpec((tm, d), lambda i, j: (i, 0)),
                  pl.BlockSpec((1, d), lambda i, j: (0, 0)),
                  pl.BlockSpec((d, tn), lambda i, j: (0, j))],
        out_specs=pl.BlockSpec((tm, tn), lambda i, j: (i, j)),
        out_shape=jax.ShapeDtypeStruct((t, n), F32),
        scratch_shapes=[pltpu.VMEM((tm, d), BF16)],
        compiler_params=_params("parallel", "arbitrary"),
        name="in_proj",
    )(x, g.reshape(1, d), w)


def _merge_body(x_ref, gate_ref, ya_ref, yb_ref, yc_ref, yd_ref,
                wa_ref, wb_ref, wc_ref, wd_ref, wo_ref, o_ref):
    def branch(k, y_ref, w_ref):
        p = jnp.dot(y_ref[...].astype(BF16), w_ref[...], preferred_element_type=F32)
        return jax.nn.sigmoid(gate_ref[:, k * D_MODEL:(k + 1) * D_MODEL]) * p

    merged = (branch(0, ya_ref, wa_ref) + branch(1, yb_ref, wb_ref)
              + branch(2, yc_ref, wc_ref) + branch(3, yd_ref, wd_ref))
    o_ref[...] = x_ref[...] + jnp.dot(merged.astype(BF16), wo_ref[...],
                                      preferred_element_type=F32)


def merge(x, proj, ys, ws, w_out, tm=256):
    t, d = x.shape
    gate_blk = OFF_GATE // (N_BRANCH * D_MODEL)
    row = lambda i: (i, 0)
    fixed = lambda i: (0, 0)
    in_specs = [pl.BlockSpec((tm, d), row),
                pl.BlockSpec((tm, N_BRANCH * D_MODEL), lambda i: (i, gate_blk))]
    in_specs += [pl.BlockSpec((tm, y.shape[1]), row) for y in ys]
    in_specs += [pl.BlockSpec(w.shape, fixed) for w in ws]
    in_specs += [pl.BlockSpec(w_out.shape, fixed)]
    return pl.pallas_call(
        _merge_body,
        grid=(t // tm,),
        in_specs=in_specs,
        out_specs=pl.BlockSpec((tm, d), row),
        out_shape=jax.ShapeDtypeStruct((t, d), F32),
        compiler_params=_params("parallel"),
        name="merge",
    )(x, proj, *ys, *ws, w_out)


def _ffn_body(x_ref, g_ref, wg_ref, wu_ref, wd_ref, o_ref, h_ref, acc_ref):
    f = pl.program_id(1)

    @pl.when(f == 0)
    def _():
        h_ref[...] = _rms(x_ref[...], g_ref[...]).astype(BF16)
        acc_ref[...] = x_ref[...]

    h = h_ref[...]
    a = jnp.dot(h, wg_ref[...], preferred_element_type=F32)
    u = jnp.dot(h, wu_ref[...], preferred_element_type=F32)
    act = (a * jax.nn.sigmoid(a) * u).astype(BF16)
    acc_ref[...] += jnp.dot(act, wd_ref[...], preferred_element_type=F32)

    @pl.when(f == pl.num_programs(1) - 1)
    def _():
        o_ref[...] = acc_ref[...]


def dense_ffn(x, g, wg, wu, wd, tm=1024, tf=256):
    t, d = x.shape
    nf = wg.shape[1]
    return pl.pallas_call(
        _ffn_body,
        grid=(t // tm, nf // tf),
        in_specs=[pl.BlockSpec((tm, d), lambda i, f: (i, 0)),
                  pl.BlockSpec((1, d), lambda i, f: (0, 0)),
                  pl.BlockSpec((d, tf), lambda i, f: (0, f)),
                  pl.BlockSpec((d, tf), lambda i, f: (0, f)),
                  pl.BlockSpec((tf, d), lambda i, f: (f, 0))],
        out_specs=pl.BlockSpec((tm, d), lambda i, f: (i, 0)),
        out_shape=jax.ShapeDtypeStruct((t, d), F32),
        scratch_shapes=[pltpu.VMEM((tm, d), BF16), pltpu.VMEM((tm, d), F32)],
        compiler_params=_params("parallel", "arbitrary"),
        name="dense_ffn",
    )(x, g.reshape(1, d), wg, wu, wd)


def _route_body(x_ref, g_ref, rhi_ref, rlo_ref, h_ref, idx_ref, wt_ref):
    h = _rms(x_ref[...], g_ref[...])
    hi = h.astype(BF16)
    lo = (h - hi.astype(F32)).astype(BF16)
    h_ref[...] = hi
    logits = (jnp.dot(hi, rhi_ref[...], preferred_element_type=F32)
              + jnp.dot(hi, rlo_ref[...], preferred_element_type=F32)
              + jnp.dot(lo, rhi_ref[...], preferred_element_type=F32))
    col = lax.broadcasted_iota(jnp.int32, logits.shape, 1).astype(F32)
    neg = jnp.float32(-jnp.inf)
    lg = jnp.where(col < N_EXPERTS, logits, neg)
    m1 = jnp.max(lg, axis=-1, keepdims=True)
    i1 = jnp.min(jnp.where(lg == m1, col, float(LANES)), axis=-1, keepdims=True)
    lg2 = jnp.where(col == i1, neg, lg)
    m2 = jnp.max(lg2, axis=-1, keepdims=True)
    i2 = jnp.min(jnp.where(lg2 == m2, col, float(LANES)), axis=-1, keepdims=True)
    e = jnp.exp(m2 - m1)
    w1 = 1.0 / (1.0 + e)
    w2 = e / (1.0 + e)
    idx_ref[...] = jnp.where(col == 0, i1, jnp.where(col == 1, i2, 0.0)).astype(jnp.int32)
    wt_ref[...] = jnp.where(col == 0, w1, jnp.where(col == 1, w2, 0.0))


def moe_route(x, g, r_hi, r_lo, tm=512):
    t, d = x.shape
    row = lambda i: (i, 0)
    fixed = lambda i: (0, 0)
    return pl.pallas_call(
        _route_body,
        grid=(t // tm,),
        in_specs=[pl.BlockSpec((tm, d), row), pl.BlockSpec((1, d), fixed),
                  pl.BlockSpec((d, LANES), fixed), pl.BlockSpec((d, LANES), fixed)],
        out_specs=[pl.BlockSpec((tm, d), row), pl.BlockSpec((tm, LANES), row),
                   pl.BlockSpec((tm, LANES), row)],
        out_shape=[jax.ShapeDtypeStruct((t, d), BF16),
                   jax.ShapeDtypeStruct((t, LANES), jnp.int32),
                   jax.ShapeDtypeStruct((t, LANES), F32)],
        compiler_params=_params("parallel"),
        name="moe_route",
    )(x, g.reshape(1, d), r_hi, r_lo)


def _experts_body(te_ref, nu_ref, xs_ref, rw_ref, wg_ref, wu_ref, wd_ref, o_ref, acc_ref):
    i = pl.program_id(0)
    f = pl.program_id(1)
    used = i < nu_ref[0]

    @pl.when(f == 0)
    def _():
        acc_ref[...] = jnp.zeros_like(acc_ref)

    @pl.when(used)
    def _():
        h = xs_ref[...]
        a = jnp.dot(h, wg_ref[0], preferred_element_type=F32)
        u = jnp.dot(h, wu_ref[0], preferred_element_type=F32)
        act = (a * jax.nn.sigmoid(a) * u).astype(BF16)
        acc_ref[...] += jnp.dot(act, wd_ref[0], preferred_element_type=F32)

    @pl.when(f == pl.num_programs(1) - 1)
    def _():
        o_ref[...] = acc_ref[...] * rw_ref[...]


def moe_experts(tile_expert, n_used, xs, row_w, wg, wu, wd, tm, tf=512):
    p, d = xs.shape
    nf = wg.shape[2]
    grid_spec = pltpu.PrefetchScalarGridSpec(
        num_scalar_prefetch=2,
        grid=(p // tm, nf // tf),
        in_specs=[pl.BlockSpec((tm, d), lambda i, f, te, nu: (i, 0)),
                  pl.BlockSpec((tm, 1), lambda i, f, te, nu: (i, 0)),
                  pl.BlockSpec((1, d, tf), lambda i, f, te, nu: (te[i], 0, f)),
                  pl.BlockSpec((1, d, tf), lambda i, f, te, nu: (te[i], 0, f)),
                  pl.BlockSpec((1, tf, d), lambda i, f, te, nu: (te[i], f, 0))],
        out_specs=pl.BlockSpec((tm, d), lambda i, f, te, nu: (i, 0)),
        scratch_shapes=[pltpu.VMEM((tm, d), F32)],
    )
    return pl.pallas_call(
        _experts_body,
        grid_spec=grid_spec,
        out_shape=jax.ShapeDtypeStruct((p, d), F32),
        compiler_params=_params("parallel", "arbitrary"),
        name="moe_experts",
    )(tile_expert, n_used, xs, row_w, wg, wu, wd)


def _combine_norm_body(x_ref, ya_ref, yb_ref, g_ref, o_ref):
    o_ref[...] = _rms(x_ref[...] + ya_ref[...] + yb_ref[...], g_ref[...])


def combine_norm(x, ya, yb, g, tm=1024):
    t, d = x.shape
    row = lambda i: (i, 0)
    return pl.pallas_call(
        _combine_norm_body,
        grid=(t // tm,),
        in_specs=[pl.BlockSpec((tm, d), row)] * 3 + [pl.BlockSpec((1, d), lambda i: (0, 0))],
        out_specs=pl.BlockSpec((tm, d), row),
        out_shape=jax.ShapeDtypeStruct((t, d), F32),
        compiler_params=_params("parallel"),
        name="combine_norm",
    )(x, ya, yb, g.reshape(1, d))


def moe_layer(x, g_norm, router, wg, wu, wd, g_final, tm=512):
    t, d = x.shape
    r = jnp.zeros((d, LANES), F32).at[:, :N_EXPERTS].set(router)
    r_hi = r.astype(BF16)
    r_lo = (r - r_hi.astype(F32)).astype(BF16)
    h, idx, wts = moe_route(x, g_norm, r_hi, r_lo)
    e_flat = jnp.concatenate([idx[:, 0], idx[:, 1]])
    w_flat = jnp.concatenate([wts[:, 0], wts[:, 1]])
    onehot = (e_flat[:, None] == jnp.arange(N_EXPERTS)[None, :]).astype(jnp.int32)
    csum = jnp.cumsum(onehot, axis=0)
    rank = jnp.sum((csum - onehot) * onehot, axis=1)
    counts = csum[-1]
    padded = ((counts + tm - 1) // tm) * tm
    ends = jnp.cumsum(padded)
    starts = ends - padded
    dest = starts[e_flat] + rank
    n_tiles = (TOP_K * t) // tm + N_EXPERTS
    p = n_tiles * tm
    tok = jnp.concatenate([jnp.arange(t, dtype=jnp.int32)] * TOP_K)
    src = jnp.zeros((p,), jnp.int32).at[dest].set(tok)
    row_w = jnp.zeros((p,), F32).at[dest].set(w_flat)
    tile_expert = jnp.minimum(
        jnp.searchsorted(ends, jnp.arange(n_tiles, dtype=jnp.int32) * tm, side="right"),
        N_EXPERTS - 1).astype(jnp.int32)
    n_used = (ends[-1] // tm).astype(jnp.int32).reshape(1)
    xs = jnp.take(h, src, axis=0)
    ys = moe_experts(tile_expert, n_used, xs, row_w.reshape(p, 1), wg, wu, wd, tm)
    ya = jnp.take(ys, dest[:t], axis=0)
    yb = jnp.take(ys, dest[t:], axis=0)
    return combine_norm(x, ya, yb, g_final)


def _l2_norm(t):
    return t * lax.rsqrt(jnp.sum(t * t, axis=-1, keepdims=True) + 1e-6)


def _masked_softmax(logits, valid):
    logits = jnp.where(valid, logits.astype(F32), -1e30)
    return jax.nn.softmax(logits, axis=-1) * valid


def _rel_bucket(dist):
    n = jnp.maximum(dist, 0)
    max_exact = REL_BUCKETS // 2
    nf = jnp.maximum(n, 1).astype(F32)
    large = max_exact + (jnp.log(nf / max_exact) / math.log(REL_MAX_DIST / max_exact)
                         * (REL_BUCKETS - max_exact)).astype(jnp.int32)
    large = jnp.minimum(large, REL_BUCKETS - 1)
    return jnp.where(n < max_exact, n, large)


SSM_TC = 512
SSM_SUB = 128
SSM_HALF = 256
SSM_NSTATE = SSM_GROUPS * SSM_STATE


def _ssm_body(u_ref, bw_ref, cw_ref, d_ref, glu_ref, ar_ref, ai_ref, pr_ref, pi_ref, o_ref,
              xr_ref, xi_ref, cr_ref, ci_ref):
    @pl.when(pl.program_id(1) == 0)
    def _():
        cr_ref[...] = jnp.zeros_like(cr_ref)
        ci_ref[...] = jnp.zeros_like(ci_ref)

    u = u_ref[0]
    ub = u.astype(BF16)
    nblk = SSM_WIDTH // SSM_HALF
    sblk = SSM_NSTATE // nblk
    for k in range(nblk):
        bu = jnp.dot(ub[:, k * SSM_HALF:(k + 1) * SSM_HALF], bw_ref[k], preferred_element_type=F32)
        xr_ref[:, k * sblk:(k + 1) * sblk] = bu[:, :sblk]
        xi_ref[:, k * sblk:(k + 1) * sblk] = bu[:, sblk:]

    row = lax.broadcasted_iota(jnp.int32, (SSM_SUB, LANES), 0)
    n_steps = SSM_SUB.bit_length() - 1

    def shift(x, d):
        if d % 8 == 0:
            return jnp.concatenate([jnp.zeros((d, LANES), F32), x[:SSM_SUB - d]], axis=0)
        return jnp.where(row >= d, pltpu.roll(x, d, 0), 0.0)

    def strip(c, carry):
        col = pl.ds(pl.multiple_of(c * LANES, LANES), LANES)
        c_r = cr_ref[:, col]
        c_i = ci_ref[:, col]
        for sub in range(SSM_TC // SSM_SUB):
            rows = slice(sub * SSM_SUB, (sub + 1) * SSM_SUB)
            xr = xr_ref[rows, col]
            xi = xi_ref[rows, col]
            for i in range(n_steps):
                a_r = ar_ref[i:i + 1, col]
                a_i = ai_ref[i:i + 1, col]
                sr, si = shift(xr, 1 << i), shift(xi, 1 << i)
                xr, xi = xr + a_r * sr - a_i * si, xi + a_r * si + a_i * sr
            p_r = pr_ref[:, col]
            p_i = pi_ref[:, col]
            xr, xi = xr + p_r * c_r - p_i * c_i, xi + p_r * c_i + p_i * c_r
            xr_ref[rows, col] = xr
            xi_ref[rows, col] = xi
            c_r = xr[SSM_SUB - 1:SSM_SUB, :]
            c_i = xi[SSM_SUB - 1:SSM_SUB, :]
        cr_ref[:, col] = c_r
        ci_ref[:, col] = c_i
        return carry

    lax.fori_loop(0, SSM_NSTATE // LANES, strip, 0)

    ys = []
    for k in range(nblk):
        st = jnp.concatenate([xr_ref[:, k * sblk:(k + 1) * sblk].astype(BF16),
                              xi_ref[:, k * sblk:(k + 1) * sblk].astype(BF16)], axis=1)
        ys.append(jnp.dot(st, cw_ref[k], preferred_element_type=F32))
    y = jax.nn.gelu(jnp.concatenate(ys, axis=1) + d_ref[...] * u)
    o_ref[0] = y * jax.nn.sigmoid(jnp.dot(y.astype(BF16), glu_ref[...], preferred_element_type=F32))


def ssm_mixer(u, a_re, a_im, log_dt, b_re, b_im, c_re, c_im, d_skip, glu_w):
    bs, s, _ = u.shape
    assert s % SSM_TC == 0
    nblk = SSM_WIDTH // SSM_HALF
    gpb = SSM_GROUPS // nblk
    dt = jnp.exp(log_dt)[:, None]
    mag = jnp.exp(a_re * dt)
    abar_r, abar_i = mag * jnp.cos(a_im * dt), mag * jnp.sin(a_im * dt)
    nr, ni = abar_r - 1.0, abar_i
    den = a_re * a_re + a_im * a_im
    sr, si = (nr * a_re + ni * a_im) / den, (ni * a_re - nr * a_im) / den
    bbar_r = sr[..., None] * b_re - si[..., None] * b_im
    bbar_i = sr[..., None] * b_im + si[..., None] * b_re

    def powers(k):
        kk = k.astype(F32)[:, None, None]
        m = jnp.exp(kk * (a_re * dt))
        return ((m * jnp.cos(kk * (a_im * dt))).reshape(-1, SSM_NSTATE),
                (m * jnp.sin(kk * (a_im * dt))).reshape(-1, SSM_NSTATE))

    ar, ai = powers(2 ** jnp.arange(8))
    pr, pi = powers(jnp.arange(1, SSM_SUB + 1))
    eye = jnp.eye(gpb, dtype=F32)

    def in_block(w):
        return jnp.einsum('gnp,gh->gphn', w, eye).reshape(gpb * SSM_GROUP, gpb * SSM_STATE)

    def out_block(w):
        return jnp.einsum('gpn,gh->gnhp', w, eye).reshape(gpb * SSM_STATE, gpb * SSM_GROUP)

    bw = jnp.stack([jnp.concatenate([in_block(bbar_r[k * gpb:(k + 1) * gpb]),
                                     in_block(bbar_i[k * gpb:(k + 1) * gpb])], axis=1)
                    for k in range(nblk)]).astype(BF16)
    cw = jnp.stack([jnp.concatenate([out_block(c_re[k * gpb:(k + 1) * gpb]),
                                     -out_block(c_im[k * gpb:(k + 1) * gpb])], axis=0)
                    for k in range(nblk)]).astype(BF16)
    fixed2 = lambda b, i: (0, 0)
    fixed3 = lambda b, i: (0, 0, 0)
    return pl.pallas_call(
        _ssm_body,
        grid=(bs, s // SSM_TC),
        in_specs=[pl.BlockSpec((1, SSM_TC, SSM_WIDTH), lambda b, i: (b, i, 0)),
                  pl.BlockSpec(bw.shape, fixed3), pl.BlockSpec(cw.shape, fixed3),
                  pl.BlockSpec((1, SSM_WIDTH), fixed2), pl.BlockSpec((SSM_WIDTH, SSM_WIDTH), fixed2),
                  pl.BlockSpec((8, SSM_NSTATE), fixed2), pl.BlockSpec((8, SSM_NSTATE), fixed2),
                  pl.BlockSpec((SSM_SUB, SSM_NSTATE), fixed2), pl.BlockSpec((SSM_SUB, SSM_NSTATE), fixed2)],
        out_specs=pl.BlockSpec((1, SSM_TC, SSM_WIDTH), lambda b, i: (b, i, 0)),
        out_shape=jax.ShapeDtypeStruct((bs, s, SSM_WIDTH), F32),
        scratch_shapes=[pltpu.VMEM((SSM_TC, SSM_NSTATE), F32), pltpu.VMEM((SSM_TC, SSM_NSTATE), F32),
                        pltpu.VMEM((1, SSM_NSTATE), F32), pltpu.VMEM((1, SSM_NSTATE), F32)],
        compiler_params=_params("parallel", "arbitrary"),
        name="ssm_scan",
    )(u, bw, cw, d_skip.reshape(1, SSM_WIDTH), glu_w.astype(BF16), ar, ai, pr, pi)


def _ssm_mixer_jax(u, a_re, a_im, log_dt, b_re, b_im, c_re, c_im, d_skip, glu_w):
    bs, s, _ = u.shape
    uf = u.reshape(bs, s, SSM_GROUPS, SSM_GROUP)
    dt = jnp.exp(log_dt)[:, None]
    mag = jnp.exp(a_re * dt)
    abar_r, abar_i = mag * jnp.cos(a_im * dt), mag * jnp.sin(a_im * dt)
    nr, ni = abar_r - 1.0, abar_i
    den = a_re * a_re + a_im * a_im
    sr, si = (nr * a_re + ni * a_im) / den, (ni * a_re - nr * a_im) / den
    bbar_r = sr[..., None] * b_re - si[..., None] * b_im
    bbar_i = sr[..., None] * b_im + si[..., None] * b_re
    bu_r = jnp.einsum('bsgp,gnp->bsgn', uf, bbar_r)
    bu_i = jnp.einsum('bsgp,gnp->bsgn', uf, bbar_i)
    ar = jnp.broadcast_to(abar_r, bu_r.shape)
    ai = jnp.broadcast_to(abar_i, bu_r.shape)

    def combine(e1, e2):
        a1r, a1i, b1r, b1i = e1
        a2r, a2i, b2r, b2i = e2
        return (a2r * a1r - a2i * a1i, a2r * a1i + a2i * a1r,
                a2r * b1r - a2i * b1i + b2r, a2r * b1i + a2i * b1r + b2i)

    _, _, st_r, st_i = lax.associative_scan(combine, (ar, ai, bu_r, bu_i), axis=1)
    y = (jnp.einsum('gpn,bsgn->bsgp', c_re, st_r) - jnp.einsum('gpn,bsgn->bsgp', c_im, st_i)
         + d_skip.reshape(SSM_GROUPS, SSM_GROUP) * uf)
    y = jax.nn.gelu(y.reshape(bs, s, SSM_WIDTH))
    return y * jax.nn.sigmoid(y @ glu_w)


GDN_TC = 256
GDN_HALO = 8
MASKED = -1e30


def _dot_nt(a, b):
    return lax.dot_general(a, b, (((1,), (1,)), ((), ())), preferred_element_type=F32)


def _dot_tn(a, b):
    return lax.dot_general(a, b, (((0,), (0,)), ((), ())), preferred_element_type=F32)


def _split_bf16(x):
    hi = x.astype(BF16)
    return hi, (x - hi.astype(F32)).astype(BF16)


def _gdn_body(nega_ref, dtb_ref, q_ref, k_ref, v_ref, z_ref, sm_ref, wq_ref, wk_ref, wv_ref, ng_ref,
              o_ref, state_ref, hq_ref, hk_ref, hv_ref, vnew_ref):
    TC, CH, DK = GDN_TC, GDN_CHUNK, GDN_DK
    h = pl.program_id(1)

    @pl.when(pl.program_id(2) == 0)
    def _():
        state_ref[...] = jnp.zeros_like(state_ref)
        hq_ref[...] = jnp.zeros_like(hq_ref)
        hk_ref[...] = jnp.zeros_like(hk_ref)
        hv_ref[...] = jnp.zeros_like(hv_ref)

    def conv_silu(x_ref, halo_ref, w_ref):
        x = x_ref[...]
        xx = jnp.concatenate([halo_ref[...], x], axis=0)
        w = w_ref[...]
        y = w[GDN_CONV - 1:GDN_CONV] * x
        for d in range(1, GDN_CONV):
            y = y + w[GDN_CONV - 1 - d:GDN_CONV - d] * pltpu.roll(xx, d, 0)[GDN_HALO:GDN_HALO + TC]
        halo_ref[...] = x[TC - GDN_HALO:TC]
        return y * jax.nn.sigmoid(y)

    q = conv_silu(q_ref, hq_ref, wq_ref)
    k = conv_silu(k_ref, hk_ref, wk_ref)
    v = conv_silu(v_ref, hv_ref, wv_ref)
    q = q * lax.rsqrt(jnp.sum(q * q, axis=-1, keepdims=True) + 1e-6) * (DK ** -0.5)
    k = k * lax.rsqrt(jnp.sum(k * k, axis=-1, keepdims=True) + 1e-6)

    small = sm_ref[...]
    lane_s = lax.broadcasted_iota(jnp.int32, small.shape, 1)
    a_col = jnp.sum(jnp.where(lane_s == h, small, 0.0), axis=-1, keepdims=True)
    b_col = jnp.sum(jnp.where(lane_s == GDN_HEADS + h, small, 0.0), axis=-1, keepdims=True)
    beta = jax.nn.sigmoid(b_col)
    xa = a_col + dtb_ref[h]
    softplus = jnp.maximum(xa, 0.0) + jnp.log(1.0 + jnp.exp(-jnp.abs(xa)))
    la = jnp.broadcast_to(nega_ref[h] * softplus, (TC, LANES))

    ri = lax.broadcasted_iota(jnp.int32, (TC, TC), 0)
    ci = lax.broadcasted_iota(jnp.int32, (TC, TC), 1)
    same = (ri // CH) == (ci // CH)
    causal = jnp.where(same, jnp.where(ci <= ri, 1.0, 0.0), 0.0)
    strict = jnp.where(ci < ri, causal, 0.0)
    tri = causal.astype(BF16)
    la_hi, la_lo = _split_bf16(la)
    g = (jnp.dot(tri, la_hi, preferred_element_type=F32)
         + jnp.dot(tri, la_lo, preferred_element_type=F32))
    g_hi = g.astype(BF16).astype(F32)
    g_lo = g - g_hi
    lane = lax.broadcasted_iota(jnp.int32, (TC, LANES), 1)
    lhs = jnp.where(lane == 0, g_hi, jnp.where(lane == 1, g_lo, jnp.where(lane < 4, 1.0, 0.0))).astype(BF16)
    rhs = jnp.where(lane < 2, 1.0, jnp.where(lane == 2, -g_hi, jnp.where(lane == 3, -g_lo, 0.0))).astype(BF16)
    decay = jnp.exp(jnp.where(causal > 0.5, _dot_nt(lhs, rhs), MASKED))

    kb = k * beta
    k_b = k.astype(BF16)
    lmat = strict * _dot_nt(kb.astype(BF16), k_b) * decay
    eye = jnp.where(ri == ci, 1.0, 0.0)
    tinv = eye - lmat
    pw = lmat
    for _ in range(CH.bit_length() - 2):
        pb = pw.astype(BF16)
        pw = jnp.dot(pb, pb, preferred_element_type=F32)
        tinv = tinv + jnp.dot(tinv.astype(BF16), pw.astype(BF16), preferred_element_type=F32)
    eg = jnp.exp(g)
    rhs_all = jnp.concatenate([v * beta, kb * eg], axis=1).astype(BF16)
    sol = jnp.dot(tinv.astype(BF16), rhs_all, preferred_element_type=F32)
    u_val, w_val = sol[:, :GDN_DV], sol[:, GDN_DV:]
    attn = (causal * _dot_nt(q.astype(BF16), k_b) * decay).astype(BF16)
    q_st = (q * eg).astype(BF16)

    vnew_ref[...] = jnp.zeros_like(vnew_ref)
    for c in range(TC // CH):
        rows = slice(c * CH, (c + 1) * CH)
        g_last = g[(c + 1) * CH - 1:(c + 1) * CH, :]
        st = state_ref[...]
        st_b = st.astype(BF16)
        v_new = u_val[rows] - jnp.dot(w_val[rows].astype(BF16), st_b, preferred_element_type=F32)
        vnew_ref[rows, :] = v_new.astype(BF16)
        o_c = (jnp.dot(q_st[rows], st_b, preferred_element_type=F32)
               + jnp.dot(attn[rows], vnew_ref[...], preferred_element_type=F32))
        k_st = (k[rows] * jnp.exp(g_last - g[rows])).astype(BF16)
        state_ref[...] = st * jnp.exp(g_last[:, :1]) + _dot_tn(k_st, v_new.astype(BF16))
        o_c = o_c * lax.rsqrt(jnp.mean(o_c * o_c, axis=-1, keepdims=True) + RMS_EPS) * ng_ref[...]
        zc = z_ref[rows, :]
        o_ref[rows, :] = o_c * (zc * jax.nn.sigmoid(zc))


def gdn_mixer(proj, bs, conv_w, a_log, dt_bias, norm_g):
    t = proj.shape[0]
    s = t // bs
    nt = s // GDN_TC
    assert s % GDN_TC == 0 and GDN_DK == LANES and GDN_DV == LANES and OFF_QKV == 0
    H = GDN_HEADS
    cw = jnp.zeros((8, GDN_QKV), F32).at[:GDN_CONV].set(conv_w)
    tok = lambda off: pl.BlockSpec((GDN_TC, LANES), lambda b, h, i, *_: (b * nt + i, off + h))
    wspec = lambda off: pl.BlockSpec((8, LANES), lambda b, h, i, *_: (0, off + h))
    grid_spec = pltpu.PrefetchScalarGridSpec(
        num_scalar_prefetch=0,
        grid=(bs, H, nt),
        in_specs=[pl.BlockSpec(memory_space=pltpu.SMEM), pl.BlockSpec(memory_space=pltpu.SMEM),
                  tok(0), tok(H), tok(2 * H), tok(OFF_Z // LANES),
                  pl.BlockSpec((GDN_TC, 2 * LANES), lambda b, h, i, *_: (b * nt + i, OFF_SMALL // (2 * LANES))),
                  wspec(0), wspec(H), wspec(2 * H),
                  pl.BlockSpec((1, GDN_DV), lambda b, h, i, *_: (0, 0))],
        out_specs=pl.BlockSpec((GDN_TC, GDN_DV), lambda b, h, i, *_: (b * nt + i, h)),
        scratch_shapes=[pltpu.VMEM((GDN_DK, GDN_DV), F32)] + [pltpu.VMEM((GDN_HALO, LANES), F32)] * 3
        + [pltpu.VMEM((GDN_TC, GDN_DV), BF16)],
    )
    return pl.pallas_call(
        _gdn_body,
        grid_spec=grid_spec,
        out_shape=jax.ShapeDtypeStruct((t, H * GDN_DV), F32),
        compiler_params=_params("parallel", "parallel", "arbitrary"),
        name="gdn_delta",
    )(-jnp.exp(a_log), dt_bias.astype(F32), proj, proj, proj, proj, proj, cw, cw, cw,
      norm_g.reshape(1, GDN_DV))


def _gdn_mixer_jax(qkv, a, b, z, conv_w, a_log, dt_bias, norm_g):
    bs, s, c = qkv.shape
    H, DK, DV, CH = GDN_HEADS, GDN_DK, GDN_DV, GDN_CHUNK
    qkv = lax.conv_general_dilated(qkv, conv_w[:, None, :], window_strides=(1,),
                                   padding=[(GDN_CONV - 1, 0)],
                                   dimension_numbers=('NWC', 'WIO', 'NWC'), feature_group_count=c)
    qkv = jax.nn.silu(qkv)
    q, k, v = jnp.split(qkv, [H * DK, 2 * H * DK], axis=-1)
    q = _l2_norm(q.reshape(bs, s, H, DK)) * (DK ** -0.5)
    k = _l2_norm(k.reshape(bs, s, H, DK))
    v = v.reshape(bs, s, H, DV)
    beta = jax.nn.sigmoid(b)
    log_alpha = -jnp.exp(a_log) * jax.nn.softplus(a + dt_bias)
    nc = s // CH

    def chunks(t):
        return t.reshape(bs, nc, CH, H, -1).transpose(0, 3, 1, 2, 4)

    q, k, v = chunks(q), chunks(k), chunks(v)
    beta = chunks(beta[..., None])
    g = jnp.cumsum(chunks(log_alpha[..., None]), axis=3)
    gv = g[..., 0]
    causal = jnp.tril(jnp.ones((CH, CH), bool))
    strict = jnp.tril(jnp.ones((CH, CH), bool), -1)
    decay = jnp.exp(jnp.where(causal, gv[..., :, None] - gv[..., None, :], -jnp.inf))
    kb = k * beta
    a_mat = (jnp.where(strict, jnp.einsum('bhncd,bhnkd->bhnck', kb, k) * decay, 0.0)
             + jnp.eye(CH, dtype=F32))
    rhs = jnp.concatenate([v * beta, kb * jnp.exp(g)], axis=-1)
    sol = lax.linalg.triangular_solve(a_mat, rhs, left_side=True, lower=True, unit_diagonal=True)
    u_val, w = sol[..., :DV], sol[..., DV:]
    attn = jnp.einsum('bhncd,bhnkd->bhnck', q, k) * decay
    g_last = g[..., -1:, :]
    k_st = k * jnp.exp(g_last - g)
    q_st = q * jnp.exp(g)

    def step(state, xs):
        q_c, k_c, u_c, w_c, a_c, gl_c = xs
        v_new = u_c - jnp.einsum('bhcd,bhde->bhce', w_c, state)
        o = jnp.einsum('bhcd,bhde->bhce', q_c, state) + jnp.einsum('bhck,bhke->bhce', a_c, v_new)
        state = state * jnp.exp(gl_c) + jnp.einsum('bhcd,bhce->bhde', k_c, v_new)
        return state, o

    xs = tuple(jnp.moveaxis(t, 2, 0) for t in (q_st, k_st, u_val, w, attn, g_last))
    _, o = lax.scan(step, jnp.zeros((bs, H, DK, DV), F32), xs)
    o = o.transpose(1, 0, 3, 2, 4).reshape(bs, s, H, DV)
    o = o * lax.rsqrt(jnp.mean(o * o, axis=-1, keepdims=True) + RMS_EPS) * norm_g
    o = o * jax.nn.silu(z.reshape(bs, s, H, DV))
    return o.reshape(bs, s, H * DV)


NSA_KT = 256
NSA_NEG = MASKED
NSA_WTILES = NSA_WINDOW // LANES + 1


def _bucket_table():
    n = np.arange(LANES)
    max_exact = REL_BUCKETS // 2
    nf = np.maximum(n, 1).astype(np.float32)
    large = max_exact + (np.log(nf / np.float32(max_exact)) / np.float32(math.log(REL_MAX_DIST / max_exact))
                         * np.float32(REL_BUCKETS - max_exact)).astype(np.int32)
    tbl = np.where(n < max_exact, n, np.minimum(large, REL_BUCKETS - 1))
    assert tbl[-1] == REL_BUCKETS - 1 and REL_MAX_DIST <= LANES
    return tbl


def _compress_body(x_ref, pe_ref, w1_ref, w2_ref, o_ref):
    x = x_ref[0, 0, 0]
    w1 = w1_ref[0]
    half = NSA_CMP_STRIDE * NSA_DH
    nc = x.shape[0]
    a = jnp.dot(x, w1[:half], preferred_element_type=F32)
    b = jnp.dot(x, w1[half:], preferred_element_type=F32)
    pe_h = jnp.dot(pe_ref[0], w1, preferred_element_type=F32)
    hid = a + pltpu.roll(b, nc - 1, 0) + pe_h[0:1, :]
    o_ref[0, 0, 0] = jnp.dot(jax.nn.gelu(hid).astype(BF16), w2_ref[0], preferred_element_type=F32)


def nsa_compress(x, pe, w1, w2):
    _, bs, g, nc, width = x.shape
    return pl.pallas_call(
        _compress_body,
        grid=(2, bs, g),
        in_specs=[pl.BlockSpec((1, 1, 1, nc, width), lambda i, b, j: (i, b, j, 0, 0)),
                  pl.BlockSpec((1,) + pe.shape[1:], lambda i, b, j: (i, 0, 0)),
                  pl.BlockSpec((1,) + w1.shape[1:], lambda i, b, j: (i, 0, 0)),
                  pl.BlockSpec((1,) + w2.shape[1:], lambda i, b, j: (i, 0, 0))],
        out_specs=pl.BlockSpec((1, 1, 1, nc, NSA_DH), lambda i, b, j: (i, b, j, 0, 0)),
        out_shape=jax.ShapeDtypeStruct((2, bs, g, nc, NSA_DH), F32),
        compiler_params=_params("parallel", "parallel", "parallel"),
        name="nsa_compress",
    )(x, pe, w1, w2)


def _nsa_body(near_s, near_w, near_c, q_ref, gl_ref, posq_ref, posk_ref, cpos_ref,
              ks_ref, vs_ref, kw_ref, vw_ref, kc_ref, vc_ref, agg_ref, td_ref, o_ref,
              s_ref, m_ref, l_ref, acc_ref, *, n_sel, sel_k):
    QB, HG = NSA_QBLOCK, NSA_HPG
    g = pl.program_id(1)
    qi = pl.program_id(2)
    s0 = qi * QB
    nc = kc_ref.shape[2]
    n_ct = nc // LANES
    n_kt = ks_ref.shape[2] // NSA_KT
    q = q_ref[0, 0, 0]
    tp = posq_ref[...]
    row = lax.broadcasted_iota(jnp.int32, (QB, LANES), 0)
    lane = lax.broadcasted_iota(jnp.int32, (QB, LANES), 1)
    t_tok = s0 + row

    def add_delta(col0, pk_row):
        idx = jnp.clip(tp - pk_row, 0, LANES - 1)
        for hg in range(HG):
            tb = jnp.broadcast_to(td_ref[pl.ds(g * HG + hg, 1), :], (QB, LANES))
            s_ref[hg * QB:(hg + 1) * QB, col0:col0 + LANES] += jnp.take_along_axis(tb, idx, axis=1)

    def heads(x):
        return jnp.concatenate([x] * HG, axis=0)

    gate = jax.nn.sigmoid(gl_ref[0, 0, 0])

    s_ref[:, :nc] = _dot_nt(q, kc_ref[0, 0])
    for ct in range(n_ct):
        @pl.when(near_c[qi * n_ct + ct] != 0)
        def _():
            add_delta(ct * LANES, cpos_ref[ct:ct + 1, :])
    c_id = lax.broadcasted_iota(jnp.int32, (QB, nc), 1)
    c_row = lax.broadcasted_iota(jnp.int32, (QB, nc), 0)
    ok_c = heads(jnp.where(c_id * NSA_CMP_STRIDE + (NSA_CMP_LEN - 1) <= s0 + c_row, 1.0, 0.0))
    sc = jnp.where(ok_c > 0.5, s_ref[:, :nc], NSA_NEG)
    e = jnp.exp(sc - jnp.max(sc, axis=-1, keepdims=True)) * ok_c
    p = e * (1.0 / jnp.maximum(jnp.sum(e, axis=-1, keepdims=True), 1e-30))
    o_c = jnp.dot(p.astype(BF16), vc_ref[0, 0], preferred_element_type=F32)
    o_ref[0, 0, 0] = gate[:, 0:1] * o_c
    ps = p[0:QB]
    for hg in range(1, HG):
        ps = ps + p[hg * QB:(hg + 1) * QB]
    ps_hi = ps.astype(BF16)
    ps_lo = (ps - ps_hi.astype(F32)).astype(BF16)
    imp = (jnp.dot(ps_hi, agg_ref[...], preferred_element_type=F32)
           + jnp.dot(ps_lo, agg_ref[...], preferred_element_type=F32))
    tb_blk = t_tok // NSA_SEL_LEN
    forced = jnp.where(lane == 0, 1.0, 0.0) + jnp.where(lane == tb_blk, 1.0, 0.0) \
        + jnp.where(lane == tb_blk - 1, 1.0, 0.0)
    score = jnp.where(forced > 0.5, 1e9, jnp.where(lane * NSA_SEL_LEN <= t_tok, imp, -1e9))
    score = jnp.where(lane < n_sel, score, -jnp.inf)
    lanef = lane.astype(F32)
    sel = jnp.zeros((QB, LANES), F32)
    for _ in range(sel_k):
        mx = jnp.max(score, axis=-1, keepdims=True)
        first = jnp.min(jnp.where(score == mx, lanef, float(LANES)), axis=-1, keepdims=True)
        hit = lanef == first
        sel = jnp.where(hit, 1.0, sel)
        score = jnp.where(hit, -jnp.inf, score)
    sel_b = sel.astype(BF16)

    def reset():
        m_ref[...] = jnp.full_like(m_ref, NSA_NEG)
        l_ref[...] = jnp.zeros_like(l_ref)
        acc_ref[...] = jnp.zeros_like(acc_ref)

    def tile_step(width, mask_add, v):
        s = s_ref[:, :width]
        if mask_add is not None:
            s = s + heads(mask_add)
        m_old = m_ref[...]
        m_new = jnp.maximum(m_old, jnp.max(s, axis=-1, keepdims=True))
        alpha = jnp.exp(m_old - m_new)
        p = jnp.exp(s - m_new)
        psum = p[:, :LANES]
        for c in range(1, width // LANES):
            psum = psum + p[:, c * LANES:(c + 1) * LANES]
        l_ref[...] = l_ref[...] * alpha + psum
        acc_ref[...] = acc_ref[...] * alpha + jnp.dot(p.astype(BF16), v, preferred_element_type=F32)
        m_ref[...] = m_new

    def finish():
        return acc_ref[...] * (1.0 / jnp.sum(l_ref[...], axis=-1, keepdims=True))

    blk_i = lax.broadcasted_iota(jnp.int32, (LANES, NSA_KT), 0)
    key_i = lax.broadcasted_iota(jnp.int32, (LANES, NSA_KT), 1)
    key_q = lax.broadcasted_iota(jnp.int32, (QB, NSA_KT), 1)
    row_q = lax.broadcasted_iota(jnp.int32, (QB, NSA_KT), 0)
    per_tile = NSA_KT // NSA_SEL_LEN

    def sel_tile(kt, diag):
        k0 = pl.multiple_of(kt * NSA_KT, NSA_KT)
        s_ref[:, :NSA_KT] = _dot_nt(q, ks_ref[0, 0, pl.ds(k0, NSA_KT), :])

        @pl.when(near_s[qi * n_kt + kt] != 0)
        def _():
            for c in range(NSA_KT // LANES):
                add_delta(c * LANES, posk_ref[pl.ds(kt * (NSA_KT // LANES) + c, 1), :])

        expand = jnp.where(blk_i == kt * per_tile + key_i // NSA_SEL_LEN, 1.0, 0.0).astype(BF16)
        picked = jnp.dot(sel_b, expand, preferred_element_type=F32)
        if diag:
            picked = jnp.where(k0 + key_q <= s0 + row_q, picked, 0.0)
        tile_step(NSA_KT, jnp.where(picked > 0.5, 0.0, NSA_NEG), vs_ref[0, 0, pl.ds(k0, NSA_KT), :])

    reset()
    n_past = (s0 + QB - 1) // NSA_KT

    def past_body(kt, carry):
        sel_tile(kt, False)
        return carry

    lax.fori_loop(0, n_past, past_body, 0)
    sel_tile(n_past, True)
    o_ref[0, 0, 0] += gate[:, 1:2] * finish()

    reset()
    for j in range(NSA_WTILES):
        start = s0 - NSA_WINDOW + j * LANES

        @pl.when(start >= 0)
        def _():
            st = pl.multiple_of(start, LANES)
            s_ref[:, :LANES] = _dot_nt(q, kw_ref[0, 0, pl.ds(st, LANES), :])

            @pl.when(near_w[qi * NSA_WTILES + j] != 0)
            def _():
                add_delta(0, posk_ref[pl.ds(qi - (NSA_WTILES - 1) + j, 1), :])

            if j == 0:
                mask_add = jnp.where(lane > row, 0.0, NSA_NEG)
            elif j == NSA_WTILES - 1:
                mask_add = jnp.where(lane <= row, 0.0, NSA_NEG)
            else:
                mask_add = None
            tile_step(LANES, mask_add, vw_ref[0, 0, pl.ds(st, LANES), :])

    o_ref[0, 0, 0] += gate[:, 2:3] * finish()


def nsa_mixer(q, k_c, v_c, k_s, v_s, k_w, v_w, gate_logits, positions,
              ck_pe, ck_w1, ck_w2, cv_pe, cv_w1, cv_w2, rel_bias):
    bs, s, _ = q.shape
    G, HG, DH, QB = NSA_KV_GROUPS, NSA_HPG, NSA_DH, NSA_QBLOCK
    nqt = s // QB
    nc = s // NSA_CMP_STRIDE
    n_cmp = (s - NSA_CMP_LEN) // NSA_CMP_STRIDE + 1
    n_sel = s // NSA_SEL_LEN
    sel_k = min(NSA_SEL_TOPK, n_sel)
    n_kt = s // NSA_KT
    n_ct = nc // LANES
    assert s % (LANES * NSA_CMP_STRIDE) == 0 and n_sel <= LANES and NSA_KT == 2 * QB

    def by_group(t):
        return t.reshape(bs, s, G, DH).transpose(0, 2, 1, 3)

    ones2 = jnp.zeros((LANES - DH,), F32).at[:2].set(1.0)

    def keys_aug(t):
        return jnp.concatenate([t, jnp.broadcast_to(ones2, t.shape[:-1] + (LANES - DH,))],
                               axis=-1).astype(BF16)

    chunks = jnp.stack([by_group(k_c), by_group(v_c)]).reshape(2, bs, G, nc, NSA_CMP_STRIDE * DH)
    pe = jnp.stack([ck_pe, cv_pe]).reshape(2, 1, NSA_CMP_LEN * DH)
    cmp = nsa_compress(chunks.astype(BF16), jnp.broadcast_to(pe, (2, 8, NSA_CMP_LEN * DH)).astype(BF16),
                       jnp.stack([ck_w1, cv_w1]).astype(BF16), jnp.stack([ck_w2, cv_w2]).astype(BF16))
    kc, vc = keys_aug(cmp[0]), cmp[1].astype(BF16)

    far = rel_bias[REL_BUCKETS - 1].astype(F32)
    far_hi = far.astype(BF16).astype(F32)
    extra = jnp.zeros((NSA_HEADS, LANES - DH), F32).at[:, 0].set(far_hi).at[:, 1].set(far - far_hi)
    qh = (q * (DH ** -0.5)).reshape(bs, nqt, QB, G, HG, DH).transpose(0, 3, 1, 4, 2, 5)
    ex = jnp.broadcast_to(extra.reshape(1, G, 1, HG, 1, LANES - DH), (bs, G, nqt, HG, QB, LANES - DH))
    qs = jnp.concatenate([qh, ex], axis=-1).astype(BF16).reshape(bs, G, nqt, HG * QB, LANES)
    gl = gate_logits.reshape(bs, nqt, QB, G, HG, 3).transpose(0, 3, 1, 4, 2, 5)
    gl = jnp.pad(gl, ((0, 0),) * 5 + ((0, 5),)).reshape(bs, G, nqt, HG * QB, 8)

    td = (rel_bias[_bucket_table()] - rel_bias[REL_BUCKETS - 1][None, :]).T.astype(F32)
    cmp_end = jnp.minimum(jnp.arange(nc) * NSA_CMP_STRIDE + NSA_CMP_LEN - 1, s - 1)
    cpos = positions[cmp_end]
    qmin = positions.reshape(nqt, QB).min(axis=1)
    kmax = positions.reshape(nqt, QB).max(axis=1)
    near_s = (qmin[:, None] - positions.reshape(n_kt, NSA_KT).max(axis=1)[None, :]) < REL_MAX_DIST
    wt = jnp.clip(jnp.arange(nqt)[:, None] - (NSA_WTILES - 1) + jnp.arange(NSA_WTILES)[None, :], 0, nqt - 1)
    near_w = (qmin[:, None] - kmax[wt]) < REL_MAX_DIST
    near_c = (qmin[:, None] - cpos.reshape(n_ct, LANES).max(axis=1)[None, :]) < REL_MAX_DIST

    cmp_start = np.arange(nc) * NSA_CMP_STRIDE
    sel_start = np.arange(LANES) * NSA_SEL_LEN
    agg = ((cmp_start[:, None] <= sel_start[None, :] + NSA_SEL_LEN - 1)
           & (cmp_start[:, None] + NSA_CMP_LEN - 1 >= sel_start[None, :])
           & (np.arange(nc)[:, None] < n_cmp) & (np.arange(LANES)[None, :] < n_sel))

    rows = HG * QB
    full = lambda shape: pl.BlockSpec(shape, lambda b, g, i, *_: (0,) * len(shape))
    per_bg = lambda n, w: pl.BlockSpec((1, 1, n, w), lambda b, g, i, *_: (b, g, 0, 0))
    grid_spec = pltpu.PrefetchScalarGridSpec(
        num_scalar_prefetch=3,
        grid=(bs, G, nqt),
        in_specs=[pl.BlockSpec((1, 1, 1, rows, LANES), lambda b, g, i, *_: (b, g, i, 0, 0)),
                  pl.BlockSpec((1, 1, 1, rows, 8), lambda b, g, i, *_: (b, g, i, 0, 0)),
                  pl.BlockSpec((QB, 1), lambda b, g, i, *_: (i, 0)),
                  full((nqt, LANES)), full((n_ct, LANES)),
                  per_bg(s, LANES), per_bg(s, DH), per_bg(s, LANES), per_bg(s, DH),
                  per_bg(nc, LANES), per_bg(nc, DH),
                  full((nc, LANES)), full((NSA_HEADS, LANES))],
        out_specs=pl.BlockSpec((1, 1, 1, rows, DH), lambda b, g, i, *_: (b, g, i, 0, 0)),
        scratch_shapes=[pltpu.VMEM((rows, max(nc, NSA_KT)), F32), pltpu.VMEM((rows, 1), F32),
                        pltpu.VMEM((rows, LANES), F32), pltpu.VMEM((rows, DH), F32)],
    )
    out = pl.pallas_call(
        functools.partial(_nsa_body, n_sel=n_sel, sel_k=sel_k),
        grid_spec=grid_spec,
        out_shape=jax.ShapeDtypeStruct((bs, G, nqt, rows, DH), F32),
        compiler_params=_params("parallel", "parallel", "arbitrary"),
        name="nsa_attention",
    )(near_s.reshape(-1).astype(jnp.int32), near_w.reshape(-1).astype(jnp.int32),
      near_c.reshape(-1).astype(jnp.int32),
      qs, gl, positions.reshape(s, 1), positions.reshape(nqt, LANES), cpos.reshape(n_ct, LANES),
      keys_aug(by_group(k_s)), by_group(v_s).astype(BF16), keys_aug(by_group(k_w)),
      by_group(v_w).astype(BF16), kc, vc, jnp.asarray(agg, BF16), td)
    out = out.reshape(bs, G, nqt, HG, QB, DH).transpose(0, 2, 4, 1, 3, 5)
    return out.reshape(bs, s, G * HG * DH)


def _nsa_mixer_jax(q, k_c, v_c, k_s, v_s, k_w, v_w, gate_logits, positions,
                   ck_pe, ck_w1, ck_w2, cv_pe, cv_w1, cv_w2, rel_bias):
    bs, s, _ = q.shape
    G, HG, DH, QB, W = NSA_KV_GROUPS, NSA_HPG, NSA_DH, NSA_QBLOCK, NSA_WINDOW
    q = q.reshape(bs, s, G, HG, DH) * (DH ** -0.5)

    def kv(t):
        return t.reshape(bs, s, G, DH)

    k_c, v_c, k_s, v_s, k_w, v_w = (kv(t) for t in (k_c, v_c, k_s, v_s, k_w, v_w))
    gates = jax.nn.sigmoid(gate_logits).reshape(bs, s, G, HG, 3)
    n_cmp = (s - NSA_CMP_LEN) // NSA_CMP_STRIDE + 1
    cmp_start = jnp.arange(n_cmp) * NSA_CMP_STRIDE
    cmp_end = cmp_start + NSA_CMP_LEN - 1
    cmp_tok = cmp_start[:, None] + jnp.arange(NSA_CMP_LEN)[None, :]

    def compress(t, pe, w1, w2):
        blk = t[:, cmp_tok] + pe[None, None, :, None, :]
        blk = blk.transpose(0, 1, 3, 2, 4).reshape(bs, n_cmp, G, NSA_CMP_LEN * DH)
        return jax.nn.gelu(blk @ w1) @ w2

    kc = compress(k_c, ck_pe, ck_w1, ck_w2)
    vc = compress(v_c, cv_pe, cv_w1, cv_w2)
    cmp_pos = positions[cmp_end]
    n_sel = s // NSA_SEL_LEN
    sel_k = min(NSA_SEL_TOPK, n_sel)
    sel_start = jnp.arange(n_sel) * NSA_SEL_LEN
    agg = ((cmp_start[:, None] <= sel_start[None, :] + NSA_SEL_LEN - 1)
           & (cmp_end[:, None] >= sel_start[None, :])).astype(F32)
    ks_blk = k_s.reshape(bs, n_sel, NSA_SEL_LEN, G, DH).transpose(0, 3, 1, 2, 4)
    vs_blk = v_s.reshape(bs, n_sel, NSA_SEL_LEN, G, DH).transpose(0, 3, 1, 2, 4)
    pos_blk = positions.reshape(n_sel, NSA_SEL_LEN)
    kw_pad = jnp.pad(k_w, ((0, 0), (W, 0), (0, 0), (0, 0)))
    vw_pad = jnp.pad(v_w, ((0, 0), (W, 0), (0, 0), (0, 0)))
    pos_pad = jnp.pad(positions, (W, 0))
    tbl = rel_bias.reshape(REL_BUCKETS, G, HG)
    b_ix = jnp.arange(bs)[:, None, None, None]
    g_ix = jnp.arange(G)[None, :, None, None]
    sel_offs = jnp.arange(NSA_SEL_LEN)
    win_offs = jnp.arange(QB + W)
    blk_j = jnp.arange(n_sel)

    def one_block(qi):
        s0 = qi * QB
        qb = lax.dynamic_slice_in_dim(q, s0, QB, axis=1)
        t = s0 + jnp.arange(QB)
        tp = lax.dynamic_slice_in_dim(positions, s0, QB)
        bias_c = tbl[_rel_bucket(tp[:, None] - cmp_pos[None, :])].transpose(2, 3, 0, 1)
        lg_c = jnp.einsum('bqghd,bcgd->bghqc', qb, kc) + bias_c
        p_c = _masked_softmax(lg_c, cmp_end[None, :] <= t[:, None])
        o_c = jnp.einsum('bghqc,bcgd->bqghd', p_c, vc)
        imp = jnp.einsum('bghqc,cj->bgqj', p_c, agg)
        tb = t // NSA_SEL_LEN
        forced = ((blk_j[None, :] == 0) | (blk_j[None, :] == tb[:, None])
                  | (blk_j[None, :] == tb[:, None] - 1))
        eligible = sel_start[None, :] <= t[:, None]
        score = jnp.where(forced, 1e9, jnp.where(eligible, imp, -1e9))
        _, sel = lax.top_k(score, sel_k)
        k_sel = ks_blk[b_ix, g_ix, sel]
        v_sel = vs_blk[b_ix, g_ix, sel]
        tok_s = sel[..., None] * NSA_SEL_LEN + sel_offs
        bias_s = tbl[_rel_bucket(tp[:, None, None] - pos_blk[sel]), g_ix[..., None]]
        lg_s = jnp.einsum('bqghd,bgqnkd->bghqnk', qb, k_sel) + bias_s.transpose(0, 1, 5, 2, 3, 4)
        valid_s = (tok_s <= t[:, None, None])[:, :, None].reshape(bs, G, 1, QB, -1)
        p_s = _masked_softmax(lg_s.reshape(bs, G, HG, QB, -1), valid_s)
        p_s = p_s.reshape(bs, G, HG, QB, sel_k, NSA_SEL_LEN)
        o_s = jnp.einsum('bghqnk,bgqnkd->bqghd', p_s, v_sel)
        kwb = lax.dynamic_slice_in_dim(kw_pad, s0, QB + W, axis=1)
        vwb = lax.dynamic_slice_in_dim(vw_pad, s0, QB + W, axis=1)
        tok_w = s0 - W + win_offs
        pos_w = lax.dynamic_slice_in_dim(pos_pad, s0, QB + W)
        d_tok = t[:, None] - tok_w[None, :]
        valid_w = (d_tok >= 0) & (d_tok < W) & (tok_w[None, :] >= 0)
        bias_w = tbl[_rel_bucket(tp[:, None] - pos_w[None, :])].transpose(2, 3, 0, 1)
        lg_w = jnp.einsum('bqghd,bkgd->bghqk', qb, kwb) + bias_w
        p_w = _masked_softmax(lg_w, valid_w)
        o_w = jnp.einsum('bghqk,bkgd->bqghd', p_w, vwb)
        gb = lax.dynamic_slice_in_dim(gates, s0, QB, axis=1)
        return gb[..., 0:1] * o_c + gb[..., 1:2] * o_s + gb[..., 2:3] * o_w

    out = lax.map(one_block, jnp.arange(s // QB))
    return out.transpose(1, 0, 2, 3, 4, 5).reshape(bs, s, G * HG * DH)


SGU_TC = 512


def _sgu_body(uv_ref, lg_ref, lb_ref, w_ref, b_ref, o_ref):
    uv = jax.nn.gelu(uv_ref[...])
    u, v = uv[:, :SGU_WIDTH], uv[:, SGU_WIDTH:]
    mu = jnp.mean(v, axis=-1, keepdims=True)
    vc = v - mu
    var = jnp.mean(vc * vc, axis=-1, keepdims=True)
    vn = (vc * lax.rsqrt(var + RMS_EPS) * lg_ref[...] + lb_ref[...]).astype(BF16)
    gw = SGU_WIDTH // SGU_GROUPS
    for c in range(SGU_TC // SGU_CHUNK):
        rows = slice(c * SGU_CHUNK, (c + 1) * SGU_CHUNK)
        for g in range(SGU_GROUPS):
            cols = slice(g * gw, (g + 1) * gw)
            mix = jnp.dot(w_ref[g], vn[rows, cols], preferred_element_type=F32) + b_ref[:, g:g + 1]
            o_ref[rows, cols] = u[rows, cols] * mix


def sgu_mixer(proj, ln_g, ln_b, w_s, b_s):
    t = proj.shape[0]
    assert t % SGU_TC == 0 and OFF_SGU % (2 * SGU_WIDTH) == 0
    fixed = lambda i: (0, 0)
    return pl.pallas_call(
        _sgu_body,
        grid=(t // SGU_TC,),
        in_specs=[pl.BlockSpec((SGU_TC, 2 * SGU_WIDTH), lambda i: (i, OFF_SGU // (2 * SGU_WIDTH))),
                  pl.BlockSpec((1, SGU_WIDTH), fixed), pl.BlockSpec((1, SGU_WIDTH), fixed),
                  pl.BlockSpec((SGU_GROUPS, SGU_CHUNK, SGU_CHUNK), lambda i: (0, 0, 0)),
                  pl.BlockSpec((SGU_CHUNK, SGU_GROUPS), fixed)],
        out_specs=pl.BlockSpec((SGU_TC, SGU_WIDTH), lambda i: (i, 0)),
        out_shape=jax.ShapeDtypeStruct((t, SGU_WIDTH), F32),
        compiler_params=_params("parallel"),
        name="sgu_gate",
    )(proj, ln_g.reshape(1, SGU_WIDTH), ln_b.reshape(1, SGU_WIDTH), jnp.tril(w_s).astype(BF16), b_s.T)


def _sgu_mixer_jax(uv, ln_g, ln_b, w_s, b_s):
    bs, s, _ = uv.shape
    uv = jax.nn.gelu(uv)
    u, v = uv[..., :SGU_WIDTH], uv[..., SGU_WIDTH:]
    mu = jnp.mean(v, axis=-1, keepdims=True)
    var = jnp.mean(jnp.square(v - mu), axis=-1, keepdims=True)
    v = (v - mu) * lax.rsqrt(var + RMS_EPS) * ln_g + ln_b
    nc = s // SGU_CHUNK
    v = v.reshape(bs, nc, SGU_CHUNK, SGU_GROUPS, SGU_WIDTH // SGU_GROUPS)
    mix = (jnp.einsum('gij,bcjgd->bcigd', jnp.tril(w_s), v) + b_s.T[None, None, :, :, None])
    return u * mix.reshape(bs, s, SGU_WIDTH)


def _permute_w_in(w):
    sizes = ([SSM_WIDTH, GDN_QKV, GDN_HEADS, GDN_HEADS, GDN_HEADS * GDN_DV, NSA_Q]
             + [NSA_KV] * 6 + [3 * NSA_HEADS, 2 * SGU_WIDTH, N_BRANCH * D_MODEL])
    cuts = [int(c) for c in np.cumsum(sizes)[:-1]]
    (w_ssm, w_qkv, w_a, w_b, w_z, w_nq, w_kc, w_vc, w_ks, w_vs, w_kw, w_vw,
     w_ng, w_sgu, w_gate) = jnp.split(w, cuts, axis=-1)
    pad = jnp.zeros((w.shape[0], PROJ_COLS - OFF_SMALL - SMALL_USED), w.dtype)
    return jnp.concatenate([w_qkv, w_ssm, w_z, w_nq, w_sgu, w_gate, w_kc, w_vc, w_ks, w_vs,
                            w_kw, w_vw, w_a, w_b, w_ng, pad], axis=-1).astype(BF16)


def kernel(x, positions, norm_mix, norm_ffn, norm_final, w_in, ssm_a_re, ssm_a_im, ssm_log_dt, ssm_b_re, ssm_b_im, ssm_c_re, ssm_c_im, ssm_d, ssm_glu_w, gdn_conv_w, gdn_a_log, gdn_dt_bias, gdn_norm, nsa_cmp_k_pe, nsa_cmp_k_w1, nsa_cmp_k_w2, nsa_cmp_v_pe, nsa_cmp_v_w1, nsa_cmp_v_w2, rel_bias, sgu_ln_g, sgu_ln_b, sgu_w, sgu_b, w_br_ssm, w_br_gdn, w_br_nsa, w_br_sgu, w_out, ffn_w_gate, ffn_w_up, ffn_w_down, moe_router, moe_w_gate, moe_w_up, moe_w_down):
    bs, s, d = x.shape
    t = bs * s
    xf = x.reshape(t, d)
    for layer in range(DEPTH):
        proj = in_proj(xf, norm_mix[layer], _permute_w_in(w_in[layer]))
        p3 = proj.reshape(bs, s, PROJ_COLS)

        def seg(off, width):
            return p3[..., off:off + width]

        y_ssm = ssm_mixer(seg(OFF_SSM, SSM_WIDTH), ssm_a_re[layer], ssm_a_im[layer],
                          ssm_log_dt[layer], ssm_b_re[layer], ssm_b_im[layer], ssm_c_re[layer],
                          ssm_c_im[layer], ssm_d[layer], ssm_glu_w[layer])
        y_gdn = gdn_mixer(proj, bs, gdn_conv_w[layer], gdn_a_log[layer], gdn_dt_bias[layer],
                          gdn_norm[layer])
        kvs = [seg(OFF_KV + i * NSA_KV, NSA_KV) for i in range(6)]
        y_nsa = nsa_mixer(seg(OFF_NQ, NSA_Q), *kvs, seg(OFF_SMALL + 2 * GDN_HEADS, 3 * NSA_HEADS),
                          positions, nsa_cmp_k_pe[layer], nsa_cmp_k_w1[layer], nsa_cmp_k_w2[layer],
                          nsa_cmp_v_pe[layer], nsa_cmp_v_w1[layer], nsa_cmp_v_w2[layer], rel_bias)
        y_sgu = sgu_mixer(proj, sgu_ln_g[layer], sgu_ln_b[layer], sgu_w[layer], sgu_b[layer])
        ys = [y_ssm.reshape(t, -1), y_gdn, y_nsa.reshape(t, -1), y_sgu]
        ws = [w[layer].astype(BF16) for w in (w_br_ssm, w_br_gdn, w_br_nsa, w_br_sgu)]
        xf = merge(xf, proj, ys, ws, w_out[layer].astype(BF16))
        i = layer // 2
        if layer % 2 == 0:
            xf = dense_ffn(xf, norm_ffn[layer], ffn_w_gate[i].astype(BF16),
                           ffn_w_up[i].astype(BF16), ffn_w_down[i].astype(BF16))
        else:
            xf = moe_layer(xf, norm_ffn[layer], moe_router[i], moe_w_gate[i].astype(BF16),
                           moe_w_up[i].astype(BF16), moe_w_down[i].astype(BF16), norm_final)
    return xf.reshape(bs, s, d)
```

```python
import functools
import math

import jax
import jax.numpy as jnp
from jax import lax
import numpy as np
from jax.experimental import pallas as pl
from jax.experimental.pallas import tpu as pltpu

F32 = jnp.float32
BF16 = jnp.bfloat16

D_MODEL = 1024
DEPTH = 2
SSM_WIDTH = D_MODEL // 2
SSM_GROUP = 16
SSM_GROUPS = SSM_WIDTH // SSM_GROUP
SSM_STATE = 64
GDN_HEADS = 4
GDN_DK = 128
GDN_DV = 128
GDN_CONV = 4
GDN_CHUNK = 64
NSA_HEADS = 8
NSA_KV_GROUPS = 2
NSA_HPG = NSA_HEADS // NSA_KV_GROUPS
NSA_DH = 64
NSA_CMP_LEN = 32
NSA_CMP_STRIDE = 16
NSA_CMP_HIDDEN = 128
NSA_SEL_LEN = 64
NSA_SEL_TOPK = 16
NSA_WINDOW = 512
NSA_QBLOCK = 128
SGU_WIDTH = D_MODEL // 2
SGU_GROUPS = 4
SGU_CHUNK = 128
REL_BUCKETS = 32
REL_MAX_DIST = 128
FFN_DENSE = 2816
N_EXPERTS = 8
TOP_K = 2
FFN_EXPERT = 3584
N_BRANCH = 4
RMS_EPS = 1e-6
GDN_QKV = GDN_HEADS * (2 * GDN_DK + GDN_DV)
NSA_Q = NSA_HEADS * NSA_DH
NSA_KV = NSA_KV_GROUPS * NSA_DH

LANES = 128
VMEM_LIMIT = 48 * 1024 * 1024

OFF_QKV = 0
OFF_SSM = OFF_QKV + GDN_QKV
OFF_Z = OFF_SSM + SSM_WIDTH
OFF_NQ = OFF_Z + GDN_HEADS * GDN_DV
OFF_SGU = OFF_NQ + NSA_Q
OFF_GATE = OFF_SGU + 2 * SGU_WIDTH
OFF_KV = OFF_GATE + N_BRANCH * D_MODEL
OFF_SMALL = OFF_KV + 6 * NSA_KV
SMALL_USED = 2 * GDN_HEADS + 3 * NSA_HEADS
PROJ_COLS = 9216


def _params(*sem):
    return pltpu.CompilerParams(dimension_semantics=sem, vmem_limit_bytes=VMEM_LIMIT)


def _rms(x, g):
    return x * lax.rsqrt(jnp.mean(x * x, axis=-1, keepdims=True) + RMS_EPS) * g


def _in_proj_body(x_ref, g_ref, w_ref, o_ref, h_ref):
    @pl.when(pl.program_id(1) == 0)
    def _():
        h_ref[...] = _rms(x_ref[...], g_ref[...]).astype(BF16)

    o_ref[...] = jnp.dot(h_ref[...], w_ref[...], preferred_element_type=F32)


def in_proj(x, g, w, tm=1024, tn=512):
    t, d = x.shape
    n = w.shape[1]
    return pl.pallas_call(
        _in_proj_body,
        grid=(t // tm, n // tn),
        in_specs=[pl.BlockSpec((tm, d), lambda i, j: (i, 0)),
                  pl.BlockSpec((1, d), lambda i, j: (0, 0)),
                  pl.BlockSpec((d, tn), lambda i, j: (0, j))],
        out_specs=pl.BlockSpec((tm, tn), lambda i, j: (i, j)),
        out_shape=jax.ShapeDtypeStruct((t, n), F32),
        scratch_shapes=[pltpu.VMEM((tm, d), BF16)],
        compiler_params=_params("parallel", "arbitrary"),
        name="in_proj",
    )(x, g.reshape(1, d), w)


def _merge_body(x_ref, gate_ref, ya_ref, yb_ref, yc_ref, yd_ref,
                wa_ref, wb_ref, wc_ref, wd_ref, wo_ref, o_ref):
    def branch(k, y_ref, w_ref):
        p = jnp.dot(y_ref[...].astype(BF16), w_ref[...], preferred_element_type=F32)
        return jax.nn.sigmoid(gate_ref[:, k * D_MODEL:(k + 1) * D_MODEL]) * p

    merged = (branch(0, ya_ref, wa_ref) + branch(1, yb_ref, wb_ref)
              + branch(2, yc_ref, wc_ref) + branch(3, yd_ref, wd_ref))
    o_ref[...] = x_ref[...] + jnp.dot(merged.astype(BF16), wo_ref[...],
                                      preferred_element_type=F32)


def merge(x, proj, ys, ws, w_out, tm=256):
    t, d = x.shape
    gate_blk = OFF_GATE // (N_BRANCH * D_MODEL)
    row = lambda i: (i, 0)
    fixed = lambda i: (0, 0)
    in_specs = [pl.BlockSpec((tm, d), row),
                pl.BlockSpec((tm, N_BRANCH * D_MODEL), lambda i: (i, gate_blk))]
    in_specs += [pl.BlockSpec((tm, y.shape[1]), row) for y in ys]
    in_specs += [pl.BlockSpec(w.shape, fixed) for w in ws]
    in_specs += [pl.BlockSpec(w_out.shape, fixed)]
    return pl.pallas_call(
        _merge_body,
        grid=(t // tm,),
        in_specs=in_specs,
        out_specs=pl.BlockSpec((tm, d), row),
        out_shape=jax.ShapeDtypeStruct((t, d), F32),
        compiler_params=_params("parallel"),
        name="merge",
    )(x, proj, *ys, *ws, w_out)


def _ffn_body(x_ref, g_ref, wg_ref, wu_ref, wd_ref, o_ref, h_ref, acc_ref):
    f = pl.program_id(1)

    @pl.when(f == 0)
    def _():
        h_ref[...] = _rms(x_ref[...], g_ref[...]).astype(BF16)
        acc_ref[...] = x_ref[...]

    h = h_ref[...]
    a = jnp.dot(h, wg_ref[...], preferred_element_type=F32)
    u = jnp.dot(h, wu_ref[...], preferred_element_type=F32)
    act = (a * jax.nn.sigmoid(a) * u).astype(BF16)
    acc_ref[...] += jnp.dot(act, wd_ref[...], preferred_element_type=F32)

    @pl.when(f == pl.num_programs(1) - 1)
    def _():
        o_ref[...] = acc_ref[...]


def dense_ffn(x, g, wg, wu, wd, tm=1024, tf=256):
    t, d = x.shape
    nf = wg.shape[1]
    return pl.pallas_call(
        _ffn_body,
        grid=(t // tm, nf // tf),
        in_specs=[pl.BlockSpec((tm, d), lambda i, f: (i, 0)),
                  pl.BlockSpec((1, d), lambda i, f: (0, 0)),
                  pl.BlockSpec((d, tf), lambda i, f: (0, f)),
                  pl.BlockSpec((d, tf), lambda i, f: (0, f)),
                  pl.BlockSpec((tf, d), lambda i, f: (f, 0))],
        out_specs=pl.BlockSpec((tm, d), lambda i, f: (i, 0)),
        out_shape=jax.ShapeDtypeStruct((t, d), F32),
        scratch_shapes=[pltpu.VMEM((tm, d), BF16), pltpu.VMEM((tm, d), F32)],
        compiler_params=_params("parallel", "arbitrary"),
        name="dense_ffn",
    )(x, g.reshape(1, d), wg, wu, wd)


def _route_body(x_ref, g_ref, rhi_ref, rlo_ref, h_ref, idx_ref, wt_ref):
    h = _rms(x_ref[...], g_ref[...])
    hi = h.astype(BF16)
    lo = (h - hi.astype(F32)).astype(BF16)
    h_ref[...] = hi
    logits = (jnp.dot(hi, rhi_ref[...], preferred_element_type=F32)
              + jnp.dot(hi, rlo_ref[...], preferred_element_type=F32)
              + jnp.dot(lo, rhi_ref[...], preferred_element_type=F32))
    col = lax.broadcasted_iota(jnp.int32, logits.shape, 1).astype(F32)
    neg = jnp.float32(-jnp.inf)
    lg = jnp.where(col < N_EXPERTS, logits, neg)
    m1 = jnp.max(lg, axis=-1, keepdims=True)
    i1 = jnp.min(jnp.where(lg == m1, col, float(LANES)), axis=-1, keepdims=True)
    lg2 = jnp.where(col == i1, neg, lg)
    m2 = jnp.max(lg2, axis=-1, keepdims=True)
    i2 = jnp.min(jnp.where(lg2 == m2, col, float(LANES)), axis=-1, keepdims=True)
    e = jnp.exp(m2 - m1)
    w1 = 1.0 / (1.0 + e)
    w2 = e / (1.0 + e)
    idx_ref[...] = jnp.where(col == 0, i1, jnp.where(col == 1, i2, 0.0)).astype(jnp.int32)
    wt_ref[...] = jnp.where(col == 0, w1, jnp.where(col == 1, w2, 0.0))


def moe_route(x, g, r_hi, r_lo, tm=512):
    t, d = x.shape
    row = lambda i: (i, 0)
    fixed = lambda i: (0, 0)
    return pl.pallas_call(
        _route_body,
        grid=(t // tm,),
        in_specs=[pl.BlockSpec((tm, d), row), pl.BlockSpec((1, d), fixed),
                  pl.BlockSpec((d, LANES), fixed), pl.BlockSpec((d, LANES), fixed)],
        out_specs=[pl.BlockSpec((tm, d), row), pl.BlockSpec((tm, LANES), row),
                   pl.BlockSpec((tm, LANES), row)],
        out_shape=[jax.ShapeDtypeStruct((t, d), BF16),
                   jax.ShapeDtypeStruct((t, LANES), jnp.int32),
                   jax.ShapeDtypeStruct((t, LANES), F32)],
        compiler_params=_params("parallel"),
        name="moe_route",
    )(x, g.reshape(1, d), r_hi, r_lo)


def _experts_body(te_ref, nu_ref, xs_ref, rw_ref, wg_ref, wu_ref, wd_ref, o_ref, acc_ref):
    i = pl.program_id(0)
    f = pl.program_id(1)
    used = i < nu_ref[0]

    @pl.when(f == 0)
    def _():
        acc_ref[...] = jnp.zeros_like(acc_ref)

    @pl.when(used)
    def _():
        h = xs_ref[...]
        a = jnp.dot(h, wg_ref[0], preferred_element_type=F32)
        u = jnp.dot(h, wu_ref[0], preferred_element_type=F32)
        act = (a * jax.nn.sigmoid(a) * u).astype(BF16)
        acc_ref[...] += jnp.dot(act, wd_ref[0], preferred_element_type=F32)

    @pl.when(f == pl.num_programs(1) - 1)
    def _():
        o_ref[...] = acc_ref[...] * rw_ref[...]


def moe_experts(tile_expert, n_used, xs, row_w, wg, wu, wd, tm, tf=512):
    p, d = xs.shape
    nf = wg.shape[2]
    grid_spec = pltpu.PrefetchScalarGridSpec(
        num_scalar_prefetch=2,
        grid=(p // tm, nf // tf),
        in_specs=[pl.BlockSpec((tm, d), lambda i, f, te, nu: (i, 0)),
                  pl.BlockSpec((tm, 1), lambda i, f, te, nu: (i, 0)),
                  pl.BlockSpec((1, d, tf), lambda i, f, te, nu: (te[i], 0, f)),
                  pl.BlockSpec((1, d, tf), lambda i, f, te, nu: (te[i], 0, f)),
                  pl.BlockSpec((1, tf, d), lambda i, f, te, nu: (te[i], f, 0))],
        out_specs=pl.BlockSpec((tm, d), lambda i, f, te, nu: (i, 0)),
        scratch_shapes=[pltpu.VMEM((tm, d), F32)],
    )
    return pl.pallas_call(
        _experts_body,
        grid_spec=grid_spec,
        out_shape=jax.ShapeDtypeStruct((p, d), F32),
        compiler_params=_params("parallel", "arbitrary"),
        name="moe_experts",
    )(tile_expert, n_used, xs, row_w, wg, wu, wd)


def _combine_norm_body(x_ref, ya_ref, yb_ref, g_ref, o_ref):
    o_ref[...] = _rms(x_ref[...] + ya_ref[...] + yb_ref[...], g_ref[...])


def combine_norm(x, ya, yb, g, tm=1024):
    t, d = x.shape
    row = lambda i: (i, 0)
    return pl.pallas_call(
        _combine_norm_body,
        grid=(t // tm,),
        in_specs=[pl.BlockSpec((tm, d), row)] * 3 + [pl.BlockSpec((1, d), lambda i: (0, 0))],
        out_specs=pl.BlockSpec((tm, d), row),
        out_shape=jax.ShapeDtypeStruct((t, d), F32),
        compiler_params=_params("parallel"),
        name="combine_norm",
    )(x, ya, yb, g.reshape(1, d))


def moe_layer(x, g_norm, router, wg, wu, wd, g_final, tm=512):
    t, d = x.shape
    r = jnp.zeros((d, LANES), F32).at[:, :N_EXPERTS].set(router)
    r_hi = r.astype(BF16)
    r_lo = (r - r_hi.astype(F32)).astype(BF16)
    h, idx, wts = moe_route(x, g_norm, r_hi, r_lo)
    e_flat = jnp.concatenate([idx[:, 0], idx[:, 1]])
    w_flat = jnp.concatenate([wts[:, 0], wts[:, 1]])
    onehot = (e_flat[:, None] == jnp.arange(N_EXPERTS)[None, :]).astype(jnp.int32)
    csum = jnp.cumsum(onehot, axis=0)
    rank = jnp.sum((csum - onehot) * onehot, axis=1)
    counts = csum[-1]
    padded = ((counts + tm - 1) // tm) * tm
    ends = jnp.cumsum(padded)
    starts = ends - padded
    dest = starts[e_flat] + rank
    n_tiles = (TOP_K * t) // tm + N_EXPERTS
    p = n_tiles * tm
    tok = jnp.concatenate([jnp.arange(t, dtype=jnp.int32)] * TOP_K)
    src = jnp.zeros((p,), jnp.int32).at[dest].set(tok)
    row_w = jnp.zeros((p,), F32).at[dest].set(w_flat)
    tile_expert = jnp.minimum(
        jnp.searchsorted(ends, jnp.arange(n_tiles, dtype=jnp.int32) * tm, side="right"),
        N_EXPERTS - 1).astype(jnp.int32)
    n_used = (ends[-1] // tm).astype(jnp.int32).reshape(1)
    xs = jnp.take(h, src, axis=0)
    ys = moe_experts(tile_expert, n_used, xs, row_w.reshape(p, 1), wg, wu, wd, tm)
    ya = jnp.take(ys, dest[:t], axis=0)
    yb = jnp.take(ys, dest[t:], axis=0)
    return combine_norm(x, ya, yb, g_final)


def _l2_norm(t):
    return t * lax.rsqrt(jnp.sum(t * t, axis=-1, keepdims=True) + 1e-6)


def _masked_softmax(logits, valid):
    logits = jnp.where(valid, logits.astype(F32), -1e30)
    return jax.nn.softmax(logits, axis=-1) * valid


def _rel_bucket(dist):
    n = jnp.maximum(dist, 0)
    max_exact = REL_BUCKETS // 2
    nf = jnp.maximum(n, 1).astype(F32)
    large = max_exact + (jnp.log(nf / max_exact) / math.log(REL_MAX_DIST / max_exact)
                         * (REL_BUCKETS - max_exact)).astype(jnp.int32)
    large = jnp.minimum(large, REL_BUCKETS - 1)
    return jnp.where(n < max_exact, n, large)


SSM_TC = 512
SSM_SUB = 128
SSM_HALF = 256
SSM_NSTATE = SSM_GROUPS * SSM_STATE


def _ssm_body(u_ref, bw_ref, cw_ref, d_ref, glu_ref, ar_ref, ai_ref, pr_ref, pi_ref, o_ref,
              xr_ref, xi_ref, cr_ref, ci_ref):
    @pl.when(pl.program_id(1) == 0)
    def _():
        cr_ref[...] = jnp.zeros_like(cr_ref)
        ci_ref[...] = jnp.zeros_like(ci_ref)

    u = u_ref[0]
    ub = u.astype(BF16)
    nblk = SSM_WIDTH // SSM_HALF
    sblk = SSM_NSTATE // nblk
    for k in range(nblk):
        bu = jnp.dot(ub[:, k * SSM_HALF:(k + 1) * SSM_HALF], bw_ref[k], preferred_element_type=F32)
        xr_ref[:, k * sblk:(k + 1) * sblk] = bu[:, :sblk]
        xi_ref[:, k * sblk:(k + 1) * sblk] = bu[:, sblk:]

    row = lax.broadcasted_iota(jnp.int32, (SSM_SUB, LANES), 0)
    n_steps = SSM_SUB.bit_length() - 1

    def shift(x, d):
        if d % 8 == 0:
            return jnp.concatenate([jnp.zeros((d, LANES), F32), x[:SSM_SUB - d]], axis=0)
        return jnp.where(row >= d, pltpu.roll(x, d, 0), 0.0)

    def strip(c, carry):
        col = pl.ds(pl.multiple_of(c * LANES, LANES), LANES)
        c_r = cr_ref[:, col]
        c_i = ci_ref[:, col]
        for sub in range(SSM_TC // SSM_SUB):
            rows = slice(sub * SSM_SUB, (sub + 1) * SSM_SUB)
            xr = xr_ref[rows, col]
            xi = xi_ref[rows, col]
            for i in range(n_steps):
                a_r = ar_ref[i:i + 1, col]
                a_i = ai_ref[i:i + 1, col]
                sr, si = shift(xr, 1 << i), shift(xi, 1 << i)
                xr, xi = xr + a_r * sr - a_i * si, xi + a_r * si + a_i * sr
            p_r = pr_ref[:, col]
            p_i = pi_ref[:, col]
            xr, xi = xr + p_r * c_r - p_i * c_i, xi + p_r * c_i + p_i * c_r
            xr_ref[rows, col] = xr
            xi_ref[rows, col] = xi
            c_r = xr[SSM_SUB - 1:SSM_SUB, :]
            c_i = xi[SSM_SUB - 1:SSM_SUB, :]
        cr_ref[:, col] = c_r
        ci_ref[:, col] = c_i
        return carry

    lax.fori_loop(0, SSM_NSTATE // LANES, strip, 0)

    ys = []
    for k in range(nblk):
        st = jnp.concatenate([xr_ref[:, k * sblk:(k + 1) * sblk].astype(BF16),
                              xi_ref[:, k * sblk:(k + 1) * sblk].astype(BF16)], axis=1)
        ys.append(jnp.dot(st, cw_ref[k], preferred_element_type=F32))
    y = jax.nn.gelu(jnp.concatenate(ys, axis=1) + d_ref[...] * u)
    o_ref[0] = y * jax.nn.sigmoid(jnp.dot(y.astype(BF16), glu_ref[...], preferred_element_type=F32))


def ssm_mixer(u, a_re, a_im, log_dt, b_re, b_im, c_re, c_im, d_skip, glu_w):
    bs, s, _ = u.shape
    assert s % SSM_TC == 0
    nblk = SSM_WIDTH // SSM_HALF
    gpb = SSM_GROUPS // nblk
    dt = jnp.exp(log_dt)[:, None]
    mag = jnp.exp(a_re * dt)
    abar_r, abar_i = mag * jnp.cos(a_im * dt), mag * jnp.sin(a_im * dt)
    nr, ni = abar_r - 1.0, abar_i
    den = a_re * a_re + a_im * a_im
    sr, si = (nr * a_re + ni * a_im) / den, (ni * a_re - nr * a_im) / den
    bbar_r = sr[..., None] * b_re - si[..., None] * b_im
    bbar_i = sr[..., None] * b_im + si[..., None] * b_re

    def powers(k):
        kk = k.astype(F32)[:, None, None]
        m = jnp.exp(kk * (a_re * dt))
        return ((m * jnp.cos(kk * (a_im * dt))).reshape(-1, SSM_NSTATE),
                (m * jnp.sin(kk * (a_im * dt))).reshape(-1, SSM_NSTATE))

    ar, ai = powers(2 ** jnp.arange(8))
    pr, pi = powers(jnp.arange(1, SSM_SUB + 1))
    eye = jnp.eye(gpb, dtype=F32)

    def in_block(w):
        return jnp.einsum('gnp,gh->gphn', w, eye).reshape(gpb * SSM_GROUP, gpb * SSM_STATE)

    def out_block(w):
        return jnp.einsum('gpn,gh->gnhp', w, eye).reshape(gpb * SSM_STATE, gpb * SSM_GROUP)

    bw = jnp.stack([jnp.concatenate([in_block(bbar_r[k * gpb:(k + 1) * gpb]),
                                     in_block(bbar_i[k * gpb:(k + 1) * gpb])], axis=1)
                    for k in range(nblk)]).astype(BF16)
    cw = jnp.stack([jnp.concatenate([out_block(c_re[k * gpb:(k + 1) * gpb]),
                                     -out_block(c_im[k * gpb:(k + 1) * gpb])], axis=0)
                    for k in range(nblk)]).astype(BF16)
    fixed2 = lambda b, i: (0, 0)
    fixed3 = lambda b, i: (0, 0, 0)
    return pl.pallas_call(
        _ssm_body,
        grid=(bs, s // SSM_TC),
        in_specs=[pl.BlockSpec((1, SSM_TC, SSM_WIDTH), lambda b, i: (b, i, 0)),
                  pl.BlockSpec(bw.shape, fixed3), pl.BlockSpec(cw.shape, fixed3),
                  pl.BlockSpec((1, SSM_WIDTH), fixed2), pl.BlockSpec((SSM_WIDTH, SSM_WIDTH), fixed2),
                  pl.BlockSpec((8, SSM_NSTATE), fixed2), pl.BlockSpec((8, SSM_NSTATE), fixed2),
                  pl.BlockSpec((SSM_SUB, SSM_NSTATE), fixed2), pl.BlockSpec((SSM_SUB, SSM_NSTATE), fixed2)],
        out_specs=pl.BlockSpec((1, SSM_TC, SSM_WIDTH), lambda b, i: (b, i, 0)),
        out_shape=jax.ShapeDtypeStruct((bs, s, SSM_WIDTH), F32),
        scratch_shapes=[pltpu.VMEM((SSM_TC, SSM_NSTATE), F32), pltpu.VMEM((SSM_TC, SSM_NSTATE), F32),
                        pltpu.VMEM((1, SSM_NSTATE), F32), pltpu.VMEM((1, SSM_NSTATE), F32)],
        compiler_params=_params("parallel", "arbitrary"),
        name="ssm_scan",
    )(u, bw, cw, d_skip.reshape(1, SSM_WIDTH), glu_w.astype(BF16), ar, ai, pr, pi)


def _ssm_mixer_jax(u, a_re, a_im, log_dt, b_re, b_im, c_re, c_im, d_skip, glu_w):
    bs, s, _ = u.shape
    uf = u.reshape(bs, s, SSM_GROUPS, SSM_GROUP)
    dt = jnp.exp(log_dt)[:, None]
    mag = jnp.exp(a_re * dt)
    abar_r, abar_i = mag * jnp.cos(a_im * dt), mag * jnp.sin(a_im * dt)
    nr, ni = abar_r - 1.0, abar_i
    den = a_re * a_re + a_im * a_im
    sr, si = (nr * a_re + ni * a_im) / den, (ni * a_re - nr * a_im) / den
    bbar_r = sr[..., None] * b_re - si[..., None] * b_im
    bbar_i = sr[..., None] * b_im + si[..., None] * b_re
    bu_r = jnp.einsum('bsgp,gnp->bsgn', uf, bbar_r)
    bu_i = jnp.einsum('bsgp,gnp->bsgn', uf, bbar_i)
    ar = jnp.broadcast_to(abar_r, bu_r.shape)
    ai = jnp.broadcast_to(abar_i, bu_r.shape)

    def combine(e1, e2):
        a1r, a1i, b1r, b1i = e1
        a2r, a2i, b2r, b2i = e2
        return (a2r * a1r - a2i * a1i, a2r * a1i + a2i * a1r,
                a2r * b1r - a2i * b1i + b2r, a2r * b1i + a2i * b1r + b2i)

    _, _, st_r, st_i = lax.associative_scan(combine, (ar, ai, bu_r, bu_i), axis=1)
    y = (jnp.einsum('gpn,bsgn->bsgp', c_re, st_r) - jnp.einsum('gpn,bsgn->bsgp', c_im, st_i)
         + d_skip.reshape(SSM_GROUPS, SSM_GROUP) * uf)
    y = jax.nn.gelu(y.reshape(bs, s, SSM_WIDTH))
    return y * jax.nn.sigmoid(y @ glu_w)


GDN_TC = 256
GDN_HALO = 8
MASKED = -1e30


def _dot_nt(a, b):
    return lax.dot_general(a, b, (((1,), (1,)), ((), ())), preferred_element_type=F32)


def _dot_tn(a, b):
    return lax.dot_general(a, b, (((0,), (0,)), ((), ())), preferred_element_type=F32)


def _split_bf16(x):
    hi = x.astype(BF16)
    return hi, (x - hi.astype(F32)).astype(BF16)


def _gdn_body(nega_ref, dtb_ref, q_ref, k_ref, v_ref, z_ref, sm_ref, wq_ref, wk_ref, wv_ref, ng_ref,
              o_ref, state_ref, hq_ref, hk_ref, hv_ref, vnew_ref):
    TC, CH, DK = GDN_TC, GDN_CHUNK, GDN_DK
    h = pl.program_id(1)

    @pl.when(pl.program_id(2) == 0)
    def _():
        state_ref[...] = jnp.zeros_like(state_ref)
        hq_ref[...] = jnp.zeros_like(hq_ref)
        hk_ref[...] = jnp.zeros_like(hk_ref)
        hv_ref[...] = jnp.zeros_like(hv_ref)

    def conv_silu(x_ref, halo_ref, w_ref):
        x = x_ref[...]
        xx = jnp.concatenate([halo_ref[...], x], axis=0)
        w = w_ref[...]
        y = w[GDN_CONV - 1:GDN_CONV] * x
        for d in range(1, GDN_CONV):
            y = y + w[GDN_CONV - 1 - d:GDN_CONV - d] * pltpu.roll(xx, d, 0)[GDN_HALO:GDN_HALO + TC]
        halo_ref[...] = x[TC - GDN_HALO:TC]
        return y * jax.nn.sigmoid(y)

    q = conv_silu(q_ref, hq_ref, wq_ref)
    k = conv_silu(k_ref, hk_ref, wk_ref)
    v = conv_silu(v_ref, hv_ref, wv_ref)
    q = q * lax.rsqrt(jnp.sum(q * q, axis=-1, keepdims=True) + 1e-6) * (DK ** -0.5)
    k = k * lax.rsqrt(jnp.sum(k * k, axis=-1, keepdims=True) + 1e-6)

    small = sm_ref[...]
    lane_s = lax.broadcasted_iota(jnp.int32, small.shape, 1)
    a_col = jnp.sum(jnp.where(lane_s == h, small, 0.0), axis=-1, keepdims=True)
    b_col = jnp.sum(jnp.where(lane_s == GDN_HEADS + h, small, 0.0), axis=-1, keepdims=True)
    beta = jax.nn.sigmoid(b_col)
    xa = a_col + dtb_ref[h]
    softplus = jnp.maximum(xa, 0.0) + jnp.log(1.0 + jnp.exp(-jnp.abs(xa)))
    la = jnp.broadcast_to(nega_ref[h] * softplus, (TC, LANES))

    ri = lax.broadcasted_iota(jnp.int32, (TC, TC), 0)
    ci = lax.broadcasted_iota(jnp.int32, (TC, TC), 1)
    same = (ri // CH) == (ci // CH)
    causal = jnp.where(same, jnp.where(ci <= ri, 1.0, 0.0), 0.0)
    strict = jnp.where(ci < ri, causal, 0.0)
    tri = causal.astype(BF16)
    la_hi, la_lo = _split_bf16(la)
    g = (jnp.dot(tri, la_hi, preferred_element_type=F32)
         + jnp.dot(tri, la_lo, preferred_element_type=F32))
    g_hi = g.astype(BF16).astype(F32)
    g_lo = g - g_hi
    lane = lax.broadcasted_iota(jnp.int32, (TC, LANES), 1)
    lhs = jnp.where(lane == 0, g_hi, jnp.where(lane == 1, g_lo, jnp.where(lane < 4, 1.0, 0.0))).astype(BF16)
    rhs = jnp.where(lane < 2, 1.0, jnp.where(lane == 2, -g_hi, jnp.where(lane == 3, -g_lo, 0.0))).astype(BF16)
    decay = jnp.exp(jnp.where(causal > 0.5, _dot_nt(lhs, rhs), MASKED))

    kb = k * beta
    k_b = k.astype(BF16)
    lmat = strict * _dot_nt(kb.astype(BF16), k_b) * decay
    eye = jnp.where(ri == ci, 1.0, 0.0)
    tinv = eye - lmat
    pw = lmat
    for _ in range(CH.bit_length() - 2):
        pb = pw.astype(BF16)
        pw = jnp.dot(pb, pb, preferred_element_type=F32)
        tinv = tinv + jnp.dot(tinv.astype(BF16), pw.astype(BF16), preferred_element_type=F32)
    eg = jnp.exp(g)
    rhs_all = jnp.concatenate([v * beta, kb * eg], axis=1).astype(BF16)
    sol = jnp.dot(tinv.astype(BF16), rhs_all, preferred_element_type=F32)
    u_val, w_val = sol[:, :GDN_DV], sol[:, GDN_DV:]
    attn = (causal * _dot_nt(q.astype(BF16), k_b) * decay).astype(BF16)
    q_st = (q * eg).astype(BF16)

    vnew_ref[...] = jnp.zeros_like(vnew_ref)
    for c in range(TC // CH):
        rows = slice(c * CH, (c + 1) * CH)
        g_last = g[(c + 1) * CH - 1:(c + 1) * CH, :]
        st = state_ref[...]
        st_b = st.astype(BF16)
        v_new = u_val[rows] - jnp.dot(w_val[rows].astype(BF16), st_b, preferred_element_type=F32)
        vnew_ref[rows, :] = v_new.astype(BF16)
        o_c = (jnp.dot(q_st[rows], st_b, preferred_element_type=F32)
               + jnp.dot(attn[rows], vnew_ref[...], preferred_element_type=F32))
        k_st = (k[rows] * jnp.exp(g_last - g[rows])).astype(BF16)
        state_ref[...] = st * jnp.exp(g_last[:, :1]) + _dot_tn(k_st, v_new.astype(BF16))
        o_c = o_c * lax.rsqrt(jnp.mean(o_c * o_c, axis=-1, keepdims=True) + RMS_EPS) * ng_ref[...]
        zc = z_ref[rows, :]
        o_ref[rows, :] = o_c * (zc * jax.nn.sigmoid(zc))


def gdn_mixer(proj, bs, conv_w, a_log, dt_bias, norm_g):
    t = proj.shape[0]
    s = t // bs
    nt = s // GDN_TC
    assert s % GDN_TC == 0 and GDN_DK == LANES and GDN_DV == LANES and OFF_QKV == 0
    H = GDN_HEADS
    cw = jnp.zeros((8, GDN_QKV), F32).at[:GDN_CONV].set(conv_w)
    tok = lambda off: pl.BlockSpec((GDN_TC, LANES), lambda b, h, i, *_: (b * nt + i, off + h))
    wspec = lambda off: pl.BlockSpec((8, LANES), lambda b, h, i, *_: (0, off + h))
    grid_spec = pltpu.PrefetchScalarGridSpec(
        num_scalar_prefetch=0,
        grid=(bs, H, nt),
        in_specs=[pl.BlockSpec(memory_space=pltpu.SMEM), pl.BlockSpec(memory_space=pltpu.SMEM),
                  tok(0), tok(H), tok(2 * H), tok(OFF_Z // LANES),
                  pl.BlockSpec((GDN_TC, 2 * LANES), lambda b, h, i, *_: (b * nt + i, OFF_SMALL // (2 * LANES))),
                  wspec(0), wspec(H), wspec(2 * H),
                  pl.BlockSpec((1, GDN_DV), lambda b, h, i, *_: (0, 0))],
        out_specs=pl.BlockSpec((GDN_TC, GDN_DV), lambda b, h, i, *_: (b * nt + i, h)),
        scratch_shapes=[pltpu.VMEM((GDN_DK, GDN_DV), F32)] + [pltpu.VMEM((GDN_HALO, LANES), F32)] * 3
        + [pltpu.VMEM((GDN_TC, GDN_DV), BF16)],
    )
    return pl.pallas_call(
        _gdn_body,
        grid_spec=grid_spec,
        out_shape=jax.ShapeDtypeStruct((t, H * GDN_DV), F32),
        compiler_params=_params("parallel", "parallel", "arbitrary"),
        name="gdn_delta",
    )(-jnp.exp(a_log), dt_bias.astype(F32), proj, proj, proj, proj, proj, cw, cw, cw,
      norm_g.reshape(1, GDN_DV))


def _gdn_mixer_jax(qkv, a, b, z, conv_w, a_log, dt_bias, norm_g):
    bs, s, c = qkv.shape
    H, DK, DV, CH = GDN_HEADS, GDN_DK, GDN_DV, GDN_CHUNK
    qkv = lax.conv_general_dilated(qkv, conv_w[:, None, :], window_strides=(1,),
                                   padding=[(GDN_CONV - 1, 0)],
                                   dimension_numbers=('NWC', 'WIO', 'NWC'), feature_group_count=c)
    qkv = jax.nn.silu(qkv)
    q, k, v = jnp.split(qkv, [H * DK, 2 * H * DK], axis=-1)
    q = _l2_norm(q.reshape(bs, s, H, DK)) * (DK ** -0.5)
    k = _l2_norm(k.reshape(bs, s, H, DK))
    v = v.reshape(bs, s, H, DV)
    beta = jax.nn.sigmoid(b)
    log_alpha = -jnp.exp(a_log) * jax.nn.softplus(a + dt_bias)
    nc = s // CH

    def chunks(t):
        return t.reshape(bs, nc, CH, H, -1).transpose(0, 3, 1, 2, 4)

    q, k, v = chunks(q), chunks(k), chunks(v)
    beta = chunks(beta[..., None])
    g = jnp.cumsum(chunks(log_alpha[..., None]), axis=3)
    gv = g[..., 0]
    causal = jnp.tril(jnp.ones((CH, CH), bool))
    strict = jnp.tril(jnp.ones((CH, CH), bool), -1)
    decay = jnp.exp(jnp.where(causal, gv[..., :, None] - gv[..., None, :], -jnp.inf))
    kb = k * beta
    a_mat = (jnp.where(strict, jnp.einsum('bhncd,bhnkd->bhnck', kb, k) * decay, 0.0)
             + jnp.eye(CH, dtype=F32))
    rhs = jnp.concatenate([v * beta, kb * jnp.exp(g)], axis=-1)
    sol = lax.linalg.triangular_solve(a_mat, rhs, left_side=True, lower=True, unit_diagonal=True)
    u_val, w = sol[..., :DV], sol[..., DV:]
    attn = jnp.einsum('bhncd,bhnkd->bhnck', q, k) * decay
    g_last = g[..., -1:, :]
    k_st = k * jnp.exp(g_last - g)
    q_st = q * jnp.exp(g)

    def step(state, xs):
        q_c, k_c, u_c, w_c, a_c, gl_c = xs
        v_new = u_c - jnp.einsum('bhcd,bhde->bhce', w_c, state)
        o = jnp.einsum('bhcd,bhde->bhce', q_c, state) + jnp.einsum('bhck,bhke->bhce', a_c, v_new)
        state = state * jnp.exp(gl_c) + jnp.einsum('bhcd,bhce->bhde', k_c, v_new)
        return state, o

    xs = tuple(jnp.moveaxis(t, 2, 0) for t in (q_st, k_st, u_val, w, attn, g_last))
    _, o = lax.scan(step, jnp.zeros((bs, H, DK, DV), F32), xs)
    o = o.transpose(1, 0, 3, 2, 4).reshape(bs, s, H, DV)
    o = o * lax.rsqrt(jnp.mean(o * o, axis=-1, keepdims=True) + RMS_EPS) * norm_g
    o = o * jax.nn.silu(z.reshape(bs, s, H, DV))
    return o.reshape(bs, s, H * DV)


NSA_KT = 256
NSA_NEG = MASKED
NSA_WTILES = NSA_WINDOW // LANES + 1


def _bucket_table():
    n = np.arange(LANES)
    max_exact = REL_BUCKETS // 2
    nf = np.maximum(n, 1).astype(np.float32)
    large = max_exact + (np.log(nf / np.float32(max_exact)) / np.float32(math.log(REL_MAX_DIST / max_exact))
                         * np.float32(REL_BUCKETS - max_exact)).astype(np.int32)
    tbl = np.where(n < max_exact, n, np.minimum(large, REL_BUCKETS - 1))
    assert tbl[-1] == REL_BUCKETS - 1 and REL_MAX_DIST <= LANES
    return tbl


def _compress_body(x_ref, pe_ref, w1_ref, w2_ref, o_ref):
    x = x_ref[0, 0, 0]
    w1 = w1_ref[0]
    half = NSA_CMP_STRIDE * NSA_DH
    nc = x.shape[0]
    a = jnp.dot(x, w1[:half], preferred_element_type=F32)
    b = jnp.dot(x, w1[half:], preferred_element_type=F32)
    pe_h = jnp.dot(pe_ref[0], w1, preferred_element_type=F32)
    hid = a + pltpu.roll(b, nc - 1, 0) + pe_h[0:1, :]
    o_ref[0, 0, 0] = jnp.dot(jax.nn.gelu(hid).astype(BF16), w2_ref[0], preferred_element_type=F32)


def nsa_compress(x, pe, w1, w2):
    _, bs, g, nc, width = x.shape
    return pl.pallas_call(
        _compress_body,
        grid=(2, bs, g),
        in_specs=[pl.BlockSpec((1, 1, 1, nc, width), lambda i, b, j: (i, b, j, 0, 0)),
                  pl.BlockSpec((1,) + pe.shape[1:], lambda i, b, j: (i, 0, 0)),
                  pl.BlockSpec((1,) + w1.shape[1:], lambda i, b, j: (i, 0, 0)),
                  pl.BlockSpec((1,) + w2.shape[1:], lambda i, b, j: (i, 0, 0))],
        out_specs=pl.BlockSpec((1, 1, 1, nc, NSA_DH), lambda i, b, j: (i, b, j, 0, 0)),
        out_shape=jax.ShapeDtypeStruct((2, bs, g, nc, NSA_DH), F32),
        compiler_params=_params("parallel", "parallel", "parallel"),
        name="nsa_compress",
    )(x, pe, w1, w2)


def _nsa_body(nfast, near_w, near_c, q_ref, gl_ref, qp_ref, kp_ref, cp_ref,
              ks_ref, vs_ref, kw_ref, vw_ref, kc_ref, vc_ref, agg_ref, td_ref, o_ref,
              m_ref, l_ref, acc_ref, sel_ref, *, n_sel, sel_k):
    QB, HG = NSA_QBLOCK, NSA_HPG
    g = pl.program_id(1)
    qi = pl.program_id(2)
    s0 = qi * QB
    nc = kc_ref.shape[2]
    n_ct = nc // LANES
    q = q_ref[0, 0, 0]
    qp = qp_ref[...]

    def lanes4(x):
        return jnp.concatenate([x] * HG, axis=1)

    def delta_t(kp):
        idx = jnp.clip(_dot_nt(kp, qp), 0.0, float(LANES - 1)).astype(jnp.int32)
        outs = []
        for hg in range(HG):
            tb = jnp.broadcast_to(td_ref[pl.ds(g * HG + hg, 1), :], idx.shape)
            outs.append(jnp.take_along_axis(tb, idx, axis=1))
        return jnp.concatenate(outs, axis=1)

    def maybe_delta(flag, kp):
        return lax.cond(flag != 0, lambda: delta_t(kp),
                        lambda: jnp.zeros((kp.shape[0], HG * QB), F32))

    gate = jax.nn.sigmoid(gl_ref[0, 0, 0])

    sc = _dot_nt(kc_ref[0, 0], q)
    sc = sc + jnp.concatenate(
        [maybe_delta(near_c[qi * n_ct + ct], cp_ref[ct * LANES:(ct + 1) * LANES, :])
         for ct in range(n_ct)], axis=0)
    c_id = lax.broadcasted_iota(jnp.int32, (nc, QB), 0)
    c_q = lax.broadcasted_iota(jnp.int32, (nc, QB), 1)
    ok_c = lanes4(jnp.where(c_id * NSA_CMP_STRIDE + (NSA_CMP_LEN - 1) <= s0 + c_q, 1.0, 0.0))
    sc = jnp.where(ok_c > 0.5, sc, NSA_NEG)
    e = jnp.exp(sc - jnp.max(sc, axis=0, keepdims=True)) * ok_c
    p = e * (1.0 / jnp.maximum(jnp.sum(e, axis=0, keepdims=True), 1e-30))
    out = gate[0:1, :] * jnp.dot(vc_ref[0, 0], p.astype(BF16), preferred_element_type=F32)
    ps = p[:, 0:QB]
    for hg in range(1, HG):
        ps = ps + p[:, hg * QB:(hg + 1) * QB]
    ps_hi, ps_lo = _split_bf16(ps)
    imp = (jnp.dot(agg_ref[...], ps_hi, preferred_element_type=F32)
           + jnp.dot(agg_ref[...], ps_lo, preferred_element_type=F32))
    blk = lax.broadcasted_iota(jnp.int32, (LANES, QB), 0)
    t_tok = s0 + lax.broadcasted_iota(jnp.int32, (LANES, QB), 1)
    tb_blk = t_tok // NSA_SEL_LEN
    forced = jnp.where(blk == 0, 1.0, 0.0) + jnp.where(blk == tb_blk, 1.0, 0.0) \
        + jnp.where(blk == tb_blk - 1, 1.0, 0.0)
    score = jnp.where(forced > 0.5, 1e9, jnp.where(blk * NSA_SEL_LEN <= t_tok, imp, -1e9))
    score = jnp.where(blk < n_sel, score, -jnp.inf)
    blkf = blk.astype(F32)
    sel = jnp.zeros((LANES, QB), F32)
    for _ in range(sel_k):
        mx = jnp.max(score, axis=0, keepdims=True)
        first = jnp.min(jnp.where(score == mx, blkf, float(LANES)), axis=0, keepdims=True)
        hit = blkf == first
        sel = jnp.where(hit, 1.0, sel)
        score = jnp.where(hit, -jnp.inf, score)
    sel_ref[...] = jnp.where(sel > 0.5, 0.0, NSA_NEG)

    def reset():
        m_ref[...] = jnp.full_like(m_ref, NSA_NEG)
        l_ref[...] = jnp.zeros_like(l_ref)
        acc_ref[...] = jnp.zeros_like(acc_ref)

    def tile_step(s, v_t):
        m_old = m_ref[...]
        m_new = jnp.maximum(m_old, jnp.max(s, axis=0, keepdims=True))
        alpha = jnp.exp(m_old - m_new)
        p = jnp.exp(s - m_new)
        l_ref[...] = l_ref[...] * alpha + jnp.sum(p, axis=0, keepdims=True)
        acc_ref[...] = acc_ref[...] * alpha + jnp.dot(v_t, p.astype(BF16), preferred_element_type=F32)
        m_ref[...] = m_new

    def finish():
        return acc_ref[...] * (1.0 / l_ref[...])

    per_tile = NSA_KT // NSA_SEL_LEN
    key_r = lax.broadcasted_iota(jnp.int32, (NSA_KT, QB), 0)
    key_q = lax.broadcasted_iota(jnp.int32, (NSA_KT, QB), 1)

    def sel_tile(kt, with_delta, diag):
        k0 = pl.multiple_of(kt * NSA_KT, NSA_KT)
        s = _dot_nt(ks_ref[0, 0, pl.ds(k0, NSA_KT), :], q)
        if with_delta:
            s = s + delta_t(kp_ref[pl.ds(k0, NSA_KT), :])
        mask = jnp.concatenate(
            [jnp.broadcast_to(sel_ref[pl.ds(kt * per_tile + i, 1), :], (NSA_SEL_LEN, QB))
             for i in range(per_tile)], axis=0)
        if diag:
            mask = jnp.where(k0 + key_r <= s0 + key_q, mask, NSA_NEG)
        tile_step(s + lanes4(mask), vs_ref[0, 0, :, pl.ds(k0, NSA_KT)])

    reset()
    n_past = (s0 + QB - 1) // NSA_KT
    n_fast = nfast[qi]

    def fast_body(kt, carry):
        sel_tile(kt, False, False)
        return carry

    def slow_body(kt, carry):
        sel_tile(kt, True, False)
        return carry

    lax.fori_loop(0, n_fast, fast_body, 0)
    lax.fori_loop(n_fast, n_past, slow_body, 0)
    sel_tile(n_past, True, True)
    out = out + gate[1:2, :] * finish()

    reset()
    w_r = lax.broadcasted_iota(jnp.int32, (LANES, QB), 0)
    w_q = lax.broadcasted_iota(jnp.int32, (LANES, QB), 1)
    for j in range(NSA_WTILES):
        start = s0 - NSA_WINDOW + j * LANES

        @pl.when(start >= 0)
        def _():
            st = pl.multiple_of(start, LANES)
            s = _dot_nt(kw_ref[0, 0, pl.ds(st, LANES), :], q)
            s = s + maybe_delta(near_w[qi * NSA_WTILES + j], kp_ref[pl.ds(st, LANES), :])
            if j == 0:
                s = s + lanes4(jnp.where(w_r > w_q, 0.0, NSA_NEG))
            elif j == NSA_WTILES - 1:
                s = s + lanes4(jnp.where(w_r <= w_q, 0.0, NSA_NEG))
            tile_step(s, vw_ref[0, 0, :, pl.ds(st, LANES)])

    o_ref[0, 0, 0] = out + gate[2:3, :] * finish()


def _pos_digits(p, key_side):
    d0, d1, d2 = (p & 127).astype(F32), ((p >> 7) & 127).astype(F32), (p >> 14).astype(F32)
    one = jnp.ones_like(d0)
    if key_side:
        cols = [one, one, one, -d2, -d1, -d0]
    else:
        cols = [16384.0 * d2, 128.0 * d1, d0, 16384.0 * one, 128.0 * one, one]
    out = jnp.stack(cols, axis=-1)
    return jnp.pad(out, ((0, 0), (0, LANES - out.shape[-1]))).astype(BF16)


def nsa_mixer(q, k_c, v_c, k_s, v_s, k_w, v_w, gate_logits, positions,
              ck_pe, ck_w1, ck_w2, cv_pe, cv_w1, cv_w2, rel_bias):
    bs, s, _ = q.shape
    G, HG, DH, QB = NSA_KV_GROUPS, NSA_HPG, NSA_DH, NSA_QBLOCK
    nqt = s // QB
    nc = s // NSA_CMP_STRIDE
    n_cmp = (s - NSA_CMP_LEN) // NSA_CMP_STRIDE + 1
    n_sel = s // NSA_SEL_LEN
    sel_k = min(NSA_SEL_TOPK, n_sel)
    n_kt = s // NSA_KT
    n_ct = nc // LANES
    assert s % (LANES * NSA_CMP_STRIDE) == 0 and n_sel <= LANES and NSA_KT == 2 * QB

    def by_group(t):
        return t.reshape(bs, s, G, DH).transpose(0, 2, 1, 3)

    def by_group_t(t):
        return t.reshape(bs, s, G, DH).transpose(0, 2, 3, 1).astype(BF16)

    ones2 = jnp.zeros((LANES - DH,), F32).at[:2].set(1.0)

    def keys_aug(t):
        return jnp.concatenate([t, jnp.broadcast_to(ones2, t.shape[:-1] + (LANES - DH,))],
                               axis=-1).astype(BF16)

    chunks = jnp.stack([by_group(k_c), by_group(v_c)]).reshape(2, bs, G, nc, NSA_CMP_STRIDE * DH)
    pe = jnp.stack([ck_pe, cv_pe]).reshape(2, 1, NSA_CMP_LEN * DH)
    cmp = nsa_compress(chunks.astype(BF16), jnp.broadcast_to(pe, (2, 8, NSA_CMP_LEN * DH)).astype(BF16),
                       jnp.stack([ck_w1, cv_w1]).astype(BF16), jnp.stack([ck_w2, cv_w2]).astype(BF16))
    kc, vc_t = keys_aug(cmp[0]), cmp[1].transpose(0, 1, 3, 2).astype(BF16)

    far = rel_bias[REL_BUCKETS - 1].astype(F32)
    far_hi = far.astype(BF16).astype(F32)
    extra = jnp.zeros((NSA_HEADS, LANES - DH), F32).at[:, 0].set(far_hi).at[:, 1].set(far - far_hi)
    qh = (q * (DH ** -0.5)).reshape(bs, nqt, QB, G, HG, DH).transpose(0, 3, 1, 4, 2, 5)
    ex = jnp.broadcast_to(extra.reshape(1, G, 1, HG, 1, LANES - DH), (bs, G, nqt, HG, QB, LANES - DH))
    qs = jnp.concatenate([qh, ex], axis=-1).astype(BF16).reshape(bs, G, nqt, HG * QB, LANES)
    gl = gate_logits.reshape(bs, nqt, QB, G, HG, 3).transpose(0, 3, 1, 5, 4, 2)
    gl = jnp.pad(gl.reshape(bs, G, nqt, 3, HG * QB), ((0, 0),) * 3 + ((0, 5), (0, 0)))

    td = (rel_bias[_bucket_table()] - rel_bias[REL_BUCKETS - 1][None, :]).T.astype(F32)
    cmp_end = jnp.minimum(jnp.arange(nc) * NSA_CMP_STRIDE + NSA_CMP_LEN - 1, s - 1)
    cpos = positions[cmp_end]
    qmin = positions.reshape(nqt, QB).min(axis=1)
    kmax = positions.reshape(nqt, QB).max(axis=1)
    near_s = (qmin[:, None] - positions.reshape(n_kt, NSA_KT).max(axis=1)[None, :]) < REL_MAX_DIST
    n_past = (jnp.arange(nqt) * QB + QB - 1) // NSA_KT
    n_fast = jnp.minimum(jnp.argmax(jnp.concatenate([near_s, jnp.ones((nqt, 1), bool)], axis=1), axis=1),
                         n_past)
    wt = jnp.clip(jnp.arange(nqt)[:, None] - (NSA_WTILES - 1) + jnp.arange(NSA_WTILES)[None, :], 0, nqt - 1)
    near_w = (qmin[:, None] - kmax[wt]) < REL_MAX_DIST
    near_c = (qmin[:, None] - cpos.reshape(n_ct, LANES).max(axis=1)[None, :]) < REL_MAX_DIST

    cmp_start = np.arange(nc) * NSA_CMP_STRIDE
    sel_start = np.arange(LANES) * NSA_SEL_LEN
    agg_t = ((cmp_start[None, :] <= sel_start[:, None] + NSA_SEL_LEN - 1)
             & (cmp_start[None, :] + NSA_CMP_LEN - 1 >= sel_start[:, None])
             & (np.arange(nc)[None, :] < n_cmp) & (np.arange(LANES)[:, None] < n_sel))

    cols = HG * QB
    full = lambda shape: pl.BlockSpec(shape, lambda b, g, i, *_: (0,) * len(shape))
    per_bg = lambda n, w: pl.BlockSpec((1, 1, n, w), lambda b, g, i, *_: (b, g, 0, 0))
    grid_spec = pltpu.PrefetchScalarGridSpec(
        num_scalar_prefetch=3,
        grid=(bs, G, nqt),
        in_specs=[pl.BlockSpec((1, 1, 1, cols, LANES), lambda b, g, i, *_: (b, g, i, 0, 0)),
                  pl.BlockSpec((1, 1, 1, 8, cols), lambda b, g, i, *_: (b, g, i, 0, 0)),
                  pl.BlockSpec((QB, LANES), lambda b, g, i, *_: (i, 0)),
                  full((s, LANES)), full((nc, LANES)),
                  per_bg(s, LANES), per_bg(DH, s), per_bg(s, LANES), per_bg(DH, s),
                  per_bg(nc, LANES), per_bg(DH, nc),
                  full((LANES, nc)), full((NSA_HEADS, LANES))],
        out_specs=pl.BlockSpec((1, 1, 1, DH, cols), lambda b, g, i, *_: (b, g, i, 0, 0)),
        scratch_shapes=[pltpu.VMEM((1, cols), F32), pltpu.VMEM((1, cols), F32),
                        pltpu.VMEM((DH, cols), F32), pltpu.VMEM((LANES, QB), F32)],
    )
    out = pl.pallas_call(
        functools.partial(_nsa_body, n_sel=n_sel, sel_k=sel_k),
        grid_spec=grid_spec,
        out_shape=jax.ShapeDtypeStruct((bs, G, nqt, DH, cols), F32),
        compiler_params=_params("parallel", "parallel", "arbitrary"),
        name="nsa_attention",
    )(n_fast.astype(jnp.int32), near_w.reshape(-1).astype(jnp.int32), near_c.reshape(-1).astype(jnp.int32),
      qs, gl, _pos_digits(positions, False), _pos_digits(positions, True), _pos_digits(cpos, True),
      keys_aug(by_group(k_s)), by_group_t(v_s), keys_aug(by_group(k_w)), by_group_t(v_w),
      kc, vc_t, jnp.asarray(agg_t, BF16), td)
    out = out.reshape(bs, G, nqt, DH, HG, QB).transpose(0, 2, 5, 1, 4, 3)
    return out.reshape(bs, s, G * HG * DH)


def _nsa_body_rowmajor(near_s, near_w, near_c, q_ref, gl_ref, posq_ref, posk_ref, cpos_ref,
              ks_ref, vs_ref, kw_ref, vw_ref, kc_ref, vc_ref, agg_ref, td_ref, o_ref,
              s_ref, m_ref, l_ref, acc_ref, *, n_sel, sel_k):
    QB, HG = NSA_QBLOCK, NSA_HPG
    g = pl.program_id(1)
    qi = pl.program_id(2)
    s0 = qi * QB
    nc = kc_ref.shape[2]
    n_ct = nc // LANES
    n_kt = ks_ref.shape[2] // NSA_KT
    q = q_ref[0, 0, 0]
    tp = posq_ref[...]
    row = lax.broadcasted_iota(jnp.int32, (QB, LANES), 0)
    lane = lax.broadcasted_iota(jnp.int32, (QB, LANES), 1)
    t_tok = s0 + row

    def add_delta(col0, pk_row):
        idx = jnp.clip(tp - pk_row, 0, LANES - 1)
        for hg in range(HG):
            tb = jnp.broadcast_to(td_ref[pl.ds(g * HG + hg, 1), :], (QB, LANES))
            s_ref[hg * QB:(hg + 1) * QB, col0:col0 + LANES] += jnp.take_along_axis(tb, idx, axis=1)

    def heads(x):
        return jnp.concatenate([x] * HG, axis=0)

    gate = jax.nn.sigmoid(gl_ref[0, 0, 0])

    s_ref[:, :nc] = _dot_nt(q, kc_ref[0, 0])
    for ct in range(n_ct):
        @pl.when(near_c[qi * n_ct + ct] != 0)
        def _():
            add_delta(ct * LANES, cpos_ref[ct:ct + 1, :])
    c_id = lax.broadcasted_iota(jnp.int32, (QB, nc), 1)
    c_row = lax.broadcasted_iota(jnp.int32, (QB, nc), 0)
    ok_c = heads(jnp.where(c_id * NSA_CMP_STRIDE + (NSA_CMP_LEN - 1) <= s0 + c_row, 1.0, 0.0))
    sc = jnp.where(ok_c > 0.5, s_ref[:, :nc], NSA_NEG)
    e = jnp.exp(sc - jnp.max(sc, axis=-1, keepdims=True)) * ok_c
    p = e * (1.0 / jnp.maximum(jnp.sum(e, axis=-1, keepdims=True), 1e-30))
    o_c = jnp.dot(p.astype(BF16), vc_ref[0, 0], preferred_element_type=F32)
    o_ref[0, 0, 0] = gate[:, 0:1] * o_c
    ps = p[0:QB]
    for hg in range(1, HG):
        ps = ps + p[hg * QB:(hg + 1) * QB]
    ps_hi = ps.astype(BF16)
    ps_lo = (ps - ps_hi.astype(F32)).astype(BF16)
    imp = (jnp.dot(ps_hi, agg_ref[...], preferred_element_type=F32)
           + jnp.dot(ps_lo, agg_ref[...], preferred_element_type=F32))
    tb_blk = t_tok // NSA_SEL_LEN
    forced = jnp.where(lane == 0, 1.0, 0.0) + jnp.where(lane == tb_blk, 1.0, 0.0) \
        + jnp.where(lane == tb_blk - 1, 1.0, 0.0)
    score = jnp.where(forced > 0.5, 1e9, jnp.where(lane * NSA_SEL_LEN <= t_tok, imp, -1e9))
    score = jnp.where(lane < n_sel, score, -jnp.inf)
    lanef = lane.astype(F32)
    sel = jnp.zeros((QB, LANES), F32)
    for _ in range(sel_k):
        mx = jnp.max(score, axis=-1, keepdims=True)
        first = jnp.min(jnp.where(score == mx, lanef, float(LANES)), axis=-1, keepdims=True)
        hit = lanef == first
        sel = jnp.where(hit, 1.0, sel)
        score = jnp.where(hit, -jnp.inf, score)
    sel_b = sel.astype(BF16)

    def reset():
        m_ref[...] = jnp.full_like(m_ref, NSA_NEG)
        l_ref[...] = jnp.zeros_like(l_ref)
        acc_ref[...] = jnp.zeros_like(acc_ref)

    def tile_step(width, mask_add, v):
        s = s_ref[:, :width]
        if mask_add is not None:
            s = s + heads(mask_add)
        m_old = m_ref[...]
        m_new = jnp.maximum(m_old, jnp.max(s, axis=-1, keepdims=True))
        alpha = jnp.exp(m_old - m_new)
        p = jnp.exp(s - m_new)
        psum = p[:, :LANES]
        for c in range(1, width // LANES):
            psum = psum + p[:, c * LANES:(c + 1) * LANES]
        l_ref[...] = l_ref[...] * alpha + psum
        acc_ref[...] = acc_ref[...] * alpha + jnp.dot(p.astype(BF16), v, preferred_element_type=F32)
        m_ref[...] = m_new

    def finish():
        return acc_ref[...] * (1.0 / jnp.sum(l_ref[...], axis=-1, keepdims=True))

    blk_i = lax.broadcasted_iota(jnp.int32, (LANES, NSA_KT), 0)
    key_i = lax.broadcasted_iota(jnp.int32, (LANES, NSA_KT), 1)
    key_q = lax.broadcasted_iota(jnp.int32, (QB, NSA_KT), 1)
    row_q = lax.broadcasted_iota(jnp.int32, (QB, NSA_KT), 0)
    per_tile = NSA_KT // NSA_SEL_LEN

    def sel_tile(kt, diag):
        k0 = pl.multiple_of(kt * NSA_KT, NSA_KT)
        s_ref[:, :NSA_KT] = _dot_nt(q, ks_ref[0, 0, pl.ds(k0, NSA_KT), :])

        @pl.when(near_s[qi * n_kt + kt] != 0)
        def _():
            for c in range(NSA_KT // LANES):
                add_delta(c * LANES, posk_ref[pl.ds(kt * (NSA_KT // LANES) + c, 1), :])

        expand = jnp.where(blk_i == kt * per_tile + key_i // NSA_SEL_LEN, 1.0, 0.0).astype(BF16)
        picked = jnp.dot(sel_b, expand, preferred_element_type=F32)
        if diag:
            picked = jnp.where(k0 + key_q <= s0 + row_q, picked, 0.0)
        tile_step(NSA_KT, jnp.where(picked > 0.5, 0.0, NSA_NEG), vs_ref[0, 0, pl.ds(k0, NSA_KT), :])

    reset()
    n_past = (s0 + QB - 1) // NSA_KT

    def past_body(kt, carry):
        sel_tile(kt, False)
        return carry

    lax.fori_loop(0, n_past, past_body, 0)
    sel_tile(n_past, True)
    o_ref[0, 0, 0] += gate[:, 1:2] * finish()

    reset()
    for j in range(NSA_WTILES):
        start = s0 - NSA_WINDOW + j * LANES

        @pl.when(start >= 0)
        def _():
            st = pl.multiple_of(start, LANES)
            s_ref[:, :LANES] = _dot_nt(q, kw_ref[0, 0, pl.ds(st, LANES), :])

            @pl.when(near_w[qi * NSA_WTILES + j] != 0)
            def _():
                add_delta(0, posk_ref[pl.ds(qi - (NSA_WTILES - 1) + j, 1), :])

            if j == 0:
                mask_add = jnp.where(lane > row, 0.0, NSA_NEG)
            elif j == NSA_WTILES - 1:
                mask_add = jnp.where(lane <= row, 0.0, NSA_NEG)
            else:
                mask_add = None
            tile_step(LANES, mask_add, vw_ref[0, 0, pl.ds(st, LANES), :])

    o_ref[0, 0, 0] += gate[:, 2:3] * finish()


def _nsa_mixer_rowmajor(q, k_c, v_c, k_s, v_s, k_w, v_w, gate_logits, positions,
                        ck_pe, ck_w1, ck_w2, cv_pe, cv_w1, cv_w2, rel_bias):
    bs, s, _ = q.shape
    G, HG, DH, QB = NSA_KV_GROUPS, NSA_HPG, NSA_DH, NSA_QBLOCK
    nqt = s // QB
    nc = s // NSA_CMP_STRIDE
    n_cmp = (s - NSA_CMP_LEN) // NSA_CMP_STRIDE + 1
    n_sel = s // NSA_SEL_LEN
    sel_k = min(NSA_SEL_TOPK, n_sel)
    n_kt = s // NSA_KT
    n_ct = nc // LANES
    assert s % (LANES * NSA_CMP_STRIDE) == 0 and n_sel <= LANES and NSA_KT == 2 * QB

    def by_group(t):
        return t.reshape(bs, s, G, DH).transpose(0, 2, 1, 3)

    ones2 = jnp.zeros((LANES - DH,), F32).at[:2].set(1.0)

    def keys_aug(t):
        return jnp.concatenate([t, jnp.broadcast_to(ones2, t.shape[:-1] + (LANES - DH,))],
                               axis=-1).astype(BF16)

    chunks = jnp.stack([by_group(k_c), by_group(v_c)]).reshape(2, bs, G, nc, NSA_CMP_STRIDE * DH)
    pe = jnp.stack([ck_pe, cv_pe]).reshape(2, 1, NSA_CMP_LEN * DH)
    cmp = nsa_compress(chunks.astype(BF16), jnp.broadcast_to(pe, (2, 8, NSA_CMP_LEN * DH)).astype(BF16),
                       jnp.stack([ck_w1, cv_w1]).astype(BF16), jnp.stack([ck_w2, cv_w2]).astype(BF16))
    kc, vc = keys_aug(cmp[0]), cmp[1].astype(BF16)

    far = rel_bias[REL_BUCKETS - 1].astype(F32)
    far_hi = far.astype(BF16).astype(F32)
    extra = jnp.zeros((NSA_HEADS, LANES - DH), F32).at[:, 0].set(far_hi).at[:, 1].set(far - far_hi)
    qh = (q * (DH ** -0.5)).reshape(bs, nqt, QB, G, HG, DH).transpose(0, 3, 1, 4, 2, 5)
    ex = jnp.broadcast_to(extra.reshape(1, G, 1, HG, 1, LANES - DH), (bs, G, nqt, HG, QB, LANES - DH))
    qs = jnp.concatenate([qh, ex], axis=-1).astype(BF16).reshape(bs, G, nqt, HG * QB, LANES)
    gl = gate_logits.reshape(bs, nqt, QB, G, HG, 3).transpose(0, 3, 1, 4, 2, 5)
    gl = jnp.pad(gl, ((0, 0),) * 5 + ((0, 5),)).reshape(bs, G, nqt, HG * QB, 8)

    td = (rel_bias[_bucket_table()] - rel_bias[REL_BUCKETS - 1][None, :]).T.astype(F32)
    cmp_end = jnp.minimum(jnp.arange(nc) * NSA_CMP_STRIDE + NSA_CMP_LEN - 1, s - 1)
    cpos = positions[cmp_end]
    qmin = positions.reshape(nqt, QB).min(axis=1)
    kmax = positions.reshape(nqt, QB).max(axis=1)
    near_s = (qmin[:, None] - positions.reshape(n_kt, NSA_KT).max(axis=1)[None, :]) < REL_MAX_DIST
    wt = jnp.clip(jnp.arange(nqt)[:, None] - (NSA_WTILES - 1) + jnp.arange(NSA_WTILES)[None, :], 0, nqt - 1)
    near_w = (qmin[:, None] - kmax[wt]) < REL_MAX_DIST
    near_c = (qmin[:, None] - cpos.reshape(n_ct, LANES).max(axis=1)[None, :]) < REL_MAX_DIST

    cmp_start = np.arange(nc) * NSA_CMP_STRIDE
    sel_start = np.arange(LANES) * NSA_SEL_LEN
    agg = ((cmp_start[:, None] <= sel_start[None, :] + NSA_SEL_LEN - 1)
           & (cmp_start[:, None] + NSA_CMP_LEN - 1 >= sel_start[None, :])
           & (np.arange(nc)[:, None] < n_cmp) & (np.arange(LANES)[None, :] < n_sel))

    rows = HG * QB
    full = lambda shape: pl.BlockSpec(shape, lambda b, g, i, *_: (0,) * len(shape))
    per_bg = lambda n, w: pl.BlockSpec((1, 1, n, w), lambda b, g, i, *_: (b, g, 0, 0))
    grid_spec = pltpu.PrefetchScalarGridSpec(
        num_scalar_prefetch=3,
        grid=(bs, G, nqt),
        in_specs=[pl.BlockSpec((1, 1, 1, rows, LANES), lambda b, g, i, *_: (b, g, i, 0, 0)),
                  pl.BlockSpec((1, 1, 1, rows, 8), lambda b, g, i, *_: (b, g, i, 0, 0)),
                  pl.BlockSpec((QB, 1), lambda b, g, i, *_: (i, 0)),
                  full((nqt, LANES)), full((n_ct, LANES)),
                  per_bg(s, LANES), per_bg(s, DH), per_bg(s, LANES), per_bg(s, DH),
                  per_bg(nc, LANES), per_bg(nc, DH),
                  full((nc, LANES)), full((NSA_HEADS, LANES))],
        out_specs=pl.BlockSpec((1, 1, 1, rows, DH), lambda b, g, i, *_: (b, g, i, 0, 0)),
        scratch_shapes=[pltpu.VMEM((rows, max(nc, NSA_KT)), F32), pltpu.VMEM((rows, 1), F32),
                        pltpu.VMEM((rows, LANES), F32), pltpu.VMEM((rows, DH), F32)],
    )
    out = pl.pallas_call(
        functools.partial(_nsa_body, n_sel=n_sel, sel_k=sel_k),
        grid_spec=grid_spec,
        out_shape=jax.ShapeDtypeStruct((bs, G, nqt, rows, DH), F32),
        compiler_params=_params("parallel", "parallel", "arbitrary"),
        name="nsa_attention",
    )(near_s.reshape(-1).astype(jnp.int32), near_w.reshape(-1).astype(jnp.int32),
      near_c.reshape(-1).astype(jnp.int32),
      qs, gl, positions.reshape(s, 1), positions.reshape(nqt, LANES), cpos.reshape(n_ct, LANES),
      keys_aug(by_group(k_s)), by_group(v_s).astype(BF16), keys_aug(by_group(k_w)),
      by_group(v_w).astype(BF16), kc, vc, jnp.asarray(agg, BF16), td)
    out = out.reshape(bs, G, nqt, HG, QB, DH).transpose(0, 2, 4, 1, 3, 5)
    return out.reshape(bs, s, G * HG * DH)


def _nsa_mixer_jax(q, k_c, v_c, k_s, v_s, k_w, v_w, gate_logits, positions,
                   ck_pe, ck_w1, ck_w2, cv_pe, cv_w1, cv_w2, rel_bias):
    bs, s, _ = q.shape
    G, HG, DH, QB, W = NSA_KV_GROUPS, NSA_HPG, NSA_DH, NSA_QBLOCK, NSA_WINDOW
    q = q.reshape(bs, s, G, HG, DH) * (DH ** -0.5)

    def kv(t):
        return t.reshape(bs, s, G, DH)

    k_c, v_c, k_s, v_s, k_w, v_w = (kv(t) for t in (k_c, v_c, k_s, v_s, k_w, v_w))
    gates = jax.nn.sigmoid(gate_logits).reshape(bs, s, G, HG, 3)
    n_cmp = (s - NSA_CMP_LEN) // NSA_CMP_STRIDE + 1
    cmp_start = jnp.arange(n_cmp) * NSA_CMP_STRIDE
    cmp_end = cmp_start + NSA_CMP_LEN - 1
    cmp_tok = cmp_start[:, None] + jnp.arange(NSA_CMP_LEN)[None, :]

    def compress(t, pe, w1, w2):
        blk = t[:, cmp_tok] + pe[None, None, :, None, :]
        blk = blk.transpose(0, 1, 3, 2, 4).reshape(bs, n_cmp, G, NSA_CMP_LEN * DH)
        return jax.nn.gelu(blk @ w1) @ w2

    kc = compress(k_c, ck_pe, ck_w1, ck_w2)
    vc = compress(v_c, cv_pe, cv_w1, cv_w2)
    cmp_pos = positions[cmp_end]
    n_sel = s // NSA_SEL_LEN
    sel_k = min(NSA_SEL_TOPK, n_sel)
    sel_start = jnp.arange(n_sel) * NSA_SEL_LEN
    agg = ((cmp_start[:, None] <= sel_start[None, :] + NSA_SEL_LEN - 1)
           & (cmp_end[:, None] >= sel_start[None, :])).astype(F32)
    ks_blk = k_s.reshape(bs, n_sel, NSA_SEL_LEN, G, DH).transpose(0, 3, 1, 2, 4)
    vs_blk = v_s.reshape(bs, n_sel, NSA_SEL_LEN, G, DH).transpose(0, 3, 1, 2, 4)
    pos_blk = positions.reshape(n_sel, NSA_SEL_LEN)
    kw_pad = jnp.pad(k_w, ((0, 0), (W, 0), (0, 0), (0, 0)))
    vw_pad = jnp.pad(v_w, ((0, 0), (W, 0), (0, 0), (0, 0)))
    pos_pad = jnp.pad(positions, (W, 0))
    tbl = rel_bias.reshape(REL_BUCKETS, G, HG)
    b_ix = jnp.arange(bs)[:, None, None, None]
    g_ix = jnp.arange(G)[None, :, None, None]
    sel_offs = jnp.arange(NSA_SEL_LEN)
    win_offs = jnp.arange(QB + W)
    blk_j = jnp.arange(n_sel)

    def one_block(qi):
        s0 = qi * QB
        qb = lax.dynamic_slice_in_dim(q, s0, QB, axis=1)
        t = s0 + jnp.arange(QB)
        tp = lax.dynamic_slice_in_dim(positions, s0, QB)
        bias_c = tbl[_rel_bucket(tp[:, None] - cmp_pos[None, :])].transpose(2, 3, 0, 1)
        lg_c = jnp.einsum('bqghd,bcgd->bghqc', qb, kc) + bias_c
        p_c = _masked_softmax(lg_c, cmp_end[None, :] <= t[:, None])
        o_c = jnp.einsum('bghqc,bcgd->bqghd', p_c, vc)
        imp = jnp.einsum('bghqc,cj->bgqj', p_c, agg)
        tb = t // NSA_SEL_LEN
        forced = ((blk_j[None, :] == 0) | (blk_j[None, :] == tb[:, None])
                  | (blk_j[None, :] == tb[:, None] - 1))
        eligible = sel_start[None, :] <= t[:, None]
        score = jnp.where(forced, 1e9, jnp.where(eligible, imp, -1e9))
        _, sel = lax.top_k(score, sel_k)
        k_sel = ks_blk[b_ix, g_ix, sel]
        v_sel = vs_blk[b_ix, g_ix, sel]
        tok_s = sel[..., None] * NSA_SEL_LEN + sel_offs
        bias_s = tbl[_rel_bucket(tp[:, None, None] - pos_blk[sel]), g_ix[..., None]]
        lg_s = jnp.einsum('bqghd,bgqnkd->bghqnk', qb, k_sel) + bias_s.transpose(0, 1, 5, 2, 3, 4)
        valid_s = (tok_s <= t[:, None, None])[:, :, None].reshape(bs, G, 1, QB, -1)
        p_s = _masked_softmax(lg_s.reshape(bs, G, HG, QB, -1), valid_s)
        p_s = p_s.reshape(bs, G, HG, QB, sel_k, NSA_SEL_LEN)
        o_s = jnp.einsum('bghqnk,bgqnkd->bqghd', p_s, v_sel)
        kwb = lax.dynamic_slice_in_dim(kw_pad, s0, QB + W, axis=1)
        vwb = lax.dynamic_slice_in_dim(vw_pad, s0, QB + W, axis=1)
        tok_w = s0 - W + win_offs
        pos_w = lax.dynamic_slice_in_dim(pos_pad, s0, QB + W)
        d_tok = t[:, None] - tok_w[None, :]
        valid_w = (d_tok >= 0) & (d_tok < W) & (tok_w[None, :] >= 0)
        bias_w = tbl[_rel_bucket(tp[:, None] - pos_w[None, :])].transpose(2, 3, 0, 1)
        lg_w = jnp.einsum('bqghd,bkgd->bghqk', qb, kwb) + bias_w
        p_w = _masked_softmax(lg_w, valid_w)
        o_w = jnp.einsum('bghqk,bkgd->bqghd', p_w, vwb)
        gb = lax.dynamic_slice_in_dim(gates, s0, QB, axis=1)
        return gb[..., 0:1] * o_c + gb[..., 1:2] * o_s + gb[..., 2:3] * o_w

    out = lax.map(one_block, jnp.arange(s // QB))
    return out.transpose(1, 0, 2, 3, 4, 5).reshape(bs, s, G * HG * DH)


SGU_TC = 512


def _sgu_body(uv_ref, lg_ref, lb_ref, w_ref, b_ref, o_ref):
    uv = jax.nn.gelu(uv_ref[...])
    u, v = uv[:, :SGU_WIDTH], uv[:, SGU_WIDTH:]
    mu = jnp.mean(v, axis=-1, keepdims=True)
    vc = v - mu
    var = jnp.mean(vc * vc, axis=-1, keepdims=True)
    vn = (vc * lax.rsqrt(var + RMS_EPS) * lg_ref[...] + lb_ref[...]).astype(BF16)
    gw = SGU_WIDTH // SGU_GROUPS
    for c in range(SGU_TC // SGU_CHUNK):
        rows = slice(c * SGU_CHUNK, (c + 1) * SGU_CHUNK)
        for g in range(SGU_GROUPS):
            cols = slice(g * gw, (g + 1) * gw)
            mix = jnp.dot(w_ref[g], vn[rows, cols], preferred_element_type=F32) + b_ref[:, g:g + 1]
            o_ref[rows, cols] = u[rows, cols] * mix


def sgu_mixer(proj, ln_g, ln_b, w_s, b_s):
    t = proj.shape[0]
    assert t % SGU_TC == 0 and OFF_SGU % (2 * SGU_WIDTH) == 0
    fixed = lambda i: (0, 0)
    return pl.pallas_call(
        _sgu_body,
        grid=(t // SGU_TC,),
        in_specs=[pl.BlockSpec((SGU_TC, 2 * SGU_WIDTH), lambda i: (i, OFF_SGU // (2 * SGU_WIDTH))),
                  pl.BlockSpec((1, SGU_WIDTH), fixed), pl.BlockSpec((1, SGU_WIDTH), fixed),
                  pl.BlockSpec((SGU_GROUPS, SGU_CHUNK, SGU_CHUNK), lambda i: (0, 0, 0)),
                  pl.BlockSpec((SGU_CHUNK, SGU_GROUPS), fixed)],
        out_specs=pl.BlockSpec((SGU_TC, SGU_WIDTH), lambda i: (i, 0)),
        out_shape=jax.ShapeDtypeStruct((t, SGU_WIDTH), F32),
        compiler_params=_params("parallel"),
        name="sgu_gate",
    )(proj, ln_g.reshape(1, SGU_WIDTH), ln_b.reshape(1, SGU_WIDTH), jnp.tril(w_s).astype(BF16), b_s.T)


def _sgu_mixer_jax(uv, ln_g, ln_b, w_s, b_s):
    bs, s, _ = uv.shape
    uv = jax.nn.gelu(uv)
    u, v = uv[..., :SGU_WIDTH], uv[..., SGU_WIDTH:]
    mu = jnp.mean(v, axis=-1, keepdims=True)
    var = jnp.mean(jnp.square(v - mu), axis=-1, keepdims=True)
    v = (v - mu) * lax.rsqrt(var + RMS_EPS) * ln_g + ln_b
    nc = s // SGU_CHUNK
    v = v.reshape(bs, nc, SGU_CHUNK, SGU_GROUPS, SGU_WIDTH // SGU_GROUPS)
    mix = (jnp.einsum('gij,bcjgd->bcigd', jnp.tril(w_s), v) + b_s.T[None, None, :, :, None])
    return u * mix.reshape(bs, s, SGU_WIDTH)


def _permute_w_in(w):
    sizes = ([SSM_WIDTH, GDN_QKV, GDN_HEADS, GDN_HEADS, GDN_HEADS * GDN_DV, NSA_Q]
             + [NSA_KV] * 6 + [3 * NSA_HEADS, 2 * SGU_WIDTH, N_BRANCH * D_MODEL])
    cuts = [int(c) for c in np.cumsum(sizes)[:-1]]
    (w_ssm, w_qkv, w_a, w_b, w_z, w_nq, w_kc, w_vc, w_ks, w_vs, w_kw, w_vw,
     w_ng, w_sgu, w_gate) = jnp.split(w, cuts, axis=-1)
    pad = jnp.zeros((w.shape[0], PROJ_COLS - OFF_SMALL - SMALL_USED), w.dtype)
    return jnp.concatenate([w_qkv, w_ssm, w_z, w_nq, w_sgu, w_gate, w_kc, w_vc, w_ks, w_vs,
                            w_kw, w_vw, w_a, w_b, w_ng, pad], axis=-1).astype(BF16)


def kernel(x, positions, norm_mix, norm_ffn, norm_final, w_in, ssm_a_re, ssm_a_im, ssm_log_dt, ssm_b_re, ssm_b_im, ssm_c_re, ssm_c_im, ssm_d, ssm_glu_w, gdn_conv_w, gdn_a_log, gdn_dt_bias, gdn_norm, nsa_cmp_k_pe, nsa_cmp_k_w1, nsa_cmp_k_w2, nsa_cmp_v_pe, nsa_cmp_v_w1, nsa_cmp_v_w2, rel_bias, sgu_ln_g, sgu_ln_b, sgu_w, sgu_b, w_br_ssm, w_br_gdn, w_br_nsa, w_br_sgu, w_out, ffn_w_gate, ffn_w_up, ffn_w_down, moe_router, moe_w_gate, moe_w_up, moe_w_down):
    bs, s, d = x.shape
    t = bs * s
    xf = x.reshape(t, d)
    for layer in range(DEPTH):
        proj = in_proj(xf, norm_mix[layer], _permute_w_in(w_in[layer]))
        p3 = proj.reshape(bs, s, PROJ_COLS)

        def seg(off, width):
            return p3[..., off:off + width]

        y_ssm = ssm_mixer(seg(OFF_SSM, SSM_WIDTH), ssm_a_re[layer], ssm_a_im[layer],
                          ssm_log_dt[layer], ssm_b_re[layer], ssm_b_im[layer], ssm_c_re[layer],
                          ssm_c_im[layer], ssm_d[layer], ssm_glu_w[layer])
        y_gdn = gdn_mixer(proj, bs, gdn_conv_w[layer], gdn_a_log[layer], gdn_dt_bias[layer],
                          gdn_norm[layer])
        kvs = [seg(OFF_KV + i * NSA_KV, NSA_KV) for i in range(6)]
        y_nsa = nsa_mixer(seg(OFF_NQ, NSA_Q), *kvs, seg(OFF_SMALL + 2 * GDN_HEADS, 3 * NSA_HEADS),
                          positions, nsa_cmp_k_pe[layer], nsa_cmp_k_w1[layer], nsa_cmp_k_w2[layer],
                          nsa_cmp_v_pe[layer], nsa_cmp_v_w1[layer], nsa_cmp_v_w2[layer], rel_bias)
        y_sgu = sgu_mixer(proj, sgu_ln_g[layer], sgu_ln_b[layer], sgu_w[layer], sgu_b[layer])
        ys = [y_ssm.reshape(t, -1), y_gdn, y_nsa.reshape(t, -1), y_sgu]
        ws = [w[layer].astype(BF16) for w in (w_br_ssm, w_br_gdn, w_br_nsa, w_br_sgu)]
        xf = merge(xf, proj, ys, ws, w_out[layer].astype(BF16))
        i = layer // 2
        if layer % 2 == 0:
            xf = dense_ffn(xf, norm_ffn[layer], ffn_w_gate[i].astype(BF16),
                           ffn_w_up[i].astype(BF16), ffn_w_down[i].astype(BF16))
        else:
            xf = moe_layer(xf, norm_ffn[layer], moe_router[i], moe_w_gate[i].astype(BF16),
                           moe_w_up[i].astype(BF16), moe_w_down[i].astype(BF16), norm_final)
    return xf.reshape(bs, s, d)
```

```python
import functools
import math

import jax
import jax.numpy as jnp
from jax import lax
import numpy as np
from jax.experimental import pallas as pl
from jax.experimental.pallas import tpu as pltpu

F32 = jnp.float32
BF16 = jnp.bfloat16

D_MODEL = 1024
DEPTH = 2
SSM_WIDTH = D_MODEL // 2
SSM_GROUP = 16
SSM_GROUPS = SSM_WIDTH // SSM_GROUP
SSM_STATE = 64
GDN_HEADS = 4
GDN_DK = 128
GDN_DV = 128
GDN_CONV = 4
GDN_CHUNK = 64
NSA_HEADS = 8
NSA_KV_GROUPS = 2
NSA_HPG = NSA_HEADS // NSA_KV_GROUPS
NSA_DH = 64
NSA_CMP_LEN = 32
NSA_CMP_STRIDE = 16
NSA_CMP_HIDDEN = 128
NSA_SEL_LEN = 64
NSA_SEL_TOPK = 16
NSA_WINDOW = 512
NSA_QBLOCK = 128
SGU_WIDTH = D_MODEL // 2
SGU_GROUPS = 4
SGU_CHUNK = 128
REL_BUCKETS = 32
REL_MAX_DIST = 128
FFN_DENSE = 2816
N_EXPERTS = 8
TOP_K = 2
FFN_EXPERT = 3584
N_BRANCH = 4
RMS_EPS = 1e-6
GDN_QKV = GDN_HEADS * (2 * GDN_DK + GDN_DV)
NSA_Q = NSA_HEADS * NSA_DH
NSA_KV = NSA_KV_GROUPS * NSA_DH

LANES = 128
VMEM_LIMIT = 48 * 1024 * 1024

OFF_QKV = 0
OFF_SSM = OFF_QKV + GDN_QKV
OFF_Z = OFF_SSM + SSM_WIDTH
OFF_NQ = OFF_Z + GDN_HEADS * GDN_DV
OFF_SGU = OFF_NQ + NSA_Q
OFF_GATE = OFF_SGU + 2 * SGU_WIDTH
OFF_KV = OFF_GATE + N_BRANCH * D_MODEL
OFF_SMALL = OFF_KV + 6 * NSA_KV
SMALL_USED = 2 * GDN_HEADS + 3 * NSA_HEADS
PROJ_COLS = 9216


def _params(*sem):
    return pltpu.CompilerParams(dimension_semantics=sem, vmem_limit_bytes=VMEM_LIMIT)


def _rms(x, g):
    return x * lax.rsqrt(jnp.mean(x * x, axis=-1, keepdims=True) + RMS_EPS) * g


def _in_proj_body(x_ref, g_ref, w_ref, o_ref, h_ref):
    @pl.when(pl.program_id(1) == 0)
    def _():
        h_ref[...] = _rms(x_ref[...], g_ref[...]).astype(BF16)

    o_ref[...] = jnp.dot(h_ref[...], w_ref[...], preferred_element_type=F32)


def in_proj(x, g, w, tm=1024, tn=512):
    t, d = x.shape
    n = w.shape[1]
    return pl.pallas_call(
        _in_proj_body,
        grid=(t // tm, n // tn),
        in_specs=[pl.BlockSpec((tm, d), lambda i, j: (i, 0)),
                  pl.BlockSpec((1, d), lambda i, j: (0, 0)),
                  pl.BlockSpec((d, tn), lambda i, j: (0, j))],
        out_specs=pl.BlockSpec((tm, tn), lambda i, j: (i, j)),
        out_shape=jax.ShapeDtypeStruct((t, n), F32),
        scratch_shapes=[pltpu.VMEM((tm, d), BF16)],
        compiler_params=_params("parallel", "arbitrary"),
        name="in_proj",
    )(x, g.reshape(1, d), w)


def _merge_body(x_ref, gate_ref, ya_ref, yb_ref, yc_ref, yd_ref,
                wa_ref, wb_ref, wc_ref, wd_ref, wo_ref, o_ref):
    def branch(k, y_ref, w_ref):
        p = jnp.dot(y_ref[...].astype(BF16), w_ref[...], preferred_element_type=F32)
        return jax.nn.sigmoid(gate_ref[:, k * D_MODEL:(k + 1) * D_MODEL]) * p

    merged = (branch(0, ya_ref, wa_ref) + branch(1, yb_ref, wb_ref)
              + branch(2, yc_ref, wc_ref) + branch(3, yd_ref, wd_ref))
    o_ref[...] = x_ref[...] + jnp.dot(merged.astype(BF16), wo_ref[...],
                                      preferred_element_type=F32)


def merge(x, proj, ys, ws, w_out, tm=256):
    t, d = x.shape
    gate_blk = OFF_GATE // (N_BRANCH * D_MODEL)
    row = lambda i: (i, 0)
    fixed = lambda i: (0, 0)
    in_specs = [pl.BlockSpec((tm, d), row),
                pl.BlockSpec((tm, N_BRANCH * D_MODEL), lambda i: (i, gate_blk))]
    in_specs += [pl.BlockSpec((tm, y.shape[1]), row) for y in ys]
    in_specs += [pl.BlockSpec(w.shape, fixed) for w in ws]
    in_specs += [pl.BlockSpec(w_out.shape, fixed)]
    return pl.pallas_call(
        _merge_body,
        grid=(t // tm,),
        in_specs=in_specs,
        out_specs=pl.BlockSpec((tm, d), row),
        out_shape=jax.ShapeDtypeStruct((t, d), F32),
        compiler_params=_params("parallel"),
        name="merge",
    )(x, proj, *ys, *ws, w_out)


def _ffn_body(x_ref, g_ref, wg_ref, wu_ref, wd_ref, o_ref, h_ref, acc_ref):
    f = pl.program_id(1)

    @pl.when(f == 0)
    def _():
        h_ref[...] = _rms(x_ref[...], g_ref[...]).astype(BF16)
        acc_ref[...] = x_ref[...]

    h = h_ref[...]
    a = jnp.dot(h, wg_ref[...], preferred_element_type=F32)
    u = jnp.dot(h, wu_ref[...], preferred_element_type=F32)
    act = (a * jax.nn.sigmoid(a) * u).astype(BF16)
    acc_ref[...] += jnp.dot(act, wd_ref[...], preferred_element_type=F32)

    @pl.when(f == pl.num_programs(1) - 1)
    def _():
        o_ref[...] = acc_ref[...]


def dense_ffn(x, g, wg, wu, wd, tm=1024, tf=256):
    t, d = x.shape
    nf = wg.shape[1]
    return pl.pallas_call(
        _ffn_body,
        grid=(t // tm, nf // tf),
        in_specs=[pl.BlockSpec((tm, d), lambda i, f: (i, 0)),
                  pl.BlockSpec((1, d), lambda i, f: (0, 0)),
                  pl.BlockSpec((d, tf), lambda i, f: (0, f)),
                  pl.BlockSpec((d, tf), lambda i, f: (0, f)),
                  pl.BlockSpec((tf, d), lambda i, f: (f, 0))],
        out_specs=pl.BlockSpec((tm, d), lambda i, f: (i, 0)),
        out_shape=jax.ShapeDtypeStruct((t, d), F32),
        scratch_shapes=[pltpu.VMEM((tm, d), BF16), pltpu.VMEM((tm, d), F32)],
        compiler_params=_params("parallel", "arbitrary"),
        name="dense_ffn",
    )(x, g.reshape(1, d), wg, wu, wd)


def _route_body(x_ref, g_ref, rhi_ref, rlo_ref, h_ref, idx_ref, wt_ref):
    h = _rms(x_ref[...], g_ref[...])
    hi = h.astype(BF16)
    lo = (h - hi.astype(F32)).astype(BF16)
    h_ref[...] = hi
    logits = (jnp.dot(hi, rhi_ref[...], preferred_element_type=F32)
              + jnp.dot(hi, rlo_ref[...], preferred_element_type=F32)
              + jnp.dot(lo, rhi_ref[...], preferred_element_type=F32))
    col = lax.broadcasted_iota(jnp.int32, logits.shape, 1).astype(F32)
    neg = jnp.float32(-jnp.inf)
    lg = jnp.where(col < N_EXPERTS, logits, neg)
    m1 = jnp.max(lg, axis=-1, keepdims=True)
    i1 = jnp.min(jnp.where(lg == m1, col, float(LANES)), axis=-1, keepdims=True)
    lg2 = jnp.where(col == i1, neg, lg)
    m2 = jnp.max(lg2, axis=-1, keepdims=True)
    i2 = jnp.min(jnp.where(lg2 == m2, col, float(LANES)), axis=-1, keepdims=True)
    e = jnp.exp(m2 - m1)
    w1 = 1.0 / (1.0 + e)
    w2 = e / (1.0 + e)
    idx_ref[...] = jnp.where(col == 0, i1, jnp.where(col == 1, i2, 0.0)).astype(jnp.int32)
    wt_ref[...] = jnp.where(col == 0, w1, jnp.where(col == 1, w2, 0.0))


def moe_route(x, g, r_hi, r_lo, tm=512):
    t, d = x.shape
    row = lambda i: (i, 0)
    fixed = lambda i: (0, 0)
    return pl.pallas_call(
        _route_body,
        grid=(t // tm,),
        in_specs=[pl.BlockSpec((tm, d), row), pl.BlockSpec((1, d), fixed),
                  pl.BlockSpec((d, LANES), fixed), pl.BlockSpec((d, LANES), fixed)],
        out_specs=[pl.BlockSpec((tm, d), row), pl.BlockSpec((tm, LANES), row),
                   pl.BlockSpec((tm, LANES), row)],
        out_shape=[jax.ShapeDtypeStruct((t, d), BF16),
                   jax.ShapeDtypeStruct((t, LANES), jnp.int32),
                   jax.ShapeDtypeStruct((t, LANES), F32)],
        compiler_params=_params("parallel"),
        name="moe_route",
    )(x, g.reshape(1, d), r_hi, r_lo)


def _experts_body(te_ref, nu_ref, xs_ref, rw_ref, wg_ref, wu_ref, wd_ref, o_ref, acc_ref):
    i = pl.program_id(0)
    f = pl.program_id(1)
    used = i < nu_ref[0]

    @pl.when(f == 0)
    def _():
        acc_ref[...] = jnp.zeros_like(acc_ref)

    @pl.when(used)
    def _():
        h = xs_ref[...]
        a = jnp.dot(h, wg_ref[0], preferred_element_type=F32)
        u = jnp.dot(h, wu_ref[0], preferred_element_type=F32)
        act = (a * jax.nn.sigmoid(a) * u).astype(BF16)
        acc_ref[...] += jnp.dot(act, wd_ref[0], preferred_element_type=F32)

    @pl.when(f == pl.num_programs(1) - 1)
    def _():
        o_ref[...] = acc_ref[...] * rw_ref[...]


def moe_experts(tile_expert, n_used, xs, row_w, wg, wu, wd, tm, tf=512):
    p, d = xs.shape
    nf = wg.shape[2]
    grid_spec = pltpu.PrefetchScalarGridSpec(
        num_scalar_prefetch=2,
        grid=(p // tm, nf // tf),
        in_specs=[pl.BlockSpec((tm, d), lambda i, f, te, nu: (i, 0)),
                  pl.BlockSpec((tm, 1), lambda i, f, te, nu: (i, 0)),
                  pl.BlockSpec((1, d, tf), lambda i, f, te, nu: (te[i], 0, f)),
                  pl.BlockSpec((1, d, tf), lambda i, f, te, nu: (te[i], 0, f)),
                  pl.BlockSpec((1, tf, d), lambda i, f, te, nu: (te[i], f, 0))],
        out_specs=pl.BlockSpec((tm, d), lambda i, f, te, nu: (i, 0)),
        scratch_shapes=[pltpu.VMEM((tm, d), F32)],
    )
    return pl.pallas_call(
        _experts_body,
        grid_spec=grid_spec,
        out_shape=jax.ShapeDtypeStruct((p, d), F32),
        compiler_params=_params("parallel", "arbitrary"),
        name="moe_experts",
    )(tile_expert, n_used, xs, row_w, wg, wu, wd)


def _combine_norm_body(x_ref, ya_ref, yb_ref, g_ref, o_ref):
    o_ref[...] = _rms(x_ref[...] + ya_ref[...] + yb_ref[...], g_ref[...])


def combine_norm(x, ya, yb, g, tm=1024):
    t, d = x.shape
    row = lambda i: (i, 0)
    return pl.pallas_call(
        _combine_norm_body,
        grid=(t // tm,),
        in_specs=[pl.BlockSpec((tm, d), row)] * 3 + [pl.BlockSpec((1, d), lambda i: (0, 0))],
        out_specs=pl.BlockSpec((tm, d), row),
        out_shape=jax.ShapeDtypeStruct((t, d), F32),
        compiler_params=_params("parallel"),
        name="combine_norm",
    )(x, ya, yb, g.reshape(1, d))


def moe_layer(x, g_norm, router, wg, wu, wd, g_final, tm=512):
    t, d = x.shape
    r = jnp.zeros((d, LANES), F32).at[:, :N_EXPERTS].set(router)
    r_hi = r.astype(BF16)
    r_lo = (r - r_hi.astype(F32)).astype(BF16)
    h, idx, wts = moe_route(x, g_norm, r_hi, r_lo)
    e_flat = jnp.concatenate([idx[:, 0], idx[:, 1]])
    w_flat = jnp.concatenate([wts[:, 0], wts[:, 1]])
    onehot = (e_flat[:, None] == jnp.arange(N_EXPERTS)[None, :]).astype(jnp.int32)
    csum = jnp.cumsum(onehot, axis=0)
    rank = jnp.sum((csum - onehot) * onehot, axis=1)
    counts = csum[-1]
    padded = ((counts + tm - 1) // tm) * tm
    ends = jnp.cumsum(padded)
    starts = ends - padded
    dest = starts[e_flat] + rank
    n_tiles = (TOP_K * t) // tm + N_EXPERTS
    p = n_tiles * tm
    tok = jnp.concatenate([jnp.arange(t, dtype=jnp.int32)] * TOP_K)
    src = jnp.zeros((p,), jnp.int32).at[dest].set(tok)
    row_w = jnp.zeros((p,), F32).at[dest].set(w_flat)
    tile_expert = jnp.minimum(
        jnp.searchsorted(ends, jnp.arange(n_tiles, dtype=jnp.int32) * tm, side="right"),
        N_EXPERTS - 1).astype(jnp.int32)
    n_used = (ends[-1] // tm).astype(jnp.int32).reshape(1)
    xs = jnp.take(h, src, axis=0)
    ys = moe_experts(tile_expert, n_used, xs, row_w.reshape(p, 1), wg, wu, wd, tm)
    ya = jnp.take(ys, dest[:t], axis=0)
    yb = jnp.take(ys, dest[t:], axis=0)
    return combine_norm(x, ya, yb, g_final)


def _l2_norm(t):
    return t * lax.rsqrt(jnp.sum(t * t, axis=-1, keepdims=True) + 1e-6)


def _masked_softmax(logits, valid):
    logits = jnp.where(valid, logits.astype(F32), -1e30)
    return jax.nn.softmax(logits, axis=-1) * valid


def _rel_bucket(dist):
    n = jnp.maximum(dist, 0)
    max_exact = REL_BUCKETS // 2
    nf = jnp.maximum(n, 1).astype(F32)
    large = max_exact + (jnp.log(nf / max_exact) / math.log(REL_MAX_DIST / max_exact)
                         * (REL_BUCKETS - max_exact)).astype(jnp.int32)
    large = jnp.minimum(large, REL_BUCKETS - 1)
    return jnp.where(n < max_exact, n, large)


SSM_TC = 512
SSM_SUB = 128
SSM_HALF = 256
SSM_NSTATE = SSM_GROUPS * SSM_STATE


def _ssm_body(u_ref, bw_ref, cw_ref, d_ref, glu_ref, ar_ref, ai_ref, pr_ref, pi_ref, o_ref,
              xr_ref, xi_ref, cr_ref, ci_ref):
    @pl.when(pl.program_id(1) == 0)
    def _():
        cr_ref[...] = jnp.zeros_like(cr_ref)
        ci_ref[...] = jnp.zeros_like(ci_ref)

    u = u_ref[0]
    ub = u.astype(BF16)
    nblk = SSM_WIDTH // SSM_HALF
    sblk = SSM_NSTATE // nblk
    for k in range(nblk):
        bu = jnp.dot(ub[:, k * SSM_HALF:(k + 1) * SSM_HALF], bw_ref[k], preferred_element_type=F32)
        xr_ref[:, k * sblk:(k + 1) * sblk] = bu[:, :sblk]
        xi_ref[:, k * sblk:(k + 1) * sblk] = bu[:, sblk:]

    row = lax.broadcasted_iota(jnp.int32, (SSM_SUB, LANES), 0)
    n_steps = SSM_SUB.bit_length() - 1

    def shift(x, d):
        if d % 8 == 0:
            return jnp.concatenate([jnp.zeros((d, LANES), F32), x[:SSM_SUB - d]], axis=0)
        return jnp.where(row >= d, pltpu.roll(x, d, 0), 0.0)

    def strip(c, carry):
        col = pl.ds(pl.multiple_of(c * LANES, LANES), LANES)
        c_r = cr_ref[:, col]
        c_i = ci_ref[:, col]
        for sub in range(SSM_TC // SSM_SUB):
            rows = slice(sub * SSM_SUB, (sub + 1) * SSM_SUB)
            xr = xr_ref[rows, col]
            xi = xi_ref[rows, col]
            for i in range(n_steps):
                a_r = ar_ref[i:i + 1, col]
                a_i = ai_ref[i:i + 1, col]
                sr, si = shift(xr, 1 << i), shift(xi, 1 << i)
                xr, xi = xr + a_r * sr - a_i * si, xi + a_r * si + a_i * sr
            p_r = pr_ref[:, col]
            p_i = pi_ref[:, col]
            xr, xi = xr + p_r * c_r - p_i * c_i, xi + p_r * c_i + p_i * c_r
            xr_ref[rows, col] = xr
            xi_ref[rows, col] = xi
            c_r = xr[SSM_SUB - 1:SSM_SUB, :]
            c_i = xi[SSM_SUB - 1:SSM_SUB, :]
        cr_ref[:, col] = c_r
        ci_ref[:, col] = c_i
        return carry

    lax.fori_loop(0, SSM_NSTATE // LANES, strip, 0)

    ys = []
    for k in range(nblk):
        st = jnp.concatenate([xr_ref[:, k * sblk:(k + 1) * sblk].astype(BF16),
                              xi_ref[:, k * sblk:(k + 1) * sblk].astype(BF16)], axis=1)
        ys.append(jnp.dot(st, cw_ref[k], preferred_element_type=F32))
    y = jax.nn.gelu(jnp.concatenate(ys, axis=1) + d_ref[...] * u)
    o_ref[0] = y * jax.nn.sigmoid(jnp.dot(y.astype(BF16), glu_ref[...], preferred_element_type=F32))


def ssm_mixer(u, a_re, a_im, log_dt, b_re, b_im, c_re, c_im, d_skip, glu_w):
    bs, s, _ = u.shape
    assert s % SSM_TC == 0
    nblk = SSM_WIDTH // SSM_HALF
    gpb = SSM_GROUPS // nblk
    dt = jnp.exp(log_dt)[:, None]
    mag = jnp.exp(a_re * dt)
    abar_r, abar_i = mag * jnp.cos(a_im * dt), mag * jnp.sin(a_im * dt)
    nr, ni = abar_r - 1.0, abar_i
    den = a_re * a_re + a_im * a_im
    sr, si = (nr * a_re + ni * a_im) / den, (ni * a_re - nr * a_im) / den
    bbar_r = sr[..., None] * b_re - si[..., None] * b_im
    bbar_i = sr[..., None] * b_im + si[..., None] * b_re

    def powers(k):
        kk = k.astype(F32)[:, None, None]
        m = jnp.exp(kk * (a_re * dt))
        return ((m * jnp.cos(kk * (a_im * dt))).reshape(-1, SSM_NSTATE),
                (m * jnp.sin(kk * (a_im * dt))).reshape(-1, SSM_NSTATE))

    ar, ai = powers(2 ** jnp.arange(8))
    pr, pi = powers(jnp.arange(1, SSM_SUB + 1))
    eye = jnp.eye(gpb, dtype=F32)

    def in_block(w):
        return jnp.einsum('gnp,gh->gphn', w, eye).reshape(gpb * SSM_GROUP, gpb * SSM_STATE)

    def out_block(w):
        return jnp.einsum('gpn,gh->gnhp', w, eye).reshape(gpb * SSM_STATE, gpb * SSM_GROUP)

    bw = jnp.stack([jnp.concatenate([in_block(bbar_r[k * gpb:(k + 1) * gpb]),
                                     in_block(bbar_i[k * gpb:(k + 1) * gpb])], axis=1)
                    for k in range(nblk)]).astype(BF16)
    cw = jnp.stack([jnp.concatenate([out_block(c_re[k * gpb:(k + 1) * gpb]),
                                     -out_block(c_im[k * gpb:(k + 1) * gpb])], axis=0)
                    for k in range(nblk)]).astype(BF16)
    fixed2 = lambda b, i: (0, 0)
    fixed3 = lambda b, i: (0, 0, 0)
    return pl.pallas_call(
        _ssm_body,
        grid=(bs, s // SSM_TC),
        in_specs=[pl.BlockSpec((1, SSM_TC, SSM_WIDTH), lambda b, i: (b, i, 0)),
                  pl.BlockSpec(bw.shape, fixed3), pl.BlockSpec(cw.shape, fixed3),
                  pl.BlockSpec((1, SSM_WIDTH), fixed2), pl.BlockSpec((SSM_WIDTH, SSM_WIDTH), fixed2),
                  pl.BlockSpec((8, SSM_NSTATE), fixed2), pl.BlockSpec((8, SSM_NSTATE), fixed2),
                  pl.BlockSpec((SSM_SUB, SSM_NSTATE), fixed2), pl.BlockSpec((SSM_SUB, SSM_NSTATE), fixed2)],
        out_specs=pl.BlockSpec((1, SSM_TC, SSM_WIDTH), lambda b, i: (b, i, 0)),
        out_shape=jax.ShapeDtypeStruct((bs, s, SSM_WIDTH), F32),
        scratch_shapes=[pltpu.VMEM((SSM_TC, SSM_NSTATE), F32), pltpu.VMEM((SSM_TC, SSM_NSTATE), F32),
                        pltpu.VMEM((1, SSM_NSTATE), F32), pltpu.VMEM((1, SSM_NSTATE), F32)],
        compiler_params=_params("parallel", "arbitrary"),
        name="ssm_scan",
    )(u, bw, cw, d_skip.reshape(1, SSM_WIDTH), glu_w.astype(BF16), ar, ai, pr, pi)


def _ssm_mixer_jax(u, a_re, a_im, log_dt, b_re, b_im, c_re, c_im, d_skip, glu_w):
    bs, s, _ = u.shape
    uf = u.reshape(bs, s, SSM_GROUPS, SSM_GROUP)
    dt = jnp.exp(log_dt)[:, None]
    mag = jnp.exp(a_re * dt)
    abar_r, abar_i = mag * jnp.cos(a_im * dt), mag * jnp.sin(a_im * dt)
    nr, ni = abar_r - 1.0, abar_i
    den = a_re * a_re + a_im * a_im
    sr, si = (nr * a_re + ni * a_im) / den, (ni * a_re - nr * a_im) / den
    bbar_r = sr[..., None] * b_re - si[..., None] * b_im
    bbar_i = sr[..., None] * b_im + si[..., None] * b_re
    bu_r = jnp.einsum('bsgp,gnp->bsgn', uf, bbar_r)
    bu_i = jnp.einsum('bsgp,gnp->bsgn', uf, bbar_i)
    ar = jnp.broadcast_to(abar_r, bu_r.shape)
    ai = jnp.broadcast_to(abar_i, bu_r.shape)

    def combine(e1, e2):
        a1r, a1i, b1r, b1i = e1
        a2r, a2i, b2r, b2i = e2
        return (a2r * a1r - a2i * a1i, a2r * a1i + a2i * a1r,
                a2r * b1r - a2i * b1i + b2r, a2r * b1i + a2i * b1r + b2i)

    _, _, st_r, st_i = lax.associative_scan(combine, (ar, ai, bu_r, bu_i), axis=1)
    y = (jnp.einsum('gpn,bsgn->bsgp', c_re, st_r) - jnp.einsum('gpn,bsgn->bsgp', c_im, st_i)
         + d_skip.reshape(SSM_GROUPS, SSM_GROUP) * uf)
    y = jax.nn.gelu(y.reshape(bs, s, SSM_WIDTH))
    return y * jax.nn.sigmoid(y @ glu_w)


GDN_TC = 256
GDN_HALO = 8
MASKED = -1e30


def _dot_nt(a, b):
    return lax.dot_general(a, b, (((1,), (1,)), ((), ())), preferred_element_type=F32)


def _dot_tn(a, b):
    return lax.dot_general(a, b, (((0,), (0,)), ((), ())), preferred_element_type=F32)


def _split_bf16(x):
    hi = x.astype(BF16)
    return hi, (x - hi.astype(F32)).astype(BF16)


def _gdn_body(nega_ref, dtb_ref, q_ref, k_ref, v_ref, z_ref, sm_ref, wq_ref, wk_ref, wv_ref, ng_ref,
              o_ref, state_ref, hq_ref, hk_ref, hv_ref, vnew_ref):
    TC, CH, DK = GDN_TC, GDN_CHUNK, GDN_DK
    h = pl.program_id(1)

    @pl.when(pl.program_id(2) == 0)
    def _():
        state_ref[...] = jnp.zeros_like(state_ref)
        hq_ref[...] = jnp.zeros_like(hq_ref)
        hk_ref[...] = jnp.zeros_like(hk_ref)
        hv_ref[...] = jnp.zeros_like(hv_ref)

    def conv_silu(x_ref, halo_ref, w_ref):
        x = x_ref[...]
        xx = jnp.concatenate([halo_ref[...], x], axis=0)
        w = w_ref[...]
        y = w[GDN_CONV - 1:GDN_CONV] * x
        for d in range(1, GDN_CONV):
            y = y + w[GDN_CONV - 1 - d:GDN_CONV - d] * pltpu.roll(xx, d, 0)[GDN_HALO:GDN_HALO + TC]
        halo_ref[...] = x[TC - GDN_HALO:TC]
        return y * jax.nn.sigmoid(y)

    q = conv_silu(q_ref, hq_ref, wq_ref)
    k = conv_silu(k_ref, hk_ref, wk_ref)
    v = conv_silu(v_ref, hv_ref, wv_ref)
    q = q * lax.rsqrt(jnp.sum(q * q, axis=-1, keepdims=True) + 1e-6) * (DK ** -0.5)
    k = k * lax.rsqrt(jnp.sum(k * k, axis=-1, keepdims=True) + 1e-6)

    small = sm_ref[...]
    lane_s = lax.broadcasted_iota(jnp.int32, small.shape, 1)
    a_col = jnp.sum(jnp.where(lane_s == h, small, 0.0), axis=-1, keepdims=True)
    b_col = jnp.sum(jnp.where(lane_s == GDN_HEADS + h, small, 0.0), axis=-1, keepdims=True)
    beta = jax.nn.sigmoid(b_col)
    xa = a_col + dtb_ref[h]
    softplus = jnp.maximum(xa, 0.0) + jnp.log(1.0 + jnp.exp(-jnp.abs(xa)))
    la = jnp.broadcast_to(nega_ref[h] * softplus, (TC, LANES))

    ri = lax.broadcasted_iota(jnp.int32, (TC, TC), 0)
    ci = lax.broadcasted_iota(jnp.int32, (TC, TC), 1)
    same = (ri // CH) == (ci // CH)
    causal = jnp.where(same, jnp.where(ci <= ri, 1.0, 0.0), 0.0)
    strict = jnp.where(ci < ri, causal, 0.0)
    tri = causal.astype(BF16)
    la_hi, la_lo = _split_bf16(la)
    g = (jnp.dot(tri, la_hi, preferred_element_type=F32)
         + jnp.dot(tri, la_lo, preferred_element_type=F32))
    g_hi = g.astype(BF16).astype(F32)
    g_lo = g - g_hi
    lane = lax.broadcasted_iota(jnp.int32, (TC, LANES), 1)
    lhs = jnp.where(lane == 0, g_hi, jnp.where(lane == 1, g_lo, jnp.where(lane < 4, 1.0, 0.0))).astype(BF16)
    rhs = jnp.where(lane < 2, 1.0, jnp.where(lane == 2, -g_hi, jnp.where(lane == 3, -g_lo, 0.0))).astype(BF16)
    decay = jnp.exp(jnp.where(causal > 0.5, _dot_nt(lhs, rhs), MASKED))

    kb = k * beta
    k_b = k.astype(BF16)
    lmat = strict * _dot_nt(kb.astype(BF16), k_b) * decay
    eye = jnp.where(ri == ci, 1.0, 0.0)
    tinv = eye - lmat
    pw = lmat
    for _ in range(CH.bit_length() - 2):
        pb = pw.astype(BF16)
        pw = jnp.dot(pb, pb, preferred_element_type=F32)
        tinv = tinv + jnp.dot(tinv.astype(BF16), pw.astype(BF16), preferred_element_type=F32)
    eg = jnp.exp(g)
    rhs_all = jnp.concatenate([v * beta, kb * eg], axis=1).astype(BF16)
    sol = jnp.dot(tinv.astype(BF16), rhs_all, preferred_element_type=F32)
    u_val, w_val = sol[:, :GDN_DV], sol[:, GDN_DV:]
    attn = (causal * _dot_nt(q.astype(BF16), k_b) * decay).astype(BF16)
    q_st = (q * eg).astype(BF16)

    vnew_ref[...] = jnp.zeros_like(vnew_ref)
    for c in range(TC // CH):
        rows = slice(c * CH, (c + 1) * CH)
        g_last = g[(c + 1) * CH - 1:(c + 1) * CH, :]
        st = state_ref[...]
        st_b = st.astype(BF16)
        v_new = u_val[rows] - jnp.dot(w_val[rows].astype(BF16), st_b, preferred_element_type=F32)
        vnew_ref[rows, :] = v_new.astype(BF16)
        o_c = (jnp.dot(q_st[rows], st_b, preferred_element_type=F32)
               + jnp.dot(attn[rows], vnew_ref[...], preferred_element_type=F32))
        k_st = (k[rows] * jnp.exp(g_last - g[rows])).astype(BF16)
        state_ref[...] = st * jnp.exp(g_last[:, :1]) + _dot_tn(k_st, v_new.astype(BF16))
        o_c = o_c * lax.rsqrt(jnp.mean(o_c * o_c, axis=-1, keepdims=True) + RMS_EPS) * ng_ref[...]
        zc = z_ref[rows, :]
        o_ref[rows, :] = o_c * (zc * jax.nn.sigmoid(zc))


def gdn_mixer(proj, bs, conv_w, a_log, dt_bias, norm_g):
    t = proj.shape[0]
    s = t // bs
    nt = s // GDN_TC
    assert s % GDN_TC == 0 and GDN_DK == LANES and GDN_DV == LANES and OFF_QKV == 0
    H = GDN_HEADS
    cw = jnp.zeros((8, GDN_QKV), F32).at[:GDN_CONV].set(conv_w)
    tok = lambda off: pl.BlockSpec((GDN_TC, LANES), lambda b, h, i, *_: (b * nt + i, off + h))
    wspec = lambda off: pl.BlockSpec((8, LANES), lambda b, h, i, *_: (0, off + h))
    grid_spec = pltpu.PrefetchScalarGridSpec(
        num_scalar_prefetch=0,
        grid=(bs, H, nt),
        in_specs=[pl.BlockSpec(memory_space=pltpu.SMEM), pl.BlockSpec(memory_space=pltpu.SMEM),
                  tok(0), tok(H), tok(2 * H), tok(OFF_Z // LANES),
                  pl.BlockSpec((GDN_TC, 2 * LANES), lambda b, h, i, *_: (b * nt + i, OFF_SMALL // (2 * LANES))),
                  wspec(0), wspec(H), wspec(2 * H),
                  pl.BlockSpec((1, GDN_DV), lambda b, h, i, *_: (0, 0))],
        out_specs=pl.BlockSpec((GDN_TC, GDN_DV), lambda b, h, i, *_: (b * nt + i, h)),
        scratch_shapes=[pltpu.VMEM((GDN_DK, GDN_DV), F32)] + [pltpu.VMEM((GDN_HALO, LANES), F32)] * 3
        + [pltpu.VMEM((GDN_TC, GDN_DV), BF16)],
    )
    return pl.pallas_call(
        _gdn_body,
        grid_spec=grid_spec,
        out_shape=jax.ShapeDtypeStruct((t, H * GDN_DV), F32),
        compiler_params=_params("parallel", "parallel", "arbitrary"),
        name="gdn_delta",
    )(-jnp.exp(a_log), dt_bias.astype(F32), proj, proj, proj, proj, proj, cw, cw, cw,
      norm_g.reshape(1, GDN_DV))


def _gdn_mixer_jax(qkv, a, b, z, conv_w, a_log, dt_bias, norm_g):
    bs, s, c = qkv.shape
    H, DK, DV, CH = GDN_HEADS, GDN_DK, GDN_DV, GDN_CHUNK
    qkv = lax.conv_general_dilated(qkv, conv_w[:, None, :], window_strides=(1,),
                                   padding=[(GDN_CONV - 1, 0)],
                                   dimension_numbers=('NWC', 'WIO', 'NWC'), feature_group_count=c)
    qkv = jax.nn.silu(qkv)
    q, k, v = jnp.split(qkv, [H * DK, 2 * H * DK], axis=-1)
    q = _l2_norm(q.reshape(bs, s, H, DK)) * (DK ** -0.5)
    k = _l2_norm(k.reshape(bs, s, H, DK))
    v = v.reshape(bs, s, H, DV)
    beta = jax.nn.sigmoid(b)
    log_alpha = -jnp.exp(a_log) * jax.nn.softplus(a + dt_bias)
    nc = s // CH

    def chunks(t):
        return t.reshape(bs, nc, CH, H, -1).transpose(0, 3, 1, 2, 4)

    q, k, v = chunks(q), chunks(k), chunks(v)
    beta = chunks(beta[..., None])
    g = jnp.cumsum(chunks(log_alpha[..., None]), axis=3)
    gv = g[..., 0]
    causal = jnp.tril(jnp.ones((CH, CH), bool))
    strict = jnp.tril(jnp.ones((CH, CH), bool), -1)
    decay = jnp.exp(jnp.where(causal, gv[..., :, None] - gv[..., None, :], -jnp.inf))
    kb = k * beta
    a_mat = (jnp.where(strict, jnp.einsum('bhncd,bhnkd->bhnck', kb, k) * decay, 0.0)
             + jnp.eye(CH, dtype=F32))
    rhs = jnp.concatenate([v * beta, kb * jnp.exp(g)], axis=-1)
    sol = lax.linalg.triangular_solve(a_mat, rhs, left_side=True, lower=True, unit_diagonal=True)
    u_val, w = sol[..., :DV], sol[..., DV:]
    attn = jnp.einsum('bhncd,bhnkd->bhnck', q, k) * decay
    g_last = g[..., -1:, :]
    k_st = k * jnp.exp(g_last - g)
    q_st = q * jnp.exp(g)

    def step(state, xs):
        q_c, k_c, u_c, w_c, a_c, gl_c = xs
        v_new = u_c - jnp.einsum('bhcd,bhde->bhce', w_c, state)
        o = jnp.einsum('bhcd,bhde->bhce', q_c, state) + jnp.einsum('bhck,bhke->bhce', a_c, v_new)
        state = state * jnp.exp(gl_c) + jnp.einsum('bhcd,bhce->bhde', k_c, v_new)
        return state, o

    xs = tuple(jnp.moveaxis(t, 2, 0) for t in (q_st, k_st, u_val, w, attn, g_last))
    _, o = lax.scan(step, jnp.zeros((bs, H, DK, DV), F32), xs)
    o = o.transpose(1, 0, 3, 2, 4).reshape(bs, s, H, DV)
    o = o * lax.rsqrt(jnp.mean(o * o, axis=-1, keepdims=True) + RMS_EPS) * norm_g
    o = o * jax.nn.silu(z.reshape(bs, s, H, DV))
    return o.reshape(bs, s, H * DV)


NSA_KT = 256
NSA_NEG = MASKED
NSA_WTILES = NSA_WINDOW // LANES + 1


def _bucket_table():
    n = np.arange(LANES)
    max_exact = REL_BUCKETS // 2
    nf = np.maximum(n, 1).astype(np.float32)
    large = max_exact + (np.log(nf / np.float32(max_exact)) / np.float32(math.log(REL_MAX_DIST / max_exact))
                         * np.float32(REL_BUCKETS - max_exact)).astype(np.int32)
    tbl = np.where(n < max_exact, n, np.minimum(large, REL_BUCKETS - 1))
    assert tbl[-1] == REL_BUCKETS - 1 and REL_MAX_DIST <= LANES
    return tbl


def _compress_body(x_ref, pe_ref, w1_ref, w2_ref, o_ref):
    x = x_ref[0, 0, 0]
    w1 = w1_ref[0]
    half = NSA_CMP_STRIDE * NSA_DH
    nc = x.shape[0]
    a = jnp.dot(x, w1[:half], preferred_element_type=F32)
    b = jnp.dot(x, w1[half:], preferred_element_type=F32)
    pe_h = jnp.dot(pe_ref[0], w1, preferred_element_type=F32)
    hid = a + pltpu.roll(b, nc - 1, 0) + pe_h[0:1, :]
    o_ref[0, 0, 0] = jnp.dot(jax.nn.gelu(hid).astype(BF16), w2_ref[0], preferred_element_type=F32)


def nsa_compress(x, pe, w1, w2):
    _, bs, g, nc, width = x.shape
    return pl.pallas_call(
        _compress_body,
        grid=(2, bs, g),
        in_specs=[pl.BlockSpec((1, 1, 1, nc, width), lambda i, b, j: (i, b, j, 0, 0)),
                  pl.BlockSpec((1,) + pe.shape[1:], lambda i, b, j: (i, 0, 0)),
                  pl.BlockSpec((1,) + w1.shape[1:], lambda i, b, j: (i, 0, 0)),
                  pl.BlockSpec((1,) + w2.shape[1:], lambda i, b, j: (i, 0, 0))],
        out_specs=pl.BlockSpec((1, 1, 1, nc, NSA_DH), lambda i, b, j: (i, b, j, 0, 0)),
        out_shape=jax.ShapeDtypeStruct((2, bs, g, nc, NSA_DH), F32),
        compiler_params=_params("parallel", "parallel", "parallel"),
        name="nsa_compress",
    )(x, pe, w1, w2)


def _nsa_body(nfast, wstd, near_w, near_c, q_ref, gl_ref, qp_ref, kp_ref, cp_ref,
              ks_ref, vs_ref, kw_ref, vw_ref, kc_ref, vc_ref, agg_ref, td_ref, o_ref,
              m_ref, l_ref, acc_ref, sel_ref, *, n_sel, sel_k):
    QB, HG = NSA_QBLOCK, NSA_HPG
    g = pl.program_id(1)
    qi = pl.program_id(2)
    s0 = qi * QB
    nc = kc_ref.shape[2]
    n_ct = nc // LANES
    q = q_ref[0, 0, 0]
    qp = qp_ref[...]

    def lanes4(x):
        return jnp.concatenate([x] * HG, axis=1)

    def delta_t(kp):
        idx = jnp.clip(_dot_nt(kp, qp), 0.0, float(LANES - 1)).astype(jnp.int32)
        outs = []
        for hg in range(HG):
            tb = jnp.broadcast_to(td_ref[pl.ds(g * HG + hg, 1), :], idx.shape)
            outs.append(jnp.take_along_axis(tb, idx, axis=1))
        return jnp.concatenate(outs, axis=1)

    def maybe_delta(flag, kp):
        return lax.cond(flag != 0, lambda: delta_t(kp),
                        lambda: jnp.zeros((kp.shape[0], HG * QB), F32))

    gate = jax.nn.sigmoid(gl_ref[0, 0, 0])

    sc = _dot_nt(kc_ref[0, 0], q)
    sc = sc + jnp.concatenate(
        [maybe_delta(near_c[qi * n_ct + ct], cp_ref[ct * LANES:(ct + 1) * LANES, :])
         for ct in range(n_ct)], axis=0)
    c_id = lax.broadcasted_iota(jnp.int32, (nc, QB), 0)
    c_q = lax.broadcasted_iota(jnp.int32, (nc, QB), 1)
    ok_c = lanes4(jnp.where(c_id * NSA_CMP_STRIDE + (NSA_CMP_LEN - 1) <= s0 + c_q, 1.0, 0.0))
    sc = jnp.where(ok_c > 0.5, sc, NSA_NEG)
    e = jnp.exp(sc - jnp.max(sc, axis=0, keepdims=True)) * ok_c
    p = e * (1.0 / jnp.maximum(jnp.sum(e, axis=0, keepdims=True), 1e-30))
    out = gate[0:1, :] * jnp.dot(vc_ref[0, 0], p.astype(BF16), preferred_element_type=F32)
    ps = p[:, 0:QB]
    for hg in range(1, HG):
        ps = ps + p[:, hg * QB:(hg + 1) * QB]
    ps_hi, ps_lo = _split_bf16(ps)
    imp = (jnp.dot(agg_ref[...], ps_hi, preferred_element_type=F32)
           + jnp.dot(agg_ref[...], ps_lo, preferred_element_type=F32))
    blk = lax.broadcasted_iota(jnp.int32, (LANES, QB), 0)
    t_tok = s0 + lax.broadcasted_iota(jnp.int32, (LANES, QB), 1)
    tb_blk = t_tok // NSA_SEL_LEN
    forced = jnp.where(blk == 0, 1.0, 0.0) + jnp.where(blk == tb_blk, 1.0, 0.0) \
        + jnp.where(blk == tb_blk - 1, 1.0, 0.0)
    score = jnp.where(forced > 0.5, 1e9, jnp.where(blk * NSA_SEL_LEN <= t_tok, imp, -1e9))
    score = jnp.where(blk < n_sel, score, -jnp.inf)
    blkf = blk.astype(F32)
    sel = jnp.zeros((LANES, QB), F32)
    for _ in range(sel_k):
        mx = jnp.max(score, axis=0, keepdims=True)
        first = jnp.min(jnp.where(score == mx, blkf, float(LANES)), axis=0, keepdims=True)
        hit = blkf == first
        sel = jnp.where(hit, 1.0, sel)
        score = jnp.where(hit, -jnp.inf, score)
    sel_ref[...] = jnp.where(sel > 0.5, 0.0, NSA_NEG)

    def reset():
        m_ref[...] = jnp.full_like(m_ref, NSA_NEG)
        l_ref[...] = jnp.zeros_like(l_ref)
        acc_ref[...] = jnp.zeros_like(acc_ref)

    def tile_step(s, v_t, k):
        m_old = m_ref[k]
        m_new = jnp.maximum(m_old, jnp.max(s, axis=0, keepdims=True))
        alpha = jnp.exp(m_old - m_new)
        p = jnp.exp(s - m_new)
        l_ref[k] = l_ref[k] * alpha + jnp.sum(p, axis=0, keepdims=True)
        acc_ref[k] = acc_ref[k] * alpha + jnp.dot(v_t, p.astype(BF16), preferred_element_type=F32)
        m_ref[k] = m_new

    def finish():
        m = jnp.maximum(m_ref[0], m_ref[1])
        w0, w1 = jnp.exp(m_ref[0] - m), jnp.exp(m_ref[1] - m)
        return (acc_ref[0] * w0 + acc_ref[1] * w1) * (1.0 / (l_ref[0] * w0 + l_ref[1] * w1))

    per_tile = NSA_KT // NSA_SEL_LEN
    key_r = lax.broadcasted_iota(jnp.int32, (NSA_KT, QB), 0)
    key_q = lax.broadcasted_iota(jnp.int32, (NSA_KT, QB), 1)

    def sel_scores(kt, with_delta, diag):
        k0 = pl.multiple_of(kt * NSA_KT, NSA_KT)
        s = _dot_nt(ks_ref[0, 0, pl.ds(k0, NSA_KT), :], q)
        if with_delta:
            s = s + delta_t(kp_ref[pl.ds(k0, NSA_KT), :])
        mask = jnp.concatenate(
            [jnp.broadcast_to(sel_ref[pl.ds(kt * per_tile + i, 1), :], (NSA_SEL_LEN, QB))
             for i in range(per_tile)], axis=0)
        if diag:
            mask = jnp.where(k0 + key_r <= s0 + key_q, mask, NSA_NEG)
        return s + lanes4(mask), vs_ref[0, 0, :, pl.ds(k0, NSA_KT)]

    reset()
    n_past = (s0 + QB - 1) // NSA_KT
    n_pairs = nfast[qi] // 2

    def fast_body(i, carry):
        a = sel_scores(2 * i, False, False)
        b = sel_scores(2 * i + 1, False, False)
        tile_step(*a, 0)
        tile_step(*b, 1)
        return carry

    def slow_body(kt, carry):
        tile_step(*sel_scores(kt, True, False), 1)
        return carry

    lax.fori_loop(0, n_pairs, fast_body, 0)
    lax.fori_loop(2 * n_pairs, n_past, slow_body, 0)
    tile_step(*sel_scores(n_past, True, True), 0)
    out = out + gate[1:2, :] * finish()

    reset()
    w_r = lax.broadcasted_iota(jnp.int32, (LANES, QB), 0)
    w_q = lax.broadcasted_iota(jnp.int32, (LANES, QB), 1)
    in_window = lanes4(jnp.where(w_r > w_q, 0.0, NSA_NEG))
    causal_w = lanes4(jnp.where(w_r <= w_q, 0.0, NSA_NEG))

    def win_scores(st, n):
        st = pl.multiple_of(st, LANES)
        return _dot_nt(kw_ref[0, 0, pl.ds(st, n), :], q), vw_ref[0, 0, :, pl.ds(st, n)]

    @pl.when(wstd[qi] != 0)
    def _():
        half = NSA_WINDOW // 2
        zeros = jnp.zeros((LANES, HG * QB), F32)
        sa, va = win_scores(s0 - NSA_WINDOW, half)
        sb, vb = win_scores(s0 - half, half)
        sd, vd = win_scores(s0, LANES)
        tile_step(sa + jnp.concatenate([in_window, zeros], axis=0), va, 0)
        near = delta_t(kp_ref[pl.ds(pl.multiple_of(s0 - LANES, LANES), LANES), :])
        tile_step(sb + jnp.concatenate([zeros, near], axis=0), vb, 1)
        tile_step(sd + delta_t(kp_ref[pl.ds(pl.multiple_of(s0, LANES), LANES), :]) + causal_w, vd, 0)

    for j in range(NSA_WTILES):
        start = s0 - NSA_WINDOW + j * LANES

        @pl.when((wstd[qi] == 0) & (start >= 0))
        def _():
            s, v_t = win_scores(start, LANES)
            s = s + maybe_delta(near_w[qi * NSA_WTILES + j],
                                kp_ref[pl.ds(pl.multiple_of(start, LANES), LANES), :])
            if j == 0:
                s = s + in_window
            elif j == NSA_WTILES - 1:
                s = s + causal_w
            tile_step(s, v_t, j % 2)

    o_ref[0, 0, 0] = out + gate[2:3, :] * finish()


def _pos_digits(p, key_side):
    d0, d1, d2 = (p & 127).astype(F32), ((p >> 7) & 127).astype(F32), (p >> 14).astype(F32)
    one = jnp.ones_like(d0)
    if key_side:
        cols = [one, one, one, -d2, -d1, -d0]
    else:
        cols = [16384.0 * d2, 128.0 * d1, d0, 16384.0 * one, 128.0 * one, one]
    out = jnp.stack(cols, axis=-1)
    return jnp.pad(out, ((0, 0), (0, LANES - out.shape[-1]))).astype(BF16)


def nsa_mixer(q, k_c, v_c, k_s, v_s, k_w, v_w, gate_logits, positions,
              ck_pe, ck_w1, ck_w2, cv_pe, cv_w1, cv_w2, rel_bias):
    bs, s, _ = q.shape
    G, HG, DH, QB = NSA_KV_GROUPS, NSA_HPG, NSA_DH, NSA_QBLOCK
    nqt = s // QB
    nc = s // NSA_CMP_STRIDE
    n_cmp = (s - NSA_CMP_LEN) // NSA_CMP_STRIDE + 1
    n_sel = s // NSA_SEL_LEN
    sel_k = min(NSA_SEL_TOPK, n_sel)
    n_kt = s // NSA_KT
    n_ct = nc // LANES
    assert s % (LANES * NSA_CMP_STRIDE) == 0 and n_sel <= LANES and NSA_KT == 2 * QB

    def by_group(t):
        return t.reshape(bs, s, G, DH).transpose(0, 2, 1, 3)

    def by_group_t(t):
        return t.reshape(bs, s, G, DH).transpose(0, 2, 3, 1).astype(BF16)

    ones2 = jnp.zeros((LANES - DH,), F32).at[:2].set(1.0)

    def keys_aug(t):
        return jnp.concatenate([t, jnp.broadcast_to(ones2, t.shape[:-1] + (LANES - DH,))],
                               axis=-1).astype(BF16)

    chunks = jnp.stack([by_group(k_c), by_group(v_c)]).reshape(2, bs, G, nc, NSA_CMP_STRIDE * DH)
    pe = jnp.stack([ck_pe, cv_pe]).reshape(2, 1, NSA_CMP_LEN * DH)
    cmp = nsa_compress(chunks.astype(BF16), jnp.broadcast_to(pe, (2, 8, NSA_CMP_LEN * DH)).astype(BF16),
                       jnp.stack([ck_w1, cv_w1]).astype(BF16), jnp.stack([ck_w2, cv_w2]).astype(BF16))
    kc, vc_t = keys_aug(cmp[0]), cmp[1].transpose(0, 1, 3, 2).astype(BF16)

    far = rel_bias[REL_BUCKETS - 1].astype(F32)
    far_hi = far.astype(BF16).astype(F32)
    extra = jnp.zeros((NSA_HEADS, LANES - DH), F32).at[:, 0].set(far_hi).at[:, 1].set(far - far_hi)
    qh = (q * (DH ** -0.5)).reshape(bs, nqt, QB, G, HG, DH).transpose(0, 3, 1, 4, 2, 5)
    ex = jnp.broadcast_to(extra.reshape(1, G, 1, HG, 1, LANES - DH), (bs, G, nqt, HG, QB, LANES - DH))
    qs = jnp.concatenate([qh, ex], axis=-1).astype(BF16).reshape(bs, G, nqt, HG * QB, LANES)
    gl = gate_logits.reshape(bs, nqt, QB, G, HG, 3).transpose(0, 3, 1, 5, 4, 2)
    gl = jnp.pad(gl.reshape(bs, G, nqt, 3, HG * QB), ((0, 0),) * 3 + ((0, 5), (0, 0)))

    td = (rel_bias[_bucket_table()] - rel_bias[REL_BUCKETS - 1][None, :]).T.astype(F32)
    cmp_end = jnp.minimum(jnp.arange(nc) * NSA_CMP_STRIDE + NSA_CMP_LEN - 1, s - 1)
    cpos = positions[cmp_end]
    qmin = positions.reshape(nqt, QB).min(axis=1)
    kmax = positions.reshape(nqt, QB).max(axis=1)
    near_s = (qmin[:, None] - positions.reshape(n_kt, NSA_KT).max(axis=1)[None, :]) < REL_MAX_DIST
    n_past = (jnp.arange(nqt) * QB + QB - 1) // NSA_KT
    n_fast = jnp.minimum(jnp.argmax(jnp.concatenate([near_s, jnp.ones((nqt, 1), bool)], axis=1), axis=1),
                         n_past)
    wt = jnp.clip(jnp.arange(nqt)[:, None] - (NSA_WTILES - 1) + jnp.arange(NSA_WTILES)[None, :], 0, nqt - 1)
    near_w = (qmin[:, None] - kmax[wt]) < REL_MAX_DIST
    near_c = (qmin[:, None] - cpos.reshape(n_ct, LANES).max(axis=1)[None, :]) < REL_MAX_DIST
    w_std = ((jnp.arange(nqt) >= NSA_WTILES - 1)
             & jnp.all(near_w == (jnp.arange(NSA_WTILES) >= NSA_WTILES - 2)[None, :], axis=1))

    cmp_start = np.arange(nc) * NSA_CMP_STRIDE
    sel_start = np.arange(LANES) * NSA_SEL_LEN
    agg_t = ((cmp_start[None, :] <= sel_start[:, None] + NSA_SEL_LEN - 1)
             & (cmp_start[None, :] + NSA_CMP_LEN - 1 >= sel_start[:, None])
             & (np.arange(nc)[None, :] < n_cmp) & (np.arange(LANES)[:, None] < n_sel))

    cols = HG * QB
    full = lambda shape: pl.BlockSpec(shape, lambda b, g, i, *_: (0,) * len(shape))
    per_bg = lambda n, w: pl.BlockSpec((1, 1, n, w), lambda b, g, i, *_: (b, g, 0, 0))
    grid_spec = pltpu.PrefetchScalarGridSpec(
        num_scalar_prefetch=4,
        grid=(bs, G, nqt),
        in_specs=[pl.BlockSpec((1, 1, 1, cols, LANES), lambda b, g, i, *_: (b, g, i, 0, 0)),
                  pl.BlockSpec((1, 1, 1, 8, cols), lambda b, g, i, *_: (b, g, i, 0, 0)),
                  pl.BlockSpec((QB, LANES), lambda b, g, i, *_: (i, 0)),
                  full((s, LANES)), full((nc, LANES)),
                  per_bg(s, LANES), per_bg(DH, s), per_bg(s, LANES), per_bg(DH, s),
                  per_bg(nc, LANES), per_bg(DH, nc),
                  full((LANES, nc)), full((NSA_HEADS, LANES))],
        out_specs=pl.BlockSpec((1, 1, 1, DH, cols), lambda b, g, i, *_: (b, g, i, 0, 0)),
        scratch_shapes=[pltpu.VMEM((2, 1, cols), F32), pltpu.VMEM((2, 1, cols), F32),
                        pltpu.VMEM((2, DH, cols), F32), pltpu.VMEM((LANES, QB), F32)],
    )
    out = pl.pallas_call(
        functools.partial(_nsa_body, n_sel=n_sel, sel_k=sel_k),
        grid_spec=grid_spec,
        out_shape=jax.ShapeDtypeStruct((bs, G, nqt, DH, cols), F32),
        compiler_params=_params("parallel", "parallel", "arbitrary"),
        name="nsa_attention",
    )(n_fast.astype(jnp.int32), w_std.astype(jnp.int32), near_w.reshape(-1).astype(jnp.int32),
      near_c.reshape(-1).astype(jnp.int32),
      qs, gl, _pos_digits(positions, False), _pos_digits(positions, True), _pos_digits(cpos, True),
      keys_aug(by_group(k_s)), by_group_t(v_s), keys_aug(by_group(k_w)), by_group_t(v_w),
      kc, vc_t, jnp.asarray(agg_t, BF16), td)
    out = out.reshape(bs, G, nqt, DH, HG, QB).transpose(0, 2, 5, 1, 4, 3)
    return out.reshape(bs, s, G * HG * DH)


def _nsa_body_rowmajor(near_s, near_w, near_c, q_ref, gl_ref, posq_ref, posk_ref, cpos_ref,
              ks_ref, vs_ref, kw_ref, vw_ref, kc_ref, vc_ref, agg_ref, td_ref, o_ref,
              s_ref, m_ref, l_ref, acc_ref, *, n_sel, sel_k):
    QB, HG = NSA_QBLOCK, NSA_HPG
    g = pl.program_id(1)
    qi = pl.program_id(2)
    s0 = qi * QB
    nc = kc_ref.shape[2]
    n_ct = nc // LANES
    n_kt = ks_ref.shape[2] // NSA_KT
    q = q_ref[0, 0, 0]
    tp = posq_ref[...]
    row = lax.broadcasted_iota(jnp.int32, (QB, LANES), 0)
    lane = lax.broadcasted_iota(jnp.int32, (QB, LANES), 1)
    t_tok = s0 + row

    def add_delta(col0, pk_row):
        idx = jnp.clip(tp - pk_row, 0, LANES - 1)
        for hg in range(HG):
            tb = jnp.broadcast_to(td_ref[pl.ds(g * HG + hg, 1), :], (QB, LANES))
            s_ref[hg * QB:(hg + 1) * QB, col0:col0 + LANES] += jnp.take_along_axis(tb, idx, axis=1)

    def heads(x):
        return jnp.concatenate([x] * HG, axis=0)

    gate = jax.nn.sigmoid(gl_ref[0, 0, 0])

    s_ref[:, :nc] = _dot_nt(q, kc_ref[0, 0])
    for ct in range(n_ct):
        @pl.when(near_c[qi * n_ct + ct] != 0)
        def _():
            add_delta(ct * LANES, cpos_ref[ct:ct + 1, :])
    c_id = lax.broadcasted_iota(jnp.int32, (QB, nc), 1)
    c_row = lax.broadcasted_iota(jnp.int32, (QB, nc), 0)
    ok_c = heads(jnp.where(c_id * NSA_CMP_STRIDE + (NSA_CMP_LEN - 1) <= s0 + c_row, 1.0, 0.0))
    sc = jnp.where(ok_c > 0.5, s_ref[:, :nc], NSA_NEG)
    e = jnp.exp(sc - jnp.max(sc, axis=-1, keepdims=True)) * ok_c
    p = e * (1.0 / jnp.maximum(jnp.sum(e, axis=-1, keepdims=True), 1e-30))
    o_c = jnp.dot(p.astype(BF16), vc_ref[0, 0], preferred_element_type=F32)
    o_ref[0, 0, 0] = gate[:, 0:1] * o_c
    ps = p[0:QB]
    for hg in range(1, HG):
        ps = ps + p[hg * QB:(hg + 1) * QB]
    ps_hi = ps.astype(BF16)
    ps_lo = (ps - ps_hi.astype(F32)).astype(BF16)
    imp = (jnp.dot(ps_hi, agg_ref[...], preferred_element_type=F32)
           + jnp.dot(ps_lo, agg_ref[...], preferred_element_type=F32))
    tb_blk = t_tok // NSA_SEL_LEN
    forced = jnp.where(lane == 0, 1.0, 0.0) + jnp.where(lane == tb_blk, 1.0, 0.0) \
        + jnp.where(lane == tb_blk - 1, 1.0, 0.0)
    score = jnp.where(forced > 0.5, 1e9, jnp.where(lane * NSA_SEL_LEN <= t_tok, imp, -1e9))
    score = jnp.where(lane < n_sel, score, -jnp.inf)
    lanef = lane.astype(F32)
    sel = jnp.zeros((QB, LANES), F32)
    for _ in range(sel_k):
        mx = jnp.max(score, axis=-1, keepdims=True)
        first = jnp.min(jnp.where(score == mx, lanef, float(LANES)), axis=-1, keepdims=True)
        hit = lanef == first
        sel = jnp.where(hit, 1.0, sel)
        score = jnp.where(hit, -jnp.inf, score)
    sel_b = sel.astype(BF16)

    def reset():
        m_ref[...] = jnp.full_like(m_ref, NSA_NEG)
        l_ref[...] = jnp.zeros_like(l_ref)
        acc_ref[...] = jnp.zeros_like(acc_ref)

    def tile_step(width, mask_add, v):
        s = s_ref[:, :width]
        if mask_add is not None:
            s = s + heads(mask_add)
        m_old = m_ref[...]
        m_new = jnp.maximum(m_old, jnp.max(s, axis=-1, keepdims=True))
        alpha = jnp.exp(m_old - m_new)
        p = jnp.exp(s - m_new)
        psum = p[:, :LANES]
        for c in range(1, width // LANES):
            psum = psum + p[:, c * LANES:(c + 1) * LANES]
        l_ref[...] = l_ref[...] * alpha + psum
        acc_ref[...] = acc_ref[...] * alpha + jnp.dot(p.astype(BF16), v, preferred_element_type=F32)
        m_ref[...] = m_new

    def finish():
        return acc_ref[...] * (1.0 / jnp.sum(l_ref[...], axis=-1, keepdims=True))

    blk_i = lax.broadcasted_iota(jnp.int32, (LANES, NSA_KT), 0)
    key_i = lax.broadcasted_iota(jnp.int32, (LANES, NSA_KT), 1)
    key_q = lax.broadcasted_iota(jnp.int32, (QB, NSA_KT), 1)
    row_q = lax.broadcasted_iota(jnp.int32, (QB, NSA_KT), 0)
    per_tile = NSA_KT // NSA_SEL_LEN

    def sel_tile(kt, diag):
        k0 = pl.multiple_of(kt * NSA_KT, NSA_KT)
        s_ref[:, :NSA_KT] = _dot_nt(q, ks_ref[0, 0, pl.ds(k0, NSA_KT), :])

        @pl.when(near_s[qi * n_kt + kt] != 0)
        def _():
            for c in range(NSA_KT // LANES):
                add_delta(c * LANES, posk_ref[pl.ds(kt * (NSA_KT // LANES) + c, 1), :])

        expand = jnp.where(blk_i == kt * per_tile + key_i // NSA_SEL_LEN, 1.0, 0.0).astype(BF16)
        picked = jnp.dot(sel_b, expand, preferred_element_type=F32)
        if diag:
            picked = jnp.where(k0 + key_q <= s0 + row_q, picked, 0.0)
        tile_step(NSA_KT, jnp.where(picked > 0.5, 0.0, NSA_NEG), vs_ref[0, 0, pl.ds(k0, NSA_KT), :])

    reset()
    n_past = (s0 + QB - 1) // NSA_KT

    def past_body(kt, carry):
        sel_tile(kt, False)
        return carry

    lax.fori_loop(0, n_past, past_body, 0)
    sel_tile(n_past, True)
    o_ref[0, 0, 0] += gate[:, 1:2] * finish()

    reset()
    for j in range(NSA_WTILES):
        start = s0 - NSA_WINDOW + j * LANES

        @pl.when(start >= 0)
        def _():
            st = pl.multiple_of(start, LANES)
            s_ref[:, :LANES] = _dot_nt(q, kw_ref[0, 0, pl.ds(st, LANES), :])

            @pl.when(near_w[qi * NSA_WTILES + j] != 0)
            def _():
                add_delta(0, posk_ref[pl.ds(qi - (NSA_WTILES - 1) + j, 1), :])

            if j == 0:
                mask_add = jnp.where(lane > row, 0.0, NSA_NEG)
            elif j == NSA_WTILES - 1:
                mask_add = jnp.where(lane <= row, 0.0, NSA_NEG)
            else:
                mask_add = None
            tile_step(LANES, mask_add, vw_ref[0, 0, pl.ds(st, LANES), :])

    o_ref[0, 0, 0] += gate[:, 2:3] * finish()


def _nsa_mixer_rowmajor(q, k_c, v_c, k_s, v_s, k_w, v_w, gate_logits, positions,
                        ck_pe, ck_w1, ck_w2, cv_pe, cv_w1, cv_w2, rel_bias):
    bs, s, _ = q.shape
    G, HG, DH, QB = NSA_KV_GROUPS, NSA_HPG, NSA_DH, NSA_QBLOCK
    nqt = s // QB
    nc = s // NSA_CMP_STRIDE
    n_cmp = (s - NSA_CMP_LEN) // NSA_CMP_STRIDE + 1
    n_sel = s // NSA_SEL_LEN
    sel_k = min(NSA_SEL_TOPK, n_sel)
    n_kt = s // NSA_KT
    n_ct = nc // LANES
    assert s % (LANES * NSA_CMP_STRIDE) == 0 and n_sel <= LANES and NSA_KT == 2 * QB

    def by_group(t):
        return t.reshape(bs, s, G, DH).transpose(0, 2, 1, 3)

    ones2 = jnp.zeros((LANES - DH,), F32).at[:2].set(1.0)

    def keys_aug(t):
        return jnp.concatenate([t, jnp.broadcast_to(ones2, t.shape[:-1] + (LANES - DH,))],
                               axis=-1).astype(BF16)

    chunks = jnp.stack([by_group(k_c), by_group(v_c)]).reshape(2, bs, G, nc, NSA_CMP_STRIDE * DH)
    pe = jnp.stack([ck_pe, cv_pe]).reshape(2, 1, NSA_CMP_LEN * DH)
    cmp = nsa_compress(chunks.astype(BF16), jnp.broadcast_to(pe, (2, 8, NSA_CMP_LEN * DH)).astype(BF16),
                       jnp.stack([ck_w1, cv_w1]).astype(BF16), jnp.stack([ck_w2, cv_w2]).astype(BF16))
    kc, vc = keys_aug(cmp[0]), cmp[1].astype(BF16)

    far = rel_bias[REL_BUCKETS - 1].astype(F32)
    far_hi = far.astype(BF16).astype(F32)
    extra = jnp.zeros((NSA_HEADS, LANES - DH), F32).at[:, 0].set(far_hi).at[:, 1].set(far - far_hi)
    qh = (q * (DH ** -0.5)).reshape(bs, nqt, QB, G, HG, DH).transpose(0, 3, 1, 4, 2, 5)
    ex = jnp.broadcast_to(extra.reshape(1, G, 1, HG, 1, LANES - DH), (bs, G, nqt, HG, QB, LANES - DH))
    qs = jnp.concatenate([qh, ex], axis=-1).astype(BF16).reshape(bs, G, nqt, HG * QB, LANES)
    gl = gate_logits.reshape(bs, nqt, QB, G, HG, 3).transpose(0, 3, 1, 4, 2, 5)
    gl = jnp.pad(gl, ((0, 0),) * 5 + ((0, 5),)).reshape(bs, G, nqt, HG * QB, 8)

    td = (rel_bias[_bucket_table()] - rel_bias[REL_BUCKETS - 1][None, :]).T.astype(F32)
    cmp_end = jnp.minimum(jnp.arange(nc) * NSA_CMP_STRIDE + NSA_CMP_LEN - 1, s - 1)
    cpos = positions[cmp_end]
    qmin = positions.reshape(nqt, QB).min(axis=1)
    kmax = positions.reshape(nqt, QB).max(axis=1)
    near_s = (qmin[:, None] - positions.reshape(n_kt, NSA_KT).max(axis=1)[None, :]) < REL_MAX_DIST
    wt = jnp.clip(jnp.arange(nqt)[:, None] - (NSA_WTILES - 1) + jnp.arange(NSA_WTILES)[None, :], 0, nqt - 1)
    near_w = (qmin[:, None] - kmax[wt]) < REL_MAX_DIST
    near_c = (qmin[:, None] - cpos.reshape(n_ct, LANES).max(axis=1)[None, :]) < REL_MAX_DIST

    cmp_start = np.arange(nc) * NSA_CMP_STRIDE
    sel_start = np.arange(LANES) * NSA_SEL_LEN
    agg = ((cmp_start[:, None] <= sel_start[None, :] + NSA_SEL_LEN - 1)
           & (cmp_start[:, None] + NSA_CMP_LEN - 1 >= sel_start[None, :])
           & (np.arange(nc)[:, None] < n_cmp) & (np.arange(LANES)[None, :] < n_sel))

    rows = HG * QB
    full = lambda shape: pl.BlockSpec(shape, lambda b, g, i, *_: (0,) * len(shape))
    per_bg = lambda n, w: pl.BlockSpec((1, 1, n, w), lambda b, g, i, *_: (b, g, 0, 0))
    grid_spec = pltpu.PrefetchScalarGridSpec(
        num_scalar_prefetch=3,
        grid=(bs, G, nqt),
        in_specs=[pl.BlockSpec((1, 1, 1, rows, LANES), lambda b, g, i, *_: (b, g, i, 0, 0)),
                  pl.BlockSpec((1, 1, 1, rows, 8), lambda b, g, i, *_: (b, g, i, 0, 0)),
                  pl.BlockSpec((QB, 1), lambda b, g, i, *_: (i, 0)),
                  full((nqt, LANES)), full((n_ct, LANES)),
                  per_bg(s, LANES), per_bg(s, DH), per_bg(s, LANES), per_bg(s, DH),
                  per_bg(nc, LANES), per_bg(nc, DH),
                  full((nc, LANES)), full((NSA_HEADS, LANES))],
        out_specs=pl.BlockSpec((1, 1, 1, rows, DH), lambda b, g, i, *_: (b, g, i, 0, 0)),
        scratch_shapes=[pltpu.VMEM((rows, max(nc, NSA_KT)), F32), pltpu.VMEM((rows, 1), F32),
                        pltpu.VMEM((rows, LANES), F32), pltpu.VMEM((rows, DH), F32)],
    )
    out = pl.pallas_call(
        functools.partial(_nsa_body, n_sel=n_sel, sel_k=sel_k),
        grid_spec=grid_spec,
        out_shape=jax.ShapeDtypeStruct((bs, G, nqt, rows, DH), F32),
        compiler_params=_params("parallel", "parallel", "arbitrary"),
        name="nsa_attention",
    )(near_s.reshape(-1).astype(jnp.int32), near_w.reshape(-1).astype(jnp.int32),
      near_c.reshape(-1).astype(jnp.int32),
      qs, gl, positions.reshape(s, 1), positions.reshape(nqt, LANES), cpos.reshape(n_ct, LANES),
      keys_aug(by_group(k_s)), by_group(v_s).astype(BF16), keys_aug(by_group(k_w)),
      by_group(v_w).astype(BF16), kc, vc, jnp.asarray(agg, BF16), td)
    out = out.reshape(bs, G, nqt, HG, QB, DH).transpose(0, 2, 4, 1, 3, 5)
    return out.reshape(bs, s, G * HG * DH)


def _nsa_mixer_jax(q, k_c, v_c, k_s, v_s, k_w, v_w, gate_logits, positions,
                   ck_pe, ck_w1, ck_w2, cv_pe, cv_w1, cv_w2, rel_bias):
    bs, s, _ = q.shape
    G, HG, DH, QB, W = NSA_KV_GROUPS, NSA_HPG, NSA_DH, NSA_QBLOCK, NSA_WINDOW
    q = q.reshape(bs, s, G, HG, DH) * (DH ** -0.5)

    def kv(t):
        return t.reshape(bs, s, G, DH)

    k_c, v_c, k_s, v_s, k_w, v_w = (kv(t) for t in (k_c, v_c, k_s, v_s, k_w, v_w))
    gates = jax.nn.sigmoid(gate_logits).reshape(bs, s, G, HG, 3)
    n_cmp = (s - NSA_CMP_LEN) // NSA_CMP_STRIDE + 1
    cmp_start = jnp.arange(n_cmp) * NSA_CMP_STRIDE
    cmp_end = cmp_start + NSA_CMP_LEN - 1
    cmp_tok = cmp_start[:, None] + jnp.arange(NSA_CMP_LEN)[None, :]

    def compress(t, pe, w1, w2):
        blk = t[:, cmp_tok] + pe[None, None, :, None, :]
        blk = blk.transpose(0, 1, 3, 2, 4).reshape(bs, n_cmp, G, NSA_CMP_LEN * DH)
        return jax.nn.gelu(blk @ w1) @ w2

    kc = compress(k_c, ck_pe, ck_w1, ck_w2)
    vc = compress(v_c, cv_pe, cv_w1, cv_w2)
    cmp_pos = positions[cmp_end]
    n_sel = s // NSA_SEL_LEN
    sel_k = min(NSA_SEL_TOPK, n_sel)
    sel_start = jnp.arange(n_sel) * NSA_SEL_LEN
    agg = ((cmp_start[:, None] <= sel_start[None, :] + NSA_SEL_LEN - 1)
           & (cmp_end[:, None] >= sel_start[None, :])).astype(F32)
    ks_blk = k_s.reshape(bs, n_sel, NSA_SEL_LEN, G, DH).transpose(0, 3, 1, 2, 4)
    vs_blk = v_s.reshape(bs, n_sel, NSA_SEL_LEN, G, DH).transpose(0, 3, 1, 2, 4)
    pos_blk = positions.reshape(n_sel, NSA_SEL_LEN)
    kw_pad = jnp.pad(k_w, ((0, 0), (W, 0), (0, 0), (0, 0)))
    vw_pad = jnp.pad(v_w, ((0, 0), (W, 0), (0, 0), (0, 0)))
    pos_pad = jnp.pad(positions, (W, 0))
    tbl = rel_bias.reshape(REL_BUCKETS, G, HG)
    b_ix = jnp.arange(bs)[:, None, None, None]
    g_ix = jnp.arange(G)[None, :, None, None]
    sel_offs = jnp.arange(NSA_SEL_LEN)
    win_offs = jnp.arange(QB + W)
    blk_j = jnp.arange(n_sel)

    def one_block(qi):
        s0 = qi * QB
        qb = lax.dynamic_slice_in_dim(q, s0, QB, axis=1)
        t = s0 + jnp.arange(QB)
        tp = lax.dynamic_slice_in_dim(positions, s0, QB)
        bias_c = tbl[_rel_bucket(tp[:, None] - cmp_pos[None, :])].transpose(2, 3, 0, 1)
        lg_c = jnp.einsum('bqghd,bcgd->bghqc', qb, kc) + bias_c
        p_c = _masked_softmax(lg_c, cmp_end[None, :] <= t[:, None])
        o_c = jnp.einsum('bghqc,bcgd->bqghd', p_c, vc)
        imp = jnp.einsum('bghqc,cj->bgqj', p_c, agg)
        tb = t // NSA_SEL_LEN
        forced = ((blk_j[None, :] == 0) | (blk_j[None, :] == tb[:, None])
                  | (blk_j[None, :] == tb[:, None] - 1))
        eligible = sel_start[None, :] <= t[:, None]
        score = jnp.where(forced, 1e9, jnp.where(eligible, imp, -1e9))
        _, sel = lax.top_k(score, sel_k)
        k_sel = ks_blk[b_ix, g_ix, sel]
        v_sel = vs_blk[b_ix, g_ix, sel]
        tok_s = sel[..., None] * NSA_SEL_LEN + sel_offs
        bias_s = tbl[_rel_bucket(tp[:, None, None] - pos_blk[sel]), g_ix[..., None]]
        lg_s = jnp.einsum('bqghd,bgqnkd->bghqnk', qb, k_sel) + bias_s.transpose(0, 1, 5, 2, 3, 4)
        valid_s = (tok_s <= t[:, None, None])[:, :, None].reshape(bs, G, 1, QB, -1)
        p_s = _masked_softmax(lg_s.reshape(bs, G, HG, QB, -1), valid_s)
        p_s = p_s.reshape(bs, G, HG, QB, sel_k, NSA_SEL_LEN)
        o_s = jnp.einsum('bghqnk,bgqnkd->bqghd', p_s, v_sel)
        kwb = lax.dynamic_slice_in_dim(kw_pad, s0, QB + W, axis=1)
        vwb = lax.dynamic_slice_in_dim(vw_pad, s0, QB + W, axis=1)
        tok_w = s0 - W + win_offs
        pos_w = lax.dynamic_slice_in_dim(pos_pad, s0, QB + W)
        d_tok = t[:, None] - tok_w[None, :]
        valid_w = (d_tok >= 0) & (d_tok < W) & (tok_w[None, :] >= 0)
        bias_w = tbl[_rel_bucket(tp[:, None] - pos_w[None, :])].transpose(2, 3, 0, 1)
        lg_w = jnp.einsum('bqghd,bkgd->bghqk', qb, kwb) + bias_w
        p_w = _masked_softmax(lg_w, valid_w)
        o_w = jnp.einsum('bghqk,bkgd->bqghd', p_w, vwb)
        gb = lax.dynamic_slice_in_dim(gates, s0, QB, axis=1)
        return gb[..., 0:1] * o_c + gb[..., 1:2] * o_s + gb[..., 2:3] * o_w

    out = lax.map(one_block, jnp.arange(s // QB))
    return out.transpose(1, 0, 2, 3, 4, 5).reshape(bs, s, G * HG * DH)


SGU_TC = 512


def _sgu_body(uv_ref, lg_ref, lb_ref, w_ref, b_ref, o_ref):
    uv = jax.nn.gelu(uv_ref[...])
    u, v = uv[:, :SGU_WIDTH], uv[:, SGU_WIDTH:]
    mu = jnp.mean(v, axis=-1, keepdims=True)
    vc = v - mu
    var = jnp.mean(vc * vc, axis=-1, keepdims=True)
    vn = (vc * lax.rsqrt(var + RMS_EPS) * lg_ref[...] + lb_ref[...]).astype(BF16)
    gw = SGU_WIDTH // SGU_GROUPS
    for c in range(SGU_TC // SGU_CHUNK):
        rows = slice(c * SGU_CHUNK, (c + 1) * SGU_CHUNK)
        for g in range(SGU_GROUPS):
            cols = slice(g * gw, (g + 1) * gw)
            mix = jnp.dot(w_ref[g], vn[rows, cols], preferred_element_type=F32) + b_ref[:, g:g + 1]
            o_ref[rows, cols] = u[rows, cols] * mix


def sgu_mixer(proj, ln_g, ln_b, w_s, b_s):
    t = proj.shape[0]
    assert t % SGU_TC == 0 and OFF_SGU % (2 * SGU_WIDTH) == 0
    fixed = lambda i: (0, 0)
    return pl.pallas_call(
        _sgu_body,
        grid=(t // SGU_TC,),
        in_specs=[pl.BlockSpec((SGU_TC, 2 * SGU_WIDTH), lambda i: (i, OFF_SGU // (2 * SGU_WIDTH))),
                  pl.BlockSpec((1, SGU_WIDTH), fixed), pl.BlockSpec((1, SGU_WIDTH), fixed),
                  pl.BlockSpec((SGU_GROUPS, SGU_CHUNK, SGU_CHUNK), lambda i: (0, 0, 0)),
                  pl.BlockSpec((SGU_CHUNK, SGU_GROUPS), fixed)],
        out_specs=pl.BlockSpec((SGU_TC, SGU_WIDTH), lambda i: (i, 0)),
        out_shape=jax.ShapeDtypeStruct((t, SGU_WIDTH), F32),
        compiler_params=_params("parallel"),
        name="sgu_gate",
    )(proj, ln_g.reshape(1, SGU_WIDTH), ln_b.reshape(1, SGU_WIDTH), jnp.tril(w_s).astype(BF16), b_s.T)


def _sgu_mixer_jax(uv, ln_g, ln_b, w_s, b_s):
    bs, s, _ = uv.shape
    uv = jax.nn.gelu(uv)
    u, v = uv[..., :SGU_WIDTH], uv[..., SGU_WIDTH:]
    mu = jnp.mean(v, axis=-1, keepdims=True)
    var = jnp.mean(jnp.square(v - mu), axis=-1, keepdims=True)
    v = (v - mu) * lax.rsqrt(var + RMS_EPS) * ln_g + ln_b
    nc = s // SGU_CHUNK
    v = v.reshape(bs, nc, SGU_CHUNK, SGU_GROUPS, SGU_WIDTH // SGU_GROUPS)
    mix = (jnp.einsum('gij,bcjgd->bcigd', jnp.tril(w_s), v) + b_s.T[None, None, :, :, None])
    return u * mix.reshape(bs, s, SGU_WIDTH)


def _permute_w_in(w):
    sizes = ([SSM_WIDTH, GDN_QKV, GDN_HEADS, GDN_HEADS, GDN_HEADS * GDN_DV, NSA_Q]
             + [NSA_KV] * 6 + [3 * NSA_HEADS, 2 * SGU_WIDTH, N_BRANCH * D_MODEL])
    cuts = [int(c) for c in np.cumsum(sizes)[:-1]]
    (w_ssm, w_qkv, w_a, w_b, w_z, w_nq, w_kc, w_vc, w_ks, w_vs, w_kw, w_vw,
     w_ng, w_sgu, w_gate) = jnp.split(w, cuts, axis=-1)
    pad = jnp.zeros((w.shape[0], PROJ_COLS - OFF_SMALL - SMALL_USED), w.dtype)
    return jnp.concatenate([w_qkv, w_ssm, w_z, w_nq, w_sgu, w_gate, w_kc, w_vc, w_ks, w_vs,
                            w_kw, w_vw, w_a, w_b, w_ng, pad], axis=-1).astype(BF16)


def kernel(x, positions, norm_mix, norm_ffn, norm_final, w_in, ssm_a_re, ssm_a_im, ssm_log_dt, ssm_b_re, ssm_b_im, ssm_c_re, ssm_c_im, ssm_d, ssm_glu_w, gdn_conv_w, gdn_a_log, gdn_dt_bias, gdn_norm, nsa_cmp_k_pe, nsa_cmp_k_w1, nsa_cmp_k_w2, nsa_cmp_v_pe, nsa_cmp_v_w1, nsa_cmp_v_w2, rel_bias, sgu_ln_g, sgu_ln_b, sgu_w, sgu_b, w_br_ssm, w_br_gdn, w_br_nsa, w_br_sgu, w_out, ffn_w_gate, ffn_w_up, ffn_w_down, moe_router, moe_w_gate, moe_w_up, moe_w_down):
    bs, s, d = x.shape
    t = bs * s
    xf = x.reshape(t, d)
    for layer in range(DEPTH):
        proj = in_proj(xf, norm_mix[layer], _permute_w_in(w_in[layer]))
        p3 = proj.reshape(bs, s, PROJ_COLS)

        def seg(off, width):
            return p3[..., off:off + width]

        y_ssm = ssm_mixer(seg(OFF_SSM, SSM_WIDTH), ssm_a_re[layer], ssm_a_im[layer],
                          ssm_log_dt[layer], ssm_b_re[layer], ssm_b_im[layer], ssm_c_re[layer],
                          ssm_c_im[layer], ssm_d[layer], ssm_glu_w[layer])
        y_gdn = gdn_mixer(proj, bs, gdn_conv_w[layer], gdn_a_log[layer], gdn_dt_bias[layer],
                          gdn_norm[layer])
        kvs = [seg(OFF_KV + i * NSA_KV, NSA_KV) for i in range(6)]
        y_nsa = nsa_mixer(seg(OFF_NQ, NSA_Q), *kvs, seg(OFF_SMALL + 2 * GDN_HEADS, 3 * NSA_HEADS),
                          positions, nsa_cmp_k_pe[layer], nsa_cmp_k_w1[layer], nsa_cmp_k_w2[layer],
                          nsa_cmp_v_pe[layer], nsa_cmp_v_w1[layer], nsa_cmp_v_w2[layer], rel_bias)
        y_sgu = sgu_mixer(proj, sgu_ln_g[layer], sgu_ln_b[layer], sgu_w[layer], sgu_b[layer])
        ys = [y_ssm.reshape(t, -1), y_gdn, y_nsa.reshape(t, -1), y_sgu]
        ws = [w[layer].astype(BF16) for w in (w_br_ssm, w_br_gdn, w_br_nsa, w_br_sgu)]
        xf = merge(xf, proj, ys, ws, w_out[layer].astype(BF16))
        i = layer // 2
        if layer % 2 == 0:
            xf = dense_ffn(xf, norm_ffn[layer], ffn_w_gate[i].astype(BF16),
                           ffn_w_up[i].astype(BF16), ffn_w_down[i].astype(BF16))
        else:
            xf = moe_layer(xf, norm_ffn[layer], moe_router[i], moe_w_gate[i].astype(BF16),
                           moe_w_up[i].astype(BF16), moe_w_down[i].astype(BF16), norm_final)
    return xf.reshape(bs, s, d)
```

```python
import functools
import math

import jax
import jax.numpy as jnp
from jax import lax
import numpy as np
from jax.experimental import pallas as pl
from jax.experimental.pallas import tpu as pltpu

F32 = jnp.float32
BF16 = jnp.bfloat16

D_MODEL = 1024
DEPTH = 2
SSM_WIDTH = D_MODEL // 2
SSM_GROUP = 16
SSM_GROUPS = SSM_WIDTH // SSM_GROUP
SSM_STATE = 64
GDN_HEADS = 4
GDN_DK = 128
GDN_DV = 128
GDN_CONV = 4
GDN_CHUNK = 64
NSA_HEADS = 8
NSA_KV_GROUPS = 2
NSA_HPG = NSA_HEADS // NSA_KV_GROUPS
NSA_DH = 64
NSA_CMP_LEN = 32
NSA_CMP_STRIDE = 16
NSA_CMP_HIDDEN = 128
NSA_SEL_LEN = 64
NSA_SEL_TOPK = 16
NSA_WINDOW = 512
NSA_QBLOCK = 128
SGU_WIDTH = D_MODEL // 2
SGU_GROUPS = 4
SGU_CHUNK = 128
REL_BUCKETS = 32
REL_MAX_DIST = 128
FFN_DENSE = 2816
N_EXPERTS = 8
TOP_K = 2
FFN_EXPERT = 3584
N_BRANCH = 4
RMS_EPS = 1e-6
GDN_QKV = GDN_HEADS * (2 * GDN_DK + GDN_DV)
NSA_Q = NSA_HEADS * NSA_DH
NSA_KV = NSA_KV_GROUPS * NSA_DH

LANES = 128
VMEM_LIMIT = 48 * 1024 * 1024

OFF_QKV = 0
OFF_SSM = OFF_QKV + GDN_QKV
OFF_Z = OFF_SSM + SSM_WIDTH
OFF_NQ = OFF_Z + GDN_HEADS * GDN_DV
OFF_SGU = OFF_NQ + NSA_Q
OFF_GATE = OFF_SGU + 2 * SGU_WIDTH
OFF_KV = OFF_GATE + N_BRANCH * D_MODEL
OFF_SMALL = OFF_KV + 6 * NSA_KV
SMALL_USED = 2 * GDN_HEADS + 3 * NSA_HEADS
PROJ_COLS = 9216


def _params(*sem):
    return pltpu.CompilerParams(dimension_semantics=sem, vmem_limit_bytes=VMEM_LIMIT)


def _rms(x, g):
    return x * lax.rsqrt(jnp.mean(x * x, axis=-1, keepdims=True) + RMS_EPS) * g


def _in_proj_body(x_ref, g_ref, w_ref, o_ref, h_ref):
    @pl.when(pl.program_id(1) == 0)
    def _():
        h_ref[...] = _rms(x_ref[...], g_ref[...]).astype(BF16)

    o_ref[...] = jnp.dot(h_ref[...], w_ref[...].astype(BF16), preferred_element_type=F32)


def in_proj(x, g, w, tm=1024, tn=512):
    t, d = x.shape
    n = w.shape[1]
    return pl.pallas_call(
        _in_proj_body,
        grid=(t // tm, n // tn),
        in_specs=[pl.BlockSpec((tm, d), lambda i, j: (i, 0)),
                  pl.BlockSpec((1, d), lambda i, j: (0, 0)),
                  pl.BlockSpec((d, tn), lambda i, j: (0, j))],
        out_specs=pl.BlockSpec((tm, tn), lambda i, j: (i, j)),
        out_shape=jax.ShapeDtypeStruct((t, n), F32),
        scratch_shapes=[pltpu.VMEM((tm, d), BF16)],
        compiler_params=_params("parallel", "arbitrary"),
        name="in_proj",
    )(x, g.reshape(1, d), w)


def _merge_body(x_ref, gate_ref, ya_ref, yb_ref, yc_ref, yd_ref,
                wa_ref, wb_ref, wc_ref, wd_ref, wo_ref, o_ref):
    def branch(k, y_ref, w_ref):
        p = jnp.dot(y_ref[...].astype(BF16), w_ref[...], preferred_element_type=F32)
        return jax.nn.sigmoid(gate_ref[:, k * D_MODEL:(k + 1) * D_MODEL]) * p

    merged = (branch(0, ya_ref, wa_ref) + branch(1, yb_ref, wb_ref)
              + branch(2, yc_ref, wc_ref) + branch(3, yd_ref, wd_ref))
    o_ref[...] = x_ref[...] + jnp.dot(merged.astype(BF16), wo_ref[...],
                                      preferred_element_type=F32)


def merge(x, proj, ys, ws, w_out, tm=256):
    t, d = x.shape
    gate_blk = OFF_GATE // (N_BRANCH * D_MODEL)
    row = lambda i: (i, 0)
    fixed = lambda i: (0, 0)
    in_specs = [pl.BlockSpec((tm, d), row),
                pl.BlockSpec((tm, N_BRANCH * D_MODEL), lambda i: (i, gate_blk))]
    in_specs += [pl.BlockSpec((tm, y.shape[1]), row) for y in ys]
    in_specs += [pl.BlockSpec(w.shape, fixed) for w in ws]
    in_specs += [pl.BlockSpec(w_out.shape, fixed)]
    return pl.pallas_call(
        _merge_body,
        grid=(t // tm,),
        in_specs=in_specs,
        out_specs=pl.BlockSpec((tm, d), row),
        out_shape=jax.ShapeDtypeStruct((t, d), F32),
        compiler_params=_params("parallel"),
        name="merge",
    )(x, proj, *ys, *ws, w_out)


def _ffn_body(x_ref, g_ref, wg_ref, wu_ref, wd_ref, o_ref, h_ref, acc_ref):
    f = pl.program_id(1)

    @pl.when(f == 0)
    def _():
        h_ref[...] = _rms(x_ref[...], g_ref[...]).astype(BF16)
        acc_ref[...] = x_ref[...]

    h = h_ref[...]
    a = jnp.dot(h, wg_ref[...].astype(BF16), preferred_element_type=F32)
    u = jnp.dot(h, wu_ref[...].astype(BF16), preferred_element_type=F32)
    act = (a * jax.nn.sigmoid(a) * u).astype(BF16)
    acc_ref[...] += jnp.dot(act, wd_ref[...].astype(BF16), preferred_element_type=F32)

    @pl.when(f == pl.num_programs(1) - 1)
    def _():
        o_ref[...] = acc_ref[...]


def dense_ffn(x, g, wg, wu, wd, tm=1024, tf=256):
    t, d = x.shape
    nf = wg.shape[1]
    return pl.pallas_call(
        _ffn_body,
        grid=(t // tm, nf // tf),
        in_specs=[pl.BlockSpec((tm, d), lambda i, f: (i, 0)),
                  pl.BlockSpec((1, d), lambda i, f: (0, 0)),
                  pl.BlockSpec((d, tf), lambda i, f: (0, f)),
                  pl.BlockSpec((d, tf), lambda i, f: (0, f)),
                  pl.BlockSpec((tf, d), lambda i, f: (f, 0))],
        out_specs=pl.BlockSpec((tm, d), lambda i, f: (i, 0)),
        out_shape=jax.ShapeDtypeStruct((t, d), F32),
        scratch_shapes=[pltpu.VMEM((tm, d), BF16), pltpu.VMEM((tm, d), F32)],
        compiler_params=_params("parallel", "arbitrary"),
        name="dense_ffn",
    )(x, g.reshape(1, d), wg, wu, wd)


def _route_body(x_ref, g_ref, rhi_ref, rlo_ref, h_ref, idx_ref, wt_ref):
    h = _rms(x_ref[...], g_ref[...])
    hi = h.astype(BF16)
    lo = (h - hi.astype(F32)).astype(BF16)
    h_ref[...] = hi
    logits = (jnp.dot(hi, rhi_ref[...], preferred_element_type=F32)
              + jnp.dot(hi, rlo_ref[...], preferred_element_type=F32)
              + jnp.dot(lo, rhi_ref[...], preferred_element_type=F32))
    col = lax.broadcasted_iota(jnp.int32, logits.shape, 1).astype(F32)
    neg = jnp.float32(-jnp.inf)
    lg = jnp.where(col < N_EXPERTS, logits, neg)
    m1 = jnp.max(lg, axis=-1, keepdims=True)
    i1 = jnp.min(jnp.where(lg == m1, col, float(LANES)), axis=-1, keepdims=True)
    lg2 = jnp.where(col == i1, neg, lg)
    m2 = jnp.max(lg2, axis=-1, keepdims=True)
    i2 = jnp.min(jnp.where(lg2 == m2, col, float(LANES)), axis=-1, keepdims=True)
    e = jnp.exp(m2 - m1)
    w1 = 1.0 / (1.0 + e)
    w2 = e / (1.0 + e)
    idx_ref[...] = jnp.where(col == 0, i1, jnp.where(col == 1, i2, 0.0)).astype(jnp.int32)
    wt_ref[...] = jnp.where(col == 0, w1, jnp.where(col == 1, w2, 0.0))


def moe_route(x, g, r_hi, r_lo, tm=512):
    t, d = x.shape
    row = lambda i: (i, 0)
    fixed = lambda i: (0, 0)
    return pl.pallas_call(
        _route_body,
        grid=(t // tm,),
        in_specs=[pl.BlockSpec((tm, d), row), pl.BlockSpec((1, d), fixed),
                  pl.BlockSpec((d, LANES), fixed), pl.BlockSpec((d, LANES), fixed)],
        out_specs=[pl.BlockSpec((tm, d), row), pl.BlockSpec((tm, LANES), row),
                   pl.BlockSpec((tm, LANES), row)],
        out_shape=[jax.ShapeDtypeStruct((t, d), BF16),
                   jax.ShapeDtypeStruct((t, LANES), jnp.int32),
                   jax.ShapeDtypeStruct((t, LANES), F32)],
        compiler_params=_params("parallel"),
        name="moe_route",
    )(x, g.reshape(1, d), r_hi, r_lo)


def _experts_body(te_ref, nu_ref, xs_ref, rw_ref, wg_ref, wu_ref, wd_ref, o_ref, acc_ref):
    i = pl.program_id(0)
    f = pl.program_id(1)
    used = i < nu_ref[0]

    @pl.when(f == 0)
    def _():
        acc_ref[...] = jnp.zeros_like(acc_ref)

    @pl.when(used)
    def _():
        h = xs_ref[...]
        a = jnp.dot(h, wg_ref[0].astype(BF16), preferred_element_type=F32)
        u = jnp.dot(h, wu_ref[0].astype(BF16), preferred_element_type=F32)
        act = (a * jax.nn.sigmoid(a) * u).astype(BF16)
        acc_ref[...] += jnp.dot(act, wd_ref[0].astype(BF16), preferred_element_type=F32)

    @pl.when(f == pl.num_programs(1) - 1)
    def _():
        o_ref[...] = (acc_ref[...] * rw_ref[...]).astype(o_ref.dtype)


def moe_experts(tile_expert, n_used, xs, row_w, wg, wu, wd, tm, tf=512):
    p, d = xs.shape
    nf = wg.shape[2]
    grid_spec = pltpu.PrefetchScalarGridSpec(
        num_scalar_prefetch=2,
        grid=(p // tm, nf // tf),
        in_specs=[pl.BlockSpec((tm, d), lambda i, f, te, nu: (i, 0)),
                  pl.BlockSpec((tm, 1), lambda i, f, te, nu: (i, 0)),
                  pl.BlockSpec((1, d, tf), lambda i, f, te, nu: (te[i], 0, f)),
                  pl.BlockSpec((1, d, tf), lambda i, f, te, nu: (te[i], 0, f)),
                  pl.BlockSpec((1, tf, d), lambda i, f, te, nu: (te[i], f, 0))],
        out_specs=pl.BlockSpec((tm, d), lambda i, f, te, nu: (i, 0)),
        scratch_shapes=[pltpu.VMEM((tm, d), F32)],
    )
    return pl.pallas_call(
        _experts_body,
        grid_spec=grid_spec,
        out_shape=jax.ShapeDtypeStruct((p, d), BF16),
        compiler_params=_params("parallel", "arbitrary"),
        name="moe_experts",
    )(tile_expert, n_used, xs, row_w, wg, wu, wd)


def _combine_norm_body(x_ref, ya_ref, yb_ref, g_ref, o_ref):
    o_ref[...] = _rms(x_ref[...] + ya_ref[...].astype(F32) + yb_ref[...].astype(F32), g_ref[...])


def combine_norm(x, ya, yb, g, tm=1024):
    t, d = x.shape
    row = lambda i: (i, 0)
    return pl.pallas_call(
        _combine_norm_body,
        grid=(t // tm,),
        in_specs=[pl.BlockSpec((tm, d), row)] * 3 + [pl.BlockSpec((1, d), lambda i: (0, 0))],
        out_specs=pl.BlockSpec((tm, d), row),
        out_shape=jax.ShapeDtypeStruct((t, d), F32),
        compiler_params=_params("parallel"),
        name="combine_norm",
    )(x, ya, yb, g.reshape(1, d))


def moe_layer(x, g_norm, router, wg, wu, wd, g_final, tm=1024):
    t, d = x.shape
    r = jnp.zeros((d, LANES), F32).at[:, :N_EXPERTS].set(router)
    r_hi = r.astype(BF16)
    r_lo = (r - r_hi.astype(F32)).astype(BF16)
    h, idx, wts = moe_route(x, g_norm, r_hi, r_lo)
    e_flat = jnp.concatenate([idx[:, 0], idx[:, 1]])
    w_flat = jnp.concatenate([wts[:, 0], wts[:, 1]])
    onehot = (e_flat[:, None] == jnp.arange(N_EXPERTS)[None, :]).astype(jnp.int32)
    csum = jnp.cumsum(onehot, axis=0)
    rank = jnp.sum((csum - onehot) * onehot, axis=1)
    counts = csum[-1]
    padded = ((counts + tm - 1) // tm) * tm
    ends = jnp.cumsum(padded)
    starts = ends - padded
    dest = starts[e_flat] + rank
    n_tiles = (TOP_K * t) // tm + N_EXPERTS
    p = n_tiles * tm
    tok = jnp.concatenate([jnp.arange(t, dtype=jnp.int32)] * TOP_K)
    src = jnp.zeros((p,), jnp.int32).at[dest].set(tok)
    row_w = jnp.zeros((p,), F32).at[dest].set(w_flat)
    tile_expert = jnp.minimum(
        jnp.searchsorted(ends, jnp.arange(n_tiles, dtype=jnp.int32) * tm, side="right"),
        N_EXPERTS - 1).astype(jnp.int32)
    n_used = (ends[-1] // tm).astype(jnp.int32).reshape(1)
    xs = jnp.take(h, src, axis=0)
    ys = moe_experts(tile_expert, n_used, xs, row_w.reshape(p, 1), wg, wu, wd, tm)
    ya = jnp.take(ys, dest[:t], axis=0)
    yb = jnp.take(ys, dest[t:], axis=0)
    return combine_norm(x, ya, yb, g_final)


def _l2_norm(t):
    return t * lax.rsqrt(jnp.sum(t * t, axis=-1, keepdims=True) + 1e-6)


def _masked_softmax(logits, valid):
    logits = jnp.where(valid, logits.astype(F32), -1e30)
    return jax.nn.softmax(logits, axis=-1) * valid


def _rel_bucket(dist):
    n = jnp.maximum(dist, 0)
    max_exact = REL_BUCKETS // 2
    nf = jnp.maximum(n, 1).astype(F32)
    large = max_exact + (jnp.log(nf / max_exact) / math.log(REL_MAX_DIST / max_exact)
                         * (REL_BUCKETS - max_exact)).astype(jnp.int32)
    large = jnp.minimum(large, REL_BUCKETS - 1)
    return jnp.where(n < max_exact, n, large)


SSM_TC = 512
SSM_SUB = 128
SSM_HALF = 256
SSM_NSTATE = SSM_GROUPS * SSM_STATE


def _ssm_body(u_ref, bw_ref, cw_ref, d_ref, glu_ref, ar_ref, ai_ref, pr_ref, pi_ref, o_ref,
              xr_ref, xi_ref, cr_ref, ci_ref):
    @pl.when(pl.program_id(1) == 0)
    def _():
        cr_ref[...] = jnp.zeros_like(cr_ref)
        ci_ref[...] = jnp.zeros_like(ci_ref)

    u = u_ref[0]
    ub = u.astype(BF16)
    nblk = SSM_WIDTH // SSM_HALF
    sblk = SSM_NSTATE // nblk
    for k in range(nblk):
        bu = jnp.dot(ub[:, k * SSM_HALF:(k + 1) * SSM_HALF], bw_ref[k], preferred_element_type=F32)
        xr_ref[:, k * sblk:(k + 1) * sblk] = bu[:, :sblk]
        xi_ref[:, k * sblk:(k + 1) * sblk] = bu[:, sblk:]

    row = lax.broadcasted_iota(jnp.int32, (SSM_SUB, LANES), 0)
    n_steps = SSM_SUB.bit_length() - 1

    def shift(x, d):
        if d % 8 == 0:
            return jnp.concatenate([jnp.zeros((d, LANES), F32), x[:SSM_SUB - d]], axis=0)
        return jnp.where(row >= d, pltpu.roll(x, d, 0), 0.0)

    def strip(c, carry):
        col = pl.ds(pl.multiple_of(c * LANES, LANES), LANES)
        c_r = cr_ref[:, col]
        c_i = ci_ref[:, col]
        for sub in range(SSM_TC // SSM_SUB):
            rows = slice(sub * SSM_SUB, (sub + 1) * SSM_SUB)
            xr = xr_ref[rows, col]
            xi = xi_ref[rows, col]
            for i in range(n_steps):
                a_r = ar_ref[i:i + 1, col]
                a_i = ai_ref[i:i + 1, col]
                sr, si = shift(xr, 1 << i), shift(xi, 1 << i)
                xr, xi = xr + a_r * sr - a_i * si, xi + a_r * si + a_i * sr
            p_r = pr_ref[:, col]
            p_i = pi_ref[:, col]
            xr, xi = xr + p_r * c_r - p_i * c_i, xi + p_r * c_i + p_i * c_r
            xr_ref[rows, col] = xr
            xi_ref[rows, col] = xi
            c_r = xr[SSM_SUB - 1:SSM_SUB, :]
            c_i = xi[SSM_SUB - 1:SSM_SUB, :]
        cr_ref[:, col] = c_r
        ci_ref[:, col] = c_i
        return carry

    lax.fori_loop(0, SSM_NSTATE // LANES, strip, 0)

    ys = []
    for k in range(nblk):
        st = jnp.concatenate([xr_ref[:, k * sblk:(k + 1) * sblk].astype(BF16),
                              xi_ref[:, k * sblk:(k + 1) * sblk].astype(BF16)], axis=1)
        ys.append(jnp.dot(st, cw_ref[k], preferred_element_type=F32))
    y = jax.nn.gelu(jnp.concatenate(ys, axis=1) + d_ref[...] * u)
    o_ref[0] = y * jax.nn.sigmoid(jnp.dot(y.astype(BF16), glu_ref[...], preferred_element_type=F32))


def ssm_mixer(u, a_re, a_im, log_dt, b_re, b_im, c_re, c_im, d_skip, glu_w):
    bs, s, _ = u.shape
    assert s % SSM_TC == 0
    nblk = SSM_WIDTH // SSM_HALF
    gpb = SSM_GROUPS // nblk
    dt = jnp.exp(log_dt)[:, None]
    mag = jnp.exp(a_re * dt)
    abar_r, abar_i = mag * jnp.cos(a_im * dt), mag * jnp.sin(a_im * dt)
    nr, ni = abar_r - 1.0, abar_i
    den = a_re * a_re + a_im * a_im
    sr, si = (nr * a_re + ni * a_im) / den, (ni * a_re - nr * a_im) / den
    bbar_r = sr[..., None] * b_re - si[..., None] * b_im
    bbar_i = sr[..., None] * b_im + si[..., None] * b_re

    def powers(k):
        kk = k.astype(F32)[:, None, None]
        m = jnp.exp(kk * (a_re * dt))
        return ((m * jnp.cos(kk * (a_im * dt))).reshape(-1, SSM_NSTATE),
                (m * jnp.sin(kk * (a_im * dt))).reshape(-1, SSM_NSTATE))

    ar, ai = powers(2 ** jnp.arange(8))
    pr, pi = powers(jnp.arange(1, SSM_SUB + 1))
    eye = jnp.eye(gpb, dtype=F32)

    def in_block(w):
        return jnp.einsum('gnp,gh->gphn', w, eye).reshape(gpb * SSM_GROUP, gpb * SSM_STATE)

    def out_block(w):
        return jnp.einsum('gpn,gh->gnhp', w, eye).reshape(gpb * SSM_STATE, gpb * SSM_GROUP)

    bw = jnp.stack([jnp.concatenate([in_block(bbar_r[k * gpb:(k + 1) * gpb]),
                                     in_block(bbar_i[k * gpb:(k + 1) * gpb])], axis=1)
                    for k in range(nblk)]).astype(BF16)
    cw = jnp.stack([jnp.concatenate([out_block(c_re[k * gpb:(k + 1) * gpb]),
                                     -out_block(c_im[k * gpb:(k + 1) * gpb])], axis=0)
                    for k in range(nblk)]).astype(BF16)
    fixed2 = lambda b, i: (0, 0)
    fixed3 = lambda b, i: (0, 0, 0)
    return pl.pallas_call(
        _ssm_body,
        grid=(bs, s // SSM_TC),
        in_specs=[pl.BlockSpec((1, SSM_TC, SSM_WIDTH), lambda b, i: (b, i, 0)),
                  pl.BlockSpec(bw.shape, fixed3), pl.BlockSpec(cw.shape, fixed3),
                  pl.BlockSpec((1, SSM_WIDTH), fixed2), pl.BlockSpec((SSM_WIDTH, SSM_WIDTH), fixed2),
                  pl.BlockSpec((8, SSM_NSTATE), fixed2), pl.BlockSpec((8, SSM_NSTATE), fixed2),
                  pl.BlockSpec((SSM_SUB, SSM_NSTATE), fixed2), pl.BlockSpec((SSM_SUB, SSM_NSTATE), fixed2)],
        out_specs=pl.BlockSpec((1, SSM_TC, SSM_WIDTH), lambda b, i: (b, i, 0)),
        out_shape=jax.ShapeDtypeStruct((bs, s, SSM_WIDTH), F32),
        scratch_shapes=[pltpu.VMEM((SSM_TC, SSM_NSTATE), F32), pltpu.VMEM((SSM_TC, SSM_NSTATE), F32),
                        pltpu.VMEM((1, SSM_NSTATE), F32), pltpu.VMEM((1, SSM_NSTATE), F32)],
        compiler_params=_params("parallel", "arbitrary"),
        name="ssm_scan",
    )(u, bw, cw, d_skip.reshape(1, SSM_WIDTH), glu_w.astype(BF16), ar, ai, pr, pi)


def _ssm_mixer_jax(u, a_re, a_im, log_dt, b_re, b_im, c_re, c_im, d_skip, glu_w):
    bs, s, _ = u.shape
    uf = u.reshape(bs, s, SSM_GROUPS, SSM_GROUP)
    dt = jnp.exp(log_dt)[:, None]
    mag = jnp.exp(a_re * dt)
    abar_r, abar_i = mag * jnp.cos(a_im * dt), mag * jnp.sin(a_im * dt)
    nr, ni = abar_r - 1.0, abar_i
    den = a_re * a_re + a_im * a_im
    sr, si = (nr * a_re + ni * a_im) / den, (ni * a_re - nr * a_im) / den
    bbar_r = sr[..., None] * b_re - si[..., None] * b_im
    bbar_i = sr[..., None] * b_im + si[..., None] * b_re
    bu_r = jnp.einsum('bsgp,gnp->bsgn', uf, bbar_r)
    bu_i = jnp.einsum('bsgp,gnp->bsgn', uf, bbar_i)
    ar = jnp.broadcast_to(abar_r, bu_r.shape)
    ai = jnp.broadcast_to(abar_i, bu_r.shape)

    def combine(e1, e2):
        a1r, a1i, b1r, b1i = e1
        a2r, a2i, b2r, b2i = e2
        return (a2r * a1r - a2i * a1i, a2r * a1i + a2i * a1r,
                a2r * b1r - a2i * b1i + b2r, a2r * b1i + a2i * b1r + b2i)

    _, _, st_r, st_i = lax.associative_scan(combine, (ar, ai, bu_r, bu_i), axis=1)
    y = (jnp.einsum('gpn,bsgn->bsgp', c_re, st_r) - jnp.einsum('gpn,bsgn->bsgp', c_im, st_i)
         + d_skip.reshape(SSM_GROUPS, SSM_GROUP) * uf)
    y = jax.nn.gelu(y.reshape(bs, s, SSM_WIDTH))
    return y * jax.nn.sigmoid(y @ glu_w)


GDN_TC = 256
GDN_HALO = 8
MASKED = -1e30


def _dot_nt(a, b):
    return lax.dot_general(a, b, (((1,), (1,)), ((), ())), preferred_element_type=F32)


def _dot_tn(a, b):
    return lax.dot_general(a, b, (((0,), (0,)), ((), ())), preferred_element_type=F32)


def _split_bf16(x):
    hi = x.astype(BF16)
    return hi, (x - hi.astype(F32)).astype(BF16)


def _gdn_body(nega_ref, dtb_ref, q_ref, k_ref, v_ref, z_ref, sm_ref, wq_ref, wk_ref, wv_ref, ng_ref,
              o_ref, state_ref, hq_ref, hk_ref, hv_ref, vnew_ref):
    TC, CH, DK = GDN_TC, GDN_CHUNK, GDN_DK
    h = pl.program_id(1)

    @pl.when(pl.program_id(2) == 0)
    def _():
        state_ref[...] = jnp.zeros_like(state_ref)
        hq_ref[...] = jnp.zeros_like(hq_ref)
        hk_ref[...] = jnp.zeros_like(hk_ref)
        hv_ref[...] = jnp.zeros_like(hv_ref)

    def conv_silu(x_ref, halo_ref, w_ref):
        x = x_ref[...]
        xx = jnp.concatenate([halo_ref[...], x], axis=0)
        w = w_ref[...]
        y = w[GDN_CONV - 1:GDN_CONV] * x
        for d in range(1, GDN_CONV):
            y = y + w[GDN_CONV - 1 - d:GDN_CONV - d] * pltpu.roll(xx, d, 0)[GDN_HALO:GDN_HALO + TC]
        halo_ref[...] = x[TC - GDN_HALO:TC]
        return y * jax.nn.sigmoid(y)

    q = conv_silu(q_ref, hq_ref, wq_ref)
    k = conv_silu(k_ref, hk_ref, wk_ref)
    v = conv_silu(v_ref, hv_ref, wv_ref)
    q = q * lax.rsqrt(jnp.sum(q * q, axis=-1, keepdims=True) + 1e-6) * (DK ** -0.5)
    k = k * lax.rsqrt(jnp.sum(k * k, axis=-1, keepdims=True) + 1e-6)

    small = sm_ref[...]
    lane_s = lax.broadcasted_iota(jnp.int32, small.shape, 1)
    a_col = jnp.sum(jnp.where(lane_s == h, small, 0.0), axis=-1, keepdims=True)
    b_col = jnp.sum(jnp.where(lane_s == GDN_HEADS + h, small, 0.0), axis=-1, keepdims=True)
    beta = jax.nn.sigmoid(b_col)
    xa = a_col + dtb_ref[h]
    softplus = jnp.maximum(xa, 0.0) + jnp.log(1.0 + jnp.exp(-jnp.abs(xa)))
    la = jnp.broadcast_to(nega_ref[h] * softplus, (TC, LANES))

    ri = lax.broadcasted_iota(jnp.int32, (TC, TC), 0)
    ci = lax.broadcasted_iota(jnp.int32, (TC, TC), 1)
    same = (ri // CH) == (ci // CH)
    causal = jnp.where(same, jnp.where(ci <= ri, 1.0, 0.0), 0.0)
    strict = jnp.where(ci < ri, causal, 0.0)
    tri = causal.astype(BF16)
    la_hi, la_lo = _split_bf16(la)
    g = (jnp.dot(tri, la_hi, preferred_element_type=F32)
         + jnp.dot(tri, la_lo, preferred_element_type=F32))
    g_hi = g.astype(BF16).astype(F32)
    g_lo = g - g_hi
    lane = lax.broadcasted_iota(jnp.int32, (TC, LANES), 1)
    lhs = jnp.where(lane == 0, g_hi, jnp.where(lane == 1, g_lo, jnp.where(lane < 4, 1.0, 0.0))).astype(BF16)
    rhs = jnp.where(lane < 2, 1.0, jnp.where(lane == 2, -g_hi, jnp.where(lane == 3, -g_lo, 0.0))).astype(BF16)
    decay = jnp.exp(jnp.where(causal > 0.5, _dot_nt(lhs, rhs), MASKED))

    kb = k * beta
    k_b = k.astype(BF16)
    lmat = strict * _dot_nt(kb.astype(BF16), k_b) * decay
    eye = jnp.where(ri == ci, 1.0, 0.0)
    tinv = eye - lmat
    pw = lmat
    for _ in range(CH.bit_length() - 2):
        pb = pw.astype(BF16)
        pw = jnp.dot(pb, pb, preferred_element_type=F32)
        tinv = tinv + jnp.dot(tinv.astype(BF16), pw.astype(BF16), preferred_element_type=F32)
    eg = jnp.exp(g)
    rhs_all = jnp.concatenate([v * beta, kb * eg], axis=1).astype(BF16)
    sol = jnp.dot(tinv.astype(BF16), rhs_all, preferred_element_type=F32)
    u_val, w_val = sol[:, :GDN_DV], sol[:, GDN_DV:]
    attn = (causal * _dot_nt(q.astype(BF16), k_b) * decay).astype(BF16)
    q_st = (q * eg).astype(BF16)

    vnew_ref[...] = jnp.zeros_like(vnew_ref)
    for c in range(TC // CH):
        rows = slice(c * CH, (c + 1) * CH)
        g_last = g[(c + 1) * CH - 1:(c + 1) * CH, :]
        st = state_ref[...]
        st_b = st.astype(BF16)
        v_new = u_val[rows] - jnp.dot(w_val[rows].astype(BF16), st_b, preferred_element_type=F32)
        vnew_ref[rows, :] = v_new.astype(BF16)
        o_c = (jnp.dot(q_st[rows], st_b, preferred_element_type=F32)
               + jnp.dot(attn[rows], vnew_ref[...], preferred_element_type=F32))
        k_st = (k[rows] * jnp.exp(g_last - g[rows])).astype(BF16)
        state_ref[...] = st * jnp.exp(g_last[:, :1]) + _dot_tn(k_st, v_new.astype(BF16))
        o_c = o_c * lax.rsqrt(jnp.mean(o_c * o_c, axis=-1, keepdims=True) + RMS_EPS) * ng_ref[...]
        zc = z_ref[rows, :]
        o_ref[rows, :] = o_c * (zc * jax.nn.sigmoid(zc))


def gdn_mixer(proj, bs, conv_w, a_log, dt_bias, norm_g):
    t = proj.shape[0]
    s = t // bs
    nt = s // GDN_TC
    assert s % GDN_TC == 0 and GDN_DK == LANES and GDN_DV == LANES and OFF_QKV == 0
    H = GDN_HEADS
    cw = jnp.zeros((8, GDN_QKV), F32).at[:GDN_CONV].set(conv_w)
    tok = lambda off: pl.BlockSpec((GDN_TC, LANES), lambda b, h, i, *_: (b * nt + i, off + h))
    wspec = lambda off: pl.BlockSpec((8, LANES), lambda b, h, i, *_: (0, off + h))
    grid_spec = pltpu.PrefetchScalarGridSpec(
        num_scalar_prefetch=0,
        grid=(bs, H, nt),
        in_specs=[pl.BlockSpec(memory_space=pltpu.SMEM), pl.BlockSpec(memory_space=pltpu.SMEM),
                  tok(0), tok(H), tok(2 * H), tok(OFF_Z // LANES),
                  pl.BlockSpec((GDN_TC, 2 * LANES), lambda b, h, i, *_: (b * nt + i, OFF_SMALL // (2 * LANES))),
                  wspec(0), wspec(H), wspec(2 * H),
                  pl.BlockSpec((1, GDN_DV), lambda b, h, i, *_: (0, 0))],
        out_specs=pl.BlockSpec((GDN_TC, GDN_DV), lambda b, h, i, *_: (b * nt + i, h)),
        scratch_shapes=[pltpu.VMEM((GDN_DK, GDN_DV), F32)] + [pltpu.VMEM((GDN_HALO, LANES), F32)] * 3
        + [pltpu.VMEM((GDN_TC, GDN_DV), BF16)],
    )
    return pl.pallas_call(
        _gdn_body,
        grid_spec=grid_spec,
        out_shape=jax.ShapeDtypeStruct((t, H * GDN_DV), F32),
        compiler_params=_params("parallel", "parallel", "arbitrary"),
        name="gdn_delta",
    )(-jnp.exp(a_log), dt_bias.astype(F32), proj, proj, proj, proj, proj, cw, cw, cw,
      norm_g.reshape(1, GDN_DV))


def _gdn_mixer_jax(qkv, a, b, z, conv_w, a_log, dt_bias, norm_g):
    bs, s, c = qkv.shape
    H, DK, DV, CH = GDN_HEADS, GDN_DK, GDN_DV, GDN_CHUNK
    qkv = lax.conv_general_dilated(qkv, conv_w[:, None, :], window_strides=(1,),
                                   padding=[(GDN_CONV - 1, 0)],
                                   dimension_numbers=('NWC', 'WIO', 'NWC'), feature_group_count=c)
    qkv = jax.nn.silu(qkv)
    q, k, v = jnp.split(qkv, [H * DK, 2 * H * DK], axis=-1)
    q = _l2_norm(q.reshape(bs, s, H, DK)) * (DK ** -0.5)
    k = _l2_norm(k.reshape(bs, s, H, DK))
    v = v.reshape(bs, s, H, DV)
    beta = jax.nn.sigmoid(b)
    log_alpha = -jnp.exp(a_log) * jax.nn.softplus(a + dt_bias)
    nc = s // CH

    def chunks(t):
        return t.reshape(bs, nc, CH, H, -1).transpose(0, 3, 1, 2, 4)

    q, k, v = chunks(q), chunks(k), chunks(v)
    beta = chunks(beta[..., None])
    g = jnp.cumsum(chunks(log_alpha[..., None]), axis=3)
    gv = g[..., 0]
    causal = jnp.tril(jnp.ones((CH, CH), bool))
    strict = jnp.tril(jnp.ones((CH, CH), bool), -1)
    decay = jnp.exp(jnp.where(causal, gv[..., :, None] - gv[..., None, :], -jnp.inf))
    kb = k * beta
    a_mat = (jnp.where(strict, jnp.einsum('bhncd,bhnkd->bhnck', kb, k) * decay, 0.0)
             + jnp.eye(CH, dtype=F32))
    rhs = jnp.concatenate([v * beta, kb * jnp.exp(g)], axis=-1)
    sol = lax.linalg.triangular_solve(a_mat, rhs, left_side=True, lower=True, unit_diagonal=True)
    u_val, w = sol[..., :DV], sol[..., DV:]
    attn = jnp.einsum('bhncd,bhnkd->bhnck', q, k) * decay
    g_last = g[..., -1:, :]
    k_st = k * jnp.exp(g_last - g)
    q_st = q * jnp.exp(g)

    def step(state, xs):
        q_c, k_c, u_c, w_c, a_c, gl_c = xs
        v_new = u_c - jnp.einsum('bhcd,bhde->bhce', w_c, state)
        o = jnp.einsum('bhcd,bhde->bhce', q_c, state) + jnp.einsum('bhck,bhke->bhce', a_c, v_new)
        state = state * jnp.exp(gl_c) + jnp.einsum('bhcd,bhce->bhde', k_c, v_new)
        return state, o

    xs = tuple(jnp.moveaxis(t, 2, 0) for t in (q_st, k_st, u_val, w, attn, g_last))
    _, o = lax.scan(step, jnp.zeros((bs, H, DK, DV), F32), xs)
    o = o.transpose(1, 0, 3, 2, 4).reshape(bs, s, H, DV)
    o = o * lax.rsqrt(jnp.mean(o * o, axis=-1, keepdims=True) + RMS_EPS) * norm_g
    o = o * jax.nn.silu(z.reshape(bs, s, H, DV))
    return o.reshape(bs, s, H * DV)


NSA_KT = 256
NSA_NEG = MASKED
NSA_WTILES = NSA_WINDOW // LANES + 1


def _bucket_table():
    n = np.arange(LANES)
    max_exact = REL_BUCKETS // 2
    nf = np.maximum(n, 1).astype(np.float32)
    large = max_exact + (np.log(nf / np.float32(max_exact)) / np.float32(math.log(REL_MAX_DIST / max_exact))
                         * np.float32(REL_BUCKETS - max_exact)).astype(np.int32)
    tbl = np.where(n < max_exact, n, np.minimum(large, REL_BUCKETS - 1))
    assert tbl[-1] == REL_BUCKETS - 1 and REL_MAX_DIST <= LANES
    return tbl


def _compress_body(x_ref, pe_ref, w1_ref, w2_ref, o_ref):
    x = x_ref[0, 0, 0]
    w1 = w1_ref[0]
    half = NSA_CMP_STRIDE * NSA_DH
    nc = x.shape[0]
    a = jnp.dot(x, w1[:half], preferred_element_type=F32)
    b = jnp.dot(x, w1[half:], preferred_element_type=F32)
    pe_h = jnp.dot(pe_ref[0], w1, preferred_element_type=F32)
    hid = a + pltpu.roll(b, nc - 1, 0) + pe_h[0:1, :]
    o_ref[0, 0, 0] = jnp.dot(jax.nn.gelu(hid).astype(BF16), w2_ref[0], preferred_element_type=F32)


def nsa_compress(x, pe, w1, w2):
    _, bs, g, nc, width = x.shape
    return pl.pallas_call(
        _compress_body,
        grid=(2, bs, g),
        in_specs=[pl.BlockSpec((1, 1, 1, nc, width), lambda i, b, j: (i, b, j, 0, 0)),
                  pl.BlockSpec((1,) + pe.shape[1:], lambda i, b, j: (i, 0, 0)),
                  pl.BlockSpec((1,) + w1.shape[1:], lambda i, b, j: (i, 0, 0)),
                  pl.BlockSpec((1,) + w2.shape[1:], lambda i, b, j: (i, 0, 0))],
        out_specs=pl.BlockSpec((1, 1, 1, nc, NSA_DH), lambda i, b, j: (i, b, j, 0, 0)),
        out_shape=jax.ShapeDtypeStruct((2, bs, g, nc, NSA_DH), F32),
        compiler_params=_params("parallel", "parallel", "parallel"),
        name="nsa_compress",
    )(x, pe, w1, w2)


def _nsa_body(nfast, wstd, near_w, near_c, q_ref, gl_ref, qp_ref, kp_ref, cp_ref,
              ks_ref, vs_ref, kw_ref, vw_ref, kc_ref, vc_ref, agg_ref, td_ref, o_ref,
              m_ref, l_ref, acc_ref, sel_ref, *, n_sel, sel_k):
    QB, HG = NSA_QBLOCK, NSA_HPG
    g = pl.program_id(1)
    qi = pl.program_id(2)
    s0 = qi * QB
    nc = kc_ref.shape[2]
    n_ct = nc // LANES
    q = q_ref[0, 0, 0]
    qp = qp_ref[...]

    def lanes4(x):
        return jnp.concatenate([x] * HG, axis=1)

    def delta_t(kp):
        idx = jnp.clip(_dot_nt(kp, qp), 0.0, float(LANES - 1)).astype(jnp.int32)
        outs = []
        for hg in range(HG):
            tb = jnp.broadcast_to(td_ref[pl.ds(g * HG + hg, 1), :], idx.shape)
            outs.append(jnp.take_along_axis(tb, idx, axis=1))
        return jnp.concatenate(outs, axis=1)

    def maybe_delta(flag, kp):
        return lax.cond(flag != 0, lambda: delta_t(kp),
                        lambda: jnp.zeros((kp.shape[0], HG * QB), F32))

    gate = jax.nn.sigmoid(gl_ref[0, 0, 0])

    sc = _dot_nt(kc_ref[0, 0], q)
    sc = sc + jnp.concatenate(
        [maybe_delta(near_c[qi * n_ct + ct], cp_ref[ct * LANES:(ct + 1) * LANES, :])
         for ct in range(n_ct)], axis=0)
    c_id = lax.broadcasted_iota(jnp.int32, (nc, QB), 0)
    c_q = lax.broadcasted_iota(jnp.int32, (nc, QB), 1)
    ok_c = lanes4(jnp.where(c_id * NSA_CMP_STRIDE + (NSA_CMP_LEN - 1) <= s0 + c_q, 1.0, 0.0))
    sc = jnp.where(ok_c > 0.5, sc, NSA_NEG)
    e = jnp.exp(sc - jnp.max(sc, axis=0, keepdims=True)) * ok_c
    p = e * (1.0 / jnp.maximum(jnp.sum(e, axis=0, keepdims=True), 1e-30))
    out = gate[0:1, :] * jnp.dot(vc_ref[0, 0], p.astype(BF16), preferred_element_type=F32)
    ps = p[:, 0:QB]
    for hg in range(1, HG):
        ps = ps + p[:, hg * QB:(hg + 1) * QB]
    ps_hi, ps_lo = _split_bf16(ps)
    imp = (jnp.dot(agg_ref[...], ps_hi, preferred_element_type=F32)
           + jnp.dot(agg_ref[...], ps_lo, preferred_element_type=F32))
    blk = lax.broadcasted_iota(jnp.int32, (LANES, QB), 0)
    t_tok = s0 + lax.broadcasted_iota(jnp.int32, (LANES, QB), 1)
    tb_blk = t_tok // NSA_SEL_LEN
    forced = jnp.where(blk == 0, 1.0, 0.0) + jnp.where(blk == tb_blk, 1.0, 0.0) \
        + jnp.where(blk == tb_blk - 1, 1.0, 0.0)
    score = jnp.where(forced > 0.5, 1e9, jnp.where(blk * NSA_SEL_LEN <= t_tok, imp, -1e9))
    score = jnp.where(blk < n_sel, score, -jnp.inf)
    blkf = blk.astype(F32)
    sel = jnp.zeros((LANES, QB), F32)
    for _ in range(sel_k):
        mx = jnp.max(score, axis=0, keepdims=True)
        first = jnp.min(jnp.where(score == mx, blkf, float(LANES)), axis=0, keepdims=True)
        hit = blkf == first
        sel = jnp.where(hit, 1.0, sel)
        score = jnp.where(hit, -jnp.inf, score)
    sel_ref[...] = jnp.where(sel > 0.5, 0.0, NSA_NEG)

    def reset():
        m_ref[...] = jnp.full_like(m_ref, NSA_NEG)
        l_ref[...] = jnp.zeros_like(l_ref)
        acc_ref[...] = jnp.zeros_like(acc_ref)

    def tile_step(s, v_t, k):
        m_old = m_ref[k]
        m_new = jnp.maximum(m_old, jnp.max(s, axis=0, keepdims=True))
        alpha = jnp.exp(m_old - m_new)
        p = jnp.exp(s - m_new)
        l_ref[k] = l_ref[k] * alpha + jnp.sum(p, axis=0, keepdims=True)
        acc_ref[k] = acc_ref[k] * alpha + jnp.dot(v_t, p.astype(BF16), preferred_element_type=F32)
        m_ref[k] = m_new

    def finish():
        m = jnp.maximum(m_ref[0], m_ref[1])
        w0, w1 = jnp.exp(m_ref[0] - m), jnp.exp(m_ref[1] - m)
        return (acc_ref[0] * w0 + acc_ref[1] * w1) * (1.0 / (l_ref[0] * w0 + l_ref[1] * w1))

    per_tile = NSA_KT // NSA_SEL_LEN
    key_r = lax.broadcasted_iota(jnp.int32, (NSA_KT, QB), 0)
    key_q = lax.broadcasted_iota(jnp.int32, (NSA_KT, QB), 1)

    def sel_scores(kt, with_delta, diag):
        k0 = pl.multiple_of(kt * NSA_KT, NSA_KT)
        s = _dot_nt(ks_ref[0, 0, pl.ds(k0, NSA_KT), :], q)
        if with_delta:
            s = s + delta_t(kp_ref[pl.ds(k0, NSA_KT), :])
        mask = jnp.concatenate(
            [jnp.broadcast_to(sel_ref[pl.ds(kt * per_tile + i, 1), :], (NSA_SEL_LEN, QB))
             for i in range(per_tile)], axis=0)
        if diag:
            mask = jnp.where(k0 + key_r <= s0 + key_q, mask, NSA_NEG)
        return s + lanes4(mask), vs_ref[0, 0, :, pl.ds(k0, NSA_KT)]

    reset()
    n_past = (s0 + QB - 1) // NSA_KT
    n_pairs = nfast[qi] // 2

    def fast_body(i, carry):
        a = sel_scores(2 * i, False, False)
        b = sel_scores(2 * i + 1, False, False)
        tile_step(*a, 0)
        tile_step(*b, 1)
        return carry

    def slow_body(kt, carry):
        tile_step(*sel_scores(kt, True, False), 1)
        return carry

    lax.fori_loop(0, n_pairs, fast_body, 0)
    lax.fori_loop(2 * n_pairs, n_past, slow_body, 0)
    tile_step(*sel_scores(n_past, True, True), 0)
    out = out + gate[1:2, :] * finish()

    reset()
    w_r = lax.broadcasted_iota(jnp.int32, (LANES, QB), 0)
    w_q = lax.broadcasted_iota(jnp.int32, (LANES, QB), 1)
    in_window = lanes4(jnp.where(w_r > w_q, 0.0, NSA_NEG))
    causal_w = lanes4(jnp.where(w_r <= w_q, 0.0, NSA_NEG))

    def win_scores(st, n):
        st = pl.multiple_of(st, LANES)
        return _dot_nt(kw_ref[0, 0, pl.ds(st, n), :], q), vw_ref[0, 0, :, pl.ds(st, n)]

    @pl.when(wstd[qi] != 0)
    def _():
        half = NSA_WINDOW // 2
        zeros = jnp.zeros((LANES, HG * QB), F32)
        sa, va = win_scores(s0 - NSA_WINDOW, half)
        sb, vb = win_scores(s0 - half, half)
        sd, vd = win_scores(s0, LANES)
        tile_step(sa + jnp.concatenate([in_window, zeros], axis=0), va, 0)
        near = delta_t(kp_ref[pl.ds(pl.multiple_of(s0 - LANES, LANES), LANES), :])
        tile_step(sb + jnp.concatenate([zeros, near], axis=0), vb, 1)
        tile_step(sd + delta_t(kp_ref[pl.ds(pl.multiple_of(s0, LANES), LANES), :]) + causal_w, vd, 0)

    for j in range(NSA_WTILES):
        start = s0 - NSA_WINDOW + j * LANES

        @pl.when((wstd[qi] == 0) & (start >= 0))
        def _():
            s, v_t = win_scores(start, LANES)
            s = s + maybe_delta(near_w[qi * NSA_WTILES + j],
                                kp_ref[pl.ds(pl.multiple_of(start, LANES), LANES), :])
            if j == 0:
                s = s + in_window
            elif j == NSA_WTILES - 1:
                s = s + causal_w
            tile_step(s, v_t, j % 2)

    o_ref[0, 0, 0] = out + gate[2:3, :] * finish()


def _pos_digits(p, key_side):
    d0, d1, d2 = (p & 127).astype(F32), ((p >> 7) & 127).astype(F32), (p >> 14).astype(F32)
    one = jnp.ones_like(d0)
    if key_side:
        cols = [one, one, one, -d2, -d1, -d0]
    else:
        cols = [16384.0 * d2, 128.0 * d1, d0, 16384.0 * one, 128.0 * one, one]
    out = jnp.stack(cols, axis=-1)
    return jnp.pad(out, ((0, 0), (0, LANES - out.shape[-1]))).astype(BF16)


def nsa_mixer(q, k_c, v_c, k_s, v_s, k_w, v_w, gate_logits, positions,
              ck_pe, ck_w1, ck_w2, cv_pe, cv_w1, cv_w2, rel_bias):
    bs, s, _ = q.shape
    G, HG, DH, QB = NSA_KV_GROUPS, NSA_HPG, NSA_DH, NSA_QBLOCK
    nqt = s // QB
    nc = s // NSA_CMP_STRIDE
    n_cmp = (s - NSA_CMP_LEN) // NSA_CMP_STRIDE + 1
    n_sel = s // NSA_SEL_LEN
    sel_k = min(NSA_SEL_TOPK, n_sel)
    n_kt = s // NSA_KT
    n_ct = nc // LANES
    assert s % (LANES * NSA_CMP_STRIDE) == 0 and n_sel <= LANES and NSA_KT == 2 * QB

    def by_group(t):
        return t.reshape(bs, s, G, DH).transpose(0, 2, 1, 3)

    def by_group_t(t):
        return t.reshape(bs, s, G, DH).transpose(0, 2, 3, 1).astype(BF16)

    ones2 = jnp.zeros((LANES - DH,), F32).at[:2].set(1.0)

    def keys_aug(t):
        return jnp.concatenate([t, jnp.broadcast_to(ones2, t.shape[:-1] + (LANES - DH,))],
                               axis=-1).astype(BF16)

    chunks = jnp.stack([by_group(k_c), by_group(v_c)]).reshape(2, bs, G, nc, NSA_CMP_STRIDE * DH)
    pe = jnp.stack([ck_pe, cv_pe]).reshape(2, 1, NSA_CMP_LEN * DH)
    cmp = nsa_compress(chunks.astype(BF16), jnp.broadcast_to(pe, (2, 8, NSA_CMP_LEN * DH)).astype(BF16),
                       jnp.stack([ck_w1, cv_w1]).astype(BF16), jnp.stack([ck_w2, cv_w2]).astype(BF16))
    kc, vc_t = keys_aug(cmp[0]), cmp[1].transpose(0, 1, 3, 2).astype(BF16)

    far = rel_bias[REL_BUCKETS - 1].astype(F32)
    far_hi = far.astype(BF16).astype(F32)
    extra = jnp.zeros((NSA_HEADS, LANES - DH), F32).at[:, 0].set(far_hi).at[:, 1].set(far - far_hi)
    qh = (q * (DH ** -0.5)).reshape(bs, nqt, QB, G, HG, DH).transpose(0, 3, 1, 4, 2, 5)
    ex = jnp.broadcast_to(extra.reshape(1, G, 1, HG, 1, LANES - DH), (bs, G, nqt, HG, QB, LANES - DH))
    qs = jnp.concatenate([qh, ex], axis=-1).astype(BF16).reshape(bs, G, nqt, HG * QB, LANES)
    gl = gate_logits.reshape(bs, nqt, QB, G, HG, 3).transpose(0, 3, 1, 5, 4, 2)
    gl = jnp.pad(gl.reshape(bs, G, nqt, 3, HG * QB), ((0, 0),) * 3 + ((0, 5), (0, 0)))

    td = (rel_bias[_bucket_table()] - rel_bias[REL_BUCKETS - 1][None, :]).T.astype(F32)
    cmp_end = jnp.minimum(jnp.arange(nc) * NSA_CMP_STRIDE + NSA_CMP_LEN - 1, s - 1)
    cpos = positions[cmp_end]
    qmin = positions.reshape(nqt, QB).min(axis=1)
    kmax = positions.reshape(nqt, QB).max(axis=1)
    near_s = (qmin[:, None] - positions.reshape(n_kt, NSA_KT).max(axis=1)[None, :]) < REL_MAX_DIST
    n_past = (jnp.arange(nqt) * QB + QB - 1) // NSA_KT
    n_fast = jnp.minimum(jnp.argmax(jnp.concatenate([near_s, jnp.ones((nqt, 1), bool)], axis=1), axis=1),
                         n_past)
    wt = jnp.clip(jnp.arange(nqt)[:, None] - (NSA_WTILES - 1) + jnp.arange(NSA_WTILES)[None, :], 0, nqt - 1)
    near_w = (qmin[:, None] - kmax[wt]) < REL_MAX_DIST
    near_c = (qmin[:, None] - cpos.reshape(n_ct, LANES).max(axis=1)[None, :]) < REL_MAX_DIST
    w_std = ((jnp.arange(nqt) >= NSA_WTILES - 1)
             & jnp.all(near_w == (jnp.arange(NSA_WTILES) >= NSA_WTILES - 2)[None, :], axis=1))

    cmp_start = np.arange(nc) * NSA_CMP_STRIDE
    sel_start = np.arange(LANES) * NSA_SEL_LEN
    agg_t = ((cmp_start[None, :] <= sel_start[:, None] + NSA_SEL_LEN - 1)
             & (cmp_start[None, :] + NSA_CMP_LEN - 1 >= sel_start[:, None])
             & (np.arange(nc)[None, :] < n_cmp) & (np.arange(LANES)[:, None] < n_sel))

    cols = HG * QB
    full = lambda shape: pl.BlockSpec(shape, lambda b, g, i, *_: (0,) * len(shape))
    per_bg = lambda n, w: pl.BlockSpec((1, 1, n, w), lambda b, g, i, *_: (b, g, 0, 0))
    grid_spec = pltpu.PrefetchScalarGridSpec(
        num_scalar_prefetch=4,
        grid=(bs, G, nqt),
        in_specs=[pl.BlockSpec((1, 1, 1, cols, LANES), lambda b, g, i, *_: (b, g, i, 0, 0)),
                  pl.BlockSpec((1, 1, 1, 8, cols), lambda b, g, i, *_: (b, g, i, 0, 0)),
                  pl.BlockSpec((QB, LANES), lambda b, g, i, *_: (i, 0)),
                  full((s, LANES)), full((nc, LANES)),
                  per_bg(s, LANES), per_bg(DH, s), per_bg(s, LANES), per_bg(DH, s),
                  per_bg(nc, LANES), per_bg(DH, nc),
                  full((LANES, nc)), full((NSA_HEADS, LANES))],
        out_specs=pl.BlockSpec((1, 1, 1, DH, cols), lambda b, g, i, *_: (b, g, i, 0, 0)),
        scratch_shapes=[pltpu.VMEM((2, 1, cols), F32), pltpu.VMEM((2, 1, cols), F32),
                        pltpu.VMEM((2, DH, cols), F32), pltpu.VMEM((LANES, QB), F32)],
    )
    out = pl.pallas_call(
        functools.partial(_nsa_body, n_sel=n_sel, sel_k=sel_k),
        grid_spec=grid_spec,
        out_shape=jax.ShapeDtypeStruct((bs, G, nqt, DH, cols), F32),
        compiler_params=_params("parallel", "parallel", "arbitrary"),
        name="nsa_attention",
    )(n_fast.astype(jnp.int32), w_std.astype(jnp.int32), near_w.reshape(-1).astype(jnp.int32),
      near_c.reshape(-1).astype(jnp.int32),
      qs, gl, _pos_digits(positions, False), _pos_digits(positions, True), _pos_digits(cpos, True),
      keys_aug(by_group(k_s)), by_group_t(v_s), keys_aug(by_group(k_w)), by_group_t(v_w),
      kc, vc_t, jnp.asarray(agg_t, BF16), td)
    out = out.reshape(bs, G, nqt, DH, HG, QB).transpose(0, 2, 5, 1, 4, 3)
    return out.reshape(bs, s, G * HG * DH)


def _nsa_body_rowmajor(near_s, near_w, near_c, q_ref, gl_ref, posq_ref, posk_ref, cpos_ref,
              ks_ref, vs_ref, kw_ref, vw_ref, kc_ref, vc_ref, agg_ref, td_ref, o_ref,
              s_ref, m_ref, l_ref, acc_ref, *, n_sel, sel_k):
    QB, HG = NSA_QBLOCK, NSA_HPG
    g = pl.program_id(1)
    qi = pl.program_id(2)
    s0 = qi * QB
    nc = kc_ref.shape[2]
    n_ct = nc // LANES
    n_kt = ks_ref.shape[2] // NSA_KT
    q = q_ref[0, 0, 0]
    tp = posq_ref[...]
    row = lax.broadcasted_iota(jnp.int32, (QB, LANES), 0)
    lane = lax.broadcasted_iota(jnp.int32, (QB, LANES), 1)
    t_tok = s0 + row

    def add_delta(col0, pk_row):
        idx = jnp.clip(tp - pk_row, 0, LANES - 1)
        for hg in range(HG):
            tb = jnp.broadcast_to(td_ref[pl.ds(g * HG + hg, 1), :], (QB, LANES))
            s_ref[hg * QB:(hg + 1) * QB, col0:col0 + LANES] += jnp.take_along_axis(tb, idx, axis=1)

    def heads(x):
        return jnp.concatenate([x] * HG, axis=0)

    gate = jax.nn.sigmoid(gl_ref[0, 0, 0])

    s_ref[:, :nc] = _dot_nt(q, kc_ref[0, 0])
    for ct in range(n_ct):
        @pl.when(near_c[qi * n_ct + ct] != 0)
        def _():
            add_delta(ct * LANES, cpos_ref[ct:ct + 1, :])
    c_id = lax.broadcasted_iota(jnp.int32, (QB, nc), 1)
    c_row = lax.broadcasted_iota(jnp.int32, (QB, nc), 0)
    ok_c = heads(jnp.where(c_id * NSA_CMP_STRIDE + (NSA_CMP_LEN - 1) <= s0 + c_row, 1.0, 0.0))
    sc = jnp.where(ok_c > 0.5, s_ref[:, :nc], NSA_NEG)
    e = jnp.exp(sc - jnp.max(sc, axis=-1, keepdims=True)) * ok_c
    p = e * (1.0 / jnp.maximum(jnp.sum(e, axis=-1, keepdims=True), 1e-30))
    o_c = jnp.dot(p.astype(BF16), vc_ref[0, 0], preferred_element_type=F32)
    o_ref[0, 0, 0] = gate[:, 0:1] * o_c
    ps = p[0:QB]
    for hg in range(1, HG):
        ps = ps + p[hg * QB:(hg + 1) * QB]
    ps_hi = ps.astype(BF16)
    ps_lo = (ps - ps_hi.astype(F32)).astype(BF16)
    imp = (jnp.dot(ps_hi, agg_ref[...], preferred_element_type=F32)
           + jnp.dot(ps_lo, agg_ref[...], preferred_element_type=F32))
    tb_blk = t_tok // NSA_SEL_LEN
    forced = jnp.where(lane == 0, 1.0, 0.0) + jnp.where(lane == tb_blk, 1.0, 0.0) \
        + jnp.where(lane == tb_blk - 1, 1.0, 0.0)
    score = jnp.where(forced > 0.5, 1e9, jnp.where(lane * NSA_SEL_LEN <= t_tok, imp, -1e9))
    score = jnp.where(lane < n_sel, score, -jnp.inf)
    lanef = lane.astype(F32)
    sel = jnp.zeros((QB, LANES), F32)
    for _ in range(sel_k):
        mx = jnp.max(score, axis=-1, keepdims=True)
        first = jnp.min(jnp.where(score == mx, lanef, float(LANES)), axis=-1, keepdims=True)
        hit = lanef == first
        sel = jnp.where(hit, 1.0, sel)
        score = jnp.where(hit, -jnp.inf, score)
    sel_b = sel.astype(BF16)

    def reset():
        m_ref[...] = jnp.full_like(m_ref, NSA_NEG)
        l_ref[...] = jnp.zeros_like(l_ref)
        acc_ref[...] = jnp.zeros_like(acc_ref)

    def tile_step(width, mask_add, v):
        s = s_ref[:, :width]
        if mask_add is not None:
            s = s + heads(mask_add)
        m_old = m_ref[...]
        m_new = jnp.maximum(m_old, jnp.max(s, axis=-1, keepdims=True))
        alpha = jnp.exp(m_old - m_new)
        p = jnp.exp(s - m_new)
        psum = p[:, :LANES]
        for c in range(1, width // LANES):
            psum = psum + p[:, c * LANES:(c + 1) * LANES]
        l_ref[...] = l_ref[...] * alpha + psum
        acc_ref[...] = acc_ref[...] * alpha + jnp.dot(p.astype(BF16), v, preferred_element_type=F32)
        m_ref[...] = m_new

    def finish():
        return acc_ref[...] * (1.0 / jnp.sum(l_ref[...], axis=-1, keepdims=True))

    blk_i = lax.broadcasted_iota(jnp.int32, (LANES, NSA_KT), 0)
    key_i = lax.broadcasted_iota(jnp.int32, (LANES, NSA_KT), 1)
    key_q = lax.broadcasted_iota(jnp.int32, (QB, NSA_KT), 1)
    row_q = lax.broadcasted_iota(jnp.int32, (QB, NSA_KT), 0)
    per_tile = NSA_KT // NSA_SEL_LEN

    def sel_tile(kt, diag):
        k0 = pl.multiple_of(kt * NSA_KT, NSA_KT)
        s_ref[:, :NSA_KT] = _dot_nt(q, ks_ref[0, 0, pl.ds(k0, NSA_KT), :])

        @pl.when(near_s[qi * n_kt + kt] != 0)
        def _():
            for c in range(NSA_KT // LANES):
                add_delta(c * LANES, posk_ref[pl.ds(kt * (NSA_KT // LANES) + c, 1), :])

        expand = jnp.where(blk_i == kt * per_tile + key_i // NSA_SEL_LEN, 1.0, 0.0).astype(BF16)
        picked = jnp.dot(sel_b, expand, preferred_element_type=F32)
        if diag:
            picked = jnp.where(k0 + key_q <= s0 + row_q, picked, 0.0)
        tile_step(NSA_KT, jnp.where(picked > 0.5, 0.0, NSA_NEG), vs_ref[0, 0, pl.ds(k0, NSA_KT), :])

    reset()
    n_past = (s0 + QB - 1) // NSA_KT

    def past_body(kt, carry):
        sel_tile(kt, False)
        return carry

    lax.fori_loop(0, n_past, past_body, 0)
    sel_tile(n_past, True)
    o_ref[0, 0, 0] += gate[:, 1:2] * finish()

    reset()
    for j in range(NSA_WTILES):
        start = s0 - NSA_WINDOW + j * LANES

        @pl.when(start >= 0)
        def _():
            st = pl.multiple_of(start, LANES)
            s_ref[:, :LANES] = _dot_nt(q, kw_ref[0, 0, pl.ds(st, LANES), :])

            @pl.when(near_w[qi * NSA_WTILES + j] != 0)
            def _():
                add_delta(0, posk_ref[pl.ds(qi - (NSA_WTILES - 1) + j, 1), :])

            if j == 0:
                mask_add = jnp.where(lane > row, 0.0, NSA_NEG)
            elif j == NSA_WTILES - 1:
                mask_add = jnp.where(lane <= row, 0.0, NSA_NEG)
            else:
                mask_add = None
            tile_step(LANES, mask_add, vw_ref[0, 0, pl.ds(st, LANES), :])

    o_ref[0, 0, 0] += gate[:, 2:3] * finish()


def _nsa_mixer_rowmajor(q, k_c, v_c, k_s, v_s, k_w, v_w, gate_logits, positions,
                        ck_pe, ck_w1, ck_w2, cv_pe, cv_w1, cv_w2, rel_bias):
    bs, s, _ = q.shape
    G, HG, DH, QB = NSA_KV_GROUPS, NSA_HPG, NSA_DH, NSA_QBLOCK
    nqt = s // QB
    nc = s // NSA_CMP_STRIDE
    n_cmp = (s - NSA_CMP_LEN) // NSA_CMP_STRIDE + 1
    n_sel = s // NSA_SEL_LEN
    sel_k = min(NSA_SEL_TOPK, n_sel)
    n_kt = s // NSA_KT
    n_ct = nc // LANES
    assert s % (LANES * NSA_CMP_STRIDE) == 0 and n_sel <= LANES and NSA_KT == 2 * QB

    def by_group(t):
        return t.reshape(bs, s, G, DH).transpose(0, 2, 1, 3)

    ones2 = jnp.zeros((LANES - DH,), F32).at[:2].set(1.0)

    def keys_aug(t):
        return jnp.concatenate([t, jnp.broadcast_to(ones2, t.shape[:-1] + (LANES - DH,))],
                               axis=-1).astype(BF16)

    chunks = jnp.stack([by_group(k_c), by_group(v_c)]).reshape(2, bs, G, nc, NSA_CMP_STRIDE * DH)
    pe = jnp.stack([ck_pe, cv_pe]).reshape(2, 1, NSA_CMP_LEN * DH)
    cmp = nsa_compress(chunks.astype(BF16), jnp.broadcast_to(pe, (2, 8, NSA_CMP_LEN * DH)).astype(BF16),
                       jnp.stack([ck_w1, cv_w1]).astype(BF16), jnp.stack([ck_w2, cv_w2]).astype(BF16))
    kc, vc = keys_aug(cmp[0]), cmp[1].astype(BF16)

    far = rel_bias[REL_BUCKETS - 1].astype(F32)
    far_hi = far.astype(BF16).astype(F32)
    extra = jnp.zeros((NSA_HEADS, LANES - DH), F32).at[:, 0].set(far_hi).at[:, 1].set(far - far_hi)
    qh = (q * (DH ** -0.5)).reshape(bs, nqt, QB, G, HG, DH).transpose(0, 3, 1, 4, 2, 5)
    ex = jnp.broadcast_to(extra.reshape(1, G, 1, HG, 1, LANES - DH), (bs, G, nqt, HG, QB, LANES - DH))
    qs = jnp.concatenate([qh, ex], axis=-1).astype(BF16).reshape(bs, G, nqt, HG * QB, LANES)
    gl = gate_logits.reshape(bs, nqt, QB, G, HG, 3).transpose(0, 3, 1, 4, 2, 5)
    gl = jnp.pad(gl, ((0, 0),) * 5 + ((0, 5),)).reshape(bs, G, nqt, HG * QB, 8)

    td = (rel_bias[_bucket_table()] - rel_bias[REL_BUCKETS - 1][None, :]).T.astype(F32)
    cmp_end = jnp.minimum(jnp.arange(nc) * NSA_CMP_STRIDE + NSA_CMP_LEN - 1, s - 1)
    cpos = positions[cmp_end]
    qmin = positions.reshape(nqt, QB).min(axis=1)
    kmax = positions.reshape(nqt, QB).max(axis=1)
    near_s = (qmin[:, None] - positions.reshape(n_kt, NSA_KT).max(axis=1)[None, :]) < REL_MAX_DIST
    wt = jnp.clip(jnp.arange(nqt)[:, None] - (NSA_WTILES - 1) + jnp.arange(NSA_WTILES)[None, :], 0, nqt - 1)
    near_w = (qmin[:, None] - kmax[wt]) < REL_MAX_DIST
    near_c = (qmin[:, None] - cpos.reshape(n_ct, LANES).max(axis=1)[None, :]) < REL_MAX_DIST

    cmp_start = np.arange(nc) * NSA_CMP_STRIDE
    sel_start = np.arange(LANES) * NSA_SEL_LEN
    agg = ((cmp_start[:, None] <= sel_start[None, :] + NSA_SEL_LEN - 1)
           & (cmp_start[:, None] + NSA_CMP_LEN - 1 >= sel_start[None, :])
           & (np.arange(nc)[:, None] < n_cmp) & (np.arange(LANES)[None, :] < n_sel))

    rows = HG * QB
    full = lambda shape: pl.BlockSpec(shape, lambda b, g, i, *_: (0,) * len(shape))
    per_bg = lambda n, w: pl.BlockSpec((1, 1, n, w), lambda b, g, i, *_: (b, g, 0, 0))
    grid_spec = pltpu.PrefetchScalarGridSpec(
        num_scalar_prefetch=3,
        grid=(bs, G, nqt),
        in_specs=[pl.BlockSpec((1, 1, 1, rows, LANES), lambda b, g, i, *_: (b, g, i, 0, 0)),
                  pl.BlockSpec((1, 1, 1, rows, 8), lambda b, g, i, *_: (b, g, i, 0, 0)),
                  pl.BlockSpec((QB, 1), lambda b, g, i, *_: (i, 0)),
                  full((nqt, LANES)), full((n_ct, LANES)),
                  per_bg(s, LANES), per_bg(s, DH), per_bg(s, LANES), per_bg(s, DH),
                  per_bg(nc, LANES), per_bg(nc, DH),
                  full((nc, LANES)), full((NSA_HEADS, LANES))],
        out_specs=pl.BlockSpec((1, 1, 1, rows, DH), lambda b, g, i, *_: (b, g, i, 0, 0)),
        scratch_shapes=[pltpu.VMEM((rows, max(nc, NSA_KT)), F32), pltpu.VMEM((rows, 1), F32),
                        pltpu.VMEM((rows, LANES), F32), pltpu.VMEM((rows, DH), F32)],
    )
    out = pl.pallas_call(
        functools.partial(_nsa_body, n_sel=n_sel, sel_k=sel_k),
        grid_spec=grid_spec,
        out_shape=jax.ShapeDtypeStruct((bs, G, nqt, rows, DH), F32),
        compiler_params=_params("parallel", "parallel", "arbitrary"),
        name="nsa_attention",
    )(near_s.reshape(-1).astype(jnp.int32), near_w.reshape(-1).astype(jnp.int32),
      near_c.reshape(-1).astype(jnp.int32),
      qs, gl, positions.reshape(s, 1), positions.reshape(nqt, LANES), cpos.reshape(n_ct, LANES),
      keys_aug(by_group(k_s)), by_group(v_s).astype(BF16), keys_aug(by_group(k_w)),
      by_group(v_w).astype(BF16), kc, vc, jnp.asarray(agg, BF16), td)
    out = out.reshape(bs, G, nqt, HG, QB, DH).transpose(0, 2, 4, 1, 3, 5)
    return out.reshape(bs, s, G * HG * DH)


def _nsa_mixer_jax(q, k_c, v_c, k_s, v_s, k_w, v_w, gate_logits, positions,
                   ck_pe, ck_w1, ck_w2, cv_pe, cv_w1, cv_w2, rel_bias):
    bs, s, _ = q.shape
    G, HG, DH, QB, W = NSA_KV_GROUPS, NSA_HPG, NSA_DH, NSA_QBLOCK, NSA_WINDOW
    q = q.reshape(bs, s, G, HG, DH) * (DH ** -0.5)

    def kv(t):
        return t.reshape(bs, s, G, DH)

    k_c, v_c, k_s, v_s, k_w, v_w = (kv(t) for t in (k_c, v_c, k_s, v_s, k_w, v_w))
    gates = jax.nn.sigmoid(gate_logits).reshape(bs, s, G, HG, 3)
    n_cmp = (s - NSA_CMP_LEN) // NSA_CMP_STRIDE + 1
    cmp_start = jnp.arange(n_cmp) * NSA_CMP_STRIDE
    cmp_end = cmp_start + NSA_CMP_LEN - 1
    cmp_tok = cmp_start[:, None] + jnp.arange(NSA_CMP_LEN)[None, :]

    def compress(t, pe, w1, w2):
        blk = t[:, cmp_tok] + pe[None, None, :, None, :]
        blk = blk.transpose(0, 1, 3, 2, 4).reshape(bs, n_cmp, G, NSA_CMP_LEN * DH)
        return jax.nn.gelu(blk @ w1) @ w2

    kc = compress(k_c, ck_pe, ck_w1, ck_w2)
    vc = compress(v_c, cv_pe, cv_w1, cv_w2)
    cmp_pos = positions[cmp_end]
    n_sel = s // NSA_SEL_LEN
    sel_k = min(NSA_SEL_TOPK, n_sel)
    sel_start = jnp.arange(n_sel) * NSA_SEL_LEN
    agg = ((cmp_start[:, None] <= sel_start[None, :] + NSA_SEL_LEN - 1)
           & (cmp_end[:, None] >= sel_start[None, :])).astype(F32)
    ks_blk = k_s.reshape(bs, n_sel, NSA_SEL_LEN, G, DH).transpose(0, 3, 1, 2, 4)
    vs_blk = v_s.reshape(bs, n_sel, NSA_SEL_LEN, G, DH).transpose(0, 3, 1, 2, 4)
    pos_blk = positions.reshape(n_sel, NSA_SEL_LEN)
    kw_pad = jnp.pad(k_w, ((0, 0), (W, 0), (0, 0), (0, 0)))
    vw_pad = jnp.pad(v_w, ((0, 0), (W, 0), (0, 0), (0, 0)))
    pos_pad = jnp.pad(positions, (W, 0))
    tbl = rel_bias.reshape(REL_BUCKETS, G, HG)
    b_ix = jnp.arange(bs)[:, None, None, None]
    g_ix = jnp.arange(G)[None, :, None, None]
    sel_offs = jnp.arange(NSA_SEL_LEN)
    win_offs = jnp.arange(QB + W)
    blk_j = jnp.arange(n_sel)

    def one_block(qi):
        s0 = qi * QB
        qb = lax.dynamic_slice_in_dim(q, s0, QB, axis=1)
        t = s0 + jnp.arange(QB)
        tp = lax.dynamic_slice_in_dim(positions, s0, QB)
        bias_c = tbl[_rel_bucket(tp[:, None] - cmp_pos[None, :])].transpose(2, 3, 0, 1)
        lg_c = jnp.einsum('bqghd,bcgd->bghqc', qb, kc) + bias_c
        p_c = _masked_softmax(lg_c, cmp_end[None, :] <= t[:, None])
        o_c = jnp.einsum('bghqc,bcgd->bqghd', p_c, vc)
        imp = jnp.einsum('bghqc,cj->bgqj', p_c, agg)
        tb = t // NSA_SEL_LEN
        forced = ((blk_j[None, :] == 0) | (blk_j[None, :] == tb[:, None])
                  | (blk_j[None, :] == tb[:, None] - 1))
        eligible = sel_start[None, :] <= t[:, None]
        score = jnp.where(forced, 1e9, jnp.where(eligible, imp, -1e9))
        _, sel = lax.top_k(score, sel_k)
        k_sel = ks_blk[b_ix, g_ix, sel]
        v_sel = vs_blk[b_ix, g_ix, sel]
        tok_s = sel[..., None] * NSA_SEL_LEN + sel_offs
        bias_s = tbl[_rel_bucket(tp[:, None, None] - pos_blk[sel]), g_ix[..., None]]
        lg_s = jnp.einsum('bqghd,bgqnkd->bghqnk', qb, k_sel) + bias_s.transpose(0, 1, 5, 2, 3, 4)
        valid_s = (tok_s <= t[:, None, None])[:, :, None].reshape(bs, G, 1, QB, -1)
        p_s = _masked_softmax(lg_s.reshape(bs, G, HG, QB, -1), valid_s)
        p_s = p_s.reshape(bs, G, HG, QB, sel_k, NSA_SEL_LEN)
        o_s = jnp.einsum('bghqnk,bgqnkd->bqghd', p_s, v_sel)
        kwb = lax.dynamic_slice_in_dim(kw_pad, s0, QB + W, axis=1)
        vwb = lax.dynamic_slice_in_dim(vw_pad, s0, QB + W, axis=1)
        tok_w = s0 - W + win_offs
        pos_w = lax.dynamic_slice_in_dim(pos_pad, s0, QB + W)
        d_tok = t[:, None] - tok_w[None, :]
        valid_w = (d_tok >= 0) & (d_tok < W) & (tok_w[None, :] >= 0)
        bias_w = tbl[_rel_bucket(tp[:, None] - pos_w[None, :])].transpose(2, 3, 0, 1)
        lg_w = jnp.einsum('bqghd,bkgd->bghqk', qb, kwb) + bias_w
        p_w = _masked_softmax(lg_w, valid_w)
        o_w = jnp.einsum('bghqk,bkgd->bqghd', p_w, vwb)
        gb = lax.dynamic_slice_in_dim(gates, s0, QB, axis=1)
        return gb[..., 0:1] * o_c + gb[..., 1:2] * o_s + gb[..., 2:3] * o_w

    out = lax.map(one_block, jnp.arange(s // QB))
    return out.transpose(1, 0, 2, 3, 4, 5).reshape(bs, s, G * HG * DH)


SGU_TC = 512


def _sgu_body(uv_ref, lg_ref, lb_ref, w_ref, b_ref, o_ref):
    uv = jax.nn.gelu(uv_ref[...])
    u, v = uv[:, :SGU_WIDTH], uv[:, SGU_WIDTH:]
    mu = jnp.mean(v, axis=-1, keepdims=True)
    vc = v - mu
    var = jnp.mean(vc * vc, axis=-1, keepdims=True)
    vn = (vc * lax.rsqrt(var + RMS_EPS) * lg_ref[...] + lb_ref[...]).astype(BF16)
    gw = SGU_WIDTH // SGU_GROUPS
    for c in range(SGU_TC // SGU_CHUNK):
        rows = slice(c * SGU_CHUNK, (c + 1) * SGU_CHUNK)
        for g in range(SGU_GROUPS):
            cols = slice(g * gw, (g + 1) * gw)
            mix = jnp.dot(w_ref[g], vn[rows, cols], preferred_element_type=F32) + b_ref[:, g:g + 1]
            o_ref[rows, cols] = u[rows, cols] * mix


def sgu_mixer(proj, ln_g, ln_b, w_s, b_s):
    t = proj.shape[0]
    assert t % SGU_TC == 0 and OFF_SGU % (2 * SGU_WIDTH) == 0
    fixed = lambda i: (0, 0)
    return pl.pallas_call(
        _sgu_body,
        grid=(t // SGU_TC,),
        in_specs=[pl.BlockSpec((SGU_TC, 2 * SGU_WIDTH), lambda i: (i, OFF_SGU // (2 * SGU_WIDTH))),
                  pl.BlockSpec((1, SGU_WIDTH), fixed), pl.BlockSpec((1, SGU_WIDTH), fixed),
                  pl.BlockSpec((SGU_GROUPS, SGU_CHUNK, SGU_CHUNK), lambda i: (0, 0, 0)),
                  pl.BlockSpec((SGU_CHUNK, SGU_GROUPS), fixed)],
        out_specs=pl.BlockSpec((SGU_TC, SGU_WIDTH), lambda i: (i, 0)),
        out_shape=jax.ShapeDtypeStruct((t, SGU_WIDTH), F32),
        compiler_params=_params("parallel"),
        name="sgu_gate",
    )(proj, ln_g.reshape(1, SGU_WIDTH), ln_b.reshape(1, SGU_WIDTH), jnp.tril(w_s).astype(BF16), b_s.T)


def _sgu_mixer_jax(uv, ln_g, ln_b, w_s, b_s):
    bs, s, _ = uv.shape
    uv = jax.nn.gelu(uv)
    u, v = uv[..., :SGU_WIDTH], uv[..., SGU_WIDTH:]
    mu = jnp.mean(v, axis=-1, keepdims=True)
    var = jnp.mean(jnp.square(v - mu), axis=-1, keepdims=True)
    v = (v - mu) * lax.rsqrt(var + RMS_EPS) * ln_g + ln_b
    nc = s // SGU_CHUNK
    v = v.reshape(bs, nc, SGU_CHUNK, SGU_GROUPS, SGU_WIDTH // SGU_GROUPS)
    mix = (jnp.einsum('gij,bcjgd->bcigd', jnp.tril(w_s), v) + b_s.T[None, None, :, :, None])
    return u * mix.reshape(bs, s, SGU_WIDTH)


def _permute_w_in(w):
    sizes = ([SSM_WIDTH, GDN_QKV, GDN_HEADS, GDN_HEADS, GDN_HEADS * GDN_DV, NSA_Q]
             + [NSA_KV] * 6 + [3 * NSA_HEADS, 2 * SGU_WIDTH, N_BRANCH * D_MODEL])
    cuts = [int(c) for c in np.cumsum(sizes)[:-1]]
    (w_ssm, w_qkv, w_a, w_b, w_z, w_nq, w_kc, w_vc, w_ks, w_vs, w_kw, w_vw,
     w_ng, w_sgu, w_gate) = jnp.split(w, cuts, axis=-1)
    pad = jnp.zeros((w.shape[0], PROJ_COLS - OFF_SMALL - SMALL_USED), w.dtype)
    return jnp.concatenate([w_qkv, w_ssm, w_z, w_nq, w_sgu, w_gate, w_kc, w_vc, w_ks, w_vs,
                            w_kw, w_vw, w_a, w_b, w_ng, pad], axis=-1)


def kernel(x, positions, norm_mix, norm_ffn, norm_final, w_in, ssm_a_re, ssm_a_im, ssm_log_dt, ssm_b_re, ssm_b_im, ssm_c_re, ssm_c_im, ssm_d, ssm_glu_w, gdn_conv_w, gdn_a_log, gdn_dt_bias, gdn_norm, nsa_cmp_k_pe, nsa_cmp_k_w1, nsa_cmp_k_w2, nsa_cmp_v_pe, nsa_cmp_v_w1, nsa_cmp_v_w2, rel_bias, sgu_ln_g, sgu_ln_b, sgu_w, sgu_b, w_br_ssm, w_br_gdn, w_br_nsa, w_br_sgu, w_out, ffn_w_gate, ffn_w_up, ffn_w_down, moe_router, moe_w_gate, moe_w_up, moe_w_down):
    bs, s, d = x.shape
    t = bs * s
    xf = x.reshape(t, d)
    for layer in range(DEPTH):
        proj = in_proj(xf, norm_mix[layer], _permute_w_in(w_in[layer]))
        p3 = proj.reshape(bs, s, PROJ_COLS)

        def seg(off, width):
            return p3[..., off:off + width]

        y_ssm = ssm_mixer(seg(OFF_SSM, SSM_WIDTH), ssm_a_re[layer], ssm_a_im[layer],
                          ssm_log_dt[layer], ssm_b_re[layer], ssm_b_im[layer], ssm_c_re[layer],
                          ssm_c_im[layer], ssm_d[layer], ssm_glu_w[layer])
        y_gdn = gdn_mixer(proj, bs, gdn_conv_w[layer], gdn_a_log[layer], gdn_dt_bias[layer],
                          gdn_norm[layer])
        kvs = [seg(OFF_KV + i * NSA_KV, NSA_KV) for i in range(6)]
        y_nsa = nsa_mixer(seg(OFF_NQ, NSA_Q), *kvs, seg(OFF_SMALL + 2 * GDN_HEADS, 3 * NSA_HEADS),
                          positions, nsa_cmp_k_pe[layer], nsa_cmp_k_w1[layer], nsa_cmp_k_w2[layer],
                          nsa_cmp_v_pe[layer], nsa_cmp_v_w1[layer], nsa_cmp_v_w2[layer], rel_bias)
        y_sgu = sgu_mixer(proj, sgu_ln_g[layer], sgu_ln_b[layer], sgu_w[layer], sgu_b[layer])
        ys = [y_ssm.reshape(t, -1), y_gdn, y_nsa.reshape(t, -1), y_sgu]
        ws = [w[layer].astype(BF16) for w in (w_br_ssm, w_br_gdn, w_br_nsa, w_br_sgu)]
        xf = merge(xf, proj, ys, ws, w_out[layer].astype(BF16))
        i = layer // 2
        if layer % 2 == 0:
            xf = dense_ffn(xf, norm_ffn[layer], ffn_w_gate[i], ffn_w_up[i], ffn_w_down[i])
        else:
            xf = moe_layer(xf, norm_ffn[layer], moe_router[i], moe_w_gate[i], moe_w_up[i],
                           moe_w_down[i], norm_final)
    return xf.reshape(bs, s, d)
```

```python
import functools
import math

import jax
import jax.numpy as jnp
from jax import lax
import numpy as np
from jax.experimental import pallas as pl
from jax.experimental.pallas import tpu as pltpu

F32 = jnp.float32
BF16 = jnp.bfloat16

D_MODEL = 1024
DEPTH = 2
SSM_WIDTH = D_MODEL // 2
SSM_GROUP = 16
SSM_GROUPS = SSM_WIDTH // SSM_GROUP
SSM_STATE = 64
GDN_HEADS = 4
GDN_DK = 128
GDN_DV = 128
GDN_CONV = 4
GDN_CHUNK = 64
NSA_HEADS = 8
NSA_KV_GROUPS = 2
NSA_HPG = NSA_HEADS // NSA_KV_GROUPS
NSA_DH = 64
NSA_CMP_LEN = 32
NSA_CMP_STRIDE = 16
NSA_CMP_HIDDEN = 128
NSA_SEL_LEN = 64
NSA_SEL_TOPK = 16
NSA_WINDOW = 512
NSA_QBLOCK = 128
SGU_WIDTH = D_MODEL // 2
SGU_GROUPS = 4
SGU_CHUNK = 128
REL_BUCKETS = 32
REL_MAX_DIST = 128
FFN_DENSE = 2816
N_EXPERTS = 8
TOP_K = 2
FFN_EXPERT = 3584
N_BRANCH = 4
RMS_EPS = 1e-6
GDN_QKV = GDN_HEADS * (2 * GDN_DK + GDN_DV)
NSA_Q = NSA_HEADS * NSA_DH
NSA_KV = NSA_KV_GROUPS * NSA_DH

LANES = 128
VMEM_LIMIT = 48 * 1024 * 1024

OFF_QKV = 0
OFF_SSM = OFF_QKV + GDN_QKV
OFF_Z = OFF_SSM + SSM_WIDTH
OFF_NQ = OFF_Z + GDN_HEADS * GDN_DV
OFF_SGU = OFF_NQ + NSA_Q
OFF_GATE = OFF_SGU + 2 * SGU_WIDTH
OFF_KV = OFF_GATE + N_BRANCH * D_MODEL
OFF_SMALL = OFF_KV + 6 * NSA_KV
SMALL_USED = 2 * GDN_HEADS + 3 * NSA_HEADS
PROJ_COLS = 9216


def _params(*sem):
    return pltpu.CompilerParams(dimension_semantics=sem, vmem_limit_bytes=VMEM_LIMIT)


def _rms(x, g):
    return x * lax.rsqrt(jnp.mean(x * x, axis=-1, keepdims=True) + RMS_EPS) * g


def _in_proj_body(x_ref, g_ref, w_ref, o_ref, h_ref):
    @pl.when(pl.program_id(1) == 0)
    def _():
        h_ref[...] = _rms(x_ref[...], g_ref[...]).astype(BF16)

    o_ref[...] = jnp.dot(h_ref[...], w_ref[...].astype(BF16), preferred_element_type=F32)


def in_proj(x, g, w, tm=1024, tn=512):
    t, d = x.shape
    n = w.shape[1]
    return pl.pallas_call(
        _in_proj_body,
        grid=(t // tm, n // tn),
        in_specs=[pl.BlockSpec((tm, d), lambda i, j: (i, 0)),
                  pl.BlockSpec((1, d), lambda i, j: (0, 0)),
                  pl.BlockSpec((d, tn), lambda i, j: (0, j))],
        out_specs=pl.BlockSpec((tm, tn), lambda i, j: (i, j)),
        out_shape=jax.ShapeDtypeStruct((t, n), F32),
        scratch_shapes=[pltpu.VMEM((tm, d), BF16)],
        compiler_params=_params("parallel", "arbitrary"),
        name="in_proj",
    )(x, g.reshape(1, d), w)


def _merge_body(x_ref, gate_ref, ya_ref, yb_ref, yc_ref, yd_ref,
                wa_ref, wb_ref, wc_ref, wd_ref, wo_ref, o_ref):
    def branch(k, y_ref, w_ref):
        p = jnp.dot(y_ref[...].astype(BF16), w_ref[...], preferred_element_type=F32)
        return jax.nn.sigmoid(gate_ref[:, k * D_MODEL:(k + 1) * D_MODEL]) * p

    merged = (branch(0, ya_ref, wa_ref) + branch(1, yb_ref, wb_ref)
              + branch(2, yc_ref, wc_ref) + branch(3, yd_ref, wd_ref))
    o_ref[...] = x_ref[...] + jnp.dot(merged.astype(BF16), wo_ref[...],
                                      preferred_element_type=F32)


def merge(x, proj, ys, ws, w_out, tm=256):
    t, d = x.shape
    gate_blk = OFF_GATE // (N_BRANCH * D_MODEL)
    row = lambda i: (i, 0)
    fixed = lambda i: (0, 0)
    in_specs = [pl.BlockSpec((tm, d), row),
                pl.BlockSpec((tm, N_BRANCH * D_MODEL), lambda i: (i, gate_blk))]
    in_specs += [pl.BlockSpec((tm, y.shape[1]), row) for y in ys]
    in_specs += [pl.BlockSpec(w.shape, fixed) for w in ws]
    in_specs += [pl.BlockSpec(w_out.shape, fixed)]
    return pl.pallas_call(
        _merge_body,
        grid=(t // tm,),
        in_specs=in_specs,
        out_specs=pl.BlockSpec((tm, d), row),
        out_shape=jax.ShapeDtypeStruct((t, d), F32),
        compiler_params=_params("parallel"),
        name="merge",
    )(x, proj, *ys, *ws, w_out)


def _ffn_body(x_ref, g_ref, wg_ref, wu_ref, wd_ref, o_ref, h_ref, acc_ref):
    f = pl.program_id(1)

    @pl.when(f == 0)
    def _():
        h_ref[...] = _rms(x_ref[...], g_ref[...]).astype(BF16)
        acc_ref[...] = x_ref[...]

    h = h_ref[...]
    a = jnp.dot(h, wg_ref[...].astype(BF16), preferred_element_type=F32)
    u = jnp.dot(h, wu_ref[...].astype(BF16), preferred_element_type=F32)
    act = (a * jax.nn.sigmoid(a) * u).astype(BF16)
    acc_ref[...] += jnp.dot(act, wd_ref[...].astype(BF16), preferred_element_type=F32)

    @pl.when(f == pl.num_programs(1) - 1)
    def _():
        o_ref[...] = acc_ref[...]


def dense_ffn(x, g, wg, wu, wd, tm=1024, tf=256):
    t, d = x.shape
    nf = wg.shape[1]
    return pl.pallas_call(
        _ffn_body,
        grid=(t // tm, nf // tf),
        in_specs=[pl.BlockSpec((tm, d), lambda i, f: (i, 0)),
                  pl.BlockSpec((1, d), lambda i, f: (0, 0)),
                  pl.BlockSpec((d, tf), lambda i, f: (0, f)),
                  pl.BlockSpec((d, tf), lambda i, f: (0, f)),
                  pl.BlockSpec((tf, d), lambda i, f: (f, 0))],
        out_specs=pl.BlockSpec((tm, d), lambda i, f: (i, 0)),
        out_shape=jax.ShapeDtypeStruct((t, d), F32),
        scratch_shapes=[pltpu.VMEM((tm, d), BF16), pltpu.VMEM((tm, d), F32)],
        compiler_params=_params("parallel", "arbitrary"),
        name="dense_ffn",
    )(x, g.reshape(1, d), wg, wu, wd)


def _route_body(x_ref, g_ref, rhi_ref, rlo_ref, h_ref, idx_ref, wt_ref):
    h = _rms(x_ref[...], g_ref[...])
    hi = h.astype(BF16)
    lo = (h - hi.astype(F32)).astype(BF16)
    h_ref[...] = h
    logits = (jnp.dot(hi, rhi_ref[...], preferred_element_type=F32)
              + jnp.dot(hi, rlo_ref[...], preferred_element_type=F32)
              + jnp.dot(lo, rhi_ref[...], preferred_element_type=F32))
    col = lax.broadcasted_iota(jnp.int32, logits.shape, 1).astype(F32)
    neg = jnp.float32(-jnp.inf)
    lg = jnp.where(col < N_EXPERTS, logits, neg)
    m1 = jnp.max(lg, axis=-1, keepdims=True)
    i1 = jnp.min(jnp.where(lg == m1, col, float(LANES)), axis=-1, keepdims=True)
    lg2 = jnp.where(col == i1, neg, lg)
    m2 = jnp.max(lg2, axis=-1, keepdims=True)
    i2 = jnp.min(jnp.where(lg2 == m2, col, float(LANES)), axis=-1, keepdims=True)
    e = jnp.exp(m2 - m1)
    w1 = 1.0 / (1.0 + e)
    w2 = e / (1.0 + e)
    idx_ref[...] = jnp.where(col == 0, i1, jnp.where(col == 1, i2, 0.0)).astype(jnp.int32)
    wt_ref[...] = jnp.where(col == 0, w1, jnp.where(col == 1, w2, 0.0))


def moe_route(x, g, r_hi, r_lo, tm=512):
    t, d = x.shape
    row = lambda i: (i, 0)
    fixed = lambda i: (0, 0)
    return pl.pallas_call(
        _route_body,
        grid=(t // tm,),
        in_specs=[pl.BlockSpec((tm, d), row), pl.BlockSpec((1, d), fixed),
                  pl.BlockSpec((d, LANES), fixed), pl.BlockSpec((d, LANES), fixed)],
        out_specs=[pl.BlockSpec((tm, d), row), pl.BlockSpec((tm, LANES), row),
                   pl.BlockSpec((tm, LANES), row)],
        out_shape=[jax.ShapeDtypeStruct((t, d), F32),
                   jax.ShapeDtypeStruct((t, LANES), jnp.int32),
                   jax.ShapeDtypeStruct((t, LANES), F32)],
        compiler_params=_params("parallel"),
        name="moe_route",
    )(x, g.reshape(1, d), r_hi, r_lo)


def _experts_body(te_ref, nu_ref, src_ref, h_hbm, rw_ref, wg_ref, wu_ref, wd_ref, o_ref,
                  xbuf_ref, xs_ref, acc_ref, sem):
    i = pl.program_id(0)
    f = pl.program_id(1)
    tm = xs_ref.shape[0]
    n_used = nu_ref[0]
    used = i < n_used

    def gather(tile, slot):
        def row(r, carry):
            tok = src_ref[tile * tm + r]
            pltpu.make_async_copy(h_hbm.at[pl.ds(tok, 1), :], xbuf_ref.at[slot, pl.ds(r, 1), :],
                                  sem.at[slot]).start()
            return carry

        lax.fori_loop(0, tm, row, 0, unroll=8)

    @pl.when(f == 0)
    def _():
        acc_ref[...] = jnp.zeros_like(acc_ref)

        @pl.when(used)
        def _():
            slot = i % 2

            @pl.when(i == 0)
            def _():
                gather(0, 0)

            pltpu.make_async_copy(h_hbm.at[pl.ds(0, tm), :], xbuf_ref.at[slot], sem.at[slot]).wait()

            @pl.when(i + 1 < n_used)
            def _():
                gather(i + 1, 1 - slot)

            xs_ref[...] = xbuf_ref[slot].astype(BF16)

    @pl.when(used)
    def _():
        h = xs_ref[...]
        a = jnp.dot(h, wg_ref[0].astype(BF16), preferred_element_type=F32)
        u = jnp.dot(h, wu_ref[0].astype(BF16), preferred_element_type=F32)
        act = (a * jax.nn.sigmoid(a) * u).astype(BF16)
        acc_ref[...] += jnp.dot(act, wd_ref[0].astype(BF16), preferred_element_type=F32)

    @pl.when(f == pl.num_programs(1) - 1)
    def _():
        o_ref[...] = (acc_ref[...] * rw_ref[...]).astype(o_ref.dtype)


def moe_experts(tile_expert, n_used, src, h, row_w, wg, wu, wd, tm, tf=512):
    p = src.shape[0]
    d = h.shape[1]
    nf = wg.shape[2]
    grid_spec = pltpu.PrefetchScalarGridSpec(
        num_scalar_prefetch=3,
        grid=(p // tm, nf // tf),
        in_specs=[pl.BlockSpec(memory_space=pl.ANY),
                  pl.BlockSpec((tm, 1), lambda i, f, *_: (i, 0)),
                  pl.BlockSpec((1, d, tf), lambda i, f, te, *_: (te[i], 0, f)),
                  pl.BlockSpec((1, d, tf), lambda i, f, te, *_: (te[i], 0, f)),
                  pl.BlockSpec((1, tf, d), lambda i, f, te, *_: (te[i], f, 0))],
        out_specs=pl.BlockSpec((tm, d), lambda i, f, *_: (i, 0)),
        scratch_shapes=[pltpu.VMEM((2, tm, d), F32), pltpu.VMEM((tm, d), BF16),
                        pltpu.VMEM((tm, d), F32), pltpu.SemaphoreType.DMA((2,))],
    )
    return pl.pallas_call(
        _experts_body,
        grid_spec=grid_spec,
        out_shape=jax.ShapeDtypeStruct((p, d), BF16),
        compiler_params=_params("arbitrary", "arbitrary"),
        name="moe_experts",
    )(tile_expert, n_used, src, h, row_w, wg, wu, wd)


def _combine_norm_body(x_ref, ya_ref, yb_ref, g_ref, o_ref):
    o_ref[...] = _rms(x_ref[...] + ya_ref[...].astype(F32) + yb_ref[...].astype(F32), g_ref[...])


def combine_norm(x, ya, yb, g, tm=1024):
    t, d = x.shape
    row = lambda i: (i, 0)
    return pl.pallas_call(
        _combine_norm_body,
        grid=(t // tm,),
        in_specs=[pl.BlockSpec((tm, d), row)] * 3 + [pl.BlockSpec((1, d), lambda i: (0, 0))],
        out_specs=pl.BlockSpec((tm, d), row),
        out_shape=jax.ShapeDtypeStruct((t, d), F32),
        compiler_params=_params("parallel"),
        name="combine_norm",
    )(x, ya, yb, g.reshape(1, d))


def moe_layer(x, g_norm, router, wg, wu, wd, g_final, tm=1024):
    t, d = x.shape
    r = jnp.zeros((d, LANES), F32).at[:, :N_EXPERTS].set(router)
    r_hi = r.astype(BF16)
    r_lo = (r - r_hi.astype(F32)).astype(BF16)
    h, idx, wts = moe_route(x, g_norm, r_hi, r_lo)
    e_flat = jnp.concatenate([idx[:, 0], idx[:, 1]])
    w_flat = jnp.concatenate([wts[:, 0], wts[:, 1]])
    onehot = (e_flat[:, None] == jnp.arange(N_EXPERTS)[None, :]).astype(jnp.int32)
    csum = jnp.cumsum(onehot, axis=0)
    rank = jnp.sum((csum - onehot) * onehot, axis=1)
    counts = csum[-1]
    padded = ((counts + tm - 1) // tm) * tm
    ends = jnp.cumsum(padded)
    starts = ends - padded
    dest = starts[e_flat] + rank
    n_tiles = (TOP_K * t) // tm + N_EXPERTS
    p = n_tiles * tm
    tok = jnp.concatenate([jnp.arange(t, dtype=jnp.int32)] * TOP_K)
    src = jnp.zeros((p,), jnp.int32).at[dest].set(tok)
    row_w = jnp.zeros((p,), F32).at[dest].set(w_flat)
    tile_expert = jnp.minimum(
        jnp.searchsorted(ends, jnp.arange(n_tiles, dtype=jnp.int32) * tm, side="right"),
        N_EXPERTS - 1).astype(jnp.int32)
    n_used = (ends[-1] // tm).astype(jnp.int32).reshape(1)
    ys = moe_experts(tile_expert, n_used, src, h, row_w.reshape(p, 1), wg, wu, wd, tm)
    ya = jnp.take(ys, dest[:t], axis=0)
    yb = jnp.take(ys, dest[t:], axis=0)
    return combine_norm(x, ya, yb, g_final)


def _l2_norm(t):
    return t * lax.rsqrt(jnp.sum(t * t, axis=-1, keepdims=True) + 1e-6)


def _masked_softmax(logits, valid):
    logits = jnp.where(valid, logits.astype(F32), -1e30)
    return jax.nn.softmax(logits, axis=-1) * valid


def _rel_bucket(dist):
    n = jnp.maximum(dist, 0)
    max_exact = REL_BUCKETS // 2
    nf = jnp.maximum(n, 1).astype(F32)
    large = max_exact + (jnp.log(nf / max_exact) / math.log(REL_MAX_DIST / max_exact)
                         * (REL_BUCKETS - max_exact)).astype(jnp.int32)
    large = jnp.minimum(large, REL_BUCKETS - 1)
    return jnp.where(n < max_exact, n, large)


SSM_TC = 512
SSM_SUB = 128
SSM_HALF = 256
SSM_NSTATE = SSM_GROUPS * SSM_STATE


def _ssm_body(u_ref, bw_ref, cw_ref, d_ref, glu_ref, ar_ref, ai_ref, pr_ref, pi_ref, o_ref,
              xr_ref, xi_ref, cr_ref, ci_ref):
    @pl.when(pl.program_id(1) == 0)
    def _():
        cr_ref[...] = jnp.zeros_like(cr_ref)
        ci_ref[...] = jnp.zeros_like(ci_ref)

    u = u_ref[0]
    ub = u.astype(BF16)
    nblk = SSM_WIDTH // SSM_HALF
    sblk = SSM_NSTATE // nblk
    for k in range(nblk):
        bu = jnp.dot(ub[:, k * SSM_HALF:(k + 1) * SSM_HALF], bw_ref[k], preferred_element_type=F32)
        xr_ref[:, k * sblk:(k + 1) * sblk] = bu[:, :sblk]
        xi_ref[:, k * sblk:(k + 1) * sblk] = bu[:, sblk:]

    row = lax.broadcasted_iota(jnp.int32, (SSM_SUB, LANES), 0)
    n_steps = SSM_SUB.bit_length() - 1

    def shift(x, d):
        if d % 8 == 0:
            return jnp.concatenate([jnp.zeros((d, LANES), F32), x[:SSM_SUB - d]], axis=0)
        return jnp.where(row >= d, pltpu.roll(x, d, 0), 0.0)

    def strip(c, carry):
        col = pl.ds(pl.multiple_of(c * LANES, LANES), LANES)
        c_r = cr_ref[:, col]
        c_i = ci_ref[:, col]
        for sub in range(SSM_TC // SSM_SUB):
            rows = slice(sub * SSM_SUB, (sub + 1) * SSM_SUB)
            xr = xr_ref[rows, col]
            xi = xi_ref[rows, col]
            for i in range(n_steps):
                a_r = ar_ref[i:i + 1, col]
                a_i = ai_ref[i:i + 1, col]
                sr, si = shift(xr, 1 << i), shift(xi, 1 << i)
                xr, xi = xr + a_r * sr - a_i * si, xi + a_r * si + a_i * sr
            p_r = pr_ref[:, col]
            p_i = pi_ref[:, col]
            xr, xi = xr + p_r * c_r - p_i * c_i, xi + p_r * c_i + p_i * c_r
            xr_ref[rows, col] = xr
            xi_ref[rows, col] = xi
            c_r = xr[SSM_SUB - 1:SSM_SUB, :]
            c_i = xi[SSM_SUB - 1:SSM_SUB, :]
        cr_ref[:, col] = c_r
        ci_ref[:, col] = c_i
        return carry

    lax.fori_loop(0, SSM_NSTATE // LANES, strip, 0)

    ys = []
    for k in range(nblk):
        st = jnp.concatenate([xr_ref[:, k * sblk:(k + 1) * sblk].astype(BF16),
                              xi_ref[:, k * sblk:(k + 1) * sblk].astype(BF16)], axis=1)
        ys.append(jnp.dot(st, cw_ref[k], preferred_element_type=F32))
    y = jax.nn.gelu(jnp.concatenate(ys, axis=1) + d_ref[...] * u)
    o_ref[0] = y * jax.nn.sigmoid(jnp.dot(y.astype(BF16), glu_ref[...], preferred_element_type=F32))


def ssm_mixer(u, a_re, a_im, log_dt, b_re, b_im, c_re, c_im, d_skip, glu_w):
    bs, s, _ = u.shape
    assert s % SSM_TC == 0
    nblk = SSM_WIDTH // SSM_HALF
    gpb = SSM_GROUPS // nblk
    dt = jnp.exp(log_dt)[:, None]
    mag = jnp.exp(a_re * dt)
    abar_r, abar_i = mag * jnp.cos(a_im * dt), mag * jnp.sin(a_im * dt)
    nr, ni = abar_r - 1.0, abar_i
    den = a_re * a_re + a_im * a_im
    sr, si = (nr * a_re + ni * a_im) / den, (ni * a_re - nr * a_im) / den
    bbar_r = sr[..., None] * b_re - si[..., None] * b_im
    bbar_i = sr[..., None] * b_im + si[..., None] * b_re

    def powers(k):
        kk = k.astype(F32)[:, None, None]
        m = jnp.exp(kk * (a_re * dt))
        return ((m * jnp.cos(kk * (a_im * dt))).reshape(-1, SSM_NSTATE),
                (m * jnp.sin(kk * (a_im * dt))).reshape(-1, SSM_NSTATE))

    ar, ai = powers(2 ** jnp.arange(8))
    pr, pi = powers(jnp.arange(1, SSM_SUB + 1))
    eye = jnp.eye(gpb, dtype=F32)

    def in_block(w):
        return jnp.einsum('gnp,gh->gphn', w, eye).reshape(gpb * SSM_GROUP, gpb * SSM_STATE)

    def out_block(w):
        return jnp.einsum('gpn,gh->gnhp', w, eye).reshape(gpb * SSM_STATE, gpb * SSM_GROUP)

    bw = jnp.stack([jnp.concatenate([in_block(bbar_r[k * gpb:(k + 1) * gpb]),
                                     in_block(bbar_i[k * gpb:(k + 1) * gpb])], axis=1)
                    for k in range(nblk)]).astype(BF16)
    cw = jnp.stack([jnp.concatenate([out_block(c_re[k * gpb:(k + 1) * gpb]),
                                     -out_block(c_im[k * gpb:(k + 1) * gpb])], axis=0)
                    for k in range(nblk)]).astype(BF16)
    fixed2 = lambda b, i: (0, 0)
    fixed3 = lambda b, i: (0, 0, 0)
    return pl.pallas_call(
        _ssm_body,
        grid=(bs, s // SSM_TC),
        in_specs=[pl.BlockSpec((1, SSM_TC, SSM_WIDTH), lambda b, i: (b, i, 0)),
                  pl.BlockSpec(bw.shape, fixed3), pl.BlockSpec(cw.shape, fixed3),
                  pl.BlockSpec((1, SSM_WIDTH), fixed2), pl.BlockSpec((SSM_WIDTH, SSM_WIDTH), fixed2),
                  pl.BlockSpec((8, SSM_NSTATE), fixed2), pl.BlockSpec((8, SSM_NSTATE), fixed2),
                  pl.BlockSpec((SSM_SUB, SSM_NSTATE), fixed2), pl.BlockSpec((SSM_SUB, SSM_NSTATE), fixed2)],
        out_specs=pl.BlockSpec((1, SSM_TC, SSM_WIDTH), lambda b, i: (b, i, 0)),
        out_shape=jax.ShapeDtypeStruct((bs, s, SSM_WIDTH), F32),
        scratch_shapes=[pltpu.VMEM((SSM_TC, SSM_NSTATE), F32), pltpu.VMEM((SSM_TC, SSM_NSTATE), F32),
                        pltpu.VMEM((1, SSM_NSTATE), F32), pltpu.VMEM((1, SSM_NSTATE), F32)],
        compiler_params=_params("parallel", "arbitrary"),
        name="ssm_scan",
    )(u, bw, cw, d_skip.reshape(1, SSM_WIDTH), glu_w.astype(BF16), ar, ai, pr, pi)


def _ssm_mixer_jax(u, a_re, a_im, log_dt, b_re, b_im, c_re, c_im, d_skip, glu_w):
    bs, s, _ = u.shape
    uf = u.reshape(bs, s, SSM_GROUPS, SSM_GROUP)
    dt = jnp.exp(log_dt)[:, None]
    mag = jnp.exp(a_re * dt)
    abar_r, abar_i = mag * jnp.cos(a_im * dt), mag * jnp.sin(a_im * dt)
    nr, ni = abar_r - 1.0, abar_i
    den = a_re * a_re + a_im * a_im
    sr, si = (nr * a_re + ni * a_im) / den, (ni * a_re - nr * a_im) / den
    bbar_r = sr[..., None] * b_re - si[..., None] * b_im
    bbar_i = sr[..., None] * b_im + si[..., None] * b_re
    bu_r = jnp.einsum('bsgp,gnp->bsgn', uf, bbar_r)
    bu_i = jnp.einsum('bsgp,gnp->bsgn', uf, bbar_i)
    ar = jnp.broadcast_to(abar_r, bu_r.shape)
    ai = jnp.broadcast_to(abar_i, bu_r.shape)

    def combine(e1, e2):
        a1r, a1i, b1r, b1i = e1
        a2r, a2i, b2r, b2i = e2
        return (a2r * a1r - a2i * a1i, a2r * a1i + a2i * a1r,
                a2r * b1r - a2i * b1i + b2r, a2r * b1i + a2i * b1r + b2i)

    _, _, st_r, st_i = lax.associative_scan(combine, (ar, ai, bu_r, bu_i), axis=1)
    y = (jnp.einsum('gpn,bsgn->bsgp', c_re, st_r) - jnp.einsum('gpn,bsgn->bsgp', c_im, st_i)
         + d_skip.reshape(SSM_GROUPS, SSM_GROUP) * uf)
    y = jax.nn.gelu(y.reshape(bs, s, SSM_WIDTH))
    return y * jax.nn.sigmoid(y @ glu_w)


GDN_TC = 256
GDN_HALO = 8
MASKED = -1e30


def _dot_nt(a, b):
    return lax.dot_general(a, b, (((1,), (1,)), ((), ())), preferred_element_type=F32)


def _dot_tn(a, b):
    return lax.dot_general(a, b, (((0,), (0,)), ((), ())), preferred_element_type=F32)


def _split_bf16(x):
    hi = x.astype(BF16)
    return hi, (x - hi.astype(F32)).astype(BF16)


def _gdn_body(nega_ref, dtb_ref, q_ref, k_ref, v_ref, z_ref, sm_ref, wq_ref, wk_ref, wv_ref, ng_ref,
              o_ref, state_ref, hq_ref, hk_ref, hv_ref, vnew_ref):
    TC, CH, DK = GDN_TC, GDN_CHUNK, GDN_DK
    h = pl.program_id(1)

    @pl.when(pl.program_id(2) == 0)
    def _():
        state_ref[...] = jnp.zeros_like(state_ref)
        hq_ref[...] = jnp.zeros_like(hq_ref)
        hk_ref[...] = jnp.zeros_like(hk_ref)
        hv_ref[...] = jnp.zeros_like(hv_ref)

    def conv_silu(x_ref, halo_ref, w_ref):
        x = x_ref[...]
        xx = jnp.concatenate([halo_ref[...], x], axis=0)
        w = w_ref[...]
        y = w[GDN_CONV - 1:GDN_CONV] * x
        for d in range(1, GDN_CONV):
            y = y + w[GDN_CONV - 1 - d:GDN_CONV - d] * pltpu.roll(xx, d, 0)[GDN_HALO:GDN_HALO + TC]
        halo_ref[...] = x[TC - GDN_HALO:TC]
        return y * jax.nn.sigmoid(y)

    q = conv_silu(q_ref, hq_ref, wq_ref)
    k = conv_silu(k_ref, hk_ref, wk_ref)
    v = conv_silu(v_ref, hv_ref, wv_ref)
    q = q * lax.rsqrt(jnp.sum(q * q, axis=-1, keepdims=True) + 1e-6) * (DK ** -0.5)
    k = k * lax.rsqrt(jnp.sum(k * k, axis=-1, keepdims=True) + 1e-6)

    small = sm_ref[...]
    lane_s = lax.broadcasted_iota(jnp.int32, small.shape, 1)
    a_col = jnp.sum(jnp.where(lane_s == h, small, 0.0), axis=-1, keepdims=True)
    b_col = jnp.sum(jnp.where(lane_s == GDN_HEADS + h, small, 0.0), axis=-1, keepdims=True)
    beta = jax.nn.sigmoid(b_col)
    xa = a_col + dtb_ref[h]
    softplus = jnp.maximum(xa, 0.0) + jnp.log(1.0 + jnp.exp(-jnp.abs(xa)))
    la = jnp.broadcast_to(nega_ref[h] * softplus, (TC, LANES))

    ri = lax.broadcasted_iota(jnp.int32, (TC, TC), 0)
    ci = lax.broadcasted_iota(jnp.int32, (TC, TC), 1)
    same = (ri // CH) == (ci // CH)
    causal = jnp.where(same, jnp.where(ci <= ri, 1.0, 0.0), 0.0)
    strict = jnp.where(ci < ri, causal, 0.0)
    tri = causal.astype(BF16)
    la_hi, la_lo = _split_bf16(la)
    g = (jnp.dot(tri, la_hi, preferred_element_type=F32)
         + jnp.dot(tri, la_lo, preferred_element_type=F32))
    g_hi = g.astype(BF16).astype(F32)
    g_lo = g - g_hi
    lane = lax.broadcasted_iota(jnp.int32, (TC, LANES), 1)
    lhs = jnp.where(lane == 0, g_hi, jnp.where(lane == 1, g_lo, jnp.where(lane < 4, 1.0, 0.0))).astype(BF16)
    rhs = jnp.where(lane < 2, 1.0, jnp.where(lane == 2, -g_hi, jnp.where(lane == 3, -g_lo, 0.0))).astype(BF16)
    decay = jnp.exp(jnp.where(causal > 0.5, _dot_nt(lhs, rhs), MASKED))

    kb = k * beta
    k_b = k.astype(BF16)
    lmat = strict * _dot_nt(kb.astype(BF16), k_b) * decay
    eye = jnp.where(ri == ci, 1.0, 0.0)
    tinv = eye - lmat
    pw = lmat
    for _ in range(CH.bit_length() - 2):
        pb = pw.astype(BF16)
        pw = jnp.dot(pb, pb, preferred_element_type=F32)
        tinv = tinv + jnp.dot(tinv.astype(BF16), pw.astype(BF16), preferred_element_type=F32)
    eg = jnp.exp(g)
    rhs_all = jnp.concatenate([v * beta, kb * eg], axis=1).astype(BF16)
    sol = jnp.dot(tinv.astype(BF16), rhs_all, preferred_element_type=F32)
    u_val, w_val = sol[:, :GDN_DV], sol[:, GDN_DV:]
    attn = (causal * _dot_nt(q.astype(BF16), k_b) * decay).astype(BF16)
    q_st = (q * eg).astype(BF16)

    vnew_ref[...] = jnp.zeros_like(vnew_ref)
    for c in range(TC // CH):
        rows = slice(c * CH, (c + 1) * CH)
        g_last = g[(c + 1) * CH - 1:(c + 1) * CH, :]
        st = state_ref[...]
        st_b = st.astype(BF16)
        v_new = u_val[rows] - jnp.dot(w_val[rows].astype(BF16), st_b, preferred_element_type=F32)
        vnew_ref[rows, :] = v_new.astype(BF16)
        o_c = (jnp.dot(q_st[rows], st_b, preferred_element_type=F32)
               + jnp.dot(attn[rows], vnew_ref[...], preferred_element_type=F32))
        k_st = (k[rows] * jnp.exp(g_last - g[rows])).astype(BF16)
        state_ref[...] = st * jnp.exp(g_last[:, :1]) + _dot_tn(k_st, v_new.astype(BF16))
        o_c = o_c * lax.rsqrt(jnp.mean(o_c * o_c, axis=-1, keepdims=True) + RMS_EPS) * ng_ref[...]
        zc = z_ref[rows, :]
        o_ref[rows, :] = o_c * (zc * jax.nn.sigmoid(zc))


def gdn_mixer(proj, bs, conv_w, a_log, dt_bias, norm_g):
    t = proj.shape[0]
    s = t // bs
    nt = s // GDN_TC
    assert s % GDN_TC == 0 and GDN_DK == LANES and GDN_DV == LANES and OFF_QKV == 0
    H = GDN_HEADS
    cw = jnp.zeros((8, GDN_QKV), F32).at[:GDN_CONV].set(conv_w)
    tok = lambda off: pl.BlockSpec((GDN_TC, LANES), lambda b, h, i, *_: (b * nt + i, off + h))
    wspec = lambda off: pl.BlockSpec((8, LANES), lambda b, h, i, *_: (0, off + h))
    grid_spec = pltpu.PrefetchScalarGridSpec(
        num_scalar_prefetch=0,
        grid=(bs, H, nt),
        in_specs=[pl.BlockSpec(memory_space=pltpu.SMEM), pl.BlockSpec(memory_space=pltpu.SMEM),
                  tok(0), tok(H), tok(2 * H), tok(OFF_Z // LANES),
                  pl.BlockSpec((GDN_TC, 2 * LANES), lambda b, h, i, *_: (b * nt + i, OFF_SMALL // (2 * LANES))),
                  wspec(0), wspec(H), wspec(2 * H),
                  pl.BlockSpec((1, GDN_DV), lambda b, h, i, *_: (0, 0))],
        out_specs=pl.BlockSpec((GDN_TC, GDN_DV), lambda b, h, i, *_: (b * nt + i, h)),
        scratch_shapes=[pltpu.VMEM((GDN_DK, GDN_DV), F32)] + [pltpu.VMEM((GDN_HALO, LANES), F32)] * 3
        + [pltpu.VMEM((GDN_TC, GDN_DV), BF16)],
    )
    return pl.pallas_call(
        _gdn_body,
        grid_spec=grid_spec,
        out_shape=jax.ShapeDtypeStruct((t, H * GDN_DV), F32),
        compiler_params=_params("parallel", "parallel", "arbitrary"),
        name="gdn_delta",
    )(-jnp.exp(a_log), dt_bias.astype(F32), proj, proj, proj, proj, proj, cw, cw, cw,
      norm_g.reshape(1, GDN_DV))


def _gdn_mixer_jax(qkv, a, b, z, conv_w, a_log, dt_bias, norm_g):
    bs, s, c = qkv.shape
    H, DK, DV, CH = GDN_HEADS, GDN_DK, GDN_DV, GDN_CHUNK
    qkv = lax.conv_general_dilated(qkv, conv_w[:, None, :], window_strides=(1,),
                                   padding=[(GDN_CONV - 1, 0)],
                                   dimension_numbers=('NWC', 'WIO', 'NWC'), feature_group_count=c)
    qkv = jax.nn.silu(qkv)
    q, k, v = jnp.split(qkv, [H * DK, 2 * H * DK], axis=-1)
    q = _l2_norm(q.reshape(bs, s, H, DK)) * (DK ** -0.5)
    k = _l2_norm(k.reshape(bs, s, H, DK))
    v = v.reshape(bs, s, H, DV)
    beta = jax.nn.sigmoid(b)
    log_alpha = -jnp.exp(a_log) * jax.nn.softplus(a + dt_bias)
    nc = s // CH

    def chunks(t):
        return t.reshape(bs, nc, CH, H, -1).transpose(0, 3, 1, 2, 4)

    q, k, v = chunks(q), chunks(k), chunks(v)
    beta = chunks(beta[..., None])
    g = jnp.cumsum(chunks(log_alpha[..., None]), axis=3)
    gv = g[..., 0]
    causal = jnp.tril(jnp.ones((CH, CH), bool))
    strict = jnp.tril(jnp.ones((CH, CH), bool), -1)
    decay = jnp.exp(jnp.where(causal, gv[..., :, None] - gv[..., None, :], -jnp.inf))
    kb = k * beta
    a_mat = (jnp.where(strict, jnp.einsum('bhncd,bhnkd->bhnck', kb, k) * decay, 0.0)
             + jnp.eye(CH, dtype=F32))
    rhs = jnp.concatenate([v * beta, kb * jnp.exp(g)], axis=-1)
    sol = lax.linalg.triangular_solve(a_mat, rhs, left_side=True, lower=True, unit_diagonal=True)
    u_val, w = sol[..., :DV], sol[..., DV:]
    attn = jnp.einsum('bhncd,bhnkd->bhnck', q, k) * decay
    g_last = g[..., -1:, :]
    k_st = k * jnp.exp(g_last - g)
    q_st = q * jnp.exp(g)

    def step(state, xs):
        q_c, k_c, u_c, w_c, a_c, gl_c = xs
        v_new = u_c - jnp.einsum('bhcd,bhde->bhce', w_c, state)
        o = jnp.einsum('bhcd,bhde->bhce', q_c, state) + jnp.einsum('bhck,bhke->bhce', a_c, v_new)
        state = state * jnp.exp(gl_c) + jnp.einsum('bhcd,bhce->bhde', k_c, v_new)
        return state, o

    xs = tuple(jnp.moveaxis(t, 2, 0) for t in (q_st, k_st, u_val, w, attn, g_last))
    _, o = lax.scan(step, jnp.zeros((bs, H, DK, DV), F32), xs)
    o = o.transpose(1, 0, 3, 2, 4).reshape(bs, s, H, DV)
    o = o * lax.rsqrt(jnp.mean(o * o, axis=-1, keepdims=True) + RMS_EPS) * norm_g
    o = o * jax.nn.silu(z.reshape(bs, s, H, DV))
    return o.reshape(bs, s, H * DV)


NSA_KT = 256
NSA_NEG = MASKED
NSA_WTILES = NSA_WINDOW // LANES + 1


def _bucket_table():
    n = np.arange(LANES)
    max_exact = REL_BUCKETS // 2
    nf = np.maximum(n, 1).astype(np.float32)
    large = max_exact + (np.log(nf / np.float32(max_exact)) / np.float32(math.log(REL_MAX_DIST / max_exact))
                         * np.float32(REL_BUCKETS - max_exact)).astype(np.int32)
    tbl = np.where(n < max_exact, n, np.minimum(large, REL_BUCKETS - 1))
    assert tbl[-1] == REL_BUCKETS - 1 and REL_MAX_DIST <= LANES
    return tbl


def _compress_body(x_ref, pe_ref, w1_ref, w2_ref, o_ref):
    x = x_ref[0, 0, 0]
    w1 = w1_ref[0]
    half = NSA_CMP_STRIDE * NSA_DH
    nc = x.shape[0]
    a = jnp.dot(x, w1[:half], preferred_element_type=F32)
    b = jnp.dot(x, w1[half:], preferred_element_type=F32)
    pe_h = jnp.dot(pe_ref[0], w1, preferred_element_type=F32)
    hid = a + pltpu.roll(b, nc - 1, 0) + pe_h[0:1, :]
    o_ref[0, 0, 0] = jnp.dot(jax.nn.gelu(hid).astype(BF16), w2_ref[0], preferred_element_type=F32)


def nsa_compress(x, pe, w1, w2):
    _, bs, g, nc, width = x.shape
    return pl.pallas_call(
        _compress_body,
        grid=(2, bs, g),
        in_specs=[pl.BlockSpec((1, 1, 1, nc, width), lambda i, b, j: (i, b, j, 0, 0)),
                  pl.BlockSpec((1,) + pe.shape[1:], lambda i, b, j: (i, 0, 0)),
                  pl.BlockSpec((1,) + w1.shape[1:], lambda i, b, j: (i, 0, 0)),
                  pl.BlockSpec((1,) + w2.shape[1:], lambda i, b, j: (i, 0, 0))],
        out_specs=pl.BlockSpec((1, 1, 1, nc, NSA_DH), lambda i, b, j: (i, b, j, 0, 0)),
        out_shape=jax.ShapeDtypeStruct((2, bs, g, nc, NSA_DH), F32),
        compiler_params=_params("parallel", "parallel", "parallel"),
        name="nsa_compress",
    )(x, pe, w1, w2)


def _nsa_body(nfast, wstd, near_w, near_c, q_ref, gl_ref, qp_ref, kp_ref, cp_ref,
              ks_ref, vs_ref, kw_ref, vw_ref, kc_ref, vc_ref, agg_ref, td_ref, o_ref,
              m_ref, l_ref, acc_ref, sel_ref, *, n_sel, sel_k):
    QB, HG = NSA_QBLOCK, NSA_HPG
    g = pl.program_id(1)
    qi = pl.program_id(2)
    s0 = qi * QB
    nc = kc_ref.shape[2]
    n_ct = nc // LANES
    q = q_ref[0, 0, 0]
    qp = qp_ref[...]

    def lanes4(x):
        return jnp.concatenate([x] * HG, axis=1)

    def delta_t(kp):
        idx = jnp.clip(_dot_nt(kp, qp), 0.0, float(LANES - 1)).astype(jnp.int32)
        outs = []
        for hg in range(HG):
            tb = jnp.broadcast_to(td_ref[pl.ds(g * HG + hg, 1), :], idx.shape)
            outs.append(jnp.take_along_axis(tb, idx, axis=1))
        return jnp.concatenate(outs, axis=1)

    def maybe_delta(flag, kp):
        return lax.cond(flag != 0, lambda: delta_t(kp),
                        lambda: jnp.zeros((kp.shape[0], HG * QB), F32))

    gate = jax.nn.sigmoid(gl_ref[0, 0, 0])

    sc = _dot_nt(kc_ref[0, 0], q)
    sc = sc + jnp.concatenate(
        [maybe_delta(near_c[qi * n_ct + ct], cp_ref[ct * LANES:(ct + 1) * LANES, :])
         for ct in range(n_ct)], axis=0)
    c_id = lax.broadcasted_iota(jnp.int32, (nc, QB), 0)
    c_q = lax.broadcasted_iota(jnp.int32, (nc, QB), 1)
    ok_c = lanes4(jnp.where(c_id * NSA_CMP_STRIDE + (NSA_CMP_LEN - 1) <= s0 + c_q, 1.0, 0.0))
    sc = jnp.where(ok_c > 0.5, sc, NSA_NEG)
    e = jnp.exp(sc - jnp.max(sc, axis=0, keepdims=True)) * ok_c
    p = e * (1.0 / jnp.maximum(jnp.sum(e, axis=0, keepdims=True), 1e-30))
    out = gate[0:1, :] * jnp.dot(vc_ref[0, 0], p.astype(BF16), preferred_element_type=F32)
    ps = p[:, 0:QB]
    for hg in range(1, HG):
        ps = ps + p[:, hg * QB:(hg + 1) * QB]
    ps_hi, ps_lo = _split_bf16(ps)
    imp = (jnp.dot(agg_ref[...], ps_hi, preferred_element_type=F32)
           + jnp.dot(agg_ref[...], ps_lo, preferred_element_type=F32))
    blk = lax.broadcasted_iota(jnp.int32, (LANES, QB), 0)
    t_tok = s0 + lax.broadcasted_iota(jnp.int32, (LANES, QB), 1)
    tb_blk = t_tok // NSA_SEL_LEN
    forced = jnp.where(blk == 0, 1.0, 0.0) + jnp.where(blk == tb_blk, 1.0, 0.0) \
        + jnp.where(blk == tb_blk - 1, 1.0, 0.0)
    score = jnp.where(forced > 0.5, 1e9, jnp.where(blk * NSA_SEL_LEN <= t_tok, imp, -1e9))
    score = jnp.where(blk < n_sel, score, -jnp.inf)
    blkf = blk.astype(F32)
    sel = jnp.zeros((LANES, QB), F32)
    for _ in range(sel_k):
        mx = jnp.max(score, axis=0, keepdims=True)
        first = jnp.min(jnp.where(score == mx, blkf, float(LANES)), axis=0, keepdims=True)
        hit = blkf == first
        sel = jnp.where(hit, 1.0, sel)
        score = jnp.where(hit, -jnp.inf, score)
    sel_ref[...] = jnp.where(sel > 0.5, 0.0, NSA_NEG)

    def reset():
        m_ref[...] = jnp.full_like(m_ref, NSA_NEG)
        l_ref[...] = jnp.zeros_like(l_ref)
        acc_ref[...] = jnp.zeros_like(acc_ref)

    def tile_step(s, v_t, k):
        m_old = m_ref[k]
        m_new = jnp.maximum(m_old, jnp.max(s, axis=0, keepdims=True))
        alpha = jnp.exp(m_old - m_new)
        p = jnp.exp(s - m_new)
        l_ref[k] = l_ref[k] * alpha + jnp.sum(p, axis=0, keepdims=True)
        acc_ref[k] = acc_ref[k] * alpha + jnp.dot(v_t, p.astype(BF16), preferred_element_type=F32)
        m_ref[k] = m_new

    def finish():
        m = jnp.maximum(m_ref[0], m_ref[1])
        w0, w1 = jnp.exp(m_ref[0] - m), jnp.exp(m_ref[1] - m)
        return (acc_ref[0] * w0 + acc_ref[1] * w1) * (1.0 / (l_ref[0] * w0 + l_ref[1] * w1))

    per_tile = NSA_KT // NSA_SEL_LEN
    key_r = lax.broadcasted_iota(jnp.int32, (NSA_KT, QB), 0)
    key_q = lax.broadcasted_iota(jnp.int32, (NSA_KT, QB), 1)

    def sel_scores(kt, with_delta, diag):
        k0 = pl.multiple_of(kt * NSA_KT, NSA_KT)
        s = _dot_nt(ks_ref[0, 0, pl.ds(k0, NSA_KT), :], q)
        if with_delta:
            s = s + delta_t(kp_ref[pl.ds(k0, NSA_KT), :])
        mask = jnp.concatenate(
            [jnp.broadcast_to(sel_ref[pl.ds(kt * per_tile + i, 1), :], (NSA_SEL_LEN, QB))
             for i in range(per_tile)], axis=0)
        if diag:
            mask = jnp.where(k0 + key_r <= s0 + key_q, mask, NSA_NEG)
        return s + lanes4(mask), vs_ref[0, 0, :, pl.ds(k0, NSA_KT)]

    reset()
    n_past = (s0 + QB - 1) // NSA_KT
    n_pairs = nfast[qi] // 2

    def fast_body(i, carry):
        a = sel_scores(2 * i, False, False)
        b = sel_scores(2 * i + 1, False, False)
        tile_step(*a, 0)
        tile_step(*b, 1)
        return carry

    def slow_body(kt, carry):
        tile_step(*sel_scores(kt, True, False), 1)
        return carry

    lax.fori_loop(0, n_pairs, fast_body, 0)
    lax.fori_loop(2 * n_pairs, n_past, slow_body, 0)
    tile_step(*sel_scores(n_past, True, True), 0)
    out = out + gate[1:2, :] * finish()

    reset()
    w_r = lax.broadcasted_iota(jnp.int32, (LANES, QB), 0)
    w_q = lax.broadcasted_iota(jnp.int32, (LANES, QB), 1)
    in_window = lanes4(jnp.where(w_r > w_q, 0.0, NSA_NEG))
    causal_w = lanes4(jnp.where(w_r <= w_q, 0.0, NSA_NEG))

    def win_scores(st, n):
        st = pl.multiple_of(st, LANES)
        return _dot_nt(kw_ref[0, 0, pl.ds(st, n), :], q), vw_ref[0, 0, :, pl.ds(st, n)]

    @pl.when(wstd[qi] != 0)
    def _():
        half = NSA_WINDOW // 2
        zeros = jnp.zeros((LANES, HG * QB), F32)
        sa, va = win_scores(s0 - NSA_WINDOW, half)
        sb, vb = win_scores(s0 - half, half)
        sd, vd = win_scores(s0, LANES)
        tile_step(sa + jnp.concatenate([in_window, zeros], axis=0), va, 0)
        near = delta_t(kp_ref[pl.ds(pl.multiple_of(s0 - LANES, LANES), LANES), :])
        tile_step(sb + jnp.concatenate([zeros, near], axis=0), vb, 1)
        tile_step(sd + delta_t(kp_ref[pl.ds(pl.multiple_of(s0, LANES), LANES), :]) + causal_w, vd, 0)

    for j in range(NSA_WTILES):
        start = s0 - NSA_WINDOW + j * LANES

        @pl.when((wstd[qi] == 0) & (start >= 0))
        def _():
            s, v_t = win_scores(start, LANES)
            s = s + maybe_delta(near_w[qi * NSA_WTILES + j],
                                kp_ref[pl.ds(pl.multiple_of(start, LANES), LANES), :])
            if j == 0:
                s = s + in_window
            elif j == NSA_WTILES - 1:
                s = s + causal_w
            tile_step(s, v_t, j % 2)

    o_ref[0, 0, 0] = out + gate[2:3, :] * finish()


def _pos_digits(p, key_side):
    d0, d1, d2 = (p & 127).astype(F32), ((p >> 7) & 127).astype(F32), (p >> 14).astype(F32)
    one = jnp.ones_like(d0)
    if key_side:
        cols = [one, one, one, -d2, -d1, -d0]
    else:
        cols = [16384.0 * d2, 128.0 * d1, d0, 16384.0 * one, 128.0 * one, one]
    out = jnp.stack(cols, axis=-1)
    return jnp.pad(out, ((0, 0), (0, LANES - out.shape[-1]))).astype(BF16)


def nsa_mixer(q, k_c, v_c, k_s, v_s, k_w, v_w, gate_logits, positions,
              ck_pe, ck_w1, ck_w2, cv_pe, cv_w1, cv_w2, rel_bias):
    q, k_c, v_c, k_s, v_s, k_w, v_w = lax.optimization_barrier((q, k_c, v_c, k_s, v_s, k_w, v_w))
    bs, s, _ = q.shape
    G, HG, DH, QB = NSA_KV_GROUPS, NSA_HPG, NSA_DH, NSA_QBLOCK
    nqt = s // QB
    nc = s // NSA_CMP_STRIDE
    n_cmp = (s - NSA_CMP_LEN) // NSA_CMP_STRIDE + 1
    n_sel = s // NSA_SEL_LEN
    sel_k = min(NSA_SEL_TOPK, n_sel)
    n_kt = s // NSA_KT
    n_ct = nc // LANES
    assert s % (LANES * NSA_CMP_STRIDE) == 0 and n_sel <= LANES and NSA_KT == 2 * QB

    def by_group(t):
        return t.reshape(bs, s, G, DH).transpose(0, 2, 1, 3)

    def by_group_t(t):
        return t.reshape(bs, s, G, DH).transpose(0, 2, 3, 1).astype(BF16)

    ones2 = jnp.zeros((LANES - DH,), F32).at[:2].set(1.0)

    def keys_aug(t):
        return jnp.concatenate([t, jnp.broadcast_to(ones2, t.shape[:-1] + (LANES - DH,))],
                               axis=-1).astype(BF16)

    chunks = jnp.stack([by_group(k_c), by_group(v_c)]).reshape(2, bs, G, nc, NSA_CMP_STRIDE * DH)
    pe = jnp.stack([ck_pe, cv_pe]).reshape(2, 1, NSA_CMP_LEN * DH)
    cmp = nsa_compress(chunks.astype(BF16), jnp.broadcast_to(pe, (2, 8, NSA_CMP_LEN * DH)).astype(BF16),
                       jnp.stack([ck_w1, cv_w1]).astype(BF16), jnp.stack([ck_w2, cv_w2]).astype(BF16))
    kc, vc_t = keys_aug(cmp[0]), cmp[1].transpose(0, 1, 3, 2).astype(BF16)

    far = rel_bias[REL_BUCKETS - 1].astype(F32)
    far_hi = far.astype(BF16).astype(F32)
    extra = jnp.zeros((NSA_HEADS, LANES - DH), F32).at[:, 0].set(far_hi).at[:, 1].set(far - far_hi)
    qh = (q * (DH ** -0.5)).reshape(bs, nqt, QB, G, HG, DH).transpose(0, 3, 1, 4, 2, 5)
    ex = jnp.broadcast_to(extra.reshape(1, G, 1, HG, 1, LANES - DH), (bs, G, nqt, HG, QB, LANES - DH))
    qs = jnp.concatenate([qh, ex], axis=-1).astype(BF16).reshape(bs, G, nqt, HG * QB, LANES)
    gl = gate_logits.reshape(bs, nqt, QB, G, HG, 3).transpose(0, 3, 1, 5, 4, 2)
    gl = jnp.pad(gl.reshape(bs, G, nqt, 3, HG * QB), ((0, 0),) * 3 + ((0, 5), (0, 0)))

    td = (rel_bias[_bucket_table()] - rel_bias[REL_BUCKETS - 1][None, :]).T.astype(F32)
    cmp_end = jnp.minimum(jnp.arange(nc) * NSA_CMP_STRIDE + NSA_CMP_LEN - 1, s - 1)
    cpos = positions[cmp_end]
    qmin = positions.reshape(nqt, QB).min(axis=1)
    kmax = positions.reshape(nqt, QB).max(axis=1)
    near_s = (qmin[:, None] - positions.reshape(n_kt, NSA_KT).max(axis=1)[None, :]) < REL_MAX_DIST
    n_past = (jnp.arange(nqt) * QB + QB - 1) // NSA_KT
    n_fast = jnp.minimum(jnp.argmax(jnp.concatenate([near_s, jnp.ones((nqt, 1), bool)], axis=1), axis=1),
                         n_past)
    wt = jnp.clip(jnp.arange(nqt)[:, None] - (NSA_WTILES - 1) + jnp.arange(NSA_WTILES)[None, :], 0, nqt - 1)
    near_w = (qmin[:, None] - kmax[wt]) < REL_MAX_DIST
    near_c = (qmin[:, None] - cpos.reshape(n_ct, LANES).max(axis=1)[None, :]) < REL_MAX_DIST
    w_std = ((jnp.arange(nqt) >= NSA_WTILES - 1)
             & jnp.all(near_w == (jnp.arange(NSA_WTILES) >= NSA_WTILES - 2)[None, :], axis=1))

    cmp_start = np.arange(nc) * NSA_CMP_STRIDE
    sel_start = np.arange(LANES) * NSA_SEL_LEN
    agg_t = ((cmp_start[None, :] <= sel_start[:, None] + NSA_SEL_LEN - 1)
             & (cmp_start[None, :] + NSA_CMP_LEN - 1 >= sel_start[:, None])
             & (np.arange(nc)[None, :] < n_cmp) & (np.arange(LANES)[:, None] < n_sel))

    cols = HG * QB
    full = lambda shape: pl.BlockSpec(shape, lambda b, g, i, *_: (0,) * len(shape))
    per_bg = lambda n, w: pl.BlockSpec((1, 1, n, w), lambda b, g, i, *_: (b, g, 0, 0))
    grid_spec = pltpu.PrefetchScalarGridSpec(
        num_scalar_prefetch=4,
        grid=(bs, G, nqt),
        in_specs=[pl.BlockSpec((1, 1, 1, cols, LANES), lambda b, g, i, *_: (b, g, i, 0, 0)),
                  pl.BlockSpec((1, 1, 1, 8, cols), lambda b, g, i, *_: (b, g, i, 0, 0)),
                  pl.BlockSpec((QB, LANES), lambda b, g, i, *_: (i, 0)),
                  full((s, LANES)), full((nc, LANES)),
                  per_bg(s, LANES), per_bg(DH, s), per_bg(s, LANES), per_bg(DH, s),
                  per_bg(nc, LANES), per_bg(DH, nc),
                  full((LANES, nc)), full((NSA_HEADS, LANES))],
        out_specs=pl.BlockSpec((1, 1, 1, DH, cols), lambda b, g, i, *_: (b, g, i, 0, 0)),
        scratch_shapes=[pltpu.VMEM((2, 1, cols), F32), pltpu.VMEM((2, 1, cols), F32),
                        pltpu.VMEM((2, DH, cols), F32), pltpu.VMEM((LANES, QB), F32)],
    )
    out = pl.pallas_call(
        functools.partial(_nsa_body, n_sel=n_sel, sel_k=sel_k),
        grid_spec=grid_spec,
        out_shape=jax.ShapeDtypeStruct((bs, G, nqt, DH, cols), F32),
        compiler_params=_params("parallel", "parallel", "arbitrary"),
        name="nsa_attention",
    )(n_fast.astype(jnp.int32), w_std.astype(jnp.int32), near_w.reshape(-1).astype(jnp.int32),
      near_c.reshape(-1).astype(jnp.int32),
      qs, gl, _pos_digits(positions, False), _pos_digits(positions, True), _pos_digits(cpos, True),
      keys_aug(by_group(k_s)), by_group_t(v_s), keys_aug(by_group(k_w)), by_group_t(v_w),
      kc, vc_t, jnp.asarray(agg_t, BF16), td)
    out = out.reshape(bs, G, nqt, DH, HG, QB).transpose(0, 2, 5, 1, 4, 3)
    return out.reshape(bs, s, G * HG * DH)


def _nsa_body_rowmajor(near_s, near_w, near_c, q_ref, gl_ref, posq_ref, posk_ref, cpos_ref,
              ks_ref, vs_ref, kw_ref, vw_ref, kc_ref, vc_ref, agg_ref, td_ref, o_ref,
              s_ref, m_ref, l_ref, acc_ref, *, n_sel, sel_k):
    QB, HG = NSA_QBLOCK, NSA_HPG
    g = pl.program_id(1)
    qi = pl.program_id(2)
    s0 = qi * QB
    nc = kc_ref.shape[2]
    n_ct = nc // LANES
    n_kt = ks_ref.shape[2] // NSA_KT
    q = q_ref[0, 0, 0]
    tp = posq_ref[...]
    row = lax.broadcasted_iota(jnp.int32, (QB, LANES), 0)
    lane = lax.broadcasted_iota(jnp.int32, (QB, LANES), 1)
    t_tok = s0 + row

    def add_delta(col0, pk_row):
        idx = jnp.clip(tp - pk_row, 0, LANES - 1)
        for hg in range(HG):
            tb = jnp.broadcast_to(td_ref[pl.ds(g * HG + hg, 1), :], (QB, LANES))
            s_ref[hg * QB:(hg + 1) * QB, col0:col0 + LANES] += jnp.take_along_axis(tb, idx, axis=1)

    def heads(x):
        return jnp.concatenate([x] * HG, axis=0)

    gate = jax.nn.sigmoid(gl_ref[0, 0, 0])

    s_ref[:, :nc] = _dot_nt(q, kc_ref[0, 0])
    for ct in range(n_ct):
        @pl.when(near_c[qi * n_ct + ct] != 0)
        def _():
            add_delta(ct * LANES, cpos_ref[ct:ct + 1, :])
    c_id = lax.broadcasted_iota(jnp.int32, (QB, nc), 1)
    c_row = lax.broadcasted_iota(jnp.int32, (QB, nc), 0)
    ok_c = heads(jnp.where(c_id * NSA_CMP_STRIDE + (NSA_CMP_LEN - 1) <= s0 + c_row, 1.0, 0.0))
    sc = jnp.where(ok_c > 0.5, s_ref[:, :nc], NSA_NEG)
    e = jnp.exp(sc - jnp.max(sc, axis=-1, keepdims=True)) * ok_c
    p = e * (1.0 / jnp.maximum(jnp.sum(e, axis=-1, keepdims=True), 1e-30))
    o_c = jnp.dot(p.astype(BF16), vc_ref[0, 0], preferred_element_type=F32)
    o_ref[0, 0, 0] = gate[:, 0:1] * o_c
    ps = p[0:QB]
    for hg in range(1, HG):
        ps = ps + p[hg * QB:(hg + 1) * QB]
    ps_hi = ps.astype(BF16)
    ps_lo = (ps - ps_hi.astype(F32)).astype(BF16)
    imp = (jnp.dot(ps_hi, agg_ref[...], preferred_element_type=F32)
           + jnp.dot(ps_lo, agg_ref[...], preferred_element_type=F32))
    tb_blk = t_tok // NSA_SEL_LEN
    forced = jnp.where(lane == 0, 1.0, 0.0) + jnp.where(lane == tb_blk, 1.0, 0.0) \
        + jnp.where(lane == tb_blk - 1, 1.0, 0.0)
    score = jnp.where(forced > 0.5, 1e9, jnp.where(lane * NSA_SEL_LEN <= t_tok, imp, -1e9))
    score = jnp.where(lane < n_sel, score, -jnp.inf)
    lanef = lane.astype(F32)
    sel = jnp.zeros((QB, LANES), F32)
    for _ in range(sel_k):
        mx = jnp.max(score, axis=-1, keepdims=True)
        first = jnp.min(jnp.where(score == mx, lanef, float(LANES)), axis=-1, keepdims=True)
        hit = lanef == first
        sel = jnp.where(hit, 1.0, sel)
        score = jnp.where(hit, -jnp.inf, score)
    sel_b = sel.astype(BF16)

    def reset():
        m_ref[...] = jnp.full_like(m_ref, NSA_NEG)
        l_ref[...] = jnp.zeros_like(l_ref)
        acc_ref[...] = jnp.zeros_like(acc_ref)

    def tile_step(width, mask_add, v):
        s = s_ref[:, :width]
        if mask_add is not None:
            s = s + heads(mask_add)
        m_old = m_ref[...]
        m_new = jnp.maximum(m_old, jnp.max(s, axis=-1, keepdims=True))
        alpha = jnp.exp(m_old - m_new)
        p = jnp.exp(s - m_new)
        psum = p[:, :LANES]
        for c in range(1, width // LANES):
            psum = psum + p[:, c * LANES:(c + 1) * LANES]
        l_ref[...] = l_ref[...] * alpha + psum
        acc_ref[...] = acc_ref[...] * alpha + jnp.dot(p.astype(BF16), v, preferred_element_type=F32)
        m_ref[...] = m_new

    def finish():
        return acc_ref[...] * (1.0 / jnp.sum(l_ref[...], axis=-1, keepdims=True))

    blk_i = lax.broadcasted_iota(jnp.int32, (LANES, NSA_KT), 0)
    key_i = lax.broadcasted_iota(jnp.int32, (LANES, NSA_KT), 1)
    key_q = lax.broadcasted_iota(jnp.int32, (QB, NSA_KT), 1)
    row_q = lax.broadcasted_iota(jnp.int32, (QB, NSA_KT), 0)
    per_tile = NSA_KT // NSA_SEL_LEN

    def sel_tile(kt, diag):
        k0 = pl.multiple_of(kt * NSA_KT, NSA_KT)
        s_ref[:, :NSA_KT] = _dot_nt(q, ks_ref[0, 0, pl.ds(k0, NSA_KT), :])

        @pl.when(near_s[qi * n_kt + kt] != 0)
        def _():
            for c in range(NSA_KT // LANES):
                add_delta(c * LANES, posk_ref[pl.ds(kt * (NSA_KT // LANES) + c, 1), :])

        expand = jnp.where(blk_i == kt * per_tile + key_i // NSA_SEL_LEN, 1.0, 0.0).astype(BF16)
        picked = jnp.dot(sel_b, expand, preferred_element_type=F32)
        if diag:
            picked = jnp.where(k0 + key_q <= s0 + row_q, picked, 0.0)
        tile_step(NSA_KT, jnp.where(picked > 0.5, 0.0, NSA_NEG), vs_ref[0, 0, pl.ds(k0, NSA_KT), :])

    reset()
    n_past = (s0 + QB - 1) // NSA_KT

    def past_body(kt, carry):
        sel_tile(kt, False)
        return carry

    lax.fori_loop(0, n_past, past_body, 0)
    sel_tile(n_past, True)
    o_ref[0, 0, 0] += gate[:, 1:2] * finish()

    reset()
    for j in range(NSA_WTILES):
        start = s0 - NSA_WINDOW + j * LANES

        @pl.when(start >= 0)
        def _():
            st = pl.multiple_of(start, LANES)
            s_ref[:, :LANES] = _dot_nt(q, kw_ref[0, 0, pl.ds(st, LANES), :])

            @pl.when(near_w[qi * NSA_WTILES + j] != 0)
            def _():
                add_delta(0, posk_ref[pl.ds(qi - (NSA_WTILES - 1) + j, 1), :])

            if j == 0:
                mask_add = jnp.where(lane > row, 0.0, NSA_NEG)
            elif j == NSA_WTILES - 1:
                mask_add = jnp.where(lane <= row, 0.0, NSA_NEG)
            else:
                mask_add = None
            tile_step(LANES, mask_add, vw_ref[0, 0, pl.ds(st, LANES), :])

    o_ref[0, 0, 0] += gate[:, 2:3] * finish()


def _nsa_mixer_rowmajor(q, k_c, v_c, k_s, v_s, k_w, v_w, gate_logits, positions,
                        ck_pe, ck_w1, ck_w2, cv_pe, cv_w1, cv_w2, rel_bias):
    bs, s, _ = q.shape
    G, HG, DH, QB = NSA_KV_GROUPS, NSA_HPG, NSA_DH, NSA_QBLOCK
    nqt = s // QB
    nc = s // NSA_CMP_STRIDE
    n_cmp = (s - NSA_CMP_LEN) // NSA_CMP_STRIDE + 1
    n_sel = s // NSA_SEL_LEN
    sel_k = min(NSA_SEL_TOPK, n_sel)
    n_kt = s // NSA_KT
    n_ct = nc // LANES
    assert s % (LANES * NSA_CMP_STRIDE) == 0 and n_sel <= LANES and NSA_KT == 2 * QB

    def by_group(t):
        return t.reshape(bs, s, G, DH).transpose(0, 2, 1, 3)

    ones2 = jnp.zeros((LANES - DH,), F32).at[:2].set(1.0)

    def keys_aug(t):
        return jnp.concatenate([t, jnp.broadcast_to(ones2, t.shape[:-1] + (LANES - DH,))],
                               axis=-1).astype(BF16)

    chunks = jnp.stack([by_group(k_c), by_group(v_c)]).reshape(2, bs, G, nc, NSA_CMP_STRIDE * DH)
    pe = jnp.stack([ck_pe, cv_pe]).reshape(2, 1, NSA_CMP_LEN * DH)
    cmp = nsa_compress(chunks.astype(BF16), jnp.broadcast_to(pe, (2, 8, NSA_CMP_LEN * DH)).astype(BF16),
                       jnp.stack([ck_w1, cv_w1]).astype(BF16), jnp.stack([ck_w2, cv_w2]).astype(BF16))
    kc, vc = keys_aug(cmp[0]), cmp[1].astype(BF16)

    far = rel_bias[REL_BUCKETS - 1].astype(F32)
    far_hi = far.astype(BF16).astype(F32)
    extra = jnp.zeros((NSA_HEADS, LANES - DH), F32).at[:, 0].set(far_hi).at[:, 1].set(far - far_hi)
    qh = (q * (DH ** -0.5)).reshape(bs, nqt, QB, G, HG, DH).transpose(0, 3, 1, 4, 2, 5)
    ex = jnp.broadcast_to(extra.reshape(1, G, 1, HG, 1, LANES - DH), (bs, G, nqt, HG, QB, LANES - DH))
    qs = jnp.concatenate([qh, ex], axis=-1).astype(BF16).reshape(bs, G, nqt, HG * QB, LANES)
    gl = gate_logits.reshape(bs, nqt, QB, G, HG, 3).transpose(0, 3, 1, 4, 2, 5)
    gl = jnp.pad(gl, ((0, 0),) * 5 + ((0, 5),)).reshape(bs, G, nqt, HG * QB, 8)

    td = (rel_bias[_bucket_table()] - rel_bias[REL_BUCKETS - 1][None, :]).T.astype(F32)
    cmp_end = jnp.minimum(jnp.arange(nc) * NSA_CMP_STRIDE + NSA_CMP_LEN - 1, s - 1)
    cpos = positions[cmp_end]
    qmin = positions.reshape(nqt, QB).min(axis=1)
    kmax = positions.reshape(nqt, QB).max(axis=1)
    near_s = (qmin[:, None] - positions.reshape(n_kt, NSA_KT).max(axis=1)[None, :]) < REL_MAX_DIST
    wt = jnp.clip(jnp.arange(nqt)[:, None] - (NSA_WTILES - 1) + jnp.arange(NSA_WTILES)[None, :], 0, nqt - 1)
    near_w = (qmin[:, None] - kmax[wt]) < REL_MAX_DIST
    near_c = (qmin[:, None] - cpos.reshape(n_ct, LANES).max(axis=1)[None, :]) < REL_MAX_DIST

    cmp_start = np.arange(nc) * NSA_CMP_STRIDE
    sel_start = np.arange(LANES) * NSA_SEL_LEN
    agg = ((cmp_start[:, None] <= sel_start[None, :] + NSA_SEL_LEN - 1)
           & (cmp_start[:, None] + NSA_CMP_LEN - 1 >= sel_start[None, :])
           & (np.arange(nc)[:, None] < n_cmp) & (np.arange(LANES)[None, :] < n_sel))

    rows = HG * QB
    full = lambda shape: pl.BlockSpec(shape, lambda b, g, i, *_: (0,) * len(shape))
    per_bg = lambda n, w: pl.BlockSpec((1, 1, n, w), lambda b, g, i, *_: (b, g, 0, 0))
    grid_spec = pltpu.PrefetchScalarGridSpec(
        num_scalar_prefetch=3,
        grid=(bs, G, nqt),
        in_specs=[pl.BlockSpec((1, 1, 1, rows, LANES), lambda b, g, i, *_: (b, g, i, 0, 0)),
                  pl.BlockSpec((1, 1, 1, rows, 8), lambda b, g, i, *_: (b, g, i, 0, 0)),
                  pl.BlockSpec((QB, 1), lambda b, g, i, *_: (i, 0)),
                  full((nqt, LANES)), full((n_ct, LANES)),
                  per_bg(s, LANES), per_bg(s, DH), per_bg(s, LANES), per_bg(s, DH),
                  per_bg(nc, LANES), per_bg(nc, DH),
                  full((nc, LANES)), full((NSA_HEADS, LANES))],
        out_specs=pl.BlockSpec((1, 1, 1, rows, DH), lambda b, g, i, *_: (b, g, i, 0, 0)),
        scratch_shapes=[pltpu.VMEM((rows, max(nc, NSA_KT)), F32), pltpu.VMEM((rows, 1), F32),
                        pltpu.VMEM((rows, LANES), F32), pltpu.VMEM((rows, DH), F32)],
    )
    out = pl.pallas_call(
        functools.partial(_nsa_body, n_sel=n_sel, sel_k=sel_k),
        grid_spec=grid_spec,
        out_shape=jax.ShapeDtypeStruct((bs, G, nqt, rows, DH), F32),
        compiler_params=_params("parallel", "parallel", "arbitrary"),
        name="nsa_attention",
    )(near_s.reshape(-1).astype(jnp.int32), near_w.reshape(-1).astype(jnp.int32),
      near_c.reshape(-1).astype(jnp.int32),
      qs, gl, positions.reshape(s, 1), positions.reshape(nqt, LANES), cpos.reshape(n_ct, LANES),
      keys_aug(by_group(k_s)), by_group(v_s).astype(BF16), keys_aug(by_group(k_w)),
      by_group(v_w).astype(BF16), kc, vc, jnp.asarray(agg, BF16), td)
    out = out.reshape(bs, G, nqt, HG, QB, DH).transpose(0, 2, 4, 1, 3, 5)
    return out.reshape(bs, s, G * HG * DH)


def _nsa_mixer_jax(q, k_c, v_c, k_s, v_s, k_w, v_w, gate_logits, positions,
                   ck_pe, ck_w1, ck_w2, cv_pe, cv_w1, cv_w2, rel_bias):
    bs, s, _ = q.shape
    G, HG, DH, QB, W = NSA_KV_GROUPS, NSA_HPG, NSA_DH, NSA_QBLOCK, NSA_WINDOW
    q = q.reshape(bs, s, G, HG, DH) * (DH ** -0.5)

    def kv(t):
        return t.reshape(bs, s, G, DH)

    k_c, v_c, k_s, v_s, k_w, v_w = (kv(t) for t in (k_c, v_c, k_s, v_s, k_w, v_w))
    gates = jax.nn.sigmoid(gate_logits).reshape(bs, s, G, HG, 3)
    n_cmp = (s - NSA_CMP_LEN) // NSA_CMP_STRIDE + 1
    cmp_start = jnp.arange(n_cmp) * NSA_CMP_STRIDE
    cmp_end = cmp_start + NSA_CMP_LEN - 1
    cmp_tok = cmp_start[:, None] + jnp.arange(NSA_CMP_LEN)[None, :]

    def compress(t, pe, w1, w2):
        blk = t[:, cmp_tok] + pe[None, None, :, None, :]
        blk = blk.transpose(0, 1, 3, 2, 4).reshape(bs, n_cmp, G, NSA_CMP_LEN * DH)
        return jax.nn.gelu(blk @ w1) @ w2

    kc = compress(k_c, ck_pe, ck_w1, ck_w2)
    vc = compress(v_c, cv_pe, cv_w1, cv_w2)
    cmp_pos = positions[cmp_end]
    n_sel = s // NSA_SEL_LEN
    sel_k = min(NSA_SEL_TOPK, n_sel)
    sel_start = jnp.arange(n_sel) * NSA_SEL_LEN
    agg = ((cmp_start[:, None] <= sel_start[None, :] + NSA_SEL_LEN - 1)
           & (cmp_end[:, None] >= sel_start[None, :])).astype(F32)
    ks_blk = k_s.reshape(bs, n_sel, NSA_SEL_LEN, G, DH).transpose(0, 3, 1, 2, 4)
    vs_blk = v_s.reshape(bs, n_sel, NSA_SEL_LEN, G, DH).transpose(0, 3, 1, 2, 4)
    pos_blk = positions.reshape(n_sel, NSA_SEL_LEN)
    kw_pad = jnp.pad(k_w, ((0, 0), (W, 0), (0, 0), (0, 0)))
    vw_pad = jnp.pad(v_w, ((0, 0), (W, 0), (0, 0), (0, 0)))
    pos_pad = jnp.pad(positions, (W, 0))
    tbl = rel_bias.reshape(REL_BUCKETS, G, HG)
    b_ix = jnp.arange(bs)[:, None, None, None]
    g_ix = jnp.arange(G)[None, :, None, None]
    sel_offs = jnp.arange(NSA_SEL_LEN)
    win_offs = jnp.arange(QB + W)
    blk_j = jnp.arange(n_sel)

    def one_block(qi):
        s0 = qi * QB
        qb = lax.dynamic_slice_in_dim(q, s0, QB, axis=1)
        t = s0 + jnp.arange(QB)
        tp = lax.dynamic_slice_in_dim(positions, s0, QB)
        bias_c = tbl[_rel_bucket(tp[:, None] - cmp_pos[None, :])].transpose(2, 3, 0, 1)
        lg_c = jnp.einsum('bqghd,bcgd->bghqc', qb, kc) + bias_c
        p_c = _masked_softmax(lg_c, cmp_end[None, :] <= t[:, None])
        o_c = jnp.einsum('bghqc,bcgd->bqghd', p_c, vc)
        imp = jnp.einsum('bghqc,cj->bgqj', p_c, agg)
        tb = t // NSA_SEL_LEN
        forced = ((blk_j[None, :] == 0) | (blk_j[None, :] == tb[:, None])
                  | (blk_j[None, :] == tb[:, None] - 1))
        eligible = sel_start[None, :] <= t[:, None]
        score = jnp.where(forced, 1e9, jnp.where(eligible, imp, -1e9))
        _, sel = lax.top_k(score, sel_k)
        k_sel = ks_blk[b_ix, g_ix, sel]
        v_sel = vs_blk[b_ix, g_ix, sel]
        tok_s = sel[..., None] * NSA_SEL_LEN + sel_offs
        bias_s = tbl[_rel_bucket(tp[:, None, None] - pos_blk[sel]), g_ix[..., None]]
        lg_s = jnp.einsum('bqghd,bgqnkd->bghqnk', qb, k_sel) + bias_s.transpose(0, 1, 5, 2, 3, 4)
        valid_s = (tok_s <= t[:, None, None])[:, :, None].reshape(bs, G, 1, QB, -1)
        p_s = _masked_softmax(lg_s.reshape(bs, G, HG, QB, -1), valid_s)
        p_s = p_s.reshape(bs, G, HG, QB, sel_k, NSA_SEL_LEN)
        o_s = jnp.einsum('bghqnk,bgqnkd->bqghd', p_s, v_sel)
        kwb = lax.dynamic_slice_in_dim(kw_pad, s0, QB + W, axis=1)
        vwb = lax.dynamic_slice_in_dim(vw_pad, s0, QB + W, axis=1)
        tok_w = s0 - W + win_offs
        pos_w = lax.dynamic_slice_in_dim(pos_pad, s0, QB + W)
        d_tok = t[:, None] - tok_w[None, :]
        valid_w = (d_tok >= 0) & (d_tok < W) & (tok_w[None, :] >= 0)
        bias_w = tbl[_rel_bucket(tp[:, None] - pos_w[None, :])].transpose(2, 3, 0, 1)
        lg_w = jnp.einsum('bqghd,bkgd->bghqk', qb, kwb) + bias_w
        p_w = _masked_softmax(lg_w, valid_w)
        o_w = jnp.einsum('bghqk,bkgd->bqghd', p_w, vwb)
        gb = lax.dynamic_slice_in_dim(gates, s0, QB, axis=1)
        return gb[..., 0:1] * o_c + gb[..., 1:2] * o_s + gb[..., 2:3] * o_w

    out = lax.map(one_block, jnp.arange(s // QB))
    return out.transpose(1, 0, 2, 3, 4, 5).reshape(bs, s, G * HG * DH)


SGU_TC = 512


def _sgu_body(uv_ref, lg_ref, lb_ref, w_ref, b_ref, o_ref):
    uv = jax.nn.gelu(uv_ref[...])
    u, v = uv[:, :SGU_WIDTH], uv[:, SGU_WIDTH:]
    mu = jnp.mean(v, axis=-1, keepdims=True)
    vc = v - mu
    var = jnp.mean(vc * vc, axis=-1, keepdims=True)
    vn = (vc * lax.rsqrt(var + RMS_EPS) * lg_ref[...] + lb_ref[...]).astype(BF16)
    gw = SGU_WIDTH // SGU_GROUPS
    for c in range(SGU_TC // SGU_CHUNK):
        rows = slice(c * SGU_CHUNK, (c + 1) * SGU_CHUNK)
        for g in range(SGU_GROUPS):
            cols = slice(g * gw, (g + 1) * gw)
            mix = jnp.dot(w_ref[g], vn[rows, cols], preferred_element_type=F32) + b_ref[:, g:g + 1]
            o_ref[rows, cols] = u[rows, cols] * mix


def sgu_mixer(proj, ln_g, ln_b, w_s, b_s):
    t = proj.shape[0]
    assert t % SGU_TC == 0 and OFF_SGU % (2 * SGU_WIDTH) == 0
    fixed = lambda i: (0, 0)
    return pl.pallas_call(
        _sgu_body,
        grid=(t // SGU_TC,),
        in_specs=[pl.BlockSpec((SGU_TC, 2 * SGU_WIDTH), lambda i: (i, OFF_SGU // (2 * SGU_WIDTH))),
                  pl.BlockSpec((1, SGU_WIDTH), fixed), pl.BlockSpec((1, SGU_WIDTH), fixed),
                  pl.BlockSpec((SGU_GROUPS, SGU_CHUNK, SGU_CHUNK), lambda i: (0, 0, 0)),
                  pl.BlockSpec((SGU_CHUNK, SGU_GROUPS), fixed)],
        out_specs=pl.BlockSpec((SGU_TC, SGU_WIDTH), lambda i: (i, 0)),
        out_shape=jax.ShapeDtypeStruct((t, SGU_WIDTH), F32),
        compiler_params=_params("parallel"),
        name="sgu_gate",
    )(proj, ln_g.reshape(1, SGU_WIDTH), ln_b.reshape(1, SGU_WIDTH), jnp.tril(w_s).astype(BF16), b_s.T)


def _sgu_mixer_jax(uv, ln_g, ln_b, w_s, b_s):
    bs, s, _ = uv.shape
    uv = jax.nn.gelu(uv)
    u, v = uv[..., :SGU_WIDTH], uv[..., SGU_WIDTH:]
    mu = jnp.mean(v, axis=-1, keepdims=True)
    var = jnp.mean(jnp.square(v - mu), axis=-1, keepdims=True)
    v = (v - mu) * lax.rsqrt(var + RMS_EPS) * ln_g + ln_b
    nc = s // SGU_CHUNK
    v = v.reshape(bs, nc, SGU_CHUNK, SGU_GROUPS, SGU_WIDTH // SGU_GROUPS)
    mix = (jnp.einsum('gij,bcjgd->bcigd', jnp.tril(w_s), v) + b_s.T[None, None, :, :, None])
    return u * mix.reshape(bs, s, SGU_WIDTH)


def _permute_w_in(w):
    sizes = ([SSM_WIDTH, GDN_QKV, GDN_HEADS, GDN_HEADS, GDN_HEADS * GDN_DV, NSA_Q]
             + [NSA_KV] * 6 + [3 * NSA_HEADS, 2 * SGU_WIDTH, N_BRANCH * D_MODEL])
    cuts = [int(c) for c in np.cumsum(sizes)[:-1]]
    (w_ssm, w_qkv, w_a, w_b, w_z, w_nq, w_kc, w_vc, w_ks, w_vs, w_kw, w_vw,
     w_ng, w_sgu, w_gate) = jnp.split(w, cuts, axis=-1)
    pad = jnp.zeros((w.shape[0], PROJ_COLS - OFF_SMALL - SMALL_USED), w.dtype)
    return jnp.concatenate([w_qkv, w_ssm, w_z, w_nq, w_sgu, w_gate, w_kc, w_vc, w_ks, w_vs,
                            w_kw, w_vw, w_a, w_b, w_ng, pad], axis=-1)


def kernel(x, positions, norm_mix, norm_ffn, norm_final, w_in, ssm_a_re, ssm_a_im, ssm_log_dt, ssm_b_re, ssm_b_im, ssm_c_re, ssm_c_im, ssm_d, ssm_glu_w, gdn_conv_w, gdn_a_log, gdn_dt_bias, gdn_norm, nsa_cmp_k_pe, nsa_cmp_k_w1, nsa_cmp_k_w2, nsa_cmp_v_pe, nsa_cmp_v_w1, nsa_cmp_v_w2, rel_bias, sgu_ln_g, sgu_ln_b, sgu_w, sgu_b, w_br_ssm, w_br_gdn, w_br_nsa, w_br_sgu, w_out, ffn_w_gate, ffn_w_up, ffn_w_down, moe_router, moe_w_gate, moe_w_up, moe_w_down):
    bs, s, d = x.shape
    t = bs * s
    xf = x.reshape(t, d)
    for layer in range(DEPTH):
        proj = in_proj(xf, norm_mix[layer], _permute_w_in(w_in[layer]))
        p3 = proj.reshape(bs, s, PROJ_COLS)

        def seg(off, width):
            return p3[..., off:off + width]

        y_ssm = ssm_mixer(seg(OFF_SSM, SSM_WIDTH), ssm_a_re[layer], ssm_a_im[layer],
                          ssm_log_dt[layer], ssm_b_re[layer], ssm_b_im[layer], ssm_c_re[layer],
                          ssm_c_im[layer], ssm_d[layer], ssm_glu_w[layer])
        y_gdn = gdn_mixer(proj, bs, gdn_conv_w[layer], gdn_a_log[layer], gdn_dt_bias[layer],
                          gdn_norm[layer])
        kvs = [seg(OFF_KV + i * NSA_KV, NSA_KV) for i in range(6)]
        y_nsa = nsa_mixer(seg(OFF_NQ, NSA_Q), *kvs, seg(OFF_SMALL + 2 * GDN_HEADS, 3 * NSA_HEADS),
                          positions, nsa_cmp_k_pe[layer], nsa_cmp_k_w1[layer], nsa_cmp_k_w2[layer],
                          nsa_cmp_v_pe[layer], nsa_cmp_v_w1[layer], nsa_cmp_v_w2[layer], rel_bias)
        y_sgu = sgu_mixer(proj, sgu_ln_g[layer], sgu_ln_b[layer], sgu_w[layer], sgu_b[layer])
        ys = [y_ssm.reshape(t, -1), y_gdn, y_nsa.reshape(t, -1), y_sgu]
        ws = [w[layer].astype(BF16) for w in (w_br_ssm, w_br_gdn, w_br_nsa, w_br_sgu)]
        xf = merge(xf, proj, ys, ws, w_out[layer].astype(BF16))
        i = layer // 2
        if layer % 2 == 0:
            xf = dense_ffn(xf, norm_ffn[layer], ffn_w_gate[i], ffn_w_up[i], ffn_w_down[i])
        else:
            xf = moe_layer(xf, norm_ffn[layer], moe_router[i], moe_w_gate[i], moe_w_up[i],
                           moe_w_down[i], norm_final)
    return xf.reshape(bs, s, d)
```

```python
import functools
import math

import jax
import jax.numpy as jnp
from jax import lax
import numpy as np
from jax.experimental import pallas as pl
from jax.experimental.pallas import tpu as pltpu

F32 = jnp.float32
BF16 = jnp.bfloat16

D_MODEL = 1024
DEPTH = 2
SSM_WIDTH = D_MODEL // 2
SSM_GROUP = 16
SSM_GROUPS = SSM_WIDTH // SSM_GROUP
SSM_STATE = 64
GDN_HEADS = 4
GDN_DK = 128
GDN_DV = 128
GDN_CONV = 4
GDN_CHUNK = 64
NSA_HEADS = 8
NSA_KV_GROUPS = 2
NSA_HPG = NSA_HEADS // NSA_KV_GROUPS
NSA_DH = 64
NSA_CMP_LEN = 32
NSA_CMP_STRIDE = 16
NSA_CMP_HIDDEN = 128
NSA_SEL_LEN = 64
NSA_SEL_TOPK = 16
NSA_WINDOW = 512
NSA_QBLOCK = 128
SGU_WIDTH = D_MODEL // 2
SGU_GROUPS = 4
SGU_CHUNK = 128
REL_BUCKETS = 32
REL_MAX_DIST = 128
FFN_DENSE = 2816
N_EXPERTS = 8
TOP_K = 2
FFN_EXPERT = 3584
N_BRANCH = 4
RMS_EPS = 1e-6
GDN_QKV = GDN_HEADS * (2 * GDN_DK + GDN_DV)
NSA_Q = NSA_HEADS * NSA_DH
NSA_KV = NSA_KV_GROUPS * NSA_DH

LANES = 128
VMEM_LIMIT = 48 * 1024 * 1024

OFF_QKV = 0
OFF_SSM = OFF_QKV + GDN_QKV
OFF_Z = OFF_SSM + SSM_WIDTH
OFF_NQ = OFF_Z + GDN_HEADS * GDN_DV
OFF_SGU = OFF_NQ + NSA_Q
OFF_GATE = OFF_SGU + 2 * SGU_WIDTH
OFF_KV = OFF_GATE + N_BRANCH * D_MODEL
OFF_SMALL = OFF_KV + 6 * NSA_KV
SMALL_USED = 2 * GDN_HEADS + 3 * NSA_HEADS
PROJ_COLS = 9216


def _params(*sem):
    return pltpu.CompilerParams(dimension_semantics=sem, vmem_limit_bytes=VMEM_LIMIT)


def _rms(x, g):
    return x * lax.rsqrt(jnp.mean(x * x, axis=-1, keepdims=True) + RMS_EPS) * g


def _in_proj_body(x_ref, g_ref, w_ref, o_ref, h_ref):
    @pl.when(pl.program_id(1) == 0)
    def _():
        h_ref[...] = _rms(x_ref[...], g_ref[...]).astype(BF16)

    o_ref[...] = jnp.dot(h_ref[...], w_ref[...].astype(BF16), preferred_element_type=F32)


def in_proj(x, g, w, tm=1024, tn=512):
    t, d = x.shape
    n = w.shape[1]
    return pl.pallas_call(
        _in_proj_body,
        grid=(t // tm, n // tn),
        in_specs=[pl.BlockSpec((tm, d), lambda i, j: (i, 0)),
                  pl.BlockSpec((1, d), lambda i, j: (0, 0)),
                  pl.BlockSpec((d, tn), lambda i, j: (0, j))],
        out_specs=pl.BlockSpec((tm, tn), lambda i, j: (i, j)),
        out_shape=jax.ShapeDtypeStruct((t, n), F32),
        scratch_shapes=[pltpu.VMEM((tm, d), BF16)],
        compiler_params=_params("parallel", "arbitrary"),
        name="in_proj",
    )(x, g.reshape(1, d), w)


def _merge_body(x_ref, gate_ref, ya_ref, yb_ref, yc_ref, yd_ref,
                wa_ref, wb_ref, wc_ref, wd_ref, wo_ref, o_ref):
    def branch(k, y_ref, w_ref):
        p = jnp.dot(y_ref[...].astype(BF16), w_ref[...], preferred_element_type=F32)
        return jax.nn.sigmoid(gate_ref[:, k * D_MODEL:(k + 1) * D_MODEL]) * p

    merged = (branch(0, ya_ref, wa_ref) + branch(1, yb_ref, wb_ref)
              + branch(2, yc_ref, wc_ref) + branch(3, yd_ref, wd_ref))
    o_ref[...] = x_ref[...] + jnp.dot(merged.astype(BF16), wo_ref[...],
                                      preferred_element_type=F32)


def merge(x, proj, ys, ws, w_out, tm=256):
    t, d = x.shape
    gate_blk = OFF_GATE // (N_BRANCH * D_MODEL)
    row = lambda i: (i, 0)
    fixed = lambda i: (0, 0)
    in_specs = [pl.BlockSpec((tm, d), row),
                pl.BlockSpec((tm, N_BRANCH * D_MODEL), lambda i: (i, gate_blk))]
    in_specs += [pl.BlockSpec((tm, y.shape[1]), row) for y in ys]
    in_specs += [pl.BlockSpec(w.shape, fixed) for w in ws]
    in_specs += [pl.BlockSpec(w_out.shape, fixed)]
    return pl.pallas_call(
        _merge_body,
        grid=(t // tm,),
        in_specs=in_specs,
        out_specs=pl.BlockSpec((tm, d), row),
        out_shape=jax.ShapeDtypeStruct((t, d), F32),
        compiler_params=_params("parallel"),
        name="merge",
    )(x, proj, *ys, *ws, w_out)


def _ffn_body(x_ref, g_ref, wg_ref, wu_ref, wd_ref, o_ref, h_ref, acc_ref):
    f = pl.program_id(1)

    @pl.when(f == 0)
    def _():
        h_ref[...] = _rms(x_ref[...], g_ref[...]).astype(BF16)
        acc_ref[...] = x_ref[...]

    h = h_ref[...]
    a = jnp.dot(h, wg_ref[...].astype(BF16), preferred_element_type=F32)
    u = jnp.dot(h, wu_ref[...].astype(BF16), preferred_element_type=F32)
    act = (a * jax.nn.sigmoid(a) * u).astype(BF16)
    acc_ref[...] += jnp.dot(act, wd_ref[...].astype(BF16), preferred_element_type=F32)

    @pl.when(f == pl.num_programs(1) - 1)
    def _():
        o_ref[...] = acc_ref[...]


def dense_ffn(x, g, wg, wu, wd, tm=1024, tf=256):
    t, d = x.shape
    nf = wg.shape[1]
    return pl.pallas_call(
        _ffn_body,
        grid=(t // tm, nf // tf),
        in_specs=[pl.BlockSpec((tm, d), lambda i, f: (i, 0)),
                  pl.BlockSpec((1, d), lambda i, f: (0, 0)),
                  pl.BlockSpec((d, tf), lambda i, f: (0, f)),
                  pl.BlockSpec((d, tf), lambda i, f: (0, f)),
                  pl.BlockSpec((tf, d), lambda i, f: (f, 0))],
        out_specs=pl.BlockSpec((tm, d), lambda i, f: (i, 0)),
        out_shape=jax.ShapeDtypeStruct((t, d), F32),
        scratch_shapes=[pltpu.VMEM((tm, d), BF16), pltpu.VMEM((tm, d), F32)],
        compiler_params=_params("parallel", "arbitrary"),
        name="dense_ffn",
    )(x, g.reshape(1, d), wg, wu, wd)


def _route_body(x_ref, g_ref, rhi_ref, rlo_ref, h_ref, idx_ref, wt_ref):
    h = _rms(x_ref[...], g_ref[...])
    hi = h.astype(BF16)
    lo = (h - hi.astype(F32)).astype(BF16)
    h_ref[...] = h
    logits = (jnp.dot(hi, rhi_ref[...], preferred_element_type=F32)
              + jnp.dot(hi, rlo_ref[...], preferred_element_type=F32)
              + jnp.dot(lo, rhi_ref[...], preferred_element_type=F32))
    col = lax.broadcasted_iota(jnp.int32, logits.shape, 1).astype(F32)
    neg = jnp.float32(-jnp.inf)
    lg = jnp.where(col < N_EXPERTS, logits, neg)
    m1 = jnp.max(lg, axis=-1, keepdims=True)
    i1 = jnp.min(jnp.where(lg == m1, col, float(LANES)), axis=-1, keepdims=True)
    lg2 = jnp.where(col == i1, neg, lg)
    m2 = jnp.max(lg2, axis=-1, keepdims=True)
    i2 = jnp.min(jnp.where(lg2 == m2, col, float(LANES)), axis=-1, keepdims=True)
    e = jnp.exp(m2 - m1)
    w1 = 1.0 / (1.0 + e)
    w2 = e / (1.0 + e)
    idx_ref[...] = jnp.where(col == 0, i1, jnp.where(col == 1, i2, 0.0)).astype(jnp.int32)
    wt_ref[...] = jnp.where(col == 0, w1, jnp.where(col == 1, w2, 0.0))


def moe_route(x, g, r_hi, r_lo, tm=512):
    t, d = x.shape
    row = lambda i: (i, 0)
    fixed = lambda i: (0, 0)
    return pl.pallas_call(
        _route_body,
        grid=(t // tm,),
        in_specs=[pl.BlockSpec((tm, d), row), pl.BlockSpec((1, d), fixed),
                  pl.BlockSpec((d, LANES), fixed), pl.BlockSpec((d, LANES), fixed)],
        out_specs=[pl.BlockSpec((tm, d), row), pl.BlockSpec((tm, LANES), row),
                   pl.BlockSpec((tm, LANES), row)],
        out_shape=[jax.ShapeDtypeStruct((t, d), F32),
                   jax.ShapeDtypeStruct((t, LANES), jnp.int32),
                   jax.ShapeDtypeStruct((t, LANES), F32)],
        compiler_params=_params("parallel"),
        name="moe_route",
    )(x, g.reshape(1, d), r_hi, r_lo)


def _experts_body(te_ref, nu_ref, src_ref, h_hbm, rw_ref, wg_ref, wu_ref, wd_ref, o_ref,
                  xbuf_ref, xs_ref, acc_ref, sem):
    i = pl.program_id(0)
    f = pl.program_id(1)
    tm = xs_ref.shape[0]
    n_used = nu_ref[0]
    used = i < n_used

    def gather(tile, slot):
        def row(r, carry):
            tok = src_ref[tile * tm + r]
            pltpu.make_async_copy(h_hbm.at[pl.ds(tok, 1), :], xbuf_ref.at[slot, pl.ds(r, 1), :],
                                  sem.at[slot]).start()
            return carry

        lax.fori_loop(0, tm, row, 0, unroll=8)

    @pl.when(f == 0)
    def _():
        acc_ref[...] = jnp.zeros_like(acc_ref)

        @pl.when(used)
        def _():
            slot = i % 2

            @pl.when(i == 0)
            def _():
                gather(0, 0)

            pltpu.make_async_copy(h_hbm.at[pl.ds(0, tm), :], xbuf_ref.at[slot], sem.at[slot]).wait()

            @pl.when(i + 1 < n_used)
            def _():
                gather(i + 1, 1 - slot)

            xs_ref[...] = xbuf_ref[slot].astype(BF16)

    @pl.when(used)
    def _():
        h = xs_ref[...]
        a = jnp.dot(h, wg_ref[0].astype(BF16), preferred_element_type=F32)
        u = jnp.dot(h, wu_ref[0].astype(BF16), preferred_element_type=F32)
        act = (a * jax.nn.sigmoid(a) * u).astype(BF16)
        acc_ref[...] += jnp.dot(act, wd_ref[0].astype(BF16), preferred_element_type=F32)

    @pl.when(f == pl.num_programs(1) - 1)
    def _():
        o_ref[...] = (acc_ref[...] * rw_ref[...]).astype(o_ref.dtype)


def moe_experts(tile_expert, n_used, src, h, row_w, wg, wu, wd, tm, tf=512):
    p = src.shape[0]
    d = h.shape[1]
    nf = wg.shape[2]
    grid_spec = pltpu.PrefetchScalarGridSpec(
        num_scalar_prefetch=3,
        grid=(p // tm, nf // tf),
        in_specs=[pl.BlockSpec(memory_space=pl.ANY),
                  pl.BlockSpec((tm, 1), lambda i, f, *_: (i, 0)),
                  pl.BlockSpec((1, d, tf), lambda i, f, te, *_: (te[i], 0, f)),
                  pl.BlockSpec((1, d, tf), lambda i, f, te, *_: (te[i], 0, f)),
                  pl.BlockSpec((1, tf, d), lambda i, f, te, *_: (te[i], f, 0))],
        out_specs=pl.BlockSpec((tm, d), lambda i, f, *_: (i, 0)),
        scratch_shapes=[pltpu.VMEM((2, tm, d), F32), pltpu.VMEM((tm, d), BF16),
                        pltpu.VMEM((tm, d), F32), pltpu.SemaphoreType.DMA((2,))],
    )
    return pl.pallas_call(
        _experts_body,
        grid_spec=grid_spec,
        out_shape=jax.ShapeDtypeStruct((p, d), BF16),
        compiler_params=_params("arbitrary", "arbitrary"),
        name="moe_experts",
    )(tile_expert, n_used, src, h, row_w, wg, wu, wd)


def _combine_norm_body(x_ref, ya_ref, yb_ref, g_ref, o_ref):
    o_ref[...] = _rms(x_ref[...] + ya_ref[...].astype(F32) + yb_ref[...].astype(F32), g_ref[...])


def combine_norm(x, ya, yb, g, tm=1024):
    t, d = x.shape
    row = lambda i: (i, 0)
    return pl.pallas_call(
        _combine_norm_body,
        grid=(t // tm,),
        in_specs=[pl.BlockSpec((tm, d), row)] * 3 + [pl.BlockSpec((1, d), lambda i: (0, 0))],
        out_specs=pl.BlockSpec((tm, d), row),
        out_shape=jax.ShapeDtypeStruct((t, d), F32),
        compiler_params=_params("parallel"),
        name="combine_norm",
    )(x, ya, yb, g.reshape(1, d))


def moe_layer(x, g_norm, router, wg, wu, wd, g_final, tm=1024):
    t, d = x.shape
    r = jnp.zeros((d, LANES), F32).at[:, :N_EXPERTS].set(router)
    r_hi = r.astype(BF16)
    r_lo = (r - r_hi.astype(F32)).astype(BF16)
    h, idx, wts = moe_route(x, g_norm, r_hi, r_lo)
    e_flat = jnp.concatenate([idx[:, 0], idx[:, 1]])
    w_flat = jnp.concatenate([wts[:, 0], wts[:, 1]])
    onehot = (e_flat[:, None] == jnp.arange(N_EXPERTS)[None, :]).astype(jnp.int32)
    csum = jnp.cumsum(onehot, axis=0)
    rank = jnp.sum((csum - onehot) * onehot, axis=1)
    counts = csum[-1]
    padded = ((counts + tm - 1) // tm) * tm
    ends = jnp.cumsum(padded)
    starts = ends - padded
    dest = starts[e_flat] + rank
    n_tiles = (TOP_K * t) // tm + N_EXPERTS
    p = n_tiles * tm
    tok = jnp.concatenate([jnp.arange(t, dtype=jnp.int32)] * TOP_K)
    src = jnp.zeros((p,), jnp.int32).at[dest].set(tok)
    row_w = jnp.zeros((p,), F32).at[dest].set(w_flat)
    tile_expert = jnp.minimum(
        jnp.searchsorted(ends, jnp.arange(n_tiles, dtype=jnp.int32) * tm, side="right"),
        N_EXPERTS - 1).astype(jnp.int32)
    n_used = (ends[-1] // tm).astype(jnp.int32).reshape(1)
    ys = moe_experts(tile_expert, n_used, src, h, row_w.reshape(p, 1), wg, wu, wd, tm)
    ya = jnp.take(ys, dest[:t], axis=0)
    yb = jnp.take(ys, dest[t:], axis=0)
    return combine_norm(x, ya, yb, g_final)


def _l2_norm(t):
    return t * lax.rsqrt(jnp.sum(t * t, axis=-1, keepdims=True) + 1e-6)


def _masked_softmax(logits, valid):
    logits = jnp.where(valid, logits.astype(F32), -1e30)
    return jax.nn.softmax(logits, axis=-1) * valid


def _rel_bucket(dist):
    n = jnp.maximum(dist, 0)
    max_exact = REL_BUCKETS // 2
    nf = jnp.maximum(n, 1).astype(F32)
    large = max_exact + (jnp.log(nf / max_exact) / math.log(REL_MAX_DIST / max_exact)
                         * (REL_BUCKETS - max_exact)).astype(jnp.int32)
    large = jnp.minimum(large, REL_BUCKETS - 1)
    return jnp.where(n < max_exact, n, large)


SSM_TC = 512
SSM_SUB = 128
SSM_HALF = 256
SSM_NSTATE = SSM_GROUPS * SSM_STATE


def _ssm_body(u_ref, bw_ref, cw_ref, d_ref, glu_ref, ar_ref, ai_ref, pr_ref, pi_ref, o_ref,
              xr_ref, xi_ref, cr_ref, ci_ref):
    @pl.when(pl.program_id(1) == 0)
    def _():
        cr_ref[...] = jnp.zeros_like(cr_ref)
        ci_ref[...] = jnp.zeros_like(ci_ref)

    u = u_ref[0]
    ub = u.astype(BF16)
    nblk = SSM_WIDTH // SSM_HALF
    sblk = SSM_NSTATE // nblk
    for k in range(nblk):
        bu = jnp.dot(ub[:, k * SSM_HALF:(k + 1) * SSM_HALF], bw_ref[k], preferred_element_type=F32)
        xr_ref[:, k * sblk:(k + 1) * sblk] = bu[:, :sblk]
        xi_ref[:, k * sblk:(k + 1) * sblk] = bu[:, sblk:]

    row = lax.broadcasted_iota(jnp.int32, (SSM_SUB, LANES), 0)
    n_steps = SSM_SUB.bit_length() - 1

    def shift(x, d):
        if d % 8 == 0:
            return jnp.concatenate([jnp.zeros((d, LANES), F32), x[:SSM_SUB - d]], axis=0)
        return jnp.where(row >= d, pltpu.roll(x, d, 0), 0.0)

    def strip(c, carry):
        col = pl.ds(pl.multiple_of(c * LANES, LANES), LANES)
        c_r = cr_ref[:, col]
        c_i = ci_ref[:, col]
        for sub in range(SSM_TC // SSM_SUB):
            rows = slice(sub * SSM_SUB, (sub + 1) * SSM_SUB)
            xr = xr_ref[rows, col]
            xi = xi_ref[rows, col]
            for i in range(n_steps):
                a_r = ar_ref[i:i + 1, col]
                a_i = ai_ref[i:i + 1, col]
                sr, si = shift(xr, 1 << i), shift(xi, 1 << i)
                xr, xi = xr + a_r * sr - a_i * si, xi + a_r * si + a_i * sr
            p_r = pr_ref[:, col]
            p_i = pi_ref[:, col]
            xr, xi = xr + p_r * c_r - p_i * c_i, xi + p_r * c_i + p_i * c_r
            xr_ref[rows, col] = xr
            xi_ref[rows, col] = xi
            c_r = xr[SSM_SUB - 1:SSM_SUB, :]
            c_i = xi[SSM_SUB - 1:SSM_SUB, :]
        cr_ref[:, col] = c_r
        ci_ref[:, col] = c_i
        return carry

    lax.fori_loop(0, SSM_NSTATE // LANES, strip, 0)

    ys = []
    for k in range(nblk):
        st = jnp.concatenate([xr_ref[:, k * sblk:(k + 1) * sblk].astype(BF16),
                              xi_ref[:, k * sblk:(k + 1) * sblk].astype(BF16)], axis=1)
        ys.append(jnp.dot(st, cw_ref[k], preferred_element_type=F32))
    y = jax.nn.gelu(jnp.concatenate(ys, axis=1) + d_ref[...] * u)
    o_ref[0] = y * jax.nn.sigmoid(jnp.dot(y.astype(BF16), glu_ref[...], preferred_element_type=F32))


def ssm_mixer(u, a_re, a_im, log_dt, b_re, b_im, c_re, c_im, d_skip, glu_w):
    bs, s, _ = u.shape
    assert s % SSM_TC == 0
    nblk = SSM_WIDTH // SSM_HALF
    gpb = SSM_GROUPS // nblk
    dt = jnp.exp(log_dt)[:, None]
    mag = jnp.exp(a_re * dt)
    abar_r, abar_i = mag * jnp.cos(a_im * dt), mag * jnp.sin(a_im * dt)
    nr, ni = abar_r - 1.0, abar_i
    den = a_re * a_re + a_im * a_im
    sr, si = (nr * a_re + ni * a_im) / den, (ni * a_re - nr * a_im) / den
    bbar_r = sr[..., None] * b_re - si[..., None] * b_im
    bbar_i = sr[..., None] * b_im + si[..., None] * b_re

    def powers(k):
        kk = k.astype(F32)[:, None, None]
        m = jnp.exp(kk * (a_re * dt))
        return ((m * jnp.cos(kk * (a_im * dt))).reshape(-1, SSM_NSTATE),
                (m * jnp.sin(kk * (a_im * dt))).reshape(-1, SSM_NSTATE))

    ar, ai = powers(2 ** jnp.arange(8))
    pr, pi = powers(jnp.arange(1, SSM_SUB + 1))
    eye = jnp.eye(gpb, dtype=F32)

    def in_block(w):
        return jnp.einsum('gnp,gh->gphn', w, eye).reshape(gpb * SSM_GROUP, gpb * SSM_STATE)

    def out_block(w):
        return jnp.einsum('gpn,gh->gnhp', w, eye).reshape(gpb * SSM_STATE, gpb * SSM_GROUP)

    bw = jnp.stack([jnp.concatenate([in_block(bbar_r[k * gpb:(k + 1) * gpb]),
                                     in_block(bbar_i[k * gpb:(k + 1) * gpb])], axis=1)
                    for k in range(nblk)]).astype(BF16)
    cw = jnp.stack([jnp.concatenate([out_block(c_re[k * gpb:(k + 1) * gpb]),
                                     -out_block(c_im[k * gpb:(k + 1) * gpb])], axis=0)
                    for k in range(nblk)]).astype(BF16)
    fixed2 = lambda b, i: (0, 0)
    fixed3 = lambda b, i: (0, 0, 0)
    return pl.pallas_call(
        _ssm_body,
        grid=(bs, s // SSM_TC),
        in_specs=[pl.BlockSpec((1, SSM_TC, SSM_WIDTH), lambda b, i: (b, i, 0)),
                  pl.BlockSpec(bw.shape, fixed3), pl.BlockSpec(cw.shape, fixed3),
                  pl.BlockSpec((1, SSM_WIDTH), fixed2), pl.BlockSpec((SSM_WIDTH, SSM_WIDTH), fixed2),
                  pl.BlockSpec((8, SSM_NSTATE), fixed2), pl.BlockSpec((8, SSM_NSTATE), fixed2),
                  pl.BlockSpec((SSM_SUB, SSM_NSTATE), fixed2), pl.BlockSpec((SSM_SUB, SSM_NSTATE), fixed2)],
        out_specs=pl.BlockSpec((1, SSM_TC, SSM_WIDTH), lambda b, i: (b, i, 0)),
        out_shape=jax.ShapeDtypeStruct((bs, s, SSM_WIDTH), F32),
        scratch_shapes=[pltpu.VMEM((SSM_TC, SSM_NSTATE), F32), pltpu.VMEM((SSM_TC, SSM_NSTATE), F32),
                        pltpu.VMEM((1, SSM_NSTATE), F32), pltpu.VMEM((1, SSM_NSTATE), F32)],
        compiler_params=_params("parallel", "arbitrary"),
        name="ssm_scan",
    )(u, bw, cw, d_skip.reshape(1, SSM_WIDTH), glu_w.astype(BF16), ar, ai, pr, pi)


def _ssm_mixer_jax(u, a_re, a_im, log_dt, b_re, b_im, c_re, c_im, d_skip, glu_w):
    bs, s, _ = u.shape
    uf = u.reshape(bs, s, SSM_GROUPS, SSM_GROUP)
    dt = jnp.exp(log_dt)[:, None]
    mag = jnp.exp(a_re * dt)
    abar_r, abar_i = mag * jnp.cos(a_im * dt), mag * jnp.sin(a_im * dt)
    nr, ni = abar_r - 1.0, abar_i
    den = a_re * a_re + a_im * a_im
    sr, si = (nr * a_re + ni * a_im) / den, (ni * a_re - nr * a_im) / den
    bbar_r = sr[..., None] * b_re - si[..., None] * b_im
    bbar_i = sr[..., None] * b_im + si[..., None] * b_re
    bu_r = jnp.einsum('bsgp,gnp->bsgn', uf, bbar_r)
    bu_i = jnp.einsum('bsgp,gnp->bsgn', uf, bbar_i)
    ar = jnp.broadcast_to(abar_r, bu_r.shape)
    ai = jnp.broadcast_to(abar_i, bu_r.shape)

    def combine(e1, e2):
        a1r, a1i, b1r, b1i = e1
        a2r, a2i, b2r, b2i = e2
        return (a2r * a1r - a2i * a1i, a2r * a1i + a2i * a1r,
                a2r * b1r - a2i * b1i + b2r, a2r * b1i + a2i * b1r + b2i)

    _, _, st_r, st_i = lax.associative_scan(combine, (ar, ai, bu_r, bu_i), axis=1)
    y = (jnp.einsum('gpn,bsgn->bsgp', c_re, st_r) - jnp.einsum('gpn,bsgn->bsgp', c_im, st_i)
         + d_skip.reshape(SSM_GROUPS, SSM_GROUP) * uf)
    y = jax.nn.gelu(y.reshape(bs, s, SSM_WIDTH))
    return y * jax.nn.sigmoid(y @ glu_w)


GDN_TC = 256
GDN_HALO = 8
MASKED = -1e30


def _dot_nt(a, b):
    return lax.dot_general(a, b, (((1,), (1,)), ((), ())), preferred_element_type=F32)


def _dot_tn(a, b):
    return lax.dot_general(a, b, (((0,), (0,)), ((), ())), preferred_element_type=F32)


def _split_bf16(x):
    hi = x.astype(BF16)
    return hi, (x - hi.astype(F32)).astype(BF16)


def _gdn_body(nega_ref, dtb_ref, q_ref, k_ref, v_ref, z_ref, sm_ref, wq_ref, wk_ref, wv_ref, ng_ref,
              o_ref, state_ref, hq_ref, hk_ref, hv_ref, vnew_ref):
    TC, CH, DK, H = GDN_TC, GDN_CHUNK, GDN_DK, GDN_HEADS
    heads = range(H)

    @pl.when(pl.program_id(1) == 0)
    def _():
        state_ref[...] = jnp.zeros_like(state_ref)
        hq_ref[...] = jnp.zeros_like(hq_ref)
        hk_ref[...] = jnp.zeros_like(hk_ref)
        hv_ref[...] = jnp.zeros_like(hv_ref)

    def conv_silu(x_ref, halo_ref, w_ref):
        x = x_ref[...]
        xx = jnp.concatenate([halo_ref[...], x], axis=0)
        w = w_ref[...]
        y = w[GDN_CONV - 1:GDN_CONV] * x
        for d in range(1, GDN_CONV):
            y = y + w[GDN_CONV - 1 - d:GDN_CONV - d] * pltpu.roll(xx, d, 0)[GDN_HALO:GDN_HALO + TC]
        halo_ref[...] = x[TC - GDN_HALO:TC]
        return y * jax.nn.sigmoid(y)

    def per_head(x):
        return [x[:, h * LANES:(h + 1) * LANES] for h in heads]

    q = per_head(conv_silu(q_ref, hq_ref, wq_ref))
    k = per_head(conv_silu(k_ref, hk_ref, wk_ref))
    v = per_head(conv_silu(v_ref, hv_ref, wv_ref))
    q = [x * lax.rsqrt(jnp.sum(x * x, axis=-1, keepdims=True) + 1e-6) * (DK ** -0.5) for x in q]
    k = [x * lax.rsqrt(jnp.sum(x * x, axis=-1, keepdims=True) + 1e-6) for x in k]

    small = sm_ref[...]
    beta = [jax.nn.sigmoid(small[:, H + h:H + h + 1]) for h in heads]
    la = []
    for h in heads:
        xa = small[:, h:h + 1] + dtb_ref[h]
        softplus = jnp.maximum(xa, 0.0) + jnp.log(1.0 + jnp.exp(-jnp.abs(xa)))
        la.append(jnp.broadcast_to(nega_ref[h] * softplus, (TC, LANES)))

    ri = lax.broadcasted_iota(jnp.int32, (TC, TC), 0)
    ci = lax.broadcasted_iota(jnp.int32, (TC, TC), 1)
    same = (ri // CH) == (ci // CH)
    causal = jnp.where(same, jnp.where(ci <= ri, 1.0, 0.0), 0.0)
    strict = jnp.where(ci < ri, causal, 0.0)
    eye = jnp.where(ri == ci, 1.0, 0.0)
    tri = causal.astype(BF16)
    lane = lax.broadcasted_iota(jnp.int32, (TC, LANES), 1)

    g = []
    for h in heads:
        la_hi, la_lo = _split_bf16(la[h])
        g.append(jnp.dot(tri, la_hi, preferred_element_type=F32)
                 + jnp.dot(tri, la_lo, preferred_element_type=F32))
    decay = []
    for h in heads:
        g_hi = g[h].astype(BF16).astype(F32)
        g_lo = g[h] - g_hi
        lhs = jnp.where(lane == 0, g_hi, jnp.where(lane == 1, g_lo, jnp.where(lane < 4, 1.0, 0.0)))
        rhs = jnp.where(lane < 2, 1.0, jnp.where(lane == 2, -g_hi, jnp.where(lane == 3, -g_lo, 0.0)))
        decay.append(jnp.exp(jnp.where(causal > 0.5, _dot_nt(lhs.astype(BF16), rhs.astype(BF16)), MASKED)))

    kb = [k[h] * beta[h] for h in heads]
    k_b = [x.astype(BF16) for x in k]
    lmat = [strict * _dot_nt(kb[h].astype(BF16), k_b[h]) * decay[h] for h in heads]
    tinv = [eye - x for x in lmat]
    pw = lmat
    for _ in range(CH.bit_length() - 2):
        pw = [jnp.dot(x.astype(BF16), x.astype(BF16), preferred_element_type=F32) for x in pw]
        tinv = [tinv[h] + jnp.dot(tinv[h].astype(BF16), pw[h].astype(BF16), preferred_element_type=F32)
                for h in heads]
    eg = [jnp.exp(x) for x in g]
    sol = [jnp.dot(tinv[h].astype(BF16),
                   jnp.concatenate([v[h] * beta[h], kb[h] * eg[h]], axis=1).astype(BF16),
                   preferred_element_type=F32) for h in heads]
    attn = [(causal * _dot_nt(q[h].astype(BF16), k_b[h]) * decay[h]).astype(BF16) for h in heads]
    q_st = [(q[h] * eg[h]).astype(BF16) for h in heads]

    vnew_ref[...] = jnp.zeros_like(vnew_ref)
    for c in range(TC // CH):
        rows = slice(c * CH, (c + 1) * CH)
        g_last = [x[(c + 1) * CH - 1:(c + 1) * CH, :] for x in g]
        st = [state_ref[h] for h in heads]
        st_b = [x.astype(BF16) for x in st]
        v_new = [sol[h][rows, :GDN_DV]
                 - jnp.dot(sol[h][rows, GDN_DV:].astype(BF16), st_b[h], preferred_element_type=F32)
                 for h in heads]
        for h in heads:
            vnew_ref[h, rows, :] = v_new[h].astype(BF16)
        o_c = [jnp.dot(q_st[h][rows], st_b[h], preferred_element_type=F32)
               + jnp.dot(attn[h][rows], vnew_ref[h], preferred_element_type=F32) for h in heads]
        for h in heads:
            k_st = (k[h][rows] * jnp.exp(g_last[h] - g[h][rows])).astype(BF16)
            state_ref[h] = st[h] * jnp.exp(g_last[h][:, :1]) + _dot_tn(k_st, v_new[h].astype(BF16))
        for h in heads:
            o = o_c[h] * lax.rsqrt(jnp.mean(o_c[h] * o_c[h], axis=-1, keepdims=True) + RMS_EPS) * ng_ref[...]
            zc = z_ref[rows, h * LANES:(h + 1) * LANES]
            o_ref[rows, h * LANES:(h + 1) * LANES] = o * (zc * jax.nn.sigmoid(zc))


def gdn_mixer(proj, bs, conv_w, a_log, dt_bias, norm_g):
    t = proj.shape[0]
    s = t // bs
    nt = s // GDN_TC
    H = GDN_HEADS
    hw = H * LANES
    assert s % GDN_TC == 0 and GDN_DK == LANES and GDN_DV == LANES and OFF_QKV == 0 and OFF_Z % hw == 0
    cw = jnp.zeros((8, GDN_QKV), F32).at[:GDN_CONV].set(conv_w)
    tok = lambda blk: pl.BlockSpec((GDN_TC, hw), lambda b, i: (b * nt + i, blk))
    wspec = lambda blk: pl.BlockSpec((8, hw), lambda b, i: (0, blk))
    smem = pl.BlockSpec(memory_space=pltpu.SMEM)
    return pl.pallas_call(
        _gdn_body,
        grid=(bs, nt),
        in_specs=[smem, smem, tok(0), tok(1), tok(2), tok(OFF_Z // hw),
                  pl.BlockSpec((GDN_TC, 2 * LANES), lambda b, i: (b * nt + i, OFF_SMALL // (2 * LANES))),
                  wspec(0), wspec(1), wspec(2), pl.BlockSpec((1, GDN_DV), lambda b, i: (0, 0))],
        out_specs=pl.BlockSpec((GDN_TC, hw), lambda b, i: (b * nt + i, 0)),
        out_shape=jax.ShapeDtypeStruct((t, hw), F32),
        scratch_shapes=[pltpu.VMEM((H, GDN_DK, GDN_DV), F32)] + [pltpu.VMEM((GDN_HALO, hw), F32)] * 3
        + [pltpu.VMEM((H, GDN_TC, GDN_DV), BF16)],
        compiler_params=_params("parallel", "arbitrary"),
        name="gdn_delta",
    )(-jnp.exp(a_log), dt_bias.astype(F32), proj, proj, proj, proj, proj, cw, cw, cw,
      norm_g.reshape(1, GDN_DV))


def _gdn_body_per_head(nega_ref, dtb_ref, q_ref, k_ref, v_ref, z_ref, sm_ref, wq_ref, wk_ref, wv_ref,
                       ng_ref, o_ref, state_ref, hq_ref, hk_ref, hv_ref, vnew_ref):
    TC, CH, DK = GDN_TC, GDN_CHUNK, GDN_DK
    h = pl.program_id(1)

    @pl.when(pl.program_id(2) == 0)
    def _():
        state_ref[...] = jnp.zeros_like(state_ref)
        hq_ref[...] = jnp.zeros_like(hq_ref)
        hk_ref[...] = jnp.zeros_like(hk_ref)
        hv_ref[...] = jnp.zeros_like(hv_ref)

    def conv_silu(x_ref, halo_ref, w_ref):
        x = x_ref[...]
        xx = jnp.concatenate([halo_ref[...], x], axis=0)
        w = w_ref[...]
        y = w[GDN_CONV - 1:GDN_CONV] * x
        for d in range(1, GDN_CONV):
            y = y + w[GDN_CONV - 1 - d:GDN_CONV - d] * pltpu.roll(xx, d, 0)[GDN_HALO:GDN_HALO + TC]
        halo_ref[...] = x[TC - GDN_HALO:TC]
        return y * jax.nn.sigmoid(y)

    q = conv_silu(q_ref, hq_ref, wq_ref)
    k = conv_silu(k_ref, hk_ref, wk_ref)
    v = conv_silu(v_ref, hv_ref, wv_ref)
    q = q * lax.rsqrt(jnp.sum(q * q, axis=-1, keepdims=True) + 1e-6) * (DK ** -0.5)
    k = k * lax.rsqrt(jnp.sum(k * k, axis=-1, keepdims=True) + 1e-6)

    small = sm_ref[...]
    lane_s = lax.broadcasted_iota(jnp.int32, small.shape, 1)
    a_col = jnp.sum(jnp.where(lane_s == h, small, 0.0), axis=-1, keepdims=True)
    b_col = jnp.sum(jnp.where(lane_s == GDN_HEADS + h, small, 0.0), axis=-1, keepdims=True)
    beta = jax.nn.sigmoid(b_col)
    xa = a_col + dtb_ref[h]
    softplus = jnp.maximum(xa, 0.0) + jnp.log(1.0 + jnp.exp(-jnp.abs(xa)))
    la = jnp.broadcast_to(nega_ref[h] * softplus, (TC, LANES))

    ri = lax.broadcasted_iota(jnp.int32, (TC, TC), 0)
    ci = lax.broadcasted_iota(jnp.int32, (TC, TC), 1)
    same = (ri // CH) == (ci // CH)
    causal = jnp.where(same, jnp.where(ci <= ri, 1.0, 0.0), 0.0)
    strict = jnp.where(ci < ri, causal, 0.0)
    tri = causal.astype(BF16)
    la_hi, la_lo = _split_bf16(la)
    g = (jnp.dot(tri, la_hi, preferred_element_type=F32)
         + jnp.dot(tri, la_lo, preferred_element_type=F32))
    g_hi = g.astype(BF16).astype(F32)
    g_lo = g - g_hi
    lane = lax.broadcasted_iota(jnp.int32, (TC, LANES), 1)
    lhs = jnp.where(lane == 0, g_hi, jnp.where(lane == 1, g_lo, jnp.where(lane < 4, 1.0, 0.0))).astype(BF16)
    rhs = jnp.where(lane < 2, 1.0, jnp.where(lane == 2, -g_hi, jnp.where(lane == 3, -g_lo, 0.0))).astype(BF16)
    decay = jnp.exp(jnp.where(causal > 0.5, _dot_nt(lhs, rhs), MASKED))

    kb = k * beta
    k_b = k.astype(BF16)
    lmat = strict * _dot_nt(kb.astype(BF16), k_b) * decay
    eye = jnp.where(ri == ci, 1.0, 0.0)
    tinv = eye - lmat
    pw = lmat
    for _ in range(CH.bit_length() - 2):
        pb = pw.astype(BF16)
        pw = jnp.dot(pb, pb, preferred_element_type=F32)
        tinv = tinv + jnp.dot(tinv.astype(BF16), pw.astype(BF16), preferred_element_type=F32)
    eg = jnp.exp(g)
    rhs_all = jnp.concatenate([v * beta, kb * eg], axis=1).astype(BF16)
    sol = jnp.dot(tinv.astype(BF16), rhs_all, preferred_element_type=F32)
    u_val, w_val = sol[:, :GDN_DV], sol[:, GDN_DV:]
    attn = (causal * _dot_nt(q.astype(BF16), k_b) * decay).astype(BF16)
    q_st = (q * eg).astype(BF16)

    vnew_ref[...] = jnp.zeros_like(vnew_ref)
    for c in range(TC // CH):
        rows = slice(c * CH, (c + 1) * CH)
        g_last = g[(c + 1) * CH - 1:(c + 1) * CH, :]
        st = state_ref[...]
        st_b = st.astype(BF16)
        v_new = u_val[rows] - jnp.dot(w_val[rows].astype(BF16), st_b, preferred_element_type=F32)
        vnew_ref[rows, :] = v_new.astype(BF16)
        o_c = (jnp.dot(q_st[rows], st_b, preferred_element_type=F32)
               + jnp.dot(attn[rows], vnew_ref[...], preferred_element_type=F32))
        k_st = (k[rows] * jnp.exp(g_last - g[rows])).astype(BF16)
        state_ref[...] = st * jnp.exp(g_last[:, :1]) + _dot_tn(k_st, v_new.astype(BF16))
        o_c = o_c * lax.rsqrt(jnp.mean(o_c * o_c, axis=-1, keepdims=True) + RMS_EPS) * ng_ref[...]
        zc = z_ref[rows, :]
        o_ref[rows, :] = o_c * (zc * jax.nn.sigmoid(zc))


def _gdn_mixer_per_head(proj, bs, conv_w, a_log, dt_bias, norm_g):
    t = proj.shape[0]
    s = t // bs
    nt = s // GDN_TC
    assert s % GDN_TC == 0 and GDN_DK == LANES and GDN_DV == LANES and OFF_QKV == 0
    H = GDN_HEADS
    cw = jnp.zeros((8, GDN_QKV), F32).at[:GDN_CONV].set(conv_w)
    tok = lambda off: pl.BlockSpec((GDN_TC, LANES), lambda b, h, i, *_: (b * nt + i, off + h))
    wspec = lambda off: pl.BlockSpec((8, LANES), lambda b, h, i, *_: (0, off + h))
    grid_spec = pltpu.PrefetchScalarGridSpec(
        num_scalar_prefetch=0,
        grid=(bs, H, nt),
        in_specs=[pl.BlockSpec(memory_space=pltpu.SMEM), pl.BlockSpec(memory_space=pltpu.SMEM),
                  tok(0), tok(H), tok(2 * H), tok(OFF_Z // LANES),
                  pl.BlockSpec((GDN_TC, 2 * LANES), lambda b, h, i, *_: (b * nt + i, OFF_SMALL // (2 * LANES))),
                  wspec(0), wspec(H), wspec(2 * H),
                  pl.BlockSpec((1, GDN_DV), lambda b, h, i, *_: (0, 0))],
        out_specs=pl.BlockSpec((GDN_TC, GDN_DV), lambda b, h, i, *_: (b * nt + i, h)),
        scratch_shapes=[pltpu.VMEM((GDN_DK, GDN_DV), F32)] + [pltpu.VMEM((GDN_HALO, LANES), F32)] * 3
        + [pltpu.VMEM((GDN_TC, GDN_DV), BF16)],
    )
    return pl.pallas_call(
        _gdn_body,
        grid_spec=grid_spec,
        out_shape=jax.ShapeDtypeStruct((t, H * GDN_DV), F32),
        compiler_params=_params("parallel", "parallel", "arbitrary"),
        name="gdn_delta",
    )(-jnp.exp(a_log), dt_bias.astype(F32), proj, proj, proj, proj, proj, cw, cw, cw,
      norm_g.reshape(1, GDN_DV))


def _gdn_mixer_jax(qkv, a, b, z, conv_w, a_log, dt_bias, norm_g):
    bs, s, c = qkv.shape
    H, DK, DV, CH = GDN_HEADS, GDN_DK, GDN_DV, GDN_CHUNK
    qkv = lax.conv_general_dilated(qkv, conv_w[:, None, :], window_strides=(1,),
                                   padding=[(GDN_CONV - 1, 0)],
                                   dimension_numbers=('NWC', 'WIO', 'NWC'), feature_group_count=c)
    qkv = jax.nn.silu(qkv)
    q, k, v = jnp.split(qkv, [H * DK, 2 * H * DK], axis=-1)
    q = _l2_norm(q.reshape(bs, s, H, DK)) * (DK ** -0.5)
    k = _l2_norm(k.reshape(bs, s, H, DK))
    v = v.reshape(bs, s, H, DV)
    beta = jax.nn.sigmoid(b)
    log_alpha = -jnp.exp(a_log) * jax.nn.softplus(a + dt_bias)
    nc = s // CH

    def chunks(t):
        return t.reshape(bs, nc, CH, H, -1).transpose(0, 3, 1, 2, 4)

    q, k, v = chunks(q), chunks(k), chunks(v)
    beta = chunks(beta[..., None])
    g = jnp.cumsum(chunks(log_alpha[..., None]), axis=3)
    gv = g[..., 0]
    causal = jnp.tril(jnp.ones((CH, CH), bool))
    strict = jnp.tril(jnp.ones((CH, CH), bool), -1)
    decay = jnp.exp(jnp.where(causal, gv[..., :, None] - gv[..., None, :], -jnp.inf))
    kb = k * beta
    a_mat = (jnp.where(strict, jnp.einsum('bhncd,bhnkd->bhnck', kb, k) * decay, 0.0)
             + jnp.eye(CH, dtype=F32))
    rhs = jnp.concatenate([v * beta, kb * jnp.exp(g)], axis=-1)
    sol = lax.linalg.triangular_solve(a_mat, rhs, left_side=True, lower=True, unit_diagonal=True)
    u_val, w = sol[..., :DV], sol[..., DV:]
    attn = jnp.einsum('bhncd,bhnkd->bhnck', q, k) * decay
    g_last = g[..., -1:, :]
    k_st = k * jnp.exp(g_last - g)
    q_st = q * jnp.exp(g)

    def step(state, xs):
        q_c, k_c, u_c, w_c, a_c, gl_c = xs
        v_new = u_c - jnp.einsum('bhcd,bhde->bhce', w_c, state)
        o = jnp.einsum('bhcd,bhde->bhce', q_c, state) + jnp.einsum('bhck,bhke->bhce', a_c, v_new)
        state = state * jnp.exp(gl_c) + jnp.einsum('bhcd,bhce->bhde', k_c, v_new)
        return state, o

    xs = tuple(jnp.moveaxis(t, 2, 0) for t in (q_st, k_st, u_val, w, attn, g_last))
    _, o = lax.scan(step, jnp.zeros((bs, H, DK, DV), F32), xs)
    o = o.transpose(1, 0, 3, 2, 4).reshape(bs, s, H, DV)
    o = o * lax.rsqrt(jnp.mean(o * o, axis=-1, keepdims=True) + RMS_EPS) * norm_g
    o = o * jax.nn.silu(z.reshape(bs, s, H, DV))
    return o.reshape(bs, s, H * DV)


NSA_KT = 256
NSA_NEG = MASKED
NSA_WTILES = NSA_WINDOW // LANES + 1


def _bucket_table():
    n = np.arange(LANES)
    max_exact = REL_BUCKETS // 2
    nf = np.maximum(n, 1).astype(np.float32)
    large = max_exact + (np.log(nf / np.float32(max_exact)) / np.float32(math.log(REL_MAX_DIST / max_exact))
                         * np.float32(REL_BUCKETS - max_exact)).astype(np.int32)
    tbl = np.where(n < max_exact, n, np.minimum(large, REL_BUCKETS - 1))
    assert tbl[-1] == REL_BUCKETS - 1 and REL_MAX_DIST <= LANES
    return tbl


def _compress_body(x_ref, pe_ref, w1_ref, w2_ref, o_ref):
    x = x_ref[0, 0, 0]
    w1 = w1_ref[0]
    half = NSA_CMP_STRIDE * NSA_DH
    nc = x.shape[0]
    a = jnp.dot(x, w1[:half], preferred_element_type=F32)
    b = jnp.dot(x, w1[half:], preferred_element_type=F32)
    pe_h = jnp.dot(pe_ref[0], w1, preferred_element_type=F32)
    hid = a + pltpu.roll(b, nc - 1, 0) + pe_h[0:1, :]
    o_ref[0, 0, 0] = jnp.dot(jax.nn.gelu(hid).astype(BF16), w2_ref[0], preferred_element_type=F32)


def nsa_compress(x, pe, w1, w2):
    _, bs, g, nc, width = x.shape
    return pl.pallas_call(
        _compress_body,
        grid=(2, bs, g),
        in_specs=[pl.BlockSpec((1, 1, 1, nc, width), lambda i, b, j: (i, b, j, 0, 0)),
                  pl.BlockSpec((1,) + pe.shape[1:], lambda i, b, j: (i, 0, 0)),
                  pl.BlockSpec((1,) + w1.shape[1:], lambda i, b, j: (i, 0, 0)),
                  pl.BlockSpec((1,) + w2.shape[1:], lambda i, b, j: (i, 0, 0))],
        out_specs=pl.BlockSpec((1, 1, 1, nc, NSA_DH), lambda i, b, j: (i, b, j, 0, 0)),
        out_shape=jax.ShapeDtypeStruct((2, bs, g, nc, NSA_DH), F32),
        compiler_params=_params("parallel", "parallel", "parallel"),
        name="nsa_compress",
    )(x, pe, w1, w2)


def _nsa_body(nfast, wstd, near_w, near_c, q_ref, gl_ref, qp_ref, kp_ref, cp_ref,
              ks_ref, vs_ref, kw_ref, vw_ref, kc_ref, vc_ref, agg_ref, td_ref, o_ref,
              m_ref, l_ref, acc_ref, sel_ref, *, n_sel, sel_k):
    QB, HG = NSA_QBLOCK, NSA_HPG
    g = pl.program_id(1)
    qi = pl.program_id(2)
    s0 = qi * QB
    nc = kc_ref.shape[2]
    n_ct = nc // LANES
    q = q_ref[0, 0, 0]
    qp = qp_ref[...]

    def lanes4(x):
        return jnp.concatenate([x] * HG, axis=1)

    def delta_t(kp):
        idx = jnp.clip(_dot_nt(kp, qp), 0.0, float(LANES - 1)).astype(jnp.int32)
        outs = []
        for hg in range(HG):
            tb = jnp.broadcast_to(td_ref[pl.ds(g * HG + hg, 1), :], idx.shape)
            outs.append(jnp.take_along_axis(tb, idx, axis=1))
        return jnp.concatenate(outs, axis=1)

    def maybe_delta(flag, kp):
        return lax.cond(flag != 0, lambda: delta_t(kp),
                        lambda: jnp.zeros((kp.shape[0], HG * QB), F32))

    gate = jax.nn.sigmoid(gl_ref[0, 0, 0])

    sc = _dot_nt(kc_ref[0, 0], q)
    sc = sc + jnp.concatenate(
        [maybe_delta(near_c[qi * n_ct + ct], cp_ref[ct * LANES:(ct + 1) * LANES, :])
         for ct in range(n_ct)], axis=0)
    c_id = lax.broadcasted_iota(jnp.int32, (nc, QB), 0)
    c_q = lax.broadcasted_iota(jnp.int32, (nc, QB), 1)
    ok_c = lanes4(jnp.where(c_id * NSA_CMP_STRIDE + (NSA_CMP_LEN - 1) <= s0 + c_q, 1.0, 0.0))
    sc = jnp.where(ok_c > 0.5, sc, NSA_NEG)
    e = jnp.exp(sc - jnp.max(sc, axis=0, keepdims=True)) * ok_c
    p = e * (1.0 / jnp.maximum(jnp.sum(e, axis=0, keepdims=True), 1e-30))
    out = gate[0:1, :] * jnp.dot(vc_ref[0, 0], p.astype(BF16), preferred_element_type=F32)
    ps = p[:, 0:QB]
    for hg in range(1, HG):
        ps = ps + p[:, hg * QB:(hg + 1) * QB]
    ps_hi, ps_lo = _split_bf16(ps)
    imp = (jnp.dot(agg_ref[...], ps_hi, preferred_element_type=F32)
           + jnp.dot(agg_ref[...], ps_lo, preferred_element_type=F32))
    blk = lax.broadcasted_iota(jnp.int32, (LANES, QB), 0)
    t_tok = s0 + lax.broadcasted_iota(jnp.int32, (LANES, QB), 1)
    tb_blk = t_tok // NSA_SEL_LEN
    forced = jnp.where(blk == 0, 1.0, 0.0) + jnp.where(blk == tb_blk, 1.0, 0.0) \
        + jnp.where(blk == tb_blk - 1, 1.0, 0.0)
    score = jnp.where(forced > 0.5, 1e9, jnp.where(blk * NSA_SEL_LEN <= t_tok, imp, -1e9))
    score = jnp.where(blk < n_sel, score, -jnp.inf)
    blkf = blk.astype(F32)
    sel = jnp.zeros((LANES, QB), F32)
    for _ in range(sel_k):
        mx = jnp.max(score, axis=0, keepdims=True)
        first = jnp.min(jnp.where(score == mx, blkf, float(LANES)), axis=0, keepdims=True)
        hit = blkf == first
        sel = jnp.where(hit, 1.0, sel)
        score = jnp.where(hit, -jnp.inf, score)
    sel_ref[...] = jnp.where(sel > 0.5, 0.0, NSA_NEG)

    def reset():
        m_ref[...] = jnp.full_like(m_ref, NSA_NEG)
        l_ref[...] = jnp.zeros_like(l_ref)
        acc_ref[...] = jnp.zeros_like(acc_ref)

    def tile_step(s, v_t, k):
        m_old = m_ref[k]
        m_new = jnp.maximum(m_old, jnp.max(s, axis=0, keepdims=True))
        alpha = jnp.exp(m_old - m_new)
        p = jnp.exp(s - m_new)
        l_ref[k] = l_ref[k] * alpha + jnp.sum(p, axis=0, keepdims=True)
        acc_ref[k] = acc_ref[k] * alpha + jnp.dot(v_t, p.astype(BF16), preferred_element_type=F32)
        m_ref[k] = m_new

    def finish():
        m = jnp.maximum(m_ref[0], m_ref[1])
        w0, w1 = jnp.exp(m_ref[0] - m), jnp.exp(m_ref[1] - m)
        return (acc_ref[0] * w0 + acc_ref[1] * w1) * (1.0 / (l_ref[0] * w0 + l_ref[1] * w1))

    per_tile = NSA_KT // NSA_SEL_LEN
    key_r = lax.broadcasted_iota(jnp.int32, (NSA_KT, QB), 0)
    key_q = lax.broadcasted_iota(jnp.int32, (NSA_KT, QB), 1)

    def sel_scores(kt, with_delta, diag):
        k0 = pl.multiple_of(kt * NSA_KT, NSA_KT)
        s = _dot_nt(ks_ref[0, 0, pl.ds(k0, NSA_KT), :], q)
        if with_delta:
            s = s + delta_t(kp_ref[pl.ds(k0, NSA_KT), :])
        mask = jnp.concatenate(
            [jnp.broadcast_to(sel_ref[pl.ds(kt * per_tile + i, 1), :], (NSA_SEL_LEN, QB))
             for i in range(per_tile)], axis=0)
        if diag:
            mask = jnp.where(k0 + key_r <= s0 + key_q, mask, NSA_NEG)
        return s + lanes4(mask), vs_ref[0, 0, :, pl.ds(k0, NSA_KT)]

    reset()
    n_past = (s0 + QB - 1) // NSA_KT
    n_pairs = nfast[qi] // 2

    def fast_body(i, carry):
        a = sel_scores(2 * i, False, False)
        b = sel_scores(2 * i + 1, False, False)
        tile_step(*a, 0)
        tile_step(*b, 1)
        return carry

    def slow_body(kt, carry):
        tile_step(*sel_scores(kt, True, False), 1)
        return carry

    lax.fori_loop(0, n_pairs, fast_body, 0)
    lax.fori_loop(2 * n_pairs, n_past, slow_body, 0)
    tile_step(*sel_scores(n_past, True, True), 0)
    out = out + gate[1:2, :] * finish()

    reset()
    w_r = lax.broadcasted_iota(jnp.int32, (LANES, QB), 0)
    w_q = lax.broadcasted_iota(jnp.int32, (LANES, QB), 1)
    in_window = lanes4(jnp.where(w_r > w_q, 0.0, NSA_NEG))
    causal_w = lanes4(jnp.where(w_r <= w_q, 0.0, NSA_NEG))

    def win_scores(st, n):
        st = pl.multiple_of(st, LANES)
        return _dot_nt(kw_ref[0, 0, pl.ds(st, n), :], q), vw_ref[0, 0, :, pl.ds(st, n)]

    @pl.when(wstd[qi] != 0)
    def _():
        half = NSA_WINDOW // 2
        zeros = jnp.zeros((LANES, HG * QB), F32)
        sa, va = win_scores(s0 - NSA_WINDOW, half)
        sb, vb = win_scores(s0 - half, half)
        sd, vd = win_scores(s0, LANES)
        tile_step(sa + jnp.concatenate([in_window, zeros], axis=0), va, 0)
        near = delta_t(kp_ref[pl.ds(pl.multiple_of(s0 - LANES, LANES), LANES), :])
        tile_step(sb + jnp.concatenate([zeros, near], axis=0), vb, 1)
        tile_step(sd + delta_t(kp_ref[pl.ds(pl.multiple_of(s0, LANES), LANES), :]) + causal_w, vd, 0)

    for j in range(NSA_WTILES):
        start = s0 - NSA_WINDOW + j * LANES

        @pl.when((wstd[qi] == 0) & (start >= 0))
        def _():
            s, v_t = win_scores(start, LANES)
            s = s + maybe_delta(near_w[qi * NSA_WTILES + j],
                                kp_ref[pl.ds(pl.multiple_of(start, LANES), LANES), :])
            if j == 0:
                s = s + in_window
            elif j == NSA_WTILES - 1:
                s = s + causal_w
            tile_step(s, v_t, j % 2)

    o_ref[0, 0, 0] = out + gate[2:3, :] * finish()


def _pos_digits(p, key_side):
    d0, d1, d2 = (p & 127).astype(F32), ((p >> 7) & 127).astype(F32), (p >> 14).astype(F32)
    one = jnp.ones_like(d0)
    if key_side:
        cols = [one, one, one, -d2, -d1, -d0]
    else:
        cols = [16384.0 * d2, 128.0 * d1, d0, 16384.0 * one, 128.0 * one, one]
    out = jnp.stack(cols, axis=-1)
    return jnp.pad(out, ((0, 0), (0, LANES - out.shape[-1]))).astype(BF16)


def nsa_mixer(q, k_c, v_c, k_s, v_s, k_w, v_w, gate_logits, positions,
              ck_pe, ck_w1, ck_w2, cv_pe, cv_w1, cv_w2, rel_bias):
    q, k_c, v_c, k_s, v_s, k_w, v_w = lax.optimization_barrier((q, k_c, v_c, k_s, v_s, k_w, v_w))
    bs, s, _ = q.shape
    G, HG, DH, QB = NSA_KV_GROUPS, NSA_HPG, NSA_DH, NSA_QBLOCK
    nqt = s // QB
    nc = s // NSA_CMP_STRIDE
    n_cmp = (s - NSA_CMP_LEN) // NSA_CMP_STRIDE + 1
    n_sel = s // NSA_SEL_LEN
    sel_k = min(NSA_SEL_TOPK, n_sel)
    n_kt = s // NSA_KT
    n_ct = nc // LANES
    assert s % (LANES * NSA_CMP_STRIDE) == 0 and n_sel <= LANES and NSA_KT == 2 * QB

    def by_group(t):
        return t.reshape(bs, s, G, DH).transpose(0, 2, 1, 3)

    def by_group_t(t):
        return t.reshape(bs, s, G, DH).transpose(0, 2, 3, 1).astype(BF16)

    ones2 = jnp.zeros((LANES - DH,), F32).at[:2].set(1.0)

    def keys_aug(t):
        return jnp.concatenate([t, jnp.broadcast_to(ones2, t.shape[:-1] + (LANES - DH,))],
                               axis=-1).astype(BF16)

    chunks = jnp.stack([by_group(k_c), by_group(v_c)]).reshape(2, bs, G, nc, NSA_CMP_STRIDE * DH)
    pe = jnp.stack([ck_pe, cv_pe]).reshape(2, 1, NSA_CMP_LEN * DH)
    cmp = nsa_compress(chunks.astype(BF16), jnp.broadcast_to(pe, (2, 8, NSA_CMP_LEN * DH)).astype(BF16),
                       jnp.stack([ck_w1, cv_w1]).astype(BF16), jnp.stack([ck_w2, cv_w2]).astype(BF16))
    kc, vc_t = keys_aug(cmp[0]), cmp[1].transpose(0, 1, 3, 2).astype(BF16)

    far = rel_bias[REL_BUCKETS - 1].astype(F32)
    far_hi = far.astype(BF16).astype(F32)
    extra = jnp.zeros((NSA_HEADS, LANES - DH), F32).at[:, 0].set(far_hi).at[:, 1].set(far - far_hi)
    qh = (q * (DH ** -0.5)).reshape(bs, nqt, QB, G, HG, DH).transpose(0, 3, 1, 4, 2, 5)
    ex = jnp.broadcast_to(extra.reshape(1, G, 1, HG, 1, LANES - DH), (bs, G, nqt, HG, QB, LANES - DH))
    qs = jnp.concatenate([qh, ex], axis=-1).astype(BF16).reshape(bs, G, nqt, HG * QB, LANES)
    gl = gate_logits.reshape(bs, nqt, QB, G, HG, 3).transpose(0, 3, 1, 5, 4, 2)
    gl = jnp.pad(gl.reshape(bs, G, nqt, 3, HG * QB), ((0, 0),) * 3 + ((0, 5), (0, 0)))

    td = (rel_bias[_bucket_table()] - rel_bias[REL_BUCKETS - 1][None, :]).T.astype(F32)
    cmp_end = jnp.minimum(jnp.arange(nc) * NSA_CMP_STRIDE + NSA_CMP_LEN - 1, s - 1)
    cpos = positions[cmp_end]
    qmin = positions.reshape(nqt, QB).min(axis=1)
    kmax = positions.reshape(nqt, QB).max(axis=1)
    near_s = (qmin[:, None] - positions.reshape(n_kt, NSA_KT).max(axis=1)[None, :]) < REL_MAX_DIST
    n_past = (jnp.arange(nqt) * QB + QB - 1) // NSA_KT
    n_fast = jnp.minimum(jnp.argmax(jnp.concatenate([near_s, jnp.ones((nqt, 1), bool)], axis=1), axis=1),
                         n_past)
    wt = jnp.clip(jnp.arange(nqt)[:, None] - (NSA_WTILES - 1) + jnp.arange(NSA_WTILES)[None, :], 0, nqt - 1)
    near_w = (qmin[:, None] - kmax[wt]) < REL_MAX_DIST
    near_c = (qmin[:, None] - cpos.reshape(n_ct, LANES).max(axis=1)[None, :]) < REL_MAX_DIST
    w_std = ((jnp.arange(nqt) >= NSA_WTILES - 1)
             & jnp.all(near_w == (jnp.arange(NSA_WTILES) >= NSA_WTILES - 2)[None, :], axis=1))

    cmp_start = np.arange(nc) * NSA_CMP_STRIDE
    sel_start = np.arange(LANES) * NSA_SEL_LEN
    agg_t = ((cmp_start[None, :] <= sel_start[:, None] + NSA_SEL_LEN - 1)
             & (cmp_start[None, :] + NSA_CMP_LEN - 1 >= sel_start[:, None])
             & (np.arange(nc)[None, :] < n_cmp) & (np.arange(LANES)[:, None] < n_sel))

    cols = HG * QB
    full = lambda shape: pl.BlockSpec(shape, lambda b, g, i, *_: (0,) * len(shape))
    per_bg = lambda n, w: pl.BlockSpec((1, 1, n, w), lambda b, g, i, *_: (b, g, 0, 0))
    grid_spec = pltpu.PrefetchScalarGridSpec(
        num_scalar_prefetch=4,
        grid=(bs, G, nqt),
        in_specs=[pl.BlockSpec((1, 1, 1, cols, LANES), lambda b, g, i, *_: (b, g, i, 0, 0)),
                  pl.BlockSpec((1, 1, 1, 8, cols), lambda b, g, i, *_: (b, g, i, 0, 0)),
                  pl.BlockSpec((QB, LANES), lambda b, g, i, *_: (i, 0)),
                  full((s, LANES)), full((nc, LANES)),
                  per_bg(s, LANES), per_bg(DH, s), per_bg(s, LANES), per_bg(DH, s),
                  per_bg(nc, LANES), per_bg(DH, nc),
                  full((LANES, nc)), full((NSA_HEADS, LANES))],
        out_specs=pl.BlockSpec((1, 1, 1, DH, cols), lambda b, g, i, *_: (b, g, i, 0, 0)),
        scratch_shapes=[pltpu.VMEM((2, 1, cols), F32), pltpu.VMEM((2, 1, cols), F32),
                        pltpu.VMEM((2, DH, cols), F32), pltpu.VMEM((LANES, QB), F32)],
    )
    out = pl.pallas_call(
        functools.partial(_nsa_body, n_sel=n_sel, sel_k=sel_k),
        grid_spec=grid_spec,
        out_shape=jax.ShapeDtypeStruct((bs, G, nqt, DH, cols), F32),
        compiler_params=_params("parallel", "parallel", "arbitrary"),
        name="nsa_attention",
    )(n_fast.astype(jnp.int32), w_std.astype(jnp.int32), near_w.reshape(-1).astype(jnp.int32),
      near_c.reshape(-1).astype(jnp.int32),
      qs, gl, _pos_digits(positions, False), _pos_digits(positions, True), _pos_digits(cpos, True),
      keys_aug(by_group(k_s)), by_group_t(v_s), keys_aug(by_group(k_w)), by_group_t(v_w),
      kc, vc_t, jnp.asarray(agg_t, BF16), td)
    out = out.reshape(bs, G, nqt, DH, HG, QB).transpose(0, 2, 5, 1, 4, 3)
    return out.reshape(bs, s, G * HG * DH)


def _nsa_body_rowmajor(near_s, near_w, near_c, q_ref, gl_ref, posq_ref, posk_ref, cpos_ref,
              ks_ref, vs_ref, kw_ref, vw_ref, kc_ref, vc_ref, agg_ref, td_ref, o_ref,
              s_ref, m_ref, l_ref, acc_ref, *, n_sel, sel_k):
    QB, HG = NSA_QBLOCK, NSA_HPG
    g = pl.program_id(1)
    qi = pl.program_id(2)
    s0 = qi * QB
    nc = kc_ref.shape[2]
    n_ct = nc // LANES
    n_kt = ks_ref.shape[2] // NSA_KT
    q = q_ref[0, 0, 0]
    tp = posq_ref[...]
    row = lax.broadcasted_iota(jnp.int32, (QB, LANES), 0)
    lane = lax.broadcasted_iota(jnp.int32, (QB, LANES), 1)
    t_tok = s0 + row

    def add_delta(col0, pk_row):
        idx = jnp.clip(tp - pk_row, 0, LANES - 1)
        for hg in range(HG):
            tb = jnp.broadcast_to(td_ref[pl.ds(g * HG + hg, 1), :], (QB, LANES))
            s_ref[hg * QB:(hg + 1) * QB, col0:col0 + LANES] += jnp.take_along_axis(tb, idx, axis=1)

    def heads(x):
        return jnp.concatenate([x] * HG, axis=0)

    gate = jax.nn.sigmoid(gl_ref[0, 0, 0])

    s_ref[:, :nc] = _dot_nt(q, kc_ref[0, 0])
    for ct in range(n_ct):
        @pl.when(near_c[qi * n_ct + ct] != 0)
        def _():
            add_delta(ct * LANES, cpos_ref[ct:ct + 1, :])
    c_id = lax.broadcasted_iota(jnp.int32, (QB, nc), 1)
    c_row = lax.broadcasted_iota(jnp.int32, (QB, nc), 0)
    ok_c = heads(jnp.where(c_id * NSA_CMP_STRIDE + (NSA_CMP_LEN - 1) <= s0 + c_row, 1.0, 0.0))
    sc = jnp.where(ok_c > 0.5, s_ref[:, :nc], NSA_NEG)
    e = jnp.exp(sc - jnp.max(sc, axis=-1, keepdims=True)) * ok_c
    p = e * (1.0 / jnp.maximum(jnp.sum(e, axis=-1, keepdims=True), 1e-30))
    o_c = jnp.dot(p.astype(BF16), vc_ref[0, 0], preferred_element_type=F32)
    o_ref[0, 0, 0] = gate[:, 0:1] * o_c
    ps = p[0:QB]
    for hg in range(1, HG):
        ps = ps + p[hg * QB:(hg + 1) * QB]
    ps_hi = ps.astype(BF16)
    ps_lo = (ps - ps_hi.astype(F32)).astype(BF16)
    imp = (jnp.dot(ps_hi, agg_ref[...], preferred_element_type=F32)
           + jnp.dot(ps_lo, agg_ref[...], preferred_element_type=F32))
    tb_blk = t_tok // NSA_SEL_LEN
    forced = jnp.where(lane == 0, 1.0, 0.0) + jnp.where(lane == tb_blk, 1.0, 0.0) \
        + jnp.where(lane == tb_blk - 1, 1.0, 0.0)
    score = jnp.where(forced > 0.5, 1e9, jnp.where(lane * NSA_SEL_LEN <= t_tok, imp, -1e9))
    score = jnp.where(lane < n_sel, score, -jnp.inf)
    lanef = lane.astype(F32)
    sel = jnp.zeros((QB, LANES), F32)
    for _ in range(sel_k):
        mx = jnp.max(score, axis=-1, keepdims=True)
        first = jnp.min(jnp.where(score == mx, lanef, float(LANES)), axis=-1, keepdims=True)
        hit = lanef == first
        sel = jnp.where(hit, 1.0, sel)
        score = jnp.where(hit, -jnp.inf, score)
    sel_b = sel.astype(BF16)

    def reset():
        m_ref[...] = jnp.full_like(m_ref, NSA_NEG)
        l_ref[...] = jnp.zeros_like(l_ref)
        acc_ref[...] = jnp.zeros_like(acc_ref)

    def tile_step(width, mask_add, v):
        s = s_ref[:, :width]
        if mask_add is not None:
            s = s + heads(mask_add)
        m_old = m_ref[...]
        m_new = jnp.maximum(m_old, jnp.max(s, axis=-1, keepdims=True))
        alpha = jnp.exp(m_old - m_new)
        p = jnp.exp(s - m_new)
        psum = p[:, :LANES]
        for c in range(1, width // LANES):
            psum = psum + p[:, c * LANES:(c + 1) * LANES]
        l_ref[...] = l_ref[...] * alpha + psum
        acc_ref[...] = acc_ref[...] * alpha + jnp.dot(p.astype(BF16), v, preferred_element_type=F32)
        m_ref[...] = m_new

    def finish():
        return acc_ref[...] * (1.0 / jnp.sum(l_ref[...], axis=-1, keepdims=True))

    blk_i = lax.broadcasted_iota(jnp.int32, (LANES, NSA_KT), 0)
    key_i = lax.broadcasted_iota(jnp.int32, (LANES, NSA_KT), 1)
    key_q = lax.broadcasted_iota(jnp.int32, (QB, NSA_KT), 1)
    row_q = lax.broadcasted_iota(jnp.int32, (QB, NSA_KT), 0)
    per_tile = NSA_KT // NSA_SEL_LEN

    def sel_tile(kt, diag):
        k0 = pl.multiple_of(kt * NSA_KT, NSA_KT)
        s_ref[:, :NSA_KT] = _dot_nt(q, ks_ref[0, 0, pl.ds(k0, NSA_KT), :])

        @pl.when(near_s[qi * n_kt + kt] != 0)
        def _():
            for c in range(NSA_KT // LANES):
                add_delta(c * LANES, posk_ref[pl.ds(kt * (NSA_KT // LANES) + c, 1), :])

        expand = jnp.where(blk_i == kt * per_tile + key_i // NSA_SEL_LEN, 1.0, 0.0).astype(BF16)
        picked = jnp.dot(sel_b, expand, preferred_element_type=F32)
        if diag:
            picked = jnp.where(k0 + key_q <= s0 + row_q, picked, 0.0)
        tile_step(NSA_KT, jnp.where(picked > 0.5, 0.0, NSA_NEG), vs_ref[0, 0, pl.ds(k0, NSA_KT), :])

    reset()
    n_past = (s0 + QB - 1) // NSA_KT

    def past_body(kt, carry):
        sel_tile(kt, False)
        return carry

    lax.fori_loop(0, n_past, past_body, 0)
    sel_tile(n_past, True)
    o_ref[0, 0, 0] += gate[:, 1:2] * finish()

    reset()
    for j in range(NSA_WTILES):
        start = s0 - NSA_WINDOW + j * LANES

        @pl.when(start >= 0)
        def _():
            st = pl.multiple_of(start, LANES)
            s_ref[:, :LANES] = _dot_nt(q, kw_ref[0, 0, pl.ds(st, LANES), :])

            @pl.when(near_w[qi * NSA_WTILES + j] != 0)
            def _():
                add_delta(0, posk_ref[pl.ds(qi - (NSA_WTILES - 1) + j, 1), :])

            if j == 0:
                mask_add = jnp.where(lane > row, 0.0, NSA_NEG)
            elif j == NSA_WTILES - 1:
                mask_add = jnp.where(lane <= row, 0.0, NSA_NEG)
            else:
                mask_add = None
            tile_step(LANES, mask_add, vw_ref[0, 0, pl.ds(st, LANES), :])

    o_ref[0, 0, 0] += gate[:, 2:3] * finish()


def _nsa_mixer_rowmajor(q, k_c, v_c, k_s, v_s, k_w, v_w, gate_logits, positions,
                        ck_pe, ck_w1, ck_w2, cv_pe, cv_w1, cv_w2, rel_bias):
    bs, s, _ = q.shape
    G, HG, DH, QB = NSA_KV_GROUPS, NSA_HPG, NSA_DH, NSA_QBLOCK
    nqt = s // QB
    nc = s // NSA_CMP_STRIDE
    n_cmp = (s - NSA_CMP_LEN) // NSA_CMP_STRIDE + 1
    n_sel = s // NSA_SEL_LEN
    sel_k = min(NSA_SEL_TOPK, n_sel)
    n_kt = s // NSA_KT
    n_ct = nc // LANES
    assert s % (LANES * NSA_CMP_STRIDE) == 0 and n_sel <= LANES and NSA_KT == 2 * QB

    def by_group(t):
        return t.reshape(bs, s, G, DH).transpose(0, 2, 1, 3)

    ones2 = jnp.zeros((LANES - DH,), F32).at[:2].set(1.0)

    def keys_aug(t):
        return jnp.concatenate([t, jnp.broadcast_to(ones2, t.shape[:-1] + (LANES - DH,))],
                               axis=-1).astype(BF16)

    chunks = jnp.stack([by_group(k_c), by_group(v_c)]).reshape(2, bs, G, nc, NSA_CMP_STRIDE * DH)
    pe = jnp.stack([ck_pe, cv_pe]).reshape(2, 1, NSA_CMP_LEN * DH)
    cmp = nsa_compress(chunks.astype(BF16), jnp.broadcast_to(pe, (2, 8, NSA_CMP_LEN * DH)).astype(BF16),
                       jnp.stack([ck_w1, cv_w1]).astype(BF16), jnp.stack([ck_w2, cv_w2]).astype(BF16))
    kc, vc = keys_aug(cmp[0]), cmp[1].astype(BF16)

    far = rel_bias[REL_BUCKETS - 1].astype(F32)
    far_hi = far.astype(BF16).astype(F32)
    extra = jnp.zeros((NSA_HEADS, LANES - DH), F32).at[:, 0].set(far_hi).at[:, 1].set(far - far_hi)
    qh = (q * (DH ** -0.5)).reshape(bs, nqt, QB, G, HG, DH).transpose(0, 3, 1, 4, 2, 5)
    ex = jnp.broadcast_to(extra.reshape(1, G, 1, HG, 1, LANES - DH), (bs, G, nqt, HG, QB, LANES - DH))
    qs = jnp.concatenate([qh, ex], axis=-1).astype(BF16).reshape(bs, G, nqt, HG * QB, LANES)
    gl = gate_logits.reshape(bs, nqt, QB, G, HG, 3).transpose(0, 3, 1, 4, 2, 5)
    gl = jnp.pad(gl, ((0, 0),) * 5 + ((0, 5),)).reshape(bs, G, nqt, HG * QB, 8)

    td = (rel_bias[_bucket_table()] - rel_bias[REL_BUCKETS - 1][None, :]).T.astype(F32)
    cmp_end = jnp.minimum(jnp.arange(nc) * NSA_CMP_STRIDE + NSA_CMP_LEN - 1, s - 1)
    cpos = positions[cmp_end]
    qmin = positions.reshape(nqt, QB).min(axis=1)
    kmax = positions.reshape(nqt, QB).max(axis=1)
    near_s = (qmin[:, None] - positions.reshape(n_kt, NSA_KT).max(axis=1)[None, :]) < REL_MAX_DIST
    wt = jnp.clip(jnp.arange(nqt)[:, None] - (NSA_WTILES - 1) + jnp.arange(NSA_WTILES)[None, :], 0, nqt - 1)
    near_w = (qmin[:, None] - kmax[wt]) < REL_MAX_DIST
    near_c = (qmin[:, None] - cpos.reshape(n_ct, LANES).max(axis=1)[None, :]) < REL_MAX_DIST

    cmp_start = np.arange(nc) * NSA_CMP_STRIDE
    sel_start = np.arange(LANES) * NSA_SEL_LEN
    agg = ((cmp_start[:, None] <= sel_start[None, :] + NSA_SEL_LEN - 1)
           & (cmp_start[:, None] + NSA_CMP_LEN - 1 >= sel_start[None, :])
           & (np.arange(nc)[:, None] < n_cmp) & (np.arange(LANES)[None, :] < n_sel))

    rows = HG * QB
    full = lambda shape: pl.BlockSpec(shape, lambda b, g, i, *_: (0,) * len(shape))
    per_bg = lambda n, w: pl.BlockSpec((1, 1, n, w), lambda b, g, i, *_: (b, g, 0, 0))
    grid_spec = pltpu.PrefetchScalarGridSpec(
        num_scalar_prefetch=3,
        grid=(bs, G, nqt),
        in_specs=[pl.BlockSpec((1, 1, 1, rows, LANES), lambda b, g, i, *_: (b, g, i, 0, 0)),
                  pl.BlockSpec((1, 1, 1, rows, 8), lambda b, g, i, *_: (b, g, i, 0, 0)),
                  pl.BlockSpec((QB, 1), lambda b, g, i, *_: (i, 0)),
                  full((nqt, LANES)), full((n_ct, LANES)),
                  per_bg(s, LANES), per_bg(s, DH), per_bg(s, LANES), per_bg(s, DH),
                  per_bg(nc, LANES), per_bg(nc, DH),
                  full((nc, LANES)), full((NSA_HEADS, LANES))],
        out_specs=pl.BlockSpec((1, 1, 1, rows, DH), lambda b, g, i, *_: (b, g, i, 0, 0)),
        scratch_shapes=[pltpu.VMEM((rows, max(nc, NSA_KT)), F32), pltpu.VMEM((rows, 1), F32),
                        pltpu.VMEM((rows, LANES), F32), pltpu.VMEM((rows, DH), F32)],
    )
    out = pl.pallas_call(
        functools.partial(_nsa_body, n_sel=n_sel, sel_k=sel_k),
        grid_spec=grid_spec,
        out_shape=jax.ShapeDtypeStruct((bs, G, nqt, rows, DH), F32),
        compiler_params=_params("parallel", "parallel", "arbitrary"),
        name="nsa_attention",
    )(near_s.reshape(-1).astype(jnp.int32), near_w.reshape(-1).astype(jnp.int32),
      near_c.reshape(-1).astype(jnp.int32),
      qs, gl, positions.reshape(s, 1), positions.reshape(nqt, LANES), cpos.reshape(n_ct, LANES),
      keys_aug(by_group(k_s)), by_group(v_s).astype(BF16), keys_aug(by_group(k_w)),
      by_group(v_w).astype(BF16), kc, vc, jnp.asarray(agg, BF16), td)
    out = out.reshape(bs, G, nqt, HG, QB, DH).transpose(0, 2, 4, 1, 3, 5)
    return out.reshape(bs, s, G * HG * DH)


def _nsa_mixer_jax(q, k_c, v_c, k_s, v_s, k_w, v_w, gate_logits, positions,
                   ck_pe, ck_w1, ck_w2, cv_pe, cv_w1, cv_w2, rel_bias):
    bs, s, _ = q.shape
    G, HG, DH, QB, W = NSA_KV_GROUPS, NSA_HPG, NSA_DH, NSA_QBLOCK, NSA_WINDOW
    q = q.reshape(bs, s, G, HG, DH) * (DH ** -0.5)

    def kv(t):
        return t.reshape(bs, s, G, DH)

    k_c, v_c, k_s, v_s, k_w, v_w = (kv(t) for t in (k_c, v_c, k_s, v_s, k_w, v_w))
    gates = jax.nn.sigmoid(gate_logits).reshape(bs, s, G, HG, 3)
    n_cmp = (s - NSA_CMP_LEN) // NSA_CMP_STRIDE + 1
    cmp_start = jnp.arange(n_cmp) * NSA_CMP_STRIDE
    cmp_end = cmp_start + NSA_CMP_LEN - 1
    cmp_tok = cmp_start[:, None] + jnp.arange(NSA_CMP_LEN)[None, :]

    def compress(t, pe, w1, w2):
        blk = t[:, cmp_tok] + pe[None, None, :, None, :]
        blk = blk.transpose(0, 1, 3, 2, 4).reshape(bs, n_cmp, G, NSA_CMP_LEN * DH)
        return jax.nn.gelu(blk @ w1) @ w2

    kc = compress(k_c, ck_pe, ck_w1, ck_w2)
    vc = compress(v_c, cv_pe, cv_w1, cv_w2)
    cmp_pos = positions[cmp_end]
    n_sel = s // NSA_SEL_LEN
    sel_k = min(NSA_SEL_TOPK, n_sel)
    sel_start = jnp.arange(n_sel) * NSA_SEL_LEN
    agg = ((cmp_start[:, None] <= sel_start[None, :] + NSA_SEL_LEN - 1)
           & (cmp_end[:, None] >= sel_start[None, :])).astype(F32)
    ks_blk = k_s.reshape(bs, n_sel, NSA_SEL_LEN, G, DH).transpose(0, 3, 1, 2, 4)
    vs_blk = v_s.reshape(bs, n_sel, NSA_SEL_LEN, G, DH).transpose(0, 3, 1, 2, 4)
    pos_blk = positions.reshape(n_sel, NSA_SEL_LEN)
    kw_pad = jnp.pad(k_w, ((0, 0), (W, 0), (0, 0), (0, 0)))
    vw_pad = jnp.pad(v_w, ((0, 0), (W, 0), (0, 0), (0, 0)))
    pos_pad = jnp.pad(positions, (W, 0))
    tbl = rel_bias.reshape(REL_BUCKETS, G, HG)
    b_ix = jnp.arange(bs)[:, None, None, None]
    g_ix = jnp.arange(G)[None, :, None, None]
    sel_offs = jnp.arange(NSA_SEL_LEN)
    win_offs = jnp.arange(QB + W)
    blk_j = jnp.arange(n_sel)

    def one_block(qi):
        s0 = qi * QB
        qb = lax.dynamic_slice_in_dim(q, s0, QB, axis=1)
        t = s0 + jnp.arange(QB)
        tp = lax.dynamic_slice_in_dim(positions, s0, QB)
        bias_c = tbl[_rel_bucket(tp[:, None] - cmp_pos[None, :])].transpose(2, 3, 0, 1)
        lg_c = jnp.einsum('bqghd,bcgd->bghqc', qb, kc) + bias_c
        p_c = _masked_softmax(lg_c, cmp_end[None, :] <= t[:, None])
        o_c = jnp.einsum('bghqc,bcgd->bqghd', p_c, vc)
        imp = jnp.einsum('bghqc,cj->bgqj', p_c, agg)
        tb = t // NSA_SEL_LEN
        forced = ((blk_j[None, :] == 0) | (blk_j[None, :] == tb[:, None])
                  | (blk_j[None, :] == tb[:, None] - 1))
        eligible = sel_start[None, :] <= t[:, None]
        score = jnp.where(forced, 1e9, jnp.where(eligible, imp, -1e9))
        _, sel = lax.top_k(score, sel_k)
        k_sel = ks_blk[b_ix, g_ix, sel]
        v_sel = vs_blk[b_ix, g_ix, sel]
        tok_s = sel[..., None] * NSA_SEL_LEN + sel_offs
        bias_s = tbl[_rel_bucket(tp[:, None, None] - pos_blk[sel]), g_ix[..., None]]
        lg_s = jnp.einsum('bqghd,bgqnkd->bghqnk', qb, k_sel) + bias_s.transpose(0, 1, 5, 2, 3, 4)
        valid_s = (tok_s <= t[:, None, None])[:, :, None].reshape(bs, G, 1, QB, -1)
        p_s = _masked_softmax(lg_s.reshape(bs, G, HG, QB, -1), valid_s)
        p_s = p_s.reshape(bs, G, HG, QB, sel_k, NSA_SEL_LEN)
        o_s = jnp.einsum('bghqnk,bgqnkd->bqghd', p_s, v_sel)
        kwb = lax.dynamic_slice_in_dim(kw_pad, s0, QB + W, axis=1)
        vwb = lax.dynamic_slice_in_dim(vw_pad, s0, QB + W, axis=1)
        tok_w = s0 - W + win_offs
        pos_w = lax.dynamic_slice_in_dim(pos_pad, s0, QB + W)
        d_tok = t[:, None] - tok_w[None, :]
        valid_w = (d_tok >= 0) & (d_tok < W) & (tok_w[None, :] >= 0)
        bias_w = tbl[_rel_bucket(tp[:, None] - pos_w[None, :])].transpose(2, 3, 0, 1)
        lg_w = jnp.einsum('bqghd,bkgd->bghqk', qb, kwb) + bias_w
        p_w = _masked_softmax(lg_w, valid_w)
        o_w = jnp.einsum('bghqk,bkgd->bqghd', p_w, vwb)
        gb = lax.dynamic_slice_in_dim(gates, s0, QB, axis=1)
        return gb[..., 0:1] * o_c + gb[..., 1:2] * o_s + gb[..., 2:3] * o_w

    out = lax.map(one_block, jnp.arange(s // QB))
    return out.transpose(1, 0, 2, 3, 4, 5).reshape(bs, s, G * HG * DH)


SGU_TC = 512


def _sgu_body(uv_ref, lg_ref, lb_ref, w_ref, b_ref, o_ref):
    uv = jax.nn.gelu(uv_ref[...])
    u, v = uv[:, :SGU_WIDTH], uv[:, SGU_WIDTH:]
    mu = jnp.mean(v, axis=-1, keepdims=True)
    vc = v - mu
    var = jnp.mean(vc * vc, axis=-1, keepdims=True)
    vn = (vc * lax.rsqrt(var + RMS_EPS) * lg_ref[...] + lb_ref[...]).astype(BF16)
    gw = SGU_WIDTH // SGU_GROUPS
    for c in range(SGU_TC // SGU_CHUNK):
        rows = slice(c * SGU_CHUNK, (c + 1) * SGU_CHUNK)
        for g in range(SGU_GROUPS):
            cols = slice(g * gw, (g + 1) * gw)
            mix = jnp.dot(w_ref[g], vn[rows, cols], preferred_element_type=F32) + b_ref[:, g:g + 1]
            o_ref[rows, cols] = u[rows, cols] * mix


def sgu_mixer(proj, ln_g, ln_b, w_s, b_s):
    t = proj.shape[0]
    assert t % SGU_TC == 0 and OFF_SGU % (2 * SGU_WIDTH) == 0
    fixed = lambda i: (0, 0)
    return pl.pallas_call(
        _sgu_body,
        grid=(t // SGU_TC,),
        in_specs=[pl.BlockSpec((SGU_TC, 2 * SGU_WIDTH), lambda i: (i, OFF_SGU // (2 * SGU_WIDTH))),
                  pl.BlockSpec((1, SGU_WIDTH), fixed), pl.BlockSpec((1, SGU_WIDTH), fixed),
                  pl.BlockSpec((SGU_GROUPS, SGU_CHUNK, SGU_CHUNK), lambda i: (0, 0, 0)),
                  pl.BlockSpec((SGU_CHUNK, SGU_GROUPS), fixed)],
        out_specs=pl.BlockSpec((SGU_TC, SGU_WIDTH), lambda i: (i, 0)),
        out_shape=jax.ShapeDtypeStruct((t, SGU_WIDTH), F32),
        compiler_params=_params("parallel"),
        name="sgu_gate",
    )(proj, ln_g.reshape(1, SGU_WIDTH), ln_b.reshape(1, SGU_WIDTH), jnp.tril(w_s).astype(BF16), b_s.T)


def _sgu_mixer_jax(uv, ln_g, ln_b, w_s, b_s):
    bs, s, _ = uv.shape
    uv = jax.nn.gelu(uv)
    u, v = uv[..., :SGU_WIDTH], uv[..., SGU_WIDTH:]
    mu = jnp.mean(v, axis=-1, keepdims=True)
    var = jnp.mean(jnp.square(v - mu), axis=-1, keepdims=True)
    v = (v - mu) * lax.rsqrt(var + RMS_EPS) * ln_g + ln_b
    nc = s // SGU_CHUNK
    v = v.reshape(bs, nc, SGU_CHUNK, SGU_GROUPS, SGU_WIDTH // SGU_GROUPS)
    mix = (jnp.einsum('gij,bcjgd->bcigd', jnp.tril(w_s), v) + b_s.T[None, None, :, :, None])
    return u * mix.reshape(bs, s, SGU_WIDTH)


def _permute_w_in(w):
    sizes = ([SSM_WIDTH, GDN_QKV, GDN_HEADS, GDN_HEADS, GDN_HEADS * GDN_DV, NSA_Q]
             + [NSA_KV] * 6 + [3 * NSA_HEADS, 2 * SGU_WIDTH, N_BRANCH * D_MODEL])
    cuts = [int(c) for c in np.cumsum(sizes)[:-1]]
    (w_ssm, w_qkv, w_a, w_b, w_z, w_nq, w_kc, w_vc, w_ks, w_vs, w_kw, w_vw,
     w_ng, w_sgu, w_gate) = jnp.split(w, cuts, axis=-1)
    pad = jnp.zeros((w.shape[0], PROJ_COLS - OFF_SMALL - SMALL_USED), w.dtype)
    return jnp.concatenate([w_qkv, w_ssm, w_z, w_nq, w_sgu, w_gate, w_kc, w_vc, w_ks, w_vs,
                            w_kw, w_vw, w_a, w_b, w_ng, pad], axis=-1)


def kernel(x, positions, norm_mix, norm_ffn, norm_final, w_in, ssm_a_re, ssm_a_im, ssm_log_dt, ssm_b_re, ssm_b_im, ssm_c_re, ssm_c_im, ssm_d, ssm_glu_w, gdn_conv_w, gdn_a_log, gdn_dt_bias, gdn_norm, nsa_cmp_k_pe, nsa_cmp_k_w1, nsa_cmp_k_w2, nsa_cmp_v_pe, nsa_cmp_v_w1, nsa_cmp_v_w2, rel_bias, sgu_ln_g, sgu_ln_b, sgu_w, sgu_b, w_br_ssm, w_br_gdn, w_br_nsa, w_br_sgu, w_out, ffn_w_gate, ffn_w_up, ffn_w_down, moe_router, moe_w_gate, moe_w_up, moe_w_down):
    bs, s, d = x.shape
    t = bs * s
    xf = x.reshape(t, d)
    for layer in range(DEPTH):
        proj = in_proj(xf, norm_mix[layer], _permute_w_in(w_in[layer]))
        p3 = proj.reshape(bs, s, PROJ_COLS)

        def seg(off, width):
            return p3[..., off:off + width]

        y_ssm = ssm_mixer(seg(OFF_SSM, SSM_WIDTH), ssm_a_re[layer], ssm_a_im[layer],
                          ssm_log_dt[layer], ssm_b_re[layer], ssm_b_im[layer], ssm_c_re[layer],
                          ssm_c_im[layer], ssm_d[layer], ssm_glu_w[layer])
        y_gdn = gdn_mixer(proj, bs, gdn_conv_w[layer], gdn_a_log[layer], gdn_dt_bias[layer],
                          gdn_norm[layer])
        kvs = [seg(OFF_KV + i * NSA_KV, NSA_KV) for i in range(6)]
        y_nsa = nsa_mixer(seg(OFF_NQ, NSA_Q), *kvs, seg(OFF_SMALL + 2 * GDN_HEADS, 3 * NSA_HEADS),
                          positions, nsa_cmp_k_pe[layer], nsa_cmp_k_w1[layer], nsa_cmp_k_w2[layer],
                          nsa_cmp_v_pe[layer], nsa_cmp_v_w1[layer], nsa_cmp_v_w2[layer], rel_bias)
        y_sgu = sgu_mixer(proj, sgu_ln_g[layer], sgu_ln_b[layer], sgu_w[layer], sgu_b[layer])
        ys = [y_ssm.reshape(t, -1), y_gdn, y_nsa.reshape(t, -1), y_sgu]
        ws = [w[layer].astype(BF16) for w in (w_br_ssm, w_br_gdn, w_br_nsa, w_br_sgu)]
        xf = merge(xf, proj, ys, ws, w_out[layer].astype(BF16))
        i = layer // 2
        if layer % 2 == 0:
            xf = dense_ffn(xf, norm_ffn[layer], ffn_w_gate[i], ffn_w_up[i], ffn_w_down[i])
        else:
            xf = moe_layer(xf, norm_ffn[layer], moe_router[i], moe_w_gate[i], moe_w_up[i],
                           moe_w_down[i], norm_final)
    return xf.reshape(bs, s, d)
```

```python
import functools
import math

import jax
import jax.numpy as jnp
from jax import lax
import numpy as np
from jax.experimental import pallas as pl
from jax.experimental.pallas import tpu as pltpu

F32 = jnp.float32
BF16 = jnp.bfloat16

D_MODEL = 1024
DEPTH = 2
SSM_WIDTH = D_MODEL // 2
SSM_GROUP = 16
SSM_GROUPS = SSM_WIDTH // SSM_GROUP
SSM_STATE = 64
GDN_HEADS = 4
GDN_DK = 128
GDN_DV = 128
GDN_CONV = 4
GDN_CHUNK = 64
NSA_HEADS = 8
NSA_KV_GROUPS = 2
NSA_HPG = NSA_HEADS // NSA_KV_GROUPS
NSA_DH = 64
NSA_CMP_LEN = 32
NSA_CMP_STRIDE = 16
NSA_CMP_HIDDEN = 128
NSA_SEL_LEN = 64
NSA_SEL_TOPK = 16
NSA_WINDOW = 512
NSA_QBLOCK = 128
SGU_WIDTH = D_MODEL // 2
SGU_GROUPS = 4
SGU_CHUNK = 128
REL_BUCKETS = 32
REL_MAX_DIST = 128
FFN_DENSE = 2816
N_EXPERTS = 8
TOP_K = 2
FFN_EXPERT = 3584
N_BRANCH = 4
RMS_EPS = 1e-6
GDN_QKV = GDN_HEADS * (2 * GDN_DK + GDN_DV)
NSA_Q = NSA_HEADS * NSA_DH
NSA_KV = NSA_KV_GROUPS * NSA_DH

LANES = 128
VMEM_LIMIT = 48 * 1024 * 1024
MASKED = -1e30

OFF_QKV = 0
OFF_SSM = OFF_QKV + GDN_QKV
OFF_Z = OFF_SSM + SSM_WIDTH
OFF_NQ = OFF_Z + GDN_HEADS * GDN_DV
OFF_SGU = OFF_NQ + NSA_Q
OFF_GATE = OFF_SGU + 2 * SGU_WIDTH
OFF_KV = OFF_GATE + N_BRANCH * D_MODEL
OFF_SMALL = OFF_KV + 6 * NSA_KV
SMALL_USED = 2 * GDN_HEADS + 3 * NSA_HEADS
PROJ_COLS = 9216


def _params(*sem):
    return pltpu.CompilerParams(dimension_semantics=sem, vmem_limit_bytes=VMEM_LIMIT)


def _rms(x, g):
    return x * lax.rsqrt(jnp.mean(x * x, axis=-1, keepdims=True) + RMS_EPS) * g


def _dot_nt(a, b):
    return lax.dot_general(a, b, (((1,), (1,)), ((), ())), preferred_element_type=F32)


def _dot_tn(a, b):
    return lax.dot_general(a, b, (((0,), (0,)), ((), ())), preferred_element_type=F32)


def _split_bf16(x):
    hi = x.astype(BF16)
    return hi, (x - hi.astype(F32)).astype(BF16)


def _in_proj_body(x_ref, g_ref, w_ref, o_ref, h_ref):
    @pl.when(pl.program_id(1) == 0)
    def _():
        h_ref[...] = _rms(x_ref[...], g_ref[...]).astype(BF16)

    o_ref[...] = jnp.dot(h_ref[...], w_ref[...].astype(BF16), preferred_element_type=F32)


def in_proj(x, g, w, tm=1024, tn=512):
    t, d = x.shape
    n = w.shape[1]
    return pl.pallas_call(
        _in_proj_body,
        grid=(t // tm, n // tn),
        in_specs=[pl.BlockSpec((tm, d), lambda i, j: (i, 0)),
                  pl.BlockSpec((1, d), lambda i, j: (0, 0)),
                  pl.BlockSpec((d, tn), lambda i, j: (0, j))],
        out_specs=pl.BlockSpec((tm, tn), lambda i, j: (i, j)),
        out_shape=jax.ShapeDtypeStruct((t, n), F32),
        scratch_shapes=[pltpu.VMEM((tm, d), BF16)],
        compiler_params=_params("parallel", "arbitrary"),
        name="in_proj",
    )(x, g.reshape(1, d), w)


def _merge_body(x_ref, gate_ref, ya_ref, yb_ref, yc_ref, yd_ref,
                wa_ref, wb_ref, wc_ref, wd_ref, wo_ref, o_ref):
    def branch(k, y_ref, w_ref):
        p = jnp.dot(y_ref[...].astype(BF16), w_ref[...], preferred_element_type=F32)
        return jax.nn.sigmoid(gate_ref[:, k * D_MODEL:(k + 1) * D_MODEL]) * p

    merged = (branch(0, ya_ref, wa_ref) + branch(1, yb_ref, wb_ref)
              + branch(2, yc_ref, wc_ref) + branch(3, yd_ref, wd_ref))
    o_ref[...] = x_ref[...] + jnp.dot(merged.astype(BF16), wo_ref[...],
                                      preferred_element_type=F32)


def merge(x, proj, ys, ws, w_out, tm=256):
    t, d = x.shape
    gate_blk = OFF_GATE // (N_BRANCH * D_MODEL)
    row = lambda i: (i, 0)
    fixed = lambda i: (0, 0)
    in_specs = [pl.BlockSpec((tm, d), row),
                pl.BlockSpec((tm, N_BRANCH * D_MODEL), lambda i: (i, gate_blk))]
    in_specs += [pl.BlockSpec((tm, y.shape[1]), row) for y in ys]
    in_specs += [pl.BlockSpec(w.shape, fixed) for w in ws]
    in_specs += [pl.BlockSpec(w_out.shape, fixed)]
    return pl.pallas_call(
        _merge_body,
        grid=(t // tm,),
        in_specs=in_specs,
        out_specs=pl.BlockSpec((tm, d), row),
        out_shape=jax.ShapeDtypeStruct((t, d), F32),
        compiler_params=_params("parallel"),
        name="merge",
    )(x, proj, *ys, *ws, w_out)


def _ffn_body(x_ref, g_ref, wg_ref, wu_ref, wd_ref, o_ref, h_ref, acc_ref):
    f = pl.program_id(1)

    @pl.when(f == 0)
    def _():
        h_ref[...] = _rms(x_ref[...], g_ref[...]).astype(BF16)
        acc_ref[...] = x_ref[...]

    h = h_ref[...]
    a = jnp.dot(h, wg_ref[...].astype(BF16), preferred_element_type=F32)
    u = jnp.dot(h, wu_ref[...].astype(BF16), preferred_element_type=F32)
    act = (a * jax.nn.sigmoid(a) * u).astype(BF16)
    acc_ref[...] += jnp.dot(act, wd_ref[...].astype(BF16), preferred_element_type=F32)

    @pl.when(f == pl.num_programs(1) - 1)
    def _():
        o_ref[...] = acc_ref[...]


def dense_ffn(x, g, wg, wu, wd, tm=1024, tf=256):
    t, d = x.shape
    nf = wg.shape[1]
    return pl.pallas_call(
        _ffn_body,
        grid=(t // tm, nf // tf),
        in_specs=[pl.BlockSpec((tm, d), lambda i, f: (i, 0)),
                  pl.BlockSpec((1, d), lambda i, f: (0, 0)),
                  pl.BlockSpec((d, tf), lambda i, f: (0, f)),
                  pl.BlockSpec((d, tf), lambda i, f: (0, f)),
                  pl.BlockSpec((tf, d), lambda i, f: (f, 0))],
        out_specs=pl.BlockSpec((tm, d), lambda i, f: (i, 0)),
        out_shape=jax.ShapeDtypeStruct((t, d), F32),
        scratch_shapes=[pltpu.VMEM((tm, d), BF16), pltpu.VMEM((tm, d), F32)],
        compiler_params=_params("parallel", "arbitrary"),
        name="dense_ffn",
    )(x, g.reshape(1, d), wg, wu, wd)


def _route_body(x_ref, g_ref, rhi_ref, rlo_ref, h_ref, idx_ref, wt_ref):
    h = _rms(x_ref[...], g_ref[...])
    hi = h.astype(BF16)
    lo = (h - hi.astype(F32)).astype(BF16)
    h_ref[...] = h
    logits = (jnp.dot(hi, rhi_ref[...], preferred_element_type=F32)
              + jnp.dot(hi, rlo_ref[...], preferred_element_type=F32)
              + jnp.dot(lo, rhi_ref[...], preferred_element_type=F32))
    col = lax.broadcasted_iota(jnp.int32, logits.shape, 1).astype(F32)
    neg = jnp.float32(-jnp.inf)
    lg = jnp.where(col < N_EXPERTS, logits, neg)
    m1 = jnp.max(lg, axis=-1, keepdims=True)
    i1 = jnp.min(jnp.where(lg == m1, col, float(LANES)), axis=-1, keepdims=True)
    lg2 = jnp.where(col == i1, neg, lg)
    m2 = jnp.max(lg2, axis=-1, keepdims=True)
    i2 = jnp.min(jnp.where(lg2 == m2, col, float(LANES)), axis=-1, keepdims=True)
    e = jnp.exp(m2 - m1)
    w1 = 1.0 / (1.0 + e)
    w2 = e / (1.0 + e)
    idx_ref[...] = jnp.where(col == 0, i1, jnp.where(col == 1, i2, 0.0)).astype(jnp.int32)
    wt_ref[...] = jnp.where(col == 0, w1, jnp.where(col == 1, w2, 0.0))


def moe_route(x, g, r_hi, r_lo, tm=512):
    t, d = x.shape
    row = lambda i: (i, 0)
    fixed = lambda i: (0, 0)
    return pl.pallas_call(
        _route_body,
        grid=(t // tm,),
        in_specs=[pl.BlockSpec((tm, d), row), pl.BlockSpec((1, d), fixed),
                  pl.BlockSpec((d, LANES), fixed), pl.BlockSpec((d, LANES), fixed)],
        out_specs=[pl.BlockSpec((tm, d), row), pl.BlockSpec((tm, LANES), row),
                   pl.BlockSpec((tm, LANES), row)],
        out_shape=[jax.ShapeDtypeStruct((t, d), F32),
                   jax.ShapeDtypeStruct((t, LANES), jnp.int32),
                   jax.ShapeDtypeStruct((t, LANES), F32)],
        compiler_params=_params("parallel"),
        name="moe_route",
    )(x, g.reshape(1, d), r_hi, r_lo)


def _experts_body(te_ref, nu_ref, src_ref, h_hbm, rw_ref, wg_ref, wu_ref, wd_ref, o_ref,
                  xbuf_ref, xs_ref, acc_ref, sem):
    i = pl.program_id(0)
    f = pl.program_id(1)
    tm = xs_ref.shape[0]
    n_used = nu_ref[0]
    used = i < n_used

    def gather(tile, slot):
        def row(r, carry):
            tok = src_ref[tile * tm + r]
            pltpu.make_async_copy(h_hbm.at[pl.ds(tok, 1), :], xbuf_ref.at[slot, pl.ds(r, 1), :],
                                  sem.at[slot]).start()
            return carry

        lax.fori_loop(0, tm, row, 0, unroll=8)

    @pl.when(f == 0)
    def _():
        acc_ref[...] = jnp.zeros_like(acc_ref)

        @pl.when(used)
        def _():
            slot = i % 2

            @pl.when(i == 0)
            def _():
                gather(0, 0)

            pltpu.make_async_copy(h_hbm.at[pl.ds(0, tm), :], xbuf_ref.at[slot], sem.at[slot]).wait()

            @pl.when(i + 1 < n_used)
            def _():
                gather(i + 1, 1 - slot)

            xs_ref[...] = xbuf_ref[slot].astype(BF16)

    @pl.when(used)
    def _():
        h = xs_ref[...]
        a = jnp.dot(h, wg_ref[0].astype(BF16), preferred_element_type=F32)
        u = jnp.dot(h, wu_ref[0].astype(BF16), preferred_element_type=F32)
        act = (a * jax.nn.sigmoid(a) * u).astype(BF16)
        acc_ref[...] += jnp.dot(act, wd_ref[0].astype(BF16), preferred_element_type=F32)

    @pl.when(f == pl.num_programs(1) - 1)
    def _():
        o_ref[...] = (acc_ref[...] * rw_ref[...]).astype(o_ref.dtype)


def moe_experts(tile_expert, n_used, src, h, row_w, wg, wu, wd, tm, tf=512):
    p = src.shape[0]
    d = h.shape[1]
    nf = wg.shape[2]
    grid_spec = pltpu.PrefetchScalarGridSpec(
        num_scalar_prefetch=3,
        grid=(p // tm, nf // tf),
        in_specs=[pl.BlockSpec(memory_space=pl.ANY),
                  pl.BlockSpec((tm, 1), lambda i, f, *_: (i, 0)),
                  pl.BlockSpec((1, d, tf), lambda i, f, te, *_: (te[i], 0, f)),
                  pl.BlockSpec((1, d, tf), lambda i, f, te, *_: (te[i], 0, f)),
                  pl.BlockSpec((1, tf, d), lambda i, f, te, *_: (te[i], f, 0))],
        out_specs=pl.BlockSpec((tm, d), lambda i, f, *_: (i, 0)),
        scratch_shapes=[pltpu.VMEM((2, tm, d), F32), pltpu.VMEM((tm, d), BF16),
                        pltpu.VMEM((tm, d), F32), pltpu.SemaphoreType.DMA((2,))],
    )
    return pl.pallas_call(
        _experts_body,
        grid_spec=grid_spec,
        out_shape=jax.ShapeDtypeStruct((p, d), BF16),
        compiler_params=_params("arbitrary", "arbitrary"),
        name="moe_experts",
    )(tile_expert, n_used, src, h, row_w, wg, wu, wd)


def _combine_norm_body(x_ref, ya_ref, yb_ref, g_ref, o_ref):
    o_ref[...] = _rms(x_ref[...] + ya_ref[...].astype(F32) + yb_ref[...].astype(F32), g_ref[...])


def combine_norm(x, ya, yb, g, tm=1024):
    t, d = x.shape
    row = lambda i: (i, 0)
    return pl.pallas_call(
        _combine_norm_body,
        grid=(t // tm,),
        in_specs=[pl.BlockSpec((tm, d), row)] * 3 + [pl.BlockSpec((1, d), lambda i: (0, 0))],
        out_specs=pl.BlockSpec((tm, d), row),
        out_shape=jax.ShapeDtypeStruct((t, d), F32),
        compiler_params=_params("parallel"),
        name="combine_norm",
    )(x, ya, yb, g.reshape(1, d))


def moe_layer(x, g_norm, router, wg, wu, wd, g_final, tm=1024):
    t, d = x.shape
    r = jnp.zeros((d, LANES), F32).at[:, :N_EXPERTS].set(router)
    r_hi = r.astype(BF16)
    r_lo = (r - r_hi.astype(F32)).astype(BF16)
    h, idx, wts = moe_route(x, g_norm, r_hi, r_lo)
    e_flat = jnp.concatenate([idx[:, 0], idx[:, 1]])
    w_flat = jnp.concatenate([wts[:, 0], wts[:, 1]])
    onehot = (e_flat[:, None] == jnp.arange(N_EXPERTS)[None, :]).astype(jnp.int32)
    csum = jnp.cumsum(onehot, axis=0)
    rank = jnp.sum((csum - onehot) * onehot, axis=1)
    counts = csum[-1]
    padded = ((counts + tm - 1) // tm) * tm
    ends = jnp.cumsum(padded)
    starts = ends - padded
    dest = starts[e_flat] + rank
    n_tiles = (TOP_K * t) // tm + N_EXPERTS
    p = n_tiles * tm
    tok = jnp.concatenate([jnp.arange(t, dtype=jnp.int32)] * TOP_K)
    src = jnp.zeros((p,), jnp.int32).at[dest].set(tok)
    row_w = jnp.zeros((p,), F32).at[dest].set(w_flat)
    tile_expert = jnp.minimum(
        jnp.searchsorted(ends, jnp.arange(n_tiles, dtype=jnp.int32) * tm, side="right"),
        N_EXPERTS - 1).astype(jnp.int32)
    n_used = (ends[-1] // tm).astype(jnp.int32).reshape(1)
    ys = moe_experts(tile_expert, n_used, src, h, row_w.reshape(p, 1), wg, wu, wd, tm)
    ya = jnp.take(ys, dest[:t], axis=0)
    yb = jnp.take(ys, dest[t:], axis=0)
    return combine_norm(x, ya, yb, g_final)


SSM_TC = 512
SSM_SUB = 128
SSM_HALF = 256
SSM_NSTATE = SSM_GROUPS * SSM_STATE


def _ssm_body(u_ref, bw_ref, cw_ref, d_ref, glu_ref, ar_ref, ai_ref, pr_ref, pi_ref, o_ref,
              xr_ref, xi_ref, cr_ref, ci_ref):
    @pl.when(pl.program_id(1) == 0)
    def _():
        cr_ref[...] = jnp.zeros_like(cr_ref)
        ci_ref[...] = jnp.zeros_like(ci_ref)

    u = u_ref[0]
    ub = u.astype(BF16)
    nblk = SSM_WIDTH // SSM_HALF
    sblk = SSM_NSTATE // nblk
    for k in range(nblk):
        bu = jnp.dot(ub[:, k * SSM_HALF:(k + 1) * SSM_HALF], bw_ref[k], preferred_element_type=F32)
        xr_ref[:, k * sblk:(k + 1) * sblk] = bu[:, :sblk]
        xi_ref[:, k * sblk:(k + 1) * sblk] = bu[:, sblk:]

    row = lax.broadcasted_iota(jnp.int32, (SSM_SUB, LANES), 0)
    n_steps = SSM_SUB.bit_length() - 1

    def shift(x, d):
        if d % 8 == 0:
            return jnp.concatenate([jnp.zeros((d, LANES), F32), x[:SSM_SUB - d]], axis=0)
        return jnp.where(row >= d, pltpu.roll(x, d, 0), 0.0)

    def strip(c, carry):
        col = pl.ds(pl.multiple_of(c * LANES, LANES), LANES)
        c_r = cr_ref[:, col]
        c_i = ci_ref[:, col]
        for sub in range(SSM_TC // SSM_SUB):
            rows = slice(sub * SSM_SUB, (sub + 1) * SSM_SUB)
            xr = xr_ref[rows, col]
            xi = xi_ref[rows, col]
            for i in range(n_steps):
                a_r = ar_ref[i:i + 1, col]
                a_i = ai_ref[i:i + 1, col]
                sr, si = shift(xr, 1 << i), shift(xi, 1 << i)
                xr, xi = xr + a_r * sr - a_i * si, xi + a_r * si + a_i * sr
            p_r = pr_ref[:, col]
            p_i = pi_ref[:, col]
            xr, xi = xr + p_r * c_r - p_i * c_i, xi + p_r * c_i + p_i * c_r
            xr_ref[rows, col] = xr
            xi_ref[rows, col] = xi
            c_r = xr[SSM_SUB - 1:SSM_SUB, :]
            c_i = xi[SSM_SUB - 1:SSM_SUB, :]
        cr_ref[:, col] = c_r
        ci_ref[:, col] = c_i
        return carry

    lax.fori_loop(0, SSM_NSTATE // LANES, strip, 0)

    ys = []
    for k in range(nblk):
        st = jnp.concatenate([xr_ref[:, k * sblk:(k + 1) * sblk].astype(BF16),
                              xi_ref[:, k * sblk:(k + 1) * sblk].astype(BF16)], axis=1)
        ys.append(jnp.dot(st, cw_ref[k], preferred_element_type=F32))
    y = jax.nn.gelu(jnp.concatenate(ys, axis=1) + d_ref[...] * u)
    o_ref[0] = y * jax.nn.sigmoid(jnp.dot(y.astype(BF16), glu_ref[...], preferred_element_type=F32))


def ssm_mixer(u, a_re, a_im, log_dt, b_re, b_im, c_re, c_im, d_skip, glu_w):
    bs, s, _ = u.shape
    assert s % SSM_TC == 0
    nblk = SSM_WIDTH // SSM_HALF
    gpb = SSM_GROUPS // nblk
    dt = jnp.exp(log_dt)[:, None]
    mag = jnp.exp(a_re * dt)
    abar_r, abar_i = mag * jnp.cos(a_im * dt), mag * jnp.sin(a_im * dt)
    nr, ni = abar_r - 1.0, abar_i
    den = a_re * a_re + a_im * a_im
    sr, si = (nr * a_re + ni * a_im) / den, (ni * a_re - nr * a_im) / den
    bbar_r = sr[..., None] * b_re - si[..., None] * b_im
    bbar_i = sr[..., None] * b_im + si[..., None] * b_re

    def powers(k):
        kk = k.astype(F32)[:, None, None]
        m = jnp.exp(kk * (a_re * dt))
        return ((m * jnp.cos(kk * (a_im * dt))).reshape(-1, SSM_NSTATE),
                (m * jnp.sin(kk * (a_im * dt))).reshape(-1, SSM_NSTATE))

    ar, ai = powers(2 ** jnp.arange(8))
    pr, pi = powers(jnp.arange(1, SSM_SUB + 1))
    eye = jnp.eye(gpb, dtype=F32)

    def in_block(w):
        return jnp.einsum('gnp,gh->gphn', w, eye).reshape(gpb * SSM_GROUP, gpb * SSM_STATE)

    def out_block(w):
        return jnp.einsum('gpn,gh->gnhp', w, eye).reshape(gpb * SSM_STATE, gpb * SSM_GROUP)

    bw = jnp.stack([jnp.concatenate([in_block(bbar_r[k * gpb:(k + 1) * gpb]),
                                     in_block(bbar_i[k * gpb:(k + 1) * gpb])], axis=1)
                    for k in range(nblk)]).astype(BF16)
    cw = jnp.stack([jnp.concatenate([out_block(c_re[k * gpb:(k + 1) * gpb]),
                                     -out_block(c_im[k * gpb:(k + 1) * gpb])], axis=0)
                    for k in range(nblk)]).astype(BF16)
    fixed2 = lambda b, i: (0, 0)
    fixed3 = lambda b, i: (0, 0, 0)
    return pl.pallas_call(
        _ssm_body,
        grid=(bs, s // SSM_TC),
        in_specs=[pl.BlockSpec((1, SSM_TC, SSM_WIDTH), lambda b, i: (b, i, 0)),
                  pl.BlockSpec(bw.shape, fixed3), pl.BlockSpec(cw.shape, fixed3),
                  pl.BlockSpec((1, SSM_WIDTH), fixed2), pl.BlockSpec((SSM_WIDTH, SSM_WIDTH), fixed2),
                  pl.BlockSpec((8, SSM_NSTATE), fixed2), pl.BlockSpec((8, SSM_NSTATE), fixed2),
                  pl.BlockSpec((SSM_SUB, SSM_NSTATE), fixed2), pl.BlockSpec((SSM_SUB, SSM_NSTATE), fixed2)],
        out_specs=pl.BlockSpec((1, SSM_TC, SSM_WIDTH), lambda b, i: (b, i, 0)),
        out_shape=jax.ShapeDtypeStruct((bs, s, SSM_WIDTH), F32),
        scratch_shapes=[pltpu.VMEM((SSM_TC, SSM_NSTATE), F32), pltpu.VMEM((SSM_TC, SSM_NSTATE), F32),
                        pltpu.VMEM((1, SSM_NSTATE), F32), pltpu.VMEM((1, SSM_NSTATE), F32)],
        compiler_params=_params("parallel", "arbitrary"),
        name="ssm_scan",
    )(u, bw, cw, d_skip.reshape(1, SSM_WIDTH), glu_w.astype(BF16), ar, ai, pr, pi)


GDN_TC = 256
GDN_HALO = 8


def _gdn_body(nega_ref, dtb_ref, q_ref, k_ref, v_ref, z_ref, sm_ref, wq_ref, wk_ref, wv_ref, ng_ref,
              o_ref, state_ref, hq_ref, hk_ref, hv_ref, vnew_ref):
    TC, CH, DK, H = GDN_TC, GDN_CHUNK, GDN_DK, GDN_HEADS
    heads = range(H)

    @pl.when(pl.program_id(1) == 0)
    def _():
        state_ref[...] = jnp.zeros_like(state_ref)
        hq_ref[...] = jnp.zeros_like(hq_ref)
        hk_ref[...] = jnp.zeros_like(hk_ref)
        hv_ref[...] = jnp.zeros_like(hv_ref)

    def conv_silu(x_ref, halo_ref, w_ref):
        x = x_ref[...]
        xx = jnp.concatenate([halo_ref[...], x], axis=0)
        w = w_ref[...]
        y = w[GDN_CONV - 1:GDN_CONV] * x
        for d in range(1, GDN_CONV):
            y = y + w[GDN_CONV - 1 - d:GDN_CONV - d] * pltpu.roll(xx, d, 0)[GDN_HALO:GDN_HALO + TC]
        halo_ref[...] = x[TC - GDN_HALO:TC]
        return y * jax.nn.sigmoid(y)

    def per_head(x):
        return [x[:, h * LANES:(h + 1) * LANES] for h in heads]

    q = per_head(conv_silu(q_ref, hq_ref, wq_ref))
    k = per_head(conv_silu(k_ref, hk_ref, wk_ref))
    v = per_head(conv_silu(v_ref, hv_ref, wv_ref))
    q = [x * lax.rsqrt(jnp.sum(x * x, axis=-1, keepdims=True) + 1e-6) * (DK ** -0.5) for x in q]
    k = [x * lax.rsqrt(jnp.sum(x * x, axis=-1, keepdims=True) + 1e-6) for x in k]

    small = sm_ref[...]
    beta = [jax.nn.sigmoid(small[:, H + h:H + h + 1]) for h in heads]
    la = []
    for h in heads:
        xa = small[:, h:h + 1] + dtb_ref[h]
        softplus = jnp.maximum(xa, 0.0) + jnp.log(1.0 + jnp.exp(-jnp.abs(xa)))
        la.append(jnp.broadcast_to(nega_ref[h] * softplus, (TC, LANES)))

    ri = lax.broadcasted_iota(jnp.int32, (TC, TC), 0)
    ci = lax.broadcasted_iota(jnp.int32, (TC, TC), 1)
    same = (ri // CH) == (ci // CH)
    causal = jnp.where(same, jnp.where(ci <= ri, 1.0, 0.0), 0.0)
    strict = jnp.where(ci < ri, causal, 0.0)
    eye = jnp.where(ri == ci, 1.0, 0.0)
    tri = causal.astype(BF16)
    lane = lax.broadcasted_iota(jnp.int32, (TC, LANES), 1)

    g = []
    for h in heads:
        la_hi, la_lo = _split_bf16(la[h])
        g.append(jnp.dot(tri, la_hi, preferred_element_type=F32)
                 + jnp.dot(tri, la_lo, preferred_element_type=F32))
    decay = []
    for h in heads:
        g_hi = g[h].astype(BF16).astype(F32)
        g_lo = g[h] - g_hi
        lhs = jnp.where(lane == 0, g_hi, jnp.where(lane == 1, g_lo, jnp.where(lane < 4, 1.0, 0.0)))
        rhs = jnp.where(lane < 2, 1.0, jnp.where(lane == 2, -g_hi, jnp.where(lane == 3, -g_lo, 0.0)))
        decay.append(jnp.exp(jnp.where(causal > 0.5, _dot_nt(lhs.astype(BF16), rhs.astype(BF16)), MASKED)))

    kb = [k[h] * beta[h] for h in heads]
    k_b = [x.astype(BF16) for x in k]
    lmat = [strict * _dot_nt(kb[h].astype(BF16), k_b[h]) * decay[h] for h in heads]
    tinv = [eye - x for x in lmat]
    pw = lmat
    for _ in range(CH.bit_length() - 2):
        pw = [jnp.dot(x.astype(BF16), x.astype(BF16), preferred_element_type=F32) for x in pw]
        tinv = [tinv[h] + jnp.dot(tinv[h].astype(BF16), pw[h].astype(BF16), preferred_element_type=F32)
                for h in heads]
    eg = [jnp.exp(x) for x in g]
    sol = [jnp.dot(tinv[h].astype(BF16),
                   jnp.concatenate([v[h] * beta[h], kb[h] * eg[h]], axis=1).astype(BF16),
                   preferred_element_type=F32) for h in heads]
    attn = [(causal * _dot_nt(q[h].astype(BF16), k_b[h]) * decay[h]).astype(BF16) for h in heads]
    q_st = [(q[h] * eg[h]).astype(BF16) for h in heads]

    vnew_ref[...] = jnp.zeros_like(vnew_ref)
    for c in range(TC // CH):
        rows = slice(c * CH, (c + 1) * CH)
        g_last = [x[(c + 1) * CH - 1:(c + 1) * CH, :] for x in g]
        st = [state_ref[h] for h in heads]
        st_b = [x.astype(BF16) for x in st]
        v_new = [sol[h][rows, :GDN_DV]
                 - jnp.dot(sol[h][rows, GDN_DV:].astype(BF16), st_b[h], preferred_element_type=F32)
                 for h in heads]
        for h in heads:
            vnew_ref[h, rows, :] = v_new[h].astype(BF16)
        o_c = [jnp.dot(q_st[h][rows], st_b[h], preferred_element_type=F32)
               + jnp.dot(attn[h][rows], vnew_ref[h], preferred_element_type=F32) for h in heads]
        for h in heads:
            k_st = (k[h][rows] * jnp.exp(g_last[h] - g[h][rows])).astype(BF16)
            state_ref[h] = st[h] * jnp.exp(g_last[h][:, :1]) + _dot_tn(k_st, v_new[h].astype(BF16))
        for h in heads:
            o = o_c[h] * lax.rsqrt(jnp.mean(o_c[h] * o_c[h], axis=-1, keepdims=True) + RMS_EPS) * ng_ref[...]
            zc = z_ref[rows, h * LANES:(h + 1) * LANES]
            o_ref[rows, h * LANES:(h + 1) * LANES] = o * (zc * jax.nn.sigmoid(zc))


def gdn_mixer(proj, bs, conv_w, a_log, dt_bias, norm_g):
    t = proj.shape[0]
    s = t // bs
    nt = s // GDN_TC
    H = GDN_HEADS
    hw = H * LANES
    assert s % GDN_TC == 0 and GDN_DK == LANES and GDN_DV == LANES and OFF_QKV == 0 and OFF_Z % hw == 0
    cw = jnp.zeros((8, GDN_QKV), F32).at[:GDN_CONV].set(conv_w)
    tok = lambda blk: pl.BlockSpec((GDN_TC, hw), lambda b, i: (b * nt + i, blk))
    wspec = lambda blk: pl.BlockSpec((8, hw), lambda b, i: (0, blk))
    smem = pl.BlockSpec(memory_space=pltpu.SMEM)
    return pl.pallas_call(
        _gdn_body,
        grid=(bs, nt),
        in_specs=[smem, smem, tok(0), tok(1), tok(2), tok(OFF_Z // hw),
                  pl.BlockSpec((GDN_TC, 2 * LANES), lambda b, i: (b * nt + i, OFF_SMALL // (2 * LANES))),
                  wspec(0), wspec(1), wspec(2), pl.BlockSpec((1, GDN_DV), lambda b, i: (0, 0))],
        out_specs=pl.BlockSpec((GDN_TC, hw), lambda b, i: (b * nt + i, 0)),
        out_shape=jax.ShapeDtypeStruct((t, hw), F32),
        scratch_shapes=[pltpu.VMEM((H, GDN_DK, GDN_DV), F32)] + [pltpu.VMEM((GDN_HALO, hw), F32)] * 3
        + [pltpu.VMEM((H, GDN_TC, GDN_DV), BF16)],
        compiler_params=_params("parallel", "arbitrary"),
        name="gdn_delta",
    )(-jnp.exp(a_log), dt_bias.astype(F32), proj, proj, proj, proj, proj, cw, cw, cw,
      norm_g.reshape(1, GDN_DV))


NSA_KT = 256
NSA_NEG = MASKED
NSA_WTILES = NSA_WINDOW // LANES + 1


def _bucket_table():
    n = np.arange(LANES)
    max_exact = REL_BUCKETS // 2
    nf = np.maximum(n, 1).astype(np.float32)
    large = max_exact + (np.log(nf / np.float32(max_exact)) / np.float32(math.log(REL_MAX_DIST / max_exact))
                         * np.float32(REL_BUCKETS - max_exact)).astype(np.int32)
    tbl = np.where(n < max_exact, n, np.minimum(large, REL_BUCKETS - 1))
    assert tbl[-1] == REL_BUCKETS - 1 and REL_MAX_DIST <= LANES
    return tbl


def _compress_body(x_ref, pe_ref, w1_ref, w2_ref, o_ref):
    x = x_ref[0, 0, 0]
    w1 = w1_ref[0]
    half = NSA_CMP_STRIDE * NSA_DH
    nc = x.shape[0]
    a = jnp.dot(x, w1[:half], preferred_element_type=F32)
    b = jnp.dot(x, w1[half:], preferred_element_type=F32)
    pe_h = jnp.dot(pe_ref[0], w1, preferred_element_type=F32)
    hid = a + pltpu.roll(b, nc - 1, 0) + pe_h[0:1, :]
    o_ref[0, 0, 0] = jnp.dot(jax.nn.gelu(hid).astype(BF16), w2_ref[0], preferred_element_type=F32)


def nsa_compress(x, pe, w1, w2):
    _, bs, g, nc, width = x.shape
    return pl.pallas_call(
        _compress_body,
        grid=(2, bs, g),
        in_specs=[pl.BlockSpec((1, 1, 1, nc, width), lambda i, b, j: (i, b, j, 0, 0)),
                  pl.BlockSpec((1,) + pe.shape[1:], lambda i, b, j: (i, 0, 0)),
                  pl.BlockSpec((1,) + w1.shape[1:], lambda i, b, j: (i, 0, 0)),
                  pl.BlockSpec((1,) + w2.shape[1:], lambda i, b, j: (i, 0, 0))],
        out_specs=pl.BlockSpec((1, 1, 1, nc, NSA_DH), lambda i, b, j: (i, b, j, 0, 0)),
        out_shape=jax.ShapeDtypeStruct((2, bs, g, nc, NSA_DH), F32),
        compiler_params=_params("parallel", "parallel", "parallel"),
        name="nsa_compress",
    )(x, pe, w1, w2)


def _nsa_body(nfast, wstd, near_w, near_c, q_ref, gl_ref, qp_ref, kp_ref, cp_ref,
              ks_ref, vs_ref, kw_ref, vw_ref, kc_ref, vc_ref, agg_ref, td_ref, o_ref,
              m_ref, l_ref, acc_ref, sel_ref, *, n_sel, sel_k):
    QB, HG, G = NSA_QBLOCK, NSA_HPG, NSA_KV_GROUPS
    groups = range(G)
    qi = pl.program_id(1)
    s0 = qi * QB
    nc = kc_ref.shape[2]
    n_ct = nc // LANES
    q = [q_ref[0, g, 0] for g in groups]
    qp = qp_ref[...]

    def lanes4(x):
        return jnp.concatenate([x] * HG, axis=1)

    def delta_t(kp):
        idx = jnp.clip(_dot_nt(kp, qp), 0.0, float(LANES - 1)).astype(jnp.int32)
        outs = []
        for g in groups:
            heads = []
            for hg in range(HG):
                tb = jnp.broadcast_to(td_ref[g * HG + hg:g * HG + hg + 1, :], idx.shape)
                heads.append(jnp.take_along_axis(tb, idx, axis=1))
            outs.append(jnp.concatenate(heads, axis=1))
        return outs

    def maybe_delta(flag, kp):
        zeros = jnp.zeros((kp.shape[0], HG * QB), F32)
        return lax.cond(flag != 0, lambda: tuple(delta_t(kp)), lambda: (zeros,) * G)

    gate = [jax.nn.sigmoid(gl_ref[0, g, 0]) for g in groups]

    sc = [_dot_nt(kc_ref[0, g], q[g]) for g in groups]
    dl = [maybe_delta(near_c[qi * n_ct + ct], cp_ref[ct * LANES:(ct + 1) * LANES, :]) for ct in range(n_ct)]
    c_id = lax.broadcasted_iota(jnp.int32, (nc, QB), 0)
    c_q = lax.broadcasted_iota(jnp.int32, (nc, QB), 1)
    ok_c = lanes4(jnp.where(c_id * NSA_CMP_STRIDE + (NSA_CMP_LEN - 1) <= s0 + c_q, 1.0, 0.0))
    blk = lax.broadcasted_iota(jnp.int32, (LANES, QB), 0)
    t_tok = s0 + lax.broadcasted_iota(jnp.int32, (LANES, QB), 1)
    tb_blk = t_tok // NSA_SEL_LEN
    forced = jnp.where(blk == 0, 1.0, 0.0) + jnp.where(blk == tb_blk, 1.0, 0.0) \
        + jnp.where(blk == tb_blk - 1, 1.0, 0.0)
    blkf = blk.astype(F32)
    out, score = [], []
    for g in groups:
        s = sc[g] + jnp.concatenate([d[g] for d in dl], axis=0)
        s = jnp.where(ok_c > 0.5, s, NSA_NEG)
        e = jnp.exp(s - jnp.max(s, axis=0, keepdims=True)) * ok_c
        p = e * (1.0 / jnp.maximum(jnp.sum(e, axis=0, keepdims=True), 1e-30))
        out.append(gate[g][0:1, :] * jnp.dot(vc_ref[0, g], p.astype(BF16), preferred_element_type=F32))
        ps = p[:, 0:QB]
        for hg in range(1, HG):
            ps = ps + p[:, hg * QB:(hg + 1) * QB]
        ps_hi, ps_lo = _split_bf16(ps)
        imp = (jnp.dot(agg_ref[...], ps_hi, preferred_element_type=F32)
               + jnp.dot(agg_ref[...], ps_lo, preferred_element_type=F32))
        sco = jnp.where(forced > 0.5, 1e9, jnp.where(blk * NSA_SEL_LEN <= t_tok, imp, -1e9))
        score.append(jnp.where(blk < n_sel, sco, -jnp.inf))
    sel = [jnp.zeros((LANES, QB), F32) for _ in groups]
    for _ in range(sel_k):
        for g in groups:
            mx = jnp.max(score[g], axis=0, keepdims=True)
            first = jnp.min(jnp.where(score[g] == mx, blkf, float(LANES)), axis=0, keepdims=True)
            hit = blkf == first
            sel[g] = jnp.where(hit, 1.0, sel[g])
            score[g] = jnp.where(hit, -jnp.inf, score[g])
    for g in groups:
        sel_ref[g] = jnp.where(sel[g] > 0.5, 0.0, NSA_NEG)

    def reset():
        m_ref[...] = jnp.full_like(m_ref, NSA_NEG)
        l_ref[...] = jnp.zeros_like(l_ref)
        acc_ref[...] = jnp.zeros_like(acc_ref)

    def tile_step(g, k, s, v_t):
        m_old = m_ref[g, k]
        m_new = jnp.maximum(m_old, jnp.max(s, axis=0, keepdims=True))
        alpha = jnp.exp(m_old - m_new)
        p = jnp.exp(s - m_new)
        l_ref[g, k] = l_ref[g, k] * alpha + jnp.sum(p, axis=0, keepdims=True)
        acc_ref[g, k] = acc_ref[g, k] * alpha + jnp.dot(v_t, p.astype(BF16), preferred_element_type=F32)
        m_ref[g, k] = m_new

    def finish(g):
        m = jnp.maximum(m_ref[g, 0], m_ref[g, 1])
        w0, w1 = jnp.exp(m_ref[g, 0] - m), jnp.exp(m_ref[g, 1] - m)
        return (acc_ref[g, 0] * w0 + acc_ref[g, 1] * w1) * (1.0 / (l_ref[g, 0] * w0 + l_ref[g, 1] * w1))

    per_tile = NSA_KT // NSA_SEL_LEN
    key_r = lax.broadcasted_iota(jnp.int32, (NSA_KT, QB), 0)
    key_q = lax.broadcasted_iota(jnp.int32, (NSA_KT, QB), 1)

    def sel_scores(kt, with_delta, diag):
        k0 = pl.multiple_of(kt * NSA_KT, NSA_KT)
        s = [_dot_nt(ks_ref[0, g, pl.ds(k0, NSA_KT), :], q[g]) for g in groups]
        if with_delta:
            d = delta_t(kp_ref[pl.ds(k0, NSA_KT), :])
            s = [s[g] + d[g] for g in groups]
        res = []
        for g in groups:
            mask = jnp.concatenate(
                [jnp.broadcast_to(sel_ref[g, pl.ds(kt * per_tile + i, 1), :], (NSA_SEL_LEN, QB))
                 for i in range(per_tile)], axis=0)
            if diag:
                mask = jnp.where(k0 + key_r <= s0 + key_q, mask, NSA_NEG)
            res.append((s[g] + lanes4(mask), vs_ref[0, g, :, pl.ds(k0, NSA_KT)]))
        return res

    reset()
    n_past = (s0 + QB - 1) // NSA_KT
    n_pairs = nfast[qi] // 2

    def fast_body(i, carry):
        a = sel_scores(2 * i, False, False)
        b = sel_scores(2 * i + 1, False, False)
        for g in groups:
            tile_step(g, 0, *a[g])
        for g in groups:
            tile_step(g, 1, *b[g])
        return carry

    def slow_body(kt, carry):
        a = sel_scores(kt, True, False)
        for g in groups:
            tile_step(g, 1, *a[g])
        return carry

    lax.fori_loop(0, n_pairs, fast_body, 0)
    lax.fori_loop(2 * n_pairs, n_past, slow_body, 0)
    a = sel_scores(n_past, True, True)
    for g in groups:
        tile_step(g, 0, *a[g])
    out = [out[g] + gate[g][1:2, :] * finish(g) for g in groups]

    reset()
    w_r = lax.broadcasted_iota(jnp.int32, (LANES, QB), 0)
    w_q = lax.broadcasted_iota(jnp.int32, (LANES, QB), 1)
    in_window = lanes4(jnp.where(w_r > w_q, 0.0, NSA_NEG))
    causal_w = lanes4(jnp.where(w_r <= w_q, 0.0, NSA_NEG))

    def win_scores(g, st, n):
        st = pl.multiple_of(st, LANES)
        return _dot_nt(kw_ref[0, g, pl.ds(st, n), :], q[g]), vw_ref[0, g, :, pl.ds(st, n)]

    @pl.when(wstd[qi] != 0)
    def _():
        half = NSA_WINDOW // 2
        zeros = jnp.zeros((LANES, HG * QB), F32)
        sa = [win_scores(g, s0 - NSA_WINDOW, half) for g in groups]
        sb = [win_scores(g, s0 - half, half) for g in groups]
        sd = [win_scores(g, s0, LANES) for g in groups]
        near = delta_t(kp_ref[pl.ds(pl.multiple_of(s0 - LANES, LANES), LANES), :])
        diag = delta_t(kp_ref[pl.ds(pl.multiple_of(s0, LANES), LANES), :])
        for g in groups:
            tile_step(g, 0, sa[g][0] + jnp.concatenate([in_window, zeros], axis=0), sa[g][1])
        for g in groups:
            tile_step(g, 1, sb[g][0] + jnp.concatenate([zeros, near[g]], axis=0), sb[g][1])
        for g in groups:
            tile_step(g, 0, sd[g][0] + diag[g] + causal_w, sd[g][1])

    for j in range(NSA_WTILES):
        start = s0 - NSA_WINDOW + j * LANES

        @pl.when((wstd[qi] == 0) & (start >= 0))
        def _():
            sw = [win_scores(g, start, LANES) for g in groups]
            d = maybe_delta(near_w[qi * NSA_WTILES + j],
                            kp_ref[pl.ds(pl.multiple_of(start, LANES), LANES), :])
            for g in groups:
                s = sw[g][0] + d[g]
                if j == 0:
                    s = s + in_window
                elif j == NSA_WTILES - 1:
                    s = s + causal_w
                tile_step(g, j % 2, s, sw[g][1])

    for g in groups:
        o_ref[0, g, 0] = out[g] + gate[g][2:3, :] * finish(g)


def _pos_digits(p, key_side):
    d0, d1, d2 = (p & 127).astype(F32), ((p >> 7) & 127).astype(F32), (p >> 14).astype(F32)
    one = jnp.ones_like(d0)
    if key_side:
        cols = [one, one, one, -d2, -d1, -d0]
    else:
        cols = [16384.0 * d2, 128.0 * d1, d0, 16384.0 * one, 128.0 * one, one]
    out = jnp.stack(cols, axis=-1)
    return jnp.pad(out, ((0, 0), (0, LANES - out.shape[-1]))).astype(BF16)


def nsa_mixer(q, k_c, v_c, k_s, v_s, k_w, v_w, gate_logits, positions,
              ck_pe, ck_w1, ck_w2, cv_pe, cv_w1, cv_w2, rel_bias):
    q, k_c, v_c, k_s, v_s, k_w, v_w = lax.optimization_barrier((q, k_c, v_c, k_s, v_s, k_w, v_w))
    bs, s, _ = q.shape
    G, HG, DH, QB = NSA_KV_GROUPS, NSA_HPG, NSA_DH, NSA_QBLOCK
    nqt = s // QB
    nc = s // NSA_CMP_STRIDE
    n_cmp = (s - NSA_CMP_LEN) // NSA_CMP_STRIDE + 1
    n_sel = s // NSA_SEL_LEN
    sel_k = min(NSA_SEL_TOPK, n_sel)
    n_kt = s // NSA_KT
    n_ct = nc // LANES
    assert s % (LANES * NSA_CMP_STRIDE) == 0 and n_sel <= LANES and NSA_KT == 2 * QB

    def by_group(t):
        return t.reshape(bs, s, G, DH).transpose(0, 2, 1, 3)

    def by_group_t(t):
        return t.reshape(bs, s, G, DH).transpose(0, 2, 3, 1).astype(BF16)

    ones2 = jnp.zeros((LANES - DH,), F32).at[:2].set(1.0)

    def keys_aug(t):
        return jnp.concatenate([t, jnp.broadcast_to(ones2, t.shape[:-1] + (LANES - DH,))],
                               axis=-1).astype(BF16)

    chunks = jnp.stack([by_group(k_c), by_group(v_c)]).reshape(2, bs, G, nc, NSA_CMP_STRIDE * DH)
    pe = jnp.stack([ck_pe, cv_pe]).reshape(2, 1, NSA_CMP_LEN * DH)
    cmp = nsa_compress(chunks.astype(BF16), jnp.broadcast_to(pe, (2, 8, NSA_CMP_LEN * DH)).astype(BF16),
                       jnp.stack([ck_w1, cv_w1]).astype(BF16), jnp.stack([ck_w2, cv_w2]).astype(BF16))
    kc, vc_t = keys_aug(cmp[0]), cmp[1].transpose(0, 1, 3, 2).astype(BF16)

    far = rel_bias[REL_BUCKETS - 1].astype(F32)
    far_hi = far.astype(BF16).astype(F32)
    extra = jnp.zeros((NSA_HEADS, LANES - DH), F32).at[:, 0].set(far_hi).at[:, 1].set(far - far_hi)
    qh = (q * (DH ** -0.5)).reshape(bs, nqt, QB, G, HG, DH).transpose(0, 3, 1, 4, 2, 5)
    ex = jnp.broadcast_to(extra.reshape(1, G, 1, HG, 1, LANES - DH), (bs, G, nqt, HG, QB, LANES - DH))
    qs = jnp.concatenate([qh, ex], axis=-1).astype(BF16).reshape(bs, G, nqt, HG * QB, LANES)
    gl = gate_logits.reshape(bs, nqt, QB, G, HG, 3).transpose(0, 3, 1, 5, 4, 2)
    gl = jnp.pad(gl.reshape(bs, G, nqt, 3, HG * QB), ((0, 0),) * 3 + ((0, 5), (0, 0)))

    td = (rel_bias[_bucket_table()] - rel_bias[REL_BUCKETS - 1][None, :]).T.astype(F32)
    cmp_end = jnp.minimum(jnp.arange(nc) * NSA_CMP_STRIDE + NSA_CMP_LEN - 1, s - 1)
    cpos = positions[cmp_end]
    qmin = positions.reshape(nqt, QB).min(axis=1)
    kmax = positions.reshape(nqt, QB).max(axis=1)
    near_s = (qmin[:, None] - positions.reshape(n_kt, NSA_KT).max(axis=1)[None, :]) < REL_MAX_DIST
    n_past = (jnp.arange(nqt) * QB + QB - 1) // NSA_KT
    n_fast = jnp.minimum(jnp.argmax(jnp.concatenate([near_s, jnp.ones((nqt, 1), bool)], axis=1), axis=1),
                         n_past)
    wt = jnp.clip(jnp.arange(nqt)[:, None] - (NSA_WTILES - 1) + jnp.arange(NSA_WTILES)[None, :], 0, nqt - 1)
    near_w = (qmin[:, None] - kmax[wt]) < REL_MAX_DIST
    near_c = (qmin[:, None] - cpos.reshape(n_ct, LANES).max(axis=1)[None, :]) < REL_MAX_DIST
    w_std = ((jnp.arange(nqt) >= NSA_WTILES - 1)
             & jnp.all(near_w == (jnp.arange(NSA_WTILES) >= NSA_WTILES - 2)[None, :], axis=1))

    cmp_start = np.arange(nc) * NSA_CMP_STRIDE
    sel_start = np.arange(LANES) * NSA_SEL_LEN
    agg_t = ((cmp_start[None, :] <= sel_start[:, None] + NSA_SEL_LEN - 1)
             & (cmp_start[None, :] + NSA_CMP_LEN - 1 >= sel_start[:, None])
             & (np.arange(nc)[None, :] < n_cmp) & (np.arange(LANES)[:, None] < n_sel))

    cols = HG * QB
    full = lambda shape: pl.BlockSpec(shape, lambda b, i, *_: (0,) * len(shape))
    per_b = lambda n, w: pl.BlockSpec((1, G, n, w), lambda b, i, *_: (b, 0, 0, 0))
    per_tile = lambda n, w: pl.BlockSpec((1, G, 1, n, w), lambda b, i, *_: (b, 0, i, 0, 0))
    grid_spec = pltpu.PrefetchScalarGridSpec(
        num_scalar_prefetch=4,
        grid=(bs, nqt),
        in_specs=[per_tile(cols, LANES), per_tile(8, cols),
                  pl.BlockSpec((QB, LANES), lambda b, i, *_: (i, 0)),
                  full((s, LANES)), full((nc, LANES)),
                  per_b(s, LANES), per_b(DH, s), per_b(s, LANES), per_b(DH, s),
                  per_b(nc, LANES), per_b(DH, nc),
                  full((LANES, nc)), full((NSA_HEADS, LANES))],
        out_specs=per_tile(DH, cols),
        scratch_shapes=[pltpu.VMEM((G, 2, 1, cols), F32), pltpu.VMEM((G, 2, 1, cols), F32),
                        pltpu.VMEM((G, 2, DH, cols), F32), pltpu.VMEM((G, LANES, QB), F32)],
    )
    out = pl.pallas_call(
        functools.partial(_nsa_body, n_sel=n_sel, sel_k=sel_k),
        grid_spec=grid_spec,
        out_shape=jax.ShapeDtypeStruct((bs, G, nqt, DH, cols), F32),
        compiler_params=_params("parallel", "arbitrary"),
        name="nsa_attention",
    )(n_fast.astype(jnp.int32), w_std.astype(jnp.int32), near_w.reshape(-1).astype(jnp.int32),
      near_c.reshape(-1).astype(jnp.int32),
      qs, gl, _pos_digits(positions, False), _pos_digits(positions, True), _pos_digits(cpos, True),
      keys_aug(by_group(k_s)), by_group_t(v_s), keys_aug(by_group(k_w)), by_group_t(v_w),
      kc, vc_t, jnp.asarray(agg_t, BF16), td)
    out = out.reshape(bs, G, nqt, DH, HG, QB).transpose(0, 2, 5, 1, 4, 3)
    return out.reshape(bs, s, G * HG * DH)


SGU_TC = 512


def _sgu_body(uv_ref, lg_ref, lb_ref, w_ref, b_ref, o_ref):
    uv = jax.nn.gelu(uv_ref[...])
    u, v = uv[:, :SGU_WIDTH], uv[:, SGU_WIDTH:]
    mu = jnp.mean(v, axis=-1, keepdims=True)
    vc = v - mu
    var = jnp.mean(vc * vc, axis=-1, keepdims=True)
    vn = (vc * lax.rsqrt(var + RMS_EPS) * lg_ref[...] + lb_ref[...]).astype(BF16)
    gw = SGU_WIDTH // SGU_GROUPS
    for c in range(SGU_TC // SGU_CHUNK):
        rows = slice(c * SGU_CHUNK, (c + 1) * SGU_CHUNK)
        for g in range(SGU_GROUPS):
            cols = slice(g * gw, (g + 1) * gw)
            mix = jnp.dot(w_ref[g], vn[rows, cols], preferred_element_type=F32) + b_ref[:, g:g + 1]
            o_ref[rows, cols] = u[rows, cols] * mix


def sgu_mixer(proj, ln_g, ln_b, w_s, b_s):
    t = proj.shape[0]
    assert t % SGU_TC == 0 and OFF_SGU % (2 * SGU_WIDTH) == 0
    fixed = lambda i: (0, 0)
    return pl.pallas_call(
        _sgu_body,
        grid=(t // SGU_TC,),
        in_specs=[pl.BlockSpec((SGU_TC, 2 * SGU_WIDTH), lambda i: (i, OFF_SGU // (2 * SGU_WIDTH))),
                  pl.BlockSpec((1, SGU_WIDTH), fixed), pl.BlockSpec((1, SGU_WIDTH), fixed),
                  pl.BlockSpec((SGU_GROUPS, SGU_CHUNK, SGU_CHUNK), lambda i: (0, 0, 0)),
                  pl.BlockSpec((SGU_CHUNK, SGU_GROUPS), fixed)],
        out_specs=pl.BlockSpec((SGU_TC, SGU_WIDTH), lambda i: (i, 0)),
        out_shape=jax.ShapeDtypeStruct((t, SGU_WIDTH), F32),
        compiler_params=_params("parallel"),
        name="sgu_gate",
    )(proj, ln_g.reshape(1, SGU_WIDTH), ln_b.reshape(1, SGU_WIDTH), jnp.tril(w_s).astype(BF16), b_s.T)


def _permute_w_in(w):
    sizes = ([SSM_WIDTH, GDN_QKV, GDN_HEADS, GDN_HEADS, GDN_HEADS * GDN_DV, NSA_Q]
             + [NSA_KV] * 6 + [3 * NSA_HEADS, 2 * SGU_WIDTH, N_BRANCH * D_MODEL])
    cuts = [int(c) for c in np.cumsum(sizes)[:-1]]
    (w_ssm, w_qkv, w_a, w_b, w_z, w_nq, w_kc, w_vc, w_ks, w_vs, w_kw, w_vw,
     w_ng, w_sgu, w_gate) = jnp.split(w, cuts, axis=-1)
    pad = jnp.zeros((w.shape[0], PROJ_COLS - OFF_SMALL - SMALL_USED), w.dtype)
    return jnp.concatenate([w_qkv, w_ssm, w_z, w_nq, w_sgu, w_gate, w_kc, w_vc, w_ks, w_vs,
                            w_kw, w_vw, w_a, w_b, w_ng, pad], axis=-1)


def kernel(x, positions, norm_mix, norm_ffn, norm_final, w_in, ssm_a_re, ssm_a_im, ssm_log_dt, ssm_b_re, ssm_b_im, ssm_c_re, ssm_c_im, ssm_d, ssm_glu_w, gdn_conv_w, gdn_a_log, gdn_dt_bias, gdn_norm, nsa_cmp_k_pe, nsa_cmp_k_w1, nsa_cmp_k_w2, nsa_cmp_v_pe, nsa_cmp_v_w1, nsa_cmp_v_w2, rel_bias, sgu_ln_g, sgu_ln_b, sgu_w, sgu_b, w_br_ssm, w_br_gdn, w_br_nsa, w_br_sgu, w_out, ffn_w_gate, ffn_w_up, ffn_w_down, moe_router, moe_w_gate, moe_w_up, moe_w_down):
    bs, s, d = x.shape
    t = bs * s
    assert DEPTH == 2
    xf = x.reshape(t, d)
    for layer in range(DEPTH):
        proj = in_proj(xf, norm_mix[layer], _permute_w_in(w_in[layer]))
        p3 = proj.reshape(bs, s, PROJ_COLS)

        def seg(off, width):
            return p3[..., off:off + width]

        y_ssm = ssm_mixer(seg(OFF_SSM, SSM_WIDTH), ssm_a_re[layer], ssm_a_im[layer],
                          ssm_log_dt[layer], ssm_b_re[layer], ssm_b_im[layer], ssm_c_re[layer],
                          ssm_c_im[layer], ssm_d[layer], ssm_glu_w[layer])
        y_gdn = gdn_mixer(proj, bs, gdn_conv_w[layer], gdn_a_log[layer], gdn_dt_bias[layer],
                          gdn_norm[layer])
        kvs = [seg(OFF_KV + i * NSA_KV, NSA_KV) for i in range(6)]
        y_nsa = nsa_mixer(seg(OFF_NQ, NSA_Q), *kvs, seg(OFF_SMALL + 2 * GDN_HEADS, 3 * NSA_HEADS),
                          positions, nsa_cmp_k_pe[layer], nsa_cmp_k_w1[layer], nsa_cmp_k_w2[layer],
                          nsa_cmp_v_pe[layer], nsa_cmp_v_w1[layer], nsa_cmp_v_w2[layer], rel_bias)
        y_sgu = sgu_mixer(proj, sgu_ln_g[layer], sgu_ln_b[layer], sgu_w[layer], sgu_b[layer])
        ys = [y_ssm.reshape(t, -1), y_gdn, y_nsa.reshape(t, -1), y_sgu]
        ws = [w[layer].astype(BF16) for w in (w_br_ssm, w_br_gdn, w_br_nsa, w_br_sgu)]
        xf = merge(xf, proj, ys, ws, w_out[layer].astype(BF16))
        i = layer // 2
        if layer % 2 == 0:
            xf = dense_ffn(xf, norm_ffn[layer], ffn_w_gate[i], ffn_w_up[i], ffn_w_down[i])
        else:
            xf = moe_layer(xf, norm_ffn[layer], moe_router[i], moe_w_gate[i], moe_w_up[i],
                           moe_w_down[i], norm_final)
    return xf.reshape(bs, s, d)
```

```python
import functools
import math

import jax
import jax.numpy as jnp
from jax import lax
import numpy as np
from jax.experimental import pallas as pl
from jax.experimental.pallas import tpu as pltpu

F32 = jnp.float32
BF16 = jnp.bfloat16

D_MODEL = 1024
DEPTH = 2
SSM_WIDTH = D_MODEL // 2
SSM_GROUP = 16
SSM_GROUPS = SSM_WIDTH // SSM_GROUP
SSM_STATE = 64
GDN_HEADS = 4
GDN_DK = 128
GDN_DV = 128
GDN_CONV = 4
GDN_CHUNK = 64
NSA_HEADS = 8
NSA_KV_GROUPS = 2
NSA_HPG = NSA_HEADS // NSA_KV_GROUPS
NSA_DH = 64
NSA_CMP_LEN = 32
NSA_CMP_STRIDE = 16
NSA_CMP_HIDDEN = 128
NSA_SEL_LEN = 64
NSA_SEL_TOPK = 16
NSA_WINDOW = 512
NSA_QBLOCK = 128
SGU_WIDTH = D_MODEL // 2
SGU_GROUPS = 4
SGU_CHUNK = 128
REL_BUCKETS = 32
REL_MAX_DIST = 128
FFN_DENSE = 2816
N_EXPERTS = 8
TOP_K = 2
FFN_EXPERT = 3584
N_BRANCH = 4
RMS_EPS = 1e-6
GDN_QKV = GDN_HEADS * (2 * GDN_DK + GDN_DV)
NSA_Q = NSA_HEADS * NSA_DH
NSA_KV = NSA_KV_GROUPS * NSA_DH

LANES = 128
VMEM_LIMIT = 48 * 1024 * 1024
MASKED = -1e30

OFF_QKV = 0
OFF_SSM = OFF_QKV + GDN_QKV
OFF_Z = OFF_SSM + SSM_WIDTH
OFF_NQ = OFF_Z + GDN_HEADS * GDN_DV
OFF_SGU = OFF_NQ + NSA_Q
OFF_GATE = OFF_SGU + 2 * SGU_WIDTH
OFF_KV = OFF_GATE + N_BRANCH * D_MODEL
OFF_SMALL = OFF_KV + 6 * NSA_KV
SMALL_USED = 2 * GDN_HEADS + 3 * NSA_HEADS
PROJ_COLS = 9216


def _params(*sem):
    return pltpu.CompilerParams(dimension_semantics=sem, vmem_limit_bytes=VMEM_LIMIT)


def _rms(x, g):
    return x * lax.rsqrt(jnp.mean(x * x, axis=-1, keepdims=True) + RMS_EPS) * g


def _dot_nt(a, b):
    return lax.dot_general(a, b, (((1,), (1,)), ((), ())), preferred_element_type=F32)


def _dot_tn(a, b):
    return lax.dot_general(a, b, (((0,), (0,)), ((), ())), preferred_element_type=F32)


def _split_bf16(x):
    hi = x.astype(BF16)
    return hi, (x - hi.astype(F32)).astype(BF16)


def _in_proj_body(x_ref, g_ref, w_ref, o_ref, h_ref):
    @pl.when(pl.program_id(1) == 0)
    def _():
        h_ref[...] = _rms(x_ref[...], g_ref[...]).astype(BF16)

    o_ref[...] = jnp.dot(h_ref[...], w_ref[...].astype(BF16), preferred_element_type=F32)


def in_proj(x, g, w, tm=2048, tn=512):
    t, d = x.shape
    n = w.shape[1]
    return pl.pallas_call(
        _in_proj_body,
        grid=(t // tm, n // tn),
        in_specs=[pl.BlockSpec((tm, d), lambda i, j: (i, 0)),
                  pl.BlockSpec((1, d), lambda i, j: (0, 0)),
                  pl.BlockSpec((d, tn), lambda i, j: (0, j))],
        out_specs=pl.BlockSpec((tm, tn), lambda i, j: (i, j)),
        out_shape=jax.ShapeDtypeStruct((t, n), F32),
        scratch_shapes=[pltpu.VMEM((tm, d), BF16)],
        compiler_params=_params("parallel", "arbitrary"),
        name="in_proj",
    )(x, g.reshape(1, d), w)


def _merge_body(x_ref, gate_ref, ya_ref, yb_ref, yc_ref, yd_ref,
                wa_ref, wb_ref, wc_ref, wd_ref, wo_ref, o_ref):
    def branch(k, y_ref, w_ref):
        p = jnp.dot(y_ref[...].astype(BF16), w_ref[...], preferred_element_type=F32)
        return jax.nn.sigmoid(gate_ref[:, k * D_MODEL:(k + 1) * D_MODEL]) * p

    merged = (branch(0, ya_ref, wa_ref) + branch(1, yb_ref, wb_ref)
              + branch(2, yc_ref, wc_ref) + branch(3, yd_ref, wd_ref))
    o_ref[...] = x_ref[...] + jnp.dot(merged.astype(BF16), wo_ref[...],
                                      preferred_element_type=F32)


def merge(x, proj, ys, ws, w_out, tm=256):
    t, d = x.shape
    gate_blk = OFF_GATE // (N_BRANCH * D_MODEL)
    row = lambda i: (i, 0)
    fixed = lambda i: (0, 0)
    in_specs = [pl.BlockSpec((tm, d), row),
                pl.BlockSpec((tm, N_BRANCH * D_MODEL), lambda i: (i, gate_blk))]
    in_specs += [pl.BlockSpec((tm, y.shape[1]), row) for y in ys]
    in_specs += [pl.BlockSpec(w.shape, fixed) for w in ws]
    in_specs += [pl.BlockSpec(w_out.shape, fixed)]
    return pl.pallas_call(
        _merge_body,
        grid=(t // tm,),
        in_specs=in_specs,
        out_specs=pl.BlockSpec((tm, d), row),
        out_shape=jax.ShapeDtypeStruct((t, d), F32),
        compiler_params=_params("parallel"),
        name="merge",
    )(x, proj, *ys, *ws, w_out)


def _ffn_body(x_ref, g_ref, wg_ref, wu_ref, wd_ref, o_ref, h_ref, acc_ref):
    f = pl.program_id(1)

    @pl.when(f == 0)
    def _():
        h_ref[...] = _rms(x_ref[...], g_ref[...]).astype(BF16)
        acc_ref[...] = x_ref[...]

    h = h_ref[...]
    a = jnp.dot(h, wg_ref[...].astype(BF16), preferred_element_type=F32)
    u = jnp.dot(h, wu_ref[...].astype(BF16), preferred_element_type=F32)
    act = (a * jax.nn.sigmoid(a) * u).astype(BF16)
    acc_ref[...] += jnp.dot(act, wd_ref[...].astype(BF16), preferred_element_type=F32)

    @pl.when(f == pl.num_programs(1) - 1)
    def _():
        o_ref[...] = acc_ref[...]


def dense_ffn(x, g, wg, wu, wd, tm=1024, tf=256):
    t, d = x.shape
    nf = wg.shape[1]
    return pl.pallas_call(
        _ffn_body,
        grid=(t // tm, nf // tf),
        in_specs=[pl.BlockSpec((tm, d), lambda i, f: (i, 0)),
                  pl.BlockSpec((1, d), lambda i, f: (0, 0)),
                  pl.BlockSpec((d, tf), lambda i, f: (0, f)),
                  pl.BlockSpec((d, tf), lambda i, f: (0, f)),
                  pl.BlockSpec((tf, d), lambda i, f: (f, 0))],
        out_specs=pl.BlockSpec((tm, d), lambda i, f: (i, 0)),
        out_shape=jax.ShapeDtypeStruct((t, d), F32),
        scratch_shapes=[pltpu.VMEM((tm, d), BF16), pltpu.VMEM((tm, d), F32)],
        compiler_params=_params("parallel", "arbitrary"),
        name="dense_ffn",
    )(x, g.reshape(1, d), wg, wu, wd)


def _route_body(x_ref, g_ref, rhi_ref, rlo_ref, h_ref, idx_ref, wt_ref):
    h = _rms(x_ref[...], g_ref[...])
    hi = h.astype(BF16)
    lo = (h - hi.astype(F32)).astype(BF16)
    h_ref[...] = h
    logits = (jnp.dot(hi, rhi_ref[...], preferred_element_type=F32)
              + jnp.dot(hi, rlo_ref[...], preferred_element_type=F32)
              + jnp.dot(lo, rhi_ref[...], preferred_element_type=F32))
    col = lax.broadcasted_iota(jnp.int32, logits.shape, 1).astype(F32)
    neg = jnp.float32(-jnp.inf)
    lg = jnp.where(col < N_EXPERTS, logits, neg)
    m1 = jnp.max(lg, axis=-1, keepdims=True)
    i1 = jnp.min(jnp.where(lg == m1, col, float(LANES)), axis=-1, keepdims=True)
    lg2 = jnp.where(col == i1, neg, lg)
    m2 = jnp.max(lg2, axis=-1, keepdims=True)
    i2 = jnp.min(jnp.where(lg2 == m2, col, float(LANES)), axis=-1, keepdims=True)
    e = jnp.exp(m2 - m1)
    w1 = 1.0 / (1.0 + e)
    w2 = e / (1.0 + e)
    idx_ref[...] = jnp.where(col == 0, i1, jnp.where(col == 1, i2, 0.0)).astype(jnp.int32)
    wt_ref[...] = jnp.where(col == 0, w1, jnp.where(col == 1, w2, 0.0))


def moe_route(x, g, r_hi, r_lo, tm=512):
    t, d = x.shape
    row = lambda i: (i, 0)
    fixed = lambda i: (0, 0)
    return pl.pallas_call(
        _route_body,
        grid=(t // tm,),
        in_specs=[pl.BlockSpec((tm, d), row), pl.BlockSpec((1, d), fixed),
                  pl.BlockSpec((d, LANES), fixed), pl.BlockSpec((d, LANES), fixed)],
        out_specs=[pl.BlockSpec((tm, d), row), pl.BlockSpec((tm, LANES), row),
                   pl.BlockSpec((tm, LANES), row)],
        out_shape=[jax.ShapeDtypeStruct((t, d), F32),
                   jax.ShapeDtypeStruct((t, LANES), jnp.int32),
                   jax.ShapeDtypeStruct((t, LANES), F32)],
        compiler_params=_params("parallel"),
        name="moe_route",
    )(x, g.reshape(1, d), r_hi, r_lo)


def _experts_body(te_ref, nu_ref, src_ref, h_hbm, rw_ref, wg_ref, wu_ref, wd_ref, o_ref,
                  xbuf_ref, xs_ref, acc_ref, sem):
    i = pl.program_id(0)
    f = pl.program_id(1)
    tm = xs_ref.shape[0]
    n_used = nu_ref[0]
    used = i < n_used

    def gather(tile, slot):
        def row(r, carry):
            tok = src_ref[tile * tm + r]
            pltpu.make_async_copy(h_hbm.at[pl.ds(tok, 1), :], xbuf_ref.at[slot, pl.ds(r, 1), :],
                                  sem.at[slot]).start()
            return carry

        lax.fori_loop(0, tm, row, 0, unroll=8)

    @pl.when(f == 0)
    def _():
        acc_ref[...] = jnp.zeros_like(acc_ref)

        @pl.when(used)
        def _():
            slot = i % 2

            @pl.when(i == 0)
            def _():
                gather(0, 0)

            pltpu.make_async_copy(h_hbm.at[pl.ds(0, tm), :], xbuf_ref.at[slot], sem.at[slot]).wait()

            @pl.when(i + 1 < n_used)
            def _():
                gather(i + 1, 1 - slot)

            xs_ref[...] = xbuf_ref[slot].astype(BF16)

    @pl.when(used)
    def _():
        h = xs_ref[...]
        a = jnp.dot(h, wg_ref[0].astype(BF16), preferred_element_type=F32)
        u = jnp.dot(h, wu_ref[0].astype(BF16), preferred_element_type=F32)
        act = (a * jax.nn.sigmoid(a) * u).astype(BF16)
        acc_ref[...] += jnp.dot(act, wd_ref[0].astype(BF16), preferred_element_type=F32)

    @pl.when(f == pl.num_programs(1) - 1)
    def _():
        o_ref[...] = (acc_ref[...] * rw_ref[...]).astype(o_ref.dtype)


def moe_experts(tile_expert, n_used, src, h, row_w, wg, wu, wd, tm, tf=512):
    p = src.shape[0]
    d = h.shape[1]
    nf = wg.shape[2]
    grid_spec = pltpu.PrefetchScalarGridSpec(
        num_scalar_prefetch=3,
        grid=(p // tm, nf // tf),
        in_specs=[pl.BlockSpec(memory_space=pl.ANY),
                  pl.BlockSpec((tm, 1), lambda i, f, *_: (i, 0)),
                  pl.BlockSpec((1, d, tf), lambda i, f, te, *_: (te[i], 0, f)),
                  pl.BlockSpec((1, d, tf), lambda i, f, te, *_: (te[i], 0, f)),
                  pl.BlockSpec((1, tf, d), lambda i, f, te, *_: (te[i], f, 0))],
        out_specs=pl.BlockSpec((tm, d), lambda i, f, *_: (i, 0)),
        scratch_shapes=[pltpu.VMEM((2, tm, d), F32), pltpu.VMEM((tm, d), BF16),
                        pltpu.VMEM((tm, d), F32), pltpu.SemaphoreType.DMA((2,))],
    )
    return pl.pallas_call(
        _experts_body,
        grid_spec=grid_spec,
        out_shape=jax.ShapeDtypeStruct((p, d), BF16),
        compiler_params=_params("arbitrary", "arbitrary"),
        name="moe_experts",
    )(tile_expert, n_used, src, h, row_w, wg, wu, wd)


def _combine_norm_body(x_ref, ya_ref, yb_ref, g_ref, o_ref):
    o_ref[...] = _rms(x_ref[...] + ya_ref[...].astype(F32) + yb_ref[...].astype(F32), g_ref[...])


def combine_norm(x, ya, yb, g, tm=1024):
    t, d = x.shape
    row = lambda i: (i, 0)
    return pl.pallas_call(
        _combine_norm_body,
        grid=(t // tm,),
        in_specs=[pl.BlockSpec((tm, d), row)] * 3 + [pl.BlockSpec((1, d), lambda i: (0, 0))],
        out_specs=pl.BlockSpec((tm, d), row),
        out_shape=jax.ShapeDtypeStruct((t, d), F32),
        compiler_params=_params("parallel"),
        name="combine_norm",
    )(x, ya, yb, g.reshape(1, d))


def moe_layer(x, g_norm, router, wg, wu, wd, g_final, tm=1024):
    t, d = x.shape
    r = jnp.zeros((d, LANES), F32).at[:, :N_EXPERTS].set(router)
    r_hi = r.astype(BF16)
    r_lo = (r - r_hi.astype(F32)).astype(BF16)
    h, idx, wts = moe_route(x, g_norm, r_hi, r_lo)
    e_flat = jnp.concatenate([idx[:, 0], idx[:, 1]])
    w_flat = jnp.concatenate([wts[:, 0], wts[:, 1]])
    onehot = (e_flat[:, None] == jnp.arange(N_EXPERTS)[None, :]).astype(jnp.int32)
    csum = jnp.cumsum(onehot, axis=0)
    rank = jnp.sum((csum - onehot) * onehot, axis=1)
    counts = csum[-1]
    padded = ((counts + tm - 1) // tm) * tm
    ends = jnp.cumsum(padded)
    starts = ends - padded
    dest = starts[e_flat] + rank
    n_tiles = (TOP_K * t) // tm + N_EXPERTS
    p = n_tiles * tm
    tok = jnp.concatenate([jnp.arange(t, dtype=jnp.int32)] * TOP_K)
    src = jnp.zeros((p,), jnp.int32).at[dest].set(tok)
    row_w = jnp.zeros((p,), F32).at[dest].set(w_flat)
    tile_expert = jnp.minimum(
        jnp.searchsorted(ends, jnp.arange(n_tiles, dtype=jnp.int32) * tm, side="right"),
        N_EXPERTS - 1).astype(jnp.int32)
    n_used = (ends[-1] // tm).astype(jnp.int32).reshape(1)
    ys = moe_experts(tile_expert, n_used, src, h, row_w.reshape(p, 1), wg, wu, wd, tm)
    ya = jnp.take(ys, dest[:t], axis=0)
    yb = jnp.take(ys, dest[t:], axis=0)
    return combine_norm(x, ya, yb, g_final)


SSM_TC = 512
SSM_SUB = 128
SSM_HALF = 256
SSM_NSTATE = SSM_GROUPS * SSM_STATE


def _ssm_body(u_ref, bw_ref, cw_ref, d_ref, glu_ref, ar_ref, ai_ref, pr_ref, pi_ref, o_ref,
              xr_ref, xi_ref, cr_ref, ci_ref):
    @pl.when(pl.program_id(1) == 0)
    def _():
        cr_ref[...] = jnp.zeros_like(cr_ref)
        ci_ref[...] = jnp.zeros_like(ci_ref)

    u = u_ref[0]
    ub = u.astype(BF16)
    nblk = SSM_WIDTH // SSM_HALF
    sblk = SSM_NSTATE // nblk
    for k in range(nblk):
        bu = jnp.dot(ub[:, k * SSM_HALF:(k + 1) * SSM_HALF], bw_ref[k], preferred_element_type=F32)
        xr_ref[:, k * sblk:(k + 1) * sblk] = bu[:, :sblk]
        xi_ref[:, k * sblk:(k + 1) * sblk] = bu[:, sblk:]

    row8 = lax.broadcasted_iota(jnp.int32, (SSM_SUB, LANES), 0) % 8
    n_grp = SSM_SUB // 8

    def strip(c, carry):
        col = pl.ds(pl.multiple_of(c * LANES, LANES), LANES)
        c_r = cr_ref[:, col]
        c_i = ci_ref[:, col]
        p_r = pr_ref[0:8, col]
        p_i = pi_ref[0:8, col]
        for sub in range(SSM_TC // SSM_SUB):
            rows = slice(sub * SSM_SUB, (sub + 1) * SSM_SUB)
            xr = xr_ref[rows, col]
            xi = xi_ref[rows, col]
            for i in range(3):
                d = 1 << i
                a_r = ar_ref[i:i + 1, col]
                a_i = ai_ref[i:i + 1, col]
                sr = jnp.where(row8 >= d, pltpu.roll(xr, d, 0), 0.0)
                si = jnp.where(row8 >= d, pltpu.roll(xi, d, 0), 0.0)
                xr, xi = xr + a_r * sr - a_i * si, xi + a_r * si + a_i * sr
            out_r, out_i = [], []
            for r in range(n_grp):
                g_r = xr[8 * r:8 * r + 8] + p_r * c_r - p_i * c_i
                g_i = xi[8 * r:8 * r + 8] + p_r * c_i + p_i * c_r
                c_r, c_i = g_r[7:8, :], g_i[7:8, :]
                out_r.append(g_r)
                out_i.append(g_i)
            xr_ref[rows, col] = jnp.concatenate(out_r, axis=0)
            xi_ref[rows, col] = jnp.concatenate(out_i, axis=0)
        cr_ref[:, col] = c_r
        ci_ref[:, col] = c_i
        return carry

    lax.fori_loop(0, SSM_NSTATE // LANES, strip, 0)

    ys = []
    for k in range(nblk):
        st = jnp.concatenate([xr_ref[:, k * sblk:(k + 1) * sblk].astype(BF16),
                              xi_ref[:, k * sblk:(k + 1) * sblk].astype(BF16)], axis=1)
        ys.append(jnp.dot(st, cw_ref[k], preferred_element_type=F32))
    y = jax.nn.gelu(jnp.concatenate(ys, axis=1) + d_ref[...] * u)
    o_ref[0] = y * jax.nn.sigmoid(jnp.dot(y.astype(BF16), glu_ref[...], preferred_element_type=F32))


def ssm_mixer(u, a_re, a_im, log_dt, b_re, b_im, c_re, c_im, d_skip, glu_w):
    bs, s, _ = u.shape
    assert s % SSM_TC == 0
    nblk = SSM_WIDTH // SSM_HALF
    gpb = SSM_GROUPS // nblk
    dt = jnp.exp(log_dt)[:, None]
    mag = jnp.exp(a_re * dt)
    abar_r, abar_i = mag * jnp.cos(a_im * dt), mag * jnp.sin(a_im * dt)
    nr, ni = abar_r - 1.0, abar_i
    den = a_re * a_re + a_im * a_im
    sr, si = (nr * a_re + ni * a_im) / den, (ni * a_re - nr * a_im) / den
    bbar_r = sr[..., None] * b_re - si[..., None] * b_im
    bbar_i = sr[..., None] * b_im + si[..., None] * b_re

    def powers(k):
        kk = k.astype(F32)[:, None, None]
        m = jnp.exp(kk * (a_re * dt))
        return ((m * jnp.cos(kk * (a_im * dt))).reshape(-1, SSM_NSTATE),
                (m * jnp.sin(kk * (a_im * dt))).reshape(-1, SSM_NSTATE))

    ar, ai = powers(2 ** jnp.arange(8))
    pr, pi = powers(jnp.arange(1, SSM_SUB + 1))
    eye = jnp.eye(gpb, dtype=F32)

    def in_block(w):
        return jnp.einsum('gnp,gh->gphn', w, eye).reshape(gpb * SSM_GROUP, gpb * SSM_STATE)

    def out_block(w):
        return jnp.einsum('gpn,gh->gnhp', w, eye).reshape(gpb * SSM_STATE, gpb * SSM_GROUP)

    bw = jnp.stack([jnp.concatenate([in_block(bbar_r[k * gpb:(k + 1) * gpb]),
                                     in_block(bbar_i[k * gpb:(k + 1) * gpb])], axis=1)
                    for k in range(nblk)]).astype(BF16)
    cw = jnp.stack([jnp.concatenate([out_block(c_re[k * gpb:(k + 1) * gpb]),
                                     -out_block(c_im[k * gpb:(k + 1) * gpb])], axis=0)
                    for k in range(nblk)]).astype(BF16)
    fixed2 = lambda b, i: (0, 0)
    fixed3 = lambda b, i: (0, 0, 0)
    return pl.pallas_call(
        _ssm_body,
        grid=(bs, s // SSM_TC),
        in_specs=[pl.BlockSpec((1, SSM_TC, SSM_WIDTH), lambda b, i: (b, i, 0)),
                  pl.BlockSpec(bw.shape, fixed3), pl.BlockSpec(cw.shape, fixed3),
                  pl.BlockSpec((1, SSM_WIDTH), fixed2), pl.BlockSpec((SSM_WIDTH, SSM_WIDTH), fixed2),
                  pl.BlockSpec((8, SSM_NSTATE), fixed2), pl.BlockSpec((8, SSM_NSTATE), fixed2),
                  pl.BlockSpec((SSM_SUB, SSM_NSTATE), fixed2), pl.BlockSpec((SSM_SUB, SSM_NSTATE), fixed2)],
        out_specs=pl.BlockSpec((1, SSM_TC, SSM_WIDTH), lambda b, i: (b, i, 0)),
        out_shape=jax.ShapeDtypeStruct((bs, s, SSM_WIDTH), F32),
        scratch_shapes=[pltpu.VMEM((SSM_TC, SSM_NSTATE), F32), pltpu.VMEM((SSM_TC, SSM_NSTATE), F32),
                        pltpu.VMEM((1, SSM_NSTATE), F32), pltpu.VMEM((1, SSM_NSTATE), F32)],
        compiler_params=_params("parallel", "arbitrary"),
        name="ssm_scan",
    )(u, bw, cw, d_skip.reshape(1, SSM_WIDTH), glu_w.astype(BF16), ar, ai, pr, pi)


GDN_TC = 256
GDN_HALO = 8


def _gdn_body(nega_ref, dtb_ref, q_ref, k_ref, v_ref, z_ref, sm_ref, wq_ref, wk_ref, wv_ref, ng_ref,
              o_ref, state_ref, hq_ref, hk_ref, hv_ref, vnew_ref):
    TC, CH, DK, H = GDN_TC, GDN_CHUNK, GDN_DK, GDN_HEADS
    heads = range(H)

    @pl.when(pl.program_id(1) == 0)
    def _():
        state_ref[...] = jnp.zeros_like(state_ref)
        hq_ref[...] = jnp.zeros_like(hq_ref)
        hk_ref[...] = jnp.zeros_like(hk_ref)
        hv_ref[...] = jnp.zeros_like(hv_ref)

    def conv_silu(x_ref, halo_ref, w_ref):
        x = x_ref[...]
        xx = jnp.concatenate([halo_ref[...], x], axis=0)
        w = w_ref[...]
        y = w[GDN_CONV - 1:GDN_CONV] * x
        for d in range(1, GDN_CONV):
            y = y + w[GDN_CONV - 1 - d:GDN_CONV - d] * pltpu.roll(xx, d, 0)[GDN_HALO:GDN_HALO + TC]
        halo_ref[...] = x[TC - GDN_HALO:TC]
        return y * jax.nn.sigmoid(y)

    def per_head(x):
        return [x[:, h * LANES:(h + 1) * LANES] for h in heads]

    q = per_head(conv_silu(q_ref, hq_ref, wq_ref))
    k = per_head(conv_silu(k_ref, hk_ref, wk_ref))
    v = per_head(conv_silu(v_ref, hv_ref, wv_ref))
    q = [x * lax.rsqrt(jnp.sum(x * x, axis=-1, keepdims=True) + 1e-6) * (DK ** -0.5) for x in q]
    k = [x * lax.rsqrt(jnp.sum(x * x, axis=-1, keepdims=True) + 1e-6) for x in k]

    small = sm_ref[...]
    beta = [jax.nn.sigmoid(small[:, H + h:H + h + 1]) for h in heads]
    la = []
    for h in heads:
        xa = small[:, h:h + 1] + dtb_ref[h]
        softplus = jnp.maximum(xa, 0.0) + jnp.log(1.0 + jnp.exp(-jnp.abs(xa)))
        la.append(jnp.broadcast_to(nega_ref[h] * softplus, (TC, LANES)))

    ri = lax.broadcasted_iota(jnp.int32, (TC, TC), 0)
    ci = lax.broadcasted_iota(jnp.int32, (TC, TC), 1)
    same = (ri // CH) == (ci // CH)
    causal = jnp.where(same, jnp.where(ci <= ri, 1.0, 0.0), 0.0)
    strict = jnp.where(ci < ri, causal, 0.0)
    eye = jnp.where(ri == ci, 1.0, 0.0)
    tri = causal.astype(BF16)
    lane = lax.broadcasted_iota(jnp.int32, (TC, LANES), 1)

    g = []
    for h in heads:
        la_hi, la_lo = _split_bf16(la[h])
        g.append(jnp.dot(tri, la_hi, preferred_element_type=F32)
                 + jnp.dot(tri, la_lo, preferred_element_type=F32))
    decay = []
    for h in heads:
        g_hi = g[h].astype(BF16).astype(F32)
        g_lo = g[h] - g_hi
        lhs = jnp.where(lane == 0, g_hi, jnp.where(lane == 1, g_lo, jnp.where(lane < 4, 1.0, 0.0)))
        rhs = jnp.where(lane < 2, 1.0, jnp.where(lane == 2, -g_hi, jnp.where(lane == 3, -g_lo, 0.0)))
        decay.append(jnp.exp(jnp.where(causal > 0.5, _dot_nt(lhs.astype(BF16), rhs.astype(BF16)), MASKED)))

    kb = [k[h] * beta[h] for h in heads]
    k_b = [x.astype(BF16) for x in k]
    lmat = [strict * _dot_nt(kb[h].astype(BF16), k_b[h]) * decay[h] for h in heads]
    tinv = [eye - x for x in lmat]
    pw = lmat
    for _ in range(CH.bit_length() - 2):
        pw = [jnp.dot(x.astype(BF16), x.astype(BF16), preferred_element_type=F32) for x in pw]
        tinv = [tinv[h] + jnp.dot(tinv[h].astype(BF16), pw[h].astype(BF16), preferred_element_type=F32)
                for h in heads]
    eg = [jnp.exp(x) for x in g]
    sol = [jnp.dot(tinv[h].astype(BF16),
                   jnp.concatenate([v[h] * beta[h], kb[h] * eg[h]], axis=1).astype(BF16),
                   preferred_element_type=F32) for h in heads]
    attn = [(causal * _dot_nt(q[h].astype(BF16), k_b[h]) * decay[h]).astype(BF16) for h in heads]
    q_st = [(q[h] * eg[h]).astype(BF16) for h in heads]

    vnew_ref[...] = jnp.zeros_like(vnew_ref)
    for c in range(TC // CH):
        rows = slice(c * CH, (c + 1) * CH)
        g_last = [x[(c + 1) * CH - 1:(c + 1) * CH, :] for x in g]
        st = [state_ref[h] for h in heads]
        st_b = [x.astype(BF16) for x in st]
        v_new = [sol[h][rows, :GDN_DV]
                 - jnp.dot(sol[h][rows, GDN_DV:].astype(BF16), st_b[h], preferred_element_type=F32)
                 for h in heads]
        for h in heads:
            vnew_ref[h, rows, :] = v_new[h].astype(BF16)
        o_c = [jnp.dot(q_st[h][rows], st_b[h], preferred_element_type=F32)
               + jnp.dot(attn[h][rows], vnew_ref[h], preferred_element_type=F32) for h in heads]
        for h in heads:
            k_st = (k[h][rows] * jnp.exp(g_last[h] - g[h][rows])).astype(BF16)
            state_ref[h] = st[h] * jnp.exp(g_last[h][:, :1]) + _dot_tn(k_st, v_new[h].astype(BF16))
        for h in heads:
            o = o_c[h] * lax.rsqrt(jnp.mean(o_c[h] * o_c[h], axis=-1, keepdims=True) + RMS_EPS) * ng_ref[...]
            zc = z_ref[rows, h * LANES:(h + 1) * LANES]
            o_ref[rows, h * LANES:(h + 1) * LANES] = o * (zc * jax.nn.sigmoid(zc))


def gdn_mixer(proj, bs, conv_w, a_log, dt_bias, norm_g):
    t = proj.shape[0]
    s = t // bs
    nt = s // GDN_TC
    H = GDN_HEADS
    hw = H * LANES
    assert s % GDN_TC == 0 and GDN_DK == LANES and GDN_DV == LANES and OFF_QKV == 0 and OFF_Z % hw == 0
    cw = jnp.zeros((8, GDN_QKV), F32).at[:GDN_CONV].set(conv_w)
    tok = lambda blk: pl.BlockSpec((GDN_TC, hw), lambda b, i: (b * nt + i, blk))
    wspec = lambda blk: pl.BlockSpec((8, hw), lambda b, i: (0, blk))
    smem = pl.BlockSpec(memory_space=pltpu.SMEM)
    return pl.pallas_call(
        _gdn_body,
        grid=(bs, nt),
        in_specs=[smem, smem, tok(0), tok(1), tok(2), tok(OFF_Z // hw),
                  pl.BlockSpec((GDN_TC, 2 * LANES), lambda b, i: (b * nt + i, OFF_SMALL // (2 * LANES))),
                  wspec(0), wspec(1), wspec(2), pl.BlockSpec((1, GDN_DV), lambda b, i: (0, 0))],
        out_specs=pl.BlockSpec((GDN_TC, hw), lambda b, i: (b * nt + i, 0)),
        out_shape=jax.ShapeDtypeStruct((t, hw), F32),
        scratch_shapes=[pltpu.VMEM((H, GDN_DK, GDN_DV), F32)] + [pltpu.VMEM((GDN_HALO, hw), F32)] * 3
        + [pltpu.VMEM((H, GDN_TC, GDN_DV), BF16)],
        compiler_params=_params("parallel", "arbitrary"),
        name="gdn_delta",
    )(-jnp.exp(a_log), dt_bias.astype(F32), proj, proj, proj, proj, proj, cw, cw, cw,
      norm_g.reshape(1, GDN_DV))


NSA_KT = 256
NSA_NEG = MASKED
NSA_WTILES = NSA_WINDOW // LANES + 1


def _bucket_table():
    n = np.arange(LANES)
    max_exact = REL_BUCKETS // 2
    nf = np.maximum(n, 1).astype(np.float32)
    large = max_exact + (np.log(nf / np.float32(max_exact)) / np.float32(math.log(REL_MAX_DIST / max_exact))
                         * np.float32(REL_BUCKETS - max_exact)).astype(np.int32)
    tbl = np.where(n < max_exact, n, np.minimum(large, REL_BUCKETS - 1))
    assert tbl[-1] == REL_BUCKETS - 1 and REL_MAX_DIST <= LANES
    return tbl


def _compress_body(x_ref, pe_ref, w1_ref, w2_ref, o_ref):
    x = x_ref[0, 0, 0]
    w1 = w1_ref[0]
    half = NSA_CMP_STRIDE * NSA_DH
    nc = x.shape[0]
    a = jnp.dot(x, w1[:half], preferred_element_type=F32)
    b = jnp.dot(x, w1[half:], preferred_element_type=F32)
    pe_h = jnp.dot(pe_ref[0], w1, preferred_element_type=F32)
    hid = a + pltpu.roll(b, nc - 1, 0) + pe_h[0:1, :]
    o_ref[0, 0, 0] = jnp.dot(jax.nn.gelu(hid).astype(BF16), w2_ref[0], preferred_element_type=F32)


def nsa_compress(x, pe, w1, w2):
    _, bs, g, nc, width = x.shape
    return pl.pallas_call(
        _compress_body,
        grid=(2, bs, g),
        in_specs=[pl.BlockSpec((1, 1, 1, nc, width), lambda i, b, j: (i, b, j, 0, 0)),
                  pl.BlockSpec((1,) + pe.shape[1:], lambda i, b, j: (i, 0, 0)),
                  pl.BlockSpec((1,) + w1.shape[1:], lambda i, b, j: (i, 0, 0)),
                  pl.BlockSpec((1,) + w2.shape[1:], lambda i, b, j: (i, 0, 0))],
        out_specs=pl.BlockSpec((1, 1, 1, nc, NSA_DH), lambda i, b, j: (i, b, j, 0, 0)),
        out_shape=jax.ShapeDtypeStruct((2, bs, g, nc, NSA_DH), F32),
        compiler_params=_params("parallel", "parallel", "parallel"),
        name="nsa_compress",
    )(x, pe, w1, w2)


def _nsa_body(nfast, wstd, near_w, near_c, q_ref, gl_ref, qp_ref, kp_ref, cp_ref,
              ks_ref, vs_ref, kw_ref, vw_ref, kc_ref, vc_ref, agg_ref, td_ref, o_ref,
              m_ref, l_ref, acc_ref, *, n_sel, sel_k):
    QB, HG, G = NSA_QBLOCK, NSA_HPG, NSA_KV_GROUPS
    groups = range(G)
    qi = pl.program_id(1)
    s0 = qi * QB
    nc = kc_ref.shape[2]
    n_ct = nc // LANES
    q = [q_ref[0, g, 0] for g in groups]
    qp = qp_ref[...]

    def lanes4(x):
        return jnp.concatenate([x] * HG, axis=1)

    def delta_t(kp):
        idx = jnp.clip(_dot_nt(kp, qp), 0.0, float(LANES - 1)).astype(jnp.int32)
        outs = []
        for g in groups:
            heads = []
            for hg in range(HG):
                tb = jnp.broadcast_to(td_ref[g * HG + hg:g * HG + hg + 1, :], idx.shape)
                heads.append(jnp.take_along_axis(tb, idx, axis=1))
            outs.append(jnp.concatenate(heads, axis=1))
        return outs

    def maybe_delta(flag, kp):
        zeros = jnp.zeros((kp.shape[0], HG * QB), F32)
        return lax.cond(flag != 0, lambda: tuple(delta_t(kp)), lambda: (zeros,) * G)

    gate = [jax.nn.sigmoid(gl_ref[0, g, 0]) for g in groups]

    sc = [_dot_nt(kc_ref[0, g], q[g]) for g in groups]
    dl = [maybe_delta(near_c[qi * n_ct + ct], cp_ref[ct * LANES:(ct + 1) * LANES, :]) for ct in range(n_ct)]
    c_id = lax.broadcasted_iota(jnp.int32, (nc, QB), 0)
    c_q = lax.broadcasted_iota(jnp.int32, (nc, QB), 1)
    ok_c = lanes4(jnp.where(c_id * NSA_CMP_STRIDE + (NSA_CMP_LEN - 1) <= s0 + c_q, 1.0, 0.0))
    blk = lax.broadcasted_iota(jnp.int32, (LANES, QB), 0)
    t_tok = s0 + lax.broadcasted_iota(jnp.int32, (LANES, QB), 1)
    tb_blk = t_tok // NSA_SEL_LEN
    forced = jnp.where(blk == 0, 1.0, 0.0) + jnp.where(blk == tb_blk, 1.0, 0.0) \
        + jnp.where(blk == tb_blk - 1, 1.0, 0.0)
    blkf = blk.astype(F32)
    out, score = [], []
    for g in groups:
        s = sc[g] + jnp.concatenate([d[g] for d in dl], axis=0)
        s = jnp.where(ok_c > 0.5, s, NSA_NEG)
        e = jnp.exp2(s - jnp.max(s, axis=0, keepdims=True)) * ok_c
        p = e * (1.0 / jnp.maximum(jnp.sum(e, axis=0, keepdims=True), 1e-30))
        out.append(gate[g][0:1, :] * jnp.dot(vc_ref[0, g], p.astype(BF16), preferred_element_type=F32))
        ps = p[:, 0:QB]
        for hg in range(1, HG):
            ps = ps + p[:, hg * QB:(hg + 1) * QB]
        ps_hi, ps_lo = _split_bf16(ps)
        imp = (jnp.dot(agg_ref[...], ps_hi, preferred_element_type=F32)
               + jnp.dot(agg_ref[...], ps_lo, preferred_element_type=F32))
        sco = jnp.where(forced > 0.5, 1e9, jnp.where(blk * NSA_SEL_LEN <= t_tok, imp, -1e9))
        score.append(jnp.where(blk < n_sel, sco, -jnp.inf))
    sel = [jnp.zeros((LANES, QB), F32) for _ in groups]
    for _ in range(sel_k):
        for g in groups:
            mx = jnp.max(score[g], axis=0, keepdims=True)
            first = jnp.min(jnp.where(score[g] == mx, blkf, float(LANES)), axis=0, keepdims=True)
            hit = blkf == first
            sel[g] = jnp.where(hit, 1.0, sel[g])
            score[g] = jnp.where(hit, -jnp.inf, score[g])
    q2 = []
    for g in groups:
        sel_q = jnp.transpose(jnp.where(sel[g] > 0.5, 0.0, NSA_NEG))
        q2.append(jnp.concatenate([q[g], jnp.concatenate([sel_q] * HG, axis=0).astype(BF16)], axis=1))

    def reset():
        m_ref[...] = jnp.full_like(m_ref, NSA_NEG)
        l_ref[...] = jnp.zeros_like(l_ref)
        acc_ref[...] = jnp.zeros_like(acc_ref)

    def tile_step(g, k, s, v_t):
        m_old = m_ref[g, k]
        m_new = jnp.maximum(m_old, jnp.max(s, axis=0, keepdims=True))
        alpha = jnp.exp2(m_old - m_new)
        p = jnp.exp2(s - m_new)
        l_ref[g, k] = l_ref[g, k] * alpha + jnp.sum(p, axis=0, keepdims=True)
        acc_ref[g, k] = acc_ref[g, k] * alpha + jnp.dot(v_t, p.astype(BF16), preferred_element_type=F32)
        m_ref[g, k] = m_new

    def finish(g):
        m = jnp.maximum(m_ref[g, 0], m_ref[g, 1])
        w0, w1 = jnp.exp2(m_ref[g, 0] - m), jnp.exp2(m_ref[g, 1] - m)
        return (acc_ref[g, 0] * w0 + acc_ref[g, 1] * w1) * (1.0 / (l_ref[g, 0] * w0 + l_ref[g, 1] * w1))

    key_r = lax.broadcasted_iota(jnp.int32, (NSA_KT, QB), 0)
    key_q = lax.broadcasted_iota(jnp.int32, (NSA_KT, QB), 1)

    def sel_scores(kt, with_delta, diag):
        k0 = pl.multiple_of(kt * NSA_KT, NSA_KT)
        s = [_dot_nt(ks_ref[0, g, pl.ds(k0, NSA_KT), :], q2[g]) for g in groups]
        if with_delta:
            d = delta_t(kp_ref[pl.ds(k0, NSA_KT), :])
            s = [s[g] + d[g] for g in groups]
        if diag:
            causal = lanes4(jnp.where(k0 + key_r <= s0 + key_q, 0.0, NSA_NEG))
            s = [x + causal for x in s]
        return [(s[g], vs_ref[0, g, :, pl.ds(k0, NSA_KT)]) for g in groups]

    reset()
    n_past = (s0 + QB - 1) // NSA_KT
    n_pairs = nfast[qi] // 2

    def fast_body(i, carry):
        a = sel_scores(2 * i, False, False)
        b = sel_scores(2 * i + 1, False, False)
        for g in groups:
            tile_step(g, 0, *a[g])
        for g in groups:
            tile_step(g, 1, *b[g])
        return carry

    def slow_body(kt, carry):
        a = sel_scores(kt, True, False)
        for g in groups:
            tile_step(g, 1, *a[g])
        return carry

    lax.fori_loop(0, n_pairs, fast_body, 0)
    lax.fori_loop(2 * n_pairs, n_past, slow_body, 0)
    a = sel_scores(n_past, True, True)
    for g in groups:
        tile_step(g, 0, *a[g])
    out = [out[g] + gate[g][1:2, :] * finish(g) for g in groups]

    reset()
    w_r = lax.broadcasted_iota(jnp.int32, (LANES, QB), 0)
    w_q = lax.broadcasted_iota(jnp.int32, (LANES, QB), 1)
    in_window = lanes4(jnp.where(w_r > w_q, 0.0, NSA_NEG))
    causal_w = lanes4(jnp.where(w_r <= w_q, 0.0, NSA_NEG))

    def win_scores(g, st, n):
        st = pl.multiple_of(st, LANES)
        return _dot_nt(kw_ref[0, g, pl.ds(st, n), :], q[g]), vw_ref[0, g, :, pl.ds(st, n)]

    @pl.when(wstd[qi] != 0)
    def _():
        half = NSA_WINDOW // 2
        zeros = jnp.zeros((LANES, HG * QB), F32)
        sa = [win_scores(g, s0 - NSA_WINDOW, half) for g in groups]
        sb = [win_scores(g, s0 - half, half) for g in groups]
        sd = [win_scores(g, s0, LANES) for g in groups]
        near = delta_t(kp_ref[pl.ds(pl.multiple_of(s0 - LANES, LANES), LANES), :])
        diag = delta_t(kp_ref[pl.ds(pl.multiple_of(s0, LANES), LANES), :])
        for g in groups:
            tile_step(g, 0, sa[g][0] + jnp.concatenate([in_window, zeros], axis=0), sa[g][1])
        for g in groups:
            tile_step(g, 1, sb[g][0] + jnp.concatenate([zeros, near[g]], axis=0), sb[g][1])
        for g in groups:
            tile_step(g, 0, sd[g][0] + diag[g] + causal_w, sd[g][1])

    for j in range(NSA_WTILES):
        start = s0 - NSA_WINDOW + j * LANES

        @pl.when((wstd[qi] == 0) & (start >= 0))
        def _():
            sw = [win_scores(g, start, LANES) for g in groups]
            d = maybe_delta(near_w[qi * NSA_WTILES + j],
                            kp_ref[pl.ds(pl.multiple_of(start, LANES), LANES), :])
            for g in groups:
                s = sw[g][0] + d[g]
                if j == 0:
                    s = s + in_window
                elif j == NSA_WTILES - 1:
                    s = s + causal_w
                tile_step(g, j % 2, s, sw[g][1])

    for g in groups:
        o_ref[0, g, 0] = out[g] + gate[g][2:3, :] * finish(g)


def _pos_digits(p, key_side):
    d0, d1, d2 = (p & 127).astype(F32), ((p >> 7) & 127).astype(F32), (p >> 14).astype(F32)
    one = jnp.ones_like(d0)
    if key_side:
        cols = [one, one, one, -d2, -d1, -d0]
    else:
        cols = [16384.0 * d2, 128.0 * d1, d0, 16384.0 * one, 128.0 * one, one]
    out = jnp.stack(cols, axis=-1)
    return jnp.pad(out, ((0, 0), (0, LANES - out.shape[-1]))).astype(BF16)


def nsa_mixer(q, k_c, v_c, k_s, v_s, k_w, v_w, gate_logits, positions,
              ck_pe, ck_w1, ck_w2, cv_pe, cv_w1, cv_w2, rel_bias):
    q, k_c, v_c, k_s, v_s, k_w, v_w = lax.optimization_barrier((q, k_c, v_c, k_s, v_s, k_w, v_w))
    bs, s, _ = q.shape
    G, HG, DH, QB = NSA_KV_GROUPS, NSA_HPG, NSA_DH, NSA_QBLOCK
    nqt = s // QB
    nc = s // NSA_CMP_STRIDE
    n_cmp = (s - NSA_CMP_LEN) // NSA_CMP_STRIDE + 1
    n_sel = s // NSA_SEL_LEN
    sel_k = min(NSA_SEL_TOPK, n_sel)
    n_kt = s // NSA_KT
    n_ct = nc // LANES
    assert s % (LANES * NSA_CMP_STRIDE) == 0 and n_sel <= LANES and NSA_KT == 2 * QB

    def by_group(t):
        return t.reshape(bs, s, G, DH).transpose(0, 2, 1, 3)

    def by_group_t(t):
        return t.reshape(bs, s, G, DH).transpose(0, 2, 3, 1).astype(BF16)

    ones2 = jnp.zeros((LANES - DH,), F32).at[:2].set(1.0)

    def keys_aug(t, blocks=False):
        parts = [t, jnp.broadcast_to(ones2, t.shape[:-1] + (LANES - DH,))]
        if blocks:
            onehot = (jnp.arange(s)[:, None] // NSA_SEL_LEN == jnp.arange(LANES)[None, :]).astype(F32)
            parts.append(jnp.broadcast_to(onehot, t.shape[:-2] + (s, LANES)))
        return jnp.concatenate(parts, axis=-1).astype(BF16)

    chunks = jnp.stack([by_group(k_c), by_group(v_c)]).reshape(2, bs, G, nc, NSA_CMP_STRIDE * DH)
    pe = jnp.stack([ck_pe, cv_pe]).reshape(2, 1, NSA_CMP_LEN * DH)
    cmp = nsa_compress(chunks.astype(BF16), jnp.broadcast_to(pe, (2, 8, NSA_CMP_LEN * DH)).astype(BF16),
                       jnp.stack([ck_w1, cv_w1]).astype(BF16), jnp.stack([ck_w2, cv_w2]).astype(BF16))
    kc, vc_t = keys_aug(cmp[0]), cmp[1].transpose(0, 1, 3, 2).astype(BF16)

    log2e = math.log2(math.e)
    rel_bias = rel_bias.astype(F32) * log2e
    far = rel_bias[REL_BUCKETS - 1]
    far_hi = far.astype(BF16).astype(F32)
    extra = jnp.zeros((NSA_HEADS, LANES - DH), F32).at[:, 0].set(far_hi).at[:, 1].set(far - far_hi)
    qh = (q * (DH ** -0.5 * log2e)).reshape(bs, nqt, QB, G, HG, DH).transpose(0, 3, 1, 4, 2, 5)
    ex = jnp.broadcast_to(extra.reshape(1, G, 1, HG, 1, LANES - DH), (bs, G, nqt, HG, QB, LANES - DH))
    qs = jnp.concatenate([qh, ex], axis=-1).astype(BF16).reshape(bs, G, nqt, HG * QB, LANES)
    gl = gate_logits.reshape(bs, nqt, QB, G, HG, 3).transpose(0, 3, 1, 5, 4, 2)
    gl = jnp.pad(gl.reshape(bs, G, nqt, 3, HG * QB), ((0, 0),) * 3 + ((0, 5), (0, 0)))

    td = (rel_bias[_bucket_table()] - rel_bias[REL_BUCKETS - 1][None, :]).T.astype(F32)
    cmp_end = jnp.minimum(jnp.arange(nc) * NSA_CMP_STRIDE + NSA_CMP_LEN - 1, s - 1)
    cpos = positions[cmp_end]
    qmin = positions.reshape(nqt, QB).min(axis=1)
    kmax = positions.reshape(nqt, QB).max(axis=1)
    near_s = (qmin[:, None] - positions.reshape(n_kt, NSA_KT).max(axis=1)[None, :]) < REL_MAX_DIST
    n_past = (jnp.arange(nqt) * QB + QB - 1) // NSA_KT
    n_fast = jnp.minimum(jnp.argmax(jnp.concatenate([near_s, jnp.ones((nqt, 1), bool)], axis=1), axis=1),
                         n_past)
    wt = jnp.clip(jnp.arange(nqt)[:, None] - (NSA_WTILES - 1) + jnp.arange(NSA_WTILES)[None, :], 0, nqt - 1)
    near_w = (qmin[:, None] - kmax[wt]) < REL_MAX_DIST
    near_c = (qmin[:, None] - cpos.reshape(n_ct, LANES).max(axis=1)[None, :]) < REL_MAX_DIST
    w_std = ((jnp.arange(nqt) >= NSA_WTILES - 1)
             & jnp.all(near_w == (jnp.arange(NSA_WTILES) >= NSA_WTILES - 2)[None, :], axis=1))

    cmp_start = np.arange(nc) * NSA_CMP_STRIDE
    sel_start = np.arange(LANES) * NSA_SEL_LEN
    agg_t = ((cmp_start[None, :] <= sel_start[:, None] + NSA_SEL_LEN - 1)
             & (cmp_start[None, :] + NSA_CMP_LEN - 1 >= sel_start[:, None])
             & (np.arange(nc)[None, :] < n_cmp) & (np.arange(LANES)[:, None] < n_sel))

    cols = HG * QB
    full = lambda shape: pl.BlockSpec(shape, lambda b, i, *_: (0,) * len(shape))
    per_b = lambda n, w: pl.BlockSpec((1, G, n, w), lambda b, i, *_: (b, 0, 0, 0))
    per_tile = lambda n, w: pl.BlockSpec((1, G, 1, n, w), lambda b, i, *_: (b, 0, i, 0, 0))
    grid_spec = pltpu.PrefetchScalarGridSpec(
        num_scalar_prefetch=4,
        grid=(bs, nqt),
        in_specs=[per_tile(cols, LANES), per_tile(8, cols),
                  pl.BlockSpec((QB, LANES), lambda b, i, *_: (i, 0)),
                  full((s, LANES)), full((nc, LANES)),
                  per_b(s, 2 * LANES), per_b(DH, s), per_b(s, LANES), per_b(DH, s),
                  per_b(nc, LANES), per_b(DH, nc),
                  full((LANES, nc)), full((NSA_HEADS, LANES))],
        out_specs=per_tile(DH, cols),
        scratch_shapes=[pltpu.VMEM((G, 2, 1, cols), F32), pltpu.VMEM((G, 2, 1, cols), F32),
                        pltpu.VMEM((G, 2, DH, cols), F32)],
    )
    out = pl.pallas_call(
        functools.partial(_nsa_body, n_sel=n_sel, sel_k=sel_k),
        grid_spec=grid_spec,
        out_shape=jax.ShapeDtypeStruct((bs, G, nqt, DH, cols), F32),
        compiler_params=_params("parallel", "arbitrary"),
        name="nsa_attention",
    )(n_fast.astype(jnp.int32), w_std.astype(jnp.int32), near_w.reshape(-1).astype(jnp.int32),
      near_c.reshape(-1).astype(jnp.int32),
      qs, gl, _pos_digits(positions, False), _pos_digits(positions, True), _pos_digits(cpos, True),
      keys_aug(by_group(k_s), blocks=True), by_group_t(v_s), keys_aug(by_group(k_w)), by_group_t(v_w),
      kc, vc_t, jnp.asarray(agg_t, BF16), td)
    out = out.reshape(bs, G, nqt, DH, HG, QB).transpose(0, 2, 5, 1, 4, 3)
    return out.reshape(bs, s, G * HG * DH)


SGU_TC = 512


def _sgu_body(uv_ref, lg_ref, lb_ref, w_ref, b_ref, o_ref):
    uv = jax.nn.gelu(uv_ref[...])
    u, v = uv[:, :SGU_WIDTH], uv[:, SGU_WIDTH:]
    mu = jnp.mean(v, axis=-1, keepdims=True)
    vc = v - mu
    var = jnp.mean(vc * vc, axis=-1, keepdims=True)
    vn = (vc * lax.rsqrt(var + RMS_EPS) * lg_ref[...] + lb_ref[...]).astype(BF16)
    gw = SGU_WIDTH // SGU_GROUPS
    for c in range(SGU_TC // SGU_CHUNK):
        rows = slice(c * SGU_CHUNK, (c + 1) * SGU_CHUNK)
        for g in range(SGU_GROUPS):
            cols = slice(g * gw, (g + 1) * gw)
            mix = jnp.dot(w_ref[g], vn[rows, cols], preferred_element_type=F32) + b_ref[:, g:g + 1]
            o_ref[rows, cols] = u[rows, cols] * mix


def sgu_mixer(proj, ln_g, ln_b, w_s, b_s):
    t = proj.shape[0]
    assert t % SGU_TC == 0 and OFF_SGU % (2 * SGU_WIDTH) == 0
    fixed = lambda i: (0, 0)
    return pl.pallas_call(
        _sgu_body,
        grid=(t // SGU_TC,),
        in_specs=[pl.BlockSpec((SGU_TC, 2 * SGU_WIDTH), lambda i: (i, OFF_SGU // (2 * SGU_WIDTH))),
                  pl.BlockSpec((1, SGU_WIDTH), fixed), pl.BlockSpec((1, SGU_WIDTH), fixed),
                  pl.BlockSpec((SGU_GROUPS, SGU_CHUNK, SGU_CHUNK), lambda i: (0, 0, 0)),
                  pl.BlockSpec((SGU_CHUNK, SGU_GROUPS), fixed)],
        out_specs=pl.BlockSpec((SGU_TC, SGU_WIDTH), lambda i: (i, 0)),
        out_shape=jax.ShapeDtypeStruct((t, SGU_WIDTH), F32),
        compiler_params=_params("parallel"),
        name="sgu_gate",
    )(proj, ln_g.reshape(1, SGU_WIDTH), ln_b.reshape(1, SGU_WIDTH), jnp.tril(w_s).astype(BF16), b_s.T)


def _permute_w_in(w):
    sizes = ([SSM_WIDTH, GDN_QKV, GDN_HEADS, GDN_HEADS, GDN_HEADS * GDN_DV, NSA_Q]
             + [NSA_KV] * 6 + [3 * NSA_HEADS, 2 * SGU_WIDTH, N_BRANCH * D_MODEL])
    cuts = [int(c) for c in np.cumsum(sizes)[:-1]]
    (w_ssm, w_qkv, w_a, w_b, w_z, w_nq, w_kc, w_vc, w_ks, w_vs, w_kw, w_vw,
     w_ng, w_sgu, w_gate) = jnp.split(w, cuts, axis=-1)
    pad = jnp.zeros((w.shape[0], PROJ_COLS - OFF_SMALL - SMALL_USED), w.dtype)
    return jnp.concatenate([w_qkv, w_ssm, w_z, w_nq, w_sgu, w_gate, w_kc, w_vc, w_ks, w_vs,
                            w_kw, w_vw, w_a, w_b, w_ng, pad], axis=-1)


def kernel(x, positions, norm_mix, norm_ffn, norm_final, w_in, ssm_a_re, ssm_a_im, ssm_log_dt, ssm_b_re, ssm_b_im, ssm_c_re, ssm_c_im, ssm_d, ssm_glu_w, gdn_conv_w, gdn_a_log, gdn_dt_bias, gdn_norm, nsa_cmp_k_pe, nsa_cmp_k_w1, nsa_cmp_k_w2, nsa_cmp_v_pe, nsa_cmp_v_w1, nsa_cmp_v_w2, rel_bias, sgu_ln_g, sgu_ln_b, sgu_w, sgu_b, w_br_ssm, w_br_gdn, w_br_nsa, w_br_sgu, w_out, ffn_w_gate, ffn_w_up, ffn_w_down, moe_router, moe_w_gate, moe_w_up, moe_w_down):
    bs, s, d = x.shape
    t = bs * s
    assert DEPTH == 2
    xf = x.reshape(t, d)
    for layer in range(DEPTH):
        proj = in_proj(xf, norm_mix[layer], _permute_w_in(w_in[layer]))
        p3 = proj.reshape(bs, s, PROJ_COLS)

        def seg(off, width):
            return p3[..., off:off + width]

        y_ssm = ssm_mixer(seg(OFF_SSM, SSM_WIDTH), ssm_a_re[layer], ssm_a_im[layer],
                          ssm_log_dt[layer], ssm_b_re[layer], ssm_b_im[layer], ssm_c_re[layer],
                          ssm_c_im[layer], ssm_d[layer], ssm_glu_w[layer])
        y_gdn = gdn_mixer(proj, bs, gdn_conv_w[layer], gdn_a_log[layer], gdn_dt_bias[layer],
                          gdn_norm[layer])
        kvs = [seg(OFF_KV + i * NSA_KV, NSA_KV) for i in range(6)]
        y_nsa = nsa_mixer(seg(OFF_NQ, NSA_Q), *kvs, seg(OFF_SMALL + 2 * GDN_HEADS, 3 * NSA_HEADS),
                          positions, nsa_cmp_k_pe[layer], nsa_cmp_k_w1[layer], nsa_cmp_k_w2[layer],
                          nsa_cmp_v_pe[layer], nsa_cmp_v_w1[layer], nsa_cmp_v_w2[layer], rel_bias)
        y_sgu = sgu_mixer(proj, sgu_ln_g[layer], sgu_ln_b[layer], sgu_w[layer], sgu_b[layer])
        ys = [y_ssm.reshape(t, -1), y_gdn, y_nsa.reshape(t, -1), y_sgu]
        ws = [w[layer].astype(BF16) for w in (w_br_ssm, w_br_gdn, w_br_nsa, w_br_sgu)]
        xf = merge(xf, proj, ys, ws, w_out[layer].astype(BF16))
        i = layer // 2
        if layer % 2 == 0:
            xf = dense_ffn(xf, norm_ffn[layer], ffn_w_gate[i], ffn_w_up[i], ffn_w_down[i])
        else:
            xf = moe_layer(xf, norm_ffn[layer], moe_router[i], moe_w_gate[i], moe_w_up[i],
                           moe_w_down[i], norm_final)
    return xf.reshape(bs, s, d)
```

```python
import functools
import math

import jax
import jax.numpy as jnp
from jax import lax
import numpy as np
from jax.experimental import pallas as pl
from jax.experimental.pallas import tpu as pltpu

F32 = jnp.float32
BF16 = jnp.bfloat16

D_MODEL = 1024
DEPTH = 2
SSM_WIDTH = D_MODEL // 2
SSM_GROUP = 16
SSM_GROUPS = SSM_WIDTH // SSM_GROUP
SSM_STATE = 64
GDN_HEADS = 4
GDN_DK = 128
GDN_DV = 128
GDN_CONV = 4
GDN_CHUNK = 64
NSA_HEADS = 8
NSA_KV_GROUPS = 2
NSA_HPG = NSA_HEADS // NSA_KV_GROUPS
NSA_DH = 64
NSA_CMP_LEN = 32
NSA_CMP_STRIDE = 16
NSA_CMP_HIDDEN = 128
NSA_SEL_LEN = 64
NSA_SEL_TOPK = 16
NSA_WINDOW = 512
NSA_QBLOCK = 128
SGU_WIDTH = D_MODEL // 2
SGU_GROUPS = 4
SGU_CHUNK = 128
REL_BUCKETS = 32
REL_MAX_DIST = 128
FFN_DENSE = 2816
N_EXPERTS = 8
TOP_K = 2
FFN_EXPERT = 3584
N_BRANCH = 4
RMS_EPS = 1e-6
GDN_QKV = GDN_HEADS * (2 * GDN_DK + GDN_DV)
NSA_Q = NSA_HEADS * NSA_DH
NSA_KV = NSA_KV_GROUPS * NSA_DH

LANES = 128
VMEM_LIMIT = 48 * 1024 * 1024
MASKED = -1e30

OFF_QKV = 0
OFF_SSM = OFF_QKV + GDN_QKV
OFF_Z = OFF_SSM + SSM_WIDTH
OFF_NQ = OFF_Z + GDN_HEADS * GDN_DV
OFF_SGU = OFF_NQ + NSA_Q
OFF_GATE = OFF_SGU + 2 * SGU_WIDTH
OFF_KV = OFF_GATE + N_BRANCH * D_MODEL
OFF_SMALL = OFF_KV + 6 * NSA_KV
SMALL_USED = 2 * GDN_HEADS + 3 * NSA_HEADS
PROJ_COLS = 9216


def _params(*sem):
    return pltpu.CompilerParams(dimension_semantics=sem, vmem_limit_bytes=VMEM_LIMIT)


def _rms(x, g):
    return x * lax.rsqrt(jnp.mean(x * x, axis=-1, keepdims=True) + RMS_EPS) * g


def _dot_nt(a, b):
    return lax.dot_general(a, b, (((1,), (1,)), ((), ())), preferred_element_type=F32)


def _dot_tn(a, b):
    return lax.dot_general(a, b, (((0,), (0,)), ((), ())), preferred_element_type=F32)


def _split_bf16(x):
    hi = x.astype(BF16)
    return hi, (x - hi.astype(F32)).astype(BF16)


def _in_proj_body(x_ref, g_ref, w_ref, o_ref, h_ref):
    @pl.when(pl.program_id(1) == 0)
    def _():
        h_ref[...] = _rms(x_ref[...], g_ref[...]).astype(BF16)

    o_ref[...] = jnp.dot(h_ref[...], w_ref[...].astype(BF16), preferred_element_type=F32)


def in_proj(x, g, w, tm=2048, tn=512):
    t, d = x.shape
    n = w.shape[1]
    return pl.pallas_call(
        _in_proj_body,
        grid=(t // tm, n // tn),
        in_specs=[pl.BlockSpec((tm, d), lambda i, j: (i, 0)),
                  pl.BlockSpec((1, d), lambda i, j: (0, 0)),
                  pl.BlockSpec((d, tn), lambda i, j: (0, j))],
        out_specs=pl.BlockSpec((tm, tn), lambda i, j: (i, j)),
        out_shape=jax.ShapeDtypeStruct((t, n), F32),
        scratch_shapes=[pltpu.VMEM((tm, d), BF16)],
        compiler_params=_params("parallel", "arbitrary"),
        name="in_proj",
    )(x, g.reshape(1, d), w)


def _merge_body(x_ref, gate_ref, ya_ref, yb_ref, yc_ref, yd_ref,
                wa_ref, wb_ref, wc_ref, wd_ref, wo_ref, o_ref):
    def branch(k, y_ref, w_ref):
        p = jnp.dot(y_ref[...].astype(BF16), w_ref[...], preferred_element_type=F32)
        return jax.nn.sigmoid(gate_ref[:, k * D_MODEL:(k + 1) * D_MODEL]) * p

    merged = (branch(0, ya_ref, wa_ref) + branch(1, yb_ref, wb_ref)
              + branch(2, yc_ref, wc_ref) + branch(3, yd_ref, wd_ref))
    o_ref[...] = x_ref[...] + jnp.dot(merged.astype(BF16), wo_ref[...],
                                      preferred_element_type=F32)


def merge(x, proj, ys, ws, w_out, tm=256):
    t, d = x.shape
    gate_blk = OFF_GATE // (N_BRANCH * D_MODEL)
    row = lambda i: (i, 0)
    fixed = lambda i: (0, 0)
    in_specs = [pl.BlockSpec((tm, d), row),
                pl.BlockSpec((tm, N_BRANCH * D_MODEL), lambda i: (i, gate_blk))]
    in_specs += [pl.BlockSpec((tm, y.shape[1]), row) for y in ys]
    in_specs += [pl.BlockSpec(w.shape, fixed) for w in ws]
    in_specs += [pl.BlockSpec(w_out.shape, fixed)]
    return pl.pallas_call(
        _merge_body,
        grid=(t // tm,),
        in_specs=in_specs,
        out_specs=pl.BlockSpec((tm, d), row),
        out_shape=jax.ShapeDtypeStruct((t, d), F32),
        compiler_params=_params("parallel"),
        name="merge",
    )(x, proj, *ys, *ws, w_out)


def _ffn_body(x_ref, g_ref, wg_ref, wu_ref, wd_ref, o_ref, h_ref, acc_ref):
    f = pl.program_id(1)

    @pl.when(f == 0)
    def _():
        h_ref[...] = _rms(x_ref[...], g_ref[...]).astype(BF16)
        acc_ref[...] = x_ref[...]

    h = h_ref[...]
    a = jnp.dot(h, wg_ref[...].astype(BF16), preferred_element_type=F32)
    u = jnp.dot(h, wu_ref[...].astype(BF16), preferred_element_type=F32)
    act = (a * jax.nn.sigmoid(a) * u).astype(BF16)
    acc_ref[...] += jnp.dot(act, wd_ref[...].astype(BF16), preferred_element_type=F32)

    @pl.when(f == pl.num_programs(1) - 1)
    def _():
        o_ref[...] = acc_ref[...]


def dense_ffn(x, g, wg, wu, wd, tm=1024, tf=256):
    t, d = x.shape
    nf = wg.shape[1]
    return pl.pallas_call(
        _ffn_body,
        grid=(t // tm, nf // tf),
        in_specs=[pl.BlockSpec((tm, d), lambda i, f: (i, 0)),
                  pl.BlockSpec((1, d), lambda i, f: (0, 0)),
                  pl.BlockSpec((d, tf), lambda i, f: (0, f)),
                  pl.BlockSpec((d, tf), lambda i, f: (0, f)),
                  pl.BlockSpec((tf, d), lambda i, f: (f, 0))],
        out_specs=pl.BlockSpec((tm, d), lambda i, f: (i, 0)),
        out_shape=jax.ShapeDtypeStruct((t, d), F32),
        scratch_shapes=[pltpu.VMEM((tm, d), BF16), pltpu.VMEM((tm, d), F32)],
        compiler_params=_params("parallel", "arbitrary"),
        name="dense_ffn",
    )(x, g.reshape(1, d), wg, wu, wd)


def _route_body(x_ref, g_ref, rhi_ref, rlo_ref, h_ref, idx_ref, wt_ref):
    h = _rms(x_ref[...], g_ref[...])
    hi = h.astype(BF16)
    lo = (h - hi.astype(F32)).astype(BF16)
    h_ref[...] = h
    logits = (jnp.dot(hi, rhi_ref[...], preferred_element_type=F32)
              + jnp.dot(hi, rlo_ref[...], preferred_element_type=F32)
              + jnp.dot(lo, rhi_ref[...], preferred_element_type=F32))
    col = lax.broadcasted_iota(jnp.int32, logits.shape, 1).astype(F32)
    neg = jnp.float32(-jnp.inf)
    lg = jnp.where(col < N_EXPERTS, logits, neg)
    m1 = jnp.max(lg, axis=-1, keepdims=True)
    i1 = jnp.min(jnp.where(lg == m1, col, float(LANES)), axis=-1, keepdims=True)
    lg2 = jnp.where(col == i1, neg, lg)
    m2 = jnp.max(lg2, axis=-1, keepdims=True)
    i2 = jnp.min(jnp.where(lg2 == m2, col, float(LANES)), axis=-1, keepdims=True)
    e = jnp.exp(m2 - m1)
    w1 = 1.0 / (1.0 + e)
    w2 = e / (1.0 + e)
    idx_ref[...] = jnp.where(col == 0, i1, jnp.where(col == 1, i2, 0.0)).astype(jnp.int32)
    wt_ref[...] = jnp.where(col == 0, w1, jnp.where(col == 1, w2, 0.0))


def moe_route(x, g, r_hi, r_lo, tm=512):
    t, d = x.shape
    row = lambda i: (i, 0)
    fixed = lambda i: (0, 0)
    return pl.pallas_call(
        _route_body,
        grid=(t // tm,),
        in_specs=[pl.BlockSpec((tm, d), row), pl.BlockSpec((1, d), fixed),
                  pl.BlockSpec((d, LANES), fixed), pl.BlockSpec((d, LANES), fixed)],
        out_specs=[pl.BlockSpec((tm, d), row), pl.BlockSpec((tm, LANES), row),
                   pl.BlockSpec((tm, LANES), row)],
        out_shape=[jax.ShapeDtypeStruct((t, d), F32),
                   jax.ShapeDtypeStruct((t, LANES), jnp.int32),
                   jax.ShapeDtypeStruct((t, LANES), F32)],
        compiler_params=_params("parallel"),
        name="moe_route",
    )(x, g.reshape(1, d), r_hi, r_lo)


def _experts_body(te_ref, nu_ref, src_ref, h_hbm, wg_ref, wu_ref, wd_ref, o_ref,
                  xbuf_ref, xs_ref, acc_ref, sem, *, rows_per_step):
    i = pl.program_id(0)
    f = pl.program_id(1)
    tm = xs_ref.shape[0]
    n_rows = xbuf_ref.shape[1]
    n_used = nu_ref[0]
    used = i < n_used
    has_next = i + 1 < n_used
    slot = i % 2

    def start_row(tile, slot_, r):
        tok = src_ref[tile * tm + r]
        pltpu.make_async_copy(h_hbm.at[pl.ds(tok, 1), :], xbuf_ref.at[slot_, pl.ds(r, 1), :],
                              sem.at[slot_]).start()

    @pl.when(f == 0)
    def _():
        acc_ref[...] = jnp.zeros_like(acc_ref)

        @pl.when(used)
        def _():
            @pl.when(i == 0)
            def _():
                def row(r, carry):
                    start_row(0, 0, r)
                    return carry

                lax.fori_loop(0, n_rows, row, 0, unroll=8)

            pltpu.make_async_copy(h_hbm.at[pl.ds(0, n_rows), :], xbuf_ref.at[slot], sem.at[slot]).wait()
            xs_ref[...] = xbuf_ref[slot, 0:tm, :].astype(BF16)

    def ffn_step():
        h = xs_ref[...]
        a = jnp.dot(h, wg_ref[0].astype(BF16), preferred_element_type=F32)
        u = jnp.dot(h, wu_ref[0].astype(BF16), preferred_element_type=F32)
        act = (a * jax.nn.sigmoid(a) * u).astype(BF16)
        acc_ref[...] += jnp.dot(act, wd_ref[0].astype(BF16), preferred_element_type=F32)

    @pl.when(used & has_next)
    def _():
        for j in range(rows_per_step):
            start_row(i + 1, 1 - slot, f * rows_per_step + j)
        ffn_step()

    @pl.when(used & jnp.logical_not(has_next))
    def _():
        ffn_step()

    @pl.when(f == pl.num_programs(1) - 1)
    def _():
        o_ref[...] = acc_ref[...].astype(o_ref.dtype)


def moe_experts(tile_expert, n_used, src, h, wg, wu, wd, tm, tf=512):
    d = h.shape[1]
    nf = wg.shape[2] // tf
    rows_per_step = 8 * (-(-tm // (8 * nf)))
    n_rows = rows_per_step * nf
    n_tiles = src.shape[0] // tm
    src = jnp.pad(src, (0, n_rows - tm))
    grid_spec = pltpu.PrefetchScalarGridSpec(
        num_scalar_prefetch=3,
        grid=(n_tiles, nf),
        in_specs=[pl.BlockSpec(memory_space=pl.ANY),
                  pl.BlockSpec((1, d, tf), lambda i, f, te, *_: (te[i], 0, f)),
                  pl.BlockSpec((1, d, tf), lambda i, f, te, *_: (te[i], 0, f)),
                  pl.BlockSpec((1, tf, d), lambda i, f, te, *_: (te[i], f, 0))],
        out_specs=pl.BlockSpec((tm, d), lambda i, f, *_: (i, 0)),
        scratch_shapes=[pltpu.VMEM((2, n_rows, d), F32), pltpu.VMEM((tm, d), BF16),
                        pltpu.VMEM((tm, d), F32), pltpu.SemaphoreType.DMA((2,))],
    )
    return pl.pallas_call(
        functools.partial(_experts_body, rows_per_step=rows_per_step),
        grid_spec=grid_spec,
        out_shape=jax.ShapeDtypeStruct((n_tiles * tm, d), BF16),
        compiler_params=_params("arbitrary", "arbitrary"),
        name="moe_experts",
    )(tile_expert, n_used, src, h, wg, wu, wd)


def _combine_norm_body(x_ref, ya_ref, yb_ref, wt_ref, g_ref, o_ref):
    wt = wt_ref[...]
    y = wt[:, 0:1] * ya_ref[...].astype(F32) + wt[:, 1:2] * yb_ref[...].astype(F32)
    o_ref[...] = _rms(x_ref[...] + y, g_ref[...])


def combine_norm(x, ya, yb, wts, g, tm=1024):
    t, d = x.shape
    row = lambda i: (i, 0)
    return pl.pallas_call(
        _combine_norm_body,
        grid=(t // tm,),
        in_specs=[pl.BlockSpec((tm, d), row)] * 3 + [pl.BlockSpec((tm, LANES), row),
                                                     pl.BlockSpec((1, d), lambda i: (0, 0))],
        out_specs=pl.BlockSpec((tm, d), row),
        out_shape=jax.ShapeDtypeStruct((t, d), F32),
        compiler_params=_params("parallel"),
        name="combine_norm",
    )(x, ya, yb, wts, g.reshape(1, d))


def moe_layer(x, g_norm, router, wg, wu, wd, g_final, tm=1024):
    t, d = x.shape
    r = jnp.zeros((d, LANES), F32).at[:, :N_EXPERTS].set(router)
    r_hi = r.astype(BF16)
    r_lo = (r - r_hi.astype(F32)).astype(BF16)
    h, idx, wts = moe_route(x, g_norm, r_hi, r_lo)
    e_flat = jnp.concatenate([idx[:, 0], idx[:, 1]])
    onehot = (e_flat[:, None] == jnp.arange(N_EXPERTS)[None, :]).astype(jnp.int32)
    csum = jnp.cumsum(onehot, axis=0)
    rank = jnp.sum((csum - onehot) * onehot, axis=1)
    counts = csum[-1]
    padded = ((counts + tm - 1) // tm) * tm
    ends = jnp.cumsum(padded)
    starts = ends - padded
    dest = starts[e_flat] + rank
    n_tiles = (TOP_K * t) // tm + N_EXPERTS
    p = n_tiles * tm
    tok = jnp.concatenate([jnp.arange(t, dtype=jnp.int32)] * TOP_K)
    src = jnp.zeros((p,), jnp.int32).at[dest].set(tok)
    tile_expert = jnp.minimum(
        jnp.searchsorted(ends, jnp.arange(n_tiles, dtype=jnp.int32) * tm, side="right"),
        N_EXPERTS - 1).astype(jnp.int32)
    n_used = (ends[-1] // tm).astype(jnp.int32).reshape(1)
    ys = moe_experts(tile_expert, n_used, src, h, wg, wu, wd, tm)
    ya = jnp.take(ys, dest[:t], axis=0, mode="clip")
    yb = jnp.take(ys, dest[t:], axis=0, mode="clip")
    return combine_norm(x, ya, yb, wts, g_final)


SSM_TC = 512
SSM_SUB = 128
SSM_HALF = 256
SSM_NSTATE = SSM_GROUPS * SSM_STATE


def _ssm_body(u_ref, bw_ref, cw_ref, d_ref, glu_ref, ar_ref, ai_ref, pr_ref, pi_ref, o_ref,
              xr_ref, xi_ref, cr_ref, ci_ref):
    @pl.when(pl.program_id(1) == 0)
    def _():
        cr_ref[...] = jnp.zeros_like(cr_ref)
        ci_ref[...] = jnp.zeros_like(ci_ref)

    u = u_ref[0]
    ub = u.astype(BF16)
    nblk = SSM_WIDTH // SSM_HALF
    sblk = SSM_NSTATE // nblk
    for k in range(nblk):
        bu = jnp.dot(ub[:, k * SSM_HALF:(k + 1) * SSM_HALF], bw_ref[k], preferred_element_type=F32)
        xr_ref[:, k * sblk:(k + 1) * sblk] = bu[:, :sblk]
        xi_ref[:, k * sblk:(k + 1) * sblk] = bu[:, sblk:]

    row8 = lax.broadcasted_iota(jnp.int32, (SSM_SUB, LANES), 0) % 8
    n_grp = SSM_SUB // 8

    def strip(c, carry):
        col = pl.ds(pl.multiple_of(c * LANES, LANES), LANES)
        c_r = cr_ref[:, col]
        c_i = ci_ref[:, col]
        p_r = pr_ref[0:8, col]
        p_i = pi_ref[0:8, col]
        for sub in range(SSM_TC // SSM_SUB):
            rows = slice(sub * SSM_SUB, (sub + 1) * SSM_SUB)
            xr = xr_ref[rows, col]
            xi = xi_ref[rows, col]
            for i in range(3):
                d = 1 << i
                a_r = ar_ref[i:i + 1, col]
                a_i = ai_ref[i:i + 1, col]
                sr = jnp.where(row8 >= d, pltpu.roll(xr, d, 0), 0.0)
                si = jnp.where(row8 >= d, pltpu.roll(xi, d, 0), 0.0)
                xr, xi = xr + a_r * sr - a_i * si, xi + a_r * si + a_i * sr
            out_r, out_i = [], []
            for r in range(n_grp):
                g_r = xr[8 * r:8 * r + 8] + p_r * c_r - p_i * c_i
                g_i = xi[8 * r:8 * r + 8] + p_r * c_i + p_i * c_r
                c_r, c_i = g_r[7:8, :], g_i[7:8, :]
                out_r.append(g_r)
                out_i.append(g_i)
            xr_ref[rows, col] = jnp.concatenate(out_r, axis=0)
            xi_ref[rows, col] = jnp.concatenate(out_i, axis=0)
        cr_ref[:, col] = c_r
        ci_ref[:, col] = c_i
        return carry

    lax.fori_loop(0, SSM_NSTATE // LANES, strip, 0)

    ys = []
    for k in range(nblk):
        st = jnp.concatenate([xr_ref[:, k * sblk:(k + 1) * sblk].astype(BF16),
                              xi_ref[:, k * sblk:(k + 1) * sblk].astype(BF16)], axis=1)
        ys.append(jnp.dot(st, cw_ref[k], preferred_element_type=F32))
    y = jax.nn.gelu(jnp.concatenate(ys, axis=1) + d_ref[...] * u)
    o_ref[0] = y * jax.nn.sigmoid(jnp.dot(y.astype(BF16), glu_ref[...], preferred_element_type=F32))


def ssm_mixer(u, a_re, a_im, log_dt, b_re, b_im, c_re, c_im, d_skip, glu_w):
    bs, s, _ = u.shape
    assert s % SSM_TC == 0
    nblk = SSM_WIDTH // SSM_HALF
    gpb = SSM_GROUPS // nblk
    dt = jnp.exp(log_dt)[:, None]
    mag = jnp.exp(a_re * dt)
    abar_r, abar_i = mag * jnp.cos(a_im * dt), mag * jnp.sin(a_im * dt)
    nr, ni = abar_r - 1.0, abar_i
    den = a_re * a_re + a_im * a_im
    sr, si = (nr * a_re + ni * a_im) / den, (ni * a_re - nr * a_im) / den
    bbar_r = sr[..., None] * b_re - si[..., None] * b_im
    bbar_i = sr[..., None] * b_im + si[..., None] * b_re

    def powers(k):
        kk = k.astype(F32)[:, None, None]
        m = jnp.exp(kk * (a_re * dt))
        return ((m * jnp.cos(kk * (a_im * dt))).reshape(-1, SSM_NSTATE),
                (m * jnp.sin(kk * (a_im * dt))).reshape(-1, SSM_NSTATE))

    ar, ai = powers(2 ** jnp.arange(8))
    pr, pi = powers(jnp.arange(1, SSM_SUB + 1))
    eye = jnp.eye(gpb, dtype=F32)

    def in_block(w):
        return jnp.einsum('gnp,gh->gphn', w, eye).reshape(gpb * SSM_GROUP, gpb * SSM_STATE)

    def out_block(w):
        return jnp.einsum('gpn,gh->gnhp', w, eye).reshape(gpb * SSM_STATE, gpb * SSM_GROUP)

    bw = jnp.stack([jnp.concatenate([in_block(bbar_r[k * gpb:(k + 1) * gpb]),
                                     in_block(bbar_i[k * gpb:(k + 1) * gpb])], axis=1)
                    for k in range(nblk)]).astype(BF16)
    cw = jnp.stack([jnp.concatenate([out_block(c_re[k * gpb:(k + 1) * gpb]),
                                     -out_block(c_im[k * gpb:(k + 1) * gpb])], axis=0)
                    for k in range(nblk)]).astype(BF16)
    fixed2 = lambda b, i: (0, 0)
    fixed3 = lambda b, i: (0, 0, 0)
    return pl.pallas_call(
        _ssm_body,
        grid=(bs, s // SSM_TC),
        in_specs=[pl.BlockSpec((1, SSM_TC, SSM_WIDTH), lambda b, i: (b, i, 0)),
                  pl.BlockSpec(bw.shape, fixed3), pl.BlockSpec(cw.shape, fixed3),
                  pl.BlockSpec((1, SSM_WIDTH), fixed2), pl.BlockSpec((SSM_WIDTH, SSM_WIDTH), fixed2),
                  pl.BlockSpec((8, SSM_NSTATE), fixed2), pl.BlockSpec((8, SSM_NSTATE), fixed2),
                  pl.BlockSpec((SSM_SUB, SSM_NSTATE), fixed2), pl.BlockSpec((SSM_SUB, SSM_NSTATE), fixed2)],
        out_specs=pl.BlockSpec((1, SSM_TC, SSM_WIDTH), lambda b, i: (b, i, 0)),
        out_shape=jax.ShapeDtypeStruct((bs, s, SSM_WIDTH), F32),
        scratch_shapes=[pltpu.VMEM((SSM_TC, SSM_NSTATE), F32), pltpu.VMEM((SSM_TC, SSM_NSTATE), F32),
                        pltpu.VMEM((1, SSM_NSTATE), F32), pltpu.VMEM((1, SSM_NSTATE), F32)],
        compiler_params=_params("parallel", "arbitrary"),
        name="ssm_scan",
    )(u, bw, cw, d_skip.reshape(1, SSM_WIDTH), glu_w.astype(BF16), ar, ai, pr, pi)


GDN_TC = 256
GDN_HALO = 8


def _gdn_body(nega_ref, dtb_ref, q_ref, k_ref, v_ref, z_ref, sm_ref, wq_ref, wk_ref, wv_ref, ng_ref,
              o_ref, state_ref, hq_ref, hk_ref, hv_ref, vnew_ref):
    TC, CH, DK, H = GDN_TC, GDN_CHUNK, GDN_DK, GDN_HEADS
    heads = range(H)

    @pl.when(pl.program_id(1) == 0)
    def _():
        state_ref[...] = jnp.zeros_like(state_ref)
        hq_ref[...] = jnp.zeros_like(hq_ref)
        hk_ref[...] = jnp.zeros_like(hk_ref)
        hv_ref[...] = jnp.zeros_like(hv_ref)

    def conv_silu(x_ref, halo_ref, w_ref):
        x = x_ref[...]
        xx = jnp.concatenate([halo_ref[...], x], axis=0)
        w = w_ref[...]
        y = w[GDN_CONV - 1:GDN_CONV] * x
        for d in range(1, GDN_CONV):
            y = y + w[GDN_CONV - 1 - d:GDN_CONV - d] * pltpu.roll(xx, d, 0)[GDN_HALO:GDN_HALO + TC]
        halo_ref[...] = x[TC - GDN_HALO:TC]
        return y * jax.nn.sigmoid(y)

    def per_head(x):
        return [x[:, h * LANES:(h + 1) * LANES] for h in heads]

    q = per_head(conv_silu(q_ref, hq_ref, wq_ref))
    k = per_head(conv_silu(k_ref, hk_ref, wk_ref))
    v = per_head(conv_silu(v_ref, hv_ref, wv_ref))
    q = [x * lax.rsqrt(jnp.sum(x * x, axis=-1, keepdims=True) + 1e-6) * (DK ** -0.5) for x in q]
    k = [x * lax.rsqrt(jnp.sum(x * x, axis=-1, keepdims=True) + 1e-6) for x in k]

    small = sm_ref[...]
    beta = [jax.nn.sigmoid(small[:, H + h:H + h + 1]) for h in heads]
    la = []
    for h in heads:
        xa = small[:, h:h + 1] + dtb_ref[h]
        softplus = jnp.maximum(xa, 0.0) + jnp.log(1.0 + jnp.exp(-jnp.abs(xa)))
        la.append(jnp.broadcast_to(nega_ref[h] * softplus, (TC, LANES)))

    ri = lax.broadcasted_iota(jnp.int32, (TC, TC), 0)
    ci = lax.broadcasted_iota(jnp.int32, (TC, TC), 1)
    same = (ri // CH) == (ci // CH)
    causal = jnp.where(same, jnp.where(ci <= ri, 1.0, 0.0), 0.0)
    strict = jnp.where(ci < ri, causal, 0.0)
    eye = jnp.where(ri == ci, 1.0, 0.0)
    tri = causal.astype(BF16)
    lane = lax.broadcasted_iota(jnp.int32, (TC, LANES), 1)

    g = []
    for h in heads:
        la_hi, la_lo = _split_bf16(la[h])
        g.append(jnp.dot(tri, la_hi, preferred_element_type=F32)
                 + jnp.dot(tri, la_lo, preferred_element_type=F32))
    decay = []
    for h in heads:
        g_hi = g[h].astype(BF16).astype(F32)
        g_lo = g[h] - g_hi
        lhs = jnp.where(lane == 0, g_hi, jnp.where(lane == 1, g_lo, jnp.where(lane < 4, 1.0, 0.0)))
        rhs = jnp.where(lane < 2, 1.0, jnp.where(lane == 2, -g_hi, jnp.where(lane == 3, -g_lo, 0.0)))
        decay.append(jnp.exp(jnp.where(causal > 0.5, _dot_nt(lhs.astype(BF16), rhs.astype(BF16)), MASKED)))

    kb = [k[h] * beta[h] for h in heads]
    k_b = [x.astype(BF16) for x in k]
    lmat = [strict * _dot_nt(kb[h].astype(BF16), k_b[h]) * decay[h] for h in heads]
    tinv = [eye - x for x in lmat]
    pw = lmat
    for _ in range(CH.bit_length() - 2):
        pw = [jnp.dot(x.astype(BF16), x.astype(BF16), preferred_element_type=F32) for x in pw]
        tinv = [tinv[h] + jnp.dot(tinv[h].astype(BF16), pw[h].astype(BF16), preferred_element_type=F32)
                for h in heads]
    eg = [jnp.exp(x) for x in g]
    sol = [jnp.dot(tinv[h].astype(BF16),
                   jnp.concatenate([v[h] * beta[h], kb[h] * eg[h]], axis=1).astype(BF16),
                   preferred_element_type=F32) for h in heads]
    attn = [(causal * _dot_nt(q[h].astype(BF16), k_b[h]) * decay[h]).astype(BF16) for h in heads]
    q_st = [(q[h] * eg[h]).astype(BF16) for h in heads]

    vnew_ref[...] = jnp.zeros_like(vnew_ref)
    for c in range(TC // CH):
        rows = slice(c * CH, (c + 1) * CH)
        g_last = [x[(c + 1) * CH - 1:(c + 1) * CH, :] for x in g]
        st = [state_ref[h] for h in heads]
        st_b = [x.astype(BF16) for x in st]
        v_new = [sol[h][rows, :GDN_DV]
                 - jnp.dot(sol[h][rows, GDN_DV:].astype(BF16), st_b[h], preferred_element_type=F32)
                 for h in heads]
        for h in heads:
            vnew_ref[h, rows, :] = v_new[h].astype(BF16)
        o_c = [jnp.dot(q_st[h][rows], st_b[h], preferred_element_type=F32)
               + jnp.dot(attn[h][rows], vnew_ref[h], preferred_element_type=F32) for h in heads]
        for h in heads:
            k_st = (k[h][rows] * jnp.exp(g_last[h] - g[h][rows])).astype(BF16)
            state_ref[h] = st[h] * jnp.exp(g_last[h][:, :1]) + _dot_tn(k_st, v_new[h].astype(BF16))
        for h in heads:
            o = o_c[h] * lax.rsqrt(jnp.mean(o_c[h] * o_c[h], axis=-1, keepdims=True) + RMS_EPS) * ng_ref[...]
            zc = z_ref[rows, h * LANES:(h + 1) * LANES]
            o_ref[rows, h * LANES:(h + 1) * LANES] = o * (zc * jax.nn.sigmoid(zc))


def gdn_mixer(proj, bs, conv_w, a_log, dt_bias, norm_g):
    t = proj.shape[0]
    s = t // bs
    nt = s // GDN_TC
    H = GDN_HEADS
    hw = H * LANES
    assert s % GDN_TC == 0 and GDN_DK == LANES and GDN_DV == LANES and OFF_QKV == 0 and OFF_Z % hw == 0
    cw = jnp.zeros((8, GDN_QKV), F32).at[:GDN_CONV].set(conv_w)
    tok = lambda blk: pl.BlockSpec((GDN_TC, hw), lambda b, i: (b * nt + i, blk))
    wspec = lambda blk: pl.BlockSpec((8, hw), lambda b, i: (0, blk))
    smem = pl.BlockSpec(memory_space=pltpu.SMEM)
    return pl.pallas_call(
        _gdn_body,
        grid=(bs, nt),
        in_specs=[smem, smem, tok(0), tok(1), tok(2), tok(OFF_Z // hw),
                  pl.BlockSpec((GDN_TC, 2 * LANES), lambda b, i: (b * nt + i, OFF_SMALL // (2 * LANES))),
                  wspec(0), wspec(1), wspec(2), pl.BlockSpec((1, GDN_DV), lambda b, i: (0, 0))],
        out_specs=pl.BlockSpec((GDN_TC, hw), lambda b, i: (b * nt + i, 0)),
        out_shape=jax.ShapeDtypeStruct((t, hw), F32),
        scratch_shapes=[pltpu.VMEM((H, GDN_DK, GDN_DV), F32)] + [pltpu.VMEM((GDN_HALO, hw), F32)] * 3
        + [pltpu.VMEM((H, GDN_TC, GDN_DV), BF16)],
        compiler_params=_params("parallel", "arbitrary"),
        name="gdn_delta",
    )(-jnp.exp(a_log), dt_bias.astype(F32), proj, proj, proj, proj, proj, cw, cw, cw,
      norm_g.reshape(1, GDN_DV))


NSA_KT = 256
NSA_NEG = MASKED
NSA_WTILES = NSA_WINDOW // LANES + 1


def _bucket_table():
    n = np.arange(LANES)
    max_exact = REL_BUCKETS // 2
    nf = np.maximum(n, 1).astype(np.float32)
    large = max_exact + (np.log(nf / np.float32(max_exact)) / np.float32(math.log(REL_MAX_DIST / max_exact))
                         * np.float32(REL_BUCKETS - max_exact)).astype(np.int32)
    tbl = np.where(n < max_exact, n, np.minimum(large, REL_BUCKETS - 1))
    assert tbl[-1] == REL_BUCKETS - 1 and REL_MAX_DIST <= LANES
    return tbl


def _compress_body(x_ref, pe_ref, w1_ref, w2_ref, o_ref):
    x = x_ref[0, 0, 0]
    w1 = w1_ref[0]
    half = NSA_CMP_STRIDE * NSA_DH
    nc = x.shape[0]
    a = jnp.dot(x, w1[:half], preferred_element_type=F32)
    b = jnp.dot(x, w1[half:], preferred_element_type=F32)
    pe_h = jnp.dot(pe_ref[0], w1, preferred_element_type=F32)
    hid = a + pltpu.roll(b, nc - 1, 0) + pe_h[0:1, :]
    o_ref[0, 0, 0] = jnp.dot(jax.nn.gelu(hid).astype(BF16), w2_ref[0], preferred_element_type=F32)


def nsa_compress(x, pe, w1, w2):
    _, bs, g, nc, width = x.shape
    return pl.pallas_call(
        _compress_body,
        grid=(2, bs, g),
        in_specs=[pl.BlockSpec((1, 1, 1, nc, width), lambda i, b, j: (i, b, j, 0, 0)),
                  pl.BlockSpec((1,) + pe.shape[1:], lambda i, b, j: (i, 0, 0)),
                  pl.BlockSpec((1,) + w1.shape[1:], lambda i, b, j: (i, 0, 0)),
                  pl.BlockSpec((1,) + w2.shape[1:], lambda i, b, j: (i, 0, 0))],
        out_specs=pl.BlockSpec((1, 1, 1, nc, NSA_DH), lambda i, b, j: (i, b, j, 0, 0)),
        out_shape=jax.ShapeDtypeStruct((2, bs, g, nc, NSA_DH), F32),
        compiler_params=_params("parallel", "parallel", "parallel"),
        name="nsa_compress",
    )(x, pe, w1, w2)


def _nsa_body(nfast, wstd, near_w, near_c, q_ref, gl_ref, qp_ref, kp_ref, cp_ref,
              ks_ref, vs_ref, kw_ref, vw_ref, kc_ref, vc_ref, agg_ref, td_ref, o_ref,
              m_ref, l_ref, acc_ref, *, n_sel, sel_k):
    QB, HG, G = NSA_QBLOCK, NSA_HPG, NSA_KV_GROUPS
    groups = range(G)
    qi = pl.program_id(1)
    s0 = qi * QB
    nc = kc_ref.shape[2]
    n_ct = nc // LANES
    q = [q_ref[0, g, 0] for g in groups]
    qp = qp_ref[...]

    def lanes4(x):
        return jnp.concatenate([x] * HG, axis=1)

    def delta_t(kp):
        idx = jnp.clip(_dot_nt(kp, qp), 0.0, float(LANES - 1)).astype(jnp.int32)
        outs = []
        for g in groups:
            heads = []
            for hg in range(HG):
                tb = jnp.broadcast_to(td_ref[g * HG + hg:g * HG + hg + 1, :], idx.shape)
                heads.append(jnp.take_along_axis(tb, idx, axis=1))
            outs.append(jnp.concatenate(heads, axis=1))
        return outs

    def maybe_delta(flag, kp):
        zeros = jnp.zeros((kp.shape[0], HG * QB), F32)
        return lax.cond(flag != 0, lambda: tuple(delta_t(kp)), lambda: (zeros,) * G)

    gate = [jax.nn.sigmoid(gl_ref[0, g, 0]) for g in groups]

    sc = [_dot_nt(kc_ref[0, g], q[g]) for g in groups]
    dl = [maybe_delta(near_c[qi * n_ct + ct], cp_ref[ct * LANES:(ct + 1) * LANES, :]) for ct in range(n_ct)]
    c_id = lax.broadcasted_iota(jnp.int32, (nc, QB), 0)
    c_q = lax.broadcasted_iota(jnp.int32, (nc, QB), 1)
    ok_c = lanes4(jnp.where(c_id * NSA_CMP_STRIDE + (NSA_CMP_LEN - 1) <= s0 + c_q, 1.0, 0.0))
    blk = lax.broadcasted_iota(jnp.int32, (LANES, QB), 0)
    t_tok = s0 + lax.broadcasted_iota(jnp.int32, (LANES, QB), 1)
    tb_blk = t_tok // NSA_SEL_LEN
    forced = jnp.where(blk == 0, 1.0, 0.0) + jnp.where(blk == tb_blk, 1.0, 0.0) \
        + jnp.where(blk == tb_blk - 1, 1.0, 0.0)
    blkf = blk.astype(F32)
    out, score = [], []
    for g in groups:
        s = sc[g] + jnp.concatenate([d[g] for d in dl], axis=0)
        s = jnp.where(ok_c > 0.5, s, NSA_NEG)
        e = jnp.exp2(s - jnp.max(s, axis=0, keepdims=True)) * ok_c
        p = e * (1.0 / jnp.maximum(jnp.sum(e, axis=0, keepdims=True), 1e-30))
        out.append(gate[g][0:1, :] * jnp.dot(vc_ref[0, g], p.astype(BF16), preferred_element_type=F32))
        ps = p[:, 0:QB]
        for hg in range(1, HG):
            ps = ps + p[:, hg * QB:(hg + 1) * QB]
        ps_hi, ps_lo = _split_bf16(ps)
        imp = (jnp.dot(agg_ref[...], ps_hi, preferred_element_type=F32)
               + jnp.dot(agg_ref[...], ps_lo, preferred_element_type=F32))
        sco = jnp.where(forced > 0.5, 1e9, jnp.where(blk * NSA_SEL_LEN <= t_tok, imp, -1e9))
        score.append(jnp.where(blk < n_sel, sco, -jnp.inf))
    sel = [jnp.zeros((LANES, QB), F32) for _ in groups]
    for _ in range(sel_k):
        for g in groups:
            mx = jnp.max(score[g], axis=0, keepdims=True)
            first = jnp.min(jnp.where(score[g] == mx, blkf, float(LANES)), axis=0, keepdims=True)
            hit = blkf == first
            sel[g] = jnp.where(hit, 1.0, sel[g])
            score[g] = jnp.where(hit, -jnp.inf, score[g])
    q2 = []
    for g in groups:
        sel_q = jnp.transpose(jnp.where(sel[g] > 0.5, 0.0, NSA_NEG))
        q2.append(jnp.concatenate([q[g], jnp.concatenate([sel_q] * HG, axis=0).astype(BF16)], axis=1))

    def reset():
        m_ref[...] = jnp.full_like(m_ref, NSA_NEG)
        l_ref[...] = jnp.zeros_like(l_ref)
        acc_ref[...] = jnp.zeros_like(acc_ref)

    def tile_step(g, k, s, v_t):
        m_old = m_ref[g, k]
        m_new = jnp.maximum(m_old, jnp.max(s, axis=0, keepdims=True))
        alpha = jnp.exp2(m_old - m_new)
        p = jnp.exp2(s - m_new)
        l_ref[g, k] = l_ref[g, k] * alpha + jnp.sum(p, axis=0, keepdims=True)
        acc_ref[g, k] = acc_ref[g, k] * alpha + jnp.dot(v_t, p.astype(BF16), preferred_element_type=F32)
        m_ref[g, k] = m_new

    def finish(g):
        m = jnp.maximum(m_ref[g, 0], m_ref[g, 1])
        w0, w1 = jnp.exp2(m_ref[g, 0] - m), jnp.exp2(m_ref[g, 1] - m)
        return (acc_ref[g, 0] * w0 + acc_ref[g, 1] * w1) * (1.0 / (l_ref[g, 0] * w0 + l_ref[g, 1] * w1))

    key_r = lax.broadcasted_iota(jnp.int32, (NSA_KT, QB), 0)
    key_q = lax.broadcasted_iota(jnp.int32, (NSA_KT, QB), 1)

    def sel_scores(kt, with_delta, diag):
        k0 = pl.multiple_of(kt * NSA_KT, NSA_KT)
        s = [_dot_nt(ks_ref[0, g, pl.ds(k0, NSA_KT), :], q2[g]) for g in groups]
        if with_delta:
            d = delta_t(kp_ref[pl.ds(k0, NSA_KT), :])
            s = [s[g] + d[g] for g in groups]
        if diag:
            causal = lanes4(jnp.where(k0 + key_r <= s0 + key_q, 0.0, NSA_NEG))
            s = [x + causal for x in s]
        return [(s[g], vs_ref[0, g, :, pl.ds(k0, NSA_KT)]) for g in groups]

    reset()
    n_past = (s0 + QB - 1) // NSA_KT
    n_pairs = nfast[qi] // 2

    def fast_body(i, carry):
        a = sel_scores(2 * i, False, False)
        b = sel_scores(2 * i + 1, False, False)
        for g in groups:
            tile_step(g, 0, *a[g])
        for g in groups:
            tile_step(g, 1, *b[g])
        return carry

    def slow_body(kt, carry):
        a = sel_scores(kt, True, False)
        for g in groups:
            tile_step(g, 1, *a[g])
        return carry

    lax.fori_loop(0, n_pairs, fast_body, 0)
    lax.fori_loop(2 * n_pairs, n_past, slow_body, 0)
    a = sel_scores(n_past, True, True)
    for g in groups:
        tile_step(g, 0, *a[g])
    out = [out[g] + gate[g][1:2, :] * finish(g) for g in groups]

    reset()
    w_r = lax.broadcasted_iota(jnp.int32, (LANES, QB), 0)
    w_q = lax.broadcasted_iota(jnp.int32, (LANES, QB), 1)
    in_window = lanes4(jnp.where(w_r > w_q, 0.0, NSA_NEG))
    causal_w = lanes4(jnp.where(w_r <= w_q, 0.0, NSA_NEG))

    def win_scores(g, st, n):
        st = pl.multiple_of(st, LANES)
        return _dot_nt(kw_ref[0, g, pl.ds(st, n), :], q[g]), vw_ref[0, g, :, pl.ds(st, n)]

    @pl.when(wstd[qi] != 0)
    def _():
        half = NSA_WINDOW // 2
        zeros = jnp.zeros((LANES, HG * QB), F32)
        sa = [win_scores(g, s0 - NSA_WINDOW, half) for g in groups]
        sb = [win_scores(g, s0 - half, half) for g in groups]
        sd = [win_scores(g, s0, LANES) for g in groups]
        near = delta_t(kp_ref[pl.ds(pl.multiple_of(s0 - LANES, LANES), LANES), :])
        diag = delta_t(kp_ref[pl.ds(pl.multiple_of(s0, LANES), LANES), :])
        for g in groups:
            tile_step(g, 0, sa[g][0] + jnp.concatenate([in_window, zeros], axis=0), sa[g][1])
        for g in groups:
            tile_step(g, 1, sb[g][0] + jnp.concatenate([zeros, near[g]], axis=0), sb[g][1])
        for g in groups:
            tile_step(g, 0, sd[g][0] + diag[g] + causal_w, sd[g][1])

    for j in range(NSA_WTILES):
        start = s0 - NSA_WINDOW + j * LANES

        @pl.when((wstd[qi] == 0) & (start >= 0))
        def _():
            sw = [win_scores(g, start, LANES) for g in groups]
            d = maybe_delta(near_w[qi * NSA_WTILES + j],
                            kp_ref[pl.ds(pl.multiple_of(start, LANES), LANES), :])
            for g in groups:
                s = sw[g][0] + d[g]
                if j == 0:
                    s = s + in_window
                elif j == NSA_WTILES - 1:
                    s = s + causal_w
                tile_step(g, j % 2, s, sw[g][1])

    heads_out = []
    for g in groups:
        o_t = out[g] + gate[g][2:3, :] * finish(g)
        heads_out += [jnp.transpose(o_t[:, hg * QB:(hg + 1) * QB]) for hg in range(HG)]
    o_ref[0] = jnp.concatenate(heads_out, axis=1)


def _pos_digits(p, key_side):
    d0, d1, d2 = (p & 127).astype(F32), ((p >> 7) & 127).astype(F32), (p >> 14).astype(F32)
    one = jnp.ones_like(d0)
    if key_side:
        cols = [one, one, one, -d2, -d1, -d0]
    else:
        cols = [16384.0 * d2, 128.0 * d1, d0, 16384.0 * one, 128.0 * one, one]
    out = jnp.stack(cols, axis=-1)
    return jnp.pad(out, ((0, 0), (0, LANES - out.shape[-1]))).astype(BF16)


def nsa_mixer(q, k_c, v_c, k_s, v_s, k_w, v_w, gate_logits, positions,
              ck_pe, ck_w1, ck_w2, cv_pe, cv_w1, cv_w2, rel_bias):
    q, k_c, v_c, k_s, v_s, k_w, v_w = lax.optimization_barrier((q, k_c, v_c, k_s, v_s, k_w, v_w))
    bs, s, _ = q.shape
    G, HG, DH, QB = NSA_KV_GROUPS, NSA_HPG, NSA_DH, NSA_QBLOCK
    nqt = s // QB
    nc = s // NSA_CMP_STRIDE
    n_cmp = (s - NSA_CMP_LEN) // NSA_CMP_STRIDE + 1
    n_sel = s // NSA_SEL_LEN
    sel_k = min(NSA_SEL_TOPK, n_sel)
    n_kt = s // NSA_KT
    n_ct = nc // LANES
    assert s % (LANES * NSA_CMP_STRIDE) == 0 and n_sel <= LANES and NSA_KT == 2 * QB

    def by_group(t):
        return t.reshape(bs, s, G, DH).transpose(0, 2, 1, 3)

    def by_group_t(t):
        return t.reshape(bs, s, G, DH).transpose(0, 2, 3, 1).astype(BF16)

    ones2 = jnp.zeros((LANES - DH,), F32).at[:2].set(1.0)

    def keys_aug(t, blocks=False):
        parts = [t, jnp.broadcast_to(ones2, t.shape[:-1] + (LANES - DH,))]
        if blocks:
            onehot = (jnp.arange(s)[:, None] // NSA_SEL_LEN == jnp.arange(LANES)[None, :]).astype(F32)
            parts.append(jnp.broadcast_to(onehot, t.shape[:-2] + (s, LANES)))
        return jnp.concatenate(parts, axis=-1).astype(BF16)

    chunks = jnp.stack([by_group(k_c), by_group(v_c)]).reshape(2, bs, G, nc, NSA_CMP_STRIDE * DH)
    pe = jnp.stack([ck_pe, cv_pe]).reshape(2, 1, NSA_CMP_LEN * DH)
    cmp = nsa_compress(chunks.astype(BF16), jnp.broadcast_to(pe, (2, 8, NSA_CMP_LEN * DH)).astype(BF16),
                       jnp.stack([ck_w1, cv_w1]).astype(BF16), jnp.stack([ck_w2, cv_w2]).astype(BF16))
    kc, vc_t = keys_aug(cmp[0]), cmp[1].transpose(0, 1, 3, 2).astype(BF16)

    log2e = math.log2(math.e)
    rel_bias = rel_bias.astype(F32) * log2e
    far = rel_bias[REL_BUCKETS - 1]
    far_hi = far.astype(BF16).astype(F32)
    extra = jnp.zeros((NSA_HEADS, LANES - DH), F32).at[:, 0].set(far_hi).at[:, 1].set(far - far_hi)
    qh = (q * (DH ** -0.5 * log2e)).reshape(bs, nqt, QB, G, HG, DH).transpose(0, 3, 1, 4, 2, 5)
    ex = jnp.broadcast_to(extra.reshape(1, G, 1, HG, 1, LANES - DH), (bs, G, nqt, HG, QB, LANES - DH))
    qs = jnp.concatenate([qh, ex], axis=-1).astype(BF16).reshape(bs, G, nqt, HG * QB, LANES)
    gl = gate_logits.reshape(bs, nqt, QB, G, HG, 3).transpose(0, 3, 1, 5, 4, 2)
    gl = jnp.pad(gl.reshape(bs, G, nqt, 3, HG * QB), ((0, 0),) * 3 + ((0, 5), (0, 0)))

    td = (rel_bias[_bucket_table()] - rel_bias[REL_BUCKETS - 1][None, :]).T.astype(F32)
    cmp_end = jnp.minimum(jnp.arange(nc) * NSA_CMP_STRIDE + NSA_CMP_LEN - 1, s - 1)
    cpos = positions[cmp_end]
    qmin = positions.reshape(nqt, QB).min(axis=1)
    kmax = positions.reshape(nqt, QB).max(axis=1)
    near_s = (qmin[:, None] - positions.reshape(n_kt, NSA_KT).max(axis=1)[None, :]) < REL_MAX_DIST
    n_past = (jnp.arange(nqt) * QB + QB - 1) // NSA_KT
    n_fast = jnp.minimum(jnp.argmax(jnp.concatenate([near_s, jnp.ones((nqt, 1), bool)], axis=1), axis=1),
                         n_past)
    wt = jnp.clip(jnp.arange(nqt)[:, None] - (NSA_WTILES - 1) + jnp.arange(NSA_WTILES)[None, :], 0, nqt - 1)
    near_w = (qmin[:, None] - kmax[wt]) < REL_MAX_DIST
    near_c = (qmin[:, None] - cpos.reshape(n_ct, LANES).max(axis=1)[None, :]) < REL_MAX_DIST
    w_std = ((jnp.arange(nqt) >= NSA_WTILES - 1)
             & jnp.all(near_w == (jnp.arange(NSA_WTILES) >= NSA_WTILES - 2)[None, :], axis=1))

    cmp_start = np.arange(nc) * NSA_CMP_STRIDE
    sel_start = np.arange(LANES) * NSA_SEL_LEN
    agg_t = ((cmp_start[None, :] <= sel_start[:, None] + NSA_SEL_LEN - 1)
             & (cmp_start[None, :] + NSA_CMP_LEN - 1 >= sel_start[:, None])
             & (np.arange(nc)[None, :] < n_cmp) & (np.arange(LANES)[:, None] < n_sel))

    cols = HG * QB
    full = lambda shape: pl.BlockSpec(shape, lambda b, i, *_: (0,) * len(shape))
    per_b = lambda n, w: pl.BlockSpec((1, G, n, w), lambda b, i, *_: (b, 0, 0, 0))
    per_tile = lambda n, w: pl.BlockSpec((1, G, 1, n, w), lambda b, i, *_: (b, 0, i, 0, 0))
    grid_spec = pltpu.PrefetchScalarGridSpec(
        num_scalar_prefetch=4,
        grid=(bs, nqt),
        in_specs=[per_tile(cols, LANES), per_tile(8, cols),
                  pl.BlockSpec((QB, LANES), lambda b, i, *_: (i, 0)),
                  full((s, LANES)), full((nc, LANES)),
                  per_b(s, 2 * LANES), per_b(DH, s), per_b(s, LANES), per_b(DH, s),
                  per_b(nc, LANES), per_b(DH, nc),
                  full((LANES, nc)), full((NSA_HEADS, LANES))],
        out_specs=pl.BlockSpec((1, QB, NSA_Q), lambda b, i, *_: (b, i, 0)),
        scratch_shapes=[pltpu.VMEM((G, 2, 1, cols), F32), pltpu.VMEM((G, 2, 1, cols), F32),
                        pltpu.VMEM((G, 2, DH, cols), F32)],
    )
    return pl.pallas_call(
        functools.partial(_nsa_body, n_sel=n_sel, sel_k=sel_k),
        grid_spec=grid_spec,
        out_shape=jax.ShapeDtypeStruct((bs, s, NSA_Q), F32),
        compiler_params=_params("parallel", "arbitrary"),
        name="nsa_attention",
    )(n_fast.astype(jnp.int32), w_std.astype(jnp.int32), near_w.reshape(-1).astype(jnp.int32),
      near_c.reshape(-1).astype(jnp.int32),
      qs, gl, _pos_digits(positions, False), _pos_digits(positions, True), _pos_digits(cpos, True),
      keys_aug(by_group(k_s), blocks=True), by_group_t(v_s), keys_aug(by_group(k_w)), by_group_t(v_w),
      kc, vc_t, jnp.asarray(agg_t, BF16), td)


SGU_TC = 512


def _sgu_body(uv_ref, lg_ref, lb_ref, w_ref, b_ref, o_ref):
    uv = jax.nn.gelu(uv_ref[...])
    u, v = uv[:, :SGU_WIDTH], uv[:, SGU_WIDTH:]
    mu = jnp.mean(v, axis=-1, keepdims=True)
    vc = v - mu
    var = jnp.mean(vc * vc, axis=-1, keepdims=True)
    vn = (vc * lax.rsqrt(var + RMS_EPS) * lg_ref[...] + lb_ref[...]).astype(BF16)
    gw = SGU_WIDTH // SGU_GROUPS
    for c in range(SGU_TC // SGU_CHUNK):
        rows = slice(c * SGU_CHUNK, (c + 1) * SGU_CHUNK)
        for g in range(SGU_GROUPS):
            cols = slice(g * gw, (g + 1) * gw)
            mix = jnp.dot(w_ref[g], vn[rows, cols], preferred_element_type=F32) + b_ref[:, g:g + 1]
            o_ref[rows, cols] = u[rows, cols] * mix


def sgu_mixer(proj, ln_g, ln_b, w_s, b_s):
    t = proj.shape[0]
    assert t % SGU_TC == 0 and OFF_SGU % (2 * SGU_WIDTH) == 0
    fixed = lambda i: (0, 0)
    return pl.pallas_call(
        _sgu_body,
        grid=(t // SGU_TC,),
        in_specs=[pl.BlockSpec((SGU_TC, 2 * SGU_WIDTH), lambda i: (i, OFF_SGU // (2 * SGU_WIDTH))),
                  pl.BlockSpec((1, SGU_WIDTH), fixed), pl.BlockSpec((1, SGU_WIDTH), fixed),
                  pl.BlockSpec((SGU_GROUPS, SGU_CHUNK, SGU_CHUNK), lambda i: (0, 0, 0)),
                  pl.BlockSpec((SGU_CHUNK, SGU_GROUPS), fixed)],
        out_specs=pl.BlockSpec((SGU_TC, SGU_WIDTH), lambda i: (i, 0)),
        out_shape=jax.ShapeDtypeStruct((t, SGU_WIDTH), F32),
        compiler_params=_params("parallel"),
        name="sgu_gate",
    )(proj, ln_g.reshape(1, SGU_WIDTH), ln_b.reshape(1, SGU_WIDTH), jnp.tril(w_s).astype(BF16), b_s.T)


def _permute_w_in(w):
    sizes = ([SSM_WIDTH, GDN_QKV, GDN_HEADS, GDN_HEADS, GDN_HEADS * GDN_DV, NSA_Q]
             + [NSA_KV] * 6 + [3 * NSA_HEADS, 2 * SGU_WIDTH, N_BRANCH * D_MODEL])
    cuts = [int(c) for c in np.cumsum(sizes)[:-1]]
    (w_ssm, w_qkv, w_a, w_b, w_z, w_nq, w_kc, w_vc, w_ks, w_vs, w_kw, w_vw,
     w_ng, w_sgu, w_gate) = jnp.split(w, cuts, axis=-1)
    pad = jnp.zeros((w.shape[0], PROJ_COLS - OFF_SMALL - SMALL_USED), w.dtype)
    return jnp.concatenate([w_qkv, w_ssm, w_z, w_nq, w_sgu, w_gate, w_kc, w_vc, w_ks, w_vs,
                            w_kw, w_vw, w_a, w_b, w_ng, pad], axis=-1)


def kernel(x, positions, norm_mix, norm_ffn, norm_final, w_in, ssm_a_re, ssm_a_im, ssm_log_dt, ssm_b_re, ssm_b_im, ssm_c_re, ssm_c_im, ssm_d, ssm_glu_w, gdn_conv_w, gdn_a_log, gdn_dt_bias, gdn_norm, nsa_cmp_k_pe, nsa_cmp_k_w1, nsa_cmp_k_w2, nsa_cmp_v_pe, nsa_cmp_v_w1, nsa_cmp_v_w2, rel_bias, sgu_ln_g, sgu_ln_b, sgu_w, sgu_b, w_br_ssm, w_br_gdn, w_br_nsa, w_br_sgu, w_out, ffn_w_gate, ffn_w_up, ffn_w_down, moe_router, moe_w_gate, moe_w_up, moe_w_down):
    bs, s, d = x.shape
    t = bs * s
    assert DEPTH == 2
    xf = x.reshape(t, d)
    for layer in range(DEPTH):
        proj = in_proj(xf, norm_mix[layer], _permute_w_in(w_in[layer]))
        p3 = proj.reshape(bs, s, PROJ_COLS)

        def seg(off, width):
            return p3[..., off:off + width]

        y_ssm = ssm_mixer(seg(OFF_SSM, SSM_WIDTH), ssm_a_re[layer], ssm_a_im[layer],
                          ssm_log_dt[layer], ssm_b_re[layer], ssm_b_im[layer], ssm_c_re[layer],
                          ssm_c_im[layer], ssm_d[layer], ssm_glu_w[layer])
        y_gdn = gdn_mixer(proj, bs, gdn_conv_w[layer], gdn_a_log[layer], gdn_dt_bias[layer],
                          gdn_norm[layer])
        kvs = [seg(OFF_KV + i * NSA_KV, NSA_KV) for i in range(6)]
        y_nsa = nsa_mixer(seg(OFF_NQ, NSA_Q), *kvs, seg(OFF_SMALL + 2 * GDN_HEADS, 3 * NSA_HEADS),
                          positions, nsa_cmp_k_pe[layer], nsa_cmp_k_w1[layer], nsa_cmp_k_w2[layer],
                          nsa_cmp_v_pe[layer], nsa_cmp_v_w1[layer], nsa_cmp_v_w2[layer], rel_bias)
        y_sgu = sgu_mixer(proj, sgu_ln_g[layer], sgu_ln_b[layer], sgu_w[layer], sgu_b[layer])
        ys = [y_ssm.reshape(t, -1), y_gdn, y_nsa.reshape(t, -1), y_sgu]
        ws = [w[layer].astype(BF16) for w in (w_br_ssm, w_br_gdn, w_br_nsa, w_br_sgu)]
        xf = merge(xf, proj, ys, ws, w_out[layer].astype(BF16))
        i = layer // 2
        if layer % 2 == 0:
            xf = dense_ffn(xf, norm_ffn[layer], ffn_w_gate[i], ffn_w_up[i], ffn_w_down[i])
        else:
            xf = moe_layer(xf, norm_ffn[layer], moe_router[i], moe_w_gate[i], moe_w_up[i],
                           moe_w_down[i], norm_final)
    return xf.reshape(bs, s, d)
```

```python
import functools
import math

import jax
import jax.numpy as jnp
from jax import lax
import numpy as np
from jax.experimental import pallas as pl
from jax.experimental.pallas import tpu as pltpu

F32 = jnp.float32
BF16 = jnp.bfloat16

D_MODEL = 1024
DEPTH = 2
SSM_WIDTH = D_MODEL // 2
SSM_GROUP = 16
SSM_GROUPS = SSM_WIDTH // SSM_GROUP
SSM_STATE = 64
GDN_HEADS = 4
GDN_DK = 128
GDN_DV = 128
GDN_CONV = 4
GDN_CHUNK = 64
NSA_HEADS = 8
NSA_KV_GROUPS = 2
NSA_HPG = NSA_HEADS // NSA_KV_GROUPS
NSA_DH = 64
NSA_CMP_LEN = 32
NSA_CMP_STRIDE = 16
NSA_CMP_HIDDEN = 128
NSA_SEL_LEN = 64
NSA_SEL_TOPK = 16
NSA_WINDOW = 512
NSA_QBLOCK = 128
SGU_WIDTH = D_MODEL // 2
SGU_GROUPS = 4
SGU_CHUNK = 128
REL_BUCKETS = 32
REL_MAX_DIST = 128
FFN_DENSE = 2816
N_EXPERTS = 8
TOP_K = 2
FFN_EXPERT = 3584
N_BRANCH = 4
RMS_EPS = 1e-6
GDN_QKV = GDN_HEADS * (2 * GDN_DK + GDN_DV)
NSA_Q = NSA_HEADS * NSA_DH
NSA_KV = NSA_KV_GROUPS * NSA_DH

LANES = 128
VMEM_LIMIT = 48 * 1024 * 1024
MASKED = -1e30

OFF_QKV = 0
OFF_SSM = OFF_QKV + GDN_QKV
OFF_Z = OFF_SSM + SSM_WIDTH
OFF_NQ = OFF_Z + GDN_HEADS * GDN_DV
OFF_SGU = OFF_NQ + NSA_Q
OFF_GATE = OFF_SGU + 2 * SGU_WIDTH
OFF_KV = OFF_GATE + N_BRANCH * D_MODEL
OFF_SMALL = OFF_KV + 6 * NSA_KV
SMALL_USED = 2 * GDN_HEADS + 3 * NSA_HEADS
PROJ_COLS = 9216


def _params(*sem):
    return pltpu.CompilerParams(dimension_semantics=sem, vmem_limit_bytes=VMEM_LIMIT)


def _rms(x, g):
    return x * lax.rsqrt(jnp.mean(x * x, axis=-1, keepdims=True) + RMS_EPS) * g


def _dot_nt(a, b):
    return lax.dot_general(a, b, (((1,), (1,)), ((), ())), preferred_element_type=F32)


def _dot_tn(a, b):
    return lax.dot_general(a, b, (((0,), (0,)), ((), ())), preferred_element_type=F32)


def _split_bf16(x):
    hi = x.astype(BF16)
    return hi, (x - hi.astype(F32)).astype(BF16)


def _in_proj_body(x_ref, g_ref, w_ref, o_ref, h_ref):
    @pl.when(pl.program_id(1) == 0)
    def _():
        h_ref[...] = _rms(x_ref[...], g_ref[...]).astype(BF16)

    o_ref[...] = jnp.dot(h_ref[...], w_ref[...].astype(BF16), preferred_element_type=F32)


def in_proj(x, g, w, tm=2048, tn=512):
    t, d = x.shape
    n = w.shape[1]
    return pl.pallas_call(
        _in_proj_body,
        grid=(t // tm, n // tn),
        in_specs=[pl.BlockSpec((tm, d), lambda i, j: (i, 0)),
                  pl.BlockSpec((1, d), lambda i, j: (0, 0)),
                  pl.BlockSpec((d, tn), lambda i, j: (0, j))],
        out_specs=pl.BlockSpec((tm, tn), lambda i, j: (i, j)),
        out_shape=jax.ShapeDtypeStruct((t, n), F32),
        scratch_shapes=[pltpu.VMEM((tm, d), BF16)],
        compiler_params=_params("parallel", "arbitrary"),
        name="in_proj",
    )(x, g.reshape(1, d), w)


def _merge_body(x_ref, gate_ref, ya_ref, yb_ref, yc_ref, yd_ref,
                wa_ref, wb_ref, wc_ref, wd_ref, wo_ref, o_ref):
    def branch(k, y_ref, w_ref):
        p = jnp.dot(y_ref[...], w_ref[...], preferred_element_type=F32)
        return jax.nn.sigmoid(gate_ref[:, k * D_MODEL:(k + 1) * D_MODEL]) * p

    merged = (branch(0, ya_ref, wa_ref) + branch(1, yb_ref, wb_ref)
              + branch(2, yc_ref, wc_ref) + branch(3, yd_ref, wd_ref))
    o_ref[...] = x_ref[...] + jnp.dot(merged.astype(BF16), wo_ref[...],
                                      preferred_element_type=F32)


def merge(x, proj, ys, ws, w_out, tm=256):
    t, d = x.shape
    gate_blk = OFF_GATE // (N_BRANCH * D_MODEL)
    row = lambda i: (i, 0)
    fixed = lambda i: (0, 0)
    in_specs = [pl.BlockSpec((tm, d), row),
                pl.BlockSpec((tm, N_BRANCH * D_MODEL), lambda i: (i, gate_blk))]
    in_specs += [pl.BlockSpec((tm, y.shape[1]), row) for y in ys]
    in_specs += [pl.BlockSpec(w.shape, fixed) for w in ws]
    in_specs += [pl.BlockSpec(w_out.shape, fixed)]
    return pl.pallas_call(
        _merge_body,
        grid=(t // tm,),
        in_specs=in_specs,
        out_specs=pl.BlockSpec((tm, d), row),
        out_shape=jax.ShapeDtypeStruct((t, d), F32),
        compiler_params=_params("parallel"),
        name="merge",
    )(x, proj, *ys, *ws, w_out)


def _ffn_body(x_ref, g_ref, wg_ref, wu_ref, wd_ref, o_ref, h_ref, acc_ref):
    f = pl.program_id(1)

    @pl.when(f == 0)
    def _():
        h_ref[...] = _rms(x_ref[...], g_ref[...]).astype(BF16)
        acc_ref[...] = x_ref[...]

    h = h_ref[...]
    a = jnp.dot(h, wg_ref[...].astype(BF16), preferred_element_type=F32)
    u = jnp.dot(h, wu_ref[...].astype(BF16), preferred_element_type=F32)
    act = (a * jax.nn.sigmoid(a) * u).astype(BF16)
    acc_ref[...] += jnp.dot(act, wd_ref[...].astype(BF16), preferred_element_type=F32)

    @pl.when(f == pl.num_programs(1) - 1)
    def _():
        o_ref[...] = acc_ref[...]


def dense_ffn(x, g, wg, wu, wd, tm=1024, tf=256):
    t, d = x.shape
    nf = wg.shape[1]
    return pl.pallas_call(
        _ffn_body,
        grid=(t // tm, nf // tf),
        in_specs=[pl.BlockSpec((tm, d), lambda i, f: (i, 0)),
                  pl.BlockSpec((1, d), lambda i, f: (0, 0)),
                  pl.BlockSpec((d, tf), lambda i, f: (0, f)),
                  pl.BlockSpec((d, tf), lambda i, f: (0, f)),
                  pl.BlockSpec((tf, d), lambda i, f: (f, 0))],
        out_specs=pl.BlockSpec((tm, d), lambda i, f: (i, 0)),
        out_shape=jax.ShapeDtypeStruct((t, d), F32),
        scratch_shapes=[pltpu.VMEM((tm, d), BF16), pltpu.VMEM((tm, d), F32)],
        compiler_params=_params("parallel", "arbitrary"),
        name="dense_ffn",
    )(x, g.reshape(1, d), wg, wu, wd)


def _route_body(x_ref, g_ref, rhi_ref, rlo_ref, h_ref, idx_ref, wt_ref):
    h = _rms(x_ref[...], g_ref[...])
    hi = h.astype(BF16)
    lo = (h - hi.astype(F32)).astype(BF16)
    h_ref[...] = h
    logits = (jnp.dot(hi, rhi_ref[...], preferred_element_type=F32)
              + jnp.dot(hi, rlo_ref[...], preferred_element_type=F32)
              + jnp.dot(lo, rhi_ref[...], preferred_element_type=F32))
    col = lax.broadcasted_iota(jnp.int32, logits.shape, 1).astype(F32)
    neg = jnp.float32(-jnp.inf)
    lg = jnp.where(col < N_EXPERTS, logits, neg)
    m1 = jnp.max(lg, axis=-1, keepdims=True)
    i1 = jnp.min(jnp.where(lg == m1, col, float(LANES)), axis=-1, keepdims=True)
    lg2 = jnp.where(col == i1, neg, lg)
    m2 = jnp.max(lg2, axis=-1, keepdims=True)
    i2 = jnp.min(jnp.where(lg2 == m2, col, float(LANES)), axis=-1, keepdims=True)
    e = jnp.exp(m2 - m1)
    w1 = 1.0 / (1.0 + e)
    w2 = e / (1.0 + e)
    idx_ref[...] = jnp.where(col == 0, i1, jnp.where(col == 1, i2, 0.0)).astype(jnp.int32)
    wt_ref[...] = jnp.where(col == 0, w1, jnp.where(col == 1, w2, 0.0))


def moe_route(x, g, r_hi, r_lo, tm=512):
    t, d = x.shape
    row = lambda i: (i, 0)
    fixed = lambda i: (0, 0)
    return pl.pallas_call(
        _route_body,
        grid=(t // tm,),
        in_specs=[pl.BlockSpec((tm, d), row), pl.BlockSpec((1, d), fixed),
                  pl.BlockSpec((d, LANES), fixed), pl.BlockSpec((d, LANES), fixed)],
        out_specs=[pl.BlockSpec((tm, d), row), pl.BlockSpec((tm, LANES), row),
                   pl.BlockSpec((tm, LANES), row)],
        out_shape=[jax.ShapeDtypeStruct((t, d), F32),
                   jax.ShapeDtypeStruct((t, LANES), jnp.int32),
                   jax.ShapeDtypeStruct((t, LANES), F32)],
        compiler_params=_params("parallel"),
        name="moe_route",
    )(x, g.reshape(1, d), r_hi, r_lo)


def _experts_body(te_ref, nu_ref, src_ref, h_hbm, wg_ref, wu_ref, wd_ref, o_ref,
                  xbuf_ref, xs_ref, acc_ref, sem, *, rows_per_step):
    i = pl.program_id(0)
    f = pl.program_id(1)
    tm = xs_ref.shape[0]
    n_rows = xbuf_ref.shape[1]
    n_used = nu_ref[0]
    used = i < n_used
    has_next = i + 1 < n_used
    slot = i % 2

    def start_row(tile, slot_, r):
        tok = src_ref[tile * tm + r]
        pltpu.make_async_copy(h_hbm.at[pl.ds(tok, 1), :], xbuf_ref.at[slot_, pl.ds(r, 1), :],
                              sem.at[slot_]).start()

    @pl.when(f == 0)
    def _():
        acc_ref[...] = jnp.zeros_like(acc_ref)

        @pl.when(used)
        def _():
            @pl.when(i == 0)
            def _():
                def row(r, carry):
                    start_row(0, 0, r)
                    return carry

                lax.fori_loop(0, n_rows, row, 0, unroll=8)

            pltpu.make_async_copy(h_hbm.at[pl.ds(0, n_rows), :], xbuf_ref.at[slot], sem.at[slot]).wait()
            xs_ref[...] = xbuf_ref[slot, 0:tm, :].astype(BF16)

    def ffn_step():
        h = xs_ref[...]
        a = jnp.dot(h, wg_ref[0].astype(BF16), preferred_element_type=F32)
        u = jnp.dot(h, wu_ref[0].astype(BF16), preferred_element_type=F32)
        act = (a * jax.nn.sigmoid(a) * u).astype(BF16)
        acc_ref[...] += jnp.dot(act, wd_ref[0].astype(BF16), preferred_element_type=F32)

    @pl.when(used & has_next)
    def _():
        for j in range(rows_per_step):
            start_row(i + 1, 1 - slot, f * rows_per_step + j)
        ffn_step()

    @pl.when(used & jnp.logical_not(has_next))
    def _():
        ffn_step()

    @pl.when(f == pl.num_programs(1) - 1)
    def _():
        o_ref[...] = acc_ref[...].astype(o_ref.dtype)


def moe_experts(tile_expert, n_used, src, h, wg, wu, wd, tm, tf=512):
    d = h.shape[1]
    nf = wg.shape[2] // tf
    rows_per_step = 8 * (-(-tm // (8 * nf)))
    n_rows = rows_per_step * nf
    n_tiles = src.shape[0] // tm
    src = jnp.pad(src, (0, n_rows - tm))
    grid_spec = pltpu.PrefetchScalarGridSpec(
        num_scalar_prefetch=3,
        grid=(n_tiles, nf),
        in_specs=[pl.BlockSpec(memory_space=pl.ANY),
                  pl.BlockSpec((1, d, tf), lambda i, f, te, *_: (te[i], 0, f)),
                  pl.BlockSpec((1, d, tf), lambda i, f, te, *_: (te[i], 0, f)),
                  pl.BlockSpec((1, tf, d), lambda i, f, te, *_: (te[i], f, 0))],
        out_specs=pl.BlockSpec((tm, d), lambda i, f, *_: (i, 0)),
        scratch_shapes=[pltpu.VMEM((2, n_rows, d), F32), pltpu.VMEM((tm, d), BF16),
                        pltpu.VMEM((tm, d), F32), pltpu.SemaphoreType.DMA((2,))],
    )
    return pl.pallas_call(
        functools.partial(_experts_body, rows_per_step=rows_per_step),
        grid_spec=grid_spec,
        out_shape=jax.ShapeDtypeStruct((n_tiles * tm, d), BF16),
        compiler_params=_params("arbitrary", "arbitrary"),
        name="moe_experts",
    )(tile_expert, n_used, src, h, wg, wu, wd)


def _combine_norm_body(x_ref, ya_ref, yb_ref, wt_ref, g_ref, o_ref):
    wt = wt_ref[...]
    y = wt[:, 0:1] * ya_ref[...].astype(F32) + wt[:, 1:2] * yb_ref[...].astype(F32)
    o_ref[...] = _rms(x_ref[...] + y, g_ref[...])


def combine_norm(x, ya, yb, wts, g, tm=1024):
    t, d = x.shape
    row = lambda i: (i, 0)
    return pl.pallas_call(
        _combine_norm_body,
        grid=(t // tm,),
        in_specs=[pl.BlockSpec((tm, d), row)] * 3 + [pl.BlockSpec((tm, LANES), row),
                                                     pl.BlockSpec((1, d), lambda i: (0, 0))],
        out_specs=pl.BlockSpec((tm, d), row),
        out_shape=jax.ShapeDtypeStruct((t, d), F32),
        compiler_params=_params("parallel"),
        name="combine_norm",
    )(x, ya, yb, wts, g.reshape(1, d))


def moe_layer(x, g_norm, router, wg, wu, wd, g_final, tm=1024):
    t, d = x.shape
    r = jnp.zeros((d, LANES), F32).at[:, :N_EXPERTS].set(router)
    r_hi = r.astype(BF16)
    r_lo = (r - r_hi.astype(F32)).astype(BF16)
    h, idx, wts = moe_route(x, g_norm, r_hi, r_lo)
    e_flat = jnp.concatenate([idx[:, 0], idx[:, 1]])
    onehot = (e_flat[:, None] == jnp.arange(N_EXPERTS)[None, :]).astype(jnp.int32)
    csum = jnp.cumsum(onehot, axis=0)
    rank = jnp.sum((csum - onehot) * onehot, axis=1)
    counts = csum[-1]
    padded = ((counts + tm - 1) // tm) * tm
    ends = jnp.cumsum(padded)
    starts = ends - padded
    dest = starts[e_flat] + rank
    n_tiles = (TOP_K * t) // tm + N_EXPERTS
    p = n_tiles * tm
    tok = jnp.concatenate([jnp.arange(t, dtype=jnp.int32)] * TOP_K)
    src = jnp.zeros((p,), jnp.int32).at[dest].set(tok)
    tile_expert = jnp.minimum(
        jnp.searchsorted(ends, jnp.arange(n_tiles, dtype=jnp.int32) * tm, side="right"),
        N_EXPERTS - 1).astype(jnp.int32)
    n_used = (ends[-1] // tm).astype(jnp.int32).reshape(1)
    ys = moe_experts(tile_expert, n_used, src, h, wg, wu, wd, tm)
    ya = jnp.take(ys, dest[:t], axis=0, mode="clip")
    yb = jnp.take(ys, dest[t:], axis=0, mode="clip")
    return combine_norm(x, ya, yb, wts, g_final)


SSM_TC = 512
SSM_SUB = 128
SSM_HALF = 256
SSM_NSTATE = SSM_GROUPS * SSM_STATE


def _ssm_body(u_ref, bw_ref, cw_ref, d_ref, glu_ref, ar_ref, ai_ref, pr_ref, pi_ref, o_ref,
              xr_ref, xi_ref, cr_ref, ci_ref):
    @pl.when(pl.program_id(1) == 0)
    def _():
        cr_ref[...] = jnp.zeros_like(cr_ref)
        ci_ref[...] = jnp.zeros_like(ci_ref)

    u = u_ref[...]
    ub = u.astype(BF16)
    nblk = SSM_WIDTH // SSM_HALF
    sblk = SSM_NSTATE // nblk
    for k in range(nblk):
        bu = jnp.dot(ub[:, k * SSM_HALF:(k + 1) * SSM_HALF], bw_ref[k], preferred_element_type=F32)
        xr_ref[:, k * sblk:(k + 1) * sblk] = bu[:, :sblk]
        xi_ref[:, k * sblk:(k + 1) * sblk] = bu[:, sblk:]

    row8 = lax.broadcasted_iota(jnp.int32, (SSM_SUB, LANES), 0) % 8
    n_grp = SSM_SUB // 8

    def strip(c, carry):
        col = pl.ds(pl.multiple_of(c * LANES, LANES), LANES)
        c_r = cr_ref[:, col]
        c_i = ci_ref[:, col]
        p_r = pr_ref[0:8, col]
        p_i = pi_ref[0:8, col]
        step_mul = [(jnp.where(row8 >= (1 << i), ar_ref[i:i + 1, col], 0.0),
                     jnp.where(row8 >= (1 << i), ai_ref[i:i + 1, col], 0.0)) for i in range(3)]
        for sub in range(SSM_TC // SSM_SUB):
            rows = slice(sub * SSM_SUB, (sub + 1) * SSM_SUB)
            xr = xr_ref[rows, col]
            xi = xi_ref[rows, col]
            for i in range(3):
                a_r, a_i = step_mul[i]
                sr, si = pltpu.roll(xr, 1 << i, 0), pltpu.roll(xi, 1 << i, 0)
                xr, xi = xr + a_r * sr - a_i * si, xi + a_r * si + a_i * sr
            out_r, out_i = [], []
            for r in range(n_grp):
                g_r = xr[8 * r:8 * r + 8] + p_r * c_r - p_i * c_i
                g_i = xi[8 * r:8 * r + 8] + p_r * c_i + p_i * c_r
                c_r, c_i = g_r[7:8, :], g_i[7:8, :]
                out_r.append(g_r)
                out_i.append(g_i)
            xr_ref[rows, col] = jnp.concatenate(out_r, axis=0)
            xi_ref[rows, col] = jnp.concatenate(out_i, axis=0)
        cr_ref[:, col] = c_r
        ci_ref[:, col] = c_i
        return carry

    lax.fori_loop(0, SSM_NSTATE // LANES, strip, 0)

    ys = []
    for k in range(nblk):
        st = jnp.concatenate([xr_ref[:, k * sblk:(k + 1) * sblk].astype(BF16),
                              xi_ref[:, k * sblk:(k + 1) * sblk].astype(BF16)], axis=1)
        ys.append(jnp.dot(st, cw_ref[k], preferred_element_type=F32))
    y = jax.nn.gelu(jnp.concatenate(ys, axis=1) + d_ref[...] * u)
    gated = y * jax.nn.sigmoid(jnp.dot(y.astype(BF16), glu_ref[...], preferred_element_type=F32))
    o_ref[...] = gated.astype(o_ref.dtype)


def ssm_mixer(proj, bs, a_re, a_im, log_dt, b_re, b_im, c_re, c_im, d_skip, glu_w):
    t = proj.shape[0]
    s = t // bs
    nt = s // SSM_TC
    assert s % SSM_TC == 0 and OFF_SSM % SSM_WIDTH == 0
    nblk = SSM_WIDTH // SSM_HALF
    gpb = SSM_GROUPS // nblk
    dt = jnp.exp(log_dt)[:, None]
    mag = jnp.exp(a_re * dt)
    abar_r, abar_i = mag * jnp.cos(a_im * dt), mag * jnp.sin(a_im * dt)
    nr, ni = abar_r - 1.0, abar_i
    den = a_re * a_re + a_im * a_im
    sr, si = (nr * a_re + ni * a_im) / den, (ni * a_re - nr * a_im) / den
    bbar_r = sr[..., None] * b_re - si[..., None] * b_im
    bbar_i = sr[..., None] * b_im + si[..., None] * b_re

    def powers(k):
        kk = k.astype(F32)[:, None, None]
        m = jnp.exp(kk * (a_re * dt))
        return ((m * jnp.cos(kk * (a_im * dt))).reshape(-1, SSM_NSTATE),
                (m * jnp.sin(kk * (a_im * dt))).reshape(-1, SSM_NSTATE))

    ar, ai = powers(2 ** jnp.arange(8))
    pr, pi = powers(jnp.arange(1, SSM_SUB + 1))
    eye = jnp.eye(gpb, dtype=F32)

    def in_block(w):
        return jnp.einsum('gnp,gh->gphn', w, eye).reshape(gpb * SSM_GROUP, gpb * SSM_STATE)

    def out_block(w):
        return jnp.einsum('gpn,gh->gnhp', w, eye).reshape(gpb * SSM_STATE, gpb * SSM_GROUP)

    bw = jnp.stack([jnp.concatenate([in_block(bbar_r[k * gpb:(k + 1) * gpb]),
                                     in_block(bbar_i[k * gpb:(k + 1) * gpb])], axis=1)
                    for k in range(nblk)]).astype(BF16)
    cw = jnp.stack([jnp.concatenate([out_block(c_re[k * gpb:(k + 1) * gpb]),
                                     -out_block(c_im[k * gpb:(k + 1) * gpb])], axis=0)
                    for k in range(nblk)]).astype(BF16)
    fixed2 = lambda b, i: (0, 0)
    fixed3 = lambda b, i: (0, 0, 0)
    return pl.pallas_call(
        _ssm_body,
        grid=(bs, nt),
        in_specs=[pl.BlockSpec((SSM_TC, SSM_WIDTH), lambda b, i: (b * nt + i, OFF_SSM // SSM_WIDTH)),
                  pl.BlockSpec(bw.shape, fixed3), pl.BlockSpec(cw.shape, fixed3),
                  pl.BlockSpec((1, SSM_WIDTH), fixed2), pl.BlockSpec((SSM_WIDTH, SSM_WIDTH), fixed2),
                  pl.BlockSpec((8, SSM_NSTATE), fixed2), pl.BlockSpec((8, SSM_NSTATE), fixed2),
                  pl.BlockSpec((SSM_SUB, SSM_NSTATE), fixed2), pl.BlockSpec((SSM_SUB, SSM_NSTATE), fixed2)],
        out_specs=pl.BlockSpec((SSM_TC, SSM_WIDTH), lambda b, i: (b * nt + i, 0)),
        out_shape=jax.ShapeDtypeStruct((t, SSM_WIDTH), BF16),
        scratch_shapes=[pltpu.VMEM((SSM_TC, SSM_NSTATE), F32), pltpu.VMEM((SSM_TC, SSM_NSTATE), F32),
                        pltpu.VMEM((1, SSM_NSTATE), F32), pltpu.VMEM((1, SSM_NSTATE), F32)],
        compiler_params=_params("parallel", "arbitrary"),
        name="ssm_scan",
    )(proj, bw, cw, d_skip.reshape(1, SSM_WIDTH), glu_w.astype(BF16), ar, ai, pr, pi)


GDN_TC = 256
GDN_HALO = 8


def _gdn_body(nega_ref, dtb_ref, q_ref, k_ref, v_ref, z_ref, sm_ref, wq_ref, wk_ref, wv_ref, ng_ref,
              o_ref, state_ref, hq_ref, hk_ref, hv_ref, vnew_ref):
    TC, CH, DK, H = GDN_TC, GDN_CHUNK, GDN_DK, GDN_HEADS
    heads = range(H)

    @pl.when(pl.program_id(1) == 0)
    def _():
        state_ref[...] = jnp.zeros_like(state_ref)
        hq_ref[...] = jnp.zeros_like(hq_ref)
        hk_ref[...] = jnp.zeros_like(hk_ref)
        hv_ref[...] = jnp.zeros_like(hv_ref)

    def conv_silu(x_ref, halo_ref, w_ref):
        x = x_ref[...]
        xx = jnp.concatenate([halo_ref[...], x], axis=0)
        w = w_ref[...]
        y = w[GDN_CONV - 1:GDN_CONV] * x
        for d in range(1, GDN_CONV):
            y = y + w[GDN_CONV - 1 - d:GDN_CONV - d] * pltpu.roll(xx, d, 0)[GDN_HALO:GDN_HALO + TC]
        halo_ref[...] = x[TC - GDN_HALO:TC]
        return y * jax.nn.sigmoid(y)

    def per_head(x):
        return [x[:, h * LANES:(h + 1) * LANES] for h in heads]

    q = per_head(conv_silu(q_ref, hq_ref, wq_ref))
    k = per_head(conv_silu(k_ref, hk_ref, wk_ref))
    v = per_head(conv_silu(v_ref, hv_ref, wv_ref))
    q = [x * lax.rsqrt(jnp.sum(x * x, axis=-1, keepdims=True) + 1e-6) * (DK ** -0.5) for x in q]
    k = [x * lax.rsqrt(jnp.sum(x * x, axis=-1, keepdims=True) + 1e-6) for x in k]

    small = sm_ref[...]
    beta = [jax.nn.sigmoid(small[:, H + h:H + h + 1]) for h in heads]
    la = []
    for h in heads:
        xa = small[:, h:h + 1] + dtb_ref[h]
        softplus = jnp.maximum(xa, 0.0) + jnp.log(1.0 + jnp.exp(-jnp.abs(xa)))
        la.append(jnp.broadcast_to(nega_ref[h] * softplus, (TC, LANES)))

    ri = lax.broadcasted_iota(jnp.int32, (TC, TC), 0)
    ci = lax.broadcasted_iota(jnp.int32, (TC, TC), 1)
    same = (ri // CH) == (ci // CH)
    causal = jnp.where(same, jnp.where(ci <= ri, 1.0, 0.0), 0.0)
    strict = jnp.where(ci < ri, causal, 0.0)
    eye = jnp.where(ri == ci, 1.0, 0.0)
    tri = causal.astype(BF16)
    lane = lax.broadcasted_iota(jnp.int32, (TC, LANES), 1)

    g = []
    for h in heads:
        la_hi, la_lo = _split_bf16(la[h])
        g.append(jnp.dot(tri, la_hi, preferred_element_type=F32)
                 + jnp.dot(tri, la_lo, preferred_element_type=F32))
    decay = []
    for h in heads:
        g_hi = g[h].astype(BF16).astype(F32)
        g_lo = g[h] - g_hi
        lhs = jnp.where(lane == 0, g_hi, jnp.where(lane == 1, g_lo, jnp.where(lane < 4, 1.0, 0.0)))
        rhs = jnp.where(lane < 2, 1.0, jnp.where(lane == 2, -g_hi, jnp.where(lane == 3, -g_lo, 0.0)))
        decay.append(jnp.exp(jnp.where(causal > 0.5, _dot_nt(lhs.astype(BF16), rhs.astype(BF16)), MASKED)))

    kb = [k[h] * beta[h] for h in heads]
    k_b = [x.astype(BF16) for x in k]
    lmat = [strict * _dot_nt(kb[h].astype(BF16), k_b[h]) * decay[h] for h in heads]
    tinv = [eye - x for x in lmat]
    pw = lmat
    for _ in range(CH.bit_length() - 2):
        pw = [jnp.dot(x.astype(BF16), x.astype(BF16), preferred_element_type=F32) for x in pw]
        tinv = [tinv[h] + jnp.dot(tinv[h].astype(BF16), pw[h].astype(BF16), preferred_element_type=F32)
                for h in heads]
    eg = [jnp.exp(x) for x in g]
    sol = [jnp.dot(tinv[h].astype(BF16),
                   jnp.concatenate([v[h] * beta[h], kb[h] * eg[h]], axis=1).astype(BF16),
                   preferred_element_type=F32) for h in heads]
    attn = [(causal * _dot_nt(q[h].astype(BF16), k_b[h]) * decay[h]).astype(BF16) for h in heads]
    q_st = [(q[h] * eg[h]).astype(BF16) for h in heads]

    vnew_ref[...] = jnp.zeros_like(vnew_ref)
    for c in range(TC // CH):
        rows = slice(c * CH, (c + 1) * CH)
        g_last = [x[(c + 1) * CH - 1:(c + 1) * CH, :] for x in g]
        st = [state_ref[h] for h in heads]
        st_b = [x.astype(BF16) for x in st]
        v_new = [sol[h][rows, :GDN_DV]
                 - jnp.dot(sol[h][rows, GDN_DV:].astype(BF16), st_b[h], preferred_element_type=F32)
                 for h in heads]
        for h in heads:
            vnew_ref[h, rows, :] = v_new[h].astype(BF16)
        o_c = [jnp.dot(q_st[h][rows], st_b[h], preferred_element_type=F32)
               + jnp.dot(attn[h][rows], vnew_ref[h], preferred_element_type=F32) for h in heads]
        for h in heads:
            k_st = (k[h][rows] * jnp.exp(g_last[h] - g[h][rows])).astype(BF16)
            state_ref[h] = st[h] * jnp.exp(g_last[h][:, :1]) + _dot_tn(k_st, v_new[h].astype(BF16))
        for h in heads:
            o = o_c[h] * lax.rsqrt(jnp.mean(o_c[h] * o_c[h], axis=-1, keepdims=True) + RMS_EPS) * ng_ref[...]
            zc = z_ref[rows, h * LANES:(h + 1) * LANES]
            o_ref[rows, h * LANES:(h + 1) * LANES] = (o * (zc * jax.nn.sigmoid(zc))).astype(o_ref.dtype)


def gdn_mixer(proj, bs, conv_w, a_log, dt_bias, norm_g):
    t = proj.shape[0]
    s = t // bs
    nt = s // GDN_TC
    H = GDN_HEADS
    hw = H * LANES
    assert s % GDN_TC == 0 and GDN_DK == LANES and GDN_DV == LANES and OFF_QKV == 0 and OFF_Z % hw == 0
    cw = jnp.zeros((8, GDN_QKV), F32).at[:GDN_CONV].set(conv_w)
    tok = lambda blk: pl.BlockSpec((GDN_TC, hw), lambda b, i: (b * nt + i, blk))
    wspec = lambda blk: pl.BlockSpec((8, hw), lambda b, i: (0, blk))
    smem = pl.BlockSpec(memory_space=pltpu.SMEM)
    return pl.pallas_call(
        _gdn_body,
        grid=(bs, nt),
        in_specs=[smem, smem, tok(0), tok(1), tok(2), tok(OFF_Z // hw),
                  pl.BlockSpec((GDN_TC, 2 * LANES), lambda b, i: (b * nt + i, OFF_SMALL // (2 * LANES))),
                  wspec(0), wspec(1), wspec(2), pl.BlockSpec((1, GDN_DV), lambda b, i: (0, 0))],
        out_specs=pl.BlockSpec((GDN_TC, hw), lambda b, i: (b * nt + i, 0)),
        out_shape=jax.ShapeDtypeStruct((t, hw), BF16),
        scratch_shapes=[pltpu.VMEM((H, GDN_DK, GDN_DV), F32)] + [pltpu.VMEM((GDN_HALO, hw), F32)] * 3
        + [pltpu.VMEM((H, GDN_TC, GDN_DV), BF16)],
        compiler_params=_params("parallel", "arbitrary"),
        name="gdn_delta",
    )(-jnp.exp(a_log), dt_bias.astype(F32), proj, proj, proj, proj, proj, cw, cw, cw,
      norm_g.reshape(1, GDN_DV))


NSA_KT = 256
NSA_NEG = MASKED
NSA_WTILES = NSA_WINDOW // LANES + 1


def _bucket_table():
    n = np.arange(LANES)
    max_exact = REL_BUCKETS // 2
    nf = np.maximum(n, 1).astype(np.float32)
    large = max_exact + (np.log(nf / np.float32(max_exact)) / np.float32(math.log(REL_MAX_DIST / max_exact))
                         * np.float32(REL_BUCKETS - max_exact)).astype(np.int32)
    tbl = np.where(n < max_exact, n, np.minimum(large, REL_BUCKETS - 1))
    assert tbl[-1] == REL_BUCKETS - 1 and REL_MAX_DIST <= LANES
    return tbl


def _compress_body(x_ref, pe_ref, w1_ref, w2_ref, o_ref):
    x = x_ref[0, 0, 0]
    w1 = w1_ref[0]
    half = NSA_CMP_STRIDE * NSA_DH
    nc = x.shape[0]
    a = jnp.dot(x, w1[:half], preferred_element_type=F32)
    b = jnp.dot(x, w1[half:], preferred_element_type=F32)
    pe_h = jnp.dot(pe_ref[0], w1, preferred_element_type=F32)
    hid = a + pltpu.roll(b, nc - 1, 0) + pe_h[0:1, :]
    o_ref[0, 0, 0] = jnp.dot(jax.nn.gelu(hid).astype(BF16), w2_ref[0], preferred_element_type=F32)


def nsa_compress(x, pe, w1, w2):
    _, bs, g, nc, width = x.shape
    return pl.pallas_call(
        _compress_body,
        grid=(2, bs, g),
        in_specs=[pl.BlockSpec((1, 1, 1, nc, width), lambda i, b, j: (i, b, j, 0, 0)),
                  pl.BlockSpec((1,) + pe.shape[1:], lambda i, b, j: (i, 0, 0)),
                  pl.BlockSpec((1,) + w1.shape[1:], lambda i, b, j: (i, 0, 0)),
                  pl.BlockSpec((1,) + w2.shape[1:], lambda i, b, j: (i, 0, 0))],
        out_specs=pl.BlockSpec((1, 1, 1, nc, NSA_DH), lambda i, b, j: (i, b, j, 0, 0)),
        out_shape=jax.ShapeDtypeStruct((2, bs, g, nc, NSA_DH), F32),
        compiler_params=_params("parallel", "parallel", "parallel"),
        name="nsa_compress",
    )(x, pe, w1, w2)


def _nsa_body(nfast, wstd, near_w, near_c, q_ref, gl_ref, qp_ref, kp_ref, cp_ref,
              ks_ref, vs_ref, kw_ref, vw_ref, kc_ref, vc_ref, agg_ref, td_ref, o_ref,
              m_ref, l_ref, acc_ref, *, n_sel, sel_k):
    QB, HG, G = NSA_QBLOCK, NSA_HPG, NSA_KV_GROUPS
    groups = range(G)
    qi = pl.program_id(1)
    s0 = qi * QB
    nc = kc_ref.shape[2]
    n_ct = nc // LANES
    q = [q_ref[0, g, 0] for g in groups]
    qp = qp_ref[...]

    def lanes4(x):
        return jnp.concatenate([x] * HG, axis=1)

    def delta_t(kp):
        idx = jnp.clip(_dot_nt(kp, qp), 0.0, float(LANES - 1)).astype(jnp.int32)
        outs = []
        for g in groups:
            heads = []
            for hg in range(HG):
                tb = jnp.broadcast_to(td_ref[g * HG + hg:g * HG + hg + 1, :], idx.shape)
                heads.append(jnp.take_along_axis(tb, idx, axis=1))
            outs.append(jnp.concatenate(heads, axis=1))
        return outs

    def maybe_delta(flag, kp):
        zeros = jnp.zeros((kp.shape[0], HG * QB), F32)
        return lax.cond(flag != 0, lambda: tuple(delta_t(kp)), lambda: (zeros,) * G)

    gate = [jax.nn.sigmoid(gl_ref[0, g, 0]) for g in groups]

    sc = [_dot_nt(kc_ref[0, g], q[g]) for g in groups]
    dl = [maybe_delta(near_c[qi * n_ct + ct], cp_ref[ct * LANES:(ct + 1) * LANES, :]) for ct in range(n_ct)]
    c_id = lax.broadcasted_iota(jnp.int32, (nc, QB), 0)
    c_q = lax.broadcasted_iota(jnp.int32, (nc, QB), 1)
    ok_c = lanes4(jnp.where(c_id * NSA_CMP_STRIDE + (NSA_CMP_LEN - 1) <= s0 + c_q, 1.0, 0.0))
    blk = lax.broadcasted_iota(jnp.int32, (LANES, QB), 0)
    t_tok = s0 + lax.broadcasted_iota(jnp.int32, (LANES, QB), 1)
    tb_blk = t_tok // NSA_SEL_LEN
    forced = jnp.where(blk == 0, 1.0, 0.0) + jnp.where(blk == tb_blk, 1.0, 0.0) \
        + jnp.where(blk == tb_blk - 1, 1.0, 0.0)
    blkf = blk.astype(F32)
    out, score = [], []
    for g in groups:
        s = sc[g] + jnp.concatenate([d[g] for d in dl], axis=0)
        s = jnp.where(ok_c > 0.5, s, NSA_NEG)
        e = jnp.exp2(s - jnp.max(s, axis=0, keepdims=True)) * ok_c
        p = e * (1.0 / jnp.maximum(jnp.sum(e, axis=0, keepdims=True), 1e-30))
        out.append(gate[g][0:1, :] * jnp.dot(vc_ref[0, g], p.astype(BF16), preferred_element_type=F32))
        ps = p[:, 0:QB]
        for hg in range(1, HG):
            ps = ps + p[:, hg * QB:(hg + 1) * QB]
        ps_hi, ps_lo = _split_bf16(ps)
        imp = (jnp.dot(agg_ref[...], ps_hi, preferred_element_type=F32)
               + jnp.dot(agg_ref[...], ps_lo, preferred_element_type=F32))
        sco = jnp.where(forced > 0.5, 1e9, jnp.where(blk * NSA_SEL_LEN <= t_tok, imp, -1e9))
        score.append(jnp.where(blk < n_sel, sco, -jnp.inf))
    sel = [jnp.zeros((LANES, QB), F32) for _ in groups]
    for _ in range(sel_k):
        for g in groups:
            mx = jnp.max(score[g], axis=0, keepdims=True)
            first = jnp.min(jnp.where(score[g] == mx, blkf, float(LANES)), axis=0, keepdims=True)
            hit = blkf == first
            sel[g] = jnp.where(hit, 1.0, sel[g])
            score[g] = jnp.where(hit, -jnp.inf, score[g])
    q2 = []
    for g in groups:
        sel_q = jnp.transpose(jnp.where(sel[g] > 0.5, 0.0, NSA_NEG))
        q2.append(jnp.concatenate([q[g], jnp.concatenate([sel_q] * HG, axis=0).astype(BF16)], axis=1))

    def reset():
        m_ref[...] = jnp.full_like(m_ref, NSA_NEG)
        l_ref[...] = jnp.zeros_like(l_ref)
        acc_ref[...] = jnp.zeros_like(acc_ref)

    def tile_step(g, k, s, v_t):
        m_old = m_ref[g, k]
        m_new = jnp.maximum(m_old, jnp.max(s, axis=0, keepdims=True))
        alpha = jnp.exp2(m_old - m_new)
        p = jnp.exp2(s - m_new)
        l_ref[g, k] = l_ref[g, k] * alpha + jnp.sum(p, axis=0, keepdims=True)
        acc_ref[g, k] = acc_ref[g, k] * alpha + jnp.dot(v_t, p.astype(BF16), preferred_element_type=F32)
        m_ref[g, k] = m_new

    def finish(g):
        m = jnp.maximum(m_ref[g, 0], m_ref[g, 1])
        w0, w1 = jnp.exp2(m_ref[g, 0] - m), jnp.exp2(m_ref[g, 1] - m)
        return (acc_ref[g, 0] * w0 + acc_ref[g, 1] * w1) * (1.0 / (l_ref[g, 0] * w0 + l_ref[g, 1] * w1))

    key_r = lax.broadcasted_iota(jnp.int32, (NSA_KT, QB), 0)
    key_q = lax.broadcasted_iota(jnp.int32, (NSA_KT, QB), 1)

    def sel_scores(kt, with_delta, diag):
        k0 = pl.multiple_of(kt * NSA_KT, NSA_KT)
        s = [_dot_nt(ks_ref[0, g, pl.ds(k0, NSA_KT), :], q2[g]) for g in groups]
        if with_delta:
            d = delta_t(kp_ref[pl.ds(k0, NSA_KT), :])
            s = [s[g] + d[g] for g in groups]
        if diag:
            causal = lanes4(jnp.where(k0 + key_r <= s0 + key_q, 0.0, NSA_NEG))
            s = [x + causal for x in s]
        return [(s[g], vs_ref[0, g, :, pl.ds(k0, NSA_KT)]) for g in groups]

    reset()
    n_past = (s0 + QB - 1) // NSA_KT
    n_pairs = nfast[qi] // 2

    def fast_body(i, carry):
        a = sel_scores(2 * i, False, False)
        b = sel_scores(2 * i + 1, False, False)
        for g in groups:
            tile_step(g, 0, *a[g])
        for g in groups:
            tile_step(g, 1, *b[g])
        return carry

    def slow_body(kt, carry):
        a = sel_scores(kt, True, False)
        for g in groups:
            tile_step(g, 1, *a[g])
        return carry

    lax.fori_loop(0, n_pairs, fast_body, 0)
    lax.fori_loop(2 * n_pairs, n_past, slow_body, 0)
    a = sel_scores(n_past, True, True)
    for g in groups:
        tile_step(g, 0, *a[g])
    out = [out[g] + gate[g][1:2, :] * finish(g) for g in groups]

    reset()
    w_r = lax.broadcasted_iota(jnp.int32, (LANES, QB), 0)
    w_q = lax.broadcasted_iota(jnp.int32, (LANES, QB), 1)
    in_window = lanes4(jnp.where(w_r > w_q, 0.0, NSA_NEG))
    causal_w = lanes4(jnp.where(w_r <= w_q, 0.0, NSA_NEG))

    def win_scores(g, st, n):
        st = pl.multiple_of(st, LANES)
        return _dot_nt(kw_ref[0, g, pl.ds(st, n), :], q[g]), vw_ref[0, g, :, pl.ds(st, n)]

    @pl.when(wstd[qi] != 0)
    def _():
        half = NSA_WINDOW // 2
        zeros = jnp.zeros((LANES, HG * QB), F32)
        sa = [win_scores(g, s0 - NSA_WINDOW, half) for g in groups]
        sb = [win_scores(g, s0 - half, half) for g in groups]
        sd = [win_scores(g, s0, LANES) for g in groups]
        near = delta_t(kp_ref[pl.ds(pl.multiple_of(s0 - LANES, LANES), LANES), :])
        diag = delta_t(kp_ref[pl.ds(pl.multiple_of(s0, LANES), LANES), :])
        for g in groups:
            tile_step(g, 0, sa[g][0] + jnp.concatenate([in_window, zeros], axis=0), sa[g][1])
        for g in groups:
            tile_step(g, 1, sb[g][0] + jnp.concatenate([zeros, near[g]], axis=0), sb[g][1])
        for g in groups:
            tile_step(g, 0, sd[g][0] + diag[g] + causal_w, sd[g][1])

    for j in range(NSA_WTILES):
        start = s0 - NSA_WINDOW + j * LANES

        @pl.when((wstd[qi] == 0) & (start >= 0))
        def _():
            sw = [win_scores(g, start, LANES) for g in groups]
            d = maybe_delta(near_w[qi * NSA_WTILES + j],
                            kp_ref[pl.ds(pl.multiple_of(start, LANES), LANES), :])
            for g in groups:
                s = sw[g][0] + d[g]
                if j == 0:
                    s = s + in_window
                elif j == NSA_WTILES - 1:
                    s = s + causal_w
                tile_step(g, j % 2, s, sw[g][1])

    heads_out = []
    for g in groups:
        o_t = out[g] + gate[g][2:3, :] * finish(g)
        heads_out += [jnp.transpose(o_t[:, hg * QB:(hg + 1) * QB]) for hg in range(HG)]
    o_ref[0] = jnp.concatenate(heads_out, axis=1).astype(o_ref.dtype)


def _pos_digits(p, key_side):
    d0, d1, d2 = (p & 127).astype(F32), ((p >> 7) & 127).astype(F32), (p >> 14).astype(F32)
    one = jnp.ones_like(d0)
    if key_side:
        cols = [one, one, one, -d2, -d1, -d0]
    else:
        cols = [16384.0 * d2, 128.0 * d1, d0, 16384.0 * one, 128.0 * one, one]
    out = jnp.stack(cols, axis=-1)
    return jnp.pad(out, ((0, 0), (0, LANES - out.shape[-1]))).astype(BF16)


def nsa_mixer(q, k_c, v_c, k_s, v_s, k_w, v_w, gate_logits, positions,
              ck_pe, ck_w1, ck_w2, cv_pe, cv_w1, cv_w2, rel_bias):
    q, k_c, v_c, k_s, v_s, k_w, v_w = lax.optimization_barrier((q, k_c, v_c, k_s, v_s, k_w, v_w))
    bs, s, _ = q.shape
    G, HG, DH, QB = NSA_KV_GROUPS, NSA_HPG, NSA_DH, NSA_QBLOCK
    nqt = s // QB
    nc = s // NSA_CMP_STRIDE
    n_cmp = (s - NSA_CMP_LEN) // NSA_CMP_STRIDE + 1
    n_sel = s // NSA_SEL_LEN
    sel_k = min(NSA_SEL_TOPK, n_sel)
    n_kt = s // NSA_KT
    n_ct = nc // LANES
    assert s % (LANES * NSA_CMP_STRIDE) == 0 and n_sel <= LANES and NSA_KT == 2 * QB

    def by_group(t):
        return t.reshape(bs, s, G, DH).transpose(0, 2, 1, 3)

    def by_group_t(t):
        return t.reshape(bs, s, G, DH).transpose(0, 2, 3, 1).astype(BF16)

    ones2 = jnp.zeros((LANES - DH,), F32).at[:2].set(1.0)

    def keys_aug(t, blocks=False):
        parts = [t, jnp.broadcast_to(ones2, t.shape[:-1] + (LANES - DH,))]
        if blocks:
            onehot = (jnp.arange(s)[:, None] // NSA_SEL_LEN == jnp.arange(LANES)[None, :]).astype(F32)
            parts.append(jnp.broadcast_to(onehot, t.shape[:-2] + (s, LANES)))
        return jnp.concatenate(parts, axis=-1).astype(BF16)

    chunks = jnp.stack([by_group(k_c), by_group(v_c)]).reshape(2, bs, G, nc, NSA_CMP_STRIDE * DH)
    pe = jnp.stack([ck_pe, cv_pe]).reshape(2, 1, NSA_CMP_LEN * DH)
    cmp = nsa_compress(chunks.astype(BF16), jnp.broadcast_to(pe, (2, 8, NSA_CMP_LEN * DH)).astype(BF16),
                       jnp.stack([ck_w1, cv_w1]).astype(BF16), jnp.stack([ck_w2, cv_w2]).astype(BF16))
    kc, vc_t = keys_aug(cmp[0]), cmp[1].transpose(0, 1, 3, 2).astype(BF16)

    log2e = math.log2(math.e)
    rel_bias = rel_bias.astype(F32) * log2e
    far = rel_bias[REL_BUCKETS - 1]
    far_hi = far.astype(BF16).astype(F32)
    extra = jnp.zeros((NSA_HEADS, LANES - DH), F32).at[:, 0].set(far_hi).at[:, 1].set(far - far_hi)
    qh = (q * (DH ** -0.5 * log2e)).reshape(bs, nqt, QB, G, HG, DH).transpose(0, 3, 1, 4, 2, 5)
    ex = jnp.broadcast_to(extra.reshape(1, G, 1, HG, 1, LANES - DH), (bs, G, nqt, HG, QB, LANES - DH))
    qs = jnp.concatenate([qh, ex], axis=-1).astype(BF16).reshape(bs, G, nqt, HG * QB, LANES)
    gl = gate_logits.reshape(bs, nqt, QB, G, HG, 3).transpose(0, 3, 1, 5, 4, 2)
    gl = jnp.pad(gl.reshape(bs, G, nqt, 3, HG * QB), ((0, 0),) * 3 + ((0, 5), (0, 0)))

    td = (rel_bias[_bucket_table()] - rel_bias[REL_BUCKETS - 1][None, :]).T.astype(F32)
    cmp_end = jnp.minimum(jnp.arange(nc) * NSA_CMP_STRIDE + NSA_CMP_LEN - 1, s - 1)
    cpos = positions[cmp_end]
    qmin = positions.reshape(nqt, QB).min(axis=1)
    kmax = positions.reshape(nqt, QB).max(axis=1)
    near_s = (qmin[:, None] - positions.reshape(n_kt, NSA_KT).max(axis=1)[None, :]) < REL_MAX_DIST
    n_past = (jnp.arange(nqt) * QB + QB - 1) // NSA_KT
    n_fast = jnp.minimum(jnp.argmax(jnp.concatenate([near_s, jnp.ones((nqt, 1), bool)], axis=1), axis=1),
                         n_past)
    wt = jnp.clip(jnp.arange(nqt)[:, None] - (NSA_WTILES - 1) + jnp.arange(NSA_WTILES)[None, :], 0, nqt - 1)
    near_w = (qmin[:, None] - kmax[wt]) < REL_MAX_DIST
    near_c = (qmin[:, None] - cpos.reshape(n_ct, LANES).max(axis=1)[None, :]) < REL_MAX_DIST
    w_std = ((jnp.arange(nqt) >= NSA_WTILES - 1)
             & jnp.all(near_w == (jnp.arange(NSA_WTILES) >= NSA_WTILES - 2)[None, :], axis=1))

    cmp_start = np.arange(nc) * NSA_CMP_STRIDE
    sel_start = np.arange(LANES) * NSA_SEL_LEN
    agg_t = ((cmp_start[None, :] <= sel_start[:, None] + NSA_SEL_LEN - 1)
             & (cmp_start[None, :] + NSA_CMP_LEN - 1 >= sel_start[:, None])
             & (np.arange(nc)[None, :] < n_cmp) & (np.arange(LANES)[:, None] < n_sel))

    cols = HG * QB
    full = lambda shape: pl.BlockSpec(shape, lambda b, i, *_: (0,) * len(shape))
    per_b = lambda n, w: pl.BlockSpec((1, G, n, w), lambda b, i, *_: (b, 0, 0, 0))
    per_tile = lambda n, w: pl.BlockSpec((1, G, 1, n, w), lambda b, i, *_: (b, 0, i, 0, 0))
    grid_spec = pltpu.PrefetchScalarGridSpec(
        num_scalar_prefetch=4,
        grid=(bs, nqt),
        in_specs=[per_tile(cols, LANES), per_tile(8, cols),
                  pl.BlockSpec((QB, LANES), lambda b, i, *_: (i, 0)),
                  full((s, LANES)), full((nc, LANES)),
                  per_b(s, 2 * LANES), per_b(DH, s), per_b(s, LANES), per_b(DH, s),
                  per_b(nc, LANES), per_b(DH, nc),
                  full((LANES, nc)), full((NSA_HEADS, LANES))],
        out_specs=pl.BlockSpec((1, QB, NSA_Q), lambda b, i, *_: (b, i, 0)),
        scratch_shapes=[pltpu.VMEM((G, 2, 1, cols), F32), pltpu.VMEM((G, 2, 1, cols), F32),
                        pltpu.VMEM((G, 2, DH, cols), F32)],
    )
    return pl.pallas_call(
        functools.partial(_nsa_body, n_sel=n_sel, sel_k=sel_k),
        grid_spec=grid_spec,
        out_shape=jax.ShapeDtypeStruct((bs, s, NSA_Q), BF16),
        compiler_params=_params("parallel", "arbitrary"),
        name="nsa_attention",
    )(n_fast.astype(jnp.int32), w_std.astype(jnp.int32), near_w.reshape(-1).astype(jnp.int32),
      near_c.reshape(-1).astype(jnp.int32),
      qs, gl, _pos_digits(positions, False), _pos_digits(positions, True), _pos_digits(cpos, True),
      keys_aug(by_group(k_s), blocks=True), by_group_t(v_s), keys_aug(by_group(k_w)), by_group_t(v_w),
      kc, vc_t, jnp.asarray(agg_t, BF16), td)


SGU_TC = 512


def _sgu_body(uv_ref, lg_ref, lb_ref, w_ref, b_ref, o_ref):
    uv = jax.nn.gelu(uv_ref[...])
    u, v = uv[:, :SGU_WIDTH], uv[:, SGU_WIDTH:]
    mu = jnp.mean(v, axis=-1, keepdims=True)
    vc = v - mu
    var = jnp.mean(vc * vc, axis=-1, keepdims=True)
    vn = (vc * lax.rsqrt(var + RMS_EPS) * lg_ref[...] + lb_ref[...]).astype(BF16)
    gw = SGU_WIDTH // SGU_GROUPS
    for c in range(SGU_TC // SGU_CHUNK):
        rows = slice(c * SGU_CHUNK, (c + 1) * SGU_CHUNK)
        for g in range(SGU_GROUPS):
            cols = slice(g * gw, (g + 1) * gw)
            mix = jnp.dot(w_ref[g], vn[rows, cols], preferred_element_type=F32) + b_ref[:, g:g + 1]
            o_ref[rows, cols] = (u[rows, cols] * mix).astype(o_ref.dtype)


def sgu_mixer(proj, ln_g, ln_b, w_s, b_s):
    t = proj.shape[0]
    assert t % SGU_TC == 0 and OFF_SGU % (2 * SGU_WIDTH) == 0
    fixed = lambda i: (0, 0)
    return pl.pallas_call(
        _sgu_body,
        grid=(t // SGU_TC,),
        in_specs=[pl.BlockSpec((SGU_TC, 2 * SGU_WIDTH), lambda i: (i, OFF_SGU // (2 * SGU_WIDTH))),
                  pl.BlockSpec((1, SGU_WIDTH), fixed), pl.BlockSpec((1, SGU_WIDTH), fixed),
                  pl.BlockSpec((SGU_GROUPS, SGU_CHUNK, SGU_CHUNK), lambda i: (0, 0, 0)),
                  pl.BlockSpec((SGU_CHUNK, SGU_GROUPS), fixed)],
        out_specs=pl.BlockSpec((SGU_TC, SGU_WIDTH), lambda i: (i, 0)),
        out_shape=jax.ShapeDtypeStruct((t, SGU_WIDTH), BF16),
        compiler_params=_params("parallel"),
        name="sgu_gate",
    )(proj, ln_g.reshape(1, SGU_WIDTH), ln_b.reshape(1, SGU_WIDTH), jnp.tril(w_s).astype(BF16), b_s.T)


def _permute_w_in(w):
    sizes = ([SSM_WIDTH, GDN_QKV, GDN_HEADS, GDN_HEADS, GDN_HEADS * GDN_DV, NSA_Q]
             + [NSA_KV] * 6 + [3 * NSA_HEADS, 2 * SGU_WIDTH, N_BRANCH * D_MODEL])
    cuts = [int(c) for c in np.cumsum(sizes)[:-1]]
    (w_ssm, w_qkv, w_a, w_b, w_z, w_nq, w_kc, w_vc, w_ks, w_vs, w_kw, w_vw,
     w_ng, w_sgu, w_gate) = jnp.split(w, cuts, axis=-1)
    pad = jnp.zeros((w.shape[0], PROJ_COLS - OFF_SMALL - SMALL_USED), w.dtype)
    return jnp.concatenate([w_qkv, w_ssm, w_z, w_nq, w_sgu, w_gate, w_kc, w_vc, w_ks, w_vs,
                            w_kw, w_vw, w_a, w_b, w_ng, pad], axis=-1)


def kernel(x, positions, norm_mix, norm_ffn, norm_final, w_in, ssm_a_re, ssm_a_im, ssm_log_dt, ssm_b_re, ssm_b_im, ssm_c_re, ssm_c_im, ssm_d, ssm_glu_w, gdn_conv_w, gdn_a_log, gdn_dt_bias, gdn_norm, nsa_cmp_k_pe, nsa_cmp_k_w1, nsa_cmp_k_w2, nsa_cmp_v_pe, nsa_cmp_v_w1, nsa_cmp_v_w2, rel_bias, sgu_ln_g, sgu_ln_b, sgu_w, sgu_b, w_br_ssm, w_br_gdn, w_br_nsa, w_br_sgu, w_out, ffn_w_gate, ffn_w_up, ffn_w_down, moe_router, moe_w_gate, moe_w_up, moe_w_down):
    bs, s, d = x.shape
    t = bs * s
    assert DEPTH == 2
    xf = x.reshape(t, d)
    for layer in range(DEPTH):
        proj = in_proj(xf, norm_mix[layer], _permute_w_in(w_in[layer]))
        p3 = proj.reshape(bs, s, PROJ_COLS)

        def seg(off, width):
            return p3[..., off:off + width]

        y_ssm = ssm_mixer(proj, bs, ssm_a_re[layer], ssm_a_im[layer], ssm_log_dt[layer],
                          ssm_b_re[layer], ssm_b_im[layer], ssm_c_re[layer], ssm_c_im[layer],
                          ssm_d[layer], ssm_glu_w[layer])
        y_gdn = gdn_mixer(proj, bs, gdn_conv_w[layer], gdn_a_log[layer], gdn_dt_bias[layer],
                          gdn_norm[layer])
        kvs = [seg(OFF_KV + i * NSA_KV, NSA_KV) for i in range(6)]
        y_nsa = nsa_mixer(seg(OFF_NQ, NSA_Q), *kvs, seg(OFF_SMALL + 2 * GDN_HEADS, 3 * NSA_HEADS),
                          positions, nsa_cmp_k_pe[layer], nsa_cmp_k_w1[layer], nsa_cmp_k_w2[layer],
                          nsa_cmp_v_pe[layer], nsa_cmp_v_w1[layer], nsa_cmp_v_w2[layer], rel_bias)
        y_sgu = sgu_mixer(proj, sgu_ln_g[layer], sgu_ln_b[layer], sgu_w[layer], sgu_b[layer])
        ys = [y_ssm, y_gdn, y_nsa.reshape(t, -1), y_sgu]
        ws = [w[layer].astype(BF16) for w in (w_br_ssm, w_br_gdn, w_br_nsa, w_br_sgu)]
        xf = merge(xf, proj, ys, ws, w_out[layer].astype(BF16))
        i = layer // 2
        if layer % 2 == 0:
            xf = dense_ffn(xf, norm_ffn[layer], ffn_w_gate[i], ffn_w_up[i], ffn_w_down[i])
        else:
            xf = moe_layer(xf, norm_ffn[layer], moe_router[i], moe_w_gate[i], moe_w_up[i],
                           moe_w_down[i], norm_final)
    return xf.reshape(bs, s, d)
```

```python
import functools
import math

import jax
import jax.numpy as jnp
from jax import lax
import numpy as np
from jax.experimental import pallas as pl
from jax.experimental.pallas import tpu as pltpu

F32 = jnp.float32
BF16 = jnp.bfloat16

D_MODEL = 1024
DEPTH = 2
SSM_WIDTH = D_MODEL // 2
SSM_GROUP = 16
SSM_GROUPS = SSM_WIDTH // SSM_GROUP
SSM_STATE = 64
GDN_HEADS = 4
GDN_DK = 128
GDN_DV = 128
GDN_CONV = 4
GDN_CHUNK = 64
NSA_HEADS = 8
NSA_KV_GROUPS = 2
NSA_HPG = NSA_HEADS // NSA_KV_GROUPS
NSA_DH = 64
NSA_CMP_LEN = 32
NSA_CMP_STRIDE = 16
NSA_CMP_HIDDEN = 128
NSA_SEL_LEN = 64
NSA_SEL_TOPK = 16
NSA_WINDOW = 512
NSA_QBLOCK = 128
SGU_WIDTH = D_MODEL // 2
SGU_GROUPS = 4
SGU_CHUNK = 128
REL_BUCKETS = 32
REL_MAX_DIST = 128
FFN_DENSE = 2816
N_EXPERTS = 8
TOP_K = 2
FFN_EXPERT = 3584
N_BRANCH = 4
RMS_EPS = 1e-6
GDN_QKV = GDN_HEADS * (2 * GDN_DK + GDN_DV)
NSA_Q = NSA_HEADS * NSA_DH
NSA_KV = NSA_KV_GROUPS * NSA_DH

LANES = 128
VMEM_LIMIT = 48 * 1024 * 1024
MASKED = -1e30

OFF_QKV = 0
OFF_SSM = OFF_QKV + GDN_QKV
OFF_Z = OFF_SSM + SSM_WIDTH
OFF_NQ = OFF_Z + GDN_HEADS * GDN_DV
OFF_SGU = OFF_NQ + NSA_Q
OFF_GATE = OFF_SGU + 2 * SGU_WIDTH
OFF_KV = OFF_GATE + N_BRANCH * D_MODEL
OFF_SMALL = OFF_KV + 6 * NSA_KV
SMALL_USED = 2 * GDN_HEADS + 3 * NSA_HEADS
PROJ_COLS = 9216


def _params(*sem):
    return pltpu.CompilerParams(dimension_semantics=sem, vmem_limit_bytes=VMEM_LIMIT)


def _rms(x, g):
    return x * lax.rsqrt(jnp.mean(x * x, axis=-1, keepdims=True) + RMS_EPS) * g


def _dot_nt(a, b):
    return lax.dot_general(a, b, (((1,), (1,)), ((), ())), preferred_element_type=F32)


def _dot_tn(a, b):
    return lax.dot_general(a, b, (((0,), (0,)), ((), ())), preferred_element_type=F32)


def _split_bf16(x):
    hi = x.astype(BF16)
    return hi, (x - hi.astype(F32)).astype(BF16)


def _in_proj_body(x_ref, g_ref, w_ref, o_ref, h_ref):
    @pl.when(pl.program_id(1) == 0)
    def _():
        h_ref[...] = _rms(x_ref[...], g_ref[...]).astype(BF16)

    o_ref[...] = jnp.dot(h_ref[...], w_ref[...].astype(BF16),
                         preferred_element_type=F32).astype(o_ref.dtype)


def in_proj(x, g, w, tm=2048, tn=512):
    t, d = x.shape
    n = w.shape[1]
    return pl.pallas_call(
        _in_proj_body,
        grid=(t // tm, n // tn),
        in_specs=[pl.BlockSpec((tm, d), lambda i, j: (i, 0)),
                  pl.BlockSpec((1, d), lambda i, j: (0, 0)),
                  pl.BlockSpec((d, tn), lambda i, j: (0, j))],
        out_specs=pl.BlockSpec((tm, tn), lambda i, j: (i, j)),
        out_shape=jax.ShapeDtypeStruct((t, n), BF16),
        scratch_shapes=[pltpu.VMEM((tm, d), BF16)],
        compiler_params=_params("parallel", "arbitrary"),
        name="in_proj",
    )(x, g.reshape(1, d), w)


def _merge_body(x_ref, gate_ref, ya_ref, yb_ref, yc_ref, yd_ref,
                wa_ref, wb_ref, wc_ref, wd_ref, wo_ref, o_ref):
    def branch(k, y_ref, w_ref):
        p = jnp.dot(y_ref[...], w_ref[...], preferred_element_type=F32)
        return jax.nn.sigmoid(gate_ref[:, k * D_MODEL:(k + 1) * D_MODEL].astype(F32)) * p

    merged = (branch(0, ya_ref, wa_ref) + branch(1, yb_ref, wb_ref)
              + branch(2, yc_ref, wc_ref) + branch(3, yd_ref, wd_ref))
    o_ref[...] = x_ref[...] + jnp.dot(merged.astype(BF16), wo_ref[...],
                                      preferred_element_type=F32)


def merge(x, proj, ys, ws, w_out, tm=256):
    t, d = x.shape
    gate_blk = OFF_GATE // (N_BRANCH * D_MODEL)
    row = lambda i: (i, 0)
    fixed = lambda i: (0, 0)
    in_specs = [pl.BlockSpec((tm, d), row),
                pl.BlockSpec((tm, N_BRANCH * D_MODEL), lambda i: (i, gate_blk))]
    in_specs += [pl.BlockSpec((tm, y.shape[1]), row) for y in ys]
    in_specs += [pl.BlockSpec(w.shape, fixed) for w in ws]
    in_specs += [pl.BlockSpec(w_out.shape, fixed)]
    return pl.pallas_call(
        _merge_body,
        grid=(t // tm,),
        in_specs=in_specs,
        out_specs=pl.BlockSpec((tm, d), row),
        out_shape=jax.ShapeDtypeStruct((t, d), F32),
        compiler_params=_params("parallel"),
        name="merge",
    )(x, proj, *ys, *ws, w_out)


def _ffn_body(x_ref, g_ref, wg_ref, wu_ref, wd_ref, o_ref, h_ref, acc_ref):
    f = pl.program_id(1)

    @pl.when(f == 0)
    def _():
        h_ref[...] = _rms(x_ref[...], g_ref[...]).astype(BF16)
        acc_ref[...] = x_ref[...]

    h = h_ref[...]
    a = jnp.dot(h, wg_ref[...].astype(BF16), preferred_element_type=F32)
    u = jnp.dot(h, wu_ref[...].astype(BF16), preferred_element_type=F32)
    act = (a * jax.nn.sigmoid(a) * u).astype(BF16)
    acc_ref[...] += jnp.dot(act, wd_ref[...].astype(BF16), preferred_element_type=F32)

    @pl.when(f == pl.num_programs(1) - 1)
    def _():
        o_ref[...] = acc_ref[...]


def dense_ffn(x, g, wg, wu, wd, tm=1024, tf=256):
    t, d = x.shape
    nf = wg.shape[1]
    return pl.pallas_call(
        _ffn_body,
        grid=(t // tm, nf // tf),
        in_specs=[pl.BlockSpec((tm, d), lambda i, f: (i, 0)),
                  pl.BlockSpec((1, d), lambda i, f: (0, 0)),
                  pl.BlockSpec((d, tf), lambda i, f: (0, f)),
                  pl.BlockSpec((d, tf), lambda i, f: (0, f)),
                  pl.BlockSpec((tf, d), lambda i, f: (f, 0))],
        out_specs=pl.BlockSpec((tm, d), lambda i, f: (i, 0)),
        out_shape=jax.ShapeDtypeStruct((t, d), F32),
        scratch_shapes=[pltpu.VMEM((tm, d), BF16), pltpu.VMEM((tm, d), F32)],
        compiler_params=_params("parallel", "arbitrary"),
        name="dense_ffn",
    )(x, g.reshape(1, d), wg, wu, wd)


def _route_body(x_ref, g_ref, rhi_ref, rlo_ref, h_ref, idx_ref, wt_ref):
    h = _rms(x_ref[...], g_ref[...])
    hi = h.astype(BF16)
    lo = (h - hi.astype(F32)).astype(BF16)
    h_ref[...] = h
    logits = (jnp.dot(hi, rhi_ref[...], preferred_element_type=F32)
              + jnp.dot(hi, rlo_ref[...], preferred_element_type=F32)
              + jnp.dot(lo, rhi_ref[...], preferred_element_type=F32))
    col = lax.broadcasted_iota(jnp.int32, logits.shape, 1).astype(F32)
    neg = jnp.float32(-jnp.inf)
    lg = jnp.where(col < N_EXPERTS, logits, neg)
    m1 = jnp.max(lg, axis=-1, keepdims=True)
    i1 = jnp.min(jnp.where(lg == m1, col, float(LANES)), axis=-1, keepdims=True)
    lg2 = jnp.where(col == i1, neg, lg)
    m2 = jnp.max(lg2, axis=-1, keepdims=True)
    i2 = jnp.min(jnp.where(lg2 == m2, col, float(LANES)), axis=-1, keepdims=True)
    e = jnp.exp(m2 - m1)
    w1 = 1.0 / (1.0 + e)
    w2 = e / (1.0 + e)
    idx_ref[...] = jnp.where(col == 0, i1, jnp.where(col == 1, i2, 0.0)).astype(jnp.int32)
    wt_ref[...] = jnp.where(col == 0, w1, jnp.where(col == 1, w2, 0.0))


def moe_route(x, g, r_hi, r_lo, tm=512):
    t, d = x.shape
    row = lambda i: (i, 0)
    fixed = lambda i: (0, 0)
    return pl.pallas_call(
        _route_body,
        grid=(t // tm,),
        in_specs=[pl.BlockSpec((tm, d), row), pl.BlockSpec((1, d), fixed),
                  pl.BlockSpec((d, LANES), fixed), pl.BlockSpec((d, LANES), fixed)],
        out_specs=[pl.BlockSpec((tm, d), row), pl.BlockSpec((tm, LANES), row),
                   pl.BlockSpec((tm, LANES), row)],
        out_shape=[jax.ShapeDtypeStruct((t, d), F32),
                   jax.ShapeDtypeStruct((t, LANES), jnp.int32),
                   jax.ShapeDtypeStruct((t, LANES), F32)],
        compiler_params=_params("parallel"),
        name="moe_route",
    )(x, g.reshape(1, d), r_hi, r_lo)


def _experts_body(te_ref, nu_ref, src_ref, h_hbm, wg_ref, wu_ref, wd_ref, o_ref,
                  xbuf_ref, xs_ref, acc_ref, sem, *, rows_per_step):
    i = pl.program_id(0)
    f = pl.program_id(1)
    tm = xs_ref.shape[0]
    n_rows = xbuf_ref.shape[1]
    n_used = nu_ref[0]
    used = i < n_used
    has_next = i + 1 < n_used
    slot = i % 2

    def start_row(tile, slot_, r):
        tok = src_ref[tile * tm + r]
        pltpu.make_async_copy(h_hbm.at[pl.ds(tok, 1), :], xbuf_ref.at[slot_, pl.ds(r, 1), :],
                              sem.at[slot_]).start()

    @pl.when(f == 0)
    def _():
        acc_ref[...] = jnp.zeros_like(acc_ref)

        @pl.when(used)
        def _():
            @pl.when(i == 0)
            def _():
                def row(r, carry):
                    start_row(0, 0, r)
                    return carry

                lax.fori_loop(0, n_rows, row, 0, unroll=8)

            pltpu.make_async_copy(h_hbm.at[pl.ds(0, n_rows), :], xbuf_ref.at[slot], sem.at[slot]).wait()
            xs_ref[...] = xbuf_ref[slot, 0:tm, :].astype(BF16)

    def ffn_step():
        h = xs_ref[...]
        a = jnp.dot(h, wg_ref[0].astype(BF16), preferred_element_type=F32)
        u = jnp.dot(h, wu_ref[0].astype(BF16), preferred_element_type=F32)
        act = (a * jax.nn.sigmoid(a) * u).astype(BF16)
        acc_ref[...] += jnp.dot(act, wd_ref[0].astype(BF16), preferred_element_type=F32)

    @pl.when(used & has_next)
    def _():
        for j in range(rows_per_step):
            start_row(i + 1, 1 - slot, f * rows_per_step + j)
        ffn_step()

    @pl.when(used & jnp.logical_not(has_next))
    def _():
        ffn_step()

    @pl.when(f == pl.num_programs(1) - 1)
    def _():
        o_ref[...] = acc_ref[...].astype(o_ref.dtype)


def moe_experts(tile_expert, n_used, src, h, wg, wu, wd, tm, tf=512):
    d = h.shape[1]
    nf = wg.shape[2] // tf
    rows_per_step = 8 * (-(-tm // (8 * nf)))
    n_rows = rows_per_step * nf
    n_tiles = src.shape[0] // tm
    src = jnp.pad(src, (0, n_rows - tm))
    grid_spec = pltpu.PrefetchScalarGridSpec(
        num_scalar_prefetch=3,
        grid=(n_tiles, nf),
        in_specs=[pl.BlockSpec(memory_space=pl.ANY),
                  pl.BlockSpec((1, d, tf), lambda i, f, te, *_: (te[i], 0, f)),
                  pl.BlockSpec((1, d, tf), lambda i, f, te, *_: (te[i], 0, f)),
                  pl.BlockSpec((1, tf, d), lambda i, f, te, *_: (te[i], f, 0))],
        out_specs=pl.BlockSpec((tm, d), lambda i, f, *_: (i, 0)),
        scratch_shapes=[pltpu.VMEM((2, n_rows, d), F32), pltpu.VMEM((tm, d), BF16),
                        pltpu.VMEM((tm, d), F32), pltpu.SemaphoreType.DMA((2,))],
    )
    return pl.pallas_call(
        functools.partial(_experts_body, rows_per_step=rows_per_step),
        grid_spec=grid_spec,
        out_shape=jax.ShapeDtypeStruct((n_tiles * tm, d), BF16),
        compiler_params=_params("arbitrary", "arbitrary"),
        name="moe_experts",
    )(tile_expert, n_used, src, h, wg, wu, wd)


def _combine_norm_body(x_ref, ya_ref, yb_ref, wt_ref, g_ref, o_ref):
    wt = wt_ref[...]
    y = wt[:, 0:1] * ya_ref[...].astype(F32) + wt[:, 1:2] * yb_ref[...].astype(F32)
    o_ref[...] = _rms(x_ref[...] + y, g_ref[...])


def combine_norm(x, ya, yb, wts, g, tm=1024):
    t, d = x.shape
    row = lambda i: (i, 0)
    return pl.pallas_call(
        _combine_norm_body,
        grid=(t // tm,),
        in_specs=[pl.BlockSpec((tm, d), row)] * 3 + [pl.BlockSpec((tm, LANES), row),
                                                     pl.BlockSpec((1, d), lambda i: (0, 0))],
        out_specs=pl.BlockSpec((tm, d), row),
        out_shape=jax.ShapeDtypeStruct((t, d), F32),
        compiler_params=_params("parallel"),
        name="combine_norm",
    )(x, ya, yb, wts, g.reshape(1, d))


def moe_layer(x, g_norm, router, wg, wu, wd, g_final, tm=1024):
    t, d = x.shape
    r = jnp.zeros((d, LANES), F32).at[:, :N_EXPERTS].set(router)
    r_hi = r.astype(BF16)
    r_lo = (r - r_hi.astype(F32)).astype(BF16)
    h, idx, wts = moe_route(x, g_norm, r_hi, r_lo)
    e_flat = jnp.concatenate([idx[:, 0], idx[:, 1]])
    onehot = (e_flat[:, None] == jnp.arange(N_EXPERTS)[None, :]).astype(jnp.int32)
    csum = jnp.cumsum(onehot, axis=0)
    rank = jnp.sum((csum - onehot) * onehot, axis=1)
    counts = csum[-1]
    padded = ((counts + tm - 1) // tm) * tm
    ends = jnp.cumsum(padded)
    starts = ends - padded
    dest = starts[e_flat] + rank
    n_tiles = (TOP_K * t) // tm + N_EXPERTS
    p = n_tiles * tm
    tok = jnp.concatenate([jnp.arange(t, dtype=jnp.int32)] * TOP_K)
    src = jnp.zeros((p,), jnp.int32).at[dest].set(tok)
    tile_expert = jnp.minimum(
        jnp.searchsorted(ends, jnp.arange(n_tiles, dtype=jnp.int32) * tm, side="right"),
        N_EXPERTS - 1).astype(jnp.int32)
    n_used = (ends[-1] // tm).astype(jnp.int32).reshape(1)
    ys = moe_experts(tile_expert, n_used, src, h, wg, wu, wd, tm)
    ya = jnp.take(ys, dest[:t], axis=0, mode="clip")
    yb = jnp.take(ys, dest[t:], axis=0, mode="clip")
    return combine_norm(x, ya, yb, wts, g_final)


SSM_TC = 512
SSM_SUB = 128
SSM_HALF = 256
SSM_NSTATE = SSM_GROUPS * SSM_STATE


def _ssm_body(u_ref, bw_ref, cw_ref, d_ref, glu_ref, ar_ref, ai_ref, pr_ref, pi_ref, o_ref,
              xr_ref, xi_ref, cr_ref, ci_ref):
    @pl.when(pl.program_id(1) == 0)
    def _():
        cr_ref[...] = jnp.zeros_like(cr_ref)
        ci_ref[...] = jnp.zeros_like(ci_ref)

    ub = u_ref[...]
    u = ub.astype(F32)
    nblk = SSM_WIDTH // SSM_HALF
    sblk = SSM_NSTATE // nblk
    for k in range(nblk):
        bu = jnp.dot(ub[:, k * SSM_HALF:(k + 1) * SSM_HALF], bw_ref[k], preferred_element_type=F32)
        xr_ref[:, k * sblk:(k + 1) * sblk] = bu[:, :sblk]
        xi_ref[:, k * sblk:(k + 1) * sblk] = bu[:, sblk:]

    row8 = lax.broadcasted_iota(jnp.int32, (SSM_SUB, LANES), 0) % 8
    n_grp = SSM_SUB // 8

    def strip(c, carry):
        col = pl.ds(pl.multiple_of(c * LANES, LANES), LANES)
        c_r = cr_ref[:, col]
        c_i = ci_ref[:, col]
        p_r = pr_ref[0:8, col]
        p_i = pi_ref[0:8, col]
        step_mul = [(jnp.where(row8 >= (1 << i), ar_ref[i:i + 1, col], 0.0),
                     jnp.where(row8 >= (1 << i), ai_ref[i:i + 1, col], 0.0)) for i in range(3)]
        for sub in range(SSM_TC // SSM_SUB):
            rows = slice(sub * SSM_SUB, (sub + 1) * SSM_SUB)
            xr = xr_ref[rows, col]
            xi = xi_ref[rows, col]
            for i in range(3):
                a_r, a_i = step_mul[i]
                sr, si = pltpu.roll(xr, 1 << i, 0), pltpu.roll(xi, 1 << i, 0)
                xr, xi = xr + a_r * sr - a_i * si, xi + a_r * si + a_i * sr
            out_r, out_i = [], []
            for r in range(n_grp):
                g_r = xr[8 * r:8 * r + 8] + p_r * c_r - p_i * c_i
                g_i = xi[8 * r:8 * r + 8] + p_r * c_i + p_i * c_r
                c_r, c_i = g_r[7:8, :], g_i[7:8, :]
                out_r.append(g_r)
                out_i.append(g_i)
            xr_ref[rows, col] = jnp.concatenate(out_r, axis=0)
            xi_ref[rows, col] = jnp.concatenate(out_i, axis=0)
        cr_ref[:, col] = c_r
        ci_ref[:, col] = c_i
        return carry

    lax.fori_loop(0, SSM_NSTATE // LANES, strip, 0)

    ys = []
    for k in range(nblk):
        st = jnp.concatenate([xr_ref[:, k * sblk:(k + 1) * sblk].astype(BF16),
                              xi_ref[:, k * sblk:(k + 1) * sblk].astype(BF16)], axis=1)
        ys.append(jnp.dot(st, cw_ref[k], preferred_element_type=F32))
    y = jax.nn.gelu(jnp.concatenate(ys, axis=1) + d_ref[...] * u)
    gated = y * jax.nn.sigmoid(jnp.dot(y.astype(BF16), glu_ref[...], preferred_element_type=F32))
    o_ref[...] = gated.astype(o_ref.dtype)


def ssm_mixer(proj, bs, a_re, a_im, log_dt, b_re, b_im, c_re, c_im, d_skip, glu_w):
    t = proj.shape[0]
    s = t // bs
    nt = s // SSM_TC
    assert s % SSM_TC == 0 and OFF_SSM % SSM_WIDTH == 0
    nblk = SSM_WIDTH // SSM_HALF
    gpb = SSM_GROUPS // nblk
    dt = jnp.exp(log_dt)[:, None]
    mag = jnp.exp(a_re * dt)
    abar_r, abar_i = mag * jnp.cos(a_im * dt), mag * jnp.sin(a_im * dt)
    nr, ni = abar_r - 1.0, abar_i
    den = a_re * a_re + a_im * a_im
    sr, si = (nr * a_re + ni * a_im) / den, (ni * a_re - nr * a_im) / den
    bbar_r = sr[..., None] * b_re - si[..., None] * b_im
    bbar_i = sr[..., None] * b_im + si[..., None] * b_re

    def powers(k):
        kk = k.astype(F32)[:, None, None]
        m = jnp.exp(kk * (a_re * dt))
        return ((m * jnp.cos(kk * (a_im * dt))).reshape(-1, SSM_NSTATE),
                (m * jnp.sin(kk * (a_im * dt))).reshape(-1, SSM_NSTATE))

    ar, ai = powers(2 ** jnp.arange(8))
    pr, pi = powers(jnp.arange(1, SSM_SUB + 1))
    eye = jnp.eye(gpb, dtype=F32)

    def in_block(w):
        return jnp.einsum('gnp,gh->gphn', w, eye).reshape(gpb * SSM_GROUP, gpb * SSM_STATE)

    def out_block(w):
        return jnp.einsum('gpn,gh->gnhp', w, eye).reshape(gpb * SSM_STATE, gpb * SSM_GROUP)

    bw = jnp.stack([jnp.concatenate([in_block(bbar_r[k * gpb:(k + 1) * gpb]),
                                     in_block(bbar_i[k * gpb:(k + 1) * gpb])], axis=1)
                    for k in range(nblk)]).astype(BF16)
    cw = jnp.stack([jnp.concatenate([out_block(c_re[k * gpb:(k + 1) * gpb]),
                                     -out_block(c_im[k * gpb:(k + 1) * gpb])], axis=0)
                    for k in range(nblk)]).astype(BF16)
    fixed2 = lambda b, i: (0, 0)
    fixed3 = lambda b, i: (0, 0, 0)
    return pl.pallas_call(
        _ssm_body,
        grid=(bs, nt),
        in_specs=[pl.BlockSpec((SSM_TC, SSM_WIDTH), lambda b, i: (b * nt + i, OFF_SSM // SSM_WIDTH)),
                  pl.BlockSpec(bw.shape, fixed3), pl.BlockSpec(cw.shape, fixed3),
                  pl.BlockSpec((1, SSM_WIDTH), fixed2), pl.BlockSpec((SSM_WIDTH, SSM_WIDTH), fixed2),
                  pl.BlockSpec((8, SSM_NSTATE), fixed2), pl.BlockSpec((8, SSM_NSTATE), fixed2),
                  pl.BlockSpec((SSM_SUB, SSM_NSTATE), fixed2), pl.BlockSpec((SSM_SUB, SSM_NSTATE), fixed2)],
        out_specs=pl.BlockSpec((SSM_TC, SSM_WIDTH), lambda b, i: (b * nt + i, 0)),
        out_shape=jax.ShapeDtypeStruct((t, SSM_WIDTH), BF16),
        scratch_shapes=[pltpu.VMEM((SSM_TC, SSM_NSTATE), F32), pltpu.VMEM((SSM_TC, SSM_NSTATE), F32),
                        pltpu.VMEM((1, SSM_NSTATE), F32), pltpu.VMEM((1, SSM_NSTATE), F32)],
        compiler_params=_params("parallel", "arbitrary"),
        name="ssm_scan",
    )(proj, bw, cw, d_skip.reshape(1, SSM_WIDTH), glu_w.astype(BF16), ar, ai, pr, pi)


GDN_TC = 256
GDN_HALO = 8


def _gdn_body(nega_ref, dtb_ref, q_ref, k_ref, v_ref, z_ref, sm_ref, wq_ref, wk_ref, wv_ref, ng_ref,
              o_ref, state_ref, hq_ref, hk_ref, hv_ref, vnew_ref):
    TC, CH, DK, H = GDN_TC, GDN_CHUNK, GDN_DK, GDN_HEADS
    heads = range(H)

    @pl.when(pl.program_id(1) == 0)
    def _():
        state_ref[...] = jnp.zeros_like(state_ref)
        hq_ref[...] = jnp.zeros_like(hq_ref)
        hk_ref[...] = jnp.zeros_like(hk_ref)
        hv_ref[...] = jnp.zeros_like(hv_ref)

    def conv_silu(x_ref, halo_ref, w_ref):
        x = x_ref[...].astype(F32)
        xx = jnp.concatenate([halo_ref[...], x], axis=0)
        w = w_ref[...]
        y = w[GDN_CONV - 1:GDN_CONV] * x
        for d in range(1, GDN_CONV):
            y = y + w[GDN_CONV - 1 - d:GDN_CONV - d] * pltpu.roll(xx, d, 0)[GDN_HALO:GDN_HALO + TC]
        halo_ref[...] = x[TC - GDN_HALO:TC]
        return y * jax.nn.sigmoid(y)

    def per_head(x):
        return [x[:, h * LANES:(h + 1) * LANES] for h in heads]

    q = per_head(conv_silu(q_ref, hq_ref, wq_ref))
    k = per_head(conv_silu(k_ref, hk_ref, wk_ref))
    v = per_head(conv_silu(v_ref, hv_ref, wv_ref))
    q = [x * lax.rsqrt(jnp.sum(x * x, axis=-1, keepdims=True) + 1e-6) * (DK ** -0.5) for x in q]
    k = [x * lax.rsqrt(jnp.sum(x * x, axis=-1, keepdims=True) + 1e-6) for x in k]

    small = sm_ref[...].astype(F32)
    beta = [jax.nn.sigmoid(small[:, H + h:H + h + 1]) for h in heads]
    la = []
    for h in heads:
        xa = small[:, h:h + 1] + dtb_ref[h]
        softplus = jnp.maximum(xa, 0.0) + jnp.log(1.0 + jnp.exp(-jnp.abs(xa)))
        la.append(jnp.broadcast_to(nega_ref[h] * softplus, (TC, LANES)))

    ri = lax.broadcasted_iota(jnp.int32, (TC, TC), 0)
    ci = lax.broadcasted_iota(jnp.int32, (TC, TC), 1)
    same = (ri // CH) == (ci // CH)
    causal = jnp.where(same, jnp.where(ci <= ri, 1.0, 0.0), 0.0)
    strict = jnp.where(ci < ri, causal, 0.0)
    eye = jnp.where(ri == ci, 1.0, 0.0)
    tri = causal.astype(BF16)
    lane = lax.broadcasted_iota(jnp.int32, (TC, LANES), 1)

    g = []
    for h in heads:
        la_hi, la_lo = _split_bf16(la[h])
        g.append(jnp.dot(tri, la_hi, preferred_element_type=F32)
                 + jnp.dot(tri, la_lo, preferred_element_type=F32))
    decay = []
    for h in heads:
        g_hi = g[h].astype(BF16).astype(F32)
        g_lo = g[h] - g_hi
        lhs = jnp.where(lane == 0, g_hi, jnp.where(lane == 1, g_lo, jnp.where(lane < 4, 1.0, 0.0)))
        rhs = jnp.where(lane < 2, 1.0, jnp.where(lane == 2, -g_hi, jnp.where(lane == 3, -g_lo, 0.0)))
        decay.append(jnp.exp(jnp.where(causal > 0.5, _dot_nt(lhs.astype(BF16), rhs.astype(BF16)), MASKED)))

    kb = [k[h] * beta[h] for h in heads]
    k_b = [x.astype(BF16) for x in k]
    lmat = [strict * _dot_nt(kb[h].astype(BF16), k_b[h]) * decay[h] for h in heads]
    tinv = [eye - x for x in lmat]
    pw = lmat
    for _ in range(CH.bit_length() - 2):
        pw = [jnp.dot(x.astype(BF16), x.astype(BF16), preferred_element_type=F32) for x in pw]
        tinv = [tinv[h] + jnp.dot(tinv[h].astype(BF16), pw[h].astype(BF16), preferred_element_type=F32)
                for h in heads]
    eg = [jnp.exp(x) for x in g]
    sol = [jnp.dot(tinv[h].astype(BF16),
                   jnp.concatenate([v[h] * beta[h], kb[h] * eg[h]], axis=1).astype(BF16),
                   preferred_element_type=F32) for h in heads]
    attn = [(causal * _dot_nt(q[h].astype(BF16), k_b[h]) * decay[h]).astype(BF16) for h in heads]
    q_st = [(q[h] * eg[h]).astype(BF16) for h in heads]

    vnew_ref[...] = jnp.zeros_like(vnew_ref)
    for c in range(TC // CH):
        rows = slice(c * CH, (c + 1) * CH)
        g_last = [x[(c + 1) * CH - 1:(c + 1) * CH, :] for x in g]
        st = [state_ref[h] for h in heads]
        st_b = [x.astype(BF16) for x in st]
        v_new = [sol[h][rows, :GDN_DV]
                 - jnp.dot(sol[h][rows, GDN_DV:].astype(BF16), st_b[h], preferred_element_type=F32)
                 for h in heads]
        for h in heads:
            vnew_ref[h, rows, :] = v_new[h].astype(BF16)
        o_c = [jnp.dot(q_st[h][rows], st_b[h], preferred_element_type=F32)
               + jnp.dot(attn[h][rows], vnew_ref[h], preferred_element_type=F32) for h in heads]
        for h in heads:
            k_st = (k[h][rows] * jnp.exp(g_last[h] - g[h][rows])).astype(BF16)
            state_ref[h] = st[h] * jnp.exp(g_last[h][:, :1]) + _dot_tn(k_st, v_new[h].astype(BF16))
        for h in heads:
            o = o_c[h] * lax.rsqrt(jnp.mean(o_c[h] * o_c[h], axis=-1, keepdims=True) + RMS_EPS) * ng_ref[...]
            zc = z_ref[rows, h * LANES:(h + 1) * LANES].astype(F32)
            o_ref[rows, h * LANES:(h + 1) * LANES] = (o * (zc * jax.nn.sigmoid(zc))).astype(o_ref.dtype)


def gdn_mixer(proj, bs, conv_w, a_log, dt_bias, norm_g):
    t = proj.shape[0]
    s = t // bs
    nt = s // GDN_TC
    H = GDN_HEADS
    hw = H * LANES
    assert s % GDN_TC == 0 and GDN_DK == LANES and GDN_DV == LANES and OFF_QKV == 0 and OFF_Z % hw == 0
    cw = jnp.zeros((8, GDN_QKV), F32).at[:GDN_CONV].set(conv_w)
    tok = lambda blk: pl.BlockSpec((GDN_TC, hw), lambda b, i: (b * nt + i, blk))
    wspec = lambda blk: pl.BlockSpec((8, hw), lambda b, i: (0, blk))
    smem = pl.BlockSpec(memory_space=pltpu.SMEM)
    return pl.pallas_call(
        _gdn_body,
        grid=(bs, nt),
        in_specs=[smem, smem, tok(0), tok(1), tok(2), tok(OFF_Z // hw),
                  pl.BlockSpec((GDN_TC, 2 * LANES), lambda b, i: (b * nt + i, OFF_SMALL // (2 * LANES))),
                  wspec(0), wspec(1), wspec(2), pl.BlockSpec((1, GDN_DV), lambda b, i: (0, 0))],
        out_specs=pl.BlockSpec((GDN_TC, hw), lambda b, i: (b * nt + i, 0)),
        out_shape=jax.ShapeDtypeStruct((t, hw), BF16),
        scratch_shapes=[pltpu.VMEM((H, GDN_DK, GDN_DV), F32)] + [pltpu.VMEM((GDN_HALO, hw), F32)] * 3
        + [pltpu.VMEM((H, GDN_TC, GDN_DV), BF16)],
        compiler_params=_params("parallel", "arbitrary"),
        name="gdn_delta",
    )(-jnp.exp(a_log), dt_bias.astype(F32), proj, proj, proj, proj, proj, cw, cw, cw,
      norm_g.reshape(1, GDN_DV))


NSA_KT = 256
NSA_NEG = MASKED
NSA_WTILES = NSA_WINDOW // LANES + 1


def _bucket_table():
    n = np.arange(LANES)
    max_exact = REL_BUCKETS // 2
    nf = np.maximum(n, 1).astype(np.float32)
    large = max_exact + (np.log(nf / np.float32(max_exact)) / np.float32(math.log(REL_MAX_DIST / max_exact))
                         * np.float32(REL_BUCKETS - max_exact)).astype(np.int32)
    tbl = np.where(n < max_exact, n, np.minimum(large, REL_BUCKETS - 1))
    assert tbl[-1] == REL_BUCKETS - 1 and REL_MAX_DIST <= LANES
    return tbl


def _compress_body(x_ref, pe_ref, w1_ref, w2_ref, o_ref):
    x = x_ref[0, 0, 0]
    w1 = w1_ref[0]
    half = NSA_CMP_STRIDE * NSA_DH
    nc = x.shape[0]
    a = jnp.dot(x, w1[:half], preferred_element_type=F32)
    b = jnp.dot(x, w1[half:], preferred_element_type=F32)
    pe_h = jnp.dot(pe_ref[0], w1, preferred_element_type=F32)
    hid = a + pltpu.roll(b, nc - 1, 0) + pe_h[0:1, :]
    o_ref[0, 0, 0] = jnp.dot(jax.nn.gelu(hid).astype(BF16), w2_ref[0], preferred_element_type=F32)


def nsa_compress(x, pe, w1, w2):
    _, bs, g, nc, width = x.shape
    return pl.pallas_call(
        _compress_body,
        grid=(2, bs, g),
        in_specs=[pl.BlockSpec((1, 1, 1, nc, width), lambda i, b, j: (i, b, j, 0, 0)),
                  pl.BlockSpec((1,) + pe.shape[1:], lambda i, b, j: (i, 0, 0)),
                  pl.BlockSpec((1,) + w1.shape[1:], lambda i, b, j: (i, 0, 0)),
                  pl.BlockSpec((1,) + w2.shape[1:], lambda i, b, j: (i, 0, 0))],
        out_specs=pl.BlockSpec((1, 1, 1, nc, NSA_DH), lambda i, b, j: (i, b, j, 0, 0)),
        out_shape=jax.ShapeDtypeStruct((2, bs, g, nc, NSA_DH), F32),
        compiler_params=_params("parallel", "parallel", "parallel"),
        name="nsa_compress",
    )(x, pe, w1, w2)


def _nsa_body(nfast, wstd, near_w, near_c, q_ref, gl_ref, qp_ref, kp_ref, cp_ref,
              ks_ref, vs_ref, kw_ref, vw_ref, kc_ref, vc_ref, agg_ref, td_ref, o_ref,
              m_ref, l_ref, acc_ref, *, n_sel, sel_k):
    QB, HG, G = NSA_QBLOCK, NSA_HPG, NSA_KV_GROUPS
    groups = range(G)
    qi = pl.program_id(1)
    s0 = qi * QB
    nc = kc_ref.shape[2]
    n_ct = nc // LANES
    q = [q_ref[0, g, 0] for g in groups]
    qp = qp_ref[...]

    def lanes4(x):
        return jnp.concatenate([x] * HG, axis=1)

    def delta_t(kp):
        idx = jnp.clip(_dot_nt(kp, qp), 0.0, float(LANES - 1)).astype(jnp.int32)
        outs = []
        for g in groups:
            heads = []
            for hg in range(HG):
                tb = jnp.broadcast_to(td_ref[g * HG + hg:g * HG + hg + 1, :], idx.shape)
                heads.append(jnp.take_along_axis(tb, idx, axis=1))
            outs.append(jnp.concatenate(heads, axis=1))
        return outs

    def maybe_delta(flag, kp):
        zeros = jnp.zeros((kp.shape[0], HG * QB), F32)
        return lax.cond(flag != 0, lambda: tuple(delta_t(kp)), lambda: (zeros,) * G)

    gate = [jax.nn.sigmoid(gl_ref[0, g, 0]) for g in groups]

    sc = [_dot_nt(kc_ref[0, g], q[g]) for g in groups]
    dl = [maybe_delta(near_c[qi * n_ct + ct], cp_ref[ct * LANES:(ct + 1) * LANES, :]) for ct in range(n_ct)]
    c_id = lax.broadcasted_iota(jnp.int32, (nc, QB), 0)
    c_q = lax.broadcasted_iota(jnp.int32, (nc, QB), 1)
    ok_c = lanes4(jnp.where(c_id * NSA_CMP_STRIDE + (NSA_CMP_LEN - 1) <= s0 + c_q, 1.0, 0.0))
    blk = lax.broadcasted_iota(jnp.int32, (LANES, QB), 0)
    t_tok = s0 + lax.broadcasted_iota(jnp.int32, (LANES, QB), 1)
    tb_blk = t_tok // NSA_SEL_LEN
    forced = jnp.where(blk == 0, 1.0, 0.0) + jnp.where(blk == tb_blk, 1.0, 0.0) \
        + jnp.where(blk == tb_blk - 1, 1.0, 0.0)
    blkf = blk.astype(F32)
    out, score = [], []
    for g in groups:
        s = sc[g] + jnp.concatenate([d[g] for d in dl], axis=0)
        s = jnp.where(ok_c > 0.5, s, NSA_NEG)
        e = jnp.exp2(s - jnp.max(s, axis=0, keepdims=True)) * ok_c
        p = e * (1.0 / jnp.maximum(jnp.sum(e, axis=0, keepdims=True), 1e-30))
        out.append(gate[g][0:1, :] * jnp.dot(vc_ref[0, g], p.astype(BF16), preferred_element_type=F32))
        ps = p[:, 0:QB]
        for hg in range(1, HG):
            ps = ps + p[:, hg * QB:(hg + 1) * QB]
        ps_hi, ps_lo = _split_bf16(ps)
        imp = (jnp.dot(agg_ref[...], ps_hi, preferred_element_type=F32)
               + jnp.dot(agg_ref[...], ps_lo, preferred_element_type=F32))
        sco = jnp.where(forced > 0.5, 1e9, jnp.where(blk * NSA_SEL_LEN <= t_tok, imp, -1e9))
        score.append(jnp.where(blk < n_sel, sco, -jnp.inf))
    sel = [jnp.zeros((LANES, QB), F32) for _ in groups]
    for _ in range(sel_k):
        for g in groups:
            mx = jnp.max(score[g], axis=0, keepdims=True)
            first = jnp.min(jnp.where(score[g] == mx, blkf, float(LANES)), axis=0, keepdims=True)
            hit = blkf == first
            sel[g] = jnp.where(hit, 1.0, sel[g])
            score[g] = jnp.where(hit, -jnp.inf, score[g])
    q2 = []
    for g in groups:
        sel_q = jnp.transpose(jnp.where(sel[g] > 0.5, 0.0, NSA_NEG))
        q2.append(jnp.concatenate([q[g], jnp.concatenate([sel_q] * HG, axis=0).astype(BF16)], axis=1))

    def reset():
        m_ref[...] = jnp.full_like(m_ref, NSA_NEG)
        l_ref[...] = jnp.zeros_like(l_ref)
        acc_ref[...] = jnp.zeros_like(acc_ref)

    def tile_step(g, k, s, v_t):
        m_old = m_ref[g, k]
        m_new = jnp.maximum(m_old, jnp.max(s, axis=0, keepdims=True))
        alpha = jnp.exp2(m_old - m_new)
        p = jnp.exp2(s - m_new)
        l_ref[g, k] = l_ref[g, k] * alpha + jnp.sum(p, axis=0, keepdims=True)
        acc_ref[g, k] = acc_ref[g, k] * alpha + jnp.dot(v_t, p.astype(BF16), preferred_element_type=F32)
        m_ref[g, k] = m_new

    def finish(g):
        m = jnp.maximum(m_ref[g, 0], m_ref[g, 1])
        w0, w1 = jnp.exp2(m_ref[g, 0] - m), jnp.exp2(m_ref[g, 1] - m)
        return (acc_ref[g, 0] * w0 + acc_ref[g, 1] * w1) * (1.0 / (l_ref[g, 0] * w0 + l_ref[g, 1] * w1))

    key_r = lax.broadcasted_iota(jnp.int32, (NSA_KT, QB), 0)
    key_q = lax.broadcasted_iota(jnp.int32, (NSA_KT, QB), 1)

    def sel_scores(kt, with_delta, diag):
        k0 = pl.multiple_of(kt * NSA_KT, NSA_KT)
        s = [_dot_nt(ks_ref[0, g, pl.ds(k0, NSA_KT), :], q2[g]) for g in groups]
        if with_delta:
            d = delta_t(kp_ref[pl.ds(k0, NSA_KT), :])
            s = [s[g] + d[g] for g in groups]
        if diag:
            causal = lanes4(jnp.where(k0 + key_r <= s0 + key_q, 0.0, NSA_NEG))
            s = [x + causal for x in s]
        return [(s[g], vs_ref[0, g, :, pl.ds(k0, NSA_KT)]) for g in groups]

    reset()
    n_past = (s0 + QB - 1) // NSA_KT
    n_pairs = nfast[qi] // 2

    def fast_body(i, carry):
        a = sel_scores(2 * i, False, False)
        b = sel_scores(2 * i + 1, False, False)
        for g in groups:
            tile_step(g, 0, *a[g])
        for g in groups:
            tile_step(g, 1, *b[g])
        return carry

    def slow_body(kt, carry):
        a = sel_scores(kt, True, False)
        for g in groups:
            tile_step(g, 1, *a[g])
        return carry

    lax.fori_loop(0, n_pairs, fast_body, 0)
    lax.fori_loop(2 * n_pairs, n_past, slow_body, 0)
    a = sel_scores(n_past, True, True)
    for g in groups:
        tile_step(g, 0, *a[g])
    out = [out[g] + gate[g][1:2, :] * finish(g) for g in groups]

    reset()
    w_r = lax.broadcasted_iota(jnp.int32, (LANES, QB), 0)
    w_q = lax.broadcasted_iota(jnp.int32, (LANES, QB), 1)
    in_window = lanes4(jnp.where(w_r > w_q, 0.0, NSA_NEG))
    causal_w = lanes4(jnp.where(w_r <= w_q, 0.0, NSA_NEG))

    def win_scores(g, st, n):
        st = pl.multiple_of(st, LANES)
        return _dot_nt(kw_ref[0, g, pl.ds(st, n), :], q[g]), vw_ref[0, g, :, pl.ds(st, n)]

    @pl.when(wstd[qi] != 0)
    def _():
        half = NSA_WINDOW // 2
        zeros = jnp.zeros((LANES, HG * QB), F32)
        sa = [win_scores(g, s0 - NSA_WINDOW, half) for g in groups]
        sb = [win_scores(g, s0 - half, half) for g in groups]
        sd = [win_scores(g, s0, LANES) for g in groups]
        near = delta_t(kp_ref[pl.ds(pl.multiple_of(s0 - LANES, LANES), LANES), :])
        diag = delta_t(kp_ref[pl.ds(pl.multiple_of(s0, LANES), LANES), :])
        for g in groups:
            tile_step(g, 0, sa[g][0] + jnp.concatenate([in_window, zeros], axis=0), sa[g][1])
        for g in groups:
            tile_step(g, 1, sb[g][0] + jnp.concatenate([zeros, near[g]], axis=0), sb[g][1])
        for g in groups:
            tile_step(g, 0, sd[g][0] + diag[g] + causal_w, sd[g][1])

    for j in range(NSA_WTILES):
        start = s0 - NSA_WINDOW + j * LANES

        @pl.when((wstd[qi] == 0) & (start >= 0))
        def _():
            sw = [win_scores(g, start, LANES) for g in groups]
            d = maybe_delta(near_w[qi * NSA_WTILES + j],
                            kp_ref[pl.ds(pl.multiple_of(start, LANES), LANES), :])
            for g in groups:
                s = sw[g][0] + d[g]
                if j == 0:
                    s = s + in_window
                elif j == NSA_WTILES - 1:
                    s = s + causal_w
                tile_step(g, j % 2, s, sw[g][1])

    heads_out = []
    for g in groups:
        o_t = out[g] + gate[g][2:3, :] * finish(g)
        heads_out += [jnp.transpose(o_t[:, hg * QB:(hg + 1) * QB]) for hg in range(HG)]
    o_ref[0] = jnp.concatenate(heads_out, axis=1).astype(o_ref.dtype)


def _pos_digits(p, key_side):
    d0, d1, d2 = (p & 127).astype(F32), ((p >> 7) & 127).astype(F32), (p >> 14).astype(F32)
    one = jnp.ones_like(d0)
    if key_side:
        cols = [one, one, one, -d2, -d1, -d0]
    else:
        cols = [16384.0 * d2, 128.0 * d1, d0, 16384.0 * one, 128.0 * one, one]
    out = jnp.stack(cols, axis=-1)
    return jnp.pad(out, ((0, 0), (0, LANES - out.shape[-1]))).astype(BF16)


def nsa_mixer(q, k_c, v_c, k_s, v_s, k_w, v_w, gate_logits, positions,
              ck_pe, ck_w1, ck_w2, cv_pe, cv_w1, cv_w2, rel_bias):
    q, k_c, v_c, k_s, v_s, k_w, v_w = lax.optimization_barrier((q, k_c, v_c, k_s, v_s, k_w, v_w))
    bs, s, _ = q.shape
    G, HG, DH, QB = NSA_KV_GROUPS, NSA_HPG, NSA_DH, NSA_QBLOCK
    nqt = s // QB
    nc = s // NSA_CMP_STRIDE
    n_cmp = (s - NSA_CMP_LEN) // NSA_CMP_STRIDE + 1
    n_sel = s // NSA_SEL_LEN
    sel_k = min(NSA_SEL_TOPK, n_sel)
    n_kt = s // NSA_KT
    n_ct = nc // LANES
    assert s % (LANES * NSA_CMP_STRIDE) == 0 and n_sel <= LANES and NSA_KT == 2 * QB

    def by_group(t):
        return t.reshape(bs, s, G, DH).transpose(0, 2, 1, 3).astype(F32)

    def by_group_t(t):
        return t.reshape(bs, s, G, DH).transpose(0, 2, 3, 1).astype(BF16)

    ones2 = jnp.zeros((LANES - DH,), F32).at[:2].set(1.0)

    def keys_aug(t, blocks=False):
        parts = [t, jnp.broadcast_to(ones2, t.shape[:-1] + (LANES - DH,))]
        if blocks:
            onehot = (jnp.arange(s)[:, None] // NSA_SEL_LEN == jnp.arange(LANES)[None, :]).astype(F32)
            parts.append(jnp.broadcast_to(onehot, t.shape[:-2] + (s, LANES)))
        return jnp.concatenate(parts, axis=-1).astype(BF16)

    chunks = jnp.stack([by_group(k_c), by_group(v_c)]).reshape(2, bs, G, nc, NSA_CMP_STRIDE * DH)
    pe = jnp.stack([ck_pe, cv_pe]).reshape(2, 1, NSA_CMP_LEN * DH)
    cmp = nsa_compress(chunks.astype(BF16), jnp.broadcast_to(pe, (2, 8, NSA_CMP_LEN * DH)).astype(BF16),
                       jnp.stack([ck_w1, cv_w1]).astype(BF16), jnp.stack([ck_w2, cv_w2]).astype(BF16))
    kc, vc_t = keys_aug(cmp[0]), cmp[1].transpose(0, 1, 3, 2).astype(BF16)

    log2e = math.log2(math.e)
    rel_bias = rel_bias.astype(F32) * log2e
    far = rel_bias[REL_BUCKETS - 1]
    far_hi = far.astype(BF16).astype(F32)
    extra = jnp.zeros((NSA_HEADS, LANES - DH), F32).at[:, 0].set(far_hi).at[:, 1].set(far - far_hi)
    qh = (q.astype(F32) * (DH ** -0.5 * log2e)).reshape(bs, nqt, QB, G, HG, DH).transpose(0, 3, 1, 4, 2, 5)
    ex = jnp.broadcast_to(extra.reshape(1, G, 1, HG, 1, LANES - DH), (bs, G, nqt, HG, QB, LANES - DH))
    qs = jnp.concatenate([qh, ex], axis=-1).astype(BF16).reshape(bs, G, nqt, HG * QB, LANES)
    gl = gate_logits.astype(F32).reshape(bs, nqt, QB, G, HG, 3).transpose(0, 3, 1, 5, 4, 2)
    gl = jnp.pad(gl.reshape(bs, G, nqt, 3, HG * QB), ((0, 0),) * 3 + ((0, 5), (0, 0)))

    td = (rel_bias[_bucket_table()] - rel_bias[REL_BUCKETS - 1][None, :]).T.astype(F32)
    cmp_end = jnp.minimum(jnp.arange(nc) * NSA_CMP_STRIDE + NSA_CMP_LEN - 1, s - 1)
    cpos = positions[cmp_end]
    qmin = positions.reshape(nqt, QB).min(axis=1)
    kmax = positions.reshape(nqt, QB).max(axis=1)
    near_s = (qmin[:, None] - positions.reshape(n_kt, NSA_KT).max(axis=1)[None, :]) < REL_MAX_DIST
    n_past = (jnp.arange(nqt) * QB + QB - 1) // NSA_KT
    n_fast = jnp.minimum(jnp.argmax(jnp.concatenate([near_s, jnp.ones((nqt, 1), bool)], axis=1), axis=1),
                         n_past)
    wt = jnp.clip(jnp.arange(nqt)[:, None] - (NSA_WTILES - 1) + jnp.arange(NSA_WTILES)[None, :], 0, nqt - 1)
    near_w = (qmin[:, None] - kmax[wt]) < REL_MAX_DIST
    near_c = (qmin[:, None] - cpos.reshape(n_ct, LANES).max(axis=1)[None, :]) < REL_MAX_DIST
    w_std = ((jnp.arange(nqt) >= NSA_WTILES - 1)
             & jnp.all(near_w == (jnp.arange(NSA_WTILES) >= NSA_WTILES - 2)[None, :], axis=1))

    cmp_start = np.arange(nc) * NSA_CMP_STRIDE
    sel_start = np.arange(LANES) * NSA_SEL_LEN
    agg_t = ((cmp_start[None, :] <= sel_start[:, None] + NSA_SEL_LEN - 1)
             & (cmp_start[None, :] + NSA_CMP_LEN - 1 >= sel_start[:, None])
             & (np.arange(nc)[None, :] < n_cmp) & (np.arange(LANES)[:, None] < n_sel))

    cols = HG * QB
    full = lambda shape: pl.BlockSpec(shape, lambda b, i, *_: (0,) * len(shape))
    per_b = lambda n, w: pl.BlockSpec((1, G, n, w), lambda b, i, *_: (b, 0, 0, 0))
    per_tile = lambda n, w: pl.BlockSpec((1, G, 1, n, w), lambda b, i, *_: (b, 0, i, 0, 0))
    grid_spec = pltpu.PrefetchScalarGridSpec(
        num_scalar_prefetch=4,
        grid=(bs, nqt),
        in_specs=[per_tile(cols, LANES), per_tile(8, cols),
                  pl.BlockSpec((QB, LANES), lambda b, i, *_: (i, 0)),
                  full((s, LANES)), full((nc, LANES)),
                  per_b(s, 2 * LANES), per_b(DH, s), per_b(s, LANES), per_b(DH, s),
                  per_b(nc, LANES), per_b(DH, nc),
                  full((LANES, nc)), full((NSA_HEADS, LANES))],
        out_specs=pl.BlockSpec((1, QB, NSA_Q), lambda b, i, *_: (b, i, 0)),
        scratch_shapes=[pltpu.VMEM((G, 2, 1, cols), F32), pltpu.VMEM((G, 2, 1, cols), F32),
                        pltpu.VMEM((G, 2, DH, cols), F32)],
    )
    return pl.pallas_call(
        functools.partial(_nsa_body, n_sel=n_sel, sel_k=sel_k),
        grid_spec=grid_spec,
        out_shape=jax.ShapeDtypeStruct((bs, s, NSA_Q), BF16),
        compiler_params=_params("parallel", "arbitrary"),
        name="nsa_attention",
    )(n_fast.astype(jnp.int32), w_std.astype(jnp.int32), near_w.reshape(-1).astype(jnp.int32),
      near_c.reshape(-1).astype(jnp.int32),
      qs, gl, _pos_digits(positions, False), _pos_digits(positions, True), _pos_digits(cpos, True),
      keys_aug(by_group(k_s), blocks=True), by_group_t(v_s), keys_aug(by_group(k_w)), by_group_t(v_w),
      kc, vc_t, jnp.asarray(agg_t, BF16), td)


SGU_TC = 512


def _sgu_body(uv_ref, lg_ref, lb_ref, w_ref, b_ref, o_ref):
    uv = jax.nn.gelu(uv_ref[...].astype(F32))
    u, v = uv[:, :SGU_WIDTH], uv[:, SGU_WIDTH:]
    mu = jnp.mean(v, axis=-1, keepdims=True)
    vc = v - mu
    var = jnp.mean(vc * vc, axis=-1, keepdims=True)
    vn = (vc * lax.rsqrt(var + RMS_EPS) * lg_ref[...] + lb_ref[...]).astype(BF16)
    gw = SGU_WIDTH // SGU_GROUPS
    for c in range(SGU_TC // SGU_CHUNK):
        rows = slice(c * SGU_CHUNK, (c + 1) * SGU_CHUNK)
        for g in range(SGU_GROUPS):
            cols = slice(g * gw, (g + 1) * gw)
            mix = jnp.dot(w_ref[g], vn[rows, cols], preferred_element_type=F32) + b_ref[:, g:g + 1]
            o_ref[rows, cols] = (u[rows, cols] * mix).astype(o_ref.dtype)


def sgu_mixer(proj, ln_g, ln_b, w_s, b_s):
    t = proj.shape[0]
    assert t % SGU_TC == 0 and OFF_SGU % (2 * SGU_WIDTH) == 0
    fixed = lambda i: (0, 0)
    return pl.pallas_call(
        _sgu_body,
        grid=(t // SGU_TC,),
        in_specs=[pl.BlockSpec((SGU_TC, 2 * SGU_WIDTH), lambda i: (i, OFF_SGU // (2 * SGU_WIDTH))),
                  pl.BlockSpec((1, SGU_WIDTH), fixed), pl.BlockSpec((1, SGU_WIDTH), fixed),
                  pl.BlockSpec((SGU_GROUPS, SGU_CHUNK, SGU_CHUNK), lambda i: (0, 0, 0)),
                  pl.BlockSpec((SGU_CHUNK, SGU_GROUPS), fixed)],
        out_specs=pl.BlockSpec((SGU_TC, SGU_WIDTH), lambda i: (i, 0)),
        out_shape=jax.ShapeDtypeStruct((t, SGU_WIDTH), BF16),
        compiler_params=_params("parallel"),
        name="sgu_gate",
    )(proj, ln_g.reshape(1, SGU_WIDTH), ln_b.reshape(1, SGU_WIDTH), jnp.tril(w_s).astype(BF16), b_s.T)


def _permute_w_in(w):
    sizes = ([SSM_WIDTH, GDN_QKV, GDN_HEADS, GDN_HEADS, GDN_HEADS * GDN_DV, NSA_Q]
             + [NSA_KV] * 6 + [3 * NSA_HEADS, 2 * SGU_WIDTH, N_BRANCH * D_MODEL])
    cuts = [int(c) for c in np.cumsum(sizes)[:-1]]
    (w_ssm, w_qkv, w_a, w_b, w_z, w_nq, w_kc, w_vc, w_ks, w_vs, w_kw, w_vw,
     w_ng, w_sgu, w_gate) = jnp.split(w, cuts, axis=-1)
    pad = jnp.zeros((w.shape[0], PROJ_COLS - OFF_SMALL - SMALL_USED), w.dtype)
    return jnp.concatenate([w_qkv, w_ssm, w_z, w_nq, w_sgu, w_gate, w_kc, w_vc, w_ks, w_vs,
                            w_kw, w_vw, w_a, w_b, w_ng, pad], axis=-1)


def kernel(x, positions, norm_mix, norm_ffn, norm_final, w_in, ssm_a_re, ssm_a_im, ssm_log_dt, ssm_b_re, ssm_b_im, ssm_c_re, ssm_c_im, ssm_d, ssm_glu_w, gdn_conv_w, gdn_a_log, gdn_dt_bias, gdn_norm, nsa_cmp_k_pe, nsa_cmp_k_w1, nsa_cmp_k_w2, nsa_cmp_v_pe, nsa_cmp_v_w1, nsa_cmp_v_w2, rel_bias, sgu_ln_g, sgu_ln_b, sgu_w, sgu_b, w_br_ssm, w_br_gdn, w_br_nsa, w_br_sgu, w_out, ffn_w_gate, ffn_w_up, ffn_w_down, moe_router, moe_w_gate, moe_w_up, moe_w_down):
    bs, s, d = x.shape
    t = bs * s
    assert DEPTH == 2
    xf = x.reshape(t, d)
    for layer in range(DEPTH):
        proj = in_proj(xf, norm_mix[layer], _permute_w_in(w_in[layer]))
        p3 = proj.reshape(bs, s, PROJ_COLS)

        def seg(off, width):
            return p3[..., off:off + width]

        y_ssm = ssm_mixer(proj, bs, ssm_a_re[layer], ssm_a_im[layer], ssm_log_dt[layer],
                          ssm_b_re[layer], ssm_b_im[layer], ssm_c_re[layer], ssm_c_im[layer],
                          ssm_d[layer], ssm_glu_w[layer])
        y_gdn = gdn_mixer(proj, bs, gdn_conv_w[layer], gdn_a_log[layer], gdn_dt_bias[layer],
                          gdn_norm[layer])
        kvs = [seg(OFF_KV + i * NSA_KV, NSA_KV) for i in range(6)]
        y_nsa = nsa_mixer(seg(OFF_NQ, NSA_Q), *kvs, seg(OFF_SMALL + 2 * GDN_HEADS, 3 * NSA_HEADS),
                          positions, nsa_cmp_k_pe[layer], nsa_cmp_k_w1[layer], nsa_cmp_k_w2[layer],
                          nsa_cmp_v_pe[layer], nsa_cmp_v_w1[layer], nsa_cmp_v_w2[layer], rel_bias)
        y_sgu = sgu_mixer(proj, sgu_ln_g[layer], sgu_ln_b[layer], sgu_w[layer], sgu_b[layer])
        ys = [y_ssm, y_gdn, y_nsa.reshape(t, -1), y_sgu]
        ws = [w[layer].astype(BF16) for w in (w_br_ssm, w_br_gdn, w_br_nsa, w_br_sgu)]
        xf = merge(xf, proj, ys, ws, w_out[layer].astype(BF16))
        i = layer // 2
        if layer % 2 == 0:
            xf = dense_ffn(xf, norm_ffn[layer], ffn_w_gate[i], ffn_w_up[i], ffn_w_down[i])
        else:
            xf = moe_layer(xf, norm_ffn[layer], moe_router[i], moe_w_gate[i], moe_w_up[i],
                           moe_w_down[i], norm_final)
    return xf.reshape(bs, s, d)
```

```python
import functools
import math

import jax
import jax.numpy as jnp
from jax import lax
import numpy as np
from jax.experimental import pallas as pl
from jax.experimental.pallas import tpu as pltpu

F32 = jnp.float32
BF16 = jnp.bfloat16

D_MODEL = 1024
DEPTH = 2
SSM_WIDTH = D_MODEL // 2
SSM_GROUP = 16
SSM_GROUPS = SSM_WIDTH // SSM_GROUP
SSM_STATE = 64
GDN_HEADS = 4
GDN_DK = 128
GDN_DV = 128
GDN_CONV = 4
GDN_CHUNK = 64
NSA_HEADS = 8
NSA_KV_GROUPS = 2
NSA_HPG = NSA_HEADS // NSA_KV_GROUPS
NSA_DH = 64
NSA_CMP_LEN = 32
NSA_CMP_STRIDE = 16
NSA_CMP_HIDDEN = 128
NSA_SEL_LEN = 64
NSA_SEL_TOPK = 16
NSA_WINDOW = 512
NSA_QBLOCK = 128
SGU_WIDTH = D_MODEL // 2
SGU_GROUPS = 4
SGU_CHUNK = 128
REL_BUCKETS = 32
REL_MAX_DIST = 128
FFN_DENSE = 2816
N_EXPERTS = 8
TOP_K = 2
FFN_EXPERT = 3584
N_BRANCH = 4
RMS_EPS = 1e-6
GDN_QKV = GDN_HEADS * (2 * GDN_DK + GDN_DV)
NSA_Q = NSA_HEADS * NSA_DH
NSA_KV = NSA_KV_GROUPS * NSA_DH

LANES = 128
VMEM_LIMIT = 48 * 1024 * 1024
MASKED = -1e30

OFF_QKV = 0
OFF_SSM = OFF_QKV + GDN_QKV
OFF_Z = OFF_SSM + SSM_WIDTH
OFF_NQ = OFF_Z + GDN_HEADS * GDN_DV
OFF_SGU = OFF_NQ + NSA_Q
OFF_GATE = OFF_SGU + 2 * SGU_WIDTH
OFF_KV = OFF_GATE + N_BRANCH * D_MODEL
OFF_SMALL = OFF_KV + 6 * NSA_KV
SMALL_USED = 2 * GDN_HEADS + 3 * NSA_HEADS
PROJ_COLS = 9216


def _params(*sem):
    return pltpu.CompilerParams(dimension_semantics=sem, vmem_limit_bytes=VMEM_LIMIT)


def _rms(x, g):
    return x * lax.rsqrt(jnp.mean(x * x, axis=-1, keepdims=True) + RMS_EPS) * g


def _dot_nt(a, b):
    return lax.dot_general(a, b, (((1,), (1,)), ((), ())), preferred_element_type=F32)


def _dot_tn(a, b):
    return lax.dot_general(a, b, (((0,), (0,)), ((), ())), preferred_element_type=F32)


def _split_bf16(x):
    hi = x.astype(BF16)
    return hi, (x - hi.astype(F32)).astype(BF16)


def _in_proj_body(x_ref, g_ref, w_ref, o_ref, h_ref):
    @pl.when(pl.program_id(1) == 0)
    def _():
        h_ref[...] = _rms(x_ref[...], g_ref[...]).astype(BF16)

    o_ref[...] = jnp.dot(h_ref[...], w_ref[...].astype(BF16),
                         preferred_element_type=F32).astype(o_ref.dtype)


def in_proj(x, g, w, tm=2048, tn=512):
    t, d = x.shape
    n = w.shape[1]
    return pl.pallas_call(
        _in_proj_body,
        grid=(t // tm, n // tn),
        in_specs=[pl.BlockSpec((tm, d), lambda i, j: (i, 0)),
                  pl.BlockSpec((1, d), lambda i, j: (0, 0)),
                  pl.BlockSpec((d, tn), lambda i, j: (0, j))],
        out_specs=pl.BlockSpec((tm, tn), lambda i, j: (i, j)),
        out_shape=jax.ShapeDtypeStruct((t, n), BF16),
        scratch_shapes=[pltpu.VMEM((tm, d), BF16)],
        compiler_params=_params("parallel", "arbitrary"),
        name="in_proj",
    )(x, g.reshape(1, d), w)


def _merge_body(x_ref, gate_ref, ya_ref, yb_ref, yc_ref, yd_ref,
                wa_ref, wb_ref, wc_ref, wd_ref, wo_ref, o_ref):
    def branch(k, y_ref, w_ref):
        p = jnp.dot(y_ref[...], w_ref[...], preferred_element_type=F32)
        return jax.nn.sigmoid(gate_ref[:, k * D_MODEL:(k + 1) * D_MODEL].astype(F32)) * p

    merged = (branch(0, ya_ref, wa_ref) + branch(1, yb_ref, wb_ref)
              + branch(2, yc_ref, wc_ref) + branch(3, yd_ref, wd_ref))
    o_ref[...] = x_ref[...] + jnp.dot(merged.astype(BF16), wo_ref[...],
                                      preferred_element_type=F32)


def merge(x, proj, ys, ws, w_out, tm=256):
    t, d = x.shape
    gate_blk = OFF_GATE // (N_BRANCH * D_MODEL)
    row = lambda i: (i, 0)
    fixed = lambda i: (0, 0)
    in_specs = [pl.BlockSpec((tm, d), row),
                pl.BlockSpec((tm, N_BRANCH * D_MODEL), lambda i: (i, gate_blk))]
    in_specs += [pl.BlockSpec((tm, y.shape[1]), row) for y in ys]
    in_specs += [pl.BlockSpec(w.shape, fixed) for w in ws]
    in_specs += [pl.BlockSpec(w_out.shape, fixed)]
    return pl.pallas_call(
        _merge_body,
        grid=(t // tm,),
        in_specs=in_specs,
        out_specs=pl.BlockSpec((tm, d), row),
        out_shape=jax.ShapeDtypeStruct((t, d), F32),
        compiler_params=_params("parallel"),
        name="merge",
    )(x, proj, *ys, *ws, w_out)


def _ffn_body(x_ref, g_ref, wg_ref, wu_ref, wd_ref, o_ref, h_ref, acc_ref):
    f = pl.program_id(1)

    @pl.when(f == 0)
    def _():
        h_ref[...] = _rms(x_ref[...], g_ref[...]).astype(BF16)
        acc_ref[...] = x_ref[...]

    h = h_ref[...]
    a = jnp.dot(h, wg_ref[...].astype(BF16), preferred_element_type=F32)
    u = jnp.dot(h, wu_ref[...].astype(BF16), preferred_element_type=F32)
    act = (a * jax.nn.sigmoid(a) * u).astype(BF16)
    acc_ref[...] += jnp.dot(act, wd_ref[...].astype(BF16), preferred_element_type=F32)

    @pl.when(f == pl.num_programs(1) - 1)
    def _():
        o_ref[...] = acc_ref[...]


def dense_ffn(x, g, wg, wu, wd, tm=1024, tf=256):
    t, d = x.shape
    nf = wg.shape[1]
    return pl.pallas_call(
        _ffn_body,
        grid=(t // tm, nf // tf),
        in_specs=[pl.BlockSpec((tm, d), lambda i, f: (i, 0)),
                  pl.BlockSpec((1, d), lambda i, f: (0, 0)),
                  pl.BlockSpec((d, tf), lambda i, f: (0, f)),
                  pl.BlockSpec((d, tf), lambda i, f: (0, f)),
                  pl.BlockSpec((tf, d), lambda i, f: (f, 0))],
        out_specs=pl.BlockSpec((tm, d), lambda i, f: (i, 0)),
        out_shape=jax.ShapeDtypeStruct((t, d), F32),
        scratch_shapes=[pltpu.VMEM((tm, d), BF16), pltpu.VMEM((tm, d), F32)],
        compiler_params=_params("parallel", "arbitrary"),
        name="dense_ffn",
    )(x, g.reshape(1, d), wg, wu, wd)


def _route_body(x_ref, g_ref, rhi_ref, rlo_ref, h_ref, idx_ref, wt_ref):
    h = _rms(x_ref[...], g_ref[...])
    hi = h.astype(BF16)
    lo = (h - hi.astype(F32)).astype(BF16)
    h_ref[...] = h
    logits = (jnp.dot(hi, rhi_ref[...], preferred_element_type=F32)
              + jnp.dot(hi, rlo_ref[...], preferred_element_type=F32)
              + jnp.dot(lo, rhi_ref[...], preferred_element_type=F32))
    col = lax.broadcasted_iota(jnp.int32, logits.shape, 1).astype(F32)
    neg = jnp.float32(-jnp.inf)
    lg = jnp.where(col < N_EXPERTS, logits, neg)
    m1 = jnp.max(lg, axis=-1, keepdims=True)
    i1 = jnp.min(jnp.where(lg == m1, col, float(LANES)), axis=-1, keepdims=True)
    lg2 = jnp.where(col == i1, neg, lg)
    m2 = jnp.max(lg2, axis=-1, keepdims=True)
    i2 = jnp.min(jnp.where(lg2 == m2, col, float(LANES)), axis=-1, keepdims=True)
    e = jnp.exp(m2 - m1)
    w1 = 1.0 / (1.0 + e)
    w2 = e / (1.0 + e)
    idx_ref[...] = jnp.where(col == 0, i1, jnp.where(col == 1, i2, 0.0)).astype(jnp.int32)
    wt_ref[...] = jnp.where(col == 0, w1, jnp.where(col == 1, w2, 0.0))


def moe_route(x, g, r_hi, r_lo, tm=512):
    t, d = x.shape
    row = lambda i: (i, 0)
    fixed = lambda i: (0, 0)
    return pl.pallas_call(
        _route_body,
        grid=(t // tm,),
        in_specs=[pl.BlockSpec((tm, d), row), pl.BlockSpec((1, d), fixed),
                  pl.BlockSpec((d, LANES), fixed), pl.BlockSpec((d, LANES), fixed)],
        out_specs=[pl.BlockSpec((tm, d), row), pl.BlockSpec((tm, LANES), row),
                   pl.BlockSpec((tm, LANES), row)],
        out_shape=[jax.ShapeDtypeStruct((t, d), F32),
                   jax.ShapeDtypeStruct((t, LANES), jnp.int32),
                   jax.ShapeDtypeStruct((t, LANES), F32)],
        compiler_params=_params("parallel"),
        name="moe_route",
    )(x, g.reshape(1, d), r_hi, r_lo)


def _experts_body(te_ref, nu_ref, src_ref, h_hbm, wg_ref, wu_ref, wd_ref, o_ref,
                  xbuf_ref, xs_ref, acc_ref, sem, *, rows_per_step):
    i = pl.program_id(0)
    f = pl.program_id(1)
    tm = xs_ref.shape[0]
    n_rows = xbuf_ref.shape[1]
    n_used = nu_ref[0]
    used = i < n_used
    has_next = i + 1 < n_used
    slot = i % 2

    def start_row(tile, slot_, r):
        tok = src_ref[tile * tm + r]
        pltpu.make_async_copy(h_hbm.at[pl.ds(tok, 1), :], xbuf_ref.at[slot_, pl.ds(r, 1), :],
                              sem.at[slot_]).start()

    @pl.when(f == 0)
    def _():
        acc_ref[...] = jnp.zeros_like(acc_ref)

        @pl.when(used)
        def _():
            @pl.when(i == 0)
            def _():
                def row(r, carry):
                    start_row(0, 0, r)
                    return carry

                lax.fori_loop(0, n_rows, row, 0, unroll=8)

            pltpu.make_async_copy(h_hbm.at[pl.ds(0, n_rows), :], xbuf_ref.at[slot], sem.at[slot]).wait()
            xs_ref[...] = xbuf_ref[slot, 0:tm, :].astype(BF16)

    def ffn_step():
        h = xs_ref[...]
        a = jnp.dot(h, wg_ref[0].astype(BF16), preferred_element_type=F32)
        u = jnp.dot(h, wu_ref[0].astype(BF16), preferred_element_type=F32)
        act = (a * jax.nn.sigmoid(a) * u).astype(BF16)
        acc_ref[...] += jnp.dot(act, wd_ref[0].astype(BF16), preferred_element_type=F32)

    @pl.when(used & has_next)
    def _():
        for j in range(rows_per_step):
            start_row(i + 1, 1 - slot, f * rows_per_step + j)
        ffn_step()

    @pl.when(used & jnp.logical_not(has_next))
    def _():
        ffn_step()

    @pl.when(f == pl.num_programs(1) - 1)
    def _():
        o_ref[...] = acc_ref[...].astype(o_ref.dtype)


def moe_experts(tile_expert, n_used, src, h, wg, wu, wd, tm, tf=512):
    d = h.shape[1]
    nf = wg.shape[2] // tf
    rows_per_step = 8 * (-(-tm // (8 * nf)))
    n_rows = rows_per_step * nf
    n_tiles = src.shape[0] // tm
    src = jnp.pad(src, (0, n_rows - tm))
    grid_spec = pltpu.PrefetchScalarGridSpec(
        num_scalar_prefetch=3,
        grid=(n_tiles, nf),
        in_specs=[pl.BlockSpec(memory_space=pl.ANY),
                  pl.BlockSpec((1, d, tf), lambda i, f, te, *_: (te[i], 0, f)),
                  pl.BlockSpec((1, d, tf), lambda i, f, te, *_: (te[i], 0, f)),
                  pl.BlockSpec((1, tf, d), lambda i, f, te, *_: (te[i], f, 0))],
        out_specs=pl.BlockSpec((tm, d), lambda i, f, *_: (i, 0)),
        scratch_shapes=[pltpu.VMEM((2, n_rows, d), F32), pltpu.VMEM((tm, d), BF16),
                        pltpu.VMEM((tm, d), F32), pltpu.SemaphoreType.DMA((2,))],
    )
    return pl.pallas_call(
        functools.partial(_experts_body, rows_per_step=rows_per_step),
        grid_spec=grid_spec,
        out_shape=jax.ShapeDtypeStruct((n_tiles * tm, d), BF16),
        compiler_params=_params("arbitrary", "arbitrary"),
        name="moe_experts",
    )(tile_expert, n_used, src, h, wg, wu, wd)


def _combine_norm_body(x_ref, ya_ref, yb_ref, wt_ref, g_ref, o_ref):
    wt = wt_ref[...]
    y = wt[:, 0:1] * ya_ref[...].astype(F32) + wt[:, 1:2] * yb_ref[...].astype(F32)
    o_ref[...] = _rms(x_ref[...] + y, g_ref[...])


def combine_norm(x, ya, yb, wts, g, tm=1024):
    t, d = x.shape
    row = lambda i: (i, 0)
    return pl.pallas_call(
        _combine_norm_body,
        grid=(t // tm,),
        in_specs=[pl.BlockSpec((tm, d), row)] * 3 + [pl.BlockSpec((tm, LANES), row),
                                                     pl.BlockSpec((1, d), lambda i: (0, 0))],
        out_specs=pl.BlockSpec((tm, d), row),
        out_shape=jax.ShapeDtypeStruct((t, d), F32),
        compiler_params=_params("parallel"),
        name="combine_norm",
    )(x, ya, yb, wts, g.reshape(1, d))


def moe_layer(x, g_norm, router, wg, wu, wd, g_final, tm=1024):
    t, d = x.shape
    r = jnp.zeros((d, LANES), F32).at[:, :N_EXPERTS].set(router)
    r_hi = r.astype(BF16)
    r_lo = (r - r_hi.astype(F32)).astype(BF16)
    h, idx, wts = moe_route(x, g_norm, r_hi, r_lo)
    e_flat = jnp.concatenate([idx[:, 0], idx[:, 1]])
    onehot = (e_flat[:, None] == jnp.arange(N_EXPERTS)[None, :]).astype(jnp.int32)
    csum = jnp.cumsum(onehot, axis=0)
    rank = jnp.sum((csum - onehot) * onehot, axis=1)
    counts = csum[-1]
    padded = ((counts + tm - 1) // tm) * tm
    ends = jnp.cumsum(padded)
    starts = ends - padded
    dest = starts[e_flat] + rank
    n_tiles = (TOP_K * t) // tm + N_EXPERTS
    p = n_tiles * tm
    tok = jnp.concatenate([jnp.arange(t, dtype=jnp.int32)] * TOP_K)
    src = jnp.zeros((p,), jnp.int32).at[dest].set(tok)
    tile_expert = jnp.minimum(
        jnp.searchsorted(ends, jnp.arange(n_tiles, dtype=jnp.int32) * tm, side="right"),
        N_EXPERTS - 1).astype(jnp.int32)
    n_used = (ends[-1] // tm).astype(jnp.int32).reshape(1)
    ys = moe_experts(tile_expert, n_used, src, h, wg, wu, wd, tm)
    ya = jnp.take(ys, dest[:t], axis=0, mode="clip")
    yb = jnp.take(ys, dest[t:], axis=0, mode="clip")
    return combine_norm(x, ya, yb, wts, g_final)


SSM_TC = 512
SSM_SUB = 128
SSM_HALF = 256
SSM_NSTATE = SSM_GROUPS * SSM_STATE


def _ssm_body(u_ref, bw_ref, cw_ref, d_ref, glu_ref, ar_ref, ai_ref, pr_ref, pi_ref, o_ref,
              xr_ref, xi_ref, cr_ref, ci_ref):
    @pl.when(pl.program_id(1) == 0)
    def _():
        cr_ref[...] = jnp.zeros_like(cr_ref)
        ci_ref[...] = jnp.zeros_like(ci_ref)

    ub = u_ref[...]
    u = ub.astype(F32)
    nblk = SSM_WIDTH // SSM_HALF
    sblk = SSM_NSTATE // nblk
    for k in range(nblk):
        bu = jnp.dot(ub[:, k * SSM_HALF:(k + 1) * SSM_HALF], bw_ref[k], preferred_element_type=F32)
        xr_ref[:, k * sblk:(k + 1) * sblk] = bu[:, :sblk]
        xi_ref[:, k * sblk:(k + 1) * sblk] = bu[:, sblk:]

    row8 = lax.broadcasted_iota(jnp.int32, (SSM_SUB, LANES), 0) % 8
    n_grp = SSM_SUB // 8

    def strip(c, carry):
        col = pl.ds(pl.multiple_of(c * LANES, LANES), LANES)
        c_r = cr_ref[:, col]
        c_i = ci_ref[:, col]
        p_r = pr_ref[0:8, col]
        p_i = pi_ref[0:8, col]
        step_mul = [(jnp.where(row8 >= (1 << i), ar_ref[i:i + 1, col], 0.0),
                     jnp.where(row8 >= (1 << i), ai_ref[i:i + 1, col], 0.0)) for i in range(3)]
        for sub in range(SSM_TC // SSM_SUB):
            rows = slice(sub * SSM_SUB, (sub + 1) * SSM_SUB)
            xr = xr_ref[rows, col]
            xi = xi_ref[rows, col]
            for i in range(3):
                a_r, a_i = step_mul[i]
                sr, si = pltpu.roll(xr, 1 << i, 0), pltpu.roll(xi, 1 << i, 0)
                xr, xi = xr + a_r * sr - a_i * si, xi + a_r * si + a_i * sr
            out_r, out_i = [], []
            for r in range(n_grp):
                g_r = xr[8 * r:8 * r + 8] + p_r * c_r - p_i * c_i
                g_i = xi[8 * r:8 * r + 8] + p_r * c_i + p_i * c_r
                c_r, c_i = g_r[7:8, :], g_i[7:8, :]
                out_r.append(g_r)
                out_i.append(g_i)
            xr_ref[rows, col] = jnp.concatenate(out_r, axis=0)
            xi_ref[rows, col] = jnp.concatenate(out_i, axis=0)
        cr_ref[:, col] = c_r
        ci_ref[:, col] = c_i
        return carry

    lax.fori_loop(0, SSM_NSTATE // LANES, strip, 0)

    ys = []
    for k in range(nblk):
        st = jnp.concatenate([xr_ref[:, k * sblk:(k + 1) * sblk].astype(BF16),
                              xi_ref[:, k * sblk:(k + 1) * sblk].astype(BF16)], axis=1)
        ys.append(jnp.dot(st, cw_ref[k], preferred_element_type=F32))
    y = jax.nn.gelu(jnp.concatenate(ys, axis=1) + d_ref[...] * u)
    gated = y * jax.nn.sigmoid(jnp.dot(y.astype(BF16), glu_ref[...], preferred_element_type=F32))
    o_ref[...] = gated.astype(o_ref.dtype)


def ssm_mixer(proj, bs, a_re, a_im, log_dt, b_re, b_im, c_re, c_im, d_skip, glu_w):
    t = proj.shape[0]
    s = t // bs
    nt = s // SSM_TC
    assert s % SSM_TC == 0 and OFF_SSM % SSM_WIDTH == 0
    nblk = SSM_WIDTH // SSM_HALF
    gpb = SSM_GROUPS // nblk
    dt = jnp.exp(log_dt)[:, None]
    mag = jnp.exp(a_re * dt)
    abar_r, abar_i = mag * jnp.cos(a_im * dt), mag * jnp.sin(a_im * dt)
    nr, ni = abar_r - 1.0, abar_i
    den = a_re * a_re + a_im * a_im
    sr, si = (nr * a_re + ni * a_im) / den, (ni * a_re - nr * a_im) / den
    bbar_r = sr[..., None] * b_re - si[..., None] * b_im
    bbar_i = sr[..., None] * b_im + si[..., None] * b_re

    def powers(k):
        kk = k.astype(F32)[:, None, None]
        m = jnp.exp(kk * (a_re * dt))
        return ((m * jnp.cos(kk * (a_im * dt))).reshape(-1, SSM_NSTATE),
                (m * jnp.sin(kk * (a_im * dt))).reshape(-1, SSM_NSTATE))

    ar, ai = powers(2 ** jnp.arange(8))
    pr, pi = powers(jnp.arange(1, SSM_SUB + 1))
    eye = jnp.eye(gpb, dtype=F32)

    def in_block(w):
        return jnp.einsum('gnp,gh->gphn', w, eye).reshape(gpb * SSM_GROUP, gpb * SSM_STATE)

    def out_block(w):
        return jnp.einsum('gpn,gh->gnhp', w, eye).reshape(gpb * SSM_STATE, gpb * SSM_GROUP)

    bw = jnp.stack([jnp.concatenate([in_block(bbar_r[k * gpb:(k + 1) * gpb]),
                                     in_block(bbar_i[k * gpb:(k + 1) * gpb])], axis=1)
                    for k in range(nblk)]).astype(BF16)
    cw = jnp.stack([jnp.concatenate([out_block(c_re[k * gpb:(k + 1) * gpb]),
                                     -out_block(c_im[k * gpb:(k + 1) * gpb])], axis=0)
                    for k in range(nblk)]).astype(BF16)
    fixed2 = lambda b, i: (0, 0)
    fixed3 = lambda b, i: (0, 0, 0)
    return pl.pallas_call(
        _ssm_body,
        grid=(bs, nt),
        in_specs=[pl.BlockSpec((SSM_TC, SSM_WIDTH), lambda b, i: (b * nt + i, OFF_SSM // SSM_WIDTH)),
                  pl.BlockSpec(bw.shape, fixed3), pl.BlockSpec(cw.shape, fixed3),
                  pl.BlockSpec((1, SSM_WIDTH), fixed2), pl.BlockSpec((SSM_WIDTH, SSM_WIDTH), fixed2),
                  pl.BlockSpec((8, SSM_NSTATE), fixed2), pl.BlockSpec((8, SSM_NSTATE), fixed2),
                  pl.BlockSpec((SSM_SUB, SSM_NSTATE), fixed2), pl.BlockSpec((SSM_SUB, SSM_NSTATE), fixed2)],
        out_specs=pl.BlockSpec((SSM_TC, SSM_WIDTH), lambda b, i: (b * nt + i, 0)),
        out_shape=jax.ShapeDtypeStruct((t, SSM_WIDTH), BF16),
        scratch_shapes=[pltpu.VMEM((SSM_TC, SSM_NSTATE), F32), pltpu.VMEM((SSM_TC, SSM_NSTATE), F32),
                        pltpu.VMEM((1, SSM_NSTATE), F32), pltpu.VMEM((1, SSM_NSTATE), F32)],
        compiler_params=_params("parallel", "arbitrary"),
        name="ssm_scan",
    )(proj, bw, cw, d_skip.reshape(1, SSM_WIDTH), glu_w.astype(BF16), ar, ai, pr, pi)


GDN_TC = 256
GDN_HALO = 8


def _gdn_body(nega_ref, dtb_ref, q_ref, k_ref, v_ref, z_ref, sm_ref, wq_ref, wk_ref, wv_ref, ng_ref,
              o_ref, state_ref, hq_ref, hk_ref, hv_ref, vnew_ref):
    TC, CH, DK, H = GDN_TC, GDN_CHUNK, GDN_DK, GDN_HEADS
    heads = range(H)

    @pl.when(pl.program_id(1) == 0)
    def _():
        state_ref[...] = jnp.zeros_like(state_ref)
        hq_ref[...] = jnp.zeros_like(hq_ref)
        hk_ref[...] = jnp.zeros_like(hk_ref)
        hv_ref[...] = jnp.zeros_like(hv_ref)

    def conv_silu(x_ref, halo_ref, w_ref):
        x = x_ref[...].astype(F32)
        xx = jnp.concatenate([halo_ref[...], x], axis=0)
        w = w_ref[...]
        y = w[GDN_CONV - 1:GDN_CONV] * x
        for d in range(1, GDN_CONV):
            y = y + w[GDN_CONV - 1 - d:GDN_CONV - d] * pltpu.roll(xx, d, 0)[GDN_HALO:GDN_HALO + TC]
        halo_ref[...] = x[TC - GDN_HALO:TC]
        return y * jax.nn.sigmoid(y)

    def per_head(x):
        return [x[:, h * LANES:(h + 1) * LANES] for h in heads]

    q = per_head(conv_silu(q_ref, hq_ref, wq_ref))
    k = per_head(conv_silu(k_ref, hk_ref, wk_ref))
    v = per_head(conv_silu(v_ref, hv_ref, wv_ref))
    q = [x * lax.rsqrt(jnp.sum(x * x, axis=-1, keepdims=True) + 1e-6) * (DK ** -0.5) for x in q]
    k = [x * lax.rsqrt(jnp.sum(x * x, axis=-1, keepdims=True) + 1e-6) for x in k]

    small = sm_ref[...].astype(F32)
    beta = [jax.nn.sigmoid(small[:, H + h:H + h + 1]) for h in heads]
    la = []
    for h in heads:
        xa = small[:, h:h + 1] + dtb_ref[h]
        softplus = jnp.maximum(xa, 0.0) + jnp.log(1.0 + jnp.exp(-jnp.abs(xa)))
        la.append(jnp.broadcast_to(nega_ref[h] * softplus, (TC, LANES)))

    ri = lax.broadcasted_iota(jnp.int32, (TC, TC), 0)
    ci = lax.broadcasted_iota(jnp.int32, (TC, TC), 1)
    same = (ri // CH) == (ci // CH)
    causal = jnp.where(same, jnp.where(ci <= ri, 1.0, 0.0), 0.0)
    strict = jnp.where(ci < ri, causal, 0.0)
    eye = jnp.where(ri == ci, 1.0, 0.0)
    tri = causal.astype(BF16)
    lane = lax.broadcasted_iota(jnp.int32, (TC, LANES), 1)

    g = []
    for h in heads:
        la_hi, la_lo = _split_bf16(la[h])
        g.append(jnp.dot(tri, la_hi, preferred_element_type=F32)
                 + jnp.dot(tri, la_lo, preferred_element_type=F32))
    decay = []
    for h in heads:
        g_hi = g[h].astype(BF16).astype(F32)
        g_lo = g[h] - g_hi
        lhs = jnp.where(lane == 0, g_hi, jnp.where(lane == 1, g_lo, jnp.where(lane < 4, 1.0, 0.0)))
        rhs = jnp.where(lane < 2, 1.0, jnp.where(lane == 2, -g_hi, jnp.where(lane == 3, -g_lo, 0.0)))
        decay.append(jnp.exp(jnp.where(causal > 0.5, _dot_nt(lhs.astype(BF16), rhs.astype(BF16)), MASKED)))

    kb = [k[h] * beta[h] for h in heads]
    k_b = [x.astype(BF16) for x in k]
    lmat = [strict * _dot_nt(kb[h].astype(BF16), k_b[h]) * decay[h] for h in heads]
    tinv = [eye - x for x in lmat]
    pw = lmat
    for _ in range(CH.bit_length() - 2):
        pw = [jnp.dot(x.astype(BF16), x.astype(BF16), preferred_element_type=F32) for x in pw]
        tinv = [tinv[h] + jnp.dot(tinv[h].astype(BF16), pw[h].astype(BF16), preferred_element_type=F32)
                for h in heads]
    eg = [jnp.exp(x) for x in g]
    sol = [jnp.dot(tinv[h].astype(BF16),
                   jnp.concatenate([v[h] * beta[h], kb[h] * eg[h]], axis=1).astype(BF16),
                   preferred_element_type=F32) for h in heads]
    attn = [(causal * _dot_nt(q[h].astype(BF16), k_b[h]) * decay[h]).astype(BF16) for h in heads]
    q_st = [(q[h] * eg[h]).astype(BF16) for h in heads]

    vnew_ref[...] = jnp.zeros_like(vnew_ref)
    for c in range(TC // CH):
        rows = slice(c * CH, (c + 1) * CH)
        g_last = [x[(c + 1) * CH - 1:(c + 1) * CH, :] for x in g]
        st = [state_ref[h] for h in heads]
        st_b = [x.astype(BF16) for x in st]
        v_new = [sol[h][rows, :GDN_DV]
                 - jnp.dot(sol[h][rows, GDN_DV:].astype(BF16), st_b[h], preferred_element_type=F32)
                 for h in heads]
        for h in heads:
            vnew_ref[h, rows, :] = v_new[h].astype(BF16)
        o_c = [jnp.dot(q_st[h][rows], st_b[h], preferred_element_type=F32)
               + jnp.dot(attn[h][rows], vnew_ref[h], preferred_element_type=F32) for h in heads]
        for h in heads:
            k_st = (k[h][rows] * jnp.exp(g_last[h] - g[h][rows])).astype(BF16)
            state_ref[h] = st[h] * jnp.exp(g_last[h][:, :1]) + _dot_tn(k_st, v_new[h].astype(BF16))
        for h in heads:
            o = o_c[h] * lax.rsqrt(jnp.mean(o_c[h] * o_c[h], axis=-1, keepdims=True) + RMS_EPS) * ng_ref[...]
            zc = z_ref[rows, h * LANES:(h + 1) * LANES].astype(F32)
            o_ref[rows, h * LANES:(h + 1) * LANES] = (o * (zc * jax.nn.sigmoid(zc))).astype(o_ref.dtype)


def gdn_mixer(proj, bs, conv_w, a_log, dt_bias, norm_g):
    t = proj.shape[0]
    s = t // bs
    nt = s // GDN_TC
    H = GDN_HEADS
    hw = H * LANES
    assert s % GDN_TC == 0 and GDN_DK == LANES and GDN_DV == LANES and OFF_QKV == 0 and OFF_Z % hw == 0
    cw = jnp.zeros((8, GDN_QKV), F32).at[:GDN_CONV].set(conv_w)
    tok = lambda blk: pl.BlockSpec((GDN_TC, hw), lambda b, i: (b * nt + i, blk))
    wspec = lambda blk: pl.BlockSpec((8, hw), lambda b, i: (0, blk))
    smem = pl.BlockSpec(memory_space=pltpu.SMEM)
    return pl.pallas_call(
        _gdn_body,
        grid=(bs, nt),
        in_specs=[smem, smem, tok(0), tok(1), tok(2), tok(OFF_Z // hw),
                  pl.BlockSpec((GDN_TC, 2 * LANES), lambda b, i: (b * nt + i, OFF_SMALL // (2 * LANES))),
                  wspec(0), wspec(1), wspec(2), pl.BlockSpec((1, GDN_DV), lambda b, i: (0, 0))],
        out_specs=pl.BlockSpec((GDN_TC, hw), lambda b, i: (b * nt + i, 0)),
        out_shape=jax.ShapeDtypeStruct((t, hw), BF16),
        scratch_shapes=[pltpu.VMEM((H, GDN_DK, GDN_DV), F32)] + [pltpu.VMEM((GDN_HALO, hw), F32)] * 3
        + [pltpu.VMEM((H, GDN_TC, GDN_DV), BF16)],
        compiler_params=_params("parallel", "arbitrary"),
        name="gdn_delta",
    )(-jnp.exp(a_log), dt_bias.astype(F32), proj, proj, proj, proj, proj, cw, cw, cw,
      norm_g.reshape(1, GDN_DV))


NSA_KT = 256
NSA_NEG = MASKED
NSA_WTILES = NSA_WINDOW // LANES + 1


def _bucket_table():
    n = np.arange(LANES)
    max_exact = REL_BUCKETS // 2
    nf = np.maximum(n, 1).astype(np.float32)
    large = max_exact + (np.log(nf / np.float32(max_exact)) / np.float32(math.log(REL_MAX_DIST / max_exact))
                         * np.float32(REL_BUCKETS - max_exact)).astype(np.int32)
    tbl = np.where(n < max_exact, n, np.minimum(large, REL_BUCKETS - 1))
    assert tbl[-1] == REL_BUCKETS - 1 and REL_MAX_DIST <= LANES
    return tbl


def _compress_body(x_ref, pe_ref, w1_ref, w2_ref, o_ref):
    x = x_ref[0, 0, 0]
    w1 = w1_ref[0]
    half = NSA_CMP_STRIDE * NSA_DH
    nc = x.shape[0]
    a = jnp.dot(x, w1[:half], preferred_element_type=F32)
    b = jnp.dot(x, w1[half:], preferred_element_type=F32)
    pe_h = jnp.dot(pe_ref[0], w1, preferred_element_type=F32)
    hid = a + pltpu.roll(b, nc - 1, 0) + pe_h[0:1, :]
    o_ref[0, 0, 0] = jnp.dot(jax.nn.gelu(hid).astype(BF16), w2_ref[0], preferred_element_type=F32)


def nsa_compress(x, pe, w1, w2):
    _, bs, g, nc, width = x.shape
    return pl.pallas_call(
        _compress_body,
        grid=(2, bs, g),
        in_specs=[pl.BlockSpec((1, 1, 1, nc, width), lambda i, b, j: (i, b, j, 0, 0)),
                  pl.BlockSpec((1,) + pe.shape[1:], lambda i, b, j: (i, 0, 0)),
                  pl.BlockSpec((1,) + w1.shape[1:], lambda i, b, j: (i, 0, 0)),
                  pl.BlockSpec((1,) + w2.shape[1:], lambda i, b, j: (i, 0, 0))],
        out_specs=pl.BlockSpec((1, 1, 1, nc, NSA_DH), lambda i, b, j: (i, b, j, 0, 0)),
        out_shape=jax.ShapeDtypeStruct((2, bs, g, nc, NSA_DH), F32),
        compiler_params=_params("parallel", "parallel", "parallel"),
        name="nsa_compress",
    )(x, pe, w1, w2)


def _nsa_body(nfast, std, near_w, near_c, q_ref, gl_ref, qp_ref, kp_ref, cp_ref,
              ks_ref, vs_ref, kw_ref, vw_ref, kc_ref, vc_ref, agg_ref, td_ref, o_ref,
              m_ref, l_ref, acc_ref, *, n_sel, sel_k):
    QB, HG, G = NSA_QBLOCK, NSA_HPG, NSA_KV_GROUPS
    groups = range(G)
    qi = pl.program_id(1)
    s0 = qi * QB
    nc = kc_ref.shape[2]
    n_ct = nc // LANES
    q = [q_ref[0, g, 0] for g in groups]
    qp = qp_ref[...]

    def lanes4(x):
        return jnp.concatenate([x] * HG, axis=1)

    def delta_t(kp):
        idx = jnp.clip(_dot_nt(kp, qp), 0.0, float(LANES - 1)).astype(jnp.int32)
        outs = []
        for g in groups:
            heads = []
            for hg in range(HG):
                tb = jnp.broadcast_to(td_ref[g * HG + hg:g * HG + hg + 1, :], idx.shape)
                heads.append(jnp.take_along_axis(tb, idx, axis=1))
            outs.append(jnp.concatenate(heads, axis=1))
        return outs

    def maybe_delta(flag, kp):
        zeros = jnp.zeros((kp.shape[0], HG * QB), F32)
        return lax.cond(flag != 0, lambda: tuple(delta_t(kp)), lambda: (zeros,) * G)

    gate = [jax.nn.sigmoid(gl_ref[0, g, 0]) for g in groups]

    sc = [_dot_nt(kc_ref[0, g], q[g]) for g in groups]
    dl = [maybe_delta(near_c[qi * n_ct + ct], cp_ref[ct * LANES:(ct + 1) * LANES, :]) for ct in range(n_ct)]
    c_id = lax.broadcasted_iota(jnp.int32, (nc, QB), 0)
    c_q = lax.broadcasted_iota(jnp.int32, (nc, QB), 1)
    ok_c = lanes4(jnp.where(c_id * NSA_CMP_STRIDE + (NSA_CMP_LEN - 1) <= s0 + c_q, 1.0, 0.0))
    blk = lax.broadcasted_iota(jnp.int32, (LANES, QB), 0)
    t_tok = s0 + lax.broadcasted_iota(jnp.int32, (LANES, QB), 1)
    tb_blk = t_tok // NSA_SEL_LEN
    forced = jnp.where(blk == 0, 1.0, 0.0) + jnp.where(blk == tb_blk, 1.0, 0.0) \
        + jnp.where(blk == tb_blk - 1, 1.0, 0.0)
    blkf = blk.astype(F32)
    out, score = [], []
    for g in groups:
        s = sc[g] + jnp.concatenate([d[g] for d in dl], axis=0)
        s = jnp.where(ok_c > 0.5, s, NSA_NEG)
        e = jnp.exp2(s - jnp.max(s, axis=0, keepdims=True)) * ok_c
        p = e * (1.0 / jnp.maximum(jnp.sum(e, axis=0, keepdims=True), 1e-30))
        out.append(gate[g][0:1, :] * jnp.dot(vc_ref[0, g], p.astype(BF16), preferred_element_type=F32))
        ps = p[:, 0:QB]
        for hg in range(1, HG):
            ps = ps + p[:, hg * QB:(hg + 1) * QB]
        ps_hi, ps_lo = _split_bf16(ps)
        imp = (jnp.dot(agg_ref[...], ps_hi, preferred_element_type=F32)
               + jnp.dot(agg_ref[...], ps_lo, preferred_element_type=F32))
        sco = jnp.where(forced > 0.5, 1e9, jnp.where(blk * NSA_SEL_LEN <= t_tok, imp, -1e9))
        score.append(jnp.where(blk < n_sel, sco, -jnp.inf))
    sel = [jnp.zeros((LANES, QB), F32) for _ in groups]
    for _ in range(sel_k):
        for g in groups:
            mx = jnp.max(score[g], axis=0, keepdims=True)
            first = jnp.min(jnp.where(score[g] == mx, blkf, float(LANES)), axis=0, keepdims=True)
            hit = blkf == first
            sel[g] = jnp.where(hit, 1.0, sel[g])
            score[g] = jnp.where(hit, -jnp.inf, score[g])
    q2 = []
    for g in groups:
        sel_q = jnp.transpose(jnp.where(sel[g] > 0.5, 0.0, NSA_NEG))
        q2.append(jnp.concatenate([q[g], jnp.concatenate([sel_q] * HG, axis=0).astype(BF16)], axis=1))

    SEL, WIN = 0, 1

    def reset():
        m_ref[...] = jnp.full_like(m_ref, NSA_NEG)
        l_ref[...] = jnp.zeros_like(l_ref)
        acc_ref[...] = jnp.zeros_like(acc_ref)

    def tile_step(br, g, k, s, v_t):
        m_old = m_ref[br, g, k]
        m_new = jnp.maximum(m_old, jnp.max(s, axis=0, keepdims=True))
        alpha = jnp.exp2(m_old - m_new)
        p = jnp.exp2(s - m_new)
        l_ref[br, g, k] = l_ref[br, g, k] * alpha + jnp.sum(p, axis=0, keepdims=True)
        acc_ref[br, g, k] = (acc_ref[br, g, k] * alpha
                             + jnp.dot(v_t, p.astype(BF16), preferred_element_type=F32))
        m_ref[br, g, k] = m_new

    def finish(br, g):
        m = jnp.maximum(m_ref[br, g, 0], m_ref[br, g, 1])
        w0, w1 = jnp.exp2(m_ref[br, g, 0] - m), jnp.exp2(m_ref[br, g, 1] - m)
        return ((acc_ref[br, g, 0] * w0 + acc_ref[br, g, 1] * w1)
                * (1.0 / (l_ref[br, g, 0] * w0 + l_ref[br, g, 1] * w1)))

    def emit():
        heads_out = []
        for g in groups:
            o_t = out[g] + gate[g][1:2, :] * finish(SEL, g) + gate[g][2:3, :] * finish(WIN, g)
            heads_out += [jnp.transpose(o_t[:, hg * QB:(hg + 1) * QB]) for hg in range(HG)]
        o_ref[0] = jnp.concatenate(heads_out, axis=1).astype(o_ref.dtype)

    key_r = lax.broadcasted_iota(jnp.int32, (NSA_KT, QB), 0)
    key_q = lax.broadcasted_iota(jnp.int32, (NSA_KT, QB), 1)
    w_r = lax.broadcasted_iota(jnp.int32, (LANES, QB), 0)
    w_q = lax.broadcasted_iota(jnp.int32, (LANES, QB), 1)
    in_window = lanes4(jnp.where(w_r > w_q, 0.0, NSA_NEG))
    causal_w = lanes4(jnp.where(w_r <= w_q, 0.0, NSA_NEG))

    def sel_at(k0, n):
        k0 = pl.multiple_of(k0, LANES)
        return [(_dot_nt(ks_ref[0, g, pl.ds(k0, n), :], q2[g]), vs_ref[0, g, :, pl.ds(k0, n)])
                for g in groups]

    def win_at(k0, n):
        k0 = pl.multiple_of(k0, LANES)
        return [(_dot_nt(kw_ref[0, g, pl.ds(k0, n), :], q[g]), vw_ref[0, g, :, pl.ds(k0, n)])
                for g in groups]

    def delta_at(k0, n):
        return delta_t(kp_ref[pl.ds(pl.multiple_of(k0, LANES), n), :])

    def fast_body(i, carry):
        a = sel_at(2 * i * NSA_KT, NSA_KT)
        b = sel_at((2 * i + 1) * NSA_KT, NSA_KT)
        for g in groups:
            tile_step(SEL, g, 0, *a[g])
        for g in groups:
            tile_step(SEL, g, 1, *b[g])
        return carry

    @pl.when(std[qi] != 0)
    def _():
        reset()
        nb = (qi - 1) // 2
        lax.fori_loop(0, nb // 2, fast_body, 0)

        @pl.when(nb % 2 == 1)
        def _():
            a = sel_at((nb - 1) * NSA_KT, NSA_KT)
            for g in groups:
                tile_step(SEL, g, 0, *a[g])

        @pl.when(qi % 2 == 0)
        def _():
            a = sel_at(nb * NSA_KT, LANES)
            for g in groups:
                tile_step(SEL, g, 1, *a[g])

        half = NSA_WINDOW // 2
        zeros = jnp.zeros((LANES, HG * QB), F32)
        sel_a, sel_b = sel_at(s0 - LANES, LANES), sel_at(s0, LANES)
        win_a, win_b, win_d = win_at(s0 - NSA_WINDOW, half), win_at(s0 - half, half), win_at(s0, LANES)
        d_prev, d_diag = delta_at(s0 - LANES, LANES), delta_at(s0, LANES)
        for g in groups:
            tile_step(SEL, g, 0, sel_a[g][0] + d_prev[g], sel_a[g][1])
        for g in groups:
            tile_step(WIN, g, 0, win_a[g][0] + jnp.concatenate([in_window, zeros], axis=0), win_a[g][1])
        for g in groups:
            tile_step(SEL, g, 1, sel_b[g][0] + d_diag[g] + causal_w, sel_b[g][1])
        for g in groups:
            tile_step(WIN, g, 1, win_b[g][0] + jnp.concatenate([zeros, d_prev[g]], axis=0), win_b[g][1])
        for g in groups:
            tile_step(WIN, g, 0, win_d[g][0] + d_diag[g] + causal_w, win_d[g][1])
        emit()

    @pl.when(std[qi] == 0)
    def _():
        reset()
        n_past = (s0 + QB - 1) // NSA_KT
        n_pairs = nfast[qi] // 2
        lax.fori_loop(0, n_pairs, fast_body, 0)

        def slow_body(kt, carry):
            a = sel_at(kt * NSA_KT, NSA_KT)
            d = delta_at(kt * NSA_KT, NSA_KT)
            for g in groups:
                tile_step(SEL, g, 1, a[g][0] + d[g], a[g][1])
            return carry

        lax.fori_loop(2 * n_pairs, n_past, slow_body, 0)
        k0 = n_past * NSA_KT
        a = sel_at(k0, NSA_KT)
        d = delta_at(k0, NSA_KT)
        causal = lanes4(jnp.where(k0 + key_r <= s0 + key_q, 0.0, NSA_NEG))
        for g in groups:
            tile_step(SEL, g, 0, a[g][0] + d[g] + causal, a[g][1])

        for j in range(NSA_WTILES):
            start = s0 - NSA_WINDOW + j * LANES

            @pl.when(start >= 0)
            def _():
                sw = win_at(start, LANES)
                d = maybe_delta(near_w[qi * NSA_WTILES + j],
                                kp_ref[pl.ds(pl.multiple_of(start, LANES), LANES), :])
                for g in groups:
                    s = sw[g][0] + d[g]
                    if j == 0:
                        s = s + in_window
                    elif j == NSA_WTILES - 1:
                        s = s + causal_w
                    tile_step(WIN, g, j % 2, s, sw[g][1])

        emit()


def _pos_digits(p, key_side):
    d0, d1, d2 = (p & 127).astype(F32), ((p >> 7) & 127).astype(F32), (p >> 14).astype(F32)
    one = jnp.ones_like(d0)
    if key_side:
        cols = [one, one, one, -d2, -d1, -d0]
    else:
        cols = [16384.0 * d2, 128.0 * d1, d0, 16384.0 * one, 128.0 * one, one]
    out = jnp.stack(cols, axis=-1)
    return jnp.pad(out, ((0, 0), (0, LANES - out.shape[-1]))).astype(BF16)


def nsa_mixer(q, k_c, v_c, k_s, v_s, k_w, v_w, gate_logits, positions,
              ck_pe, ck_w1, ck_w2, cv_pe, cv_w1, cv_w2, rel_bias):
    q, k_c, v_c, k_s, v_s, k_w, v_w = lax.optimization_barrier((q, k_c, v_c, k_s, v_s, k_w, v_w))
    bs, s, _ = q.shape
    G, HG, DH, QB = NSA_KV_GROUPS, NSA_HPG, NSA_DH, NSA_QBLOCK
    nqt = s // QB
    nc = s // NSA_CMP_STRIDE
    n_cmp = (s - NSA_CMP_LEN) // NSA_CMP_STRIDE + 1
    n_sel = s // NSA_SEL_LEN
    sel_k = min(NSA_SEL_TOPK, n_sel)
    n_kt = s // NSA_KT
    n_ct = nc // LANES
    assert s % (LANES * NSA_CMP_STRIDE) == 0 and n_sel <= LANES and NSA_KT == 2 * QB

    def by_group(t):
        return t.reshape(bs, s, G, DH).transpose(0, 2, 1, 3).astype(F32)

    def by_group_t(t):
        return t.reshape(bs, s, G, DH).transpose(0, 2, 3, 1).astype(BF16)

    ones2 = jnp.zeros((LANES - DH,), F32).at[:2].set(1.0)

    def keys_aug(t, blocks=False):
        parts = [t, jnp.broadcast_to(ones2, t.shape[:-1] + (LANES - DH,))]
        if blocks:
            onehot = (jnp.arange(s)[:, None] // NSA_SEL_LEN == jnp.arange(LANES)[None, :]).astype(F32)
            parts.append(jnp.broadcast_to(onehot, t.shape[:-2] + (s, LANES)))
        return jnp.concatenate(parts, axis=-1).astype(BF16)

    chunks = jnp.stack([by_group(k_c), by_group(v_c)]).reshape(2, bs, G, nc, NSA_CMP_STRIDE * DH)
    pe = jnp.stack([ck_pe, cv_pe]).reshape(2, 1, NSA_CMP_LEN * DH)
    cmp = nsa_compress(chunks.astype(BF16), jnp.broadcast_to(pe, (2, 8, NSA_CMP_LEN * DH)).astype(BF16),
                       jnp.stack([ck_w1, cv_w1]).astype(BF16), jnp.stack([ck_w2, cv_w2]).astype(BF16))
    kc, vc_t = keys_aug(cmp[0]), cmp[1].transpose(0, 1, 3, 2).astype(BF16)

    log2e = math.log2(math.e)
    rel_bias = rel_bias.astype(F32) * log2e
    far = rel_bias[REL_BUCKETS - 1]
    far_hi = far.astype(BF16).astype(F32)
    extra = jnp.zeros((NSA_HEADS, LANES - DH), F32).at[:, 0].set(far_hi).at[:, 1].set(far - far_hi)
    qh = (q.astype(F32) * (DH ** -0.5 * log2e)).reshape(bs, nqt, QB, G, HG, DH).transpose(0, 3, 1, 4, 2, 5)
    ex = jnp.broadcast_to(extra.reshape(1, G, 1, HG, 1, LANES - DH), (bs, G, nqt, HG, QB, LANES - DH))
    qs = jnp.concatenate([qh, ex], axis=-1).astype(BF16).reshape(bs, G, nqt, HG * QB, LANES)
    gl = gate_logits.astype(F32).reshape(bs, nqt, QB, G, HG, 3).transpose(0, 3, 1, 5, 4, 2)
    gl = jnp.pad(gl.reshape(bs, G, nqt, 3, HG * QB), ((0, 0),) * 3 + ((0, 5), (0, 0)))

    td = (rel_bias[_bucket_table()] - rel_bias[REL_BUCKETS - 1][None, :]).T.astype(F32)
    cmp_end = jnp.minimum(jnp.arange(nc) * NSA_CMP_STRIDE + NSA_CMP_LEN - 1, s - 1)
    cpos = positions[cmp_end]
    qmin = positions.reshape(nqt, QB).min(axis=1)
    kmax = positions.reshape(nqt, QB).max(axis=1)
    near_s = (qmin[:, None] - positions.reshape(n_kt, NSA_KT).max(axis=1)[None, :]) < REL_MAX_DIST
    n_past = (jnp.arange(nqt) * QB + QB - 1) // NSA_KT
    n_fast = jnp.minimum(jnp.argmax(jnp.concatenate([near_s, jnp.ones((nqt, 1), bool)], axis=1), axis=1),
                         n_past)
    wt = jnp.clip(jnp.arange(nqt)[:, None] - (NSA_WTILES - 1) + jnp.arange(NSA_WTILES)[None, :], 0, nqt - 1)
    near_w = (qmin[:, None] - kmax[wt]) < REL_MAX_DIST
    near_c = (qmin[:, None] - cpos.reshape(n_ct, LANES).max(axis=1)[None, :]) < REL_MAX_DIST
    before = lax.cummax(kmax)[jnp.maximum(jnp.arange(nqt) - 2, 0)]
    std = (jnp.arange(nqt) >= NSA_WTILES - 1) & (qmin - before >= REL_MAX_DIST)

    cmp_start = np.arange(nc) * NSA_CMP_STRIDE
    sel_start = np.arange(LANES) * NSA_SEL_LEN
    agg_t = ((cmp_start[None, :] <= sel_start[:, None] + NSA_SEL_LEN - 1)
             & (cmp_start[None, :] + NSA_CMP_LEN - 1 >= sel_start[:, None])
             & (np.arange(nc)[None, :] < n_cmp) & (np.arange(LANES)[:, None] < n_sel))

    cols = HG * QB
    full = lambda shape: pl.BlockSpec(shape, lambda b, i, *_: (0,) * len(shape))
    per_b = lambda n, w: pl.BlockSpec((1, G, n, w), lambda b, i, *_: (b, 0, 0, 0))
    per_tile = lambda n, w: pl.BlockSpec((1, G, 1, n, w), lambda b, i, *_: (b, 0, i, 0, 0))
    grid_spec = pltpu.PrefetchScalarGridSpec(
        num_scalar_prefetch=4,
        grid=(bs, nqt),
        in_specs=[per_tile(cols, LANES), per_tile(8, cols),
                  pl.BlockSpec((QB, LANES), lambda b, i, *_: (i, 0)),
                  full((s, LANES)), full((nc, LANES)),
                  per_b(s, 2 * LANES), per_b(DH, s), per_b(s, LANES), per_b(DH, s),
                  per_b(nc, LANES), per_b(DH, nc),
                  full((LANES, nc)), full((NSA_HEADS, LANES))],
        out_specs=pl.BlockSpec((1, QB, NSA_Q), lambda b, i, *_: (b, i, 0)),
        scratch_shapes=[pltpu.VMEM((2, G, 2, 1, cols), F32), pltpu.VMEM((2, G, 2, 1, cols), F32),
                        pltpu.VMEM((2, G, 2, DH, cols), F32)],
    )
    return pl.pallas_call(
        functools.partial(_nsa_body, n_sel=n_sel, sel_k=sel_k),
        grid_spec=grid_spec,
        out_shape=jax.ShapeDtypeStruct((bs, s, NSA_Q), BF16),
        compiler_params=_params("parallel", "arbitrary"),
        name="nsa_attention",
    )(n_fast.astype(jnp.int32), std.astype(jnp.int32), near_w.reshape(-1).astype(jnp.int32),
      near_c.reshape(-1).astype(jnp.int32),
      qs, gl, _pos_digits(positions, False), _pos_digits(positions, True), _pos_digits(cpos, True),
      keys_aug(by_group(k_s), blocks=True), by_group_t(v_s), keys_aug(by_group(k_w)), by_group_t(v_w),
      kc, vc_t, jnp.asarray(agg_t, BF16), td)


SGU_TC = 512


def _sgu_body(uv_ref, lg_ref, lb_ref, w_ref, b_ref, o_ref):
    uv = jax.nn.gelu(uv_ref[...].astype(F32))
    u, v = uv[:, :SGU_WIDTH], uv[:, SGU_WIDTH:]
    mu = jnp.mean(v, axis=-1, keepdims=True)
    vc = v - mu
    var = jnp.mean(vc * vc, axis=-1, keepdims=True)
    vn = (vc * lax.rsqrt(var + RMS_EPS) * lg_ref[...] + lb_ref[...]).astype(BF16)
    gw = SGU_WIDTH // SGU_GROUPS
    for c in range(SGU_TC // SGU_CHUNK):
        rows = slice(c * SGU_CHUNK, (c + 1) * SGU_CHUNK)
        for g in range(SGU_GROUPS):
            cols = slice(g * gw, (g + 1) * gw)
            mix = jnp.dot(w_ref[g], vn[rows, cols], preferred_element_type=F32) + b_ref[:, g:g + 1]
            o_ref[rows, cols] = (u[rows, cols] * mix).astype(o_ref.dtype)


def sgu_mixer(proj, ln_g, ln_b, w_s, b_s):
    t = proj.shape[0]
    assert t % SGU_TC == 0 and OFF_SGU % (2 * SGU_WIDTH) == 0
    fixed = lambda i: (0, 0)
    return pl.pallas_call(
        _sgu_body,
        grid=(t // SGU_TC,),
        in_specs=[pl.BlockSpec((SGU_TC, 2 * SGU_WIDTH), lambda i: (i, OFF_SGU // (2 * SGU_WIDTH))),
                  pl.BlockSpec((1, SGU_WIDTH), fixed), pl.BlockSpec((1, SGU_WIDTH), fixed),
                  pl.BlockSpec((SGU_GROUPS, SGU_CHUNK, SGU_CHUNK), lambda i: (0, 0, 0)),
                  pl.BlockSpec((SGU_CHUNK, SGU_GROUPS), fixed)],
        out_specs=pl.BlockSpec((SGU_TC, SGU_WIDTH), lambda i: (i, 0)),
        out_shape=jax.ShapeDtypeStruct((t, SGU_WIDTH), BF16),
        compiler_params=_params("parallel"),
        name="sgu_gate",
    )(proj, ln_g.reshape(1, SGU_WIDTH), ln_b.reshape(1, SGU_WIDTH), jnp.tril(w_s).astype(BF16), b_s.T)


def _permute_w_in(w):
    sizes = ([SSM_WIDTH, GDN_QKV, GDN_HEADS, GDN_HEADS, GDN_HEADS * GDN_DV, NSA_Q]
             + [NSA_KV] * 6 + [3 * NSA_HEADS, 2 * SGU_WIDTH, N_BRANCH * D_MODEL])
    cuts = [int(c) for c in np.cumsum(sizes)[:-1]]
    (w_ssm, w_qkv, w_a, w_b, w_z, w_nq, w_kc, w_vc, w_ks, w_vs, w_kw, w_vw,
     w_ng, w_sgu, w_gate) = jnp.split(w, cuts, axis=-1)
    pad = jnp.zeros((w.shape[0], PROJ_COLS - OFF_SMALL - SMALL_USED), w.dtype)
    return jnp.concatenate([w_qkv, w_ssm, w_z, w_nq, w_sgu, w_gate, w_kc, w_vc, w_ks, w_vs,
                            w_kw, w_vw, w_a, w_b, w_ng, pad], axis=-1)


def kernel(x, positions, norm_mix, norm_ffn, norm_final, w_in, ssm_a_re, ssm_a_im, ssm_log_dt, ssm_b_re, ssm_b_im, ssm_c_re, ssm_c_im, ssm_d, ssm_glu_w, gdn_conv_w, gdn_a_log, gdn_dt_bias, gdn_norm, nsa_cmp_k_pe, nsa_cmp_k_w1, nsa_cmp_k_w2, nsa_cmp_v_pe, nsa_cmp_v_w1, nsa_cmp_v_w2, rel_bias, sgu_ln_g, sgu_ln_b, sgu_w, sgu_b, w_br_ssm, w_br_gdn, w_br_nsa, w_br_sgu, w_out, ffn_w_gate, ffn_w_up, ffn_w_down, moe_router, moe_w_gate, moe_w_up, moe_w_down):
    bs, s, d = x.shape
    t = bs * s
    assert DEPTH == 2
    xf = x.reshape(t, d)
    for layer in range(DEPTH):
        proj = in_proj(xf, norm_mix[layer], _permute_w_in(w_in[layer]))
        p3 = proj.reshape(bs, s, PROJ_COLS)

        def seg(off, width):
            return p3[..., off:off + width]

        y_ssm = ssm_mixer(proj, bs, ssm_a_re[layer], ssm_a_im[layer], ssm_log_dt[layer],
                          ssm_b_re[layer], ssm_b_im[layer], ssm_c_re[layer], ssm_c_im[layer],
                          ssm_d[layer], ssm_glu_w[layer])
        y_gdn = gdn_mixer(proj, bs, gdn_conv_w[layer], gdn_a_log[layer], gdn_dt_bias[layer],
                          gdn_norm[layer])
        kvs = [seg(OFF_KV + i * NSA_KV, NSA_KV) for i in range(6)]
        y_nsa = nsa_mixer(seg(OFF_NQ, NSA_Q), *kvs, seg(OFF_SMALL + 2 * GDN_HEADS, 3 * NSA_HEADS),
                          positions, nsa_cmp_k_pe[layer], nsa_cmp_k_w1[layer], nsa_cmp_k_w2[layer],
                          nsa_cmp_v_pe[layer], nsa_cmp_v_w1[layer], nsa_cmp_v_w2[layer], rel_bias)
        y_sgu = sgu_mixer(proj, sgu_ln_g[layer], sgu_ln_b[layer], sgu_w[layer], sgu_b[layer])
        ys = [y_ssm, y_gdn, y_nsa.reshape(t, -1), y_sgu]
        ws = [w[layer].astype(BF16) for w in (w_br_ssm, w_br_gdn, w_br_nsa, w_br_sgu)]
        xf = merge(xf, proj, ys, ws, w_out[layer].astype(BF16))
        i = layer // 2
        if layer % 2 == 0:
            xf = dense_ffn(xf, norm_ffn[layer], ffn_w_gate[i], ffn_w_up[i], ffn_w_down[i])
        else:
            xf = moe_layer(xf, norm_ffn[layer], moe_router[i], moe_w_gate[i], moe_w_up[i],
                           moe_w_down[i], norm_final)
    return xf.reshape(bs, s, d)
```

```python
import functools
import math

import jax
import jax.numpy as jnp
from jax import lax
import numpy as np
from jax.experimental import pallas as pl
from jax.experimental.pallas import tpu as pltpu

F32 = jnp.float32
BF16 = jnp.bfloat16

D_MODEL = 1024
DEPTH = 2
SSM_WIDTH = D_MODEL // 2
SSM_GROUP = 16
SSM_GROUPS = SSM_WIDTH // SSM_GROUP
SSM_STATE = 64
GDN_HEADS = 4
GDN_DK = 128
GDN_DV = 128
GDN_CONV = 4
GDN_CHUNK = 64
NSA_HEADS = 8
NSA_KV_GROUPS = 2
NSA_HPG = NSA_HEADS // NSA_KV_GROUPS
NSA_DH = 64
NSA_CMP_LEN = 32
NSA_CMP_STRIDE = 16
NSA_CMP_HIDDEN = 128
NSA_SEL_LEN = 64
NSA_SEL_TOPK = 16
NSA_WINDOW = 512
NSA_QBLOCK = 128
SGU_WIDTH = D_MODEL // 2
SGU_GROUPS = 4
SGU_CHUNK = 128
REL_BUCKETS = 32
REL_MAX_DIST = 128
FFN_DENSE = 2816
N_EXPERTS = 8
TOP_K = 2
FFN_EXPERT = 3584
N_BRANCH = 4
RMS_EPS = 1e-6
GDN_QKV = GDN_HEADS * (2 * GDN_DK + GDN_DV)
NSA_Q = NSA_HEADS * NSA_DH
NSA_KV = NSA_KV_GROUPS * NSA_DH

LANES = 128
VMEM_LIMIT = 48 * 1024 * 1024
MASKED = -1e30

OFF_QKV = 0
OFF_SSM = OFF_QKV + GDN_QKV
OFF_Z = OFF_SSM + SSM_WIDTH
OFF_NQ = OFF_Z + GDN_HEADS * GDN_DV
OFF_SGU = OFF_NQ + NSA_Q
OFF_GATE = OFF_SGU + 2 * SGU_WIDTH
OFF_KV = OFF_GATE + N_BRANCH * D_MODEL
OFF_SMALL = OFF_KV + 6 * NSA_KV
SMALL_USED = 2 * GDN_HEADS + 3 * NSA_HEADS
PROJ_COLS = 9216


def _params(*sem):
    return pltpu.CompilerParams(dimension_semantics=sem, vmem_limit_bytes=VMEM_LIMIT)


def _rms(x, g):
    return x * lax.rsqrt(jnp.mean(x * x, axis=-1, keepdims=True) + RMS_EPS) * g


def _dot_nt(a, b):
    return lax.dot_general(a, b, (((1,), (1,)), ((), ())), preferred_element_type=F32)


def _dot_tn(a, b):
    return lax.dot_general(a, b, (((0,), (0,)), ((), ())), preferred_element_type=F32)


def _split_bf16(x):
    hi = x.astype(BF16)
    return hi, (x - hi.astype(F32)).astype(BF16)


def _in_proj_body(x_ref, g_ref, w_ref, o_ref, h_ref):
    @pl.when(pl.program_id(1) == 0)
    def _():
        h_ref[...] = _rms(x_ref[...], g_ref[...]).astype(BF16)

    o_ref[...] = jnp.dot(h_ref[...], w_ref[...].astype(BF16),
                         preferred_element_type=F32).astype(o_ref.dtype)


def in_proj(x, g, w, tm=2048, tn=512):
    t, d = x.shape
    n = w.shape[1]
    return pl.pallas_call(
        _in_proj_body,
        grid=(t // tm, n // tn),
        in_specs=[pl.BlockSpec((tm, d), lambda i, j: (i, 0)),
                  pl.BlockSpec((1, d), lambda i, j: (0, 0)),
                  pl.BlockSpec((d, tn), lambda i, j: (0, j))],
        out_specs=pl.BlockSpec((tm, tn), lambda i, j: (i, j)),
        out_shape=jax.ShapeDtypeStruct((t, n), BF16),
        scratch_shapes=[pltpu.VMEM((tm, d), BF16)],
        compiler_params=_params("parallel", "arbitrary"),
        name="in_proj",
    )(x, g.reshape(1, d), w)


def _merge_body(x_ref, gate_ref, ya_ref, yb_ref, yc_ref, yd_ref,
                wa_ref, wb_ref, wc_ref, wd_ref, wo_ref, o_ref):
    def branch(k, y_ref, w_ref):
        p = jnp.dot(y_ref[...], w_ref[...], preferred_element_type=F32)
        return jax.nn.sigmoid(gate_ref[:, k * D_MODEL:(k + 1) * D_MODEL].astype(F32)) * p

    merged = (branch(0, ya_ref, wa_ref) + branch(1, yb_ref, wb_ref)
              + branch(2, yc_ref, wc_ref) + branch(3, yd_ref, wd_ref))
    o_ref[...] = x_ref[...] + jnp.dot(merged.astype(BF16), wo_ref[...],
                                      preferred_element_type=F32)


def merge(x, proj, ys, ws, w_out, tm=256):
    t, d = x.shape
    gate_blk = OFF_GATE // (N_BRANCH * D_MODEL)
    row = lambda i: (i, 0)
    fixed = lambda i: (0, 0)
    in_specs = [pl.BlockSpec((tm, d), row),
                pl.BlockSpec((tm, N_BRANCH * D_MODEL), lambda i: (i, gate_blk))]
    in_specs += [pl.BlockSpec((tm, y.shape[1]), row) for y in ys]
    in_specs += [pl.BlockSpec(w.shape, fixed) for w in ws]
    in_specs += [pl.BlockSpec(w_out.shape, fixed)]
    return pl.pallas_call(
        _merge_body,
        grid=(t // tm,),
        in_specs=in_specs,
        out_specs=pl.BlockSpec((tm, d), row),
        out_shape=jax.ShapeDtypeStruct((t, d), F32),
        compiler_params=_params("parallel"),
        name="merge",
    )(x, proj, *ys, *ws, w_out)


def _ffn_body(x_ref, g_ref, wg_ref, wu_ref, wd_ref, o_ref, h_ref, acc_ref):
    f = pl.program_id(1)

    @pl.when(f == 0)
    def _():
        h_ref[...] = _rms(x_ref[...], g_ref[...]).astype(BF16)
        acc_ref[...] = x_ref[...]

    h = h_ref[...]
    a = jnp.dot(h, wg_ref[...].astype(BF16), preferred_element_type=F32)
    u = jnp.dot(h, wu_ref[...].astype(BF16), preferred_element_type=F32)
    act = (a * jax.nn.sigmoid(a) * u).astype(BF16)
    acc_ref[...] += jnp.dot(act, wd_ref[...].astype(BF16), preferred_element_type=F32)

    @pl.when(f == pl.num_programs(1) - 1)
    def _():
        o_ref[...] = acc_ref[...]


def dense_ffn(x, g, wg, wu, wd, tm=1024, tf=256):
    t, d = x.shape
    nf = wg.shape[1]
    return pl.pallas_call(
        _ffn_body,
        grid=(t // tm, nf // tf),
        in_specs=[pl.BlockSpec((tm, d), lambda i, f: (i, 0)),
                  pl.BlockSpec((1, d), lambda i, f: (0, 0)),
                  pl.BlockSpec((d, tf), lambda i, f: (0, f)),
                  pl.BlockSpec((d, tf), lambda i, f: (0, f)),
                  pl.BlockSpec((tf, d), lambda i, f: (f, 0))],
        out_specs=pl.BlockSpec((tm, d), lambda i, f: (i, 0)),
        out_shape=jax.ShapeDtypeStruct((t, d), F32),
        scratch_shapes=[pltpu.VMEM((tm, d), BF16), pltpu.VMEM((tm, d), F32)],
        compiler_params=_params("parallel", "arbitrary"),
        name="dense_ffn",
    )(x, g.reshape(1, d), wg, wu, wd)


def _route_body(x_ref, g_ref, rhi_ref, rlo_ref, h_ref, idx_ref, wt_ref):
    h = _rms(x_ref[...], g_ref[...])
    hi = h.astype(BF16)
    lo = (h - hi.astype(F32)).astype(BF16)
    h_ref[...] = h
    logits = (jnp.dot(hi, rhi_ref[...], preferred_element_type=F32)
              + jnp.dot(hi, rlo_ref[...], preferred_element_type=F32)
              + jnp.dot(lo, rhi_ref[...], preferred_element_type=F32))
    col = lax.broadcasted_iota(jnp.int32, logits.shape, 1).astype(F32)
    neg = jnp.float32(-jnp.inf)
    lg = jnp.where(col < N_EXPERTS, logits, neg)
    m1 = jnp.max(lg, axis=-1, keepdims=True)
    i1 = jnp.min(jnp.where(lg == m1, col, float(LANES)), axis=-1, keepdims=True)
    lg2 = jnp.where(col == i1, neg, lg)
    m2 = jnp.max(lg2, axis=-1, keepdims=True)
    i2 = jnp.min(jnp.where(lg2 == m2, col, float(LANES)), axis=-1, keepdims=True)
    e = jnp.exp(m2 - m1)
    w1 = 1.0 / (1.0 + e)
    w2 = e / (1.0 + e)
    idx_ref[...] = jnp.where(col == 0, i1, jnp.where(col == 1, i2, 0.0)).astype(jnp.int32)
    wt_ref[...] = jnp.where(col == 0, w1, jnp.where(col == 1, w2, 0.0))


def moe_route(x, g, r_hi, r_lo, tm=512):
    t, d = x.shape
    row = lambda i: (i, 0)
    fixed = lambda i: (0, 0)
    return pl.pallas_call(
        _route_body,
        grid=(t // tm,),
        in_specs=[pl.BlockSpec((tm, d), row), pl.BlockSpec((1, d), fixed),
                  pl.BlockSpec((d, LANES), fixed), pl.BlockSpec((d, LANES), fixed)],
        out_specs=[pl.BlockSpec((tm, d), row), pl.BlockSpec((tm, LANES), row),
                   pl.BlockSpec((tm, LANES), row)],
        out_shape=[jax.ShapeDtypeStruct((t, d), F32),
                   jax.ShapeDtypeStruct((t, LANES), jnp.int32),
                   jax.ShapeDtypeStruct((t, LANES), F32)],
        compiler_params=_params("parallel"),
        name="moe_route",
    )(x, g.reshape(1, d), r_hi, r_lo)


def _experts_body(te_ref, nu_ref, src_ref, h_hbm, wg_ref, wu_ref, wd_ref, o_ref,
                  xbuf_ref, xs_ref, acc_ref, sem, *, rows_per_step):
    i = pl.program_id(0)
    f = pl.program_id(1)
    tm = xs_ref.shape[0]
    n_rows = xbuf_ref.shape[1]
    n_used = nu_ref[0]
    used = i < n_used
    has_next = i + 1 < n_used
    slot = i % 2

    def start_row(tile, slot_, r):
        tok = src_ref[tile * tm + r]
        pltpu.make_async_copy(h_hbm.at[pl.ds(tok, 1), :], xbuf_ref.at[slot_, pl.ds(r, 1), :],
                              sem.at[slot_]).start()

    @pl.when(f == 0)
    def _():
        acc_ref[...] = jnp.zeros_like(acc_ref)

        @pl.when(used)
        def _():
            @pl.when(i == 0)
            def _():
                def row(r, carry):
                    start_row(0, 0, r)
                    return carry

                lax.fori_loop(0, n_rows, row, 0, unroll=8)

            pltpu.make_async_copy(h_hbm.at[pl.ds(0, n_rows), :], xbuf_ref.at[slot], sem.at[slot]).wait()
            xs_ref[...] = xbuf_ref[slot, 0:tm, :].astype(BF16)

    def ffn_step():
        h = xs_ref[...]
        a = jnp.dot(h, wg_ref[0].astype(BF16), preferred_element_type=F32)
        u = jnp.dot(h, wu_ref[0].astype(BF16), preferred_element_type=F32)
        act = (a * jax.nn.sigmoid(a) * u).astype(BF16)
        acc_ref[...] += jnp.dot(act, wd_ref[0].astype(BF16), preferred_element_type=F32)

    @pl.when(used & has_next)
    def _():
        for j in range(rows_per_step):
            start_row(i + 1, 1 - slot, f * rows_per_step + j)
        ffn_step()

    @pl.when(used & jnp.logical_not(has_next))
    def _():
        ffn_step()

    @pl.when(f == pl.num_programs(1) - 1)
    def _():
        o_ref[...] = acc_ref[...].astype(o_ref.dtype)


def moe_experts(tile_expert, n_used, src, h, wg, wu, wd, tm, tf=512):
    d = h.shape[1]
    nf = wg.shape[2] // tf
    rows_per_step = 8 * (-(-tm // (8 * nf)))
    n_rows = rows_per_step * nf
    n_tiles = src.shape[0] // tm
    src = jnp.pad(src, (0, n_rows - tm))
    grid_spec = pltpu.PrefetchScalarGridSpec(
        num_scalar_prefetch=3,
        grid=(n_tiles, nf),
        in_specs=[pl.BlockSpec(memory_space=pl.ANY),
                  pl.BlockSpec((1, d, tf), lambda i, f, te, *_: (te[i], 0, f)),
                  pl.BlockSpec((1, d, tf), lambda i, f, te, *_: (te[i], 0, f)),
                  pl.BlockSpec((1, tf, d), lambda i, f, te, *_: (te[i], f, 0))],
        out_specs=pl.BlockSpec((tm, d), lambda i, f, *_: (i, 0)),
        scratch_shapes=[pltpu.VMEM((2, n_rows, d), F32), pltpu.VMEM((tm, d), BF16),
                        pltpu.VMEM((tm, d), F32), pltpu.SemaphoreType.DMA((2,))],
    )
    return pl.pallas_call(
        functools.partial(_experts_body, rows_per_step=rows_per_step),
        grid_spec=grid_spec,
        out_shape=jax.ShapeDtypeStruct((n_tiles * tm, d), BF16),
        compiler_params=_params("arbitrary", "arbitrary"),
        name="moe_experts",
    )(tile_expert, n_used, src, h, wg, wu, wd)


def _combine_norm_body(x_ref, ya_ref, yb_ref, wt_ref, g_ref, o_ref):
    wt = wt_ref[...]
    y = wt[:, 0:1] * ya_ref[...].astype(F32) + wt[:, 1:2] * yb_ref[...].astype(F32)
    o_ref[...] = _rms(x_ref[...] + y, g_ref[...])


def combine_norm(x, ya, yb, wts, g, tm=1024):
    t, d = x.shape
    row = lambda i: (i, 0)
    return pl.pallas_call(
        _combine_norm_body,
        grid=(t // tm,),
        in_specs=[pl.BlockSpec((tm, d), row)] * 3 + [pl.BlockSpec((tm, LANES), row),
                                                     pl.BlockSpec((1, d), lambda i: (0, 0))],
        out_specs=pl.BlockSpec((tm, d), row),
        out_shape=jax.ShapeDtypeStruct((t, d), F32),
        compiler_params=_params("parallel"),
        name="combine_norm",
    )(x, ya, yb, wts, g.reshape(1, d))


def moe_layer(x, g_norm, router, wg, wu, wd, g_final, tm=1024):
    t, d = x.shape
    r = jnp.zeros((d, LANES), F32).at[:, :N_EXPERTS].set(router)
    r_hi = r.astype(BF16)
    r_lo = (r - r_hi.astype(F32)).astype(BF16)
    h, idx, wts = moe_route(x, g_norm, r_hi, r_lo)
    e_flat = jnp.concatenate([idx[:, 0], idx[:, 1]])
    onehot = (e_flat[:, None] == jnp.arange(N_EXPERTS)[None, :]).astype(jnp.int32)
    csum = jnp.cumsum(onehot, axis=0)
    rank = jnp.sum((csum - onehot) * onehot, axis=1)
    counts = csum[-1]
    padded = ((counts + tm - 1) // tm) * tm
    ends = jnp.cumsum(padded)
    starts = ends - padded
    dest = starts[e_flat] + rank
    n_tiles = (TOP_K * t) // tm + N_EXPERTS
    p = n_tiles * tm
    tok = jnp.concatenate([jnp.arange(t, dtype=jnp.int32)] * TOP_K)
    src = jnp.zeros((p,), jnp.int32).at[dest].set(tok)
    tile_expert = jnp.minimum(
        jnp.searchsorted(ends, jnp.arange(n_tiles, dtype=jnp.int32) * tm, side="right"),
        N_EXPERTS - 1).astype(jnp.int32)
    n_used = (ends[-1] // tm).astype(jnp.int32).reshape(1)
    ys = moe_experts(tile_expert, n_used, src, h, wg, wu, wd, tm)
    ya = jnp.take(ys, dest[:t], axis=0, mode="clip")
    yb = jnp.take(ys, dest[t:], axis=0, mode="clip")
    return combine_norm(x, ya, yb, wts, g_final)


SSM_TC = 512
SSM_SUB = 128
SSM_HALF = 256
SSM_NSTATE = SSM_GROUPS * SSM_STATE


def _ssm_body(u_ref, bw_ref, cw_ref, d_ref, glu_ref, ar_ref, ai_ref, pr_ref, pi_ref, o_ref,
              xr_ref, xi_ref, cr_ref, ci_ref):
    @pl.when(pl.program_id(1) == 0)
    def _():
        cr_ref[...] = jnp.zeros_like(cr_ref)
        ci_ref[...] = jnp.zeros_like(ci_ref)

    ub = u_ref[...]
    u = ub.astype(F32)
    nblk = SSM_WIDTH // SSM_HALF
    sblk = SSM_NSTATE // nblk
    for k in range(nblk):
        bu = jnp.dot(ub[:, k * SSM_HALF:(k + 1) * SSM_HALF], bw_ref[k], preferred_element_type=F32)
        xr_ref[:, k * sblk:(k + 1) * sblk] = bu[:, :sblk]
        xi_ref[:, k * sblk:(k + 1) * sblk] = bu[:, sblk:]

    row8 = lax.broadcasted_iota(jnp.int32, (SSM_SUB, LANES), 0) % 8
    n_grp = SSM_SUB // 8

    def strip(c, carry):
        col = pl.ds(pl.multiple_of(c * LANES, LANES), LANES)
        c_r = cr_ref[:, col]
        c_i = ci_ref[:, col]
        p_r = pr_ref[0:8, col]
        p_i = pi_ref[0:8, col]
        step_mul = [(jnp.where(row8 >= (1 << i), ar_ref[i:i + 1, col], 0.0),
                     jnp.where(row8 >= (1 << i), ai_ref[i:i + 1, col], 0.0)) for i in range(3)]
        for sub in range(SSM_TC // SSM_SUB):
            rows = slice(sub * SSM_SUB, (sub + 1) * SSM_SUB)
            xr = xr_ref[rows, col]
            xi = xi_ref[rows, col]
            for i in range(3):
                a_r, a_i = step_mul[i]
                sr, si = pltpu.roll(xr, 1 << i, 0), pltpu.roll(xi, 1 << i, 0)
                xr, xi = xr + a_r * sr - a_i * si, xi + a_r * si + a_i * sr
            out_r, out_i = [], []
            for r in range(n_grp):
                g_r = xr[8 * r:8 * r + 8] + p_r * c_r - p_i * c_i
                g_i = xi[8 * r:8 * r + 8] + p_r * c_i + p_i * c_r
                c_r, c_i = g_r[7:8, :], g_i[7:8, :]
                out_r.append(g_r)
                out_i.append(g_i)
            xr_ref[rows, col] = jnp.concatenate(out_r, axis=0)
            xi_ref[rows, col] = jnp.concatenate(out_i, axis=0)
        cr_ref[:, col] = c_r
        ci_ref[:, col] = c_i
        return carry

    lax.fori_loop(0, SSM_NSTATE // LANES, strip, 0)

    ys = []
    for k in range(nblk):
        st = jnp.concatenate([xr_ref[:, k * sblk:(k + 1) * sblk].astype(BF16),
                              xi_ref[:, k * sblk:(k + 1) * sblk].astype(BF16)], axis=1)
        ys.append(jnp.dot(st, cw_ref[k], preferred_element_type=F32))
    y = jax.nn.gelu(jnp.concatenate(ys, axis=1) + d_ref[...] * u)
    gated = y * jax.nn.sigmoid(jnp.dot(y.astype(BF16), glu_ref[...], preferred_element_type=F32))
    o_ref[...] = gated.astype(o_ref.dtype)


def ssm_mixer(proj, bs, a_re, a_im, log_dt, b_re, b_im, c_re, c_im, d_skip, glu_w):
    t = proj.shape[0]
    s = t // bs
    nt = s // SSM_TC
    assert s % SSM_TC == 0 and OFF_SSM % SSM_WIDTH == 0
    nblk = SSM_WIDTH // SSM_HALF
    gpb = SSM_GROUPS // nblk
    dt = jnp.exp(log_dt)[:, None]
    mag = jnp.exp(a_re * dt)
    abar_r, abar_i = mag * jnp.cos(a_im * dt), mag * jnp.sin(a_im * dt)
    nr, ni = abar_r - 1.0, abar_i
    den = a_re * a_re + a_im * a_im
    sr, si = (nr * a_re + ni * a_im) / den, (ni * a_re - nr * a_im) / den
    bbar_r = sr[..., None] * b_re - si[..., None] * b_im
    bbar_i = sr[..., None] * b_im + si[..., None] * b_re

    def powers(k):
        kk = k.astype(F32)[:, None, None]
        m = jnp.exp(kk * (a_re * dt))
        return ((m * jnp.cos(kk * (a_im * dt))).reshape(-1, SSM_NSTATE),
                (m * jnp.sin(kk * (a_im * dt))).reshape(-1, SSM_NSTATE))

    ar, ai = powers(2 ** jnp.arange(8))
    pr, pi = powers(jnp.arange(1, SSM_SUB + 1))
    eye = jnp.eye(gpb, dtype=F32)

    def in_block(w):
        return jnp.einsum('gnp,gh->gphn', w, eye).reshape(gpb * SSM_GROUP, gpb * SSM_STATE)

    def out_block(w):
        return jnp.einsum('gpn,gh->gnhp', w, eye).reshape(gpb * SSM_STATE, gpb * SSM_GROUP)

    bw = jnp.stack([jnp.concatenate([in_block(bbar_r[k * gpb:(k + 1) * gpb]),
                                     in_block(bbar_i[k * gpb:(k + 1) * gpb])], axis=1)
                    for k in range(nblk)]).astype(BF16)
    cw = jnp.stack([jnp.concatenate([out_block(c_re[k * gpb:(k + 1) * gpb]),
                                     -out_block(c_im[k * gpb:(k + 1) * gpb])], axis=0)
                    for k in range(nblk)]).astype(BF16)
    fixed2 = lambda b, i: (0, 0)
    fixed3 = lambda b, i: (0, 0, 0)
    return pl.pallas_call(
        _ssm_body,
        grid=(bs, nt),
        in_specs=[pl.BlockSpec((SSM_TC, SSM_WIDTH), lambda b, i: (b * nt + i, OFF_SSM // SSM_WIDTH)),
                  pl.BlockSpec(bw.shape, fixed3), pl.BlockSpec(cw.shape, fixed3),
                  pl.BlockSpec((1, SSM_WIDTH), fixed2), pl.BlockSpec((SSM_WIDTH, SSM_WIDTH), fixed2),
                  pl.BlockSpec((8, SSM_NSTATE), fixed2), pl.BlockSpec((8, SSM_NSTATE), fixed2),
                  pl.BlockSpec((SSM_SUB, SSM_NSTATE), fixed2), pl.BlockSpec((SSM_SUB, SSM_NSTATE), fixed2)],
        out_specs=pl.BlockSpec((SSM_TC, SSM_WIDTH), lambda b, i: (b * nt + i, 0)),
        out_shape=jax.ShapeDtypeStruct((t, SSM_WIDTH), BF16),
        scratch_shapes=[pltpu.VMEM((SSM_TC, SSM_NSTATE), F32), pltpu.VMEM((SSM_TC, SSM_NSTATE), F32),
                        pltpu.VMEM((1, SSM_NSTATE), F32), pltpu.VMEM((1, SSM_NSTATE), F32)],
        compiler_params=_params("parallel", "arbitrary"),
        name="ssm_scan",
    )(proj, bw, cw, d_skip.reshape(1, SSM_WIDTH), glu_w.astype(BF16), ar, ai, pr, pi)


GDN_TC = 256
GDN_HALO = 8


def _gdn_body(nega_ref, dtb_ref, q_ref, k_ref, v_ref, z_ref, sm_ref, wq_ref, wk_ref, wv_ref, ng_ref,
              o_ref, state_ref, hq_ref, hk_ref, hv_ref, vnew_ref):
    TC, CH, DK, H = GDN_TC, GDN_CHUNK, GDN_DK, GDN_HEADS
    heads = range(H)

    @pl.when(pl.program_id(1) == 0)
    def _():
        state_ref[...] = jnp.zeros_like(state_ref)
        hq_ref[...] = jnp.zeros_like(hq_ref)
        hk_ref[...] = jnp.zeros_like(hk_ref)
        hv_ref[...] = jnp.zeros_like(hv_ref)

    def conv_silu(x_ref, halo_ref, w_ref):
        x = x_ref[...].astype(F32)
        xx = jnp.concatenate([halo_ref[...], x], axis=0)
        w = w_ref[...]
        y = w[GDN_CONV - 1:GDN_CONV] * x
        for d in range(1, GDN_CONV):
            y = y + w[GDN_CONV - 1 - d:GDN_CONV - d] * pltpu.roll(xx, d, 0)[GDN_HALO:GDN_HALO + TC]
        halo_ref[...] = x[TC - GDN_HALO:TC]
        return y * jax.nn.sigmoid(y)

    def per_head(x):
        return [x[:, h * LANES:(h + 1) * LANES] for h in heads]

    q = per_head(conv_silu(q_ref, hq_ref, wq_ref))
    k = per_head(conv_silu(k_ref, hk_ref, wk_ref))
    v = per_head(conv_silu(v_ref, hv_ref, wv_ref))
    q = [x * lax.rsqrt(jnp.sum(x * x, axis=-1, keepdims=True) + 1e-6) * (DK ** -0.5) for x in q]
    k = [x * lax.rsqrt(jnp.sum(x * x, axis=-1, keepdims=True) + 1e-6) for x in k]

    small = sm_ref[...].astype(F32)
    beta = [jax.nn.sigmoid(small[:, H + h:H + h + 1]) for h in heads]
    la = []
    for h in heads:
        xa = small[:, h:h + 1] + dtb_ref[h]
        softplus = jnp.maximum(xa, 0.0) + jnp.log(1.0 + jnp.exp(-jnp.abs(xa)))
        la.append(jnp.broadcast_to(nega_ref[h] * softplus, (TC, LANES)))

    ri = lax.broadcasted_iota(jnp.int32, (TC, TC), 0)
    ci = lax.broadcasted_iota(jnp.int32, (TC, TC), 1)
    same = (ri // CH) == (ci // CH)
    causal = jnp.where(same, jnp.where(ci <= ri, 1.0, 0.0), 0.0)
    strict = jnp.where(ci < ri, causal, 0.0)
    eye = jnp.where(ri == ci, 1.0, 0.0)
    tri = causal.astype(BF16)
    lane = lax.broadcasted_iota(jnp.int32, (TC, LANES), 1)

    g = []
    for h in heads:
        la_hi, la_lo = _split_bf16(la[h])
        g.append(jnp.dot(tri, la_hi, preferred_element_type=F32)
                 + jnp.dot(tri, la_lo, preferred_element_type=F32))
    decay = []
    for h in heads:
        g_hi = g[h].astype(BF16).astype(F32)
        g_lo = g[h] - g_hi
        lhs = jnp.where(lane == 0, g_hi, jnp.where(lane == 1, g_lo, jnp.where(lane < 4, 1.0, 0.0)))
        rhs = jnp.where(lane < 2, 1.0, jnp.where(lane == 2, -g_hi, jnp.where(lane == 3, -g_lo, 0.0)))
        decay.append(jnp.exp(jnp.where(causal > 0.5, _dot_nt(lhs.astype(BF16), rhs.astype(BF16)), MASKED)))

    kb = [k[h] * beta[h] for h in heads]
    k_b = [x.astype(BF16) for x in k]
    lmat = [strict * _dot_nt(kb[h].astype(BF16), k_b[h]) * decay[h] for h in heads]
    tinv = [eye - x for x in lmat]
    pw = lmat
    for _ in range(CH.bit_length() - 2):
        pw = [jnp.dot(x.astype(BF16), x.astype(BF16), preferred_element_type=F32) for x in pw]
        tinv = [tinv[h] + jnp.dot(tinv[h].astype(BF16), pw[h].astype(BF16), preferred_element_type=F32)
                for h in heads]
    eg = [jnp.exp(x) for x in g]
    sol = [jnp.dot(tinv[h].astype(BF16),
                   jnp.concatenate([v[h] * beta[h], kb[h] * eg[h]], axis=1).astype(BF16),
                   preferred_element_type=F32) for h in heads]
    attn = [(causal * _dot_nt(q[h].astype(BF16), k_b[h]) * decay[h]).astype(BF16) for h in heads]
    q_st = [(q[h] * eg[h]).astype(BF16) for h in heads]

    vnew_ref[...] = jnp.zeros_like(vnew_ref)
    for c in range(TC // CH):
        rows = slice(c * CH, (c + 1) * CH)
        g_last = [x[(c + 1) * CH - 1:(c + 1) * CH, :] for x in g]
        st = [state_ref[h] for h in heads]
        st_b = [x.astype(BF16) for x in st]
        v_new = [sol[h][rows, :GDN_DV]
                 - jnp.dot(sol[h][rows, GDN_DV:].astype(BF16), st_b[h], preferred_element_type=F32)
                 for h in heads]
        for h in heads:
            vnew_ref[h, rows, :] = v_new[h].astype(BF16)
        o_c = [jnp.dot(q_st[h][rows], st_b[h], preferred_element_type=F32)
               + jnp.dot(attn[h][rows], vnew_ref[h], preferred_element_type=F32) for h in heads]
        for h in heads:
            k_st = (k[h][rows] * jnp.exp(g_last[h] - g[h][rows])).astype(BF16)
            state_ref[h] = st[h] * jnp.exp(g_last[h][:, :1]) + _dot_tn(k_st, v_new[h].astype(BF16))
        for h in heads:
            o = o_c[h] * lax.rsqrt(jnp.mean(o_c[h] * o_c[h], axis=-1, keepdims=True) + RMS_EPS) * ng_ref[...]
            zc = z_ref[rows, h * LANES:(h + 1) * LANES].astype(F32)
            o_ref[rows, h * LANES:(h + 1) * LANES] = (o * (zc * jax.nn.sigmoid(zc))).astype(o_ref.dtype)


def gdn_mixer(proj, bs, conv_w, a_log, dt_bias, norm_g):
    t = proj.shape[0]
    s = t // bs
    nt = s // GDN_TC
    H = GDN_HEADS
    hw = H * LANES
    assert s % GDN_TC == 0 and GDN_DK == LANES and GDN_DV == LANES and OFF_QKV == 0 and OFF_Z % hw == 0
    cw = jnp.zeros((8, GDN_QKV), F32).at[:GDN_CONV].set(conv_w)
    tok = lambda blk: pl.BlockSpec((GDN_TC, hw), lambda b, i: (b * nt + i, blk))
    wspec = lambda blk: pl.BlockSpec((8, hw), lambda b, i: (0, blk))
    smem = pl.BlockSpec(memory_space=pltpu.SMEM)
    return pl.pallas_call(
        _gdn_body,
        grid=(bs, nt),
        in_specs=[smem, smem, tok(0), tok(1), tok(2), tok(OFF_Z // hw),
                  pl.BlockSpec((GDN_TC, 2 * LANES), lambda b, i: (b * nt + i, OFF_SMALL // (2 * LANES))),
                  wspec(0), wspec(1), wspec(2), pl.BlockSpec((1, GDN_DV), lambda b, i: (0, 0))],
        out_specs=pl.BlockSpec((GDN_TC, hw), lambda b, i: (b * nt + i, 0)),
        out_shape=jax.ShapeDtypeStruct((t, hw), BF16),
        scratch_shapes=[pltpu.VMEM((H, GDN_DK, GDN_DV), F32)] + [pltpu.VMEM((GDN_HALO, hw), F32)] * 3
        + [pltpu.VMEM((H, GDN_TC, GDN_DV), BF16)],
        compiler_params=_params("parallel", "arbitrary"),
        name="gdn_delta",
    )(-jnp.exp(a_log), dt_bias.astype(F32), proj, proj, proj, proj, proj, cw, cw, cw,
      norm_g.reshape(1, GDN_DV))


NSA_KT = 256
NSA_NEG = MASKED
NSA_WTILES = NSA_WINDOW // LANES + 1


def _bucket_table():
    n = np.arange(LANES)
    max_exact = REL_BUCKETS // 2
    nf = np.maximum(n, 1).astype(np.float32)
    large = max_exact + (np.log(nf / np.float32(max_exact)) / np.float32(math.log(REL_MAX_DIST / max_exact))
                         * np.float32(REL_BUCKETS - max_exact)).astype(np.int32)
    tbl = np.where(n < max_exact, n, np.minimum(large, REL_BUCKETS - 1))
    assert tbl[-1] == REL_BUCKETS - 1 and REL_MAX_DIST <= LANES
    return tbl


def _compress_body(x_ref, pe_ref, w1_ref, w2_ref, o_ref):
    x = x_ref[0, 0, 0]
    w1 = w1_ref[0]
    half = NSA_CMP_STRIDE * NSA_DH
    nc = x.shape[0]
    a = jnp.dot(x, w1[:half], preferred_element_type=F32)
    b = jnp.dot(x, w1[half:], preferred_element_type=F32)
    pe_h = jnp.dot(pe_ref[0], w1, preferred_element_type=F32)
    hid = a + pltpu.roll(b, nc - 1, 0) + pe_h[0:1, :]
    o_ref[0, 0, 0] = jnp.dot(jax.nn.gelu(hid).astype(BF16), w2_ref[0], preferred_element_type=F32)


def nsa_compress(x, pe, w1, w2):
    _, bs, g, nc, width = x.shape
    return pl.pallas_call(
        _compress_body,
        grid=(2, bs, g),
        in_specs=[pl.BlockSpec((1, 1, 1, nc, width), lambda i, b, j: (i, b, j, 0, 0)),
                  pl.BlockSpec((1,) + pe.shape[1:], lambda i, b, j: (i, 0, 0)),
                  pl.BlockSpec((1,) + w1.shape[1:], lambda i, b, j: (i, 0, 0)),
                  pl.BlockSpec((1,) + w2.shape[1:], lambda i, b, j: (i, 0, 0))],
        out_specs=pl.BlockSpec((1, 1, 1, nc, NSA_DH), lambda i, b, j: (i, b, j, 0, 0)),
        out_shape=jax.ShapeDtypeStruct((2, bs, g, nc, NSA_DH), F32),
        compiler_params=_params("parallel", "parallel", "parallel"),
        name="nsa_compress",
    )(x, pe, w1, w2)


def _nsa_body(nfast, std, near_w, near_c, q_ref, gl_ref, qp_ref, kp_ref, cp_ref,
              ks_ref, vs_ref, kw_ref, vw_ref, kc_ref, vc_ref, agg_ref, td_ref, o_ref,
              m_ref, l_ref, acc_ref, sa_ref, sb_ref, *, n_sel, sel_k):
    QB, HG, G = NSA_QBLOCK, NSA_HPG, NSA_KV_GROUPS
    groups = range(G)
    qi = pl.program_id(1)
    s0 = qi * QB
    nc = kc_ref.shape[2]
    n_ct = nc // LANES
    q = [q_ref[0, g, 0] for g in groups]
    qp = qp_ref[...]

    def lanes4(x):
        return jnp.concatenate([x] * HG, axis=1)

    def delta_t(kp):
        idx = jnp.clip(_dot_nt(kp, qp), 0.0, float(LANES - 1)).astype(jnp.int32)
        outs = []
        for g in groups:
            heads = []
            for hg in range(HG):
                tb = jnp.broadcast_to(td_ref[g * HG + hg:g * HG + hg + 1, :], idx.shape)
                heads.append(jnp.take_along_axis(tb, idx, axis=1))
            outs.append(jnp.concatenate(heads, axis=1))
        return outs

    def maybe_delta(flag, kp):
        zeros = jnp.zeros((kp.shape[0], HG * QB), F32)
        return lax.cond(flag != 0, lambda: tuple(delta_t(kp)), lambda: (zeros,) * G)

    gate = [jax.nn.sigmoid(gl_ref[0, g, 0]) for g in groups]

    sc = [_dot_nt(kc_ref[0, g], q[g]) for g in groups]
    dl = [maybe_delta(near_c[qi * n_ct + ct], cp_ref[ct * LANES:(ct + 1) * LANES, :]) for ct in range(n_ct)]
    c_id = lax.broadcasted_iota(jnp.int32, (nc, QB), 0)
    c_q = lax.broadcasted_iota(jnp.int32, (nc, QB), 1)
    ok_c = lanes4(jnp.where(c_id * NSA_CMP_STRIDE + (NSA_CMP_LEN - 1) <= s0 + c_q, 1.0, 0.0))
    blk = lax.broadcasted_iota(jnp.int32, (LANES, QB), 0)
    t_tok = s0 + lax.broadcasted_iota(jnp.int32, (LANES, QB), 1)
    tb_blk = t_tok // NSA_SEL_LEN
    forced = jnp.where(blk == 0, 1.0, 0.0) + jnp.where(blk == tb_blk, 1.0, 0.0) \
        + jnp.where(blk == tb_blk - 1, 1.0, 0.0)
    blkf = blk.astype(F32)
    out, score = [], []
    for g in groups:
        s = sc[g] + jnp.concatenate([d[g] for d in dl], axis=0)
        s = jnp.where(ok_c > 0.5, s, NSA_NEG)
        e = jnp.exp2(s - jnp.max(s, axis=0, keepdims=True)) * ok_c
        p = e * (1.0 / jnp.maximum(jnp.sum(e, axis=0, keepdims=True), 1e-30))
        out.append(gate[g][0:1, :] * jnp.dot(vc_ref[0, g], p.astype(BF16), preferred_element_type=F32))
        ps = p[:, 0:QB]
        for hg in range(1, HG):
            ps = ps + p[:, hg * QB:(hg + 1) * QB]
        ps_hi, ps_lo = _split_bf16(ps)
        imp = (jnp.dot(agg_ref[...], ps_hi, preferred_element_type=F32)
               + jnp.dot(agg_ref[...], ps_lo, preferred_element_type=F32))
        sco = jnp.where(forced > 0.5, 1e9, jnp.where(blk * NSA_SEL_LEN <= t_tok, imp, -1e9))
        score.append(jnp.where(blk < n_sel, sco, -jnp.inf))
    sel = [jnp.zeros((LANES, QB), F32) for _ in groups]
    for _ in range(sel_k):
        for g in groups:
            mx = jnp.max(score[g], axis=0, keepdims=True)
            first = jnp.min(jnp.where(score[g] == mx, blkf, float(LANES)), axis=0, keepdims=True)
            hit = blkf == first
            sel[g] = jnp.where(hit, 1.0, sel[g])
            score[g] = jnp.where(hit, -jnp.inf, score[g])
    q2 = []
    for g in groups:
        sel_q = jnp.transpose(jnp.where(sel[g] > 0.5, 0.0, NSA_NEG))
        q2.append(jnp.concatenate([q[g], jnp.concatenate([sel_q] * HG, axis=0).astype(BF16)], axis=1))

    SEL, WIN = 0, 1

    def reset():
        m_ref[...] = jnp.full_like(m_ref, NSA_NEG)
        l_ref[...] = jnp.zeros_like(l_ref)
        acc_ref[...] = jnp.zeros_like(acc_ref)

    def tile_step(br, g, k, s, v_t):
        m_old = m_ref[br, g, k]
        m_new = jnp.maximum(m_old, jnp.max(s, axis=0, keepdims=True))
        alpha = jnp.exp2(m_old - m_new)
        p = jnp.exp2(s - m_new)
        l_ref[br, g, k] = l_ref[br, g, k] * alpha + jnp.sum(p, axis=0, keepdims=True)
        acc_ref[br, g, k] = (acc_ref[br, g, k] * alpha
                             + jnp.dot(v_t, p.astype(BF16), preferred_element_type=F32))
        m_ref[br, g, k] = m_new

    def finish(br, g):
        m = jnp.maximum(m_ref[br, g, 0], m_ref[br, g, 1])
        w0, w1 = jnp.exp2(m_ref[br, g, 0] - m), jnp.exp2(m_ref[br, g, 1] - m)
        return ((acc_ref[br, g, 0] * w0 + acc_ref[br, g, 1] * w1)
                * (1.0 / (l_ref[br, g, 0] * w0 + l_ref[br, g, 1] * w1)))

    def emit():
        heads_out = []
        for g in groups:
            o_t = out[g] + gate[g][1:2, :] * finish(SEL, g) + gate[g][2:3, :] * finish(WIN, g)
            heads_out += [jnp.transpose(o_t[:, hg * QB:(hg + 1) * QB]) for hg in range(HG)]
        o_ref[0] = jnp.concatenate(heads_out, axis=1).astype(o_ref.dtype)

    key_r = lax.broadcasted_iota(jnp.int32, (NSA_KT, QB), 0)
    key_q = lax.broadcasted_iota(jnp.int32, (NSA_KT, QB), 1)
    w_r = lax.broadcasted_iota(jnp.int32, (LANES, QB), 0)
    w_q = lax.broadcasted_iota(jnp.int32, (LANES, QB), 1)
    in_window = lanes4(jnp.where(w_r > w_q, 0.0, NSA_NEG))
    causal_w = lanes4(jnp.where(w_r <= w_q, 0.0, NSA_NEG))

    def sel_at(k0, n):
        k0 = pl.multiple_of(k0, LANES)
        return [(_dot_nt(ks_ref[0, g, pl.ds(k0, n), :], q2[g]), vs_ref[0, g, :, pl.ds(k0, n)])
                for g in groups]

    def win_at(k0, n):
        k0 = pl.multiple_of(k0, LANES)
        return [(_dot_nt(kw_ref[0, g, pl.ds(k0, n), :], q[g]), vw_ref[0, g, :, pl.ds(k0, n)])
                for g in groups]

    def delta_at(k0, n):
        return delta_t(kp_ref[pl.ds(pl.multiple_of(k0, LANES), n), :])

    def pair_scores(p, buf):
        for t in range(2):
            sc_t = sel_at((2 * p + t) * NSA_KT, NSA_KT)
            for g in groups:
                buf[g, t] = sc_t[g][0]

    def pair_softmax(p, buf):
        for t in range(2):
            k0 = pl.multiple_of((2 * p + t) * NSA_KT, NSA_KT)
            for g in groups:
                tile_step(SEL, g, t, buf[g, t], vs_ref[0, g, :, pl.ds(k0, NSA_KT)])

    def run_pairs(n_pairs):
        last = jnp.maximum(n_pairs - 1, 0)
        pair_scores(0, sa_ref)

        def two_pairs(j, carry):
            pair_scores(2 * j + 1, sb_ref)
            pair_softmax(2 * j, sa_ref)
            pair_scores(jnp.minimum(2 * j + 2, last), sa_ref)
            pair_softmax(2 * j + 1, sb_ref)
            return carry

        lax.fori_loop(0, n_pairs // 2, two_pairs, 0)

        @pl.when(n_pairs % 2 == 1)
        def _():
            pair_softmax(n_pairs - 1, sa_ref)

    @pl.when(std[qi] != 0)
    def _():
        reset()
        nb = (qi - 1) // 2
        run_pairs(nb // 2)

        @pl.when(nb % 2 == 1)
        def _():
            a = sel_at((nb - 1) * NSA_KT, NSA_KT)
            for g in groups:
                tile_step(SEL, g, 0, *a[g])

        @pl.when(qi % 2 == 0)
        def _():
            a = sel_at(nb * NSA_KT, LANES)
            for g in groups:
                tile_step(SEL, g, 1, *a[g])

        half = NSA_WINDOW // 2
        zeros = jnp.zeros((LANES, HG * QB), F32)
        sel_a, sel_b = sel_at(s0 - LANES, LANES), sel_at(s0, LANES)
        win_a, win_b, win_d = win_at(s0 - NSA_WINDOW, half), win_at(s0 - half, half), win_at(s0, LANES)
        d_prev, d_diag = delta_at(s0 - LANES, LANES), delta_at(s0, LANES)
        for g in groups:
            tile_step(SEL, g, 0, sel_a[g][0] + d_prev[g], sel_a[g][1])
        for g in groups:
            tile_step(WIN, g, 0, win_a[g][0] + jnp.concatenate([in_window, zeros], axis=0), win_a[g][1])
        for g in groups:
            tile_step(SEL, g, 1, sel_b[g][0] + d_diag[g] + causal_w, sel_b[g][1])
        for g in groups:
            tile_step(WIN, g, 1, win_b[g][0] + jnp.concatenate([zeros, d_prev[g]], axis=0), win_b[g][1])
        for g in groups:
            tile_step(WIN, g, 0, win_d[g][0] + d_diag[g] + causal_w, win_d[g][1])
        emit()

    @pl.when(std[qi] == 0)
    def _():
        reset()
        n_past = (s0 + QB - 1) // NSA_KT
        n_pairs = nfast[qi] // 2
        run_pairs(n_pairs)

        def slow_body(kt, carry):
            a = sel_at(kt * NSA_KT, NSA_KT)
            d = delta_at(kt * NSA_KT, NSA_KT)
            for g in groups:
                tile_step(SEL, g, 1, a[g][0] + d[g], a[g][1])
            return carry

        lax.fori_loop(2 * n_pairs, n_past, slow_body, 0)
        k0 = n_past * NSA_KT
        a = sel_at(k0, NSA_KT)
        d = delta_at(k0, NSA_KT)
        causal = lanes4(jnp.where(k0 + key_r <= s0 + key_q, 0.0, NSA_NEG))
        for g in groups:
            tile_step(SEL, g, 0, a[g][0] + d[g] + causal, a[g][1])

        for j in range(NSA_WTILES):
            start = s0 - NSA_WINDOW + j * LANES

            @pl.when(start >= 0)
            def _():
                sw = win_at(start, LANES)
                d = maybe_delta(near_w[qi * NSA_WTILES + j],
                                kp_ref[pl.ds(pl.multiple_of(start, LANES), LANES), :])
                for g in groups:
                    s = sw[g][0] + d[g]
                    if j == 0:
                        s = s + in_window
                    elif j == NSA_WTILES - 1:
                        s = s + causal_w
                    tile_step(WIN, g, j % 2, s, sw[g][1])

        emit()


def _pos_digits(p, key_side):
    d0, d1, d2 = (p & 127).astype(F32), ((p >> 7) & 127).astype(F32), (p >> 14).astype(F32)
    one = jnp.ones_like(d0)
    if key_side:
        cols = [one, one, one, -d2, -d1, -d0]
    else:
        cols = [16384.0 * d2, 128.0 * d1, d0, 16384.0 * one, 128.0 * one, one]
    out = jnp.stack(cols, axis=-1)
    return jnp.pad(out, ((0, 0), (0, LANES - out.shape[-1]))).astype(BF16)


def nsa_mixer(q, k_c, v_c, k_s, v_s, k_w, v_w, gate_logits, positions,
              ck_pe, ck_w1, ck_w2, cv_pe, cv_w1, cv_w2, rel_bias):
    q, k_c, v_c, k_s, v_s, k_w, v_w = lax.optimization_barrier((q, k_c, v_c, k_s, v_s, k_w, v_w))
    bs, s, _ = q.shape
    G, HG, DH, QB = NSA_KV_GROUPS, NSA_HPG, NSA_DH, NSA_QBLOCK
    nqt = s // QB
    nc = s // NSA_CMP_STRIDE
    n_cmp = (s - NSA_CMP_LEN) // NSA_CMP_STRIDE + 1
    n_sel = s // NSA_SEL_LEN
    sel_k = min(NSA_SEL_TOPK, n_sel)
    n_kt = s // NSA_KT
    n_ct = nc // LANES
    assert s % (LANES * NSA_CMP_STRIDE) == 0 and n_sel <= LANES and NSA_KT == 2 * QB

    def by_group(t):
        return t.reshape(bs, s, G, DH).transpose(0, 2, 1, 3).astype(F32)

    def by_group_t(t):
        return t.reshape(bs, s, G, DH).transpose(0, 2, 3, 1).astype(BF16)

    ones2 = jnp.zeros((LANES - DH,), F32).at[:2].set(1.0)

    def keys_aug(t, blocks=False):
        parts = [t, jnp.broadcast_to(ones2, t.shape[:-1] + (LANES - DH,))]
        if blocks:
            onehot = (jnp.arange(s)[:, None] // NSA_SEL_LEN == jnp.arange(LANES)[None, :]).astype(F32)
            parts.append(jnp.broadcast_to(onehot, t.shape[:-2] + (s, LANES)))
        return jnp.concatenate(parts, axis=-1).astype(BF16)

    chunks = jnp.stack([by_group(k_c), by_group(v_c)]).reshape(2, bs, G, nc, NSA_CMP_STRIDE * DH)
    pe = jnp.stack([ck_pe, cv_pe]).reshape(2, 1, NSA_CMP_LEN * DH)
    cmp = nsa_compress(chunks.astype(BF16), jnp.broadcast_to(pe, (2, 8, NSA_CMP_LEN * DH)).astype(BF16),
                       jnp.stack([ck_w1, cv_w1]).astype(BF16), jnp.stack([ck_w2, cv_w2]).astype(BF16))
    kc, vc_t = keys_aug(cmp[0]), cmp[1].transpose(0, 1, 3, 2).astype(BF16)

    log2e = math.log2(math.e)
    rel_bias = rel_bias.astype(F32) * log2e
    far = rel_bias[REL_BUCKETS - 1]
    far_hi = far.astype(BF16).astype(F32)
    extra = jnp.zeros((NSA_HEADS, LANES - DH), F32).at[:, 0].set(far_hi).at[:, 1].set(far - far_hi)
    qh = (q.astype(F32) * (DH ** -0.5 * log2e)).reshape(bs, nqt, QB, G, HG, DH).transpose(0, 3, 1, 4, 2, 5)
    ex = jnp.broadcast_to(extra.reshape(1, G, 1, HG, 1, LANES - DH), (bs, G, nqt, HG, QB, LANES - DH))
    qs = jnp.concatenate([qh, ex], axis=-1).astype(BF16).reshape(bs, G, nqt, HG * QB, LANES)
    gl = gate_logits.astype(F32).reshape(bs, nqt, QB, G, HG, 3).transpose(0, 3, 1, 5, 4, 2)
    gl = jnp.pad(gl.reshape(bs, G, nqt, 3, HG * QB), ((0, 0),) * 3 + ((0, 5), (0, 0)))

    td = (rel_bias[_bucket_table()] - rel_bias[REL_BUCKETS - 1][None, :]).T.astype(F32)
    cmp_end = jnp.minimum(jnp.arange(nc) * NSA_CMP_STRIDE + NSA_CMP_LEN - 1, s - 1)
    cpos = positions[cmp_end]
    qmin = positions.reshape(nqt, QB).min(axis=1)
    kmax = positions.reshape(nqt, QB).max(axis=1)
    near_s = (qmin[:, None] - positions.reshape(n_kt, NSA_KT).max(axis=1)[None, :]) < REL_MAX_DIST
    n_past = (jnp.arange(nqt) * QB + QB - 1) // NSA_KT
    n_fast = jnp.minimum(jnp.argmax(jnp.concatenate([near_s, jnp.ones((nqt, 1), bool)], axis=1), axis=1),
                         n_past)
    wt = jnp.clip(jnp.arange(nqt)[:, None] - (NSA_WTILES - 1) + jnp.arange(NSA_WTILES)[None, :], 0, nqt - 1)
    near_w = (qmin[:, None] - kmax[wt]) < REL_MAX_DIST
    near_c = (qmin[:, None] - cpos.reshape(n_ct, LANES).max(axis=1)[None, :]) < REL_MAX_DIST
    before = lax.cummax(kmax)[jnp.maximum(jnp.arange(nqt) - 2, 0)]
    std = (jnp.arange(nqt) >= NSA_WTILES - 1) & (qmin - before >= REL_MAX_DIST)

    cmp_start = np.arange(nc) * NSA_CMP_STRIDE
    sel_start = np.arange(LANES) * NSA_SEL_LEN
    agg_t = ((cmp_start[None, :] <= sel_start[:, None] + NSA_SEL_LEN - 1)
             & (cmp_start[None, :] + NSA_CMP_LEN - 1 >= sel_start[:, None])
             & (np.arange(nc)[None, :] < n_cmp) & (np.arange(LANES)[:, None] < n_sel))

    cols = HG * QB
    full = lambda shape: pl.BlockSpec(shape, lambda b, i, *_: (0,) * len(shape))
    per_b = lambda n, w: pl.BlockSpec((1, G, n, w), lambda b, i, *_: (b, 0, 0, 0))
    per_tile = lambda n, w: pl.BlockSpec((1, G, 1, n, w), lambda b, i, *_: (b, 0, i, 0, 0))
    grid_spec = pltpu.PrefetchScalarGridSpec(
        num_scalar_prefetch=4,
        grid=(bs, nqt),
        in_specs=[per_tile(cols, LANES), per_tile(8, cols),
                  pl.BlockSpec((QB, LANES), lambda b, i, *_: (i, 0)),
                  full((s, LANES)), full((nc, LANES)),
                  per_b(s, 2 * LANES), per_b(DH, s), per_b(s, LANES), per_b(DH, s),
                  per_b(nc, LANES), per_b(DH, nc),
                  full((LANES, nc)), full((NSA_HEADS, LANES))],
        out_specs=pl.BlockSpec((1, QB, NSA_Q), lambda b, i, *_: (b, i, 0)),
        scratch_shapes=[pltpu.VMEM((2, G, 2, 1, cols), F32), pltpu.VMEM((2, G, 2, 1, cols), F32),
                        pltpu.VMEM((2, G, 2, DH, cols), F32)]
        + [pltpu.VMEM((G, 2, NSA_KT, cols), F32)] * 2,
    )
    return pl.pallas_call(
        functools.partial(_nsa_body, n_sel=n_sel, sel_k=sel_k),
        grid_spec=grid_spec,
        out_shape=jax.ShapeDtypeStruct((bs, s, NSA_Q), BF16),
        compiler_params=_params("parallel", "arbitrary"),
        name="nsa_attention",
    )(n_fast.astype(jnp.int32), std.astype(jnp.int32), near_w.reshape(-1).astype(jnp.int32),
      near_c.reshape(-1).astype(jnp.int32),
      qs, gl, _pos_digits(positions, False), _pos_digits(positions, True), _pos_digits(cpos, True),
      keys_aug(by_group(k_s), blocks=True), by_group_t(v_s), keys_aug(by_group(k_w)), by_group_t(v_w),
      kc, vc_t, jnp.asarray(agg_t, BF16), td)


SGU_TC = 512


def _sgu_body(uv_ref, lg_ref, lb_ref, w_ref, b_ref, o_ref):
    uv = jax.nn.gelu(uv_ref[...].astype(F32))
    u, v = uv[:, :SGU_WIDTH], uv[:, SGU_WIDTH:]
    mu = jnp.mean(v, axis=-1, keepdims=True)
    vc = v - mu
    var = jnp.mean(vc * vc, axis=-1, keepdims=True)
    vn = (vc * lax.rsqrt(var + RMS_EPS) * lg_ref[...] + lb_ref[...]).astype(BF16)
    gw = SGU_WIDTH // SGU_GROUPS
    for c in range(SGU_TC // SGU_CHUNK):
        rows = slice(c * SGU_CHUNK, (c + 1) * SGU_CHUNK)
        for g in range(SGU_GROUPS):
            cols = slice(g * gw, (g + 1) * gw)
            mix = jnp.dot(w_ref[g], vn[rows, cols], preferred_element_type=F32) + b_ref[:, g:g + 1]
            o_ref[rows, cols] = (u[rows, cols] * mix).astype(o_ref.dtype)


def sgu_mixer(proj, ln_g, ln_b, w_s, b_s):
    t = proj.shape[0]
    assert t % SGU_TC == 0 and OFF_SGU % (2 * SGU_WIDTH) == 0
    fixed = lambda i: (0, 0)
    return pl.pallas_call(
        _sgu_body,
        grid=(t // SGU_TC,),
        in_specs=[pl.BlockSpec((SGU_TC, 2 * SGU_WIDTH), lambda i: (i, OFF_SGU // (2 * SGU_WIDTH))),
                  pl.BlockSpec((1, SGU_WIDTH), fixed), pl.BlockSpec((1, SGU_WIDTH), fixed),
                  pl.BlockSpec((SGU_GROUPS, SGU_CHUNK, SGU_CHUNK), lambda i: (0, 0, 0)),
                  pl.BlockSpec((SGU_CHUNK, SGU_GROUPS), fixed)],
        out_specs=pl.BlockSpec((SGU_TC, SGU_WIDTH), lambda i: (i, 0)),
        out_shape=jax.ShapeDtypeStruct((t, SGU_WIDTH), BF16),
        compiler_params=_params("parallel"),
        name="sgu_gate",
    )(proj, ln_g.reshape(1, SGU_WIDTH), ln_b.reshape(1, SGU_WIDTH), jnp.tril(w_s).astype(BF16), b_s.T)


def _permute_w_in(w):
    sizes = ([SSM_WIDTH, GDN_QKV, GDN_HEADS, GDN_HEADS, GDN_HEADS * GDN_DV, NSA_Q]
             + [NSA_KV] * 6 + [3 * NSA_HEADS, 2 * SGU_WIDTH, N_BRANCH * D_MODEL])
    cuts = [int(c) for c in np.cumsum(sizes)[:-1]]
    (w_ssm, w_qkv, w_a, w_b, w_z, w_nq, w_kc, w_vc, w_ks, w_vs, w_kw, w_vw,
     w_ng, w_sgu, w_gate) = jnp.split(w, cuts, axis=-1)
    pad = jnp.zeros((w.shape[0], PROJ_COLS - OFF_SMALL - SMALL_USED), w.dtype)
    return jnp.concatenate([w_qkv, w_ssm, w_z, w_nq, w_sgu, w_gate, w_kc, w_vc, w_ks, w_vs,
                            w_kw, w_vw, w_a, w_b, w_ng, pad], axis=-1)


def kernel(x, positions, norm_mix, norm_ffn, norm_final, w_in, ssm_a_re, ssm_a_im, ssm_log_dt, ssm_b_re, ssm_b_im, ssm_c_re, ssm_c_im, ssm_d, ssm_glu_w, gdn_conv_w, gdn_a_log, gdn_dt_bias, gdn_norm, nsa_cmp_k_pe, nsa_cmp_k_w1, nsa_cmp_k_w2, nsa_cmp_v_pe, nsa_cmp_v_w1, nsa_cmp_v_w2, rel_bias, sgu_ln_g, sgu_ln_b, sgu_w, sgu_b, w_br_ssm, w_br_gdn, w_br_nsa, w_br_sgu, w_out, ffn_w_gate, ffn_w_up, ffn_w_down, moe_router, moe_w_gate, moe_w_up, moe_w_down):
    bs, s, d = x.shape
    t = bs * s
    assert DEPTH == 2
    xf = x.reshape(t, d)
    for layer in range(DEPTH):
        proj = in_proj(xf, norm_mix[layer], _permute_w_in(w_in[layer]))
        p3 = proj.reshape(bs, s, PROJ_COLS)

        def seg(off, width):
            return p3[..., off:off + width]

        y_ssm = ssm_mixer(proj, bs, ssm_a_re[layer], ssm_a_im[layer], ssm_log_dt[layer],
                          ssm_b_re[layer], ssm_b_im[layer], ssm_c_re[layer], ssm_c_im[layer],
                          ssm_d[layer], ssm_glu_w[layer])
        y_gdn = gdn_mixer(proj, bs, gdn_conv_w[layer], gdn_a_log[layer], gdn_dt_bias[layer],
                          gdn_norm[layer])
        kvs = [seg(OFF_KV + i * NSA_KV, NSA_KV) for i in range(6)]
        y_nsa = nsa_mixer(seg(OFF_NQ, NSA_Q), *kvs, seg(OFF_SMALL + 2 * GDN_HEADS, 3 * NSA_HEADS),
                          positions, nsa_cmp_k_pe[layer], nsa_cmp_k_w1[layer], nsa_cmp_k_w2[layer],
                          nsa_cmp_v_pe[layer], nsa_cmp_v_w1[layer], nsa_cmp_v_w2[layer], rel_bias)
        y_sgu = sgu_mixer(proj, sgu_ln_g[layer], sgu_ln_b[layer], sgu_w[layer], sgu_b[layer])
        ys = [y_ssm, y_gdn, y_nsa.reshape(t, -1), y_sgu]
        ws = [w[layer].astype(BF16) for w in (w_br_ssm, w_br_gdn, w_br_nsa, w_br_sgu)]
        xf = merge(xf, proj, ys, ws, w_out[layer].astype(BF16))
        i = layer // 2
        if layer % 2 == 0:
            xf = dense_ffn(xf, norm_ffn[layer], ffn_w_gate[i], ffn_w_up[i], ffn_w_down[i])
        else:
            xf = moe_layer(xf, norm_ffn[layer], moe_router[i], moe_w_gate[i], moe_w_up[i],
                           moe_w_down[i], norm_final)
    return xf.reshape(bs, s, d)
```

```python
import functools
import math

import jax
import jax.numpy as jnp
from jax import lax
import numpy as np
from jax.experimental import pallas as pl
from jax.experimental.pallas import tpu as pltpu

F32 = jnp.float32
BF16 = jnp.bfloat16

D_MODEL = 1024
DEPTH = 2
SSM_WIDTH = D_MODEL // 2
SSM_GROUP = 16
SSM_GROUPS = SSM_WIDTH // SSM_GROUP
SSM_STATE = 64
GDN_HEADS = 4
GDN_DK = 128
GDN_DV = 128
GDN_CONV = 4
GDN_CHUNK = 64
NSA_HEADS = 8
NSA_KV_GROUPS = 2
NSA_HPG = NSA_HEADS // NSA_KV_GROUPS
NSA_DH = 64
NSA_CMP_LEN = 32
NSA_CMP_STRIDE = 16
NSA_CMP_HIDDEN = 128
NSA_SEL_LEN = 64
NSA_SEL_TOPK = 16
NSA_WINDOW = 512
NSA_QBLOCK = 128
SGU_WIDTH = D_MODEL // 2
SGU_GROUPS = 4
SGU_CHUNK = 128
REL_BUCKETS = 32
REL_MAX_DIST = 128
FFN_DENSE = 2816
N_EXPERTS = 8
TOP_K = 2
FFN_EXPERT = 3584
N_BRANCH = 4
RMS_EPS = 1e-6
GDN_QKV = GDN_HEADS * (2 * GDN_DK + GDN_DV)
NSA_Q = NSA_HEADS * NSA_DH
NSA_KV = NSA_KV_GROUPS * NSA_DH

LANES = 128
VMEM_LIMIT = 48 * 1024 * 1024
MASKED = -1e30

OFF_QKV = 0
OFF_SSM = OFF_QKV + GDN_QKV
OFF_Z = OFF_SSM + SSM_WIDTH
OFF_NQ = OFF_Z + GDN_HEADS * GDN_DV
OFF_SGU = OFF_NQ + NSA_Q
OFF_GATE = OFF_SGU + 2 * SGU_WIDTH
OFF_KV = OFF_GATE + N_BRANCH * D_MODEL
OFF_SMALL = OFF_KV + 6 * NSA_KV
SMALL_USED = 2 * GDN_HEADS + 3 * NSA_HEADS
PROJ_COLS = 9216


def _params(*sem):
    return pltpu.CompilerParams(dimension_semantics=sem, vmem_limit_bytes=VMEM_LIMIT)


def _rms(x, g):
    return x * lax.rsqrt(jnp.mean(x * x, axis=-1, keepdims=True) + RMS_EPS) * g


def _dot_nt(a, b):
    return lax.dot_general(a, b, (((1,), (1,)), ((), ())), preferred_element_type=F32)


def _dot_tn(a, b):
    return lax.dot_general(a, b, (((0,), (0,)), ((), ())), preferred_element_type=F32)


def _split_bf16(x):
    hi = x.astype(BF16)
    return hi, (x - hi.astype(F32)).astype(BF16)


def _in_proj_body(x_ref, g_ref, w_ref, o_ref, h_ref):
    @pl.when(pl.program_id(1) == 0)
    def _():
        h_ref[...] = _rms(x_ref[...], g_ref[...]).astype(BF16)

    o_ref[...] = jnp.dot(h_ref[...], w_ref[...], preferred_element_type=F32).astype(o_ref.dtype)


def in_proj(x, g, w, tm=2048, tn=512):
    t, d = x.shape
    n = w.shape[1]
    return pl.pallas_call(
        _in_proj_body,
        grid=(t // tm, n // tn),
        in_specs=[pl.BlockSpec((tm, d), lambda i, j: (i, 0)),
                  pl.BlockSpec((1, d), lambda i, j: (0, 0)),
                  pl.BlockSpec((d, tn), lambda i, j: (0, j))],
        out_specs=pl.BlockSpec((tm, tn), lambda i, j: (i, j)),
        out_shape=jax.ShapeDtypeStruct((t, n), BF16),
        scratch_shapes=[pltpu.VMEM((tm, d), BF16)],
        compiler_params=_params("parallel", "arbitrary"),
        name="in_proj",
    )(x, g.reshape(1, d), w)


def _merge_body(x_ref, gate_ref, ya_ref, yb_ref, yc_ref, yd_ref,
                wa_ref, wb_ref, wc_ref, wd_ref, wo_ref, o_ref):
    def branch(k, y_ref, w_ref):
        p = jnp.dot(y_ref[...], w_ref[...], preferred_element_type=F32)
        return jax.nn.sigmoid(gate_ref[:, k * D_MODEL:(k + 1) * D_MODEL].astype(F32)) * p

    merged = (branch(0, ya_ref, wa_ref) + branch(1, yb_ref, wb_ref)
              + branch(2, yc_ref, wc_ref) + branch(3, yd_ref, wd_ref))
    o_ref[...] = x_ref[...] + jnp.dot(merged.astype(BF16), wo_ref[...],
                                      preferred_element_type=F32)


def merge(x, proj, ys, ws, w_out, tm=512):
    t, d = x.shape
    gate_blk = OFF_GATE // (N_BRANCH * D_MODEL)
    row = lambda i: (i, 0)
    fixed = lambda i: (0, 0)
    in_specs = [pl.BlockSpec((tm, d), row),
                pl.BlockSpec((tm, N_BRANCH * D_MODEL), lambda i: (i, gate_blk))]
    in_specs += [pl.BlockSpec((tm, y.shape[1]), row) for y in ys]
    in_specs += [pl.BlockSpec(w.shape, fixed) for w in ws]
    in_specs += [pl.BlockSpec(w_out.shape, fixed)]
    return pl.pallas_call(
        _merge_body,
        grid=(t // tm,),
        in_specs=in_specs,
        out_specs=pl.BlockSpec((tm, d), row),
        out_shape=jax.ShapeDtypeStruct((t, d), F32),
        compiler_params=_params("parallel"),
        name="merge",
    )(x, proj, *ys, *ws, w_out)


def _ffn_body(x_ref, g_ref, wg_ref, wu_ref, wd_ref, o_ref, h_ref, acc_ref):
    f = pl.program_id(1)

    @pl.when(f == 0)
    def _():
        h_ref[...] = _rms(x_ref[...], g_ref[...]).astype(BF16)
        acc_ref[...] = x_ref[...]

    h = h_ref[...]
    a = jnp.dot(h, wg_ref[...].astype(BF16), preferred_element_type=F32)
    u = jnp.dot(h, wu_ref[...].astype(BF16), preferred_element_type=F32)
    act = (a * jax.nn.sigmoid(a) * u).astype(BF16)
    acc_ref[...] += jnp.dot(act, wd_ref[...].astype(BF16), preferred_element_type=F32)

    @pl.when(f == pl.num_programs(1) - 1)
    def _():
        o_ref[...] = acc_ref[...]


def dense_ffn(x, g, wg, wu, wd, tm=1024, tf=256):
    t, d = x.shape
    nf = wg.shape[1]
    return pl.pallas_call(
        _ffn_body,
        grid=(t // tm, nf // tf),
        in_specs=[pl.BlockSpec((tm, d), lambda i, f: (i, 0)),
                  pl.BlockSpec((1, d), lambda i, f: (0, 0)),
                  pl.BlockSpec((d, tf), lambda i, f: (0, f)),
                  pl.BlockSpec((d, tf), lambda i, f: (0, f)),
                  pl.BlockSpec((tf, d), lambda i, f: (f, 0))],
        out_specs=pl.BlockSpec((tm, d), lambda i, f: (i, 0)),
        out_shape=jax.ShapeDtypeStruct((t, d), F32),
        scratch_shapes=[pltpu.VMEM((tm, d), BF16), pltpu.VMEM((tm, d), F32)],
        compiler_params=_params("parallel", "arbitrary"),
        name="dense_ffn",
    )(x, g.reshape(1, d), wg, wu, wd)


def _route_body(x_ref, g_ref, rhi_ref, rlo_ref, h_ref, idx_ref, wt_ref):
    h = _rms(x_ref[...], g_ref[...])
    hi = h.astype(BF16)
    lo = (h - hi.astype(F32)).astype(BF16)
    h_ref[...] = h
    logits = (jnp.dot(hi, rhi_ref[...], preferred_element_type=F32)
              + jnp.dot(hi, rlo_ref[...], preferred_element_type=F32)
              + jnp.dot(lo, rhi_ref[...], preferred_element_type=F32))
    col = lax.broadcasted_iota(jnp.int32, logits.shape, 1).astype(F32)
    neg = jnp.float32(-jnp.inf)
    lg = jnp.where(col < N_EXPERTS, logits, neg)
    m1 = jnp.max(lg, axis=-1, keepdims=True)
    i1 = jnp.min(jnp.where(lg == m1, col, float(LANES)), axis=-1, keepdims=True)
    lg2 = jnp.where(col == i1, neg, lg)
    m2 = jnp.max(lg2, axis=-1, keepdims=True)
    i2 = jnp.min(jnp.where(lg2 == m2, col, float(LANES)), axis=-1, keepdims=True)
    e = jnp.exp(m2 - m1)
    w1 = 1.0 / (1.0 + e)
    w2 = e / (1.0 + e)
    idx_ref[...] = jnp.where(col == 0, i1, jnp.where(col == 1, i2, 0.0)).astype(jnp.int32)
    wt_ref[...] = jnp.where(col == 0, w1, jnp.where(col == 1, w2, 0.0))


def moe_route(x, g, r_hi, r_lo, tm=512):
    t, d = x.shape
    row = lambda i: (i, 0)
    fixed = lambda i: (0, 0)
    return pl.pallas_call(
        _route_body,
        grid=(t // tm,),
        in_specs=[pl.BlockSpec((tm, d), row), pl.BlockSpec((1, d), fixed),
                  pl.BlockSpec((d, LANES), fixed), pl.BlockSpec((d, LANES), fixed)],
        out_specs=[pl.BlockSpec((tm, d), row), pl.BlockSpec((tm, LANES), row),
                   pl.BlockSpec((tm, LANES), row)],
        out_shape=[jax.ShapeDtypeStruct((t, d), F32),
                   jax.ShapeDtypeStruct((t, LANES), jnp.int32),
                   jax.ShapeDtypeStruct((t, LANES), F32)],
        compiler_params=_params("parallel"),
        name="moe_route",
    )(x, g.reshape(1, d), r_hi, r_lo)


def _experts_body(te_ref, nu_ref, src_ref, h_hbm, wg_ref, wu_ref, wd_ref, o_ref,
                  xbuf_ref, xs_ref, acc_ref, sem, *, rows_per_step):
    i = pl.program_id(0)
    f = pl.program_id(1)
    tm = xs_ref.shape[0]
    n_rows = xbuf_ref.shape[1]
    n_used = nu_ref[0]
    used = i < n_used
    has_next = i + 1 < n_used
    slot = i % 2

    def start_row(tile, slot_, r):
        tok = src_ref[tile * tm + r]
        pltpu.make_async_copy(h_hbm.at[pl.ds(tok, 1), :], xbuf_ref.at[slot_, pl.ds(r, 1), :],
                              sem.at[slot_]).start()

    @pl.when(f == 0)
    def _():
        acc_ref[...] = jnp.zeros_like(acc_ref)

        @pl.when(used)
        def _():
            @pl.when(i == 0)
            def _():
                def row(r, carry):
                    start_row(0, 0, r)
                    return carry

                lax.fori_loop(0, n_rows, row, 0, unroll=8)

            pltpu.make_async_copy(h_hbm.at[pl.ds(0, n_rows), :], xbuf_ref.at[slot], sem.at[slot]).wait()
            xs_ref[...] = xbuf_ref[slot, 0:tm, :].astype(BF16)

    def ffn_step():
        h = xs_ref[...]
        a = jnp.dot(h, wg_ref[0].astype(BF16), preferred_element_type=F32)
        u = jnp.dot(h, wu_ref[0].astype(BF16), preferred_element_type=F32)
        act = (a * jax.nn.sigmoid(a) * u).astype(BF16)
        acc_ref[...] += jnp.dot(act, wd_ref[0].astype(BF16), preferred_element_type=F32)

    @pl.when(used & has_next)
    def _():
        for j in range(rows_per_step):
            start_row(i + 1, 1 - slot, f * rows_per_step + j)
        ffn_step()

    @pl.when(used & jnp.logical_not(has_next))
    def _():
        ffn_step()

    @pl.when(f == pl.num_programs(1) - 1)
    def _():
        o_ref[...] = acc_ref[...].astype(o_ref.dtype)


def moe_experts(tile_expert, n_used, src, h, wg, wu, wd, tm, tf=512):
    d = h.shape[1]
    nf = wg.shape[2] // tf
    rows_per_step = 8 * (-(-tm // (8 * nf)))
    n_rows = rows_per_step * nf
    n_tiles = src.shape[0] // tm
    src = jnp.pad(src, (0, n_rows - tm))
    grid_spec = pltpu.PrefetchScalarGridSpec(
        num_scalar_prefetch=3,
        grid=(n_tiles, nf),
        in_specs=[pl.BlockSpec(memory_space=pl.ANY),
                  pl.BlockSpec((1, d, tf), lambda i, f, te, *_: (te[i], 0, f)),
                  pl.BlockSpec((1, d, tf), lambda i, f, te, *_: (te[i], 0, f)),
                  pl.BlockSpec((1, tf, d), lambda i, f, te, *_: (te[i], f, 0))],
        out_specs=pl.BlockSpec((tm, d), lambda i, f, *_: (i, 0)),
        scratch_shapes=[pltpu.VMEM((2, n_rows, d), F32), pltpu.VMEM((tm, d), BF16),
                        pltpu.VMEM((tm, d), F32), pltpu.SemaphoreType.DMA((2,))],
    )
    return pl.pallas_call(
        functools.partial(_experts_body, rows_per_step=rows_per_step),
        grid_spec=grid_spec,
        out_shape=jax.ShapeDtypeStruct((n_tiles * tm, d), BF16),
        compiler_params=_params("arbitrary", "arbitrary"),
        name="moe_experts",
    )(tile_expert, n_used, src, h, wg, wu, wd)


def _combine_norm_body(x_ref, ya_ref, yb_ref, wt_ref, g_ref, o_ref):
    wt = wt_ref[...]
    y = wt[:, 0:1] * ya_ref[...].astype(F32) + wt[:, 1:2] * yb_ref[...].astype(F32)
    o_ref[...] = _rms(x_ref[...] + y, g_ref[...])


def combine_norm(x, ya, yb, wts, g, tm=1024):
    t, d = x.shape
    row = lambda i: (i, 0)
    return pl.pallas_call(
        _combine_norm_body,
        grid=(t // tm,),
        in_specs=[pl.BlockSpec((tm, d), row)] * 3 + [pl.BlockSpec((tm, LANES), row),
                                                     pl.BlockSpec((1, d), lambda i: (0, 0))],
        out_specs=pl.BlockSpec((tm, d), row),
        out_shape=jax.ShapeDtypeStruct((t, d), F32),
        compiler_params=_params("parallel"),
        name="combine_norm",
    )(x, ya, yb, wts, g.reshape(1, d))


def moe_layer(x, g_norm, router, wg, wu, wd, g_final, tm=1024):
    t, d = x.shape
    r = jnp.zeros((d, LANES), F32).at[:, :N_EXPERTS].set(router)
    r_hi = r.astype(BF16)
    r_lo = (r - r_hi.astype(F32)).astype(BF16)
    h, idx, wts = moe_route(x, g_norm, r_hi, r_lo)
    e_flat = jnp.concatenate([idx[:, 0], idx[:, 1]])
    onehot = (e_flat[:, None] == jnp.arange(N_EXPERTS)[None, :]).astype(jnp.int32)
    csum = jnp.cumsum(onehot, axis=0)
    rank = jnp.sum((csum - onehot) * onehot, axis=1)
    counts = csum[-1]
    padded = ((counts + tm - 1) // tm) * tm
    ends = jnp.cumsum(padded)
    starts = ends - padded
    dest = starts[e_flat] + rank
    n_tiles = (TOP_K * t) // tm + N_EXPERTS
    p = n_tiles * tm
    tok = jnp.concatenate([jnp.arange(t, dtype=jnp.int32)] * TOP_K)
    src = jnp.zeros((p,), jnp.int32).at[dest].set(tok)
    tile_expert = jnp.minimum(
        jnp.searchsorted(ends, jnp.arange(n_tiles, dtype=jnp.int32) * tm, side="right"),
        N_EXPERTS - 1).astype(jnp.int32)
    n_used = (ends[-1] // tm).astype(jnp.int32).reshape(1)
    ys = moe_experts(tile_expert, n_used, src, h, wg, wu, wd, tm)
    ya = jnp.take(ys, dest[:t], axis=0, mode="clip")
    yb = jnp.take(ys, dest[t:], axis=0, mode="clip")
    return combine_norm(x, ya, yb, wts, g_final)


SSM_TC = 512
SSM_SUB = 128
SSM_HALF = 256
SSM_NSTATE = SSM_GROUPS * SSM_STATE


def _ssm_body(u_ref, bw_ref, cw_ref, d_ref, glu_ref, ar_ref, ai_ref, pr_ref, pi_ref, o_ref,
              xr_ref, xi_ref, cr_ref, ci_ref):
    @pl.when(pl.program_id(1) == 0)
    def _():
        cr_ref[...] = jnp.zeros_like(cr_ref)
        ci_ref[...] = jnp.zeros_like(ci_ref)

    ub = u_ref[...]
    u = ub.astype(F32)
    nblk = SSM_WIDTH // SSM_HALF
    sblk = SSM_NSTATE // nblk
    for k in range(nblk):
        bu = jnp.dot(ub[:, k * SSM_HALF:(k + 1) * SSM_HALF], bw_ref[k], preferred_element_type=F32)
        xr_ref[:, k * sblk:(k + 1) * sblk] = bu[:, :sblk]
        xi_ref[:, k * sblk:(k + 1) * sblk] = bu[:, sblk:]

    row8 = lax.broadcasted_iota(jnp.int32, (SSM_SUB, LANES), 0) % 8
    n_grp = SSM_SUB // 8

    def strip(c, carry):
        col = pl.ds(pl.multiple_of(c * LANES, LANES), LANES)
        c_r = cr_ref[:, col]
        c_i = ci_ref[:, col]
        p_r = pr_ref[0:8, col]
        p_i = pi_ref[0:8, col]
        step_mul = [(jnp.where(row8 >= (1 << i), ar_ref[i:i + 1, col], 0.0),
                     jnp.where(row8 >= (1 << i), ai_ref[i:i + 1, col], 0.0)) for i in range(3)]
        for sub in range(SSM_TC // SSM_SUB):
            rows = slice(sub * SSM_SUB, (sub + 1) * SSM_SUB)
            xr = xr_ref[rows, col]
            xi = xi_ref[rows, col]
            for i in range(3):
                a_r, a_i = step_mul[i]
                sr, si = pltpu.roll(xr, 1 << i, 0), pltpu.roll(xi, 1 << i, 0)
                xr, xi = xr + a_r * sr - a_i * si, xi + a_r * si + a_i * sr
            out_r, out_i = [], []
            for r in range(n_grp):
                g_r = xr[8 * r:8 * r + 8] + p_r * c_r - p_i * c_i
                g_i = xi[8 * r:8 * r + 8] + p_r * c_i + p_i * c_r
                c_r, c_i = g_r[7:8, :], g_i[7:8, :]
                out_r.append(g_r)
                out_i.append(g_i)
            xr_ref[rows, col] = jnp.concatenate(out_r, axis=0)
            xi_ref[rows, col] = jnp.concatenate(out_i, axis=0)
        cr_ref[:, col] = c_r
        ci_ref[:, col] = c_i
        return carry

    lax.fori_loop(0, SSM_NSTATE // LANES, strip, 0)

    ys = []
    for k in range(nblk):
        st = jnp.concatenate([xr_ref[:, k * sblk:(k + 1) * sblk].astype(BF16),
                              xi_ref[:, k * sblk:(k + 1) * sblk].astype(BF16)], axis=1)
        ys.append(jnp.dot(st, cw_ref[k], preferred_element_type=F32))
    y = jax.nn.gelu(jnp.concatenate(ys, axis=1) + d_ref[...] * u)
    gated = y * jax.nn.sigmoid(jnp.dot(y.astype(BF16), glu_ref[...], preferred_element_type=F32))
    o_ref[...] = gated.astype(o_ref.dtype)


def ssm_mixer(proj, bs, a_re, a_im, log_dt, b_re, b_im, c_re, c_im, d_skip, glu_w):
    t = proj.shape[0]
    s = t // bs
    nt = s // SSM_TC
    assert s % SSM_TC == 0 and OFF_SSM % SSM_WIDTH == 0
    nblk = SSM_WIDTH // SSM_HALF
    gpb = SSM_GROUPS // nblk
    dt = jnp.exp(log_dt)[:, None]
    mag = jnp.exp(a_re * dt)
    abar_r, abar_i = mag * jnp.cos(a_im * dt), mag * jnp.sin(a_im * dt)
    nr, ni = abar_r - 1.0, abar_i
    den = a_re * a_re + a_im * a_im
    sr, si = (nr * a_re + ni * a_im) / den, (ni * a_re - nr * a_im) / den
    bbar_r = sr[..., None] * b_re - si[..., None] * b_im
    bbar_i = sr[..., None] * b_im + si[..., None] * b_re

    def powers(k):
        kk = k.astype(F32)[:, None, None]
        m = jnp.exp(kk * (a_re * dt))
        return ((m * jnp.cos(kk * (a_im * dt))).reshape(-1, SSM_NSTATE),
                (m * jnp.sin(kk * (a_im * dt))).reshape(-1, SSM_NSTATE))

    ar, ai = powers(2 ** jnp.arange(8))
    pr, pi = powers(jnp.arange(1, SSM_SUB + 1))
    eye = jnp.eye(gpb, dtype=F32)

    def in_block(w):
        return jnp.einsum('gnp,gh->gphn', w, eye).reshape(gpb * SSM_GROUP, gpb * SSM_STATE)

    def out_block(w):
        return jnp.einsum('gpn,gh->gnhp', w, eye).reshape(gpb * SSM_STATE, gpb * SSM_GROUP)

    bw = jnp.stack([jnp.concatenate([in_block(bbar_r[k * gpb:(k + 1) * gpb]),
                                     in_block(bbar_i[k * gpb:(k + 1) * gpb])], axis=1)
                    for k in range(nblk)]).astype(BF16)
    cw = jnp.stack([jnp.concatenate([out_block(c_re[k * gpb:(k + 1) * gpb]),
                                     -out_block(c_im[k * gpb:(k + 1) * gpb])], axis=0)
                    for k in range(nblk)]).astype(BF16)
    fixed2 = lambda b, i: (0, 0)
    fixed3 = lambda b, i: (0, 0, 0)
    return pl.pallas_call(
        _ssm_body,
        grid=(bs, nt),
        in_specs=[pl.BlockSpec((SSM_TC, SSM_WIDTH), lambda b, i: (b * nt + i, OFF_SSM // SSM_WIDTH)),
                  pl.BlockSpec(bw.shape, fixed3), pl.BlockSpec(cw.shape, fixed3),
                  pl.BlockSpec((1, SSM_WIDTH), fixed2), pl.BlockSpec((SSM_WIDTH, SSM_WIDTH), fixed2),
                  pl.BlockSpec((8, SSM_NSTATE), fixed2), pl.BlockSpec((8, SSM_NSTATE), fixed2),
                  pl.BlockSpec((SSM_SUB, SSM_NSTATE), fixed2), pl.BlockSpec((SSM_SUB, SSM_NSTATE), fixed2)],
        out_specs=pl.BlockSpec((SSM_TC, SSM_WIDTH), lambda b, i: (b * nt + i, 0)),
        out_shape=jax.ShapeDtypeStruct((t, SSM_WIDTH), BF16),
        scratch_shapes=[pltpu.VMEM((SSM_TC, SSM_NSTATE), F32), pltpu.VMEM((SSM_TC, SSM_NSTATE), F32),
                        pltpu.VMEM((1, SSM_NSTATE), F32), pltpu.VMEM((1, SSM_NSTATE), F32)],
        compiler_params=_params("parallel", "arbitrary"),
        name="ssm_scan",
    )(proj, bw, cw, d_skip.reshape(1, SSM_WIDTH), glu_w.astype(BF16), ar, ai, pr, pi)


GDN_TC = 256
GDN_HALO = 8


def _gdn_body(nega_ref, dtb_ref, q_ref, k_ref, v_ref, z_ref, sm_ref, wq_ref, wk_ref, wv_ref, ng_ref,
              o_ref, state_ref, hq_ref, hk_ref, hv_ref, vnew_ref):
    TC, CH, DK, H = GDN_TC, GDN_CHUNK, GDN_DK, GDN_HEADS
    heads = range(H)

    @pl.when(pl.program_id(1) == 0)
    def _():
        state_ref[...] = jnp.zeros_like(state_ref)
        hq_ref[...] = jnp.zeros_like(hq_ref)
        hk_ref[...] = jnp.zeros_like(hk_ref)
        hv_ref[...] = jnp.zeros_like(hv_ref)

    def conv_silu(x_ref, halo_ref, w_ref):
        x = x_ref[...].astype(F32)
        xx = jnp.concatenate([halo_ref[...], x], axis=0)
        w = w_ref[...]
        y = w[GDN_CONV - 1:GDN_CONV] * x
        for d in range(1, GDN_CONV):
            y = y + w[GDN_CONV - 1 - d:GDN_CONV - d] * pltpu.roll(xx, d, 0)[GDN_HALO:GDN_HALO + TC]
        halo_ref[...] = x[TC - GDN_HALO:TC]
        return y * jax.nn.sigmoid(y)

    def per_head(x):
        return [x[:, h * LANES:(h + 1) * LANES] for h in heads]

    q = per_head(conv_silu(q_ref, hq_ref, wq_ref))
    k = per_head(conv_silu(k_ref, hk_ref, wk_ref))
    v = per_head(conv_silu(v_ref, hv_ref, wv_ref))
    q = [x * lax.rsqrt(jnp.sum(x * x, axis=-1, keepdims=True) + 1e-6) * (DK ** -0.5) for x in q]
    k = [x * lax.rsqrt(jnp.sum(x * x, axis=-1, keepdims=True) + 1e-6) for x in k]

    small = sm_ref[...].astype(F32)
    beta = [jax.nn.sigmoid(small[:, H + h:H + h + 1]) for h in heads]
    la = []
    for h in heads:
        xa = small[:, h:h + 1] + dtb_ref[h]
        softplus = jnp.maximum(xa, 0.0) + jnp.log(1.0 + jnp.exp(-jnp.abs(xa)))
        la.append(jnp.broadcast_to(nega_ref[h] * softplus, (TC, LANES)))

    ri = lax.broadcasted_iota(jnp.int32, (TC, TC), 0)
    ci = lax.broadcasted_iota(jnp.int32, (TC, TC), 1)
    same = (ri // CH) == (ci // CH)
    causal = jnp.where(same, jnp.where(ci <= ri, 1.0, 0.0), 0.0)
    strict = jnp.where(ci < ri, causal, 0.0)
    eye = jnp.where(ri == ci, 1.0, 0.0)
    tri = causal.astype(BF16)
    lane = lax.broadcasted_iota(jnp.int32, (TC, LANES), 1)

    g = []
    for h in heads:
        la_hi, la_lo = _split_bf16(la[h])
        g.append(jnp.dot(tri, la_hi, preferred_element_type=F32)
                 + jnp.dot(tri, la_lo, preferred_element_type=F32))
    decay = []
    for h in heads:
        g_hi = g[h].astype(BF16).astype(F32)
        g_lo = g[h] - g_hi
        lhs = jnp.where(lane == 0, g_hi, jnp.where(lane == 1, g_lo, jnp.where(lane < 4, 1.0, 0.0)))
        rhs = jnp.where(lane < 2, 1.0, jnp.where(lane == 2, -g_hi, jnp.where(lane == 3, -g_lo, 0.0)))
        decay.append(jnp.exp(jnp.where(causal > 0.5, _dot_nt(lhs.astype(BF16), rhs.astype(BF16)), MASKED)))

    kb = [k[h] * beta[h] for h in heads]
    k_b = [x.astype(BF16) for x in k]
    lmat = [strict * _dot_nt(kb[h].astype(BF16), k_b[h]) * decay[h] for h in heads]
    tinv = [eye - x for x in lmat]
    pw = lmat
    for _ in range(CH.bit_length() - 2):
        pw = [jnp.dot(x.astype(BF16), x.astype(BF16), preferred_element_type=F32) for x in pw]
        tinv = [tinv[h] + jnp.dot(tinv[h].astype(BF16), pw[h].astype(BF16), preferred_element_type=F32)
                for h in heads]
    eg = [jnp.exp(x) for x in g]
    sol = [jnp.dot(tinv[h].astype(BF16),
                   jnp.concatenate([v[h] * beta[h], kb[h] * eg[h]], axis=1).astype(BF16),
                   preferred_element_type=F32) for h in heads]
    attn = [(causal * _dot_nt(q[h].astype(BF16), k_b[h]) * decay[h]).astype(BF16) for h in heads]
    q_st = [(q[h] * eg[h]).astype(BF16) for h in heads]

    vnew_ref[...] = jnp.zeros_like(vnew_ref)
    for c in range(TC // CH):
        rows = slice(c * CH, (c + 1) * CH)
        g_last = [x[(c + 1) * CH - 1:(c + 1) * CH, :] for x in g]
        st = [state_ref[h] for h in heads]
        st_b = [x.astype(BF16) for x in st]
        v_new = [sol[h][rows, :GDN_DV]
                 - jnp.dot(sol[h][rows, GDN_DV:].astype(BF16), st_b[h], preferred_element_type=F32)
                 for h in heads]
        for h in heads:
            vnew_ref[h, rows, :] = v_new[h].astype(BF16)
        o_c = [jnp.dot(q_st[h][rows], st_b[h], preferred_element_type=F32)
               + jnp.dot(attn[h][rows], vnew_ref[h], preferred_element_type=F32) for h in heads]
        for h in heads:
            k_st = (k[h][rows] * jnp.exp(g_last[h] - g[h][rows])).astype(BF16)
            state_ref[h] = st[h] * jnp.exp(g_last[h][:, :1]) + _dot_tn(k_st, v_new[h].astype(BF16))
        for h in heads:
            o = o_c[h] * lax.rsqrt(jnp.mean(o_c[h] * o_c[h], axis=-1, keepdims=True) + RMS_EPS) * ng_ref[...]
            zc = z_ref[rows, h * LANES:(h + 1) * LANES].astype(F32)
            o_ref[rows, h * LANES:(h + 1) * LANES] = (o * (zc * jax.nn.sigmoid(zc))).astype(o_ref.dtype)


def gdn_mixer(proj, bs, conv_w, a_log, dt_bias, norm_g):
    t = proj.shape[0]
    s = t // bs
    nt = s // GDN_TC
    H = GDN_HEADS
    hw = H * LANES
    assert s % GDN_TC == 0 and GDN_DK == LANES and GDN_DV == LANES and OFF_QKV == 0 and OFF_Z % hw == 0
    cw = jnp.zeros((8, GDN_QKV), F32).at[:GDN_CONV].set(conv_w)
    tok = lambda blk: pl.BlockSpec((GDN_TC, hw), lambda b, i: (b * nt + i, blk))
    wspec = lambda blk: pl.BlockSpec((8, hw), lambda b, i: (0, blk))
    smem = pl.BlockSpec(memory_space=pltpu.SMEM)
    return pl.pallas_call(
        _gdn_body,
        grid=(bs, nt),
        in_specs=[smem, smem, tok(0), tok(1), tok(2), tok(OFF_Z // hw),
                  pl.BlockSpec((GDN_TC, 2 * LANES), lambda b, i: (b * nt + i, OFF_SMALL // (2 * LANES))),
                  wspec(0), wspec(1), wspec(2), pl.BlockSpec((1, GDN_DV), lambda b, i: (0, 0))],
        out_specs=pl.BlockSpec((GDN_TC, hw), lambda b, i: (b * nt + i, 0)),
        out_shape=jax.ShapeDtypeStruct((t, hw), BF16),
        scratch_shapes=[pltpu.VMEM((H, GDN_DK, GDN_DV), F32)] + [pltpu.VMEM((GDN_HALO, hw), F32)] * 3
        + [pltpu.VMEM((H, GDN_TC, GDN_DV), BF16)],
        compiler_params=_params("parallel", "arbitrary"),
        name="gdn_delta",
    )(-jnp.exp(a_log), dt_bias.astype(F32), proj, proj, proj, proj, proj, cw, cw, cw,
      norm_g.reshape(1, GDN_DV))


NSA_KT = 256
NSA_NEG = MASKED
NSA_WTILES = NSA_WINDOW // LANES + 1


def _bucket_table():
    n = np.arange(LANES)
    max_exact = REL_BUCKETS // 2
    nf = np.maximum(n, 1).astype(np.float32)
    large = max_exact + (np.log(nf / np.float32(max_exact)) / np.float32(math.log(REL_MAX_DIST / max_exact))
                         * np.float32(REL_BUCKETS - max_exact)).astype(np.int32)
    tbl = np.where(n < max_exact, n, np.minimum(large, REL_BUCKETS - 1))
    assert tbl[-1] == REL_BUCKETS - 1 and REL_MAX_DIST <= LANES
    return tbl


def _compress_body(x_ref, pe_ref, w1_ref, w2_ref, o_ref):
    x = x_ref[0, 0, 0]
    w1 = w1_ref[0]
    half = NSA_CMP_STRIDE * NSA_DH
    nc = x.shape[0]
    a = jnp.dot(x, w1[:half], preferred_element_type=F32)
    b = jnp.dot(x, w1[half:], preferred_element_type=F32)
    pe_h = jnp.dot(pe_ref[0], w1, preferred_element_type=F32)
    hid = a + pltpu.roll(b, nc - 1, 0) + pe_h[0:1, :]
    o_ref[0, 0, 0] = jnp.dot(jax.nn.gelu(hid).astype(BF16), w2_ref[0], preferred_element_type=F32)


def nsa_compress(x, pe, w1, w2):
    _, bs, g, nc, width = x.shape
    return pl.pallas_call(
        _compress_body,
        grid=(2, bs, g),
        in_specs=[pl.BlockSpec((1, 1, 1, nc, width), lambda i, b, j: (i, b, j, 0, 0)),
                  pl.BlockSpec((1,) + pe.shape[1:], lambda i, b, j: (i, 0, 0)),
                  pl.BlockSpec((1,) + w1.shape[1:], lambda i, b, j: (i, 0, 0)),
                  pl.BlockSpec((1,) + w2.shape[1:], lambda i, b, j: (i, 0, 0))],
        out_specs=pl.BlockSpec((1, 1, 1, nc, NSA_DH), lambda i, b, j: (i, b, j, 0, 0)),
        out_shape=jax.ShapeDtypeStruct((2, bs, g, nc, NSA_DH), F32),
        compiler_params=_params("parallel", "parallel", "parallel"),
        name="nsa_compress",
    )(x, pe, w1, w2)


def _nsa_body(nfast, std, near_w, near_c, q_ref, gl_ref, qp_ref, kp_ref, cp_ref,
              ks_ref, vs_ref, kw_ref, vw_ref, kc_ref, vc_ref, agg_ref, td_ref, o_ref,
              m_ref, l_ref, acc_ref, sa_ref, sb_ref, sc_ref, *, n_sel, sel_k):
    QB, HG, G = NSA_QBLOCK, NSA_HPG, NSA_KV_GROUPS
    groups = range(G)
    qi = pl.program_id(1)
    s0 = qi * QB
    nc = kc_ref.shape[2]
    n_ct = nc // LANES
    q = [q_ref[0, g, 0] for g in groups]
    qp = qp_ref[...]

    def lanes4(x):
        return jnp.concatenate([x] * HG, axis=1)

    def delta_t(kp):
        idx = jnp.clip(_dot_nt(kp, qp), 0.0, float(LANES - 1)).astype(jnp.int32)
        outs = []
        for g in groups:
            heads = []
            for hg in range(HG):
                tb = jnp.broadcast_to(td_ref[g * HG + hg:g * HG + hg + 1, :], idx.shape)
                heads.append(jnp.take_along_axis(tb, idx, axis=1))
            outs.append(jnp.concatenate(heads, axis=1))
        return outs

    def maybe_delta(flag, kp):
        zeros = jnp.zeros((kp.shape[0], HG * QB), F32)
        return lax.cond(flag != 0, lambda: tuple(delta_t(kp)), lambda: (zeros,) * G)

    gate = [jax.nn.sigmoid(gl_ref[0, g, 0]) for g in groups]

    for g in groups:
        sc_ref[g] = _dot_nt(kc_ref[0, g], q[g])
    near_rows = 4 * 8

    @pl.when(std[qi] != 0)
    def _():
        c0 = pl.multiple_of(jnp.minimum(qi * (QB // NSA_CMP_STRIDE) - near_rows // 2, nc - near_rows), 8)
        d = delta_t(cp_ref[pl.ds(c0, near_rows), :])
        for g in groups:
            sc_ref[g, pl.ds(c0, near_rows), :] += d[g]

    @pl.when(std[qi] == 0)
    def _():
        for ct in range(n_ct):
            @pl.when(near_c[qi * n_ct + ct] != 0)
            def _():
                d = delta_t(cp_ref[ct * LANES:(ct + 1) * LANES, :])
                for g in groups:
                    sc_ref[g, ct * LANES:(ct + 1) * LANES, :] += d[g]

    c_id = lax.broadcasted_iota(jnp.int32, (nc, QB), 0)
    c_q = lax.broadcasted_iota(jnp.int32, (nc, QB), 1)
    ok_c = lanes4(jnp.where(c_id * NSA_CMP_STRIDE + (NSA_CMP_LEN - 1) <= s0 + c_q, 1.0, 0.0))
    blk = lax.broadcasted_iota(jnp.int32, (LANES, QB), 0)
    t_tok = s0 + lax.broadcasted_iota(jnp.int32, (LANES, QB), 1)
    tb_blk = t_tok // NSA_SEL_LEN
    forced = jnp.where(blk == 0, 1.0, 0.0) + jnp.where(blk == tb_blk, 1.0, 0.0) \
        + jnp.where(blk == tb_blk - 1, 1.0, 0.0)
    blkf = blk.astype(F32)
    out, score = [], []
    for g in groups:
        s = jnp.where(ok_c > 0.5, sc_ref[g], NSA_NEG)
        e = jnp.exp2(s - jnp.max(s, axis=0, keepdims=True)) * ok_c
        p = e * (1.0 / jnp.maximum(jnp.sum(e, axis=0, keepdims=True), 1e-30))
        out.append(gate[g][0:1, :] * jnp.dot(vc_ref[0, g], p.astype(BF16), preferred_element_type=F32))
        ps = p[:, 0:QB]
        for hg in range(1, HG):
            ps = ps + p[:, hg * QB:(hg + 1) * QB]
        ps_hi, ps_lo = _split_bf16(ps)
        imp = (jnp.dot(agg_ref[...], ps_hi, preferred_element_type=F32)
               + jnp.dot(agg_ref[...], ps_lo, preferred_element_type=F32))
        sco = jnp.where(forced > 0.5, 1e9, jnp.where(blk * NSA_SEL_LEN <= t_tok, imp, -1e9))
        score.append(jnp.where(blk < n_sel, sco, -jnp.inf))
    sel = [jnp.zeros((LANES, QB), F32) for _ in groups]
    for _ in range(sel_k):
        for g in groups:
            mx = jnp.max(score[g], axis=0, keepdims=True)
            first = jnp.min(jnp.where(score[g] == mx, blkf, float(LANES)), axis=0, keepdims=True)
            hit = blkf == first
            sel[g] = jnp.where(hit, 1.0, sel[g])
            score[g] = jnp.where(hit, -jnp.inf, score[g])
    q2 = []
    for g in groups:
        sel_q = jnp.transpose(jnp.where(sel[g] > 0.5, 0.0, NSA_NEG))
        q2.append(jnp.concatenate([q[g], jnp.concatenate([sel_q] * HG, axis=0).astype(BF16)], axis=1))

    SEL, WIN = 0, 1

    def reset():
        m_ref[...] = jnp.full_like(m_ref, NSA_NEG)
        l_ref[...] = jnp.zeros_like(l_ref)
        acc_ref[...] = jnp.zeros_like(acc_ref)

    def tile_step(br, g, k, s, v_t):
        m_old = m_ref[br, g, k]
        m_new = jnp.maximum(m_old, jnp.max(s, axis=0, keepdims=True))
        alpha = jnp.exp2(m_old - m_new)
        p = jnp.exp2(s - m_new)
        l_ref[br, g, k] = l_ref[br, g, k] * alpha + jnp.sum(p, axis=0, keepdims=True)
        acc_ref[br, g, k] = (acc_ref[br, g, k] * alpha
                             + jnp.dot(v_t, p.astype(BF16), preferred_element_type=F32))
        m_ref[br, g, k] = m_new

    def finish(br, g):
        m = jnp.maximum(m_ref[br, g, 0], m_ref[br, g, 1])
        w0, w1 = jnp.exp2(m_ref[br, g, 0] - m), jnp.exp2(m_ref[br, g, 1] - m)
        return ((acc_ref[br, g, 0] * w0 + acc_ref[br, g, 1] * w1)
                * (1.0 / (l_ref[br, g, 0] * w0 + l_ref[br, g, 1] * w1)))

    def emit():
        heads_out = []
        for g in groups:
            o_t = out[g] + gate[g][1:2, :] * finish(SEL, g) + gate[g][2:3, :] * finish(WIN, g)
            heads_out += [jnp.transpose(o_t[:, hg * QB:(hg + 1) * QB]) for hg in range(HG)]
        o_ref[0] = jnp.concatenate(heads_out, axis=1).astype(o_ref.dtype)

    key_r = lax.broadcasted_iota(jnp.int32, (NSA_KT, QB), 0)
    key_q = lax.broadcasted_iota(jnp.int32, (NSA_KT, QB), 1)
    w_r = lax.broadcasted_iota(jnp.int32, (LANES, QB), 0)
    w_q = lax.broadcasted_iota(jnp.int32, (LANES, QB), 1)
    in_window = lanes4(jnp.where(w_r > w_q, 0.0, NSA_NEG))
    causal_w = lanes4(jnp.where(w_r <= w_q, 0.0, NSA_NEG))

    def sel_at(k0, n):
        k0 = pl.multiple_of(k0, LANES)
        return [(_dot_nt(ks_ref[0, g, pl.ds(k0, n), :], q2[g]), vs_ref[0, g, :, pl.ds(k0, n)])
                for g in groups]

    def win_at(k0, n):
        k0 = pl.multiple_of(k0, LANES)
        return [(_dot_nt(kw_ref[0, g, pl.ds(k0, n), :], q[g]), vw_ref[0, g, :, pl.ds(k0, n)])
                for g in groups]

    def delta_at(k0, n):
        return delta_t(kp_ref[pl.ds(pl.multiple_of(k0, LANES), n), :])

    def pair_scores(p, buf):
        for t in range(2):
            sc_t = sel_at((2 * p + t) * NSA_KT, NSA_KT)
            for g in groups:
                buf[g, t] = sc_t[g][0]

    def pair_softmax(p, buf):
        for t in range(2):
            k0 = pl.multiple_of((2 * p + t) * NSA_KT, NSA_KT)
            for g in groups:
                tile_step(SEL, g, t, buf[g, t], vs_ref[0, g, :, pl.ds(k0, NSA_KT)])

    def run_pairs(n_pairs):
        last = jnp.maximum(n_pairs - 1, 0)
        pair_scores(0, sa_ref)

        def two_pairs(j, carry):
            pair_scores(2 * j + 1, sb_ref)
            pair_softmax(2 * j, sa_ref)
            pair_scores(jnp.minimum(2 * j + 2, last), sa_ref)
            pair_softmax(2 * j + 1, sb_ref)
            return carry

        lax.fori_loop(0, n_pairs // 2, two_pairs, 0)

        @pl.when(n_pairs % 2 == 1)
        def _():
            pair_softmax(n_pairs - 1, sa_ref)

    @pl.when(std[qi] != 0)
    def _():
        reset()
        nb = (qi - 1) // 2
        run_pairs(nb // 2)

        @pl.when(nb % 2 == 1)
        def _():
            a = sel_at((nb - 1) * NSA_KT, NSA_KT)
            for g in groups:
                tile_step(SEL, g, 0, *a[g])

        @pl.when(qi % 2 == 0)
        def _():
            a = sel_at(nb * NSA_KT, LANES)
            for g in groups:
                tile_step(SEL, g, 1, *a[g])

        half = NSA_WINDOW // 2
        zeros = jnp.zeros((LANES, HG * QB), F32)
        sel_a, sel_b = sel_at(s0 - LANES, LANES), sel_at(s0, LANES)
        win_a, win_b, win_d = win_at(s0 - NSA_WINDOW, half), win_at(s0 - half, half), win_at(s0, LANES)
        d_prev, d_diag = delta_at(s0 - LANES, LANES), delta_at(s0, LANES)
        for g in groups:
            tile_step(SEL, g, 0, sel_a[g][0] + d_prev[g], sel_a[g][1])
        for g in groups:
            tile_step(WIN, g, 0, win_a[g][0] + jnp.concatenate([in_window, zeros], axis=0), win_a[g][1])
        for g in groups:
            tile_step(SEL, g, 1, sel_b[g][0] + d_diag[g] + causal_w, sel_b[g][1])
        for g in groups:
            tile_step(WIN, g, 1, win_b[g][0] + jnp.concatenate([zeros, d_prev[g]], axis=0), win_b[g][1])
        for g in groups:
            tile_step(WIN, g, 0, win_d[g][0] + d_diag[g] + causal_w, win_d[g][1])
        emit()

    @pl.when(std[qi] == 0)
    def _():
        reset()
        n_past = (s0 + QB - 1) // NSA_KT
        n_pairs = nfast[qi] // 2
        run_pairs(n_pairs)

        def slow_body(kt, carry):
            a = sel_at(kt * NSA_KT, NSA_KT)
            d = delta_at(kt * NSA_KT, NSA_KT)
            for g in groups:
                tile_step(SEL, g, 1, a[g][0] + d[g], a[g][1])
            return carry

        lax.fori_loop(2 * n_pairs, n_past, slow_body, 0)
        k0 = n_past * NSA_KT
        a = sel_at(k0, NSA_KT)
        d = delta_at(k0, NSA_KT)
        causal = lanes4(jnp.where(k0 + key_r <= s0 + key_q, 0.0, NSA_NEG))
        for g in groups:
            tile_step(SEL, g, 0, a[g][0] + d[g] + causal, a[g][1])

        for j in range(NSA_WTILES):
            start = s0 - NSA_WINDOW + j * LANES

            @pl.when(start >= 0)
            def _():
                sw = win_at(start, LANES)
                d = maybe_delta(near_w[qi * NSA_WTILES + j],
                                kp_ref[pl.ds(pl.multiple_of(start, LANES), LANES), :])
                for g in groups:
                    s = sw[g][0] + d[g]
                    if j == 0:
                        s = s + in_window
                    elif j == NSA_WTILES - 1:
                        s = s + causal_w
                    tile_step(WIN, g, j % 2, s, sw[g][1])

        emit()


def _pos_digits(p, key_side):
    d0, d1, d2 = (p & 127).astype(F32), ((p >> 7) & 127).astype(F32), (p >> 14).astype(F32)
    one = jnp.ones_like(d0)
    if key_side:
        cols = [one, one, one, -d2, -d1, -d0]
    else:
        cols = [16384.0 * d2, 128.0 * d1, d0, 16384.0 * one, 128.0 * one, one]
    out = jnp.stack(cols, axis=-1)
    return jnp.pad(out, ((0, 0), (0, LANES - out.shape[-1]))).astype(BF16)


def nsa_mixer(q, k_c, v_c, k_s, v_s, k_w, v_w, gate_logits, positions,
              ck_pe, ck_w1, ck_w2, cv_pe, cv_w1, cv_w2, rel_bias):
    q, k_c, v_c, k_s, v_s, k_w, v_w = lax.optimization_barrier((q, k_c, v_c, k_s, v_s, k_w, v_w))
    bs, s, _ = q.shape
    G, HG, DH, QB = NSA_KV_GROUPS, NSA_HPG, NSA_DH, NSA_QBLOCK
    nqt = s // QB
    nc = s // NSA_CMP_STRIDE
    n_cmp = (s - NSA_CMP_LEN) // NSA_CMP_STRIDE + 1
    n_sel = s // NSA_SEL_LEN
    sel_k = min(NSA_SEL_TOPK, n_sel)
    n_kt = s // NSA_KT
    n_ct = nc // LANES
    assert s % (LANES * NSA_CMP_STRIDE) == 0 and n_sel <= LANES and NSA_KT == 2 * QB

    def by_group(t):
        return t.reshape(bs, s, G, DH).transpose(0, 2, 1, 3).astype(F32)

    def by_group_t(t):
        return t.reshape(bs, s, G, DH).transpose(0, 2, 3, 1).astype(BF16)

    ones2 = jnp.zeros((LANES - DH,), F32).at[:2].set(1.0)

    def keys_aug(t, blocks=False):
        parts = [t, jnp.broadcast_to(ones2, t.shape[:-1] + (LANES - DH,))]
        if blocks:
            onehot = (jnp.arange(s)[:, None] // NSA_SEL_LEN == jnp.arange(LANES)[None, :]).astype(F32)
            parts.append(jnp.broadcast_to(onehot, t.shape[:-2] + (s, LANES)))
        return jnp.concatenate(parts, axis=-1).astype(BF16)

    chunks = jnp.stack([by_group(k_c), by_group(v_c)]).reshape(2, bs, G, nc, NSA_CMP_STRIDE * DH)
    pe = jnp.stack([ck_pe, cv_pe]).reshape(2, 1, NSA_CMP_LEN * DH)
    cmp = nsa_compress(chunks.astype(BF16), jnp.broadcast_to(pe, (2, 8, NSA_CMP_LEN * DH)).astype(BF16),
                       jnp.stack([ck_w1, cv_w1]).astype(BF16), jnp.stack([ck_w2, cv_w2]).astype(BF16))
    kc, vc_t = keys_aug(cmp[0]), cmp[1].transpose(0, 1, 3, 2).astype(BF16)

    log2e = math.log2(math.e)
    rel_bias = rel_bias.astype(F32) * log2e
    far = rel_bias[REL_BUCKETS - 1]
    far_hi = far.astype(BF16).astype(F32)
    extra = jnp.zeros((NSA_HEADS, LANES - DH), F32).at[:, 0].set(far_hi).at[:, 1].set(far - far_hi)
    qh = (q.astype(F32) * (DH ** -0.5 * log2e)).reshape(bs, nqt, QB, G, HG, DH).transpose(0, 3, 1, 4, 2, 5)
    ex = jnp.broadcast_to(extra.reshape(1, G, 1, HG, 1, LANES - DH), (bs, G, nqt, HG, QB, LANES - DH))
    qs = jnp.concatenate([qh, ex], axis=-1).astype(BF16).reshape(bs, G, nqt, HG * QB, LANES)
    gl = gate_logits.astype(F32).reshape(bs, nqt, QB, G, HG, 3).transpose(0, 3, 1, 5, 4, 2)
    gl = jnp.pad(gl.reshape(bs, G, nqt, 3, HG * QB), ((0, 0),) * 3 + ((0, 5), (0, 0)))

    td = (rel_bias[_bucket_table()] - rel_bias[REL_BUCKETS - 1][None, :]).T.astype(F32)
    cmp_end = jnp.minimum(jnp.arange(nc) * NSA_CMP_STRIDE + NSA_CMP_LEN - 1, s - 1)
    cpos = positions[cmp_end]
    qmin = positions.reshape(nqt, QB).min(axis=1)
    kmax = positions.reshape(nqt, QB).max(axis=1)
    near_s = (qmin[:, None] - positions.reshape(n_kt, NSA_KT).max(axis=1)[None, :]) < REL_MAX_DIST
    n_past = (jnp.arange(nqt) * QB + QB - 1) // NSA_KT
    n_fast = jnp.minimum(jnp.argmax(jnp.concatenate([near_s, jnp.ones((nqt, 1), bool)], axis=1), axis=1),
                         n_past)
    wt = jnp.clip(jnp.arange(nqt)[:, None] - (NSA_WTILES - 1) + jnp.arange(NSA_WTILES)[None, :], 0, nqt - 1)
    near_w = (qmin[:, None] - kmax[wt]) < REL_MAX_DIST
    near_c = (qmin[:, None] - cpos.reshape(n_ct, LANES).max(axis=1)[None, :]) < REL_MAX_DIST
    before = lax.cummax(kmax)[jnp.maximum(jnp.arange(nqt) - 2, 0)]
    std = (jnp.arange(nqt) >= NSA_WTILES - 1) & (qmin - before >= REL_MAX_DIST)

    cmp_start = np.arange(nc) * NSA_CMP_STRIDE
    sel_start = np.arange(LANES) * NSA_SEL_LEN
    agg_t = ((cmp_start[None, :] <= sel_start[:, None] + NSA_SEL_LEN - 1)
             & (cmp_start[None, :] + NSA_CMP_LEN - 1 >= sel_start[:, None])
             & (np.arange(nc)[None, :] < n_cmp) & (np.arange(LANES)[:, None] < n_sel))

    cols = HG * QB
    full = lambda shape: pl.BlockSpec(shape, lambda b, i, *_: (0,) * len(shape))
    per_b = lambda n, w: pl.BlockSpec((1, G, n, w), lambda b, i, *_: (b, 0, 0, 0))
    per_tile = lambda n, w: pl.BlockSpec((1, G, 1, n, w), lambda b, i, *_: (b, 0, i, 0, 0))
    grid_spec = pltpu.PrefetchScalarGridSpec(
        num_scalar_prefetch=4,
        grid=(bs, nqt),
        in_specs=[per_tile(cols, LANES), per_tile(8, cols),
                  pl.BlockSpec((QB, LANES), lambda b, i, *_: (i, 0)),
                  full((s, LANES)), full((nc, LANES)),
                  per_b(s, 2 * LANES), per_b(DH, s), per_b(s, LANES), per_b(DH, s),
                  per_b(nc, LANES), per_b(DH, nc),
                  full((LANES, nc)), full((NSA_HEADS, LANES))],
        out_specs=pl.BlockSpec((1, QB, NSA_Q), lambda b, i, *_: (b, i, 0)),
        scratch_shapes=[pltpu.VMEM((2, G, 2, 1, cols), F32), pltpu.VMEM((2, G, 2, 1, cols), F32),
                        pltpu.VMEM((2, G, 2, DH, cols), F32)]
        + [pltpu.VMEM((G, 2, NSA_KT, cols), F32)] * 2 + [pltpu.VMEM((G, nc, cols), F32)],
    )
    return pl.pallas_call(
        functools.partial(_nsa_body, n_sel=n_sel, sel_k=sel_k),
        grid_spec=grid_spec,
        out_shape=jax.ShapeDtypeStruct((bs, s, NSA_Q), BF16),
        compiler_params=_params("parallel", "arbitrary"),
        name="nsa_attention",
    )(n_fast.astype(jnp.int32), std.astype(jnp.int32), near_w.reshape(-1).astype(jnp.int32),
      near_c.reshape(-1).astype(jnp.int32),
      qs, gl, _pos_digits(positions, False), _pos_digits(positions, True), _pos_digits(cpos, True),
      keys_aug(by_group(k_s), blocks=True), by_group_t(v_s), keys_aug(by_group(k_w)), by_group_t(v_w),
      kc, vc_t, jnp.asarray(agg_t, BF16), td)


SGU_TC = 512


def _sgu_body(uv_ref, lg_ref, lb_ref, w_ref, b_ref, o_ref):
    uv = jax.nn.gelu(uv_ref[...].astype(F32))
    u, v = uv[:, :SGU_WIDTH], uv[:, SGU_WIDTH:]
    mu = jnp.mean(v, axis=-1, keepdims=True)
    vc = v - mu
    var = jnp.mean(vc * vc, axis=-1, keepdims=True)
    vn = (vc * lax.rsqrt(var + RMS_EPS) * lg_ref[...] + lb_ref[...]).astype(BF16)
    gw = SGU_WIDTH // SGU_GROUPS
    for c in range(SGU_TC // SGU_CHUNK):
        rows = slice(c * SGU_CHUNK, (c + 1) * SGU_CHUNK)
        for g in range(SGU_GROUPS):
            cols = slice(g * gw, (g + 1) * gw)
            mix = jnp.dot(w_ref[g], vn[rows, cols], preferred_element_type=F32) + b_ref[:, g:g + 1]
            o_ref[rows, cols] = (u[rows, cols] * mix).astype(o_ref.dtype)


def sgu_mixer(proj, ln_g, ln_b, w_s, b_s):
    t = proj.shape[0]
    assert t % SGU_TC == 0 and OFF_SGU % (2 * SGU_WIDTH) == 0
    fixed = lambda i: (0, 0)
    return pl.pallas_call(
        _sgu_body,
        grid=(t // SGU_TC,),
        in_specs=[pl.BlockSpec((SGU_TC, 2 * SGU_WIDTH), lambda i: (i, OFF_SGU // (2 * SGU_WIDTH))),
                  pl.BlockSpec((1, SGU_WIDTH), fixed), pl.BlockSpec((1, SGU_WIDTH), fixed),
                  pl.BlockSpec((SGU_GROUPS, SGU_CHUNK, SGU_CHUNK), lambda i: (0, 0, 0)),
                  pl.BlockSpec((SGU_CHUNK, SGU_GROUPS), fixed)],
        out_specs=pl.BlockSpec((SGU_TC, SGU_WIDTH), lambda i: (i, 0)),
        out_shape=jax.ShapeDtypeStruct((t, SGU_WIDTH), BF16),
        compiler_params=_params("parallel"),
        name="sgu_gate",
    )(proj, ln_g.reshape(1, SGU_WIDTH), ln_b.reshape(1, SGU_WIDTH), jnp.tril(w_s).astype(BF16), b_s.T)


def _permute_w_in(w):
    sizes = ([SSM_WIDTH, GDN_QKV, GDN_HEADS, GDN_HEADS, GDN_HEADS * GDN_DV, NSA_Q]
             + [NSA_KV] * 6 + [3 * NSA_HEADS, 2 * SGU_WIDTH, N_BRANCH * D_MODEL])
    cuts = [int(c) for c in np.cumsum(sizes)[:-1]]
    (w_ssm, w_qkv, w_a, w_b, w_z, w_nq, w_kc, w_vc, w_ks, w_vs, w_kw, w_vw,
     w_ng, w_sgu, w_gate) = jnp.split(w, cuts, axis=-1)
    pad = jnp.zeros((w.shape[0], PROJ_COLS - OFF_SMALL - SMALL_USED), w.dtype)
    return jnp.concatenate([w_qkv, w_ssm, w_z, w_nq, w_sgu, w_gate, w_kc, w_vc, w_ks, w_vs,
                            w_kw, w_vw, w_a, w_b, w_ng, pad], axis=-1).astype(BF16)


def kernel(x, positions, norm_mix, norm_ffn, norm_final, w_in, ssm_a_re, ssm_a_im, ssm_log_dt, ssm_b_re, ssm_b_im, ssm_c_re, ssm_c_im, ssm_d, ssm_glu_w, gdn_conv_w, gdn_a_log, gdn_dt_bias, gdn_norm, nsa_cmp_k_pe, nsa_cmp_k_w1, nsa_cmp_k_w2, nsa_cmp_v_pe, nsa_cmp_v_w1, nsa_cmp_v_w2, rel_bias, sgu_ln_g, sgu_ln_b, sgu_w, sgu_b, w_br_ssm, w_br_gdn, w_br_nsa, w_br_sgu, w_out, ffn_w_gate, ffn_w_up, ffn_w_down, moe_router, moe_w_gate, moe_w_up, moe_w_down):
    bs, s, d = x.shape
    t = bs * s
    assert DEPTH == 2
    xf = x.reshape(t, d)
    for layer in range(DEPTH):
        proj = in_proj(xf, norm_mix[layer], _permute_w_in(w_in[layer]))
        p3 = proj.reshape(bs, s, PROJ_COLS)

        def seg(off, width):
            return p3[..., off:off + width]

        y_ssm = ssm_mixer(proj, bs, ssm_a_re[layer], ssm_a_im[layer], ssm_log_dt[layer],
                          ssm_b_re[layer], ssm_b_im[layer], ssm_c_re[layer], ssm_c_im[layer],
                          ssm_d[layer], ssm_glu_w[layer])
        y_gdn = gdn_mixer(proj, bs, gdn_conv_w[layer], gdn_a_log[layer], gdn_dt_bias[layer],
                          gdn_norm[layer])
        kvs = [seg(OFF_KV + i * NSA_KV, NSA_KV) for i in range(6)]
        y_nsa = nsa_mixer(seg(OFF_NQ, NSA_Q), *kvs, seg(OFF_SMALL + 2 * GDN_HEADS, 3 * NSA_HEADS),
                          positions, nsa_cmp_k_pe[layer], nsa_cmp_k_w1[layer], nsa_cmp_k_w2[layer],
                          nsa_cmp_v_pe[layer], nsa_cmp_v_w1[layer], nsa_cmp_v_w2[layer], rel_bias)
        y_sgu = sgu_mixer(proj, sgu_ln_g[layer], sgu_ln_b[layer], sgu_w[layer], sgu_b[layer])
        ys = [y_ssm, y_gdn, y_nsa.reshape(t, -1), y_sgu]
        ws = [w[layer].astype(BF16) for w in (w_br_ssm, w_br_gdn, w_br_nsa, w_br_sgu)]
        xf = merge(xf, proj, ys, ws, w_out[layer].astype(BF16))
        i = layer // 2
        if layer % 2 == 0:
            xf = dense_ffn(xf, norm_ffn[layer], ffn_w_gate[i], ffn_w_up[i], ffn_w_down[i])
        else:
            xf = moe_layer(xf, norm_ffn[layer], moe_router[i], moe_w_gate[i], moe_w_up[i],
                           moe_w_down[i], norm_final)
    return xf.reshape(bs, s, d)
```

```python
import functools
import math

import jax
import jax.numpy as jnp
from jax import lax
import numpy as np
from jax.experimental import pallas as pl
from jax.experimental.pallas import tpu as pltpu

F32 = jnp.float32
BF16 = jnp.bfloat16

D_MODEL = 1024
DEPTH = 2
SSM_WIDTH = D_MODEL // 2
SSM_GROUP = 16
SSM_GROUPS = SSM_WIDTH // SSM_GROUP
SSM_STATE = 64
GDN_HEADS = 4
GDN_DK = 128
GDN_DV = 128
GDN_CONV = 4
GDN_CHUNK = 64
NSA_HEADS = 8
NSA_KV_GROUPS = 2
NSA_HPG = NSA_HEADS // NSA_KV_GROUPS
NSA_DH = 64
NSA_CMP_LEN = 32
NSA_CMP_STRIDE = 16
NSA_CMP_HIDDEN = 128
NSA_SEL_LEN = 64
NSA_SEL_TOPK = 16
NSA_WINDOW = 512
NSA_QBLOCK = 128
SGU_WIDTH = D_MODEL // 2
SGU_GROUPS = 4
SGU_CHUNK = 128
REL_BUCKETS = 32
REL_MAX_DIST = 128
FFN_DENSE = 2816
N_EXPERTS = 8
TOP_K = 2
FFN_EXPERT = 3584
N_BRANCH = 4
RMS_EPS = 1e-6
GDN_QKV = GDN_HEADS * (2 * GDN_DK + GDN_DV)
NSA_Q = NSA_HEADS * NSA_DH
NSA_KV = NSA_KV_GROUPS * NSA_DH

LANES = 128
VMEM_LIMIT = 48 * 1024 * 1024
MASKED = -1e30

OFF_QKV = 0
OFF_SSM = OFF_QKV + GDN_QKV
OFF_Z = OFF_SSM + SSM_WIDTH
OFF_NQ = OFF_Z + GDN_HEADS * GDN_DV
OFF_SGU = OFF_NQ + NSA_Q
OFF_GATE = OFF_SGU + 2 * SGU_WIDTH
OFF_KV = OFF_GATE + N_BRANCH * D_MODEL
OFF_SMALL = OFF_KV + 6 * NSA_KV
SMALL_USED = 2 * GDN_HEADS + 3 * NSA_HEADS
PROJ_COLS = 9216


def _params(*sem):
    return pltpu.CompilerParams(dimension_semantics=sem, vmem_limit_bytes=VMEM_LIMIT)


def _rms(x, g):
    return x * lax.rsqrt(jnp.mean(x * x, axis=-1, keepdims=True) + RMS_EPS) * g


def _dot_nt(a, b):
    return lax.dot_general(a, b, (((1,), (1,)), ((), ())), preferred_element_type=F32)


def _dot_tn(a, b):
    return lax.dot_general(a, b, (((0,), (0,)), ((), ())), preferred_element_type=F32)


def _split_bf16(x):
    hi = x.astype(BF16)
    return hi, (x - hi.astype(F32)).astype(BF16)


def _in_proj_body(x_ref, g_ref, w_ref, o_ref, h_ref):
    @pl.when(pl.program_id(1) == 0)
    def _():
        h_ref[...] = _rms(x_ref[...], g_ref[...]).astype(BF16)

    o_ref[...] = jnp.dot(h_ref[...], w_ref[...], preferred_element_type=F32).astype(o_ref.dtype)


def in_proj(x, g, w, tm=2048, tn=512):
    t, d = x.shape
    n = w.shape[1]
    return pl.pallas_call(
        _in_proj_body,
        grid=(t // tm, n // tn),
        in_specs=[pl.BlockSpec((tm, d), lambda i, j: (i, 0)),
                  pl.BlockSpec((1, d), lambda i, j: (0, 0)),
                  pl.BlockSpec((d, tn), lambda i, j: (0, j))],
        out_specs=pl.BlockSpec((tm, tn), lambda i, j: (i, j)),
        out_shape=jax.ShapeDtypeStruct((t, n), BF16),
        scratch_shapes=[pltpu.VMEM((tm, d), BF16)],
        compiler_params=_params("parallel", "arbitrary"),
        name="in_proj",
    )(x, g.reshape(1, d), w)


def _merge_body(x_ref, gate_ref, ya_ref, yb_ref, yc_ref, yd_ref,
                wa_ref, wb_ref, wc_ref, wd_ref, wo_ref, o_ref):
    def branch(k, y_ref, w_ref):
        p = jnp.dot(y_ref[...], w_ref[...], preferred_element_type=F32)
        return jax.nn.sigmoid(gate_ref[:, k * D_MODEL:(k + 1) * D_MODEL].astype(F32)) * p

    merged = (branch(0, ya_ref, wa_ref) + branch(1, yb_ref, wb_ref)
              + branch(2, yc_ref, wc_ref) + branch(3, yd_ref, wd_ref))
    o_ref[...] = x_ref[...] + jnp.dot(merged.astype(BF16), wo_ref[...],
                                      preferred_element_type=F32)


def merge(x, proj, ys, ws, w_out, tm=512):
    t, d = x.shape
    gate_blk = OFF_GATE // (N_BRANCH * D_MODEL)
    row = lambda i: (i, 0)
    fixed = lambda i: (0, 0)
    in_specs = [pl.BlockSpec((tm, d), row),
                pl.BlockSpec((tm, N_BRANCH * D_MODEL), lambda i: (i, gate_blk))]
    in_specs += [pl.BlockSpec((tm, y.shape[1]), row) for y in ys]
    in_specs += [pl.BlockSpec(w.shape, fixed) for w in ws]
    in_specs += [pl.BlockSpec(w_out.shape, fixed)]
    return pl.pallas_call(
        _merge_body,
        grid=(t // tm,),
        in_specs=in_specs,
        out_specs=pl.BlockSpec((tm, d), row),
        out_shape=jax.ShapeDtypeStruct((t, d), F32),
        compiler_params=_params("parallel"),
        name="merge",
    )(x, proj, *ys, *ws, w_out)


def _ffn_body(x_ref, g_ref, wg_ref, wu_ref, wd_ref, o_ref, h_ref, acc_ref):
    f = pl.program_id(1)

    @pl.when(f == 0)
    def _():
        h_ref[...] = _rms(x_ref[...], g_ref[...]).astype(BF16)
        acc_ref[...] = x_ref[...]

    h = h_ref[...]
    a = jnp.dot(h, wg_ref[...].astype(BF16), preferred_element_type=F32)
    u = jnp.dot(h, wu_ref[...].astype(BF16), preferred_element_type=F32)
    act = (a * jax.nn.sigmoid(a) * u).astype(BF16)
    acc_ref[...] += jnp.dot(act, wd_ref[...].astype(BF16), preferred_element_type=F32)

    @pl.when(f == pl.num_programs(1) - 1)
    def _():
        o_ref[...] = acc_ref[...]


def dense_ffn(x, g, wg, wu, wd, tm=1024, tf=256):
    t, d = x.shape
    nf = wg.shape[1]
    return pl.pallas_call(
        _ffn_body,
        grid=(t // tm, nf // tf),
        in_specs=[pl.BlockSpec((tm, d), lambda i, f: (i, 0)),
                  pl.BlockSpec((1, d), lambda i, f: (0, 0)),
                  pl.BlockSpec((d, tf), lambda i, f: (0, f)),
                  pl.BlockSpec((d, tf), lambda i, f: (0, f)),
                  pl.BlockSpec((tf, d), lambda i, f: (f, 0))],
        out_specs=pl.BlockSpec((tm, d), lambda i, f: (i, 0)),
        out_shape=jax.ShapeDtypeStruct((t, d), F32),
        scratch_shapes=[pltpu.VMEM((tm, d), BF16), pltpu.VMEM((tm, d), F32)],
        compiler_params=_params("parallel", "arbitrary"),
        name="dense_ffn",
    )(x, g.reshape(1, d), wg, wu, wd)


def _route_body(x_ref, g_ref, rhi_ref, rlo_ref, h_ref, idx_ref, wt_ref):
    h = _rms(x_ref[...], g_ref[...])
    hi = h.astype(BF16)
    lo = (h - hi.astype(F32)).astype(BF16)
    h_ref[...] = h
    logits = (jnp.dot(hi, rhi_ref[...], preferred_element_type=F32)
              + jnp.dot(hi, rlo_ref[...], preferred_element_type=F32)
              + jnp.dot(lo, rhi_ref[...], preferred_element_type=F32))
    col = lax.broadcasted_iota(jnp.int32, logits.shape, 1).astype(F32)
    neg = jnp.float32(-jnp.inf)
    lg = jnp.where(col < N_EXPERTS, logits, neg)
    m1 = jnp.max(lg, axis=-1, keepdims=True)
    i1 = jnp.min(jnp.where(lg == m1, col, float(LANES)), axis=-1, keepdims=True)
    lg2 = jnp.where(col == i1, neg, lg)
    m2 = jnp.max(lg2, axis=-1, keepdims=True)
    i2 = jnp.min(jnp.where(lg2 == m2, col, float(LANES)), axis=-1, keepdims=True)
    e = jnp.exp(m2 - m1)
    w1 = 1.0 / (1.0 + e)
    w2 = e / (1.0 + e)
    idx_ref[...] = jnp.where(col == 0, i1, jnp.where(col == 1, i2, 0.0)).astype(jnp.int32)
    wt_ref[...] = jnp.where(col == 0, w1, jnp.where(col == 1, w2, 0.0))


def moe_route(x, g, r_hi, r_lo, tm=512):
    t, d = x.shape
    row = lambda i: (i, 0)
    fixed = lambda i: (0, 0)
    return pl.pallas_call(
        _route_body,
        grid=(t // tm,),
        in_specs=[pl.BlockSpec((tm, d), row), pl.BlockSpec((1, d), fixed),
                  pl.BlockSpec((d, LANES), fixed), pl.BlockSpec((d, LANES), fixed)],
        out_specs=[pl.BlockSpec((tm, d), row), pl.BlockSpec((tm, LANES), row),
                   pl.BlockSpec((tm, LANES), row)],
        out_shape=[jax.ShapeDtypeStruct((t, d), F32),
                   jax.ShapeDtypeStruct((t, LANES), jnp.int32),
                   jax.ShapeDtypeStruct((t, LANES), F32)],
        compiler_params=_params("parallel"),
        name="moe_route",
    )(x, g.reshape(1, d), r_hi, r_lo)


def _experts_body(te_ref, nu_ref, src_ref, h_hbm, wg_ref, wu_ref, wd_ref, o_ref,
                  xbuf_ref, xs_ref, acc_ref, sem, *, rows_per_step):
    i = pl.program_id(0)
    f = pl.program_id(1)
    tm = xs_ref.shape[0]
    n_rows = xbuf_ref.shape[1]
    n_used = nu_ref[0]
    used = i < n_used
    has_next = i + 1 < n_used
    slot = i % 2

    def start_row(tile, slot_, r):
        tok = src_ref[tile * tm + r]
        pltpu.make_async_copy(h_hbm.at[pl.ds(tok, 1), :], xbuf_ref.at[slot_, pl.ds(r, 1), :],
                              sem.at[slot_]).start()

    @pl.when(f == 0)
    def _():
        acc_ref[...] = jnp.zeros_like(acc_ref)

        @pl.when(used)
        def _():
            @pl.when(i == 0)
            def _():
                def row(r, carry):
                    start_row(0, 0, r)
                    return carry

                lax.fori_loop(0, n_rows, row, 0, unroll=8)

            pltpu.make_async_copy(h_hbm.at[pl.ds(0, n_rows), :], xbuf_ref.at[slot], sem.at[slot]).wait()
            xs_ref[...] = xbuf_ref[slot, 0:tm, :].astype(BF16)

    def ffn_step():
        h = xs_ref[...]
        a = jnp.dot(h, wg_ref[0].astype(BF16), preferred_element_type=F32)
        u = jnp.dot(h, wu_ref[0].astype(BF16), preferred_element_type=F32)
        act = (a * jax.nn.sigmoid(a) * u).astype(BF16)
        acc_ref[...] += jnp.dot(act, wd_ref[0].astype(BF16), preferred_element_type=F32)

    @pl.when(used & has_next)
    def _():
        for j in range(rows_per_step):
            start_row(i + 1, 1 - slot, f * rows_per_step + j)
        ffn_step()

    @pl.when(used & jnp.logical_not(has_next))
    def _():
        ffn_step()

    @pl.when(f == pl.num_programs(1) - 1)
    def _():
        o_ref[...] = acc_ref[...].astype(o_ref.dtype)


def moe_experts(tile_expert, n_used, src, h, wg, wu, wd, tm, tf=512):
    d = h.shape[1]
    nf = wg.shape[2] // tf
    rows_per_step = 8 * (-(-tm // (8 * nf)))
    n_rows = rows_per_step * nf
    n_tiles = src.shape[0] // tm
    src = jnp.pad(src, (0, n_rows - tm))
    grid_spec = pltpu.PrefetchScalarGridSpec(
        num_scalar_prefetch=3,
        grid=(n_tiles, nf),
        in_specs=[pl.BlockSpec(memory_space=pl.ANY),
                  pl.BlockSpec((1, d, tf), lambda i, f, te, *_: (te[i], 0, f)),
                  pl.BlockSpec((1, d, tf), lambda i, f, te, *_: (te[i], 0, f)),
                  pl.BlockSpec((1, tf, d), lambda i, f, te, *_: (te[i], f, 0))],
        out_specs=pl.BlockSpec((tm, d), lambda i, f, *_: (i, 0)),
        scratch_shapes=[pltpu.VMEM((2, n_rows, d), F32), pltpu.VMEM((tm, d), BF16),
                        pltpu.VMEM((tm, d), F32), pltpu.SemaphoreType.DMA((2,))],
    )
    return pl.pallas_call(
        functools.partial(_experts_body, rows_per_step=rows_per_step),
        grid_spec=grid_spec,
        out_shape=jax.ShapeDtypeStruct((n_tiles * tm, d), BF16),
        compiler_params=_params("arbitrary", "arbitrary"),
        name="moe_experts",
    )(tile_expert, n_used, src, h, wg, wu, wd)


def _combine_norm_body(x_ref, ya_ref, yb_ref, wt_ref, g_ref, o_ref):
    wt = wt_ref[...]
    y = wt[:, 0:1] * ya_ref[...].astype(F32) + wt[:, 1:2] * yb_ref[...].astype(F32)
    o_ref[...] = _rms(x_ref[...] + y, g_ref[...])


def combine_norm(x, ya, yb, wts, g, tm=1024):
    t, d = x.shape
    row = lambda i: (i, 0)
    return pl.pallas_call(
        _combine_norm_body,
        grid=(t // tm,),
        in_specs=[pl.BlockSpec((tm, d), row)] * 3 + [pl.BlockSpec((tm, LANES), row),
                                                     pl.BlockSpec((1, d), lambda i: (0, 0))],
        out_specs=pl.BlockSpec((tm, d), row),
        out_shape=jax.ShapeDtypeStruct((t, d), F32),
        compiler_params=_params("parallel"),
        name="combine_norm",
    )(x, ya, yb, wts, g.reshape(1, d))


def moe_layer(x, g_norm, router, wg, wu, wd, g_final, tm=1024):
    t, d = x.shape
    r = jnp.zeros((d, LANES), F32).at[:, :N_EXPERTS].set(router)
    r_hi = r.astype(BF16)
    r_lo = (r - r_hi.astype(F32)).astype(BF16)
    h, idx, wts = moe_route(x, g_norm, r_hi, r_lo)
    e_flat = jnp.concatenate([idx[:, 0], idx[:, 1]])
    onehot = (e_flat[:, None] == jnp.arange(N_EXPERTS)[None, :]).astype(jnp.int32)
    csum = jnp.cumsum(onehot, axis=0)
    rank = jnp.sum((csum - onehot) * onehot, axis=1)
    counts = csum[-1]
    padded = ((counts + tm - 1) // tm) * tm
    ends = jnp.cumsum(padded)
    starts = ends - padded
    dest = starts[e_flat] + rank
    n_tiles = (TOP_K * t) // tm + N_EXPERTS
    p = n_tiles * tm
    tok = jnp.concatenate([jnp.arange(t, dtype=jnp.int32)] * TOP_K)
    src = jnp.zeros((p,), jnp.int32).at[dest].set(tok)
    tile_expert = jnp.minimum(
        jnp.searchsorted(ends, jnp.arange(n_tiles, dtype=jnp.int32) * tm, side="right"),
        N_EXPERTS - 1).astype(jnp.int32)
    n_used = (ends[-1] // tm).astype(jnp.int32).reshape(1)
    ys = moe_experts(tile_expert, n_used, src, h, wg, wu, wd, tm)
    ya = jnp.take(ys, dest[:t], axis=0, mode="clip")
    yb = jnp.take(ys, dest[t:], axis=0, mode="clip")
    return combine_norm(x, ya, yb, wts, g_final)


SSM_TC = 512
SSM_SUB = 128
SSM_HALF = 256
SSM_NSTATE = SSM_GROUPS * SSM_STATE


def _ssm_body(u_ref, bw_ref, cw_ref, d_ref, glu_ref, ar_ref, ai_ref, pr_ref, pi_ref, o_ref,
              xr_ref, xi_ref, cr_ref, ci_ref):
    @pl.when(pl.program_id(1) == 0)
    def _():
        cr_ref[...] = jnp.zeros_like(cr_ref)
        ci_ref[...] = jnp.zeros_like(ci_ref)

    ub = u_ref[...]
    u = ub.astype(F32)
    nblk = SSM_WIDTH // SSM_HALF
    sblk = SSM_NSTATE // nblk
    for k in range(nblk):
        bu = jnp.dot(ub[:, k * SSM_HALF:(k + 1) * SSM_HALF], bw_ref[k], preferred_element_type=F32)
        xr_ref[:, k * sblk:(k + 1) * sblk] = bu[:, :sblk]
        xi_ref[:, k * sblk:(k + 1) * sblk] = bu[:, sblk:]

    row8 = lax.broadcasted_iota(jnp.int32, (SSM_SUB, LANES), 0) % 8
    n_grp = SSM_SUB // 8

    def strip(c, carry):
        col = pl.ds(pl.multiple_of(c * LANES, LANES), LANES)
        c_r = cr_ref[:, col]
        c_i = ci_ref[:, col]
        p_r = pr_ref[0:8, col]
        p_i = pi_ref[0:8, col]
        step_mul = [(jnp.where(row8 >= (1 << i), ar_ref[i:i + 1, col], 0.0),
                     jnp.where(row8 >= (1 << i), ai_ref[i:i + 1, col], 0.0)) for i in range(3)]
        for sub in range(SSM_TC // SSM_SUB):
            rows = slice(sub * SSM_SUB, (sub + 1) * SSM_SUB)
            xr = xr_ref[rows, col]
            xi = xi_ref[rows, col]
            for i in range(3):
                a_r, a_i = step_mul[i]
                sr, si = pltpu.roll(xr, 1 << i, 0), pltpu.roll(xi, 1 << i, 0)
                xr, xi = xr + a_r * sr - a_i * si, xi + a_r * si + a_i * sr
            out_r, out_i = [], []
            for r in range(n_grp):
                g_r = xr[8 * r:8 * r + 8] + p_r * c_r - p_i * c_i
                g_i = xi[8 * r:8 * r + 8] + p_r * c_i + p_i * c_r
                c_r, c_i = g_r[7:8, :], g_i[7:8, :]
                out_r.append(g_r)
                out_i.append(g_i)
            xr_ref[rows, col] = jnp.concatenate(out_r, axis=0)
            xi_ref[rows, col] = jnp.concatenate(out_i, axis=0)
        cr_ref[:, col] = c_r
        ci_ref[:, col] = c_i
        return carry

    lax.fori_loop(0, SSM_NSTATE // LANES, strip, 0)

    ys = []
    for k in range(nblk):
        st = jnp.concatenate([xr_ref[:, k * sblk:(k + 1) * sblk].astype(BF16),
                              xi_ref[:, k * sblk:(k + 1) * sblk].astype(BF16)], axis=1)
        ys.append(jnp.dot(st, cw_ref[k], preferred_element_type=F32))
    y = jax.nn.gelu(jnp.concatenate(ys, axis=1) + d_ref[...] * u)
    gated = y * jax.nn.sigmoid(jnp.dot(y.astype(BF16), glu_ref[...], preferred_element_type=F32))
    o_ref[...] = gated.astype(o_ref.dtype)


def ssm_mixer(proj, bs, a_re, a_im, log_dt, b_re, b_im, c_re, c_im, d_skip, glu_w):
    t = proj.shape[0]
    s = t // bs
    nt = s // SSM_TC
    assert s % SSM_TC == 0 and OFF_SSM % SSM_WIDTH == 0
    nblk = SSM_WIDTH // SSM_HALF
    gpb = SSM_GROUPS // nblk
    dt = jnp.exp(log_dt)[:, None]
    mag = jnp.exp(a_re * dt)
    abar_r, abar_i = mag * jnp.cos(a_im * dt), mag * jnp.sin(a_im * dt)
    nr, ni = abar_r - 1.0, abar_i
    den = a_re * a_re + a_im * a_im
    sr, si = (nr * a_re + ni * a_im) / den, (ni * a_re - nr * a_im) / den
    bbar_r = sr[..., None] * b_re - si[..., None] * b_im
    bbar_i = sr[..., None] * b_im + si[..., None] * b_re

    def powers(k):
        kk = k.astype(F32)[:, None, None]
        m = jnp.exp(kk * (a_re * dt))
        return ((m * jnp.cos(kk * (a_im * dt))).reshape(-1, SSM_NSTATE),
                (m * jnp.sin(kk * (a_im * dt))).reshape(-1, SSM_NSTATE))

    ar, ai = powers(2 ** jnp.arange(8))
    pr, pi = powers(jnp.arange(1, SSM_SUB + 1))
    eye = jnp.eye(gpb, dtype=F32)

    def in_block(w):
        return jnp.einsum('gnp,gh->gphn', w, eye).reshape(gpb * SSM_GROUP, gpb * SSM_STATE)

    def out_block(w):
        return jnp.einsum('gpn,gh->gnhp', w, eye).reshape(gpb * SSM_STATE, gpb * SSM_GROUP)

    bw = jnp.stack([jnp.concatenate([in_block(bbar_r[k * gpb:(k + 1) * gpb]),
                                     in_block(bbar_i[k * gpb:(k + 1) * gpb])], axis=1)
                    for k in range(nblk)]).astype(BF16)
    cw = jnp.stack([jnp.concatenate([out_block(c_re[k * gpb:(k + 1) * gpb]),
                                     -out_block(c_im[k * gpb:(k + 1) * gpb])], axis=0)
                    for k in range(nblk)]).astype(BF16)
    fixed2 = lambda b, i: (0, 0)
    fixed3 = lambda b, i: (0, 0, 0)
    return pl.pallas_call(
        _ssm_body,
        grid=(bs, nt),
        in_specs=[pl.BlockSpec((SSM_TC, SSM_WIDTH), lambda b, i: (b * nt + i, OFF_SSM // SSM_WIDTH)),
                  pl.BlockSpec(bw.shape, fixed3), pl.BlockSpec(cw.shape, fixed3),
                  pl.BlockSpec((1, SSM_WIDTH), fixed2), pl.BlockSpec((SSM_WIDTH, SSM_WIDTH), fixed2),
                  pl.BlockSpec((8, SSM_NSTATE), fixed2), pl.BlockSpec((8, SSM_NSTATE), fixed2),
                  pl.BlockSpec((SSM_SUB, SSM_NSTATE), fixed2), pl.BlockSpec((SSM_SUB, SSM_NSTATE), fixed2)],
        out_specs=pl.BlockSpec((SSM_TC, SSM_WIDTH), lambda b, i: (b * nt + i, 0)),
        out_shape=jax.ShapeDtypeStruct((t, SSM_WIDTH), BF16),
        scratch_shapes=[pltpu.VMEM((SSM_TC, SSM_NSTATE), F32), pltpu.VMEM((SSM_TC, SSM_NSTATE), F32),
                        pltpu.VMEM((1, SSM_NSTATE), F32), pltpu.VMEM((1, SSM_NSTATE), F32)],
        compiler_params=_params("parallel", "arbitrary"),
        name="ssm_scan",
    )(proj, bw, cw, d_skip.reshape(1, SSM_WIDTH), glu_w.astype(BF16), ar, ai, pr, pi)


GDN_TC = 256
GDN_HALO = 8


def _gdn_body(nega_ref, dtb_ref, q_ref, k_ref, v_ref, z_ref, sm_ref, wq_ref, wk_ref, wv_ref, ng_ref,
              o_ref, state_ref, hq_ref, hk_ref, hv_ref, vnew_ref):
    TC, CH, DK, H = GDN_TC, GDN_CHUNK, GDN_DK, GDN_HEADS
    heads = range(H)

    @pl.when(pl.program_id(1) == 0)
    def _():
        state_ref[...] = jnp.zeros_like(state_ref)
        hq_ref[...] = jnp.zeros_like(hq_ref)
        hk_ref[...] = jnp.zeros_like(hk_ref)
        hv_ref[...] = jnp.zeros_like(hv_ref)

    def conv_silu(x_ref, halo_ref, w_ref):
        x = x_ref[...].astype(F32)
        xx = jnp.concatenate([halo_ref[...], x], axis=0)
        w = w_ref[...]
        y = w[GDN_CONV - 1:GDN_CONV] * x
        for d in range(1, GDN_CONV):
            y = y + w[GDN_CONV - 1 - d:GDN_CONV - d] * pltpu.roll(xx, d, 0)[GDN_HALO:GDN_HALO + TC]
        halo_ref[...] = x[TC - GDN_HALO:TC]
        return y * jax.nn.sigmoid(y)

    def per_head(x):
        return [x[:, h * LANES:(h + 1) * LANES] for h in heads]

    q = per_head(conv_silu(q_ref, hq_ref, wq_ref))
    k = per_head(conv_silu(k_ref, hk_ref, wk_ref))
    v = per_head(conv_silu(v_ref, hv_ref, wv_ref))
    q = [x * lax.rsqrt(jnp.sum(x * x, axis=-1, keepdims=True) + 1e-6) * (DK ** -0.5) for x in q]
    k = [x * lax.rsqrt(jnp.sum(x * x, axis=-1, keepdims=True) + 1e-6) for x in k]

    small = sm_ref[...].astype(F32)
    beta = [jax.nn.sigmoid(small[:, H + h:H + h + 1]) for h in heads]
    la = []
    for h in heads:
        xa = small[:, h:h + 1] + dtb_ref[h]
        softplus = jnp.maximum(xa, 0.0) + jnp.log(1.0 + jnp.exp(-jnp.abs(xa)))
        la.append(jnp.broadcast_to(nega_ref[h] * softplus, (TC, LANES)))

    ri = lax.broadcasted_iota(jnp.int32, (TC, TC), 0)
    ci = lax.broadcasted_iota(jnp.int32, (TC, TC), 1)
    same = (ri // CH) == (ci // CH)
    causal = jnp.where(same, jnp.where(ci <= ri, 1.0, 0.0), 0.0)
    strict = jnp.where(ci < ri, causal, 0.0)
    eye = jnp.where(ri == ci, 1.0, 0.0)
    tri = causal.astype(BF16)
    lane = lax.broadcasted_iota(jnp.int32, (TC, LANES), 1)

    g = []
    for h in heads:
        la_hi, la_lo = _split_bf16(la[h])
        g.append(jnp.dot(tri, la_hi, preferred_element_type=F32)
                 + jnp.dot(tri, la_lo, preferred_element_type=F32))
    decay = []
    for h in heads:
        g_hi = g[h].astype(BF16).astype(F32)
        g_lo = g[h] - g_hi
        lhs = jnp.where(lane == 0, g_hi, jnp.where(lane == 1, g_lo, jnp.where(lane < 4, 1.0, 0.0)))
        rhs = jnp.where(lane < 2, 1.0, jnp.where(lane == 2, -g_hi, jnp.where(lane == 3, -g_lo, 0.0)))
        decay.append(jnp.exp(jnp.where(causal > 0.5, _dot_nt(lhs.astype(BF16), rhs.astype(BF16)), MASKED)))

    kb = [k[h] * beta[h] for h in heads]
    k_b = [x.astype(BF16) for x in k]
    lmat = [strict * _dot_nt(kb[h].astype(BF16), k_b[h]) * decay[h] for h in heads]
    tinv = [eye - x for x in lmat]
    pw = lmat
    for _ in range(CH.bit_length() - 2):
        pw = [jnp.dot(x.astype(BF16), x.astype(BF16), preferred_element_type=F32) for x in pw]
        tinv = [tinv[h] + jnp.dot(tinv[h].astype(BF16), pw[h].astype(BF16), preferred_element_type=F32)
                for h in heads]
    eg = [jnp.exp(x) for x in g]
    sol = [jnp.dot(tinv[h].astype(BF16),
                   jnp.concatenate([v[h] * beta[h], kb[h] * eg[h]], axis=1).astype(BF16),
                   preferred_element_type=F32) for h in heads]
    attn = [(causal * _dot_nt(q[h].astype(BF16), k_b[h]) * decay[h]).astype(BF16) for h in heads]
    q_st = [(q[h] * eg[h]).astype(BF16) for h in heads]

    vnew_ref[...] = jnp.zeros_like(vnew_ref)
    for c in range(TC // CH):
        rows = slice(c * CH, (c + 1) * CH)
        g_last = [x[(c + 1) * CH - 1:(c + 1) * CH, :] for x in g]
        st = [state_ref[h] for h in heads]
        st_b = [x.astype(BF16) for x in st]
        v_new = [sol[h][rows, :GDN_DV]
                 - jnp.dot(sol[h][rows, GDN_DV:].astype(BF16), st_b[h], preferred_element_type=F32)
                 for h in heads]
        for h in heads:
            vnew_ref[h, rows, :] = v_new[h].astype(BF16)
        o_c = [jnp.dot(q_st[h][rows], st_b[h], preferred_element_type=F32)
               + jnp.dot(attn[h][rows], vnew_ref[h], preferred_element_type=F32) for h in heads]
        for h in heads:
            k_st = (k[h][rows] * jnp.exp(g_last[h] - g[h][rows])).astype(BF16)
            state_ref[h] = st[h] * jnp.exp(g_last[h][:, :1]) + _dot_tn(k_st, v_new[h].astype(BF16))
        for h in heads:
            o = o_c[h] * lax.rsqrt(jnp.mean(o_c[h] * o_c[h], axis=-1, keepdims=True) + RMS_EPS) * ng_ref[...]
            zc = z_ref[rows, h * LANES:(h + 1) * LANES].astype(F32)
            o_ref[rows, h * LANES:(h + 1) * LANES] = (o * (zc * jax.nn.sigmoid(zc))).astype(o_ref.dtype)


def gdn_mixer(proj, bs, conv_w, a_log, dt_bias, norm_g):
    t = proj.shape[0]
    s = t // bs
    nt = s // GDN_TC
    H = GDN_HEADS
    hw = H * LANES
    assert s % GDN_TC == 0 and GDN_DK == LANES and GDN_DV == LANES and OFF_QKV == 0 and OFF_Z % hw == 0
    cw = jnp.zeros((8, GDN_QKV), F32).at[:GDN_CONV].set(conv_w)
    tok = lambda blk: pl.BlockSpec((GDN_TC, hw), lambda b, i: (b * nt + i, blk))
    wspec = lambda blk: pl.BlockSpec((8, hw), lambda b, i: (0, blk))
    smem = pl.BlockSpec(memory_space=pltpu.SMEM)
    return pl.pallas_call(
        _gdn_body,
        grid=(bs, nt),
        in_specs=[smem, smem, tok(0), tok(1), tok(2), tok(OFF_Z // hw),
                  pl.BlockSpec((GDN_TC, 2 * LANES), lambda b, i: (b * nt + i, OFF_SMALL // (2 * LANES))),
                  wspec(0), wspec(1), wspec(2), pl.BlockSpec((1, GDN_DV), lambda b, i: (0, 0))],
        out_specs=pl.BlockSpec((GDN_TC, hw), lambda b, i: (b * nt + i, 0)),
        out_shape=jax.ShapeDtypeStruct((t, hw), BF16),
        scratch_shapes=[pltpu.VMEM((H, GDN_DK, GDN_DV), F32)] + [pltpu.VMEM((GDN_HALO, hw), F32)] * 3
        + [pltpu.VMEM((H, GDN_TC, GDN_DV), BF16)],
        compiler_params=_params("parallel", "arbitrary"),
        name="gdn_delta",
    )(-jnp.exp(a_log), dt_bias.astype(F32), proj, proj, proj, proj, proj, cw, cw, cw,
      norm_g.reshape(1, GDN_DV))


NSA_KT = 256
NSA_NEG = MASKED
NSA_WTILES = NSA_WINDOW // LANES + 1


def _bucket_table():
    n = np.arange(LANES)
    max_exact = REL_BUCKETS // 2
    nf = np.maximum(n, 1).astype(np.float32)
    large = max_exact + (np.log(nf / np.float32(max_exact)) / np.float32(math.log(REL_MAX_DIST / max_exact))
                         * np.float32(REL_BUCKETS - max_exact)).astype(np.int32)
    tbl = np.where(n < max_exact, n, np.minimum(large, REL_BUCKETS - 1))
    assert tbl[-1] == REL_BUCKETS - 1 and REL_MAX_DIST <= LANES
    return tbl


def _compress_body(x_ref, pe_ref, w1_ref, w2_ref, o_ref):
    x = x_ref[0, 0, 0]
    w1 = w1_ref[0]
    half = NSA_CMP_STRIDE * NSA_DH
    nc = x.shape[0]
    a = jnp.dot(x, w1[:half], preferred_element_type=F32)
    b = jnp.dot(x, w1[half:], preferred_element_type=F32)
    pe_h = jnp.dot(pe_ref[0], w1, preferred_element_type=F32)
    hid = a + pltpu.roll(b, nc - 1, 0) + pe_h[0:1, :]
    o_ref[0, 0, 0] = jnp.dot(jax.nn.gelu(hid).astype(BF16), w2_ref[0], preferred_element_type=F32)


def nsa_compress(x, pe, w1, w2):
    _, bs, g, nc, width = x.shape
    return pl.pallas_call(
        _compress_body,
        grid=(2, bs, g),
        in_specs=[pl.BlockSpec((1, 1, 1, nc, width), lambda i, b, j: (i, b, j, 0, 0)),
                  pl.BlockSpec((1,) + pe.shape[1:], lambda i, b, j: (i, 0, 0)),
                  pl.BlockSpec((1,) + w1.shape[1:], lambda i, b, j: (i, 0, 0)),
                  pl.BlockSpec((1,) + w2.shape[1:], lambda i, b, j: (i, 0, 0))],
        out_specs=pl.BlockSpec((1, 1, 1, nc, NSA_DH), lambda i, b, j: (i, b, j, 0, 0)),
        out_shape=jax.ShapeDtypeStruct((2, bs, g, nc, NSA_DH), F32),
        compiler_params=_params("parallel", "parallel", "parallel"),
        name="nsa_compress",
    )(x, pe, w1, w2)


def _nsa_body(nfast, std, near_w, near_c, q_ref, gl_ref, qp_ref, kp_ref, cp_ref,
              ks_ref, vs_ref, kw_ref, vw_ref, kc_ref, vc_ref, agg_ref, td_ref, o_ref,
              m_ref, l_ref, acc_ref, sa_ref, sb_ref, sc_ref, *, n_sel, sel_k):
    QB, HG, G = NSA_QBLOCK, NSA_HPG, NSA_KV_GROUPS
    groups = range(G)
    qi = pl.program_id(1)
    s0 = qi * QB
    nc = kc_ref.shape[2]
    n_ct = nc // LANES
    q = [q_ref[0, g, 0] for g in groups]
    qp = qp_ref[...]

    def lanes4(x):
        return jnp.concatenate([x] * HG, axis=1)

    def delta_t(kp):
        idx = jnp.clip(_dot_nt(kp, qp), 0.0, float(LANES - 1)).astype(jnp.int32)
        outs = []
        for g in groups:
            heads = []
            for hg in range(HG):
                tb = jnp.broadcast_to(td_ref[g * HG + hg:g * HG + hg + 1, :], idx.shape)
                heads.append(jnp.take_along_axis(tb, idx, axis=1))
            outs.append(jnp.concatenate(heads, axis=1))
        return outs

    def maybe_delta(flag, kp):
        zeros = jnp.zeros((kp.shape[0], HG * QB), F32)
        return lax.cond(flag != 0, lambda: tuple(delta_t(kp)), lambda: (zeros,) * G)

    gate = [jax.nn.sigmoid(gl_ref[0, g, 0]) for g in groups]

    for g in groups:
        sc_ref[g] = _dot_nt(kc_ref[0, g], q[g])
    near_rows = 4 * 8

    @pl.when(std[qi] != 0)
    def _():
        c0 = pl.multiple_of(jnp.minimum(qi * (QB // NSA_CMP_STRIDE) - near_rows // 2, nc - near_rows), 8)
        d = delta_t(cp_ref[pl.ds(c0, near_rows), :])
        for g in groups:
            sc_ref[g, pl.ds(c0, near_rows), :] += d[g]

    @pl.when(std[qi] == 0)
    def _():
        for ct in range(n_ct):
            @pl.when(near_c[qi * n_ct + ct] != 0)
            def _():
                d = delta_t(cp_ref[ct * LANES:(ct + 1) * LANES, :])
                for g in groups:
                    sc_ref[g, ct * LANES:(ct + 1) * LANES, :] += d[g]

    c_id = lax.broadcasted_iota(jnp.int32, (nc, QB), 0)
    c_q = lax.broadcasted_iota(jnp.int32, (nc, QB), 1)
    ok_c = lanes4(jnp.where(c_id * NSA_CMP_STRIDE + (NSA_CMP_LEN - 1) <= s0 + c_q, 1.0, 0.0))
    blk = lax.broadcasted_iota(jnp.int32, (LANES, QB), 0)
    t_tok = s0 + lax.broadcasted_iota(jnp.int32, (LANES, QB), 1)
    tb_blk = t_tok // NSA_SEL_LEN
    forced = jnp.where(blk == 0, 1.0, 0.0) + jnp.where(blk == tb_blk, 1.0, 0.0) \
        + jnp.where(blk == tb_blk - 1, 1.0, 0.0)
    blkf = blk.astype(F32)
    out, score = [], []
    for g in groups:
        s = jnp.where(ok_c > 0.5, sc_ref[g], NSA_NEG)
        e = jnp.exp2(s - jnp.max(s, axis=0, keepdims=True)) * ok_c
        p = e * (1.0 / jnp.maximum(jnp.sum(e, axis=0, keepdims=True), 1e-30))
        out.append(gate[g][0:1, :] * jnp.dot(vc_ref[0, g], p.astype(BF16), preferred_element_type=F32))
        ps = p[:, 0:QB]
        for hg in range(1, HG):
            ps = ps + p[:, hg * QB:(hg + 1) * QB]
        ps_hi, ps_lo = _split_bf16(ps)
        imp = (jnp.dot(agg_ref[...], ps_hi, preferred_element_type=F32)
               + jnp.dot(agg_ref[...], ps_lo, preferred_element_type=F32))
        sco = jnp.where(forced > 0.5, 1e9, jnp.where(blk * NSA_SEL_LEN <= t_tok, imp, -1e9))
        score.append(jnp.where(blk < n_sel, sco, -jnp.inf))
    sel = [jnp.zeros((LANES, QB), F32) for _ in groups]
    for _ in range(sel_k):
        for g in groups:
            mx = jnp.max(score[g], axis=0, keepdims=True)
            first = jnp.min(jnp.where(score[g] == mx, blkf, float(LANES)), axis=0, keepdims=True)
            hit = blkf == first
            sel[g] = jnp.where(hit, 1.0, sel[g])
            score[g] = jnp.where(hit, -jnp.inf, score[g])
    q2 = []
    for g in groups:
        sel_q = jnp.transpose(jnp.where(sel[g] > 0.5, 0.0, NSA_NEG))
        q2.append(jnp.concatenate([q[g], jnp.concatenate([sel_q] * HG, axis=0).astype(BF16)], axis=1))

    SEL, WIN = 0, 1

    def reset():
        m_ref[...] = jnp.full_like(m_ref, NSA_NEG)
        l_ref[...] = jnp.zeros_like(l_ref)
        acc_ref[...] = jnp.zeros_like(acc_ref)

    def tile_step(br, g, k, s, v_t):
        m_old = m_ref[br, g, k]
        m_new = jnp.maximum(m_old, jnp.max(s, axis=0, keepdims=True))
        alpha = jnp.exp2(m_old - m_new)
        p = jnp.exp2(s - m_new)
        l_ref[br, g, k] = l_ref[br, g, k] * alpha + jnp.sum(p, axis=0, keepdims=True)
        acc_ref[br, g, k] = (acc_ref[br, g, k] * alpha
                             + jnp.dot(v_t, p.astype(BF16), preferred_element_type=F32))
        m_ref[br, g, k] = m_new

    def finish(br, g):
        m = jnp.maximum(m_ref[br, g, 0], m_ref[br, g, 1])
        w0, w1 = jnp.exp2(m_ref[br, g, 0] - m), jnp.exp2(m_ref[br, g, 1] - m)
        return ((acc_ref[br, g, 0] * w0 + acc_ref[br, g, 1] * w1)
                * (1.0 / (l_ref[br, g, 0] * w0 + l_ref[br, g, 1] * w1)))

    def emit():
        heads_out = []
        for g in groups:
            o_t = out[g] + gate[g][1:2, :] * finish(SEL, g) + gate[g][2:3, :] * finish(WIN, g)
            heads_out += [jnp.transpose(o_t[:, hg * QB:(hg + 1) * QB]) for hg in range(HG)]
        o_ref[0] = jnp.concatenate(heads_out, axis=1).astype(o_ref.dtype)

    key_r = lax.broadcasted_iota(jnp.int32, (NSA_KT, QB), 0)
    key_q = lax.broadcasted_iota(jnp.int32, (NSA_KT, QB), 1)
    w_r = lax.broadcasted_iota(jnp.int32, (LANES, QB), 0)
    w_q = lax.broadcasted_iota(jnp.int32, (LANES, QB), 1)
    in_window = lanes4(jnp.where(w_r > w_q, 0.0, NSA_NEG))
    causal_w = lanes4(jnp.where(w_r <= w_q, 0.0, NSA_NEG))

    def sel_at(k0, n):
        k0 = pl.multiple_of(k0, LANES)
        return [(_dot_nt(ks_ref[0, g, pl.ds(k0, n), :], q2[g]), vs_ref[0, g, :, pl.ds(k0, n)])
                for g in groups]

    def win_at(k0, n):
        k0 = pl.multiple_of(k0, LANES)
        return [(_dot_nt(kw_ref[0, g, pl.ds(k0, n), :], q[g]), vw_ref[0, g, :, pl.ds(k0, n)])
                for g in groups]

    def delta_at(k0, n):
        return delta_t(kp_ref[pl.ds(pl.multiple_of(k0, LANES), n), :])

    def pair_scores(p, buf):
        for t in range(2):
            sc_t = sel_at((2 * p + t) * NSA_KT, NSA_KT)
            for g in groups:
                buf[g, t] = sc_t[g][0]

    def pair_softmax(p, buf):
        for t in range(2):
            k0 = pl.multiple_of((2 * p + t) * NSA_KT, NSA_KT)
            for g in groups:
                tile_step(SEL, g, t, buf[g, t], vs_ref[0, g, :, pl.ds(k0, NSA_KT)])

    def run_pairs(n_pairs):
        last = jnp.maximum(n_pairs - 1, 0)
        pair_scores(0, sa_ref)

        def two_pairs(j, carry):
            pair_scores(2 * j + 1, sb_ref)
            pair_softmax(2 * j, sa_ref)
            pair_scores(jnp.minimum(2 * j + 2, last), sa_ref)
            pair_softmax(2 * j + 1, sb_ref)
            return carry

        lax.fori_loop(0, n_pairs // 2, two_pairs, 0)

        @pl.when(n_pairs % 2 == 1)
        def _():
            pair_softmax(n_pairs - 1, sa_ref)

    @pl.when(std[qi] != 0)
    def _():
        reset()
        nb = (qi - 1) // 2
        run_pairs(nb // 2)

        @pl.when(nb % 2 == 1)
        def _():
            a = sel_at((nb - 1) * NSA_KT, NSA_KT)
            for g in groups:
                tile_step(SEL, g, 0, *a[g])

        @pl.when(qi % 2 == 0)
        def _():
            a = sel_at(nb * NSA_KT, LANES)
            for g in groups:
                tile_step(SEL, g, 1, *a[g])

        zeros = jnp.zeros((LANES, HG * QB), F32)
        sel_n = sel_at(s0 - LANES, 2 * LANES)
        win_p = win_at(s0 - NSA_WINDOW, NSA_WINDOW)
        win_d = win_at(s0, LANES)
        d_prev, d_diag = delta_at(s0 - LANES, LANES), delta_at(s0, LANES)
        for g in groups:
            bias = jnp.concatenate([d_prev[g], d_diag[g] + causal_w], axis=0)
            tile_step(SEL, g, 0, sel_n[g][0] + bias, sel_n[g][1])
        for g in groups:
            bias = jnp.concatenate([in_window] + [zeros] * (NSA_WTILES - 3) + [d_prev[g]], axis=0)
            tile_step(WIN, g, 0, win_p[g][0] + bias, win_p[g][1])
        for g in groups:
            tile_step(WIN, g, 1, win_d[g][0] + d_diag[g] + causal_w, win_d[g][1])
        emit()

    @pl.when(std[qi] == 0)
    def _():
        reset()
        n_past = (s0 + QB - 1) // NSA_KT
        n_pairs = nfast[qi] // 2
        run_pairs(n_pairs)

        def slow_body(kt, carry):
            a = sel_at(kt * NSA_KT, NSA_KT)
            d = delta_at(kt * NSA_KT, NSA_KT)
            for g in groups:
                tile_step(SEL, g, 1, a[g][0] + d[g], a[g][1])
            return carry

        lax.fori_loop(2 * n_pairs, n_past, slow_body, 0)
        k0 = n_past * NSA_KT
        a = sel_at(k0, NSA_KT)
        d = delta_at(k0, NSA_KT)
        causal = lanes4(jnp.where(k0 + key_r <= s0 + key_q, 0.0, NSA_NEG))
        for g in groups:
            tile_step(SEL, g, 0, a[g][0] + d[g] + causal, a[g][1])

        for j in range(NSA_WTILES):
            start = s0 - NSA_WINDOW + j * LANES

            @pl.when(start >= 0)
            def _():
                sw = win_at(start, LANES)
                d = maybe_delta(near_w[qi * NSA_WTILES + j],
                                kp_ref[pl.ds(pl.multiple_of(start, LANES), LANES), :])
                for g in groups:
                    s = sw[g][0] + d[g]
                    if j == 0:
                        s = s + in_window
                    elif j == NSA_WTILES - 1:
                        s = s + causal_w
                    tile_step(WIN, g, j % 2, s, sw[g][1])

        emit()


def _pos_digits(p, key_side):
    d0, d1, d2 = (p & 127).astype(F32), ((p >> 7) & 127).astype(F32), (p >> 14).astype(F32)
    one = jnp.ones_like(d0)
    if key_side:
        cols = [one, one, one, -d2, -d1, -d0]
    else:
        cols = [16384.0 * d2, 128.0 * d1, d0, 16384.0 * one, 128.0 * one, one]
    out = jnp.stack(cols, axis=-1)
    return jnp.pad(out, ((0, 0), (0, LANES - out.shape[-1]))).astype(BF16)


def nsa_mixer(q, k_c, v_c, k_s, v_s, k_w, v_w, gate_logits, positions,
              ck_pe, ck_w1, ck_w2, cv_pe, cv_w1, cv_w2, rel_bias):
    q, k_c, v_c, k_s, v_s, k_w, v_w = lax.optimization_barrier((q, k_c, v_c, k_s, v_s, k_w, v_w))
    bs, s, _ = q.shape
    G, HG, DH, QB = NSA_KV_GROUPS, NSA_HPG, NSA_DH, NSA_QBLOCK
    nqt = s // QB
    nc = s // NSA_CMP_STRIDE
    n_cmp = (s - NSA_CMP_LEN) // NSA_CMP_STRIDE + 1
    n_sel = s // NSA_SEL_LEN
    sel_k = min(NSA_SEL_TOPK, n_sel)
    n_kt = s // NSA_KT
    n_ct = nc // LANES
    assert s % (LANES * NSA_CMP_STRIDE) == 0 and n_sel <= LANES and NSA_KT == 2 * QB

    def by_group(t):
        return t.reshape(bs, s, G, DH).transpose(0, 2, 1, 3).astype(F32)

    def by_group_t(t):
        return t.reshape(bs, s, G, DH).transpose(0, 2, 3, 1).astype(BF16)

    ones2 = jnp.zeros((LANES - DH,), F32).at[:2].set(1.0)

    def keys_aug(t, blocks=False):
        parts = [t, jnp.broadcast_to(ones2, t.shape[:-1] + (LANES - DH,))]
        if blocks:
            onehot = (jnp.arange(s)[:, None] // NSA_SEL_LEN == jnp.arange(LANES)[None, :]).astype(F32)
            parts.append(jnp.broadcast_to(onehot, t.shape[:-2] + (s, LANES)))
        return jnp.concatenate(parts, axis=-1).astype(BF16)

    chunks = jnp.stack([by_group(k_c), by_group(v_c)]).reshape(2, bs, G, nc, NSA_CMP_STRIDE * DH)
    pe = jnp.stack([ck_pe, cv_pe]).reshape(2, 1, NSA_CMP_LEN * DH)
    cmp = nsa_compress(chunks.astype(BF16), jnp.broadcast_to(pe, (2, 8, NSA_CMP_LEN * DH)).astype(BF16),
                       jnp.stack([ck_w1, cv_w1]).astype(BF16), jnp.stack([ck_w2, cv_w2]).astype(BF16))
    kc, vc_t = keys_aug(cmp[0]), cmp[1].transpose(0, 1, 3, 2).astype(BF16)

    log2e = math.log2(math.e)
    rel_bias = rel_bias.astype(F32) * log2e
    far = rel_bias[REL_BUCKETS - 1]
    far_hi = far.astype(BF16).astype(F32)
    extra = jnp.zeros((NSA_HEADS, LANES - DH), F32).at[:, 0].set(far_hi).at[:, 1].set(far - far_hi)
    qh = (q.astype(F32) * (DH ** -0.5 * log2e)).reshape(bs, nqt, QB, G, HG, DH).transpose(0, 3, 1, 4, 2, 5)
    ex = jnp.broadcast_to(extra.reshape(1, G, 1, HG, 1, LANES - DH), (bs, G, nqt, HG, QB, LANES - DH))
    qs = jnp.concatenate([qh, ex], axis=-1).astype(BF16).reshape(bs, G, nqt, HG * QB, LANES)
    gl = gate_logits.astype(F32).reshape(bs, nqt, QB, G, HG, 3).transpose(0, 3, 1, 5, 4, 2)
    gl = jnp.pad(gl.reshape(bs, G, nqt, 3, HG * QB), ((0, 0),) * 3 + ((0, 5), (0, 0)))

    td = (rel_bias[_bucket_table()] - rel_bias[REL_BUCKETS - 1][None, :]).T.astype(F32)
    cmp_end = jnp.minimum(jnp.arange(nc) * NSA_CMP_STRIDE + NSA_CMP_LEN - 1, s - 1)
    cpos = positions[cmp_end]
    qmin = positions.reshape(nqt, QB).min(axis=1)
    kmax = positions.reshape(nqt, QB).max(axis=1)
    near_s = (qmin[:, None] - positions.reshape(n_kt, NSA_KT).max(axis=1)[None, :]) < REL_MAX_DIST
    n_past = (jnp.arange(nqt) * QB + QB - 1) // NSA_KT
    n_fast = jnp.minimum(jnp.argmax(jnp.concatenate([near_s, jnp.ones((nqt, 1), bool)], axis=1), axis=1),
                         n_past)
    wt = jnp.clip(jnp.arange(nqt)[:, None] - (NSA_WTILES - 1) + jnp.arange(NSA_WTILES)[None, :], 0, nqt - 1)
    near_w = (qmin[:, None] - kmax[wt]) < REL_MAX_DIST
    near_c = (qmin[:, None] - cpos.reshape(n_ct, LANES).max(axis=1)[None, :]) < REL_MAX_DIST
    before = lax.cummax(kmax)[jnp.maximum(jnp.arange(nqt) - 2, 0)]
    std = (jnp.arange(nqt) >= NSA_WTILES - 1) & (qmin - before >= REL_MAX_DIST)

    cmp_start = np.arange(nc) * NSA_CMP_STRIDE
    sel_start = np.arange(LANES) * NSA_SEL_LEN
    agg_t = ((cmp_start[None, :] <= sel_start[:, None] + NSA_SEL_LEN - 1)
             & (cmp_start[None, :] + NSA_CMP_LEN - 1 >= sel_start[:, None])
             & (np.arange(nc)[None, :] < n_cmp) & (np.arange(LANES)[:, None] < n_sel))

    cols = HG * QB
    full = lambda shape: pl.BlockSpec(shape, lambda b, i, *_: (0,) * len(shape))
    per_b = lambda n, w: pl.BlockSpec((1, G, n, w), lambda b, i, *_: (b, 0, 0, 0))
    per_tile = lambda n, w: pl.BlockSpec((1, G, 1, n, w), lambda b, i, *_: (b, 0, i, 0, 0))
    grid_spec = pltpu.PrefetchScalarGridSpec(
        num_scalar_prefetch=4,
        grid=(bs, nqt),
        in_specs=[per_tile(cols, LANES), per_tile(8, cols),
                  pl.BlockSpec((QB, LANES), lambda b, i, *_: (i, 0)),
                  full((s, LANES)), full((nc, LANES)),
                  per_b(s, 2 * LANES), per_b(DH, s), per_b(s, LANES), per_b(DH, s),
                  per_b(nc, LANES), per_b(DH, nc),
                  full((LANES, nc)), full((NSA_HEADS, LANES))],
        out_specs=pl.BlockSpec((1, QB, NSA_Q), lambda b, i, *_: (b, i, 0)),
        scratch_shapes=[pltpu.VMEM((2, G, 2, 1, cols), F32), pltpu.VMEM((2, G, 2, 1, cols), F32),
                        pltpu.VMEM((2, G, 2, DH, cols), F32)]
        + [pltpu.VMEM((G, 2, NSA_KT, cols), F32)] * 2 + [pltpu.VMEM((G, nc, cols), F32)],
    )
    return pl.pallas_call(
        functools.partial(_nsa_body, n_sel=n_sel, sel_k=sel_k),
        grid_spec=grid_spec,
        out_shape=jax.ShapeDtypeStruct((bs, s, NSA_Q), BF16),
        compiler_params=_params("parallel", "arbitrary"),
        name="nsa_attention",
    )(n_fast.astype(jnp.int32), std.astype(jnp.int32), near_w.reshape(-1).astype(jnp.int32),
      near_c.reshape(-1).astype(jnp.int32),
      qs, gl, _pos_digits(positions, False), _pos_digits(positions, True), _pos_digits(cpos, True),
      keys_aug(by_group(k_s), blocks=True), by_group_t(v_s), keys_aug(by_group(k_w)), by_group_t(v_w),
      kc, vc_t, jnp.asarray(agg_t, BF16), td)


SGU_TC = 512


def _sgu_body(uv_ref, lg_ref, lb_ref, w_ref, b_ref, o_ref):
    uv = jax.nn.gelu(uv_ref[...].astype(F32))
    u, v = uv[:, :SGU_WIDTH], uv[:, SGU_WIDTH:]
    mu = jnp.mean(v, axis=-1, keepdims=True)
    vc = v - mu
    var = jnp.mean(vc * vc, axis=-1, keepdims=True)
    vn = (vc * lax.rsqrt(var + RMS_EPS) * lg_ref[...] + lb_ref[...]).astype(BF16)
    gw = SGU_WIDTH // SGU_GROUPS
    for c in range(SGU_TC // SGU_CHUNK):
        rows = slice(c * SGU_CHUNK, (c + 1) * SGU_CHUNK)
        for g in range(SGU_GROUPS):
            cols = slice(g * gw, (g + 1) * gw)
            mix = jnp.dot(w_ref[g], vn[rows, cols], preferred_element_type=F32) + b_ref[:, g:g + 1]
            o_ref[rows, cols] = (u[rows, cols] * mix).astype(o_ref.dtype)


def sgu_mixer(proj, ln_g, ln_b, w_s, b_s):
    t = proj.shape[0]
    assert t % SGU_TC == 0 and OFF_SGU % (2 * SGU_WIDTH) == 0
    fixed = lambda i: (0, 0)
    return pl.pallas_call(
        _sgu_body,
        grid=(t // SGU_TC,),
        in_specs=[pl.BlockSpec((SGU_TC, 2 * SGU_WIDTH), lambda i: (i, OFF_SGU // (2 * SGU_WIDTH))),
                  pl.BlockSpec((1, SGU_WIDTH), fixed), pl.BlockSpec((1, SGU_WIDTH), fixed),
                  pl.BlockSpec((SGU_GROUPS, SGU_CHUNK, SGU_CHUNK), lambda i: (0, 0, 0)),
                  pl.BlockSpec((SGU_CHUNK, SGU_GROUPS), fixed)],
        out_specs=pl.BlockSpec((SGU_TC, SGU_WIDTH), lambda i: (i, 0)),
        out_shape=jax.ShapeDtypeStruct((t, SGU_WIDTH), BF16),
        compiler_params=_params("parallel"),
        name="sgu_gate",
    )(proj, ln_g.reshape(1, SGU_WIDTH), ln_b.reshape(1, SGU_WIDTH), jnp.tril(w_s).astype(BF16), b_s.T)


def _permute_w_in(w):
    sizes = ([SSM_WIDTH, GDN_QKV, GDN_HEADS, GDN_HEADS, GDN_HEADS * GDN_DV, NSA_Q]
             + [NSA_KV] * 6 + [3 * NSA_HEADS, 2 * SGU_WIDTH, N_BRANCH * D_MODEL])
    cuts = [int(c) for c in np.cumsum(sizes)[:-1]]
    (w_ssm, w_qkv, w_a, w_b, w_z, w_nq, w_kc, w_vc, w_ks, w_vs, w_kw, w_vw,
     w_ng, w_sgu, w_gate) = jnp.split(w, cuts, axis=-1)
    pad = jnp.zeros((w.shape[0], PROJ_COLS - OFF_SMALL - SMALL_USED), w.dtype)
    return jnp.concatenate([w_qkv, w_ssm, w_z, w_nq, w_sgu, w_gate, w_kc, w_vc, w_ks, w_vs,
                            w_kw, w_vw, w_a, w_b, w_ng, pad], axis=-1).astype(BF16)


def kernel(x, positions, norm_mix, norm_ffn, norm_final, w_in, ssm_a_re, ssm_a_im, ssm_log_dt, ssm_b_re, ssm_b_im, ssm_c_re, ssm_c_im, ssm_d, ssm_glu_w, gdn_conv_w, gdn_a_log, gdn_dt_bias, gdn_norm, nsa_cmp_k_pe, nsa_cmp_k_w1, nsa_cmp_k_w2, nsa_cmp_v_pe, nsa_cmp_v_w1, nsa_cmp_v_w2, rel_bias, sgu_ln_g, sgu_ln_b, sgu_w, sgu_b, w_br_ssm, w_br_gdn, w_br_nsa, w_br_sgu, w_out, ffn_w_gate, ffn_w_up, ffn_w_down, moe_router, moe_w_gate, moe_w_up, moe_w_down):
    bs, s, d = x.shape
    t = bs * s
    assert DEPTH == 2
    xf = x.reshape(t, d)
    for layer in range(DEPTH):
        proj = in_proj(xf, norm_mix[layer], _permute_w_in(w_in[layer]))
        p3 = proj.reshape(bs, s, PROJ_COLS)

        def seg(off, width):
            return p3[..., off:off + width]

        y_ssm = ssm_mixer(proj, bs, ssm_a_re[layer], ssm_a_im[layer], ssm_log_dt[layer],
                          ssm_b_re[layer], ssm_b_im[layer], ssm_c_re[layer], ssm_c_im[layer],
                          ssm_d[layer], ssm_glu_w[layer])
        y_gdn = gdn_mixer(proj, bs, gdn_conv_w[layer], gdn_a_log[layer], gdn_dt_bias[layer],
                          gdn_norm[layer])
        kvs = [seg(OFF_KV + i * NSA_KV, NSA_KV) for i in range(6)]
        y_nsa = nsa_mixer(seg(OFF_NQ, NSA_Q), *kvs, seg(OFF_SMALL + 2 * GDN_HEADS, 3 * NSA_HEADS),
                          positions, nsa_cmp_k_pe[layer], nsa_cmp_k_w1[layer], nsa_cmp_k_w2[layer],
                          nsa_cmp_v_pe[layer], nsa_cmp_v_w1[layer], nsa_cmp_v_w2[layer], rel_bias)
        y_sgu = sgu_mixer(proj, sgu_ln_g[layer], sgu_ln_b[layer], sgu_w[layer], sgu_b[layer])
        ys = [y_ssm, y_gdn, y_nsa.reshape(t, -1), y_sgu]
        ws = [w[layer].astype(BF16) for w in (w_br_ssm, w_br_gdn, w_br_nsa, w_br_sgu)]
        xf = merge(xf, proj, ys, ws, w_out[layer].astype(BF16))
        i = layer // 2
        if layer % 2 == 0:
            xf = dense_ffn(xf, norm_ffn[layer], ffn_w_gate[i], ffn_w_up[i], ffn_w_down[i])
        else:
            xf = moe_layer(xf, norm_ffn[layer], moe_router[i], moe_w_gate[i], moe_w_up[i],
                           moe_w_down[i], norm_final)
    return xf.reshape(bs, s, d)
```

```python
import functools
import math

import jax
import jax.numpy as jnp
from jax import lax
import numpy as np
from jax.experimental import pallas as pl
from jax.experimental.pallas import tpu as pltpu

F32 = jnp.float32
BF16 = jnp.bfloat16

D_MODEL = 1024
DEPTH = 2
SSM_WIDTH = D_MODEL // 2
SSM_GROUP = 16
SSM_GROUPS = SSM_WIDTH // SSM_GROUP
SSM_STATE = 64
GDN_HEADS = 4
GDN_DK = 128
GDN_DV = 128
GDN_CONV = 4
GDN_CHUNK = 64
NSA_HEADS = 8
NSA_KV_GROUPS = 2
NSA_HPG = NSA_HEADS // NSA_KV_GROUPS
NSA_DH = 64
NSA_CMP_LEN = 32
NSA_CMP_STRIDE = 16
NSA_CMP_HIDDEN = 128
NSA_SEL_LEN = 64
NSA_SEL_TOPK = 16
NSA_WINDOW = 512
NSA_QBLOCK = 128
SGU_WIDTH = D_MODEL // 2
SGU_GROUPS = 4
SGU_CHUNK = 128
REL_BUCKETS = 32
REL_MAX_DIST = 128
FFN_DENSE = 2816
N_EXPERTS = 8
TOP_K = 2
FFN_EXPERT = 3584
N_BRANCH = 4
RMS_EPS = 1e-6
GDN_QKV = GDN_HEADS * (2 * GDN_DK + GDN_DV)
NSA_Q = NSA_HEADS * NSA_DH
NSA_KV = NSA_KV_GROUPS * NSA_DH

LANES = 128
VMEM_LIMIT = 48 * 1024 * 1024
MASKED = -1e30

OFF_QKV = 0
OFF_SSM = OFF_QKV + GDN_QKV
OFF_Z = OFF_SSM + SSM_WIDTH
OFF_NQ = OFF_Z + GDN_HEADS * GDN_DV
OFF_SGU = OFF_NQ + NSA_Q
OFF_GATE = OFF_SGU + 2 * SGU_WIDTH
OFF_KV = OFF_GATE + N_BRANCH * D_MODEL
OFF_SMALL = OFF_KV + 6 * NSA_KV
SMALL_USED = 2 * GDN_HEADS + 3 * NSA_HEADS
PROJ_COLS = 9216


def _params(*sem):
    return pltpu.CompilerParams(dimension_semantics=sem, vmem_limit_bytes=VMEM_LIMIT)


def _rms(x, g):
    return x * lax.rsqrt(jnp.mean(x * x, axis=-1, keepdims=True) + RMS_EPS) * g


def _dot_nt(a, b):
    return lax.dot_general(a, b, (((1,), (1,)), ((), ())), preferred_element_type=F32)


def _dot_tn(a, b):
    return lax.dot_general(a, b, (((0,), (0,)), ((), ())), preferred_element_type=F32)


def _split_bf16(x):
    hi = x.astype(BF16)
    return hi, (x - hi.astype(F32)).astype(BF16)


def _in_proj_body(x_ref, g_ref, w_ref, o_ref, h_ref):
    @pl.when(pl.program_id(1) == 0)
    def _():
        h_ref[...] = _rms(x_ref[...], g_ref[...]).astype(BF16)

    o_ref[...] = jnp.dot(h_ref[...], w_ref[...], preferred_element_type=F32).astype(o_ref.dtype)


def in_proj(x, g, w, tm=2048, tn=1024):
    t, d = x.shape
    n = w.shape[1]
    return pl.pallas_call(
        _in_proj_body,
        grid=(t // tm, n // tn),
        in_specs=[pl.BlockSpec((tm, d), lambda i, j: (i, 0)),
                  pl.BlockSpec((1, d), lambda i, j: (0, 0)),
                  pl.BlockSpec((d, tn), lambda i, j: (0, j))],
        out_specs=pl.BlockSpec((tm, tn), lambda i, j: (i, j)),
        out_shape=jax.ShapeDtypeStruct((t, n), BF16),
        scratch_shapes=[pltpu.VMEM((tm, d), BF16)],
        compiler_params=_params("parallel", "arbitrary"),
        name="in_proj",
    )(x, g.reshape(1, d), w)


def _merge_body(x_ref, gate_ref, ya_ref, yb_ref, yc_ref, yd_ref,
                wa_ref, wb_ref, wc_ref, wd_ref, wo_ref, o_ref):
    def branch(k, y_ref, w_ref):
        p = jnp.dot(y_ref[...], w_ref[...], preferred_element_type=F32)
        return jax.nn.sigmoid(gate_ref[:, k * D_MODEL:(k + 1) * D_MODEL].astype(F32)) * p

    merged = (branch(0, ya_ref, wa_ref) + branch(1, yb_ref, wb_ref)
              + branch(2, yc_ref, wc_ref) + branch(3, yd_ref, wd_ref))
    o_ref[...] = x_ref[...] + jnp.dot(merged.astype(BF16), wo_ref[...],
                                      preferred_element_type=F32)


def merge(x, proj, ys, ws, w_out, tm=512):
    t, d = x.shape
    gate_blk = OFF_GATE // (N_BRANCH * D_MODEL)
    row = lambda i: (i, 0)
    fixed = lambda i: (0, 0)
    in_specs = [pl.BlockSpec((tm, d), row),
                pl.BlockSpec((tm, N_BRANCH * D_MODEL), lambda i: (i, gate_blk))]
    in_specs += [pl.BlockSpec((tm, y.shape[1]), row) for y in ys]
    in_specs += [pl.BlockSpec(w.shape, fixed) for w in ws]
    in_specs += [pl.BlockSpec(w_out.shape, fixed)]
    return pl.pallas_call(
        _merge_body,
        grid=(t // tm,),
        in_specs=in_specs,
        out_specs=pl.BlockSpec((tm, d), row),
        out_shape=jax.ShapeDtypeStruct((t, d), F32),
        compiler_params=_params("parallel"),
        name="merge",
    )(x, proj, *ys, *ws, w_out)


def _ffn_body(x_ref, g_ref, wg_ref, wu_ref, wd_ref, o_ref, h_ref, acc_ref):
    f = pl.program_id(1)

    @pl.when(f == 0)
    def _():
        h_ref[...] = _rms(x_ref[...], g_ref[...]).astype(BF16)
        acc_ref[...] = x_ref[...]

    h = h_ref[...]
    a = jnp.dot(h, wg_ref[...].astype(BF16), preferred_element_type=F32)
    u = jnp.dot(h, wu_ref[...].astype(BF16), preferred_element_type=F32)
    act = (a * jax.nn.sigmoid(a) * u).astype(BF16)
    acc_ref[...] += jnp.dot(act, wd_ref[...].astype(BF16), preferred_element_type=F32)

    @pl.when(f == pl.num_programs(1) - 1)
    def _():
        o_ref[...] = acc_ref[...]


def dense_ffn(x, g, wg, wu, wd, tm=1024, tf=256):
    t, d = x.shape
    nf = wg.shape[1]
    return pl.pallas_call(
        _ffn_body,
        grid=(t // tm, nf // tf),
        in_specs=[pl.BlockSpec((tm, d), lambda i, f: (i, 0)),
                  pl.BlockSpec((1, d), lambda i, f: (0, 0)),
                  pl.BlockSpec((d, tf), lambda i, f: (0, f)),
                  pl.BlockSpec((d, tf), lambda i, f: (0, f)),
                  pl.BlockSpec((tf, d), lambda i, f: (f, 0))],
        out_specs=pl.BlockSpec((tm, d), lambda i, f: (i, 0)),
        out_shape=jax.ShapeDtypeStruct((t, d), F32),
        scratch_shapes=[pltpu.VMEM((tm, d), BF16), pltpu.VMEM((tm, d), F32)],
        compiler_params=_params("parallel", "arbitrary"),
        name="dense_ffn",
    )(x, g.reshape(1, d), wg, wu, wd)


def _route_body(x_ref, g_ref, rhi_ref, rlo_ref, h_ref, idx_ref, wt_ref):
    h = _rms(x_ref[...], g_ref[...])
    hi = h.astype(BF16)
    lo = (h - hi.astype(F32)).astype(BF16)
    h_ref[...] = h
    logits = (jnp.dot(hi, rhi_ref[...], preferred_element_type=F32)
              + jnp.dot(hi, rlo_ref[...], preferred_element_type=F32)
              + jnp.dot(lo, rhi_ref[...], preferred_element_type=F32))
    col = lax.broadcasted_iota(jnp.int32, logits.shape, 1).astype(F32)
    neg = jnp.float32(-jnp.inf)
    lg = jnp.where(col < N_EXPERTS, logits, neg)
    m1 = jnp.max(lg, axis=-1, keepdims=True)
    i1 = jnp.min(jnp.where(lg == m1, col, float(LANES)), axis=-1, keepdims=True)
    lg2 = jnp.where(col == i1, neg, lg)
    m2 = jnp.max(lg2, axis=-1, keepdims=True)
    i2 = jnp.min(jnp.where(lg2 == m2, col, float(LANES)), axis=-1, keepdims=True)
    e = jnp.exp(m2 - m1)
    w1 = 1.0 / (1.0 + e)
    w2 = e / (1.0 + e)
    idx_ref[...] = jnp.where(col == 0, i1, jnp.where(col == 1, i2, 0.0)).astype(jnp.int32)
    wt_ref[...] = jnp.where(col == 0, w1, jnp.where(col == 1, w2, 0.0))


def moe_route(x, g, r_hi, r_lo, tm=512):
    t, d = x.shape
    row = lambda i: (i, 0)
    fixed = lambda i: (0, 0)
    return pl.pallas_call(
        _route_body,
        grid=(t // tm,),
        in_specs=[pl.BlockSpec((tm, d), row), pl.BlockSpec((1, d), fixed),
                  pl.BlockSpec((d, LANES), fixed), pl.BlockSpec((d, LANES), fixed)],
        out_specs=[pl.BlockSpec((tm, d), row), pl.BlockSpec((tm, LANES), row),
                   pl.BlockSpec((tm, LANES), row)],
        out_shape=[jax.ShapeDtypeStruct((t, d), F32),
                   jax.ShapeDtypeStruct((t, LANES), jnp.int32),
                   jax.ShapeDtypeStruct((t, LANES), F32)],
        compiler_params=_params("parallel"),
        name="moe_route",
    )(x, g.reshape(1, d), r_hi, r_lo)


def _experts_body(te_ref, nu_ref, src_ref, h_hbm, wg_ref, wu_ref, wd_ref, o_ref,
                  xbuf_ref, xs_ref, acc_ref, sem, *, rows_per_step):
    i = pl.program_id(0)
    f = pl.program_id(1)
    tm = xs_ref.shape[0]
    n_rows = xbuf_ref.shape[1]
    n_used = nu_ref[0]
    used = i < n_used
    has_next = i + 1 < n_used
    slot = i % 2

    def start_row(tile, slot_, r):
        tok = src_ref[tile * tm + r]
        pltpu.make_async_copy(h_hbm.at[pl.ds(tok, 1), :], xbuf_ref.at[slot_, pl.ds(r, 1), :],
                              sem.at[slot_]).start()

    @pl.when(f == 0)
    def _():
        acc_ref[...] = jnp.zeros_like(acc_ref)

        @pl.when(used)
        def _():
            @pl.when(i == 0)
            def _():
                def row(r, carry):
                    start_row(0, 0, r)
                    return carry

                lax.fori_loop(0, n_rows, row, 0, unroll=8)

            pltpu.make_async_copy(h_hbm.at[pl.ds(0, n_rows), :], xbuf_ref.at[slot], sem.at[slot]).wait()
            xs_ref[...] = xbuf_ref[slot, 0:tm, :].astype(BF16)

    def ffn_step():
        h = xs_ref[...]
        a = jnp.dot(h, wg_ref[0].astype(BF16), preferred_element_type=F32)
        u = jnp.dot(h, wu_ref[0].astype(BF16), preferred_element_type=F32)
        act = (a * jax.nn.sigmoid(a) * u).astype(BF16)
        acc_ref[...] += jnp.dot(act, wd_ref[0].astype(BF16), preferred_element_type=F32)

    @pl.when(used & has_next)
    def _():
        for j in range(rows_per_step):
            start_row(i + 1, 1 - slot, f * rows_per_step + j)
        ffn_step()

    @pl.when(used & jnp.logical_not(has_next))
    def _():
        ffn_step()

    @pl.when(f == pl.num_programs(1) - 1)
    def _():
        o_ref[...] = acc_ref[...].astype(o_ref.dtype)


def moe_experts(tile_expert, n_used, src, h, wg, wu, wd, tm, tf=512):
    d = h.shape[1]
    nf = wg.shape[2] // tf
    rows_per_step = 8 * (-(-tm // (8 * nf)))
    n_rows = rows_per_step * nf
    n_tiles = src.shape[0] // tm
    src = jnp.pad(src, (0, n_rows - tm))
    grid_spec = pltpu.PrefetchScalarGridSpec(
        num_scalar_prefetch=3,
        grid=(n_tiles, nf),
        in_specs=[pl.BlockSpec(memory_space=pl.ANY),
                  pl.BlockSpec((1, d, tf), lambda i, f, te, *_: (te[i], 0, f)),
                  pl.BlockSpec((1, d, tf), lambda i, f, te, *_: (te[i], 0, f)),
                  pl.BlockSpec((1, tf, d), lambda i, f, te, *_: (te[i], f, 0))],
        out_specs=pl.BlockSpec((tm, d), lambda i, f, *_: (i, 0)),
        scratch_shapes=[pltpu.VMEM((2, n_rows, d), F32), pltpu.VMEM((tm, d), BF16),
                        pltpu.VMEM((tm, d), F32), pltpu.SemaphoreType.DMA((2,))],
    )
    return pl.pallas_call(
        functools.partial(_experts_body, rows_per_step=rows_per_step),
        grid_spec=grid_spec,
        out_shape=jax.ShapeDtypeStruct((n_tiles * tm, d), BF16),
        compiler_params=_params("arbitrary", "arbitrary"),
        name="moe_experts",
    )(tile_expert, n_used, src, h, wg, wu, wd)


def _combine_norm_body(x_ref, ya_ref, yb_ref, wt_ref, g_ref, o_ref):
    wt = wt_ref[...]
    y = wt[:, 0:1] * ya_ref[...].astype(F32) + wt[:, 1:2] * yb_ref[...].astype(F32)
    o_ref[...] = _rms(x_ref[...] + y, g_ref[...])


def combine_norm(x, ya, yb, wts, g, tm=1024):
    t, d = x.shape
    row = lambda i: (i, 0)
    return pl.pallas_call(
        _combine_norm_body,
        grid=(t // tm,),
        in_specs=[pl.BlockSpec((tm, d), row)] * 3 + [pl.BlockSpec((tm, LANES), row),
                                                     pl.BlockSpec((1, d), lambda i: (0, 0))],
        out_specs=pl.BlockSpec((tm, d), row),
        out_shape=jax.ShapeDtypeStruct((t, d), F32),
        compiler_params=_params("parallel"),
        name="combine_norm",
    )(x, ya, yb, wts, g.reshape(1, d))


def moe_layer(x, g_norm, router, wg, wu, wd, g_final, tm=1024):
    t, d = x.shape
    r = jnp.zeros((d, LANES), F32).at[:, :N_EXPERTS].set(router)
    r_hi = r.astype(BF16)
    r_lo = (r - r_hi.astype(F32)).astype(BF16)
    h, idx, wts = moe_route(x, g_norm, r_hi, r_lo)
    e_flat = jnp.concatenate([idx[:, 0], idx[:, 1]])
    onehot = (e_flat[:, None] == jnp.arange(N_EXPERTS)[None, :]).astype(jnp.int32)
    csum = jnp.cumsum(onehot, axis=0)
    rank = jnp.sum((csum - onehot) * onehot, axis=1)
    counts = csum[-1]
    padded = ((counts + tm - 1) // tm) * tm
    ends = jnp.cumsum(padded)
    starts = ends - padded
    dest = starts[e_flat] + rank
    n_tiles = (TOP_K * t) // tm + N_EXPERTS
    p = n_tiles * tm
    tok = jnp.concatenate([jnp.arange(t, dtype=jnp.int32)] * TOP_K)
    src = jnp.zeros((p,), jnp.int32).at[dest].set(tok)
    tile_expert = jnp.minimum(
        jnp.searchsorted(ends, jnp.arange(n_tiles, dtype=jnp.int32) * tm, side="right"),
        N_EXPERTS - 1).astype(jnp.int32)
    n_used = (ends[-1] // tm).astype(jnp.int32).reshape(1)
    ys = moe_experts(tile_expert, n_used, src, h, wg, wu, wd, tm)
    ya = jnp.take(ys, dest[:t], axis=0, mode="clip")
    yb = jnp.take(ys, dest[t:], axis=0, mode="clip")
    return combine_norm(x, ya, yb, wts, g_final)


SSM_TC = 512
SSM_SUB = 128
SSM_HALF = 256
SSM_NSTATE = SSM_GROUPS * SSM_STATE


def _ssm_body(u_ref, bw_ref, cw_ref, d_ref, glu_ref, ar_ref, ai_ref, pr_ref, pi_ref, o_ref,
              xr_ref, xi_ref, cr_ref, ci_ref):
    @pl.when(pl.program_id(1) == 0)
    def _():
        cr_ref[...] = jnp.zeros_like(cr_ref)
        ci_ref[...] = jnp.zeros_like(ci_ref)

    ub = u_ref[...]
    u = ub.astype(F32)
    nblk = SSM_WIDTH // SSM_HALF
    sblk = SSM_NSTATE // nblk
    for k in range(nblk):
        bu = jnp.dot(ub[:, k * SSM_HALF:(k + 1) * SSM_HALF], bw_ref[k], preferred_element_type=F32)
        xr_ref[:, k * sblk:(k + 1) * sblk] = bu[:, :sblk]
        xi_ref[:, k * sblk:(k + 1) * sblk] = bu[:, sblk:]

    row8 = lax.broadcasted_iota(jnp.int32, (SSM_SUB, LANES), 0) % 8
    n_grp = SSM_SUB // 8

    def strip(c, carry):
        col = pl.ds(pl.multiple_of(c * LANES, LANES), LANES)
        c_r = cr_ref[:, col]
        c_i = ci_ref[:, col]
        p_r = pr_ref[0:8, col]
        p_i = pi_ref[0:8, col]
        step_mul = [(jnp.where(row8 >= (1 << i), ar_ref[i:i + 1, col], 0.0),
                     jnp.where(row8 >= (1 << i), ai_ref[i:i + 1, col], 0.0)) for i in range(3)]
        for sub in range(SSM_TC // SSM_SUB):
            rows = slice(sub * SSM_SUB, (sub + 1) * SSM_SUB)
            xr = xr_ref[rows, col]
            xi = xi_ref[rows, col]
            for i in range(3):
                a_r, a_i = step_mul[i]
                sr, si = pltpu.roll(xr, 1 << i, 0), pltpu.roll(xi, 1 << i, 0)
                xr, xi = xr + a_r * sr - a_i * si, xi + a_r * si + a_i * sr
            out_r, out_i = [], []
            for r in range(n_grp):
                g_r = xr[8 * r:8 * r + 8] + p_r * c_r - p_i * c_i
                g_i = xi[8 * r:8 * r + 8] + p_r * c_i + p_i * c_r
                c_r, c_i = g_r[7:8, :], g_i[7:8, :]
                out_r.append(g_r)
                out_i.append(g_i)
            xr_ref[rows, col] = jnp.concatenate(out_r, axis=0)
            xi_ref[rows, col] = jnp.concatenate(out_i, axis=0)
        cr_ref[:, col] = c_r
        ci_ref[:, col] = c_i
        return carry

    lax.fori_loop(0, SSM_NSTATE // LANES, strip, 0)

    ys = []
    for k in range(nblk):
        st = jnp.concatenate([xr_ref[:, k * sblk:(k + 1) * sblk].astype(BF16),
                              xi_ref[:, k * sblk:(k + 1) * sblk].astype(BF16)], axis=1)
        ys.append(jnp.dot(st, cw_ref[k], preferred_element_type=F32))
    y = jax.nn.gelu(jnp.concatenate(ys, axis=1) + d_ref[...] * u)
    gated = y * jax.nn.sigmoid(jnp.dot(y.astype(BF16), glu_ref[...], preferred_element_type=F32))
    o_ref[...] = gated.astype(o_ref.dtype)


def ssm_mixer(proj, bs, a_re, a_im, log_dt, b_re, b_im, c_re, c_im, d_skip, glu_w):
    t = proj.shape[0]
    s = t // bs
    nt = s // SSM_TC
    assert s % SSM_TC == 0 and OFF_SSM % SSM_WIDTH == 0
    nblk = SSM_WIDTH // SSM_HALF
    gpb = SSM_GROUPS // nblk
    dt = jnp.exp(log_dt)[:, None]
    mag = jnp.exp(a_re * dt)
    abar_r, abar_i = mag * jnp.cos(a_im * dt), mag * jnp.sin(a_im * dt)
    nr, ni = abar_r - 1.0, abar_i
    den = a_re * a_re + a_im * a_im
    sr, si = (nr * a_re + ni * a_im) / den, (ni * a_re - nr * a_im) / den
    bbar_r = sr[..., None] * b_re - si[..., None] * b_im
    bbar_i = sr[..., None] * b_im + si[..., None] * b_re

    def powers(k):
        kk = k.astype(F32)[:, None, None]
        m = jnp.exp(kk * (a_re * dt))
        return ((m * jnp.cos(kk * (a_im * dt))).reshape(-1, SSM_NSTATE),
                (m * jnp.sin(kk * (a_im * dt))).reshape(-1, SSM_NSTATE))

    ar, ai = powers(2 ** jnp.arange(8))
    pr, pi = powers(jnp.arange(1, SSM_SUB + 1))
    eye = jnp.eye(gpb, dtype=F32)

    def in_block(w):
        return jnp.einsum('gnp,gh->gphn', w, eye).reshape(gpb * SSM_GROUP, gpb * SSM_STATE)

    def out_block(w):
        return jnp.einsum('gpn,gh->gnhp', w, eye).reshape(gpb * SSM_STATE, gpb * SSM_GROUP)

    bw = jnp.stack([jnp.concatenate([in_block(bbar_r[k * gpb:(k + 1) * gpb]),
                                     in_block(bbar_i[k * gpb:(k + 1) * gpb])], axis=1)
                    for k in range(nblk)]).astype(BF16)
    cw = jnp.stack([jnp.concatenate([out_block(c_re[k * gpb:(k + 1) * gpb]),
                                     -out_block(c_im[k * gpb:(k + 1) * gpb])], axis=0)
                    for k in range(nblk)]).astype(BF16)
    fixed2 = lambda b, i: (0, 0)
    fixed3 = lambda b, i: (0, 0, 0)
    return pl.pallas_call(
        _ssm_body,
        grid=(bs, nt),
        in_specs=[pl.BlockSpec((SSM_TC, SSM_WIDTH), lambda b, i: (b * nt + i, OFF_SSM // SSM_WIDTH)),
                  pl.BlockSpec(bw.shape, fixed3), pl.BlockSpec(cw.shape, fixed3),
                  pl.BlockSpec((1, SSM_WIDTH), fixed2), pl.BlockSpec((SSM_WIDTH, SSM_WIDTH), fixed2),
                  pl.BlockSpec((8, SSM_NSTATE), fixed2), pl.BlockSpec((8, SSM_NSTATE), fixed2),
                  pl.BlockSpec((SSM_SUB, SSM_NSTATE), fixed2), pl.BlockSpec((SSM_SUB, SSM_NSTATE), fixed2)],
        out_specs=pl.BlockSpec((SSM_TC, SSM_WIDTH), lambda b, i: (b * nt + i, 0)),
        out_shape=jax.ShapeDtypeStruct((t, SSM_WIDTH), BF16),
        scratch_shapes=[pltpu.VMEM((SSM_TC, SSM_NSTATE), F32), pltpu.VMEM((SSM_TC, SSM_NSTATE), F32),
                        pltpu.VMEM((1, SSM_NSTATE), F32), pltpu.VMEM((1, SSM_NSTATE), F32)],
        compiler_params=_params("parallel", "arbitrary"),
        name="ssm_scan",
    )(proj, bw, cw, d_skip.reshape(1, SSM_WIDTH), glu_w.astype(BF16), ar, ai, pr, pi)


GDN_TC = 256
GDN_HALO = 8


def _gdn_body(nega_ref, dtb_ref, q_ref, k_ref, v_ref, z_ref, sm_ref, wq_ref, wk_ref, wv_ref, ng_ref,
              o_ref, state_ref, hq_ref, hk_ref, hv_ref, vnew_ref):
    TC, CH, DK, H = GDN_TC, GDN_CHUNK, GDN_DK, GDN_HEADS
    heads = range(H)

    @pl.when(pl.program_id(1) == 0)
    def _():
        state_ref[...] = jnp.zeros_like(state_ref)
        hq_ref[...] = jnp.zeros_like(hq_ref)
        hk_ref[...] = jnp.zeros_like(hk_ref)
        hv_ref[...] = jnp.zeros_like(hv_ref)

    def conv_silu(x_ref, halo_ref, w_ref):
        x = x_ref[...].astype(F32)
        xx = jnp.concatenate([halo_ref[...], x], axis=0)
        w = w_ref[...]
        y = w[GDN_CONV - 1:GDN_CONV] * x
        for d in range(1, GDN_CONV):
            y = y + w[GDN_CONV - 1 - d:GDN_CONV - d] * pltpu.roll(xx, d, 0)[GDN_HALO:GDN_HALO + TC]
        halo_ref[...] = x[TC - GDN_HALO:TC]
        return y * jax.nn.sigmoid(y)

    def per_head(x):
        return [x[:, h * LANES:(h + 1) * LANES] for h in heads]

    q = per_head(conv_silu(q_ref, hq_ref, wq_ref))
    k = per_head(conv_silu(k_ref, hk_ref, wk_ref))
    v = per_head(conv_silu(v_ref, hv_ref, wv_ref))
    q = [x * lax.rsqrt(jnp.sum(x * x, axis=-1, keepdims=True) + 1e-6) * (DK ** -0.5) for x in q]
    k = [x * lax.rsqrt(jnp.sum(x * x, axis=-1, keepdims=True) + 1e-6) for x in k]

    small = sm_ref[...].astype(F32)
    beta = [jax.nn.sigmoid(small[:, H + h:H + h + 1]) for h in heads]
    la = []
    for h in heads:
        xa = small[:, h:h + 1] + dtb_ref[h]
        softplus = jnp.maximum(xa, 0.0) + jnp.log(1.0 + jnp.exp(-jnp.abs(xa)))
        la.append(jnp.broadcast_to(nega_ref[h] * softplus, (TC, LANES)))

    ri = lax.broadcasted_iota(jnp.int32, (TC, TC), 0)
    ci = lax.broadcasted_iota(jnp.int32, (TC, TC), 1)
    same = (ri // CH) == (ci // CH)
    causal = jnp.where(same, jnp.where(ci <= ri, 1.0, 0.0), 0.0)
    strict = jnp.where(ci < ri, causal, 0.0)
    eye = jnp.where(ri == ci, 1.0, 0.0)
    tri = causal.astype(BF16)
    lane = lax.broadcasted_iota(jnp.int32, (TC, LANES), 1)

    g = []
    for h in heads:
        la_hi, la_lo = _split_bf16(la[h])
        g.append(jnp.dot(tri, la_hi, preferred_element_type=F32)
                 + jnp.dot(tri, la_lo, preferred_element_type=F32))
    decay = []
    for h in heads:
        g_hi = g[h].astype(BF16).astype(F32)
        g_lo = g[h] - g_hi
        lhs = jnp.where(lane == 0, g_hi, jnp.where(lane == 1, g_lo, jnp.where(lane < 4, 1.0, 0.0)))
        rhs = jnp.where(lane < 2, 1.0, jnp.where(lane == 2, -g_hi, jnp.where(lane == 3, -g_lo, 0.0)))
        decay.append(jnp.exp(jnp.where(causal > 0.5, _dot_nt(lhs.astype(BF16), rhs.astype(BF16)), MASKED)))

    kb = [k[h] * beta[h] for h in heads]
    k_b = [x.astype(BF16) for x in k]
    lmat = [strict * _dot_nt(kb[h].astype(BF16), k_b[h]) * decay[h] for h in heads]
    tinv = [eye - x for x in lmat]
    pw = lmat
    for _ in range(CH.bit_length() - 2):
        pw = [jnp.dot(x.astype(BF16), x.astype(BF16), preferred_element_type=F32) for x in pw]
        tinv = [tinv[h] + jnp.dot(tinv[h].astype(BF16), pw[h].astype(BF16), preferred_element_type=F32)
                for h in heads]
    eg = [jnp.exp(x) for x in g]
    sol = [jnp.dot(tinv[h].astype(BF16),
                   jnp.concatenate([v[h] * beta[h], kb[h] * eg[h]], axis=1).astype(BF16),
                   preferred_element_type=F32) for h in heads]
    attn = [(causal * _dot_nt(q[h].astype(BF16), k_b[h]) * decay[h]).astype(BF16) for h in heads]
    q_st = [(q[h] * eg[h]).astype(BF16) for h in heads]

    vnew_ref[...] = jnp.zeros_like(vnew_ref)
    for c in range(TC // CH):
        rows = slice(c * CH, (c + 1) * CH)
        g_last = [x[(c + 1) * CH - 1:(c + 1) * CH, :] for x in g]
        st = [state_ref[h] for h in heads]
        st_b = [x.astype(BF16) for x in st]
        v_new = [sol[h][rows, :GDN_DV]
                 - jnp.dot(sol[h][rows, GDN_DV:].astype(BF16), st_b[h], preferred_element_type=F32)
                 for h in heads]
        for h in heads:
            vnew_ref[h, rows, :] = v_new[h].astype(BF16)
        o_c = [jnp.dot(q_st[h][rows], st_b[h], preferred_element_type=F32)
               + jnp.dot(attn[h][rows], vnew_ref[h], preferred_element_type=F32) for h in heads]
        for h in heads:
            k_st = (k[h][rows] * jnp.exp(g_last[h] - g[h][rows])).astype(BF16)
            state_ref[h] = st[h] * jnp.exp(g_last[h][:, :1]) + _dot_tn(k_st, v_new[h].astype(BF16))
        for h in heads:
            o = o_c[h] * lax.rsqrt(jnp.mean(o_c[h] * o_c[h], axis=-1, keepdims=True) + RMS_EPS) * ng_ref[...]
            zc = z_ref[rows, h * LANES:(h + 1) * LANES].astype(F32)
            o_ref[rows, h * LANES:(h + 1) * LANES] = (o * (zc * jax.nn.sigmoid(zc))).astype(o_ref.dtype)


def gdn_mixer(proj, bs, conv_w, a_log, dt_bias, norm_g):
    t = proj.shape[0]
    s = t // bs
    nt = s // GDN_TC
    H = GDN_HEADS
    hw = H * LANES
    assert s % GDN_TC == 0 and GDN_DK == LANES and GDN_DV == LANES and OFF_QKV == 0 and OFF_Z % hw == 0
    cw = jnp.zeros((8, GDN_QKV), F32).at[:GDN_CONV].set(conv_w)
    tok = lambda blk: pl.BlockSpec((GDN_TC, hw), lambda b, i: (b * nt + i, blk))
    wspec = lambda blk: pl.BlockSpec((8, hw), lambda b, i: (0, blk))
    smem = pl.BlockSpec(memory_space=pltpu.SMEM)
    return pl.pallas_call(
        _gdn_body,
        grid=(bs, nt),
        in_specs=[smem, smem, tok(0), tok(1), tok(2), tok(OFF_Z // hw),
                  pl.BlockSpec((GDN_TC, 2 * LANES), lambda b, i: (b * nt + i, OFF_SMALL // (2 * LANES))),
                  wspec(0), wspec(1), wspec(2), pl.BlockSpec((1, GDN_DV), lambda b, i: (0, 0))],
        out_specs=pl.BlockSpec((GDN_TC, hw), lambda b, i: (b * nt + i, 0)),
        out_shape=jax.ShapeDtypeStruct((t, hw), BF16),
        scratch_shapes=[pltpu.VMEM((H, GDN_DK, GDN_DV), F32)] + [pltpu.VMEM((GDN_HALO, hw), F32)] * 3
        + [pltpu.VMEM((H, GDN_TC, GDN_DV), BF16)],
        compiler_params=_params("parallel", "arbitrary"),
        name="gdn_delta",
    )(-jnp.exp(a_log), dt_bias.astype(F32), proj, proj, proj, proj, proj, cw, cw, cw,
      norm_g.reshape(1, GDN_DV))


NSA_KT = 256
NSA_NEG = MASKED
NSA_WTILES = NSA_WINDOW // LANES + 1


def _bucket_table():
    n = np.arange(LANES)
    max_exact = REL_BUCKETS // 2
    nf = np.maximum(n, 1).astype(np.float32)
    large = max_exact + (np.log(nf / np.float32(max_exact)) / np.float32(math.log(REL_MAX_DIST / max_exact))
                         * np.float32(REL_BUCKETS - max_exact)).astype(np.int32)
    tbl = np.where(n < max_exact, n, np.minimum(large, REL_BUCKETS - 1))
    assert tbl[-1] == REL_BUCKETS - 1 and REL_MAX_DIST <= LANES
    return tbl


def _compress_body(x_ref, pe_ref, w1_ref, w2_ref, o_ref):
    x = x_ref[0, 0, 0]
    w1 = w1_ref[0]
    half = NSA_CMP_STRIDE * NSA_DH
    nc = x.shape[0]
    a = jnp.dot(x, w1[:half], preferred_element_type=F32)
    b = jnp.dot(x, w1[half:], preferred_element_type=F32)
    pe_h = jnp.dot(pe_ref[0], w1, preferred_element_type=F32)
    hid = a + pltpu.roll(b, nc - 1, 0) + pe_h[0:1, :]
    o_ref[0, 0, 0] = jnp.dot(jax.nn.gelu(hid).astype(BF16), w2_ref[0], preferred_element_type=F32)


def nsa_compress(x, pe, w1, w2):
    _, bs, g, nc, width = x.shape
    return pl.pallas_call(
        _compress_body,
        grid=(2, bs, g),
        in_specs=[pl.BlockSpec((1, 1, 1, nc, width), lambda i, b, j: (i, b, j, 0, 0)),
                  pl.BlockSpec((1,) + pe.shape[1:], lambda i, b, j: (i, 0, 0)),
                  pl.BlockSpec((1,) + w1.shape[1:], lambda i, b, j: (i, 0, 0)),
                  pl.BlockSpec((1,) + w2.shape[1:], lambda i, b, j: (i, 0, 0))],
        out_specs=pl.BlockSpec((1, 1, 1, nc, NSA_DH), lambda i, b, j: (i, b, j, 0, 0)),
        out_shape=jax.ShapeDtypeStruct((2, bs, g, nc, NSA_DH), F32),
        compiler_params=_params("parallel", "parallel", "parallel"),
        name="nsa_compress",
    )(x, pe, w1, w2)


def _nsa_body(nfast, std, near_w, near_c, q_ref, gl_ref, qp_ref, kp_ref, cp_ref,
              ks_ref, vs_ref, kw_ref, vw_ref, kc_ref, vc_ref, agg_ref, td_ref, o_ref,
              m_ref, l_ref, acc_ref, sa_ref, sb_ref, sc_ref, *, n_sel, sel_k):
    QB, HG, G = NSA_QBLOCK, NSA_HPG, NSA_KV_GROUPS
    groups = range(G)
    qi = pl.program_id(1)
    s0 = qi * QB
    nc = kc_ref.shape[2]
    n_ct = nc // LANES
    q = [q_ref[0, g, 0] for g in groups]
    qp = qp_ref[...]

    def lanes4(x):
        return jnp.concatenate([x] * HG, axis=1)

    def delta_t(kp):
        idx = jnp.clip(_dot_nt(kp, qp), 0.0, float(LANES - 1)).astype(jnp.int32)
        outs = []
        for g in groups:
            heads = []
            for hg in range(HG):
                tb = jnp.broadcast_to(td_ref[g * HG + hg:g * HG + hg + 1, :], idx.shape)
                heads.append(jnp.take_along_axis(tb, idx, axis=1))
            outs.append(jnp.concatenate(heads, axis=1))
        return outs

    def maybe_delta(flag, kp):
        zeros = jnp.zeros((kp.shape[0], HG * QB), F32)
        return lax.cond(flag != 0, lambda: tuple(delta_t(kp)), lambda: (zeros,) * G)

    gate = [jax.nn.sigmoid(gl_ref[0, g, 0]) for g in groups]

    for g in groups:
        sc_ref[g] = _dot_nt(kc_ref[0, g], q[g])
    near_rows = 4 * 8

    @pl.when(std[qi] != 0)
    def _():
        c0 = pl.multiple_of(jnp.minimum(qi * (QB // NSA_CMP_STRIDE) - near_rows // 2, nc - near_rows), 8)
        d = delta_t(cp_ref[pl.ds(c0, near_rows), :])
        for g in groups:
            sc_ref[g, pl.ds(c0, near_rows), :] += d[g]

    @pl.when(std[qi] == 0)
    def _():
        for ct in range(n_ct):
            @pl.when(near_c[qi * n_ct + ct] != 0)
            def _():
                d = delta_t(cp_ref[ct * LANES:(ct + 1) * LANES, :])
                for g in groups:
                    sc_ref[g, ct * LANES:(ct + 1) * LANES, :] += d[g]

    c_id = lax.broadcasted_iota(jnp.int32, (nc, QB), 0)
    c_q = lax.broadcasted_iota(jnp.int32, (nc, QB), 1)
    ok_c = lanes4(jnp.where(c_id * NSA_CMP_STRIDE + (NSA_CMP_LEN - 1) <= s0 + c_q, 1.0, 0.0))
    blk = lax.broadcasted_iota(jnp.int32, (LANES, QB), 0)
    t_tok = s0 + lax.broadcasted_iota(jnp.int32, (LANES, QB), 1)
    tb_blk = t_tok // NSA_SEL_LEN
    forced = jnp.where(blk == 0, 1.0, 0.0) + jnp.where(blk == tb_blk, 1.0, 0.0) \
        + jnp.where(blk == tb_blk - 1, 1.0, 0.0)
    blkf = blk.astype(F32)
    q_tok = s0 + lax.broadcasted_iota(jnp.int32, (1, QB), 1)
    any_ok = lanes4(jnp.where(q_tok >= NSA_CMP_LEN - 1, 1.0, 0.0))
    out, score = [], []
    for g in groups:
        s = jnp.where(ok_c > 0.5, sc_ref[g], NSA_NEG)
        e = jnp.exp2(s - jnp.max(s, axis=0, keepdims=True))
        p = e * (any_ok / jnp.sum(e, axis=0, keepdims=True))
        out.append(gate[g][0:1, :] * jnp.dot(vc_ref[0, g], p.astype(BF16), preferred_element_type=F32))
        ps = p[:, 0:QB]
        for hg in range(1, HG):
            ps = ps + p[:, hg * QB:(hg + 1) * QB]
        ps_hi, ps_lo = _split_bf16(ps)
        imp = (jnp.dot(agg_ref[...], ps_hi, preferred_element_type=F32)
               + jnp.dot(agg_ref[...], ps_lo, preferred_element_type=F32))
        sco = jnp.where(forced > 0.5, 1e9, jnp.where(blk * NSA_SEL_LEN <= t_tok, imp, -1e9))
        score.append(jnp.where(blk < n_sel, sco, -jnp.inf))
    for _ in range(sel_k):
        for g in groups:
            mx = jnp.max(score[g], axis=0, keepdims=True)
            first = jnp.min(jnp.where(score[g] == mx, blkf, float(LANES)), axis=0, keepdims=True)
            score[g] = jnp.where(blkf == first, -jnp.inf, score[g])
    q2 = []
    for g in groups:
        picked = jnp.where(score[g] == -jnp.inf, 0.0, NSA_NEG)
        if n_sel < LANES:
            picked = jnp.where(blk < n_sel, picked, NSA_NEG)
        sel_q = jnp.transpose(picked)
        q2.append(jnp.concatenate([q[g], jnp.concatenate([sel_q] * HG, axis=0).astype(BF16)], axis=1))

    SEL, WIN = 0, 1

    def reset():
        m_ref[...] = jnp.full_like(m_ref, NSA_NEG)
        l_ref[...] = jnp.zeros_like(l_ref)
        acc_ref[...] = jnp.zeros_like(acc_ref)

    def tile_step(br, g, k, s, v_t):
        m_old = m_ref[br, g, k]
        m_new = jnp.maximum(m_old, jnp.max(s, axis=0, keepdims=True))
        alpha = jnp.exp2(m_old - m_new)
        p = jnp.exp2(s - m_new)
        l_ref[br, g, k] = l_ref[br, g, k] * alpha + jnp.sum(p, axis=0, keepdims=True)
        acc_ref[br, g, k] = (acc_ref[br, g, k] * alpha
                             + jnp.dot(v_t, p.astype(BF16), preferred_element_type=F32))
        m_ref[br, g, k] = m_new

    def finish(br, g):
        m = jnp.maximum(m_ref[br, g, 0], m_ref[br, g, 1])
        w0, w1 = jnp.exp2(m_ref[br, g, 0] - m), jnp.exp2(m_ref[br, g, 1] - m)
        return ((acc_ref[br, g, 0] * w0 + acc_ref[br, g, 1] * w1)
                * (1.0 / (l_ref[br, g, 0] * w0 + l_ref[br, g, 1] * w1)))

    def emit():
        heads_out = []
        for g in groups:
            o_t = out[g] + gate[g][1:2, :] * finish(SEL, g) + gate[g][2:3, :] * finish(WIN, g)
            heads_out += [jnp.transpose(o_t[:, hg * QB:(hg + 1) * QB]) for hg in range(HG)]
        o_ref[0] = jnp.concatenate(heads_out, axis=1).astype(o_ref.dtype)

    key_r = lax.broadcasted_iota(jnp.int32, (NSA_KT, QB), 0)
    key_q = lax.broadcasted_iota(jnp.int32, (NSA_KT, QB), 1)
    w_r = lax.broadcasted_iota(jnp.int32, (LANES, QB), 0)
    w_q = lax.broadcasted_iota(jnp.int32, (LANES, QB), 1)
    in_window = lanes4(jnp.where(w_r > w_q, 0.0, NSA_NEG))
    causal_w = lanes4(jnp.where(w_r <= w_q, 0.0, NSA_NEG))

    def sel_at(k0, n):
        k0 = pl.multiple_of(k0, LANES)
        return [(_dot_nt(ks_ref[0, g, pl.ds(k0, n), :], q2[g]), vs_ref[0, g, :, pl.ds(k0, n)])
                for g in groups]

    def win_at(k0, n):
        k0 = pl.multiple_of(k0, LANES)
        return [(_dot_nt(kw_ref[0, g, pl.ds(k0, n), :], q[g]), vw_ref[0, g, :, pl.ds(k0, n)])
                for g in groups]

    def delta_at(k0, n):
        return delta_t(kp_ref[pl.ds(pl.multiple_of(k0, LANES), n), :])

    def pair_scores(p, buf):
        for t in range(2):
            sc_t = sel_at((2 * p + t) * NSA_KT, NSA_KT)
            for g in groups:
                buf[g, t] = sc_t[g][0]

    def pair_softmax(p, buf):
        for t in range(2):
            k0 = pl.multiple_of((2 * p + t) * NSA_KT, NSA_KT)
            for g in groups:
                tile_step(SEL, g, t, buf[g, t], vs_ref[0, g, :, pl.ds(k0, NSA_KT)])

    def run_pairs(n_pairs):
        last = jnp.maximum(n_pairs - 1, 0)
        pair_scores(0, sa_ref)

        def two_pairs(j, carry):
            pair_scores(2 * j + 1, sb_ref)
            pair_softmax(2 * j, sa_ref)
            pair_scores(jnp.minimum(2 * j + 2, last), sa_ref)
            pair_softmax(2 * j + 1, sb_ref)
            return carry

        lax.fori_loop(0, n_pairs // 2, two_pairs, 0)

        @pl.when(n_pairs % 2 == 1)
        def _():
            pair_softmax(n_pairs - 1, sa_ref)

    @pl.when(std[qi] != 0)
    def _():
        reset()
        nb = (qi - 1) // 2
        run_pairs(nb // 2)

        @pl.when(nb % 2 == 1)
        def _():
            a = sel_at((nb - 1) * NSA_KT, NSA_KT)
            for g in groups:
                tile_step(SEL, g, 0, *a[g])

        @pl.when(qi % 2 == 0)
        def _():
            a = sel_at(nb * NSA_KT, LANES)
            for g in groups:
                tile_step(SEL, g, 1, *a[g])

        zeros = jnp.zeros((LANES, HG * QB), F32)
        sel_n = sel_at(s0 - LANES, 2 * LANES)
        win_p = win_at(s0 - NSA_WINDOW, NSA_WINDOW)
        win_d = win_at(s0, LANES)
        d_prev, d_diag = delta_at(s0 - LANES, LANES), delta_at(s0, LANES)
        for g in groups:
            bias = jnp.concatenate([d_prev[g], d_diag[g] + causal_w], axis=0)
            tile_step(SEL, g, 0, sel_n[g][0] + bias, sel_n[g][1])
        for g in groups:
            bias = jnp.concatenate([in_window] + [zeros] * (NSA_WTILES - 3) + [d_prev[g]], axis=0)
            tile_step(WIN, g, 0, win_p[g][0] + bias, win_p[g][1])
        for g in groups:
            tile_step(WIN, g, 1, win_d[g][0] + d_diag[g] + causal_w, win_d[g][1])
        emit()

    @pl.when(std[qi] == 0)
    def _():
        reset()
        n_past = (s0 + QB - 1) // NSA_KT
        n_pairs = nfast[qi] // 2
        run_pairs(n_pairs)

        def slow_body(kt, carry):
            a = sel_at(kt * NSA_KT, NSA_KT)
            d = delta_at(kt * NSA_KT, NSA_KT)
            for g in groups:
                tile_step(SEL, g, 1, a[g][0] + d[g], a[g][1])
            return carry

        lax.fori_loop(2 * n_pairs, n_past, slow_body, 0)
        k0 = n_past * NSA_KT
        a = sel_at(k0, NSA_KT)
        d = delta_at(k0, NSA_KT)
        causal = lanes4(jnp.where(k0 + key_r <= s0 + key_q, 0.0, NSA_NEG))
        for g in groups:
            tile_step(SEL, g, 0, a[g][0] + d[g] + causal, a[g][1])

        for j in range(NSA_WTILES):
            start = s0 - NSA_WINDOW + j * LANES

            @pl.when(start >= 0)
            def _():
                sw = win_at(start, LANES)
                d = maybe_delta(near_w[qi * NSA_WTILES + j],
                                kp_ref[pl.ds(pl.multiple_of(start, LANES), LANES), :])
                for g in groups:
                    s = sw[g][0] + d[g]
                    if j == 0:
                        s = s + in_window
                    elif j == NSA_WTILES - 1:
                        s = s + causal_w
                    tile_step(WIN, g, j % 2, s, sw[g][1])

        emit()


def _pos_digits(p, key_side):
    d0, d1, d2 = (p & 127).astype(F32), ((p >> 7) & 127).astype(F32), (p >> 14).astype(F32)
    one = jnp.ones_like(d0)
    if key_side:
        cols = [one, one, one, -d2, -d1, -d0]
    else:
        cols = [16384.0 * d2, 128.0 * d1, d0, 16384.0 * one, 128.0 * one, one]
    out = jnp.stack(cols, axis=-1)
    return jnp.pad(out, ((0, 0), (0, LANES - out.shape[-1]))).astype(BF16)


def nsa_mixer(q, k_c, v_c, k_s, v_s, k_w, v_w, gate_logits, positions,
              ck_pe, ck_w1, ck_w2, cv_pe, cv_w1, cv_w2, rel_bias):
    q, k_c, v_c, k_s, v_s, k_w, v_w = lax.optimization_barrier((q, k_c, v_c, k_s, v_s, k_w, v_w))
    bs, s, _ = q.shape
    G, HG, DH, QB = NSA_KV_GROUPS, NSA_HPG, NSA_DH, NSA_QBLOCK
    nqt = s // QB
    nc = s // NSA_CMP_STRIDE
    n_cmp = (s - NSA_CMP_LEN) // NSA_CMP_STRIDE + 1
    n_sel = s // NSA_SEL_LEN
    sel_k = min(NSA_SEL_TOPK, n_sel)
    n_kt = s // NSA_KT
    n_ct = nc // LANES
    assert s % (LANES * NSA_CMP_STRIDE) == 0 and n_sel <= LANES and NSA_KT == 2 * QB

    def by_group(t):
        return t.reshape(bs, s, G, DH).transpose(0, 2, 1, 3).astype(F32)

    def by_group_t(t):
        return t.reshape(bs, s, G, DH).transpose(0, 2, 3, 1).astype(BF16)

    ones2 = jnp.zeros((LANES - DH,), F32).at[:2].set(1.0)

    def keys_aug(t, blocks=False):
        parts = [t, jnp.broadcast_to(ones2, t.shape[:-1] + (LANES - DH,))]
        if blocks:
            onehot = (jnp.arange(s)[:, None] // NSA_SEL_LEN == jnp.arange(LANES)[None, :]).astype(F32)
            parts.append(jnp.broadcast_to(onehot, t.shape[:-2] + (s, LANES)))
        return jnp.concatenate(parts, axis=-1).astype(BF16)

    chunks = jnp.stack([by_group(k_c), by_group(v_c)]).reshape(2, bs, G, nc, NSA_CMP_STRIDE * DH)
    pe = jnp.stack([ck_pe, cv_pe]).reshape(2, 1, NSA_CMP_LEN * DH)
    cmp = nsa_compress(chunks.astype(BF16), jnp.broadcast_to(pe, (2, 8, NSA_CMP_LEN * DH)).astype(BF16),
                       jnp.stack([ck_w1, cv_w1]).astype(BF16), jnp.stack([ck_w2, cv_w2]).astype(BF16))
    kc, vc_t = keys_aug(cmp[0]), cmp[1].transpose(0, 1, 3, 2).astype(BF16)

    log2e = math.log2(math.e)
    rel_bias = rel_bias.astype(F32) * log2e
    far = rel_bias[REL_BUCKETS - 1]
    far_hi = far.astype(BF16).astype(F32)
    extra = jnp.zeros((NSA_HEADS, LANES - DH), F32).at[:, 0].set(far_hi).at[:, 1].set(far - far_hi)
    qh = (q.astype(F32) * (DH ** -0.5 * log2e)).reshape(bs, nqt, QB, G, HG, DH).transpose(0, 3, 1, 4, 2, 5)
    ex = jnp.broadcast_to(extra.reshape(1, G, 1, HG, 1, LANES - DH), (bs, G, nqt, HG, QB, LANES - DH))
    qs = jnp.concatenate([qh, ex], axis=-1).astype(BF16).reshape(bs, G, nqt, HG * QB, LANES)
    gl = gate_logits.astype(F32).reshape(bs, nqt, QB, G, HG, 3).transpose(0, 3, 1, 5, 4, 2)
    gl = jnp.pad(gl.reshape(bs, G, nqt, 3, HG * QB), ((0, 0),) * 3 + ((0, 5), (0, 0)))

    td = (rel_bias[_bucket_table()] - rel_bias[REL_BUCKETS - 1][None, :]).T.astype(F32)
    cmp_end = jnp.minimum(jnp.arange(nc) * NSA_CMP_STRIDE + NSA_CMP_LEN - 1, s - 1)
    cpos = positions[cmp_end]
    qmin = positions.reshape(nqt, QB).min(axis=1)
    kmax = positions.reshape(nqt, QB).max(axis=1)
    near_s = (qmin[:, None] - positions.reshape(n_kt, NSA_KT).max(axis=1)[None, :]) < REL_MAX_DIST
    n_past = (jnp.arange(nqt) * QB + QB - 1) // NSA_KT
    n_fast = jnp.minimum(jnp.argmax(jnp.concatenate([near_s, jnp.ones((nqt, 1), bool)], axis=1), axis=1),
                         n_past)
    wt = jnp.clip(jnp.arange(nqt)[:, None] - (NSA_WTILES - 1) + jnp.arange(NSA_WTILES)[None, :], 0, nqt - 1)
    near_w = (qmin[:, None] - kmax[wt]) < REL_MAX_DIST
    near_c = (qmin[:, None] - cpos.reshape(n_ct, LANES).max(axis=1)[None, :]) < REL_MAX_DIST
    before = lax.cummax(kmax)[jnp.maximum(jnp.arange(nqt) - 2, 0)]
    std = (jnp.arange(nqt) >= NSA_WTILES - 1) & (qmin - before >= REL_MAX_DIST)

    cmp_start = np.arange(nc) * NSA_CMP_STRIDE
    sel_start = np.arange(LANES) * NSA_SEL_LEN
    agg_t = ((cmp_start[None, :] <= sel_start[:, None] + NSA_SEL_LEN - 1)
             & (cmp_start[None, :] + NSA_CMP_LEN - 1 >= sel_start[:, None])
             & (np.arange(nc)[None, :] < n_cmp) & (np.arange(LANES)[:, None] < n_sel))

    cols = HG * QB
    full = lambda shape: pl.BlockSpec(shape, lambda b, i, *_: (0,) * len(shape))
    per_b = lambda n, w: pl.BlockSpec((1, G, n, w), lambda b, i, *_: (b, 0, 0, 0))
    per_tile = lambda n, w: pl.BlockSpec((1, G, 1, n, w), lambda b, i, *_: (b, 0, i, 0, 0))
    grid_spec = pltpu.PrefetchScalarGridSpec(
        num_scalar_prefetch=4,
        grid=(bs, nqt),
        in_specs=[per_tile(cols, LANES), per_tile(8, cols),
                  pl.BlockSpec((QB, LANES), lambda b, i, *_: (i, 0)),
                  full((s, LANES)), full((nc, LANES)),
                  per_b(s, 2 * LANES), per_b(DH, s), per_b(s, LANES), per_b(DH, s),
                  per_b(nc, LANES), per_b(DH, nc),
                  full((LANES, nc)), full((NSA_HEADS, LANES))],
        out_specs=pl.BlockSpec((1, QB, NSA_Q), lambda b, i, *_: (b, i, 0)),
        scratch_shapes=[pltpu.VMEM((2, G, 2, 1, cols), F32), pltpu.VMEM((2, G, 2, 1, cols), F32),
                        pltpu.VMEM((2, G, 2, DH, cols), F32)]
        + [pltpu.VMEM((G, 2, NSA_KT, cols), F32)] * 2 + [pltpu.VMEM((G, nc, cols), F32)],
    )
    return pl.pallas_call(
        functools.partial(_nsa_body, n_sel=n_sel, sel_k=sel_k),
        grid_spec=grid_spec,
        out_shape=jax.ShapeDtypeStruct((bs, s, NSA_Q), BF16),
        compiler_params=_params("parallel", "arbitrary"),
        name="nsa_attention",
    )(n_fast.astype(jnp.int32), std.astype(jnp.int32), near_w.reshape(-1).astype(jnp.int32),
      near_c.reshape(-1).astype(jnp.int32),
      qs, gl, _pos_digits(positions, False), _pos_digits(positions, True), _pos_digits(cpos, True),
      keys_aug(by_group(k_s), blocks=True), by_group_t(v_s), keys_aug(by_group(k_w)), by_group_t(v_w),
      kc, vc_t, jnp.asarray(agg_t, BF16), td)


SGU_TC = 512


def _sgu_body(uv_ref, lg_ref, lb_ref, w_ref, b_ref, o_ref):
    uv = jax.nn.gelu(uv_ref[...].astype(F32))
    u, v = uv[:, :SGU_WIDTH], uv[:, SGU_WIDTH:]
    mu = jnp.mean(v, axis=-1, keepdims=True)
    vc = v - mu
    var = jnp.mean(vc * vc, axis=-1, keepdims=True)
    vn = (vc * lax.rsqrt(var + RMS_EPS) * lg_ref[...] + lb_ref[...]).astype(BF16)
    gw = SGU_WIDTH // SGU_GROUPS
    for c in range(SGU_TC // SGU_CHUNK):
        rows = slice(c * SGU_CHUNK, (c + 1) * SGU_CHUNK)
        for g in range(SGU_GROUPS):
            cols = slice(g * gw, (g + 1) * gw)
            mix = jnp.dot(w_ref[g], vn[rows, cols], preferred_element_type=F32) + b_ref[:, g:g + 1]
            o_ref[rows, cols] = (u[rows, cols] * mix).astype(o_ref.dtype)


def sgu_mixer(proj, ln_g, ln_b, w_s, b_s):
    t = proj.shape[0]
    assert t % SGU_TC == 0 and OFF_SGU % (2 * SGU_WIDTH) == 0
    fixed = lambda i: (0, 0)
    return pl.pallas_call(
        _sgu_body,
        grid=(t // SGU_TC,),
        in_specs=[pl.BlockSpec((SGU_TC, 2 * SGU_WIDTH), lambda i: (i, OFF_SGU // (2 * SGU_WIDTH))),
                  pl.BlockSpec((1, SGU_WIDTH), fixed), pl.BlockSpec((1, SGU_WIDTH), fixed),
                  pl.BlockSpec((SGU_GROUPS, SGU_CHUNK, SGU_CHUNK), lambda i: (0, 0, 0)),
                  pl.BlockSpec((SGU_CHUNK, SGU_GROUPS), fixed)],
        out_specs=pl.BlockSpec((SGU_TC, SGU_WIDTH), lambda i: (i, 0)),
        out_shape=jax.ShapeDtypeStruct((t, SGU_WIDTH), BF16),
        compiler_params=_params("parallel"),
        name="sgu_gate",
    )(proj, ln_g.reshape(1, SGU_WIDTH), ln_b.reshape(1, SGU_WIDTH), jnp.tril(w_s).astype(BF16), b_s.T)


def _permute_w_in(w):
    sizes = ([SSM_WIDTH, GDN_QKV, GDN_HEADS, GDN_HEADS, GDN_HEADS * GDN_DV, NSA_Q]
             + [NSA_KV] * 6 + [3 * NSA_HEADS, 2 * SGU_WIDTH, N_BRANCH * D_MODEL])
    cuts = [int(c) for c in np.cumsum(sizes)[:-1]]
    (w_ssm, w_qkv, w_a, w_b, w_z, w_nq, w_kc, w_vc, w_ks, w_vs, w_kw, w_vw,
     w_ng, w_sgu, w_gate) = jnp.split(w, cuts, axis=-1)
    pad = jnp.zeros((w.shape[0], PROJ_COLS - OFF_SMALL - SMALL_USED), w.dtype)
    return jnp.concatenate([w_qkv, w_ssm, w_z, w_nq, w_sgu, w_gate, w_kc, w_vc, w_ks, w_vs,
                            w_kw, w_vw, w_a, w_b, w_ng, pad], axis=-1).astype(BF16)


def kernel(x, positions, norm_mix, norm_ffn, norm_final, w_in, ssm_a_re, ssm_a_im, ssm_log_dt, ssm_b_re, ssm_b_im, ssm_c_re, ssm_c_im, ssm_d, ssm_glu_w, gdn_conv_w, gdn_a_log, gdn_dt_bias, gdn_norm, nsa_cmp_k_pe, nsa_cmp_k_w1, nsa_cmp_k_w2, nsa_cmp_v_pe, nsa_cmp_v_w1, nsa_cmp_v_w2, rel_bias, sgu_ln_g, sgu_ln_b, sgu_w, sgu_b, w_br_ssm, w_br_gdn, w_br_nsa, w_br_sgu, w_out, ffn_w_gate, ffn_w_up, ffn_w_down, moe_router, moe_w_gate, moe_w_up, moe_w_down):
    bs, s, d = x.shape
    t = bs * s
    assert DEPTH == 2
    xf = x.reshape(t, d)
    for layer in range(DEPTH):
        proj = in_proj(xf, norm_mix[layer], _permute_w_in(w_in[layer]))
        p3 = proj.reshape(bs, s, PROJ_COLS)

        def seg(off, width):
            return p3[..., off:off + width]

        y_ssm = ssm_mixer(proj, bs, ssm_a_re[layer], ssm_a_im[layer], ssm_log_dt[layer],
                          ssm_b_re[layer], ssm_b_im[layer], ssm_c_re[layer], ssm_c_im[layer],
                          ssm_d[layer], ssm_glu_w[layer])
        y_gdn = gdn_mixer(proj, bs, gdn_conv_w[layer], gdn_a_log[layer], gdn_dt_bias[layer],
                          gdn_norm[layer])
        kvs = [seg(OFF_KV + i * NSA_KV, NSA_KV) for i in range(6)]
        y_nsa = nsa_mixer(seg(OFF_NQ, NSA_Q), *kvs, seg(OFF_SMALL + 2 * GDN_HEADS, 3 * NSA_HEADS),
                          positions, nsa_cmp_k_pe[layer], nsa_cmp_k_w1[layer], nsa_cmp_k_w2[layer],
                          nsa_cmp_v_pe[layer], nsa_cmp_v_w1[layer], nsa_cmp_v_w2[layer], rel_bias)
        y_sgu = sgu_mixer(proj, sgu_ln_g[layer], sgu_ln_b[layer], sgu_w[layer], sgu_b[layer])
        ys = [y_ssm, y_gdn, y_nsa.reshape(t, -1), y_sgu]
        ws = [w[layer].astype(BF16) for w in (w_br_ssm, w_br_gdn, w_br_nsa, w_br_sgu)]
        xf = merge(xf, proj, ys, ws, w_out[layer].astype(BF16))
        i = layer // 2
        if layer % 2 == 0:
            xf = dense_ffn(xf, norm_ffn[layer], ffn_w_gate[i], ffn_w_up[i], ffn_w_down[i])
        else:
            xf = moe_layer(xf, norm_ffn[layer], moe_router[i], moe_w_gate[i], moe_w_up[i],
                           moe_w_down[i], norm_final)
    return xf.reshape(bs, s, d)
```

```python
import functools
import math

import jax
import jax.numpy as jnp
from jax import lax
import numpy as np
from jax.experimental import pallas as pl
from jax.experimental.pallas import tpu as pltpu

F32 = jnp.float32
BF16 = jnp.bfloat16

D_MODEL = 1024
DEPTH = 2
SSM_WIDTH = D_MODEL // 2
SSM_GROUP = 16
SSM_GROUPS = SSM_WIDTH // SSM_GROUP
SSM_STATE = 64
GDN_HEADS = 4
GDN_DK = 128
GDN_DV = 128
GDN_CONV = 4
GDN_CHUNK = 64
NSA_HEADS = 8
NSA_KV_GROUPS = 2
NSA_HPG = NSA_HEADS // NSA_KV_GROUPS
NSA_DH = 64
NSA_CMP_LEN = 32
NSA_CMP_STRIDE = 16
NSA_CMP_HIDDEN = 128
NSA_SEL_LEN = 64
NSA_SEL_TOPK = 16
NSA_WINDOW = 512
NSA_QBLOCK = 128
SGU_WIDTH = D_MODEL // 2
SGU_GROUPS = 4
SGU_CHUNK = 128
REL_BUCKETS = 32
REL_MAX_DIST = 128
FFN_DENSE = 2816
N_EXPERTS = 8
TOP_K = 2
FFN_EXPERT = 3584
N_BRANCH = 4
RMS_EPS = 1e-6
GDN_QKV = GDN_HEADS * (2 * GDN_DK + GDN_DV)
NSA_Q = NSA_HEADS * NSA_DH
NSA_KV = NSA_KV_GROUPS * NSA_DH

LANES = 128
VMEM_LIMIT = 48 * 1024 * 1024
MASKED = -1e30

OFF_QKV = 0
OFF_SSM = OFF_QKV + GDN_QKV
OFF_Z = OFF_SSM + SSM_WIDTH
OFF_NQ = OFF_Z + GDN_HEADS * GDN_DV
OFF_SGU = OFF_NQ + NSA_Q
OFF_GATE = OFF_SGU + 2 * SGU_WIDTH
OFF_KV = OFF_GATE + N_BRANCH * D_MODEL
OFF_SMALL = OFF_KV + 6 * NSA_KV
SMALL_USED = 2 * GDN_HEADS + 3 * NSA_HEADS
PROJ_COLS = 9216


def _params(*sem):
    return pltpu.CompilerParams(dimension_semantics=sem, vmem_limit_bytes=VMEM_LIMIT)


def _rms(x, g):
    return x * lax.rsqrt(jnp.mean(x * x, axis=-1, keepdims=True) + RMS_EPS) * g


def _dot_nt(a, b):
    return lax.dot_general(a, b, (((1,), (1,)), ((), ())), preferred_element_type=F32)


def _dot_tn(a, b):
    return lax.dot_general(a, b, (((0,), (0,)), ((), ())), preferred_element_type=F32)


def _split_bf16(x):
    hi = x.astype(BF16)
    return hi, (x - hi.astype(F32)).astype(BF16)


def _in_proj_body(x_ref, g_ref, w_ref, o_ref, h_ref):
    @pl.when(pl.program_id(1) == 0)
    def _():
        h_ref[...] = _rms(x_ref[...], g_ref[...]).astype(BF16)

    o_ref[...] = jnp.dot(h_ref[...], w_ref[...], preferred_element_type=F32).astype(o_ref.dtype)


def in_proj(x, g, w, tm=2048, tn=1024):
    t, d = x.shape
    n = w.shape[1]
    return pl.pallas_call(
        _in_proj_body,
        grid=(t // tm, n // tn),
        in_specs=[pl.BlockSpec((tm, d), lambda i, j: (i, 0)),
                  pl.BlockSpec((1, d), lambda i, j: (0, 0)),
                  pl.BlockSpec((d, tn), lambda i, j: (0, j))],
        out_specs=pl.BlockSpec((tm, tn), lambda i, j: (i, j)),
        out_shape=jax.ShapeDtypeStruct((t, n), BF16),
        scratch_shapes=[pltpu.VMEM((tm, d), BF16)],
        compiler_params=_params("parallel", "arbitrary"),
        name="in_proj",
    )(x, g.reshape(1, d), w)


def _merge_body(x_ref, gate_ref, ya_ref, yb_ref, yc_ref, yd_ref,
                wa_ref, wb_ref, wc_ref, wd_ref, wo_ref, o_ref):
    def branch(k, y_ref, w_ref):
        p = jnp.dot(y_ref[...], w_ref[...], preferred_element_type=F32)
        return jax.nn.sigmoid(gate_ref[:, k * D_MODEL:(k + 1) * D_MODEL].astype(F32)) * p

    merged = (branch(0, ya_ref, wa_ref) + branch(1, yb_ref, wb_ref)
              + branch(2, yc_ref, wc_ref) + branch(3, yd_ref, wd_ref))
    o_ref[...] = x_ref[...] + jnp.dot(merged.astype(BF16), wo_ref[...],
                                      preferred_element_type=F32)


def merge(x, proj, ys, ws, w_out, tm=512):
    t, d = x.shape
    gate_blk = OFF_GATE // (N_BRANCH * D_MODEL)
    row = lambda i: (i, 0)
    fixed = lambda i: (0, 0)
    in_specs = [pl.BlockSpec((tm, d), row),
                pl.BlockSpec((tm, N_BRANCH * D_MODEL), lambda i: (i, gate_blk))]
    in_specs += [pl.BlockSpec((tm, y.shape[1]), row) for y in ys]
    in_specs += [pl.BlockSpec(w.shape, fixed) for w in ws]
    in_specs += [pl.BlockSpec(w_out.shape, fixed)]
    return pl.pallas_call(
        _merge_body,
        grid=(t // tm,),
        in_specs=in_specs,
        out_specs=pl.BlockSpec((tm, d), row),
        out_shape=jax.ShapeDtypeStruct((t, d), F32),
        compiler_params=_params("parallel"),
        name="merge",
    )(x, proj, *ys, *ws, w_out)


def _ffn_body(x_ref, g_ref, wg_ref, wu_ref, wd_ref, o_ref, h_ref, acc_ref):
    f = pl.program_id(1)

    @pl.when(f == 0)
    def _():
        h_ref[...] = _rms(x_ref[...], g_ref[...]).astype(BF16)
        acc_ref[...] = x_ref[...]

    h = h_ref[...]
    a = jnp.dot(h, wg_ref[...].astype(BF16), preferred_element_type=F32)
    u = jnp.dot(h, wu_ref[...].astype(BF16), preferred_element_type=F32)
    act = (a * jax.nn.sigmoid(a) * u).astype(BF16)
    acc_ref[...] += jnp.dot(act, wd_ref[...].astype(BF16), preferred_element_type=F32)

    @pl.when(f == pl.num_programs(1) - 1)
    def _():
        o_ref[...] = acc_ref[...]


def dense_ffn(x, g, wg, wu, wd, tm=1024, tf=256):
    t, d = x.shape
    nf = wg.shape[1]
    return pl.pallas_call(
        _ffn_body,
        grid=(t // tm, nf // tf),
        in_specs=[pl.BlockSpec((tm, d), lambda i, f: (i, 0)),
                  pl.BlockSpec((1, d), lambda i, f: (0, 0)),
                  pl.BlockSpec((d, tf), lambda i, f: (0, f)),
                  pl.BlockSpec((d, tf), lambda i, f: (0, f)),
                  pl.BlockSpec((tf, d), lambda i, f: (f, 0))],
        out_specs=pl.BlockSpec((tm, d), lambda i, f: (i, 0)),
        out_shape=jax.ShapeDtypeStruct((t, d), F32),
        scratch_shapes=[pltpu.VMEM((tm, d), BF16), pltpu.VMEM((tm, d), F32)],
        compiler_params=_params("parallel", "arbitrary"),
        name="dense_ffn",
    )(x, g.reshape(1, d), wg, wu, wd)


def _route_body(x_ref, g_ref, rhi_ref, rlo_ref, h_ref, idx_ref, wt_ref):
    h = _rms(x_ref[...], g_ref[...])
    hi = h.astype(BF16)
    lo = (h - hi.astype(F32)).astype(BF16)
    h_ref[...] = h
    logits = (jnp.dot(hi, rhi_ref[...], preferred_element_type=F32)
              + jnp.dot(hi, rlo_ref[...], preferred_element_type=F32)
              + jnp.dot(lo, rhi_ref[...], preferred_element_type=F32))
    col = lax.broadcasted_iota(jnp.int32, logits.shape, 1).astype(F32)
    neg = jnp.float32(-jnp.inf)
    lg = jnp.where(col < N_EXPERTS, logits, neg)
    m1 = jnp.max(lg, axis=-1, keepdims=True)
    i1 = jnp.min(jnp.where(lg == m1, col, float(LANES)), axis=-1, keepdims=True)
    lg2 = jnp.where(col == i1, neg, lg)
    m2 = jnp.max(lg2, axis=-1, keepdims=True)
    i2 = jnp.min(jnp.where(lg2 == m2, col, float(LANES)), axis=-1, keepdims=True)
    e = jnp.exp(m2 - m1)
    w1 = 1.0 / (1.0 + e)
    w2 = e / (1.0 + e)
    idx_ref[...] = jnp.where(col == 0, i1, jnp.where(col == 1, i2, 0.0)).astype(jnp.int32)
    wt_ref[...] = jnp.where(col == 0, w1, jnp.where(col == 1, w2, 0.0))


def moe_route(x, g, r_hi, r_lo, tm=512):
    t, d = x.shape
    row = lambda i: (i, 0)
    fixed = lambda i: (0, 0)
    return pl.pallas_call(
        _route_body,
        grid=(t // tm,),
        in_specs=[pl.BlockSpec((tm, d), row), pl.BlockSpec((1, d), fixed),
                  pl.BlockSpec((d, LANES), fixed), pl.BlockSpec((d, LANES), fixed)],
        out_specs=[pl.BlockSpec((tm, d), row), pl.BlockSpec((tm, LANES), row),
                   pl.BlockSpec((tm, LANES), row)],
        out_shape=[jax.ShapeDtypeStruct((t, d), F32),
                   jax.ShapeDtypeStruct((t, LANES), jnp.int32),
                   jax.ShapeDtypeStruct((t, LANES), F32)],
        compiler_params=_params("parallel"),
        name="moe_route",
    )(x, g.reshape(1, d), r_hi, r_lo)


def _experts_body(te_ref, nu_ref, src_ref, h_hbm, wg_ref, wu_ref, wd_ref, o_ref,
                  xbuf_ref, xs_ref, acc_ref, sem, *, rows_per_step):
    i = pl.program_id(0)
    f = pl.program_id(1)
    tm = xs_ref.shape[0]
    n_rows = xbuf_ref.shape[1]
    n_used = nu_ref[0]
    used = i < n_used
    has_next = i + 1 < n_used
    slot = i % 2

    def start_row(tile, slot_, r):
        tok = src_ref[tile * tm + r]
        pltpu.make_async_copy(h_hbm.at[pl.ds(tok, 1), :], xbuf_ref.at[slot_, pl.ds(r, 1), :],
                              sem.at[slot_]).start()

    @pl.when(f == 0)
    def _():
        acc_ref[...] = jnp.zeros_like(acc_ref)

        @pl.when(used)
        def _():
            @pl.when(i == 0)
            def _():
                def row(r, carry):
                    start_row(0, 0, r)
                    return carry

                lax.fori_loop(0, n_rows, row, 0, unroll=8)

            pltpu.make_async_copy(h_hbm.at[pl.ds(0, n_rows), :], xbuf_ref.at[slot], sem.at[slot]).wait()
            xs_ref[...] = xbuf_ref[slot, 0:tm, :].astype(BF16)

    def ffn_step():
        h = xs_ref[...]
        a = jnp.dot(h, wg_ref[0].astype(BF16), preferred_element_type=F32)
        u = jnp.dot(h, wu_ref[0].astype(BF16), preferred_element_type=F32)
        act = (a * jax.nn.sigmoid(a) * u).astype(BF16)
        acc_ref[...] += jnp.dot(act, wd_ref[0].astype(BF16), preferred_element_type=F32)

    @pl.when(used & has_next)
    def _():
        for j in range(rows_per_step):
            start_row(i + 1, 1 - slot, f * rows_per_step + j)
        ffn_step()

    @pl.when(used & jnp.logical_not(has_next))
    def _():
        ffn_step()

    @pl.when(f == pl.num_programs(1) - 1)
    def _():
        o_ref[...] = acc_ref[...].astype(o_ref.dtype)


def moe_experts(tile_expert, n_used, src, h, wg, wu, wd, tm, tf=512):
    d = h.shape[1]
    nf = wg.shape[2] // tf
    rows_per_step = 8 * (-(-tm // (8 * nf)))
    n_rows = rows_per_step * nf
    n_tiles = src.shape[0] // tm
    src = jnp.pad(src, (0, n_rows - tm))
    grid_spec = pltpu.PrefetchScalarGridSpec(
        num_scalar_prefetch=3,
        grid=(n_tiles, nf),
        in_specs=[pl.BlockSpec(memory_space=pl.ANY),
                  pl.BlockSpec((1, d, tf), lambda i, f, te, *_: (te[i], 0, f)),
                  pl.BlockSpec((1, d, tf), lambda i, f, te, *_: (te[i], 0, f)),
                  pl.BlockSpec((1, tf, d), lambda i, f, te, *_: (te[i], f, 0))],
        out_specs=pl.BlockSpec((tm, d), lambda i, f, *_: (i, 0)),
        scratch_shapes=[pltpu.VMEM((2, n_rows, d), F32), pltpu.VMEM((tm, d), BF16),
                        pltpu.VMEM((tm, d), F32), pltpu.SemaphoreType.DMA((2,))],
    )
    return pl.pallas_call(
        functools.partial(_experts_body, rows_per_step=rows_per_step),
        grid_spec=grid_spec,
        out_shape=jax.ShapeDtypeStruct((n_tiles * tm, d), BF16),
        compiler_params=_params("arbitrary", "arbitrary"),
        name="moe_experts",
    )(tile_expert, n_used, src, h, wg, wu, wd)


def _combine_norm_body(x_ref, ya_ref, yb_ref, wt_ref, g_ref, o_ref):
    wt = wt_ref[...]
    y = wt[:, 0:1] * ya_ref[...].astype(F32) + wt[:, 1:2] * yb_ref[...].astype(F32)
    o_ref[...] = _rms(x_ref[...] + y, g_ref[...])


def combine_norm(x, ya, yb, wts, g, tm=1024):
    t, d = x.shape
    row = lambda i: (i, 0)
    return pl.pallas_call(
        _combine_norm_body,
        grid=(t // tm,),
        in_specs=[pl.BlockSpec((tm, d), row)] * 3 + [pl.BlockSpec((tm, LANES), row),
                                                     pl.BlockSpec((1, d), lambda i: (0, 0))],
        out_specs=pl.BlockSpec((tm, d), row),
        out_shape=jax.ShapeDtypeStruct((t, d), F32),
        compiler_params=_params("parallel"),
        name="combine_norm",
    )(x, ya, yb, wts, g.reshape(1, d))


def moe_layer(x, g_norm, router, wg, wu, wd, g_final, tm=1024):
    t, d = x.shape
    r = jnp.zeros((d, LANES), F32).at[:, :N_EXPERTS].set(router)
    r_hi = r.astype(BF16)
    r_lo = (r - r_hi.astype(F32)).astype(BF16)
    h, idx, wts = moe_route(x, g_norm, r_hi, r_lo)
    e_flat = jnp.concatenate([idx[:, 0], idx[:, 1]])
    onehot = (e_flat[:, None] == jnp.arange(N_EXPERTS)[None, :]).astype(jnp.int32)
    csum = jnp.cumsum(onehot, axis=0)
    rank = jnp.sum((csum - onehot) * onehot, axis=1)
    counts = csum[-1]
    padded = ((counts + tm - 1) // tm) * tm
    ends = jnp.cumsum(padded)
    starts = ends - padded
    dest = starts[e_flat] + rank
    n_tiles = (TOP_K * t) // tm + N_EXPERTS
    p = n_tiles * tm
    tok = jnp.concatenate([jnp.arange(t, dtype=jnp.int32)] * TOP_K)
    src = jnp.zeros((p,), jnp.int32).at[dest].set(tok)
    tile_expert = jnp.minimum(
        jnp.searchsorted(ends, jnp.arange(n_tiles, dtype=jnp.int32) * tm, side="right"),
        N_EXPERTS - 1).astype(jnp.int32)
    n_used = (ends[-1] // tm).astype(jnp.int32).reshape(1)
    ys = moe_experts(tile_expert, n_used, src, h, wg, wu, wd, tm)
    ya = jnp.take(ys, dest[:t], axis=0, mode="clip")
    yb = jnp.take(ys, dest[t:], axis=0, mode="clip")
    return combine_norm(x, ya, yb, wts, g_final)


SSM_TC = 512
SSM_SUB = 128
SSM_HALF = 256
SSM_NSTATE = SSM_GROUPS * SSM_STATE


def _ssm_body(u_ref, bw_ref, cw_ref, d_ref, glu_ref, ar_ref, ai_ref, pr_ref, pi_ref, o_ref,
              xr_ref, xi_ref, cr_ref, ci_ref):
    @pl.when(pl.program_id(1) == 0)
    def _():
        cr_ref[...] = jnp.zeros_like(cr_ref)
        ci_ref[...] = jnp.zeros_like(ci_ref)

    ub = u_ref[...]
    u = ub.astype(F32)
    nblk = SSM_WIDTH // SSM_HALF
    sblk = SSM_NSTATE // nblk
    for k in range(nblk):
        bu = jnp.dot(ub[:, k * SSM_HALF:(k + 1) * SSM_HALF], bw_ref[k], preferred_element_type=F32)
        xr_ref[:, k * sblk:(k + 1) * sblk] = bu[:, :sblk]
        xi_ref[:, k * sblk:(k + 1) * sblk] = bu[:, sblk:]

    row8 = lax.broadcasted_iota(jnp.int32, (SSM_SUB, LANES), 0) % 8
    n_grp = SSM_SUB // 8

    def strip(c, carry):
        col = pl.ds(pl.multiple_of(c * LANES, LANES), LANES)
        c_r = cr_ref[:, col]
        c_i = ci_ref[:, col]
        p_r = pr_ref[0:8, col]
        p_i = pi_ref[0:8, col]
        step_mul = [(jnp.where(row8 >= (1 << i), ar_ref[i:i + 1, col], 0.0),
                     jnp.where(row8 >= (1 << i), ai_ref[i:i + 1, col], 0.0)) for i in range(3)]
        for sub in range(SSM_TC // SSM_SUB):
            rows = slice(sub * SSM_SUB, (sub + 1) * SSM_SUB)
            xr = xr_ref[rows, col]
            xi = xi_ref[rows, col]
            for i in range(3):
                a_r, a_i = step_mul[i]
                sr, si = pltpu.roll(xr, 1 << i, 0), pltpu.roll(xi, 1 << i, 0)
                xr, xi = xr + a_r * sr - a_i * si, xi + a_r * si + a_i * sr
            out_r, out_i = [], []
            for r in range(n_grp):
                g_r = xr[8 * r:8 * r + 8] + p_r * c_r - p_i * c_i
                g_i = xi[8 * r:8 * r + 8] + p_r * c_i + p_i * c_r
                c_r, c_i = g_r[7:8, :], g_i[7:8, :]
                out_r.append(g_r)
                out_i.append(g_i)
            xr_ref[rows, col] = jnp.concatenate(out_r, axis=0)
            xi_ref[rows, col] = jnp.concatenate(out_i, axis=0)
        cr_ref[:, col] = c_r
        ci_ref[:, col] = c_i
        return carry

    lax.fori_loop(0, SSM_NSTATE // LANES, strip, 0)

    ys = []
    for k in range(nblk):
        st = jnp.concatenate([xr_ref[:, k * sblk:(k + 1) * sblk].astype(BF16),
                              xi_ref[:, k * sblk:(k + 1) * sblk].astype(BF16)], axis=1)
        ys.append(jnp.dot(st, cw_ref[k], preferred_element_type=F32))
    y = jax.nn.gelu(jnp.concatenate(ys, axis=1) + d_ref[...] * u)
    gated = y * jax.nn.sigmoid(jnp.dot(y.astype(BF16), glu_ref[...], preferred_element_type=F32))
    o_ref[...] = gated.astype(o_ref.dtype)


def ssm_mixer(proj, bs, a_re, a_im, log_dt, b_re, b_im, c_re, c_im, d_skip, glu_w):
    t = proj.shape[0]
    s = t // bs
    nt = s // SSM_TC
    assert s % SSM_TC == 0 and OFF_SSM % SSM_WIDTH == 0
    nblk = SSM_WIDTH // SSM_HALF
    gpb = SSM_GROUPS // nblk
    dt = jnp.exp(log_dt)[:, None]
    mag = jnp.exp(a_re * dt)
    abar_r, abar_i = mag * jnp.cos(a_im * dt), mag * jnp.sin(a_im * dt)
    nr, ni = abar_r - 1.0, abar_i
    den = a_re * a_re + a_im * a_im
    sr, si = (nr * a_re + ni * a_im) / den, (ni * a_re - nr * a_im) / den
    bbar_r = sr[..., None] * b_re - si[..., None] * b_im
    bbar_i = sr[..., None] * b_im + si[..., None] * b_re

    def powers(k):
        kk = k.astype(F32)[:, None, None]
        m = jnp.exp(kk * (a_re * dt))
        return ((m * jnp.cos(kk * (a_im * dt))).reshape(-1, SSM_NSTATE),
                (m * jnp.sin(kk * (a_im * dt))).reshape(-1, SSM_NSTATE))

    ar, ai = powers(2 ** jnp.arange(8))
    pr, pi = powers(jnp.arange(1, SSM_SUB + 1))
    eye = jnp.eye(gpb, dtype=F32)

    def in_block(w):
        return jnp.einsum('gnp,gh->gphn', w, eye).reshape(gpb * SSM_GROUP, gpb * SSM_STATE)

    def out_block(w):
        return jnp.einsum('gpn,gh->gnhp', w, eye).reshape(gpb * SSM_STATE, gpb * SSM_GROUP)

    bw = jnp.stack([jnp.concatenate([in_block(bbar_r[k * gpb:(k + 1) * gpb]),
                                     in_block(bbar_i[k * gpb:(k + 1) * gpb])], axis=1)
                    for k in range(nblk)]).astype(BF16)
    cw = jnp.stack([jnp.concatenate([out_block(c_re[k * gpb:(k + 1) * gpb]),
                                     -out_block(c_im[k * gpb:(k + 1) * gpb])], axis=0)
                    for k in range(nblk)]).astype(BF16)
    fixed2 = lambda b, i: (0, 0)
    fixed3 = lambda b, i: (0, 0, 0)
    return pl.pallas_call(
        _ssm_body,
        grid=(bs, nt),
        in_specs=[pl.BlockSpec((SSM_TC, SSM_WIDTH), lambda b, i: (b * nt + i, OFF_SSM // SSM_WIDTH)),
                  pl.BlockSpec(bw.shape, fixed3), pl.BlockSpec(cw.shape, fixed3),
                  pl.BlockSpec((1, SSM_WIDTH), fixed2), pl.BlockSpec((SSM_WIDTH, SSM_WIDTH), fixed2),
                  pl.BlockSpec((8, SSM_NSTATE), fixed2), pl.BlockSpec((8, SSM_NSTATE), fixed2),
                  pl.BlockSpec((SSM_SUB, SSM_NSTATE), fixed2), pl.BlockSpec((SSM_SUB, SSM_NSTATE), fixed2)],
        out_specs=pl.BlockSpec((SSM_TC, SSM_WIDTH), lambda b, i: (b * nt + i, 0)),
        out_shape=jax.ShapeDtypeStruct((t, SSM_WIDTH), BF16),
        scratch_shapes=[pltpu.VMEM((SSM_TC, SSM_NSTATE), F32), pltpu.VMEM((SSM_TC, SSM_NSTATE), F32),
                        pltpu.VMEM((1, SSM_NSTATE), F32), pltpu.VMEM((1, SSM_NSTATE), F32)],
        compiler_params=_params("parallel", "arbitrary"),
        name="ssm_scan",
    )(proj, bw, cw, d_skip.reshape(1, SSM_WIDTH), glu_w.astype(BF16), ar, ai, pr, pi)


GDN_TC = 256
GDN_HALO = 8


def _gdn_body(nega_ref, dtb_ref, q_ref, k_ref, v_ref, z_ref, sm_ref, wq_ref, wk_ref, wv_ref, ng_ref,
              o_ref, state_ref, hq_ref, hk_ref, hv_ref, vnew_ref):
    TC, CH, DK, H = GDN_TC, GDN_CHUNK, GDN_DK, GDN_HEADS
    heads = range(H)

    @pl.when(pl.program_id(1) == 0)
    def _():
        state_ref[...] = jnp.zeros_like(state_ref)
        hq_ref[...] = jnp.zeros_like(hq_ref)
        hk_ref[...] = jnp.zeros_like(hk_ref)
        hv_ref[...] = jnp.zeros_like(hv_ref)

    def conv_silu(x_ref, halo_ref, w_ref):
        x = x_ref[...].astype(F32)
        xx = jnp.concatenate([halo_ref[...], x], axis=0)
        w = w_ref[...]
        y = w[GDN_CONV - 1:GDN_CONV] * x
        for d in range(1, GDN_CONV):
            y = y + w[GDN_CONV - 1 - d:GDN_CONV - d] * pltpu.roll(xx, d, 0)[GDN_HALO:GDN_HALO + TC]
        halo_ref[...] = x[TC - GDN_HALO:TC]
        return y * jax.nn.sigmoid(y)

    def per_head(x):
        return [x[:, h * LANES:(h + 1) * LANES] for h in heads]

    q = per_head(conv_silu(q_ref, hq_ref, wq_ref))
    k = per_head(conv_silu(k_ref, hk_ref, wk_ref))
    v = per_head(conv_silu(v_ref, hv_ref, wv_ref))
    q = [x * lax.rsqrt(jnp.sum(x * x, axis=-1, keepdims=True) + 1e-6) * (DK ** -0.5) for x in q]
    k = [x * lax.rsqrt(jnp.sum(x * x, axis=-1, keepdims=True) + 1e-6) for x in k]

    small = sm_ref[...].astype(F32)
    beta = [jax.nn.sigmoid(small[:, H + h:H + h + 1]) for h in heads]
    la = []
    for h in heads:
        xa = small[:, h:h + 1] + dtb_ref[h]
        softplus = jnp.maximum(xa, 0.0) + jnp.log(1.0 + jnp.exp(-jnp.abs(xa)))
        la.append(jnp.broadcast_to(nega_ref[h] * softplus, (TC, LANES)))

    ri = lax.broadcasted_iota(jnp.int32, (TC, TC), 0)
    ci = lax.broadcasted_iota(jnp.int32, (TC, TC), 1)
    same = (ri // CH) == (ci // CH)
    causal = jnp.where(same, jnp.where(ci <= ri, 1.0, 0.0), 0.0)
    strict = jnp.where(ci < ri, causal, 0.0)
    eye = jnp.where(ri == ci, 1.0, 0.0)
    tri = causal.astype(BF16)
    lane = lax.broadcasted_iota(jnp.int32, (TC, LANES), 1)

    g = []
    for h in heads:
        la_hi, la_lo = _split_bf16(la[h])
        g.append(jnp.dot(tri, la_hi, preferred_element_type=F32)
                 + jnp.dot(tri, la_lo, preferred_element_type=F32))
    decay = []
    for h in heads:
        g_hi = g[h].astype(BF16).astype(F32)
        g_lo = g[h] - g_hi
        lhs = jnp.where(lane == 0, g_hi, jnp.where(lane == 1, g_lo, jnp.where(lane < 4, 1.0, 0.0)))
        rhs = jnp.where(lane < 2, 1.0, jnp.where(lane == 2, -g_hi, jnp.where(lane == 3, -g_lo, 0.0)))
        decay.append(jnp.exp(jnp.where(causal > 0.5, _dot_nt(lhs.astype(BF16), rhs.astype(BF16)), MASKED)))

    kb = [k[h] * beta[h] for h in heads]
    k_b = [x.astype(BF16) for x in k]
    lmat = [strict * _dot_nt(kb[h].astype(BF16), k_b[h]) * decay[h] for h in heads]
    tinv = [eye - x for x in lmat]
    pw = lmat
    for _ in range(CH.bit_length() - 2):
        pw = [jnp.dot(x.astype(BF16), x.astype(BF16), preferred_element_type=F32) for x in pw]
        tinv = [tinv[h] + jnp.dot(tinv[h].astype(BF16), pw[h].astype(BF16), preferred_element_type=F32)
                for h in heads]
    eg = [jnp.exp(x) for x in g]
    sol = [jnp.dot(tinv[h].astype(BF16),
                   jnp.concatenate([v[h] * beta[h], kb[h] * eg[h]], axis=1).astype(BF16),
                   preferred_element_type=F32) for h in heads]
    attn = [(causal * _dot_nt(q[h].astype(BF16), k_b[h]) * decay[h]).astype(BF16) for h in heads]
    q_st = [(q[h] * eg[h]).astype(BF16) for h in heads]

    vnew_ref[...] = jnp.zeros_like(vnew_ref)
    for c in range(TC // CH):
        rows = slice(c * CH, (c + 1) * CH)
        g_last = [x[(c + 1) * CH - 1:(c + 1) * CH, :] for x in g]
        st = [state_ref[h] for h in heads]
        st_b = [x.astype(BF16) for x in st]
        v_new = [sol[h][rows, :GDN_DV]
                 - jnp.dot(sol[h][rows, GDN_DV:].astype(BF16), st_b[h], preferred_element_type=F32)
                 for h in heads]
        for h in heads:
            vnew_ref[h, rows, :] = v_new[h].astype(BF16)
        o_c = [jnp.dot(q_st[h][rows], st_b[h], preferred_element_type=F32)
               + jnp.dot(attn[h][rows], vnew_ref[h], preferred_element_type=F32) for h in heads]
        for h in heads:
            k_st = (k[h][rows] * jnp.exp(g_last[h] - g[h][rows])).astype(BF16)
            state_ref[h] = st[h] * jnp.exp(g_last[h][:, :1]) + _dot_tn(k_st, v_new[h].astype(BF16))
        for h in heads:
            o = o_c[h] * lax.rsqrt(jnp.mean(o_c[h] * o_c[h], axis=-1, keepdims=True) + RMS_EPS) * ng_ref[...]
            zc = z_ref[rows, h * LANES:(h + 1) * LANES].astype(F32)
            o_ref[rows, h * LANES:(h + 1) * LANES] = (o * (zc * jax.nn.sigmoid(zc))).astype(o_ref.dtype)


def gdn_mixer(proj, bs, conv_w, a_log, dt_bias, norm_g):
    t = proj.shape[0]
    s = t // bs
    nt = s // GDN_TC
    H = GDN_HEADS
    hw = H * LANES
    assert s % GDN_TC == 0 and GDN_DK == LANES and GDN_DV == LANES and OFF_QKV == 0 and OFF_Z % hw == 0
    cw = jnp.zeros((8, GDN_QKV), F32).at[:GDN_CONV].set(conv_w)
    tok = lambda blk: pl.BlockSpec((GDN_TC, hw), lambda b, i: (b * nt + i, blk))
    wspec = lambda blk: pl.BlockSpec((8, hw), lambda b, i: (0, blk))
    smem = pl.BlockSpec(memory_space=pltpu.SMEM)
    return pl.pallas_call(
        _gdn_body,
        grid=(bs, nt),
        in_specs=[smem, smem, tok(0), tok(1), tok(2), tok(OFF_Z // hw),
                  pl.BlockSpec((GDN_TC, 2 * LANES), lambda b, i: (b * nt + i, OFF_SMALL // (2 * LANES))),
                  wspec(0), wspec(1), wspec(2), pl.BlockSpec((1, GDN_DV), lambda b, i: (0, 0))],
        out_specs=pl.BlockSpec((GDN_TC, hw), lambda b, i: (b * nt + i, 0)),
        out_shape=jax.ShapeDtypeStruct((t, hw), BF16),
        scratch_shapes=[pltpu.VMEM((H, GDN_DK, GDN_DV), F32)] + [pltpu.VMEM((GDN_HALO, hw), F32)] * 3
        + [pltpu.VMEM((H, GDN_TC, GDN_DV), BF16)],
        compiler_params=_params("parallel", "arbitrary"),
        name="gdn_delta",
    )(-jnp.exp(a_log), dt_bias.astype(F32), proj, proj, proj, proj, proj, cw, cw, cw,
      norm_g.reshape(1, GDN_DV))


NSA_KT = 256
NSA_NEG = MASKED
NSA_WTILES = NSA_WINDOW // LANES + 1


def _bucket_table():
    n = np.arange(LANES)
    max_exact = REL_BUCKETS // 2
    nf = np.maximum(n, 1).astype(np.float32)
    large = max_exact + (np.log(nf / np.float32(max_exact)) / np.float32(math.log(REL_MAX_DIST / max_exact))
                         * np.float32(REL_BUCKETS - max_exact)).astype(np.int32)
    tbl = np.where(n < max_exact, n, np.minimum(large, REL_BUCKETS - 1))
    assert tbl[-1] == REL_BUCKETS - 1 and REL_MAX_DIST <= LANES
    return tbl


def _compress_body(x_ref, pe_ref, w1_ref, w2_ref, o_ref):
    x = x_ref[0, 0, 0]
    w1 = w1_ref[0]
    half = NSA_CMP_STRIDE * NSA_DH
    nc = x.shape[0]
    a = jnp.dot(x, w1[:half], preferred_element_type=F32)
    b = jnp.dot(x, w1[half:], preferred_element_type=F32)
    pe_h = jnp.dot(pe_ref[0], w1, preferred_element_type=F32)
    hid = a + pltpu.roll(b, nc - 1, 0) + pe_h[0:1, :]
    o_ref[0, 0, 0] = jnp.dot(jax.nn.gelu(hid).astype(BF16), w2_ref[0], preferred_element_type=F32)


def nsa_compress(x, pe, w1, w2):
    _, bs, g, nc, width = x.shape
    return pl.pallas_call(
        _compress_body,
        grid=(2, bs, g),
        in_specs=[pl.BlockSpec((1, 1, 1, nc, width), lambda i, b, j: (i, b, j, 0, 0)),
                  pl.BlockSpec((1,) + pe.shape[1:], lambda i, b, j: (i, 0, 0)),
                  pl.BlockSpec((1,) + w1.shape[1:], lambda i, b, j: (i, 0, 0)),
                  pl.BlockSpec((1,) + w2.shape[1:], lambda i, b, j: (i, 0, 0))],
        out_specs=pl.BlockSpec((1, 1, 1, nc, NSA_DH), lambda i, b, j: (i, b, j, 0, 0)),
        out_shape=jax.ShapeDtypeStruct((2, bs, g, nc, NSA_DH), F32),
        compiler_params=_params("parallel", "parallel", "parallel"),
        name="nsa_compress",
    )(x, pe, w1, w2)


def _nsa_body(nfast, std, near_w, near_c, q_ref, ex_ref, gl_ref, qp_ref, kp_ref, cp_ref,
              ks_ref, vs_ref, kw_ref, vw_ref, kc_ref, vc_ref, agg_ref, td_ref, o_ref,
              m_ref, l_ref, acc_ref, sa_ref, sb_ref, sc_ref, *, n_sel, sel_k):
    QB, HG, G = NSA_QBLOCK, NSA_HPG, NSA_KV_GROUPS
    groups = range(G)
    qi = pl.program_id(1)
    s0 = qi * QB
    nc = kc_ref.shape[2]
    n_ct = nc // LANES
    q_all = q_ref[...].astype(F32) * (NSA_DH ** -0.5 * math.log2(math.e))
    q = []
    for g in groups:
        rows = jnp.concatenate([q_all[:, (g * HG + hg) * NSA_DH:(g * HG + hg + 1) * NSA_DH]
                                for hg in range(HG)], axis=0)
        q.append(jnp.concatenate([rows, ex_ref[g]], axis=1).astype(BF16))
    qp = qp_ref[...]

    def lanes4(x):
        return jnp.concatenate([x] * HG, axis=1)

    def delta_t(kp):
        idx = jnp.clip(_dot_nt(kp, qp), 0.0, float(LANES - 1)).astype(jnp.int32)
        outs = []
        for g in groups:
            heads = []
            for hg in range(HG):
                tb = jnp.broadcast_to(td_ref[g * HG + hg:g * HG + hg + 1, :], idx.shape)
                heads.append(jnp.take_along_axis(tb, idx, axis=1))
            outs.append(jnp.concatenate(heads, axis=1))
        return outs

    def maybe_delta(flag, kp):
        zeros = jnp.zeros((kp.shape[0], HG * QB), F32)
        return lax.cond(flag != 0, lambda: tuple(delta_t(kp)), lambda: (zeros,) * G)

    gate = [jax.nn.sigmoid(gl_ref[0, g, 0]) for g in groups]

    for g in groups:
        sc_ref[g] = _dot_nt(kc_ref[0, g], q[g])
    near_rows = 4 * 8

    @pl.when(std[qi] != 0)
    def _():
        c0 = pl.multiple_of(jnp.minimum(qi * (QB // NSA_CMP_STRIDE) - near_rows // 2, nc - near_rows), 8)
        d = delta_t(cp_ref[pl.ds(c0, near_rows), :])
        for g in groups:
            sc_ref[g, pl.ds(c0, near_rows), :] += d[g]

    @pl.when(std[qi] == 0)
    def _():
        for ct in range(n_ct):
            @pl.when(near_c[qi * n_ct + ct] != 0)
            def _():
                d = delta_t(cp_ref[ct * LANES:(ct + 1) * LANES, :])
                for g in groups:
                    sc_ref[g, ct * LANES:(ct + 1) * LANES, :] += d[g]

    c_id = lax.broadcasted_iota(jnp.int32, (nc, QB), 0)
    c_q = lax.broadcasted_iota(jnp.int32, (nc, QB), 1)
    ok_c = lanes4(jnp.where(c_id * NSA_CMP_STRIDE + (NSA_CMP_LEN - 1) <= s0 + c_q, 1.0, 0.0))
    blk = lax.broadcasted_iota(jnp.int32, (LANES, QB), 0)
    t_tok = s0 + lax.broadcasted_iota(jnp.int32, (LANES, QB), 1)
    tb_blk = t_tok // NSA_SEL_LEN
    forced = jnp.where(blk == 0, 1.0, 0.0) + jnp.where(blk == tb_blk, 1.0, 0.0) \
        + jnp.where(blk == tb_blk - 1, 1.0, 0.0)
    blkf = blk.astype(F32)
    q_tok = s0 + lax.broadcasted_iota(jnp.int32, (1, QB), 1)
    any_ok = lanes4(jnp.where(q_tok >= NSA_CMP_LEN - 1, 1.0, 0.0))
    out, score = [], []
    for g in groups:
        s = jnp.where(ok_c > 0.5, sc_ref[g], NSA_NEG)
        e = jnp.exp2(s - jnp.max(s, axis=0, keepdims=True))
        p = e * (any_ok / jnp.sum(e, axis=0, keepdims=True))
        out.append(gate[g][0:1, :] * jnp.dot(vc_ref[0, g], p.astype(BF16), preferred_element_type=F32))
        ps = p[:, 0:QB]
        for hg in range(1, HG):
            ps = ps + p[:, hg * QB:(hg + 1) * QB]
        ps_hi, ps_lo = _split_bf16(ps)
        imp = (jnp.dot(agg_ref[...], ps_hi, preferred_element_type=F32)
               + jnp.dot(agg_ref[...], ps_lo, preferred_element_type=F32))
        sco = jnp.where(forced > 0.5, 1e9, jnp.where(blk * NSA_SEL_LEN <= t_tok, imp, -1e9))
        score.append(jnp.where(blk < n_sel, sco, -jnp.inf))
    for _ in range(sel_k):
        for g in groups:
            mx = jnp.max(score[g], axis=0, keepdims=True)
            first = jnp.min(jnp.where(score[g] == mx, blkf, float(LANES)), axis=0, keepdims=True)
            score[g] = jnp.where(blkf == first, -jnp.inf, score[g])
    q2 = []
    for g in groups:
        picked = jnp.where(score[g] == -jnp.inf, 0.0, NSA_NEG)
        if n_sel < LANES:
            picked = jnp.where(blk < n_sel, picked, NSA_NEG)
        sel_q = jnp.transpose(picked)
        q2.append(jnp.concatenate([q[g], jnp.concatenate([sel_q] * HG, axis=0).astype(BF16)], axis=1))

    SEL, WIN = 0, 1

    def reset():
        m_ref[...] = jnp.full_like(m_ref, NSA_NEG)
        l_ref[...] = jnp.zeros_like(l_ref)
        acc_ref[...] = jnp.zeros_like(acc_ref)

    def tile_step(br, g, k, s, v_t):
        m_old = m_ref[br, g, k]
        m_new = jnp.maximum(m_old, jnp.max(s, axis=0, keepdims=True))
        alpha = jnp.exp2(m_old - m_new)
        p = jnp.exp2(s - m_new)
        l_ref[br, g, k] = l_ref[br, g, k] * alpha + jnp.sum(p, axis=0, keepdims=True)
        acc_ref[br, g, k] = (acc_ref[br, g, k] * alpha
                             + jnp.dot(v_t, p.astype(BF16), preferred_element_type=F32))
        m_ref[br, g, k] = m_new

    def finish(br, g):
        m = jnp.maximum(m_ref[br, g, 0], m_ref[br, g, 1])
        w0, w1 = jnp.exp2(m_ref[br, g, 0] - m), jnp.exp2(m_ref[br, g, 1] - m)
        return ((acc_ref[br, g, 0] * w0 + acc_ref[br, g, 1] * w1)
                * (1.0 / (l_ref[br, g, 0] * w0 + l_ref[br, g, 1] * w1)))

    def emit():
        heads_out = []
        for g in groups:
            o_t = out[g] + gate[g][1:2, :] * finish(SEL, g) + gate[g][2:3, :] * finish(WIN, g)
            heads_out += [jnp.transpose(o_t[:, hg * QB:(hg + 1) * QB]) for hg in range(HG)]
        o_ref[0] = jnp.concatenate(heads_out, axis=1).astype(o_ref.dtype)

    key_r = lax.broadcasted_iota(jnp.int32, (NSA_KT, QB), 0)
    key_q = lax.broadcasted_iota(jnp.int32, (NSA_KT, QB), 1)
    w_r = lax.broadcasted_iota(jnp.int32, (LANES, QB), 0)
    w_q = lax.broadcasted_iota(jnp.int32, (LANES, QB), 1)
    in_window = lanes4(jnp.where(w_r > w_q, 0.0, NSA_NEG))
    causal_w = lanes4(jnp.where(w_r <= w_q, 0.0, NSA_NEG))

    def sel_at(k0, n):
        k0 = pl.multiple_of(k0, LANES)
        return [(_dot_nt(ks_ref[0, g, pl.ds(k0, n), :], q2[g]), vs_ref[0, g, :, pl.ds(k0, n)])
                for g in groups]

    def win_at(k0, n):
        k0 = pl.multiple_of(k0, LANES)
        return [(_dot_nt(kw_ref[0, g, pl.ds(k0, n), :], q[g]), vw_ref[0, g, :, pl.ds(k0, n)])
                for g in groups]

    def delta_at(k0, n):
        return delta_t(kp_ref[pl.ds(pl.multiple_of(k0, LANES), n), :])

    def pair_scores(p, buf):
        for t in range(2):
            sc_t = sel_at((2 * p + t) * NSA_KT, NSA_KT)
            for g in groups:
                buf[g, t] = sc_t[g][0]

    def pair_softmax(p, buf):
        for t in range(2):
            k0 = pl.multiple_of((2 * p + t) * NSA_KT, NSA_KT)
            for g in groups:
                tile_step(SEL, g, t, buf[g, t], vs_ref[0, g, :, pl.ds(k0, NSA_KT)])

    def run_pairs(n_pairs):
        last = jnp.maximum(n_pairs - 1, 0)
        pair_scores(0, sa_ref)

        def two_pairs(j, carry):
            pair_scores(2 * j + 1, sb_ref)
            pair_softmax(2 * j, sa_ref)
            pair_scores(jnp.minimum(2 * j + 2, last), sa_ref)
            pair_softmax(2 * j + 1, sb_ref)
            return carry

        lax.fori_loop(0, n_pairs // 2, two_pairs, 0)

        @pl.when(n_pairs % 2 == 1)
        def _():
            pair_softmax(n_pairs - 1, sa_ref)

    @pl.when(std[qi] != 0)
    def _():
        reset()
        nb = (qi - 1) // 2
        run_pairs(nb // 2)

        @pl.when(nb % 2 == 1)
        def _():
            a = sel_at((nb - 1) * NSA_KT, NSA_KT)
            for g in groups:
                tile_step(SEL, g, 0, *a[g])

        @pl.when(qi % 2 == 0)
        def _():
            a = sel_at(nb * NSA_KT, LANES)
            for g in groups:
                tile_step(SEL, g, 1, *a[g])

        zeros = jnp.zeros((LANES, HG * QB), F32)
        sel_n = sel_at(s0 - LANES, 2 * LANES)
        win_p = win_at(s0 - NSA_WINDOW, NSA_WINDOW)
        win_d = win_at(s0, LANES)
        d_prev, d_diag = delta_at(s0 - LANES, LANES), delta_at(s0, LANES)
        for g in groups:
            bias = jnp.concatenate([d_prev[g], d_diag[g] + causal_w], axis=0)
            tile_step(SEL, g, 0, sel_n[g][0] + bias, sel_n[g][1])
        for g in groups:
            bias = jnp.concatenate([in_window] + [zeros] * (NSA_WTILES - 3) + [d_prev[g]], axis=0)
            tile_step(WIN, g, 0, win_p[g][0] + bias, win_p[g][1])
        for g in groups:
            tile_step(WIN, g, 1, win_d[g][0] + d_diag[g] + causal_w, win_d[g][1])
        emit()

    @pl.when(std[qi] == 0)
    def _():
        reset()
        n_past = (s0 + QB - 1) // NSA_KT
        n_pairs = nfast[qi] // 2
        run_pairs(n_pairs)

        def slow_body(kt, carry):
            a = sel_at(kt * NSA_KT, NSA_KT)
            d = delta_at(kt * NSA_KT, NSA_KT)
            for g in groups:
                tile_step(SEL, g, 1, a[g][0] + d[g], a[g][1])
            return carry

        lax.fori_loop(2 * n_pairs, n_past, slow_body, 0)
        k0 = n_past * NSA_KT
        a = sel_at(k0, NSA_KT)
        d = delta_at(k0, NSA_KT)
        causal = lanes4(jnp.where(k0 + key_r <= s0 + key_q, 0.0, NSA_NEG))
        for g in groups:
            tile_step(SEL, g, 0, a[g][0] + d[g] + causal, a[g][1])

        for j in range(NSA_WTILES):
            start = s0 - NSA_WINDOW + j * LANES

            @pl.when(start >= 0)
            def _():
                sw = win_at(start, LANES)
                d = maybe_delta(near_w[qi * NSA_WTILES + j],
                                kp_ref[pl.ds(pl.multiple_of(start, LANES), LANES), :])
                for g in groups:
                    s = sw[g][0] + d[g]
                    if j == 0:
                        s = s + in_window
                    elif j == NSA_WTILES - 1:
                        s = s + causal_w
                    tile_step(WIN, g, j % 2, s, sw[g][1])

        emit()


def _pos_digits(p, key_side):
    d0, d1, d2 = (p & 127).astype(F32), ((p >> 7) & 127).astype(F32), (p >> 14).astype(F32)
    one = jnp.ones_like(d0)
    if key_side:
        cols = [one, one, one, -d2, -d1, -d0]
    else:
        cols = [16384.0 * d2, 128.0 * d1, d0, 16384.0 * one, 128.0 * one, one]
    out = jnp.stack(cols, axis=-1)
    return jnp.pad(out, ((0, 0), (0, LANES - out.shape[-1]))).astype(BF16)


def nsa_mixer(proj, k_c, v_c, k_s, v_s, k_w, v_w, gate_logits, positions,
              ck_pe, ck_w1, ck_w2, cv_pe, cv_w1, cv_w2, rel_bias):
    k_c, v_c, k_s, v_s, k_w, v_w = lax.optimization_barrier((k_c, v_c, k_s, v_s, k_w, v_w))
    bs, s, _ = k_c.shape
    G, HG, DH, QB = NSA_KV_GROUPS, NSA_HPG, NSA_DH, NSA_QBLOCK
    nqt = s // QB
    nc = s // NSA_CMP_STRIDE
    n_cmp = (s - NSA_CMP_LEN) // NSA_CMP_STRIDE + 1
    n_sel = s // NSA_SEL_LEN
    sel_k = min(NSA_SEL_TOPK, n_sel)
    n_kt = s // NSA_KT
    n_ct = nc // LANES
    assert s % (LANES * NSA_CMP_STRIDE) == 0 and n_sel <= LANES and NSA_KT == 2 * QB

    def by_group(t):
        return t.reshape(bs, s, G, DH).transpose(0, 2, 1, 3).astype(F32)

    def by_group_t(t):
        return t.reshape(bs, s, G, DH).transpose(0, 2, 3, 1).astype(BF16)

    ones2 = jnp.zeros((LANES - DH,), F32).at[:2].set(1.0)

    def keys_aug(t, blocks=False):
        parts = [t, jnp.broadcast_to(ones2, t.shape[:-1] + (LANES - DH,))]
        if blocks:
            onehot = (jnp.arange(s)[:, None] // NSA_SEL_LEN == jnp.arange(LANES)[None, :]).astype(F32)
            parts.append(jnp.broadcast_to(onehot, t.shape[:-2] + (s, LANES)))
        return jnp.concatenate(parts, axis=-1).astype(BF16)

    chunks = jnp.stack([by_group(k_c), by_group(v_c)]).reshape(2, bs, G, nc, NSA_CMP_STRIDE * DH)
    pe = jnp.stack([ck_pe, cv_pe]).reshape(2, 1, NSA_CMP_LEN * DH)
    cmp = nsa_compress(chunks.astype(BF16), jnp.broadcast_to(pe, (2, 8, NSA_CMP_LEN * DH)).astype(BF16),
                       jnp.stack([ck_w1, cv_w1]).astype(BF16), jnp.stack([ck_w2, cv_w2]).astype(BF16))
    kc, vc_t = keys_aug(cmp[0]), cmp[1].transpose(0, 1, 3, 2).astype(BF16)

    rel_bias = rel_bias.astype(F32) * math.log2(math.e)
    far = rel_bias[REL_BUCKETS - 1]
    far_hi = far.astype(BF16).astype(F32)
    extra = jnp.zeros((NSA_HEADS, LANES - DH), F32).at[:, 0].set(far_hi).at[:, 1].set(far - far_hi)
    extra = jnp.broadcast_to(extra.reshape(G, HG, 1, LANES - DH), (G, HG, QB, LANES - DH))
    extra = extra.reshape(G, HG * QB, LANES - DH)
    gl = gate_logits.astype(F32).reshape(bs, nqt, QB, G, HG, 3).transpose(0, 3, 1, 5, 4, 2)
    gl = jnp.pad(gl.reshape(bs, G, nqt, 3, HG * QB), ((0, 0),) * 3 + ((0, 5), (0, 0)))

    td = (rel_bias[_bucket_table()] - rel_bias[REL_BUCKETS - 1][None, :]).T.astype(F32)
    cmp_end = jnp.minimum(jnp.arange(nc) * NSA_CMP_STRIDE + NSA_CMP_LEN - 1, s - 1)
    cpos = positions[cmp_end]
    qmin = positions.reshape(nqt, QB).min(axis=1)
    kmax = positions.reshape(nqt, QB).max(axis=1)
    near_s = (qmin[:, None] - positions.reshape(n_kt, NSA_KT).max(axis=1)[None, :]) < REL_MAX_DIST
    n_past = (jnp.arange(nqt) * QB + QB - 1) // NSA_KT
    n_fast = jnp.minimum(jnp.argmax(jnp.concatenate([near_s, jnp.ones((nqt, 1), bool)], axis=1), axis=1),
                         n_past)
    wt = jnp.clip(jnp.arange(nqt)[:, None] - (NSA_WTILES - 1) + jnp.arange(NSA_WTILES)[None, :], 0, nqt - 1)
    near_w = (qmin[:, None] - kmax[wt]) < REL_MAX_DIST
    near_c = (qmin[:, None] - cpos.reshape(n_ct, LANES).max(axis=1)[None, :]) < REL_MAX_DIST
    before = lax.cummax(kmax)[jnp.maximum(jnp.arange(nqt) - 2, 0)]
    std = (jnp.arange(nqt) >= NSA_WTILES - 1) & (qmin - before >= REL_MAX_DIST)

    cmp_start = np.arange(nc) * NSA_CMP_STRIDE
    sel_start = np.arange(LANES) * NSA_SEL_LEN
    agg_t = ((cmp_start[None, :] <= sel_start[:, None] + NSA_SEL_LEN - 1)
             & (cmp_start[None, :] + NSA_CMP_LEN - 1 >= sel_start[:, None])
             & (np.arange(nc)[None, :] < n_cmp) & (np.arange(LANES)[:, None] < n_sel))

    cols = HG * QB
    full = lambda shape: pl.BlockSpec(shape, lambda b, i, *_: (0,) * len(shape))
    per_b = lambda n, w: pl.BlockSpec((1, G, n, w), lambda b, i, *_: (b, 0, 0, 0))
    per_tile = lambda n, w: pl.BlockSpec((1, G, 1, n, w), lambda b, i, *_: (b, 0, i, 0, 0))
    grid_spec = pltpu.PrefetchScalarGridSpec(
        num_scalar_prefetch=4,
        grid=(bs, nqt),
        in_specs=[pl.BlockSpec((QB, NSA_Q), lambda b, i, *_: (b * nqt + i, OFF_NQ // NSA_Q)),
                  full((G, cols, LANES - DH)), per_tile(8, cols),
                  pl.BlockSpec((QB, LANES), lambda b, i, *_: (i, 0)),
                  full((s, LANES)), full((nc, LANES)),
                  per_b(s, 2 * LANES), per_b(DH, s), per_b(s, LANES), per_b(DH, s),
                  per_b(nc, LANES), per_b(DH, nc),
                  full((LANES, nc)), full((NSA_HEADS, LANES))],
        out_specs=pl.BlockSpec((1, QB, NSA_Q), lambda b, i, *_: (b, i, 0)),
        scratch_shapes=[pltpu.VMEM((2, G, 2, 1, cols), F32), pltpu.VMEM((2, G, 2, 1, cols), F32),
                        pltpu.VMEM((2, G, 2, DH, cols), F32)]
        + [pltpu.VMEM((G, 2, NSA_KT, cols), F32)] * 2 + [pltpu.VMEM((G, nc, cols), F32)],
    )
    return pl.pallas_call(
        functools.partial(_nsa_body, n_sel=n_sel, sel_k=sel_k),
        grid_spec=grid_spec,
        out_shape=jax.ShapeDtypeStruct((bs, s, NSA_Q), BF16),
        compiler_params=_params("parallel", "arbitrary"),
        name="nsa_attention",
    )(n_fast.astype(jnp.int32), std.astype(jnp.int32), near_w.reshape(-1).astype(jnp.int32),
      near_c.reshape(-1).astype(jnp.int32),
      proj, extra, gl, _pos_digits(positions, False), _pos_digits(positions, True), _pos_digits(cpos, True),
      keys_aug(by_group(k_s), blocks=True), by_group_t(v_s), keys_aug(by_group(k_w)), by_group_t(v_w),
      kc, vc_t, jnp.asarray(agg_t, BF16), td)


SGU_TC = 512


def _sgu_body(uv_ref, lg_ref, lb_ref, w_ref, b_ref, o_ref):
    uv = jax.nn.gelu(uv_ref[...].astype(F32))
    u, v = uv[:, :SGU_WIDTH], uv[:, SGU_WIDTH:]
    mu = jnp.mean(v, axis=-1, keepdims=True)
    vc = v - mu
    var = jnp.mean(vc * vc, axis=-1, keepdims=True)
    vn = (vc * lax.rsqrt(var + RMS_EPS) * lg_ref[...] + lb_ref[...]).astype(BF16)
    gw = SGU_WIDTH // SGU_GROUPS
    for c in range(SGU_TC // SGU_CHUNK):
        rows = slice(c * SGU_CHUNK, (c + 1) * SGU_CHUNK)
        for g in range(SGU_GROUPS):
            cols = slice(g * gw, (g + 1) * gw)
            mix = jnp.dot(w_ref[g], vn[rows, cols], preferred_element_type=F32) + b_ref[:, g:g + 1]
            o_ref[rows, cols] = (u[rows, cols] * mix).astype(o_ref.dtype)


def sgu_mixer(proj, ln_g, ln_b, w_s, b_s):
    t = proj.shape[0]
    assert t % SGU_TC == 0 and OFF_SGU % (2 * SGU_WIDTH) == 0
    fixed = lambda i: (0, 0)
    return pl.pallas_call(
        _sgu_body,
        grid=(t // SGU_TC,),
        in_specs=[pl.BlockSpec((SGU_TC, 2 * SGU_WIDTH), lambda i: (i, OFF_SGU // (2 * SGU_WIDTH))),
                  pl.BlockSpec((1, SGU_WIDTH), fixed), pl.BlockSpec((1, SGU_WIDTH), fixed),
                  pl.BlockSpec((SGU_GROUPS, SGU_CHUNK, SGU_CHUNK), lambda i: (0, 0, 0)),
                  pl.BlockSpec((SGU_CHUNK, SGU_GROUPS), fixed)],
        out_specs=pl.BlockSpec((SGU_TC, SGU_WIDTH), lambda i: (i, 0)),
        out_shape=jax.ShapeDtypeStruct((t, SGU_WIDTH), BF16),
        compiler_params=_params("parallel"),
        name="sgu_gate",
    )(proj, ln_g.reshape(1, SGU_WIDTH), ln_b.reshape(1, SGU_WIDTH), jnp.tril(w_s).astype(BF16), b_s.T)


def _permute_w_in(w):
    sizes = ([SSM_WIDTH, GDN_QKV, GDN_HEADS, GDN_HEADS, GDN_HEADS * GDN_DV, NSA_Q]
             + [NSA_KV] * 6 + [3 * NSA_HEADS, 2 * SGU_WIDTH, N_BRANCH * D_MODEL])
    cuts = [int(c) for c in np.cumsum(sizes)[:-1]]
    (w_ssm, w_qkv, w_a, w_b, w_z, w_nq, w_kc, w_vc, w_ks, w_vs, w_kw, w_vw,
     w_ng, w_sgu, w_gate) = jnp.split(w, cuts, axis=-1)
    pad = jnp.zeros((w.shape[0], PROJ_COLS - OFF_SMALL - SMALL_USED), w.dtype)
    return jnp.concatenate([w_qkv, w_ssm, w_z, w_nq, w_sgu, w_gate, w_kc, w_vc, w_ks, w_vs,
                            w_kw, w_vw, w_a, w_b, w_ng, pad], axis=-1).astype(BF16)


def kernel(x, positions, norm_mix, norm_ffn, norm_final, w_in, ssm_a_re, ssm_a_im, ssm_log_dt, ssm_b_re, ssm_b_im, ssm_c_re, ssm_c_im, ssm_d, ssm_glu_w, gdn_conv_w, gdn_a_log, gdn_dt_bias, gdn_norm, nsa_cmp_k_pe, nsa_cmp_k_w1, nsa_cmp_k_w2, nsa_cmp_v_pe, nsa_cmp_v_w1, nsa_cmp_v_w2, rel_bias, sgu_ln_g, sgu_ln_b, sgu_w, sgu_b, w_br_ssm, w_br_gdn, w_br_nsa, w_br_sgu, w_out, ffn_w_gate, ffn_w_up, ffn_w_down, moe_router, moe_w_gate, moe_w_up, moe_w_down):
    bs, s, d = x.shape
    t = bs * s
    assert DEPTH == 2
    xf = x.reshape(t, d)
    for layer in range(DEPTH):
        proj = in_proj(xf, norm_mix[layer], _permute_w_in(w_in[layer]))
        p3 = proj.reshape(bs, s, PROJ_COLS)

        def seg(off, width):
            return p3[..., off:off + width]

        y_ssm = ssm_mixer(proj, bs, ssm_a_re[layer], ssm_a_im[layer], ssm_log_dt[layer],
                          ssm_b_re[layer], ssm_b_im[layer], ssm_c_re[layer], ssm_c_im[layer],
                          ssm_d[layer], ssm_glu_w[layer])
        y_gdn = gdn_mixer(proj, bs, gdn_conv_w[layer], gdn_a_log[layer], gdn_dt_bias[layer],
                          gdn_norm[layer])
        kvs = [seg(OFF_KV + i * NSA_KV, NSA_KV) for i in range(6)]
        y_nsa = nsa_mixer(proj, *kvs, seg(OFF_SMALL + 2 * GDN_HEADS, 3 * NSA_HEADS),
                          positions, nsa_cmp_k_pe[layer], nsa_cmp_k_w1[layer], nsa_cmp_k_w2[layer],
                          nsa_cmp_v_pe[layer], nsa_cmp_v_w1[layer], nsa_cmp_v_w2[layer], rel_bias)
        y_sgu = sgu_mixer(proj, sgu_ln_g[layer], sgu_ln_b[layer], sgu_w[layer], sgu_b[layer])
        ys = [y_ssm, y_gdn, y_nsa.reshape(t, -1), y_sgu]
        ws = [w[layer].astype(BF16) for w in (w_br_ssm, w_br_gdn, w_br_nsa, w_br_sgu)]
        xf = merge(xf, proj, ys, ws, w_out[layer].astype(BF16))
        i = layer // 2
        if layer % 2 == 0:
            xf = dense_ffn(xf, norm_ffn[layer], ffn_w_gate[i], ffn_w_up[i], ffn_w_down[i])
        else:
            xf = moe_layer(xf, norm_ffn[layer], moe_router[i], moe_w_gate[i], moe_w_up[i],
                           moe_w_down[i], norm_final)
    return xf.reshape(bs, s, d)
```

```python
import functools
import math

import jax
import jax.numpy as jnp
from jax import lax
import numpy as np
from jax.experimental import pallas as pl
from jax.experimental.pallas import tpu as pltpu

F32 = jnp.float32
BF16 = jnp.bfloat16

D_MODEL = 1024
DEPTH = 2
SSM_WIDTH = D_MODEL // 2
SSM_GROUP = 16
SSM_GROUPS = SSM_WIDTH // SSM_GROUP
SSM_STATE = 64
GDN_HEADS = 4
GDN_DK = 128
GDN_DV = 128
GDN_CONV = 4
GDN_CHUNK = 64
NSA_HEADS = 8
NSA_KV_GROUPS = 2
NSA_HPG = NSA_HEADS // NSA_KV_GROUPS
NSA_DH = 64
NSA_CMP_LEN = 32
NSA_CMP_STRIDE = 16
NSA_CMP_HIDDEN = 128
NSA_SEL_LEN = 64
NSA_SEL_TOPK = 16
NSA_WINDOW = 512
NSA_QBLOCK = 128
SGU_WIDTH = D_MODEL // 2
SGU_GROUPS = 4
SGU_CHUNK = 128
REL_BUCKETS = 32
REL_MAX_DIST = 128
FFN_DENSE = 2816
N_EXPERTS = 8
TOP_K = 2
FFN_EXPERT = 3584
N_BRANCH = 4
RMS_EPS = 1e-6
GDN_QKV = GDN_HEADS * (2 * GDN_DK + GDN_DV)
NSA_Q = NSA_HEADS * NSA_DH
NSA_KV = NSA_KV_GROUPS * NSA_DH

LANES = 128
VMEM_LIMIT = 48 * 1024 * 1024
MASKED = -1e30

OFF_QKV = 0
OFF_SSM = OFF_QKV + GDN_QKV
OFF_Z = OFF_SSM + SSM_WIDTH
OFF_NQ = OFF_Z + GDN_HEADS * GDN_DV
OFF_SGU = OFF_NQ + NSA_Q
OFF_GATE = OFF_SGU + 2 * SGU_WIDTH
OFF_KV = OFF_GATE + N_BRANCH * D_MODEL
OFF_SMALL = OFF_KV + 6 * NSA_KV
SMALL_USED = 2 * GDN_HEADS + 3 * NSA_HEADS
PROJ_COLS = 9216


def _params(*sem):
    return pltpu.CompilerParams(dimension_semantics=sem, vmem_limit_bytes=VMEM_LIMIT)


def _rms(x, g):
    return x * lax.rsqrt(jnp.mean(x * x, axis=-1, keepdims=True) + RMS_EPS) * g


def _dot_nt(a, b):
    return lax.dot_general(a, b, (((1,), (1,)), ((), ())), preferred_element_type=F32)


def _dot_tn(a, b):
    return lax.dot_general(a, b, (((0,), (0,)), ((), ())), preferred_element_type=F32)


def _split_bf16(x):
    hi = x.astype(BF16)
    return hi, (x - hi.astype(F32)).astype(BF16)


def _in_proj_body(x_ref, g_ref, w_ref, o_ref, h_ref):
    @pl.when(pl.program_id(1) == 0)
    def _():
        h_ref[...] = _rms(x_ref[...], g_ref[...]).astype(BF16)

    o_ref[...] = jnp.dot(h_ref[...], w_ref[...], preferred_element_type=F32).astype(o_ref.dtype)


def in_proj(x, g, w, tm=2048, tn=1024):
    t, d = x.shape
    n = w.shape[1]
    return pl.pallas_call(
        _in_proj_body,
        grid=(t // tm, n // tn),
        in_specs=[pl.BlockSpec((tm, d), lambda i, j: (i, 0)),
                  pl.BlockSpec((1, d), lambda i, j: (0, 0)),
                  pl.BlockSpec((d, tn), lambda i, j: (0, j))],
        out_specs=pl.BlockSpec((tm, tn), lambda i, j: (i, j)),
        out_shape=jax.ShapeDtypeStruct((t, n), BF16),
        scratch_shapes=[pltpu.VMEM((tm, d), BF16)],
        compiler_params=_params("parallel", "arbitrary"),
        name="in_proj",
    )(x, g.reshape(1, d), w)


def _merge_body(x_ref, gate_ref, ya_ref, yb_ref, yc_ref, yd_ref,
                wa_ref, wb_ref, wc_ref, wd_ref, wo_ref, o_ref):
    def branch(k, y_ref, w_ref):
        p = jnp.dot(y_ref[...], w_ref[...], preferred_element_type=F32)
        return jax.nn.sigmoid(gate_ref[:, k * D_MODEL:(k + 1) * D_MODEL].astype(F32)) * p

    merged = (branch(0, ya_ref, wa_ref) + branch(1, yb_ref, wb_ref)
              + branch(2, yc_ref, wc_ref) + branch(3, yd_ref, wd_ref))
    o_ref[...] = x_ref[...] + jnp.dot(merged.astype(BF16), wo_ref[...],
                                      preferred_element_type=F32)


def merge(x, proj, ys, ws, w_out, tm=512):
    t, d = x.shape
    gate_blk = OFF_GATE // (N_BRANCH * D_MODEL)
    row = lambda i: (i, 0)
    fixed = lambda i: (0, 0)
    in_specs = [pl.BlockSpec((tm, d), row),
                pl.BlockSpec((tm, N_BRANCH * D_MODEL), lambda i: (i, gate_blk))]
    in_specs += [pl.BlockSpec((tm, y.shape[1]), row) for y in ys]
    in_specs += [pl.BlockSpec(w.shape, fixed) for w in ws]
    in_specs += [pl.BlockSpec(w_out.shape, fixed)]
    return pl.pallas_call(
        _merge_body,
        grid=(t // tm,),
        in_specs=in_specs,
        out_specs=pl.BlockSpec((tm, d), row),
        out_shape=jax.ShapeDtypeStruct((t, d), F32),
        compiler_params=_params("parallel"),
        name="merge",
    )(x, proj, *ys, *ws, w_out)


def _ffn_body(x_ref, g_ref, wg_ref, wu_ref, wd_ref, o_ref, h_ref, acc_ref):
    f = pl.program_id(1)

    @pl.when(f == 0)
    def _():
        h_ref[...] = _rms(x_ref[...], g_ref[...]).astype(BF16)
        acc_ref[...] = x_ref[...]

    h = h_ref[...]
    a = jnp.dot(h, wg_ref[...].astype(BF16), preferred_element_type=F32)
    u = jnp.dot(h, wu_ref[...].astype(BF16), preferred_element_type=F32)
    act = (a * jax.nn.sigmoid(a) * u).astype(BF16)
    acc_ref[...] += jnp.dot(act, wd_ref[...].astype(BF16), preferred_element_type=F32)

    @pl.when(f == pl.num_programs(1) - 1)
    def _():
        o_ref[...] = acc_ref[...]


def dense_ffn(x, g, wg, wu, wd, tm=1024, tf=256):
    t, d = x.shape
    nf = wg.shape[1]
    return pl.pallas_call(
        _ffn_body,
        grid=(t // tm, nf // tf),
        in_specs=[pl.BlockSpec((tm, d), lambda i, f: (i, 0)),
                  pl.BlockSpec((1, d), lambda i, f: (0, 0)),
                  pl.BlockSpec((d, tf), lambda i, f: (0, f)),
                  pl.BlockSpec((d, tf), lambda i, f: (0, f)),
                  pl.BlockSpec((tf, d), lambda i, f: (f, 0))],
        out_specs=pl.BlockSpec((tm, d), lambda i, f: (i, 0)),
        out_shape=jax.ShapeDtypeStruct((t, d), F32),
        scratch_shapes=[pltpu.VMEM((tm, d), BF16), pltpu.VMEM((tm, d), F32)],
        compiler_params=_params("parallel", "arbitrary"),
        name="dense_ffn",
    )(x, g.reshape(1, d), wg, wu, wd)


def _route_body(x_ref, g_ref, rhi_ref, rlo_ref, h_ref, idx_ref, wt_ref):
    h = _rms(x_ref[...], g_ref[...])
    hi = h.astype(BF16)
    lo = (h - hi.astype(F32)).astype(BF16)
    h_ref[...] = h
    logits = (jnp.dot(hi, rhi_ref[...], preferred_element_type=F32)
              + jnp.dot(hi, rlo_ref[...], preferred_element_type=F32)
              + jnp.dot(lo, rhi_ref[...], preferred_element_type=F32))
    col = lax.broadcasted_iota(jnp.int32, logits.shape, 1).astype(F32)
    neg = jnp.float32(-jnp.inf)
    lg = jnp.where(col < N_EXPERTS, logits, neg)
    m1 = jnp.max(lg, axis=-1, keepdims=True)
    i1 = jnp.min(jnp.where(lg == m1, col, float(LANES)), axis=-1, keepdims=True)
    lg2 = jnp.where(col == i1, neg, lg)
    m2 = jnp.max(lg2, axis=-1, keepdims=True)
    i2 = jnp.min(jnp.where(lg2 == m2, col, float(LANES)), axis=-1, keepdims=True)
    e = jnp.exp(m2 - m1)
    w1 = 1.0 / (1.0 + e)
    w2 = e / (1.0 + e)
    idx_ref[...] = jnp.where(col == 0, i1, jnp.where(col == 1, i2, 0.0)).astype(jnp.int32)
    wt_ref[...] = jnp.where(col == 0, w1, jnp.where(col == 1, w2, 0.0))


def moe_route(x, g, r_hi, r_lo, tm=512):
    t, d = x.shape
    row = lambda i: (i, 0)
    fixed = lambda i: (0, 0)
    return pl.pallas_call(
        _route_body,
        grid=(t // tm,),
        in_specs=[pl.BlockSpec((tm, d), row), pl.BlockSpec((1, d), fixed),
                  pl.BlockSpec((d, LANES), fixed), pl.BlockSpec((d, LANES), fixed)],
        out_specs=[pl.BlockSpec((tm, d), row), pl.BlockSpec((tm, LANES), row),
                   pl.BlockSpec((tm, LANES), row)],
        out_shape=[jax.ShapeDtypeStruct((t, d), F32),
                   jax.ShapeDtypeStruct((t, LANES), jnp.int32),
                   jax.ShapeDtypeStruct((t, LANES), F32)],
        compiler_params=_params("parallel"),
        name="moe_route",
    )(x, g.reshape(1, d), r_hi, r_lo)


def _experts_body(te_ref, nu_ref, src_ref, h_hbm, wg_ref, wu_ref, wd_ref, o_ref,
                  xbuf_ref, xs_ref, acc_ref, sem, *, rows_per_step):
    i = pl.program_id(0)
    f = pl.program_id(1)
    tm = xs_ref.shape[0]
    n_rows = xbuf_ref.shape[1]
    n_used = nu_ref[0]
    used = i < n_used
    has_next = i + 1 < n_used
    slot = i % 2

    def start_row(tile, slot_, r):
        tok = src_ref[tile * tm + r]
        pltpu.make_async_copy(h_hbm.at[pl.ds(tok, 1), :], xbuf_ref.at[slot_, pl.ds(r, 1), :],
                              sem.at[slot_]).start()

    @pl.when(f == 0)
    def _():
        acc_ref[...] = jnp.zeros_like(acc_ref)

        @pl.when(used)
        def _():
            @pl.when(i == 0)
            def _():
                def row(r, carry):
                    start_row(0, 0, r)
                    return carry

                lax.fori_loop(0, n_rows, row, 0, unroll=8)

            pltpu.make_async_copy(h_hbm.at[pl.ds(0, n_rows), :], xbuf_ref.at[slot], sem.at[slot]).wait()
            xs_ref[...] = xbuf_ref[slot, 0:tm, :].astype(BF16)

    def ffn_step():
        h = xs_ref[...]
        a = jnp.dot(h, wg_ref[0].astype(BF16), preferred_element_type=F32)
        u = jnp.dot(h, wu_ref[0].astype(BF16), preferred_element_type=F32)
        act = (a * jax.nn.sigmoid(a) * u).astype(BF16)
        acc_ref[...] += jnp.dot(act, wd_ref[0].astype(BF16), preferred_element_type=F32)

    @pl.when(used & has_next)
    def _():
        for j in range(rows_per_step):
            start_row(i + 1, 1 - slot, f * rows_per_step + j)
        ffn_step()

    @pl.when(used & jnp.logical_not(has_next))
    def _():
        ffn_step()

    @pl.when(f == pl.num_programs(1) - 1)
    def _():
        o_ref[...] = acc_ref[...].astype(o_ref.dtype)


def moe_experts(tile_expert, n_used, src, h, wg, wu, wd, tm, tf=512):
    d = h.shape[1]
    nf = wg.shape[2] // tf
    rows_per_step = 8 * (-(-tm // (8 * nf)))
    n_rows = rows_per_step * nf
    n_tiles = src.shape[0] // tm
    src = jnp.pad(src, (0, n_rows - tm))
    grid_spec = pltpu.PrefetchScalarGridSpec(
        num_scalar_prefetch=3,
        grid=(n_tiles, nf),
        in_specs=[pl.BlockSpec(memory_space=pl.ANY),
                  pl.BlockSpec((1, d, tf), lambda i, f, te, *_: (te[i], 0, f)),
                  pl.BlockSpec((1, d, tf), lambda i, f, te, *_: (te[i], 0, f)),
                  pl.BlockSpec((1, tf, d), lambda i, f, te, *_: (te[i], f, 0))],
        out_specs=pl.BlockSpec((tm, d), lambda i, f, *_: (i, 0)),
        scratch_shapes=[pltpu.VMEM((2, n_rows, d), F32), pltpu.VMEM((tm, d), BF16),
                        pltpu.VMEM((tm, d), F32), pltpu.SemaphoreType.DMA((2,))],
    )
    return pl.pallas_call(
        functools.partial(_experts_body, rows_per_step=rows_per_step),
        grid_spec=grid_spec,
        out_shape=jax.ShapeDtypeStruct((n_tiles * tm, d), BF16),
        compiler_params=_params("arbitrary", "arbitrary"),
        name="moe_experts",
    )(tile_expert, n_used, src, h, wg, wu, wd)


def _combine_norm_body(x_ref, ya_ref, yb_ref, wt_ref, g_ref, o_ref):
    wt = wt_ref[...]
    y = wt[:, 0:1] * ya_ref[...].astype(F32) + wt[:, 1:2] * yb_ref[...].astype(F32)
    o_ref[...] = _rms(x_ref[...] + y, g_ref[...])


def combine_norm(x, y_pairs, wts, g, tm=1024):
    t, d = x.shape
    nt = t // tm
    row = lambda i: (i, 0)
    return pl.pallas_call(
        _combine_norm_body,
        grid=(nt,),
        in_specs=[pl.BlockSpec((tm, d), row), pl.BlockSpec((tm, d), row),
                  pl.BlockSpec((tm, d), lambda i: (i + nt, 0)), pl.BlockSpec((tm, LANES), row),
                  pl.BlockSpec((1, d), lambda i: (0, 0))],
        out_specs=pl.BlockSpec((tm, d), row),
        out_shape=jax.ShapeDtypeStruct((t, d), F32),
        compiler_params=_params("parallel"),
        name="combine_norm",
    )(x, y_pairs, y_pairs, wts, g.reshape(1, d))


def moe_layer(x, g_norm, router, wg, wu, wd, g_final, tm=1024):
    t, d = x.shape
    r = jnp.zeros((d, LANES), F32).at[:, :N_EXPERTS].set(router)
    r_hi = r.astype(BF16)
    r_lo = (r - r_hi.astype(F32)).astype(BF16)
    h, idx, wts = moe_route(x, g_norm, r_hi, r_lo)
    e_flat = jnp.concatenate([idx[:, 0], idx[:, 1]])
    onehot = (e_flat[:, None] == jnp.arange(N_EXPERTS)[None, :]).astype(jnp.int32)
    csum = jnp.cumsum(onehot, axis=0)
    rank = jnp.sum((csum - onehot) * onehot, axis=1)
    counts = csum[-1]
    padded = ((counts + tm - 1) // tm) * tm
    ends = jnp.cumsum(padded)
    starts = ends - padded
    dest = starts[e_flat] + rank
    n_tiles = (TOP_K * t) // tm + N_EXPERTS
    p = n_tiles * tm
    tok = jnp.concatenate([jnp.arange(t, dtype=jnp.int32)] * TOP_K)
    src = jnp.zeros((p,), jnp.int32).at[dest].set(tok)
    tile_start = jnp.arange(n_tiles, dtype=jnp.int32) * tm
    tile_expert = jnp.minimum(jnp.sum((ends[None, :] <= tile_start[:, None]).astype(jnp.int32), axis=1),
                              N_EXPERTS - 1)
    n_used = (ends[-1] // tm).astype(jnp.int32).reshape(1)
    ys = moe_experts(tile_expert, n_used, src, h, wg, wu, wd, tm)
    y_pairs = jnp.take(ys, dest, axis=0, mode="clip")
    return combine_norm(x, y_pairs, wts, g_final)


SSM_TC = 512
SSM_SUB = 128
SSM_HALF = 256
SSM_NSTATE = SSM_GROUPS * SSM_STATE


def _ssm_body(u_ref, bw_ref, cw_ref, d_ref, glu_ref, ar_ref, ai_ref, pr_ref, pi_ref, o_ref,
              xr_ref, xi_ref, cr_ref, ci_ref):
    @pl.when(pl.program_id(1) == 0)
    def _():
        cr_ref[...] = jnp.zeros_like(cr_ref)
        ci_ref[...] = jnp.zeros_like(ci_ref)

    ub = u_ref[...]
    u = ub.astype(F32)
    nblk = SSM_WIDTH // SSM_HALF
    sblk = SSM_NSTATE // nblk
    for k in range(nblk):
        bu = jnp.dot(ub[:, k * SSM_HALF:(k + 1) * SSM_HALF], bw_ref[k], preferred_element_type=F32)
        xr_ref[:, k * sblk:(k + 1) * sblk] = bu[:, :sblk]
        xi_ref[:, k * sblk:(k + 1) * sblk] = bu[:, sblk:]

    row8 = lax.broadcasted_iota(jnp.int32, (SSM_SUB, LANES), 0) % 8
    n_grp = SSM_SUB // 8

    def strip(c, carry):
        col = pl.ds(pl.multiple_of(c * LANES, LANES), LANES)
        c_r = cr_ref[:, col]
        c_i = ci_ref[:, col]
        p_r = pr_ref[0:8, col]
        p_i = pi_ref[0:8, col]
        step_mul = [(jnp.where(row8 >= (1 << i), ar_ref[i:i + 1, col], 0.0),
                     jnp.where(row8 >= (1 << i), ai_ref[i:i + 1, col], 0.0)) for i in range(3)]
        for sub in range(SSM_TC // SSM_SUB):
            rows = slice(sub * SSM_SUB, (sub + 1) * SSM_SUB)
            xr = xr_ref[rows, col]
            xi = xi_ref[rows, col]
            for i in range(3):
                a_r, a_i = step_mul[i]
                sr, si = pltpu.roll(xr, 1 << i, 0), pltpu.roll(xi, 1 << i, 0)
                xr, xi = xr + a_r * sr - a_i * si, xi + a_r * si + a_i * sr
            out_r, out_i = [], []
            for r in range(n_grp):
                g_r = xr[8 * r:8 * r + 8] + p_r * c_r - p_i * c_i
                g_i = xi[8 * r:8 * r + 8] + p_r * c_i + p_i * c_r
                c_r, c_i = g_r[7:8, :], g_i[7:8, :]
                out_r.append(g_r)
                out_i.append(g_i)
            xr_ref[rows, col] = jnp.concatenate(out_r, axis=0)
            xi_ref[rows, col] = jnp.concatenate(out_i, axis=0)
        cr_ref[:, col] = c_r
        ci_ref[:, col] = c_i
        return carry

    lax.fori_loop(0, SSM_NSTATE // LANES, strip, 0)

    ys = []
    for k in range(nblk):
        st = jnp.concatenate([xr_ref[:, k * sblk:(k + 1) * sblk].astype(BF16),
                              xi_ref[:, k * sblk:(k + 1) * sblk].astype(BF16)], axis=1)
        ys.append(jnp.dot(st, cw_ref[k], preferred_element_type=F32))
    y = jax.nn.gelu(jnp.concatenate(ys, axis=1) + d_ref[...] * u)
    gated = y * jax.nn.sigmoid(jnp.dot(y.astype(BF16), glu_ref[...], preferred_element_type=F32))
    o_ref[...] = gated.astype(o_ref.dtype)


def ssm_mixer(proj, bs, a_re, a_im, log_dt, b_re, b_im, c_re, c_im, d_skip, glu_w):
    t = proj.shape[0]
    s = t // bs
    nt = s // SSM_TC
    assert s % SSM_TC == 0 and OFF_SSM % SSM_WIDTH == 0
    nblk = SSM_WIDTH // SSM_HALF
    gpb = SSM_GROUPS // nblk
    dt = jnp.exp(log_dt)[:, None]
    mag = jnp.exp(a_re * dt)
    abar_r, abar_i = mag * jnp.cos(a_im * dt), mag * jnp.sin(a_im * dt)
    nr, ni = abar_r - 1.0, abar_i
    den = a_re * a_re + a_im * a_im
    sr, si = (nr * a_re + ni * a_im) / den, (ni * a_re - nr * a_im) / den
    bbar_r = sr[..., None] * b_re - si[..., None] * b_im
    bbar_i = sr[..., None] * b_im + si[..., None] * b_re

    def powers(k):
        kk = k.astype(F32)[:, None, None]
        m = jnp.exp(kk * (a_re * dt))
        return ((m * jnp.cos(kk * (a_im * dt))).reshape(-1, SSM_NSTATE),
                (m * jnp.sin(kk * (a_im * dt))).reshape(-1, SSM_NSTATE))

    ar, ai = powers(2 ** jnp.arange(8))
    pr, pi = powers(jnp.arange(1, SSM_SUB + 1))
    eye = jnp.eye(gpb, dtype=F32)

    def in_block(w):
        return jnp.einsum('gnp,gh->gphn', w, eye).reshape(gpb * SSM_GROUP, gpb * SSM_STATE)

    def out_block(w):
        return jnp.einsum('gpn,gh->gnhp', w, eye).reshape(gpb * SSM_STATE, gpb * SSM_GROUP)

    bw = jnp.stack([jnp.concatenate([in_block(bbar_r[k * gpb:(k + 1) * gpb]),
                                     in_block(bbar_i[k * gpb:(k + 1) * gpb])], axis=1)
                    for k in range(nblk)]).astype(BF16)
    cw = jnp.stack([jnp.concatenate([out_block(c_re[k * gpb:(k + 1) * gpb]),
                                     -out_block(c_im[k * gpb:(k + 1) * gpb])], axis=0)
                    for k in range(nblk)]).astype(BF16)
    fixed2 = lambda b, i: (0, 0)
    fixed3 = lambda b, i: (0, 0, 0)
    return pl.pallas_call(
        _ssm_body,
        grid=(bs, nt),
        in_specs=[pl.BlockSpec((SSM_TC, SSM_WIDTH), lambda b, i: (b * nt + i, OFF_SSM // SSM_WIDTH)),
                  pl.BlockSpec(bw.shape, fixed3), pl.BlockSpec(cw.shape, fixed3),
                  pl.BlockSpec((1, SSM_WIDTH), fixed2), pl.BlockSpec((SSM_WIDTH, SSM_WIDTH), fixed2),
                  pl.BlockSpec((8, SSM_NSTATE), fixed2), pl.BlockSpec((8, SSM_NSTATE), fixed2),
                  pl.BlockSpec((SSM_SUB, SSM_NSTATE), fixed2), pl.BlockSpec((SSM_SUB, SSM_NSTATE), fixed2)],
        out_specs=pl.BlockSpec((SSM_TC, SSM_WIDTH), lambda b, i: (b * nt + i, 0)),
        out_shape=jax.ShapeDtypeStruct((t, SSM_WIDTH), BF16),
        scratch_shapes=[pltpu.VMEM((SSM_TC, SSM_NSTATE), F32), pltpu.VMEM((SSM_TC, SSM_NSTATE), F32),
                        pltpu.VMEM((1, SSM_NSTATE), F32), pltpu.VMEM((1, SSM_NSTATE), F32)],
        compiler_params=_params("parallel", "arbitrary"),
        name="ssm_scan",
    )(proj, bw, cw, d_skip.reshape(1, SSM_WIDTH), glu_w.astype(BF16), ar, ai, pr, pi)


GDN_TC = 256
GDN_HALO = 8


def _gdn_body(nega_ref, dtb_ref, q_ref, k_ref, v_ref, z_ref, sm_ref, wq_ref, wk_ref, wv_ref, ng_ref,
              o_ref, state_ref, hq_ref, hk_ref, hv_ref, vnew_ref):
    TC, CH, DK, H = GDN_TC, GDN_CHUNK, GDN_DK, GDN_HEADS
    heads = range(H)

    @pl.when(pl.program_id(1) == 0)
    def _():
        state_ref[...] = jnp.zeros_like(state_ref)
        hq_ref[...] = jnp.zeros_like(hq_ref)
        hk_ref[...] = jnp.zeros_like(hk_ref)
        hv_ref[...] = jnp.zeros_like(hv_ref)

    def conv_silu(x_ref, halo_ref, w_ref):
        x = x_ref[...].astype(F32)
        xx = jnp.concatenate([halo_ref[...], x], axis=0)
        w = w_ref[...]
        y = w[GDN_CONV - 1:GDN_CONV] * x
        for d in range(1, GDN_CONV):
            y = y + w[GDN_CONV - 1 - d:GDN_CONV - d] * pltpu.roll(xx, d, 0)[GDN_HALO:GDN_HALO + TC]
        halo_ref[...] = x[TC - GDN_HALO:TC]
        return y * jax.nn.sigmoid(y)

    def per_head(x):
        return [x[:, h * LANES:(h + 1) * LANES] for h in heads]

    q = per_head(conv_silu(q_ref, hq_ref, wq_ref))
    k = per_head(conv_silu(k_ref, hk_ref, wk_ref))
    v = per_head(conv_silu(v_ref, hv_ref, wv_ref))
    q = [x * lax.rsqrt(jnp.sum(x * x, axis=-1, keepdims=True) + 1e-6) * (DK ** -0.5) for x in q]
    k = [x * lax.rsqrt(jnp.sum(x * x, axis=-1, keepdims=True) + 1e-6) for x in k]

    small = sm_ref[...].astype(F32)
    beta = [jax.nn.sigmoid(small[:, H + h:H + h + 1]) for h in heads]
    la = []
    for h in heads:
        xa = small[:, h:h + 1] + dtb_ref[h]
        softplus = jnp.maximum(xa, 0.0) + jnp.log(1.0 + jnp.exp(-jnp.abs(xa)))
        la.append(jnp.broadcast_to(nega_ref[h] * softplus, (TC, LANES)))

    ri = lax.broadcasted_iota(jnp.int32, (TC, TC), 0)
    ci = lax.broadcasted_iota(jnp.int32, (TC, TC), 1)
    same = (ri // CH) == (ci // CH)
    causal = jnp.where(same, jnp.where(ci <= ri, 1.0, 0.0), 0.0)
    strict = jnp.where(ci < ri, causal, 0.0)
    eye = jnp.where(ri == ci, 1.0, 0.0)
    tri = causal.astype(BF16)
    lane = lax.broadcasted_iota(jnp.int32, (TC, LANES), 1)

    g = []
    for h in heads:
        la_hi, la_lo = _split_bf16(la[h])
        g.append(jnp.dot(tri, la_hi, preferred_element_type=F32)
                 + jnp.dot(tri, la_lo, preferred_element_type=F32))
    decay = []
    for h in heads:
        g_hi = g[h].astype(BF16).astype(F32)
        g_lo = g[h] - g_hi
        lhs = jnp.where(lane == 0, g_hi, jnp.where(lane == 1, g_lo, jnp.where(lane < 4, 1.0, 0.0)))
        rhs = jnp.where(lane < 2, 1.0, jnp.where(lane == 2, -g_hi, jnp.where(lane == 3, -g_lo, 0.0)))
        decay.append(jnp.exp(jnp.where(causal > 0.5, _dot_nt(lhs.astype(BF16), rhs.astype(BF16)), MASKED)))

    kb = [k[h] * beta[h] for h in heads]
    k_b = [x.astype(BF16) for x in k]
    lmat = [strict * _dot_nt(kb[h].astype(BF16), k_b[h]) * decay[h] for h in heads]
    tinv = [eye - x for x in lmat]
    pw = lmat
    for _ in range(CH.bit_length() - 2):
        pw = [jnp.dot(x.astype(BF16), x.astype(BF16), preferred_element_type=F32) for x in pw]
        tinv = [tinv[h] + jnp.dot(tinv[h].astype(BF16), pw[h].astype(BF16), preferred_element_type=F32)
                for h in heads]
    eg = [jnp.exp(x) for x in g]
    sol = [jnp.dot(tinv[h].astype(BF16),
                   jnp.concatenate([v[h] * beta[h], kb[h] * eg[h]], axis=1).astype(BF16),
                   preferred_element_type=F32) for h in heads]
    attn = [(causal * _dot_nt(q[h].astype(BF16), k_b[h]) * decay[h]).astype(BF16) for h in heads]
    q_st = [(q[h] * eg[h]).astype(BF16) for h in heads]

    vnew_ref[...] = jnp.zeros_like(vnew_ref)
    for c in range(TC // CH):
        rows = slice(c * CH, (c + 1) * CH)
        g_last = [x[(c + 1) * CH - 1:(c + 1) * CH, :] for x in g]
        st = [state_ref[h] for h in heads]
        st_b = [x.astype(BF16) for x in st]
        v_new = [sol[h][rows, :GDN_DV]
                 - jnp.dot(sol[h][rows, GDN_DV:].astype(BF16), st_b[h], preferred_element_type=F32)
                 for h in heads]
        for h in heads:
            vnew_ref[h, rows, :] = v_new[h].astype(BF16)
        o_c = [jnp.dot(q_st[h][rows], st_b[h], preferred_element_type=F32)
               + jnp.dot(attn[h][rows], vnew_ref[h], preferred_element_type=F32) for h in heads]
        for h in heads:
            k_st = (k[h][rows] * jnp.exp(g_last[h] - g[h][rows])).astype(BF16)
            state_ref[h] = st[h] * jnp.exp(g_last[h][:, :1]) + _dot_tn(k_st, v_new[h].astype(BF16))
        for h in heads:
            o = o_c[h] * lax.rsqrt(jnp.mean(o_c[h] * o_c[h], axis=-1, keepdims=True) + RMS_EPS) * ng_ref[...]
            zc = z_ref[rows, h * LANES:(h + 1) * LANES].astype(F32)
            o_ref[rows, h * LANES:(h + 1) * LANES] = (o * (zc * jax.nn.sigmoid(zc))).astype(o_ref.dtype)


def gdn_mixer(proj, bs, conv_w, a_log, dt_bias, norm_g):
    t = proj.shape[0]
    s = t // bs
    nt = s // GDN_TC
    H = GDN_HEADS
    hw = H * LANES
    assert s % GDN_TC == 0 and GDN_DK == LANES and GDN_DV == LANES and OFF_QKV == 0 and OFF_Z % hw == 0
    cw = jnp.zeros((8, GDN_QKV), F32).at[:GDN_CONV].set(conv_w)
    tok = lambda blk: pl.BlockSpec((GDN_TC, hw), lambda b, i: (b * nt + i, blk))
    wspec = lambda blk: pl.BlockSpec((8, hw), lambda b, i: (0, blk))
    smem = pl.BlockSpec(memory_space=pltpu.SMEM)
    return pl.pallas_call(
        _gdn_body,
        grid=(bs, nt),
        in_specs=[smem, smem, tok(0), tok(1), tok(2), tok(OFF_Z // hw),
                  pl.BlockSpec((GDN_TC, 2 * LANES), lambda b, i: (b * nt + i, OFF_SMALL // (2 * LANES))),
                  wspec(0), wspec(1), wspec(2), pl.BlockSpec((1, GDN_DV), lambda b, i: (0, 0))],
        out_specs=pl.BlockSpec((GDN_TC, hw), lambda b, i: (b * nt + i, 0)),
        out_shape=jax.ShapeDtypeStruct((t, hw), BF16),
        scratch_shapes=[pltpu.VMEM((H, GDN_DK, GDN_DV), F32)] + [pltpu.VMEM((GDN_HALO, hw), F32)] * 3
        + [pltpu.VMEM((H, GDN_TC, GDN_DV), BF16)],
        compiler_params=_params("parallel", "arbitrary"),
        name="gdn_delta",
    )(-jnp.exp(a_log), dt_bias.astype(F32), proj, proj, proj, proj, proj, cw, cw, cw,
      norm_g.reshape(1, GDN_DV))


NSA_KT = 256
NSA_NEG = MASKED
NSA_WTILES = NSA_WINDOW // LANES + 1


def _bucket_table():
    n = np.arange(LANES)
    max_exact = REL_BUCKETS // 2
    nf = np.maximum(n, 1).astype(np.float32)
    large = max_exact + (np.log(nf / np.float32(max_exact)) / np.float32(math.log(REL_MAX_DIST / max_exact))
                         * np.float32(REL_BUCKETS - max_exact)).astype(np.int32)
    tbl = np.where(n < max_exact, n, np.minimum(large, REL_BUCKETS - 1))
    assert tbl[-1] == REL_BUCKETS - 1 and REL_MAX_DIST <= LANES
    return tbl


def _compress_body(x_ref, pe_ref, w1_ref, w2_ref, o_ref):
    x = x_ref[0, 0, 0]
    w1 = w1_ref[0]
    half = NSA_CMP_STRIDE * NSA_DH
    nc = x.shape[0]
    a = jnp.dot(x, w1[:half], preferred_element_type=F32)
    b = jnp.dot(x, w1[half:], preferred_element_type=F32)
    pe_h = jnp.dot(pe_ref[0], w1, preferred_element_type=F32)
    hid = a + pltpu.roll(b, nc - 1, 0) + pe_h[0:1, :]
    o_ref[0, 0, 0] = jnp.dot(jax.nn.gelu(hid).astype(BF16), w2_ref[0], preferred_element_type=F32)


def nsa_compress(x, pe, w1, w2):
    _, bs, g, nc, width = x.shape
    return pl.pallas_call(
        _compress_body,
        grid=(2, bs, g),
        in_specs=[pl.BlockSpec((1, 1, 1, nc, width), lambda i, b, j: (i, b, j, 0, 0)),
                  pl.BlockSpec((1,) + pe.shape[1:], lambda i, b, j: (i, 0, 0)),
                  pl.BlockSpec((1,) + w1.shape[1:], lambda i, b, j: (i, 0, 0)),
                  pl.BlockSpec((1,) + w2.shape[1:], lambda i, b, j: (i, 0, 0))],
        out_specs=pl.BlockSpec((1, 1, 1, nc, NSA_DH), lambda i, b, j: (i, b, j, 0, 0)),
        out_shape=jax.ShapeDtypeStruct((2, bs, g, nc, NSA_DH), F32),
        compiler_params=_params("parallel", "parallel", "parallel"),
        name="nsa_compress",
    )(x, pe, w1, w2)


def _nsa_body(nfast, std, near_w, near_c, q_ref, ex_ref, gl_ref, qp_ref, kp_ref, cp_ref,
              ks_ref, vs_ref, kw_ref, vw_ref, kc_ref, vc_ref, agg_ref, td_ref, o_ref,
              m_ref, l_ref, acc_ref, sa_ref, sb_ref, sc_ref, *, n_sel, sel_k):
    QB, HG, G = NSA_QBLOCK, NSA_HPG, NSA_KV_GROUPS
    groups = range(G)
    qi = pl.program_id(1)
    s0 = qi * QB
    nc = kc_ref.shape[2]
    n_ct = nc // LANES
    q_all = q_ref[...].astype(F32) * (NSA_DH ** -0.5 * math.log2(math.e))
    q = []
    for g in groups:
        rows = jnp.concatenate([q_all[:, (g * HG + hg) * NSA_DH:(g * HG + hg + 1) * NSA_DH]
                                for hg in range(HG)], axis=0)
        q.append(jnp.concatenate([rows, ex_ref[g]], axis=1).astype(BF16))
    qp = qp_ref[...]

    def lanes4(x):
        return jnp.concatenate([x] * HG, axis=1)

    def delta_t(kp):
        idx = jnp.clip(_dot_nt(kp, qp), 0.0, float(LANES - 1)).astype(jnp.int32)
        outs = []
        for g in groups:
            heads = []
            for hg in range(HG):
                tb = jnp.broadcast_to(td_ref[g * HG + hg:g * HG + hg + 1, :], idx.shape)
                heads.append(jnp.take_along_axis(tb, idx, axis=1))
            outs.append(jnp.concatenate(heads, axis=1))
        return outs

    def maybe_delta(flag, kp):
        zeros = jnp.zeros((kp.shape[0], HG * QB), F32)
        return lax.cond(flag != 0, lambda: tuple(delta_t(kp)), lambda: (zeros,) * G)

    gate = [jax.nn.sigmoid(gl_ref[0, g, 0]) for g in groups]

    for g in groups:
        sc_ref[g] = _dot_nt(kc_ref[0, g], q[g])
    near_rows = 4 * 8

    @pl.when(std[qi] != 0)
    def _():
        c0 = pl.multiple_of(jnp.minimum(qi * (QB // NSA_CMP_STRIDE) - near_rows // 2, nc - near_rows), 8)
        d = delta_t(cp_ref[pl.ds(c0, near_rows), :])
        for g in groups:
            sc_ref[g, pl.ds(c0, near_rows), :] += d[g]

    @pl.when(std[qi] == 0)
    def _():
        for ct in range(n_ct):
            @pl.when(near_c[qi * n_ct + ct] != 0)
            def _():
                d = delta_t(cp_ref[ct * LANES:(ct + 1) * LANES, :])
                for g in groups:
                    sc_ref[g, ct * LANES:(ct + 1) * LANES, :] += d[g]

    c_id = lax.broadcasted_iota(jnp.int32, (nc, QB), 0)
    c_q = lax.broadcasted_iota(jnp.int32, (nc, QB), 1)
    ok_c = lanes4(jnp.where(c_id * NSA_CMP_STRIDE + (NSA_CMP_LEN - 1) <= s0 + c_q, 1.0, 0.0))
    blk = lax.broadcasted_iota(jnp.int32, (LANES, QB), 0)
    t_tok = s0 + lax.broadcasted_iota(jnp.int32, (LANES, QB), 1)
    tb_blk = t_tok // NSA_SEL_LEN
    forced = jnp.where(blk == 0, 1.0, 0.0) + jnp.where(blk == tb_blk, 1.0, 0.0) \
        + jnp.where(blk == tb_blk - 1, 1.0, 0.0)
    blkf = blk.astype(F32)
    q_tok = s0 + lax.broadcasted_iota(jnp.int32, (1, QB), 1)
    any_ok = lanes4(jnp.where(q_tok >= NSA_CMP_LEN - 1, 1.0, 0.0))
    out, score = [], []
    for g in groups:
        s = jnp.where(ok_c > 0.5, sc_ref[g], NSA_NEG)
        e = jnp.exp2(s - jnp.max(s, axis=0, keepdims=True))
        p = e * (any_ok / jnp.sum(e, axis=0, keepdims=True))
        out.append(gate[g][0:1, :] * jnp.dot(vc_ref[0, g], p.astype(BF16), preferred_element_type=F32))
        ps = p[:, 0:QB]
        for hg in range(1, HG):
            ps = ps + p[:, hg * QB:(hg + 1) * QB]
        ps_hi, ps_lo = _split_bf16(ps)
        imp = (jnp.dot(agg_ref[...], ps_hi, preferred_element_type=F32)
               + jnp.dot(agg_ref[...], ps_lo, preferred_element_type=F32))
        sco = jnp.where(forced > 0.5, 1e9, jnp.where(blk * NSA_SEL_LEN <= t_tok, imp, -1e9))
        score.append(jnp.where(blk < n_sel, sco, -jnp.inf))
    for _ in range(sel_k):
        for g in groups:
            mx = jnp.max(score[g], axis=0, keepdims=True)
            first = jnp.min(jnp.where(score[g] == mx, blkf, float(LANES)), axis=0, keepdims=True)
            score[g] = jnp.where(blkf == first, -jnp.inf, score[g])
    q2 = []
    for g in groups:
        picked = jnp.where(score[g] == -jnp.inf, 0.0, NSA_NEG)
        if n_sel < LANES:
            picked = jnp.where(blk < n_sel, picked, NSA_NEG)
        sel_q = jnp.transpose(picked)
        q2.append(jnp.concatenate([q[g], jnp.concatenate([sel_q] * HG, axis=0).astype(BF16)], axis=1))

    SEL, WIN = 0, 1

    def reset():
        m_ref[...] = jnp.full_like(m_ref, NSA_NEG)
        l_ref[...] = jnp.zeros_like(l_ref)
        acc_ref[...] = jnp.zeros_like(acc_ref)

    def tile_step(br, g, k, s, v_t):
        m_old = m_ref[br, g, k]
        m_new = jnp.maximum(m_old, jnp.max(s, axis=0, keepdims=True))
        alpha = jnp.exp2(m_old - m_new)
        p = jnp.exp2(s - m_new)
        l_ref[br, g, k] = l_ref[br, g, k] * alpha + jnp.sum(p, axis=0, keepdims=True)
        acc_ref[br, g, k] = (acc_ref[br, g, k] * alpha
                             + jnp.dot(v_t, p.astype(BF16), preferred_element_type=F32))
        m_ref[br, g, k] = m_new

    def finish(br, g):
        m = jnp.maximum(m_ref[br, g, 0], m_ref[br, g, 1])
        w0, w1 = jnp.exp2(m_ref[br, g, 0] - m), jnp.exp2(m_ref[br, g, 1] - m)
        return ((acc_ref[br, g, 0] * w0 + acc_ref[br, g, 1] * w1)
                * (1.0 / (l_ref[br, g, 0] * w0 + l_ref[br, g, 1] * w1)))

    def emit():
        heads_out = []
        for g in groups:
            o_t = out[g] + gate[g][1:2, :] * finish(SEL, g) + gate[g][2:3, :] * finish(WIN, g)
            heads_out += [jnp.transpose(o_t[:, hg * QB:(hg + 1) * QB]) for hg in range(HG)]
        o_ref[0] = jnp.concatenate(heads_out, axis=1).astype(o_ref.dtype)

    key_r = lax.broadcasted_iota(jnp.int32, (NSA_KT, QB), 0)
    key_q = lax.broadcasted_iota(jnp.int32, (NSA_KT, QB), 1)
    w_r = lax.broadcasted_iota(jnp.int32, (LANES, QB), 0)
    w_q = lax.broadcasted_iota(jnp.int32, (LANES, QB), 1)
    in_window = lanes4(jnp.where(w_r > w_q, 0.0, NSA_NEG))
    causal_w = lanes4(jnp.where(w_r <= w_q, 0.0, NSA_NEG))

    def sel_at(k0, n):
        k0 = pl.multiple_of(k0, LANES)
        return [(_dot_nt(ks_ref[0, g, pl.ds(k0, n), :], q2[g]), vs_ref[0, g, :, pl.ds(k0, n)])
                for g in groups]

    def win_at(k0, n):
        k0 = pl.multiple_of(k0, LANES)
        return [(_dot_nt(kw_ref[0, g, pl.ds(k0, n), :], q[g]), vw_ref[0, g, :, pl.ds(k0, n)])
                for g in groups]

    def delta_at(k0, n):
        return delta_t(kp_ref[pl.ds(pl.multiple_of(k0, LANES), n), :])

    def pair_scores(p, buf):
        for t in range(2):
            sc_t = sel_at((2 * p + t) * NSA_KT, NSA_KT)
            for g in groups:
                buf[g, t] = sc_t[g][0]

    def pair_softmax(p, buf):
        for t in range(2):
            k0 = pl.multiple_of((2 * p + t) * NSA_KT, NSA_KT)
            for g in groups:
                tile_step(SEL, g, t, buf[g, t], vs_ref[0, g, :, pl.ds(k0, NSA_KT)])

    def run_pairs(n_pairs):
        last = jnp.maximum(n_pairs - 1, 0)
        pair_scores(0, sa_ref)

        def two_pairs(j, carry):
            pair_scores(2 * j + 1, sb_ref)
            pair_softmax(2 * j, sa_ref)
            pair_scores(jnp.minimum(2 * j + 2, last), sa_ref)
            pair_softmax(2 * j + 1, sb_ref)
            return carry

        lax.fori_loop(0, n_pairs // 2, two_pairs, 0)

        @pl.when(n_pairs % 2 == 1)
        def _():
            pair_softmax(n_pairs - 1, sa_ref)

    @pl.when(std[qi] != 0)
    def _():
        reset()
        nb = (qi - 1) // 2
        run_pairs(nb // 2)

        @pl.when(nb % 2 == 1)
        def _():
            a = sel_at((nb - 1) * NSA_KT, NSA_KT)
            for g in groups:
                tile_step(SEL, g, 0, *a[g])

        @pl.when(qi % 2 == 0)
        def _():
            a = sel_at(nb * NSA_KT, LANES)
            for g in groups:
                tile_step(SEL, g, 1, *a[g])

        zeros = jnp.zeros((LANES, HG * QB), F32)
        sel_n = sel_at(s0 - LANES, 2 * LANES)
        win_p = win_at(s0 - NSA_WINDOW, NSA_WINDOW)
        win_d = win_at(s0, LANES)
        d_prev, d_diag = delta_at(s0 - LANES, LANES), delta_at(s0, LANES)
        for g in groups:
            bias = jnp.concatenate([d_prev[g], d_diag[g] + causal_w], axis=0)
            tile_step(SEL, g, 0, sel_n[g][0] + bias, sel_n[g][1])
        for g in groups:
            bias = jnp.concatenate([in_window] + [zeros] * (NSA_WTILES - 3) + [d_prev[g]], axis=0)
            tile_step(WIN, g, 0, win_p[g][0] + bias, win_p[g][1])
        for g in groups:
            tile_step(WIN, g, 1, win_d[g][0] + d_diag[g] + causal_w, win_d[g][1])
        emit()

    @pl.when(std[qi] == 0)
    def _():
        reset()
        n_past = (s0 + QB - 1) // NSA_KT
        n_pairs = nfast[qi] // 2
        run_pairs(n_pairs)

        def slow_body(kt, carry):
            a = sel_at(kt * NSA_KT, NSA_KT)
            d = delta_at(kt * NSA_KT, NSA_KT)
            for g in groups:
                tile_step(SEL, g, 1, a[g][0] + d[g], a[g][1])
            return carry

        lax.fori_loop(2 * n_pairs, n_past, slow_body, 0)
        k0 = n_past * NSA_KT
        a = sel_at(k0, NSA_KT)
        d = delta_at(k0, NSA_KT)
        causal = lanes4(jnp.where(k0 + key_r <= s0 + key_q, 0.0, NSA_NEG))
        for g in groups:
            tile_step(SEL, g, 0, a[g][0] + d[g] + causal, a[g][1])

        for j in range(NSA_WTILES):
            start = s0 - NSA_WINDOW + j * LANES

            @pl.when(start >= 0)
            def _():
                sw = win_at(start, LANES)
                d = maybe_delta(near_w[qi * NSA_WTILES + j],
                                kp_ref[pl.ds(pl.multiple_of(start, LANES), LANES), :])
                for g in groups:
                    s = sw[g][0] + d[g]
                    if j == 0:
                        s = s + in_window
                    elif j == NSA_WTILES - 1:
                        s = s + causal_w
                    tile_step(WIN, g, j % 2, s, sw[g][1])

        emit()


def _pos_digits(p, key_side):
    d0, d1, d2 = (p & 127).astype(F32), ((p >> 7) & 127).astype(F32), (p >> 14).astype(F32)
    one = jnp.ones_like(d0)
    if key_side:
        cols = [one, one, one, -d2, -d1, -d0]
    else:
        cols = [16384.0 * d2, 128.0 * d1, d0, 16384.0 * one, 128.0 * one, one]
    out = jnp.stack(cols, axis=-1)
    return jnp.pad(out, ((0, 0), (0, LANES - out.shape[-1]))).astype(BF16)


def nsa_mixer(proj, k_c, v_c, k_s, v_s, k_w, v_w, gate_logits, positions,
              ck_pe, ck_w1, ck_w2, cv_pe, cv_w1, cv_w2, rel_bias):
    k_c, v_c, k_s, v_s, k_w, v_w = lax.optimization_barrier((k_c, v_c, k_s, v_s, k_w, v_w))
    bs, s, _ = k_c.shape
    G, HG, DH, QB = NSA_KV_GROUPS, NSA_HPG, NSA_DH, NSA_QBLOCK
    nqt = s // QB
    nc = s // NSA_CMP_STRIDE
    n_cmp = (s - NSA_CMP_LEN) // NSA_CMP_STRIDE + 1
    n_sel = s // NSA_SEL_LEN
    sel_k = min(NSA_SEL_TOPK, n_sel)
    n_kt = s // NSA_KT
    n_ct = nc // LANES
    assert s % (LANES * NSA_CMP_STRIDE) == 0 and n_sel <= LANES and NSA_KT == 2 * QB

    def by_group(t):
        return t.reshape(bs, s, G, DH).transpose(0, 2, 1, 3).astype(F32)

    def by_group_t(t):
        return t.reshape(bs, s, G, DH).transpose(0, 2, 3, 1).astype(BF16)

    ones2 = jnp.zeros((LANES - DH,), F32).at[:2].set(1.0)

    def keys_aug(t, blocks=False):
        parts = [t, jnp.broadcast_to(ones2, t.shape[:-1] + (LANES - DH,))]
        if blocks:
            onehot = (jnp.arange(s)[:, None] // NSA_SEL_LEN == jnp.arange(LANES)[None, :]).astype(F32)
            parts.append(jnp.broadcast_to(onehot, t.shape[:-2] + (s, LANES)))
        return jnp.concatenate(parts, axis=-1).astype(BF16)

    chunks = jnp.stack([by_group(k_c), by_group(v_c)]).reshape(2, bs, G, nc, NSA_CMP_STRIDE * DH)
    pe = jnp.stack([ck_pe, cv_pe]).reshape(2, 1, NSA_CMP_LEN * DH)
    cmp = nsa_compress(chunks.astype(BF16), jnp.broadcast_to(pe, (2, 8, NSA_CMP_LEN * DH)).astype(BF16),
                       jnp.stack([ck_w1, cv_w1]).astype(BF16), jnp.stack([ck_w2, cv_w2]).astype(BF16))
    kc, vc_t = keys_aug(cmp[0]), cmp[1].transpose(0, 1, 3, 2).astype(BF16)

    rel_bias = rel_bias.astype(F32) * math.log2(math.e)
    far = rel_bias[REL_BUCKETS - 1]
    far_hi = far.astype(BF16).astype(F32)
    extra = jnp.zeros((NSA_HEADS, LANES - DH), F32).at[:, 0].set(far_hi).at[:, 1].set(far - far_hi)
    extra = jnp.broadcast_to(extra.reshape(G, HG, 1, LANES - DH), (G, HG, QB, LANES - DH))
    extra = extra.reshape(G, HG * QB, LANES - DH)
    gl = gate_logits.astype(F32).reshape(bs, nqt, QB, G, HG, 3).transpose(0, 3, 1, 5, 4, 2)
    gl = jnp.pad(gl.reshape(bs, G, nqt, 3, HG * QB), ((0, 0),) * 3 + ((0, 5), (0, 0)))

    td = (rel_bias[_bucket_table()] - rel_bias[REL_BUCKETS - 1][None, :]).T.astype(F32)
    cmp_end = jnp.minimum(jnp.arange(nc) * NSA_CMP_STRIDE + NSA_CMP_LEN - 1, s - 1)
    cpos = positions[cmp_end]
    qmin = positions.reshape(nqt, QB).min(axis=1)
    kmax = positions.reshape(nqt, QB).max(axis=1)
    near_s = (qmin[:, None] - positions.reshape(n_kt, NSA_KT).max(axis=1)[None, :]) < REL_MAX_DIST
    n_past = (jnp.arange(nqt) * QB + QB - 1) // NSA_KT
    n_fast = jnp.minimum(jnp.argmax(jnp.concatenate([near_s, jnp.ones((nqt, 1), bool)], axis=1), axis=1),
                         n_past)
    wt = jnp.clip(jnp.arange(nqt)[:, None] - (NSA_WTILES - 1) + jnp.arange(NSA_WTILES)[None, :], 0, nqt - 1)
    near_w = (qmin[:, None] - kmax[wt]) < REL_MAX_DIST
    near_c = (qmin[:, None] - cpos.reshape(n_ct, LANES).max(axis=1)[None, :]) < REL_MAX_DIST
    before = lax.cummax(kmax)[jnp.maximum(jnp.arange(nqt) - 2, 0)]
    std = (jnp.arange(nqt) >= NSA_WTILES - 1) & (qmin - before >= REL_MAX_DIST)

    cmp_start = np.arange(nc) * NSA_CMP_STRIDE
    sel_start = np.arange(LANES) * NSA_SEL_LEN
    agg_t = ((cmp_start[None, :] <= sel_start[:, None] + NSA_SEL_LEN - 1)
             & (cmp_start[None, :] + NSA_CMP_LEN - 1 >= sel_start[:, None])
             & (np.arange(nc)[None, :] < n_cmp) & (np.arange(LANES)[:, None] < n_sel))

    cols = HG * QB
    full = lambda shape: pl.BlockSpec(shape, lambda b, i, *_: (0,) * len(shape))
    per_b = lambda n, w: pl.BlockSpec((1, G, n, w), lambda b, i, *_: (b, 0, 0, 0))
    per_tile = lambda n, w: pl.BlockSpec((1, G, 1, n, w), lambda b, i, *_: (b, 0, i, 0, 0))
    grid_spec = pltpu.PrefetchScalarGridSpec(
        num_scalar_prefetch=4,
        grid=(bs, nqt),
        in_specs=[pl.BlockSpec((QB, NSA_Q), lambda b, i, *_: (b * nqt + i, OFF_NQ // NSA_Q)),
                  full((G, cols, LANES - DH)), per_tile(8, cols),
                  pl.BlockSpec((QB, LANES), lambda b, i, *_: (i, 0)),
                  full((s, LANES)), full((nc, LANES)),
                  per_b(s, 2 * LANES), per_b(DH, s), per_b(s, LANES), per_b(DH, s),
                  per_b(nc, LANES), per_b(DH, nc),
                  full((LANES, nc)), full((NSA_HEADS, LANES))],
        out_specs=pl.BlockSpec((1, QB, NSA_Q), lambda b, i, *_: (b, i, 0)),
        scratch_shapes=[pltpu.VMEM((2, G, 2, 1, cols), F32), pltpu.VMEM((2, G, 2, 1, cols), F32),
                        pltpu.VMEM((2, G, 2, DH, cols), F32)]
        + [pltpu.VMEM((G, 2, NSA_KT, cols), F32)] * 2 + [pltpu.VMEM((G, nc, cols), F32)],
    )
    return pl.pallas_call(
        functools.partial(_nsa_body, n_sel=n_sel, sel_k=sel_k),
        grid_spec=grid_spec,
        out_shape=jax.ShapeDtypeStruct((bs, s, NSA_Q), BF16),
        compiler_params=_params("parallel", "arbitrary"),
        name="nsa_attention",
    )(n_fast.astype(jnp.int32), std.astype(jnp.int32), near_w.reshape(-1).astype(jnp.int32),
      near_c.reshape(-1).astype(jnp.int32),
      proj, extra, gl, _pos_digits(positions, False), _pos_digits(positions, True), _pos_digits(cpos, True),
      keys_aug(by_group(k_s), blocks=True), by_group_t(v_s), keys_aug(by_group(k_w)), by_group_t(v_w),
      kc, vc_t, jnp.asarray(agg_t, BF16), td)


SGU_TC = 512


def _sgu_body(uv_ref, lg_ref, lb_ref, w_ref, b_ref, o_ref):
    uv = jax.nn.gelu(uv_ref[...].astype(F32))
    u, v = uv[:, :SGU_WIDTH], uv[:, SGU_WIDTH:]
    mu = jnp.mean(v, axis=-1, keepdims=True)
    vc = v - mu
    var = jnp.mean(vc * vc, axis=-1, keepdims=True)
    vn = (vc * lax.rsqrt(var + RMS_EPS) * lg_ref[...] + lb_ref[...]).astype(BF16)
    gw = SGU_WIDTH // SGU_GROUPS
    for c in range(SGU_TC // SGU_CHUNK):
        rows = slice(c * SGU_CHUNK, (c + 1) * SGU_CHUNK)
        for g in range(SGU_GROUPS):
            cols = slice(g * gw, (g + 1) * gw)
            mix = jnp.dot(w_ref[g], vn[rows, cols], preferred_element_type=F32) + b_ref[:, g:g + 1]
            o_ref[rows, cols] = (u[rows, cols] * mix).astype(o_ref.dtype)


def sgu_mixer(proj, ln_g, ln_b, w_s, b_s):
    t = proj.shape[0]
    assert t % SGU_TC == 0 and OFF_SGU % (2 * SGU_WIDTH) == 0
    fixed = lambda i: (0, 0)
    return pl.pallas_call(
        _sgu_body,
        grid=(t // SGU_TC,),
        in_specs=[pl.BlockSpec((SGU_TC, 2 * SGU_WIDTH), lambda i: (i, OFF_SGU // (2 * SGU_WIDTH))),
                  pl.BlockSpec((1, SGU_WIDTH), fixed), pl.BlockSpec((1, SGU_WIDTH), fixed),
                  pl.BlockSpec((SGU_GROUPS, SGU_CHUNK, SGU_CHUNK), lambda i: (0, 0, 0)),
                  pl.BlockSpec((SGU_CHUNK, SGU_GROUPS), fixed)],
        out_specs=pl.BlockSpec((SGU_TC, SGU_WIDTH), lambda i: (i, 0)),
        out_shape=jax.ShapeDtypeStruct((t, SGU_WIDTH), BF16),
        compiler_params=_params("parallel"),
        name="sgu_gate",
    )(proj, ln_g.reshape(1, SGU_WIDTH), ln_b.reshape(1, SGU_WIDTH), jnp.tril(w_s).astype(BF16), b_s.T)


def _permute_w_in(w):
    sizes = ([SSM_WIDTH, GDN_QKV, GDN_HEADS, GDN_HEADS, GDN_HEADS * GDN_DV, NSA_Q]
             + [NSA_KV] * 6 + [3 * NSA_HEADS, 2 * SGU_WIDTH, N_BRANCH * D_MODEL])
    cuts = [int(c) for c in np.cumsum(sizes)[:-1]]
    (w_ssm, w_qkv, w_a, w_b, w_z, w_nq, w_kc, w_vc, w_ks, w_vs, w_kw, w_vw,
     w_ng, w_sgu, w_gate) = jnp.split(w, cuts, axis=-1)
    pad = jnp.zeros((w.shape[0], PROJ_COLS - OFF_SMALL - SMALL_USED), w.dtype)
    return jnp.concatenate([w_qkv, w_ssm, w_z, w_nq, w_sgu, w_gate, w_kc, w_vc, w_ks, w_vs,
                            w_kw, w_vw, w_a, w_b, w_ng, pad], axis=-1).astype(BF16)


def kernel(x, positions, norm_mix, norm_ffn, norm_final, w_in, ssm_a_re, ssm_a_im, ssm_log_dt, ssm_b_re, ssm_b_im, ssm_c_re, ssm_c_im, ssm_d, ssm_glu_w, gdn_conv_w, gdn_a_log, gdn_dt_bias, gdn_norm, nsa_cmp_k_pe, nsa_cmp_k_w1, nsa_cmp_k_w2, nsa_cmp_v_pe, nsa_cmp_v_w1, nsa_cmp_v_w2, rel_bias, sgu_ln_g, sgu_ln_b, sgu_w, sgu_b, w_br_ssm, w_br_gdn, w_br_nsa, w_br_sgu, w_out, ffn_w_gate, ffn_w_up, ffn_w_down, moe_router, moe_w_gate, moe_w_up, moe_w_down):
    bs, s, d = x.shape
    t = bs * s
    assert DEPTH == 2
    xf = x.reshape(t, d)
    for layer in range(DEPTH):
        proj = in_proj(xf, norm_mix[layer], _permute_w_in(w_in[layer]))
        p3 = proj.reshape(bs, s, PROJ_COLS)

        def seg(off, width):
            return p3[..., off:off + width]

        y_ssm = ssm_mixer(proj, bs, ssm_a_re[layer], ssm_a_im[layer], ssm_log_dt[layer],
                          ssm_b_re[layer], ssm_b_im[layer], ssm_c_re[layer], ssm_c_im[layer],
                          ssm_d[layer], ssm_glu_w[layer])
        y_gdn = gdn_mixer(proj, bs, gdn_conv_w[layer], gdn_a_log[layer], gdn_dt_bias[layer],
                          gdn_norm[layer])
        kvs = [seg(OFF_KV + i * NSA_KV, NSA_KV) for i in range(6)]
        y_nsa = nsa_mixer(proj, *kvs, seg(OFF_SMALL + 2 * GDN_HEADS, 3 * NSA_HEADS),
                          positions, nsa_cmp_k_pe[layer], nsa_cmp_k_w1[layer], nsa_cmp_k_w2[layer],
                          nsa_cmp_v_pe[layer], nsa_cmp_v_w1[layer], nsa_cmp_v_w2[layer], rel_bias)
        y_sgu = sgu_mixer(proj, sgu_ln_g[layer], sgu_ln_b[layer], sgu_w[layer], sgu_b[layer])
        ys = [y_ssm, y_gdn, y_nsa.reshape(t, -1), y_sgu]
        ws = [w[layer].astype(BF16) for w in (w_br_ssm, w_br_gdn, w_br_nsa, w_br_sgu)]
        xf = merge(xf, proj, ys, ws, w_out[layer].astype(BF16))
        i = layer // 2
        if layer % 2 == 0:
            xf = dense_ffn(xf, norm_ffn[layer], ffn_w_gate[i], ffn_w_up[i], ffn_w_down[i])
        else:
            xf = moe_layer(xf, norm_ffn[layer], moe_router[i], moe_w_gate[i], moe_w_up[i],
                           moe_w_down[i], norm_final)
    return xf.reshape(bs, s, d)
```

```python
import functools
import math

import jax
import jax.numpy as jnp
from jax import lax
import numpy as np
from jax.experimental import pallas as pl
from jax.experimental.pallas import tpu as pltpu

F32 = jnp.float32
BF16 = jnp.bfloat16

D_MODEL = 1024
DEPTH = 2
SSM_WIDTH = D_MODEL // 2
SSM_GROUP = 16
SSM_GROUPS = SSM_WIDTH // SSM_GROUP
SSM_STATE = 64
GDN_HEADS = 4
GDN_DK = 128
GDN_DV = 128
GDN_CONV = 4
GDN_CHUNK = 64
NSA_HEADS = 8
NSA_KV_GROUPS = 2
NSA_HPG = NSA_HEADS // NSA_KV_GROUPS
NSA_DH = 64
NSA_CMP_LEN = 32
NSA_CMP_STRIDE = 16
NSA_CMP_HIDDEN = 128
NSA_SEL_LEN = 64
NSA_SEL_TOPK = 16
NSA_WINDOW = 512
NSA_QBLOCK = 128
SGU_WIDTH = D_MODEL // 2
SGU_GROUPS = 4
SGU_CHUNK = 128
REL_BUCKETS = 32
REL_MAX_DIST = 128
FFN_DENSE = 2816
N_EXPERTS = 8
TOP_K = 2
FFN_EXPERT = 3584
N_BRANCH = 4
RMS_EPS = 1e-6
GDN_QKV = GDN_HEADS * (2 * GDN_DK + GDN_DV)
NSA_Q = NSA_HEADS * NSA_DH
NSA_KV = NSA_KV_GROUPS * NSA_DH

LANES = 128
VMEM_LIMIT = 48 * 1024 * 1024
MASKED = -1e30

OFF_QKV = 0
OFF_SSM = OFF_QKV + GDN_QKV
OFF_Z = OFF_SSM + SSM_WIDTH
OFF_NQ = OFF_Z + GDN_HEADS * GDN_DV
OFF_SGU = OFF_NQ + NSA_Q
OFF_GATE = OFF_SGU + 2 * SGU_WIDTH
OFF_KV = OFF_GATE + N_BRANCH * D_MODEL
OFF_SMALL = OFF_KV + 6 * NSA_KV
SMALL_USED = 2 * GDN_HEADS + 3 * NSA_HEADS
PROJ_COLS = 9216


def _params(*sem):
    return pltpu.CompilerParams(dimension_semantics=sem, vmem_limit_bytes=VMEM_LIMIT)


def _rms(x, g):
    return x * lax.rsqrt(jnp.mean(x * x, axis=-1, keepdims=True) + RMS_EPS) * g


def _dot_nt(a, b):
    return lax.dot_general(a, b, (((1,), (1,)), ((), ())), preferred_element_type=F32)


def _dot_tn(a, b):
    return lax.dot_general(a, b, (((0,), (0,)), ((), ())), preferred_element_type=F32)


def _split_bf16(x):
    hi = x.astype(BF16)
    return hi, (x - hi.astype(F32)).astype(BF16)


def _in_proj_body(x_ref, g_ref, w_ref, o_ref, h_ref):
    @pl.when(pl.program_id(1) == 0)
    def _():
        h_ref[...] = _rms(x_ref[...], g_ref[...]).astype(BF16)

    o_ref[...] = jnp.dot(h_ref[...], w_ref[...], preferred_element_type=F32).astype(o_ref.dtype)


def in_proj(x, g, w, tm=2048, tn=1024):
    t, d = x.shape
    n = w.shape[1]
    return pl.pallas_call(
        _in_proj_body,
        grid=(t // tm, n // tn),
        in_specs=[pl.BlockSpec((tm, d), lambda i, j: (i, 0)),
                  pl.BlockSpec((1, d), lambda i, j: (0, 0)),
                  pl.BlockSpec((d, tn), lambda i, j: (0, j))],
        out_specs=pl.BlockSpec((tm, tn), lambda i, j: (i, j)),
        out_shape=jax.ShapeDtypeStruct((t, n), BF16),
        scratch_shapes=[pltpu.VMEM((tm, d), BF16)],
        compiler_params=_params("parallel", "arbitrary"),
        name="in_proj",
    )(x, g.reshape(1, d), w)


def _merge_body(x_ref, gate_ref, ya_ref, yb_ref, yc_ref, yd_ref,
                wa_ref, wb_ref, wc_ref, wd_ref, wo_ref, o_ref):
    def branch(k, y_ref, w_ref):
        p = jnp.dot(y_ref[...], w_ref[...], preferred_element_type=F32)
        return jax.nn.sigmoid(gate_ref[:, k * D_MODEL:(k + 1) * D_MODEL].astype(F32)) * p

    merged = (branch(0, ya_ref, wa_ref) + branch(1, yb_ref, wb_ref)
              + branch(2, yc_ref, wc_ref) + branch(3, yd_ref, wd_ref))
    o_ref[...] = x_ref[...] + jnp.dot(merged.astype(BF16), wo_ref[...],
                                      preferred_element_type=F32)


def merge(x, proj, ys, ws, w_out, tm=512):
    t, d = x.shape
    gate_blk = OFF_GATE // (N_BRANCH * D_MODEL)
    row = lambda i: (i, 0)
    fixed = lambda i: (0, 0)
    in_specs = [pl.BlockSpec((tm, d), row),
                pl.BlockSpec((tm, N_BRANCH * D_MODEL), lambda i: (i, gate_blk))]
    in_specs += [pl.BlockSpec((tm, y.shape[1]), row) for y in ys]
    in_specs += [pl.BlockSpec(w.shape, fixed) for w in ws]
    in_specs += [pl.BlockSpec(w_out.shape, fixed)]
    return pl.pallas_call(
        _merge_body,
        grid=(t // tm,),
        in_specs=in_specs,
        out_specs=pl.BlockSpec((tm, d), row),
        out_shape=jax.ShapeDtypeStruct((t, d), F32),
        compiler_params=_params("parallel"),
        name="merge",
    )(x, proj, *ys, *ws, w_out)


def _ffn_body(x_ref, g_ref, wg_ref, wu_ref, wd_ref, o_ref, h_ref, acc_ref):
    f = pl.program_id(1)

    @pl.when(f == 0)
    def _():
        h_ref[...] = _rms(x_ref[...], g_ref[...]).astype(BF16)
        acc_ref[...] = x_ref[...]

    h = h_ref[...]
    a = jnp.dot(h, wg_ref[...].astype(BF16), preferred_element_type=F32)
    u = jnp.dot(h, wu_ref[...].astype(BF16), preferred_element_type=F32)
    act = (a * jax.nn.sigmoid(a) * u).astype(BF16)
    acc_ref[...] += jnp.dot(act, wd_ref[...].astype(BF16), preferred_element_type=F32)

    @pl.when(f == pl.num_programs(1) - 1)
    def _():
        o_ref[...] = acc_ref[...]


def dense_ffn(x, g, wg, wu, wd, tm=1024, tf=256):
    t, d = x.shape
    nf = wg.shape[1]
    return pl.pallas_call(
        _ffn_body,
        grid=(t // tm, nf // tf),
        in_specs=[pl.BlockSpec((tm, d), lambda i, f: (i, 0)),
                  pl.BlockSpec((1, d), lambda i, f: (0, 0)),
                  pl.BlockSpec((d, tf), lambda i, f: (0, f)),
                  pl.BlockSpec((d, tf), lambda i, f: (0, f)),
                  pl.BlockSpec((tf, d), lambda i, f: (f, 0))],
        out_specs=pl.BlockSpec((tm, d), lambda i, f: (i, 0)),
        out_shape=jax.ShapeDtypeStruct((t, d), F32),
        scratch_shapes=[pltpu.VMEM((tm, d), BF16), pltpu.VMEM((tm, d), F32)],
        compiler_params=_params("parallel", "arbitrary"),
        name="dense_ffn",
    )(x, g.reshape(1, d), wg, wu, wd)


def _route_body(x_ref, g_ref, rhi_ref, rlo_ref, h_ref, idx_ref, wt_ref):
    h = _rms(x_ref[...], g_ref[...])
    hi = h.astype(BF16)
    lo = (h - hi.astype(F32)).astype(BF16)
    h_ref[...] = h
    logits = (jnp.dot(hi, rhi_ref[...], preferred_element_type=F32)
              + jnp.dot(hi, rlo_ref[...], preferred_element_type=F32)
              + jnp.dot(lo, rhi_ref[...], preferred_element_type=F32))
    col = lax.broadcasted_iota(jnp.int32, logits.shape, 1).astype(F32)
    neg = jnp.float32(-jnp.inf)
    lg = jnp.where(col < N_EXPERTS, logits, neg)
    m1 = jnp.max(lg, axis=-1, keepdims=True)
    i1 = jnp.min(jnp.where(lg == m1, col, float(LANES)), axis=-1, keepdims=True)
    lg2 = jnp.where(col == i1, neg, lg)
    m2 = jnp.max(lg2, axis=-1, keepdims=True)
    i2 = jnp.min(jnp.where(lg2 == m2, col, float(LANES)), axis=-1, keepdims=True)
    e = jnp.exp(m2 - m1)
    w1 = 1.0 / (1.0 + e)
    w2 = e / (1.0 + e)
    idx_ref[...] = jnp.where(col == 0, i1, jnp.where(col == 1, i2, 0.0)).astype(jnp.int32)
    wt_ref[...] = jnp.where(col == 0, w1, jnp.where(col == 1, w2, 0.0))


def moe_route(x, g, r_hi, r_lo, tm=512):
    t, d = x.shape
    row = lambda i: (i, 0)
    fixed = lambda i: (0, 0)
    return pl.pallas_call(
        _route_body,
        grid=(t // tm,),
        in_specs=[pl.BlockSpec((tm, d), row), pl.BlockSpec((1, d), fixed),
                  pl.BlockSpec((d, LANES), fixed), pl.BlockSpec((d, LANES), fixed)],
        out_specs=[pl.BlockSpec((tm, d), row), pl.BlockSpec((tm, LANES), row),
                   pl.BlockSpec((tm, LANES), row)],
        out_shape=[jax.ShapeDtypeStruct((t, d), F32),
                   jax.ShapeDtypeStruct((t, LANES), jnp.int32),
                   jax.ShapeDtypeStruct((t, LANES), F32)],
        compiler_params=_params("parallel"),
        name="moe_route",
    )(x, g.reshape(1, d), r_hi, r_lo)


def _experts_body(te_ref, nu_ref, src_ref, h_hbm, wg_ref, wu_ref, wd_ref, o_ref,
                  xbuf_ref, xs_ref, acc_ref, sem, *, rows_per_step):
    i = pl.program_id(0)
    f = pl.program_id(1)
    tm = xs_ref.shape[0]
    n_rows = xbuf_ref.shape[1]
    n_used = nu_ref[0]
    used = i < n_used
    has_next = i + 1 < n_used
    slot = i % 2

    def start_row(tile, slot_, r):
        tok = src_ref[tile * tm + r]
        pltpu.make_async_copy(h_hbm.at[pl.ds(tok, 1), :], xbuf_ref.at[slot_, pl.ds(r, 1), :],
                              sem.at[slot_]).start()

    @pl.when(f == 0)
    def _():
        acc_ref[...] = jnp.zeros_like(acc_ref)

        @pl.when(used)
        def _():
            @pl.when(i == 0)
            def _():
                def row(r, carry):
                    start_row(0, 0, r)
                    return carry

                lax.fori_loop(0, n_rows, row, 0, unroll=8)

            pltpu.make_async_copy(h_hbm.at[pl.ds(0, n_rows), :], xbuf_ref.at[slot], sem.at[slot]).wait()
            xs_ref[...] = xbuf_ref[slot, 0:tm, :].astype(BF16)

    def ffn_step():
        h = xs_ref[...]
        a = jnp.dot(h, wg_ref[0].astype(BF16), preferred_element_type=F32)
        u = jnp.dot(h, wu_ref[0].astype(BF16), preferred_element_type=F32)
        act = (a * jax.nn.sigmoid(a) * u).astype(BF16)
        acc_ref[...] += jnp.dot(act, wd_ref[0].astype(BF16), preferred_element_type=F32)

    @pl.when(used & has_next)
    def _():
        for j in range(rows_per_step):
            start_row(i + 1, 1 - slot, f * rows_per_step + j)
        ffn_step()

    @pl.when(used & jnp.logical_not(has_next))
    def _():
        ffn_step()

    @pl.when(f == pl.num_programs(1) - 1)
    def _():
        o_ref[...] = acc_ref[...].astype(o_ref.dtype)


def moe_experts(tile_expert, n_used, src, h, wg, wu, wd, tm, tf=512):
    d = h.shape[1]
    nf = wg.shape[2] // tf
    rows_per_step = 8 * (-(-tm // (8 * nf)))
    n_rows = rows_per_step * nf
    n_tiles = src.shape[0] // tm
    src = jnp.pad(src, (0, n_rows - tm))
    grid_spec = pltpu.PrefetchScalarGridSpec(
        num_scalar_prefetch=3,
        grid=(n_tiles, nf),
        in_specs=[pl.BlockSpec(memory_space=pl.ANY),
                  pl.BlockSpec((1, d, tf), lambda i, f, te, *_: (te[i], 0, f)),
                  pl.BlockSpec((1, d, tf), lambda i, f, te, *_: (te[i], 0, f)),
                  pl.BlockSpec((1, tf, d), lambda i, f, te, *_: (te[i], f, 0))],
        out_specs=pl.BlockSpec((tm, d), lambda i, f, *_: (i, 0)),
        scratch_shapes=[pltpu.VMEM((2, n_rows, d), F32), pltpu.VMEM((tm, d), BF16),
                        pltpu.VMEM((tm, d), F32), pltpu.SemaphoreType.DMA((2,))],
    )
    return pl.pallas_call(
        functools.partial(_experts_body, rows_per_step=rows_per_step),
        grid_spec=grid_spec,
        out_shape=jax.ShapeDtypeStruct((n_tiles * tm, d), BF16),
        compiler_params=_params("arbitrary", "arbitrary"),
        name="moe_experts",
    )(tile_expert, n_used, src, h, wg, wu, wd)


def _combine_norm_body(x_ref, ya_ref, yb_ref, wt_ref, g_ref, o_ref):
    wt = wt_ref[...]
    y = wt[:, 0:1] * ya_ref[...].astype(F32) + wt[:, 1:2] * yb_ref[...].astype(F32)
    o_ref[...] = _rms(x_ref[...] + y, g_ref[...])


def combine_norm(x, y_pairs, wts, g, tm=1024):
    t, d = x.shape
    nt = t // tm
    row = lambda i: (i, 0)
    return pl.pallas_call(
        _combine_norm_body,
        grid=(nt,),
        in_specs=[pl.BlockSpec((tm, d), row), pl.BlockSpec((tm, d), row),
                  pl.BlockSpec((tm, d), lambda i: (i + nt, 0)), pl.BlockSpec((tm, LANES), row),
                  pl.BlockSpec((1, d), lambda i: (0, 0))],
        out_specs=pl.BlockSpec((tm, d), row),
        out_shape=jax.ShapeDtypeStruct((t, d), F32),
        compiler_params=_params("parallel"),
        name="combine_norm",
    )(x, y_pairs, y_pairs, wts, g.reshape(1, d))


def moe_layer(x, g_norm, router, wg, wu, wd, g_final, tm=1024):
    t, d = x.shape
    r = jnp.zeros((d, LANES), F32).at[:, :N_EXPERTS].set(router)
    r_hi = r.astype(BF16)
    r_lo = (r - r_hi.astype(F32)).astype(BF16)
    h, idx, wts = moe_route(x, g_norm, r_hi, r_lo)
    e_flat = jnp.concatenate([idx[:, 0], idx[:, 1]])
    onehot = (e_flat[:, None] == jnp.arange(N_EXPERTS)[None, :]).astype(jnp.int32)
    csum = jnp.cumsum(onehot, axis=0)
    rank = jnp.sum((csum - onehot) * onehot, axis=1)
    counts = csum[-1]
    padded = ((counts + tm - 1) // tm) * tm
    ends = jnp.cumsum(padded)
    starts = ends - padded
    dest = starts[e_flat] + rank
    n_tiles = (TOP_K * t) // tm + N_EXPERTS
    p = n_tiles * tm
    tok = jnp.concatenate([jnp.arange(t, dtype=jnp.int32)] * TOP_K)
    src = jnp.zeros((p,), jnp.int32).at[dest].set(tok)
    tile_start = jnp.arange(n_tiles, dtype=jnp.int32) * tm
    tile_expert = jnp.minimum(jnp.sum((ends[None, :] <= tile_start[:, None]).astype(jnp.int32), axis=1),
                              N_EXPERTS - 1)
    n_used = (ends[-1] // tm).astype(jnp.int32).reshape(1)
    ys = moe_experts(tile_expert, n_used, src, h, wg, wu, wd, tm)
    y_pairs = jnp.take(ys, dest, axis=0, mode="clip")
    return combine_norm(x, y_pairs, wts, g_final)


SSM_TC = 512
SSM_SUB = 128
SSM_HALF = 256
SSM_NSTATE = SSM_GROUPS * SSM_STATE


def _ssm_body(u_ref, bw_ref, cw_ref, d_ref, glu_ref, ar_ref, ai_ref, pr_ref, pi_ref, o_ref,
              xr_ref, xi_ref, cr_ref, ci_ref):
    @pl.when(pl.program_id(1) == 0)
    def _():
        cr_ref[...] = jnp.zeros_like(cr_ref)
        ci_ref[...] = jnp.zeros_like(ci_ref)

    ub = u_ref[...]
    u = ub.astype(F32)
    nblk = SSM_WIDTH // SSM_HALF
    sblk = SSM_NSTATE // nblk
    for k in range(nblk):
        bu = jnp.dot(ub[:, k * SSM_HALF:(k + 1) * SSM_HALF], bw_ref[k], preferred_element_type=F32)
        xr_ref[:, k * sblk:(k + 1) * sblk] = bu[:, :sblk]
        xi_ref[:, k * sblk:(k + 1) * sblk] = bu[:, sblk:]

    row8 = lax.broadcasted_iota(jnp.int32, (SSM_SUB, LANES), 0) % 8
    n_grp = SSM_SUB // 8

    def strip(c, carry):
        col = pl.ds(pl.multiple_of(c * LANES, LANES), LANES)
        c_r = cr_ref[:, col]
        c_i = ci_ref[:, col]
        p_r = pr_ref[0:8, col]
        p_i = pi_ref[0:8, col]
        step_mul = [(jnp.where(row8 >= (1 << i), ar_ref[i:i + 1, col], 0.0),
                     jnp.where(row8 >= (1 << i), ai_ref[i:i + 1, col], 0.0)) for i in range(3)]
        for sub in range(SSM_TC // SSM_SUB):
            rows = slice(sub * SSM_SUB, (sub + 1) * SSM_SUB)
            xr = xr_ref[rows, col]
            xi = xi_ref[rows, col]
            for i in range(3):
                a_r, a_i = step_mul[i]
                sr, si = pltpu.roll(xr, 1 << i, 0), pltpu.roll(xi, 1 << i, 0)
                xr, xi = xr + a_r * sr - a_i * si, xi + a_r * si + a_i * sr
            out_r, out_i = [], []
            for r in range(n_grp):
                g_r = xr[8 * r:8 * r + 8] + p_r * c_r - p_i * c_i
                g_i = xi[8 * r:8 * r + 8] + p_r * c_i + p_i * c_r
                c_r, c_i = g_r[7:8, :], g_i[7:8, :]
                out_r.append(g_r)
                out_i.append(g_i)
            xr_ref[rows, col] = jnp.concatenate(out_r, axis=0)
            xi_ref[rows, col] = jnp.concatenate(out_i, axis=0)
        cr_ref[:, col] = c_r
        ci_ref[:, col] = c_i
        return carry

    lax.fori_loop(0, SSM_NSTATE // LANES, strip, 0)

    ys = []
    for k in range(nblk):
        st = jnp.concatenate([xr_ref[:, k * sblk:(k + 1) * sblk].astype(BF16),
                              xi_ref[:, k * sblk:(k + 1) * sblk].astype(BF16)], axis=1)
        ys.append(jnp.dot(st, cw_ref[k], preferred_element_type=F32))
    y = jax.nn.gelu(jnp.concatenate(ys, axis=1) + d_ref[...] * u)
    gated = y * jax.nn.sigmoid(jnp.dot(y.astype(BF16), glu_ref[...], preferred_element_type=F32))
    o_ref[...] = gated.astype(o_ref.dtype)


def ssm_mixer(proj, bs, a_re, a_im, log_dt, b_re, b_im, c_re, c_im, d_skip, glu_w):
    t = proj.shape[0]
    s = t // bs
    nt = s // SSM_TC
    assert s % SSM_TC == 0 and OFF_SSM % SSM_WIDTH == 0
    nblk = SSM_WIDTH // SSM_HALF
    gpb = SSM_GROUPS // nblk
    dt = jnp.exp(log_dt)[:, None]
    mag = jnp.exp(a_re * dt)
    abar_r, abar_i = mag * jnp.cos(a_im * dt), mag * jnp.sin(a_im * dt)
    nr, ni = abar_r - 1.0, abar_i
    den = a_re * a_re + a_im * a_im
    sr, si = (nr * a_re + ni * a_im) / den, (ni * a_re - nr * a_im) / den
    bbar_r = sr[..., None] * b_re - si[..., None] * b_im
    bbar_i = sr[..., None] * b_im + si[..., None] * b_re

    def powers(k):
        kk = k.astype(F32)[:, None, None]
        m = jnp.exp(kk * (a_re * dt))
        return ((m * jnp.cos(kk * (a_im * dt))).reshape(-1, SSM_NSTATE),
                (m * jnp.sin(kk * (a_im * dt))).reshape(-1, SSM_NSTATE))

    ar, ai = powers(2 ** jnp.arange(8))
    pr, pi = powers(jnp.arange(1, SSM_SUB + 1))
    eye = jnp.eye(gpb, dtype=F32)

    def in_block(w):
        return jnp.einsum('gnp,gh->gphn', w, eye).reshape(gpb * SSM_GROUP, gpb * SSM_STATE)

    def out_block(w):
        return jnp.einsum('gpn,gh->gnhp', w, eye).reshape(gpb * SSM_STATE, gpb * SSM_GROUP)

    bw = jnp.stack([jnp.concatenate([in_block(bbar_r[k * gpb:(k + 1) * gpb]),
                                     in_block(bbar_i[k * gpb:(k + 1) * gpb])], axis=1)
                    for k in range(nblk)]).astype(BF16)
    cw = jnp.stack([jnp.concatenate([out_block(c_re[k * gpb:(k + 1) * gpb]),
                                     -out_block(c_im[k * gpb:(k + 1) * gpb])], axis=0)
                    for k in range(nblk)]).astype(BF16)
    fixed2 = lambda b, i: (0, 0)
    fixed3 = lambda b, i: (0, 0, 0)
    return pl.pallas_call(
        _ssm_body,
        grid=(bs, nt),
        in_specs=[pl.BlockSpec((SSM_TC, SSM_WIDTH), lambda b, i: (b * nt + i, OFF_SSM // SSM_WIDTH)),
                  pl.BlockSpec(bw.shape, fixed3), pl.BlockSpec(cw.shape, fixed3),
                  pl.BlockSpec((1, SSM_WIDTH), fixed2), pl.BlockSpec((SSM_WIDTH, SSM_WIDTH), fixed2),
                  pl.BlockSpec((8, SSM_NSTATE), fixed2), pl.BlockSpec((8, SSM_NSTATE), fixed2),
                  pl.BlockSpec((SSM_SUB, SSM_NSTATE), fixed2), pl.BlockSpec((SSM_SUB, SSM_NSTATE), fixed2)],
        out_specs=pl.BlockSpec((SSM_TC, SSM_WIDTH), lambda b, i: (b * nt + i, 0)),
        out_shape=jax.ShapeDtypeStruct((t, SSM_WIDTH), BF16),
        scratch_shapes=[pltpu.VMEM((SSM_TC, SSM_NSTATE), F32), pltpu.VMEM((SSM_TC, SSM_NSTATE), F32),
                        pltpu.VMEM((1, SSM_NSTATE), F32), pltpu.VMEM((1, SSM_NSTATE), F32)],
        compiler_params=_params("parallel", "arbitrary"),
        name="ssm_scan",
    )(proj, bw, cw, d_skip.reshape(1, SSM_WIDTH), glu_w.astype(BF16), ar, ai, pr, pi)


GDN_TC = 256
GDN_HALO = 8


def _gdn_body(nega_ref, dtb_ref, q_ref, k_ref, v_ref, z_ref, sm_ref, wq_ref, wk_ref, wv_ref, ng_ref,
              o_ref, state_ref, hq_ref, hk_ref, hv_ref, vnew_ref):
    TC, CH, DK, H = GDN_TC, GDN_CHUNK, GDN_DK, GDN_HEADS
    heads = range(H)

    @pl.when(pl.program_id(1) == 0)
    def _():
        state_ref[...] = jnp.zeros_like(state_ref)
        hq_ref[...] = jnp.zeros_like(hq_ref)
        hk_ref[...] = jnp.zeros_like(hk_ref)
        hv_ref[...] = jnp.zeros_like(hv_ref)

    def conv_silu(x_ref, halo_ref, w_ref):
        x = x_ref[...].astype(F32)
        xx = jnp.concatenate([halo_ref[...], x], axis=0)
        w = w_ref[...]
        y = w[GDN_CONV - 1:GDN_CONV] * x
        for d in range(1, GDN_CONV):
            y = y + w[GDN_CONV - 1 - d:GDN_CONV - d] * pltpu.roll(xx, d, 0)[GDN_HALO:GDN_HALO + TC]
        halo_ref[...] = x[TC - GDN_HALO:TC]
        return y * jax.nn.sigmoid(y)

    def per_head(x):
        return [x[:, h * LANES:(h + 1) * LANES] for h in heads]

    q = per_head(conv_silu(q_ref, hq_ref, wq_ref))
    k = per_head(conv_silu(k_ref, hk_ref, wk_ref))
    v = per_head(conv_silu(v_ref, hv_ref, wv_ref))
    q = [x * lax.rsqrt(jnp.sum(x * x, axis=-1, keepdims=True) + 1e-6) * (DK ** -0.5) for x in q]
    k = [x * lax.rsqrt(jnp.sum(x * x, axis=-1, keepdims=True) + 1e-6) for x in k]

    small = sm_ref[...].astype(F32)
    beta = [jax.nn.sigmoid(small[:, H + h:H + h + 1]) for h in heads]
    la = []
    for h in heads:
        xa = small[:, h:h + 1] + dtb_ref[h]
        softplus = jnp.maximum(xa, 0.0) + jnp.log(1.0 + jnp.exp(-jnp.abs(xa)))
        la.append(jnp.broadcast_to(nega_ref[h] * softplus, (TC, LANES)))

    ri = lax.broadcasted_iota(jnp.int32, (TC, TC), 0)
    ci = lax.broadcasted_iota(jnp.int32, (TC, TC), 1)
    same = (ri // CH) == (ci // CH)
    causal = jnp.where(same, jnp.where(ci <= ri, 1.0, 0.0), 0.0)
    strict = jnp.where(ci < ri, causal, 0.0)
    eye = jnp.where(ri == ci, 1.0, 0.0)
    tri = causal.astype(BF16)
    lane = lax.broadcasted_iota(jnp.int32, (TC, LANES), 1)

    g = []
    for h in heads:
        la_hi, la_lo = _split_bf16(la[h])
        g.append(jnp.dot(tri, la_hi, preferred_element_type=F32)
                 + jnp.dot(tri, la_lo, preferred_element_type=F32))
    decay = []
    for h in heads:
        g_hi = g[h].astype(BF16).astype(F32)
        g_lo = g[h] - g_hi
        lhs = jnp.where(lane == 0, g_hi, jnp.where(lane == 1, g_lo, jnp.where(lane < 4, 1.0, 0.0)))
        rhs = jnp.where(lane < 2, 1.0, jnp.where(lane == 2, -g_hi, jnp.where(lane == 3, -g_lo, 0.0)))
        decay.append(jnp.exp(jnp.where(causal > 0.5, _dot_nt(lhs.astype(BF16), rhs.astype(BF16)), MASKED)))

    kb = [k[h] * beta[h] for h in heads]
    k_b = [x.astype(BF16) for x in k]
    lmat = [strict * _dot_nt(kb[h].astype(BF16), k_b[h]) * decay[h] for h in heads]
    tinv = [eye - x for x in lmat]
    pw = lmat
    for _ in range(CH.bit_length() - 2):
        pw = [jnp.dot(x.astype(BF16), x.astype(BF16), preferred_element_type=F32) for x in pw]
        tinv = [tinv[h] + jnp.dot(tinv[h].astype(BF16), pw[h].astype(BF16), preferred_element_type=F32)
                for h in heads]
    eg = [jnp.exp(x) for x in g]
    sol = [jnp.dot(tinv[h].astype(BF16),
                   jnp.concatenate([v[h] * beta[h], kb[h] * eg[h]], axis=1).astype(BF16),
                   preferred_element_type=F32) for h in heads]
    attn = [(causal * _dot_nt(q[h].astype(BF16), k_b[h]) * decay[h]).astype(BF16) for h in heads]
    q_st = [(q[h] * eg[h]).astype(BF16) for h in heads]

    vnew_ref[...] = jnp.zeros_like(vnew_ref)
    for c in range(TC // CH):
        rows = slice(c * CH, (c + 1) * CH)
        g_last = [x[(c + 1) * CH - 1:(c + 1) * CH, :] for x in g]
        st = [state_ref[h] for h in heads]
        st_b = [x.astype(BF16) for x in st]
        v_new = [sol[h][rows, :GDN_DV]
                 - jnp.dot(sol[h][rows, GDN_DV:].astype(BF16), st_b[h], preferred_element_type=F32)
                 for h in heads]
        for h in heads:
            vnew_ref[h, rows, :] = v_new[h].astype(BF16)
        o_c = [jnp.dot(q_st[h][rows], st_b[h], preferred_element_type=F32)
               + jnp.dot(attn[h][rows], vnew_ref[h], preferred_element_type=F32) for h in heads]
        for h in heads:
            k_st = (k[h][rows] * jnp.exp(g_last[h] - g[h][rows])).astype(BF16)
            state_ref[h] = st[h] * jnp.exp(g_last[h][:, :1]) + _dot_tn(k_st, v_new[h].astype(BF16))
        for h in heads:
            o = o_c[h] * lax.rsqrt(jnp.mean(o_c[h] * o_c[h], axis=-1, keepdims=True) + RMS_EPS) * ng_ref[...]
            zc = z_ref[rows, h * LANES:(h + 1) * LANES].astype(F32)
            o_ref[rows, h * LANES:(h + 1) * LANES] = (o * (zc * jax.nn.sigmoid(zc))).astype(o_ref.dtype)


def gdn_mixer(proj, bs, conv_w, a_log, dt_bias, norm_g):
    t = proj.shape[0]
    s = t // bs
    nt = s // GDN_TC
    H = GDN_HEADS
    hw = H * LANES
    assert s % GDN_TC == 0 and GDN_DK == LANES and GDN_DV == LANES and OFF_QKV == 0 and OFF_Z % hw == 0
    cw = jnp.zeros((8, GDN_QKV), F32).at[:GDN_CONV].set(conv_w)
    tok = lambda blk: pl.BlockSpec((GDN_TC, hw), lambda b, i: (b * nt + i, blk))
    wspec = lambda blk: pl.BlockSpec((8, hw), lambda b, i: (0, blk))
    smem = pl.BlockSpec(memory_space=pltpu.SMEM)
    return pl.pallas_call(
        _gdn_body,
        grid=(bs, nt),
        in_specs=[smem, smem, tok(0), tok(1), tok(2), tok(OFF_Z // hw),
                  pl.BlockSpec((GDN_TC, 2 * LANES), lambda b, i: (b * nt + i, OFF_SMALL // (2 * LANES))),
                  wspec(0), wspec(1), wspec(2), pl.BlockSpec((1, GDN_DV), lambda b, i: (0, 0))],
        out_specs=pl.BlockSpec((GDN_TC, hw), lambda b, i: (b * nt + i, 0)),
        out_shape=jax.ShapeDtypeStruct((t, hw), BF16),
        scratch_shapes=[pltpu.VMEM((H, GDN_DK, GDN_DV), F32)] + [pltpu.VMEM((GDN_HALO, hw), F32)] * 3
        + [pltpu.VMEM((H, GDN_TC, GDN_DV), BF16)],
        compiler_params=_params("parallel", "arbitrary"),
        name="gdn_delta",
    )(-jnp.exp(a_log), dt_bias.astype(F32), proj, proj, proj, proj, proj, cw, cw, cw,
      norm_g.reshape(1, GDN_DV))


NSA_KT = 256
NSA_NEG = MASKED
NSA_WTILES = NSA_WINDOW // LANES + 1


def _bucket_table():
    n = np.arange(LANES)
    max_exact = REL_BUCKETS // 2
    nf = np.maximum(n, 1).astype(np.float32)
    large = max_exact + (np.log(nf / np.float32(max_exact)) / np.float32(math.log(REL_MAX_DIST / max_exact))
                         * np.float32(REL_BUCKETS - max_exact)).astype(np.int32)
    tbl = np.where(n < max_exact, n, np.minimum(large, REL_BUCKETS - 1))
    assert tbl[-1] == REL_BUCKETS - 1 and REL_MAX_DIST <= LANES
    return tbl


def _compress_body(x_ref, pe_ref, w1_ref, w2_ref, o_ref):
    x = x_ref[0, 0, 0]
    w1 = w1_ref[0]
    half = NSA_CMP_STRIDE * NSA_DH
    nc = x.shape[0]
    a = jnp.dot(x, w1[:half], preferred_element_type=F32)
    b = jnp.dot(x, w1[half:], preferred_element_type=F32)
    pe_h = jnp.dot(pe_ref[0], w1, preferred_element_type=F32)
    hid = a + pltpu.roll(b, nc - 1, 0) + pe_h[0:1, :]
    o_ref[0, 0, 0] = jnp.dot(jax.nn.gelu(hid).astype(BF16), w2_ref[0], preferred_element_type=F32)


def nsa_compress(x, pe, w1, w2):
    _, bs, g, nc, width = x.shape
    return pl.pallas_call(
        _compress_body,
        grid=(2, bs, g),
        in_specs=[pl.BlockSpec((1, 1, 1, nc, width), lambda i, b, j: (i, b, j, 0, 0)),
                  pl.BlockSpec((1,) + pe.shape[1:], lambda i, b, j: (i, 0, 0)),
                  pl.BlockSpec((1,) + w1.shape[1:], lambda i, b, j: (i, 0, 0)),
                  pl.BlockSpec((1,) + w2.shape[1:], lambda i, b, j: (i, 0, 0))],
        out_specs=pl.BlockSpec((1, 1, 1, nc, NSA_DH), lambda i, b, j: (i, b, j, 0, 0)),
        out_shape=jax.ShapeDtypeStruct((2, bs, g, nc, NSA_DH), F32),
        compiler_params=_params("parallel", "parallel", "parallel"),
        name="nsa_compress",
    )(x, pe, w1, w2)


def _nsa_body(nfast, std, near_w, near_c, q_ref, ex_ref, gl_ref, qp_ref, kp_ref, cp_ref,
              ks_ref, vs_ref, kw_ref, vw_ref, kc_ref, vc_ref, agg_ref, td_ref, o_ref,
              m_ref, l_ref, acc_ref, sa_ref, sb_ref, sc_ref, oc_ref, imp_ref, *, n_sel, sel_k):
    QB, HG, G = NSA_QBLOCK, NSA_HPG, NSA_KV_GROUPS
    groups = range(G)
    qi = pl.program_id(1)
    s0 = qi * QB
    nc = kc_ref.shape[2]
    n_ct = nc // LANES
    q_all = q_ref[...].astype(F32) * (NSA_DH ** -0.5 * math.log2(math.e))
    q = []
    for g in groups:
        rows = jnp.concatenate([q_all[:, (g * HG + hg) * NSA_DH:(g * HG + hg + 1) * NSA_DH]
                                for hg in range(HG)], axis=0)
        q.append(jnp.concatenate([rows, ex_ref[g]], axis=1).astype(BF16))
    qp = qp_ref[...]

    def lanes4(x):
        return jnp.concatenate([x] * HG, axis=1)

    def delta_t(kp):
        idx = jnp.clip(_dot_nt(kp, qp), 0.0, float(LANES - 1)).astype(jnp.int32)
        outs = []
        for g in groups:
            heads = []
            for hg in range(HG):
                tb = jnp.broadcast_to(td_ref[g * HG + hg:g * HG + hg + 1, :], idx.shape)
                heads.append(jnp.take_along_axis(tb, idx, axis=1))
            outs.append(jnp.concatenate(heads, axis=1))
        return outs

    def maybe_delta(flag, kp):
        zeros = jnp.zeros((kp.shape[0], HG * QB), F32)
        return lax.cond(flag != 0, lambda: tuple(delta_t(kp)), lambda: (zeros,) * G)

    gate = [jax.nn.sigmoid(gl_ref[0, g, 0]) for g in groups]

    for g in groups:
        sc_ref[g] = _dot_nt(kc_ref[0, g], q[g])
    near_rows = 4 * 8

    @pl.when(std[qi] != 0)
    def _():
        c0 = pl.multiple_of(jnp.minimum(qi * (QB // NSA_CMP_STRIDE) - near_rows // 2, nc - near_rows), 8)
        d = delta_t(cp_ref[pl.ds(c0, near_rows), :])
        for g in groups:
            sc_ref[g, pl.ds(c0, near_rows), :] += d[g]

    @pl.when(std[qi] == 0)
    def _():
        for ct in range(n_ct):
            @pl.when(near_c[qi * n_ct + ct] != 0)
            def _():
                d = delta_t(cp_ref[ct * LANES:(ct + 1) * LANES, :])
                for g in groups:
                    sc_ref[g, ct * LANES:(ct + 1) * LANES, :] += d[g]

    blk = lax.broadcasted_iota(jnp.int32, (LANES, QB), 0)
    t_tok = s0 + lax.broadcasted_iota(jnp.int32, (LANES, QB), 1)
    tb_blk = t_tok // NSA_SEL_LEN
    forced = jnp.where(blk == 0, 1.0, 0.0) + jnp.where(blk == tb_blk, 1.0, 0.0) \
        + jnp.where(blk == tb_blk - 1, 1.0, 0.0)
    blkf = blk.astype(F32)
    q_tok = s0 + lax.broadcasted_iota(jnp.int32, (1, QB), 1)
    any_ok = lanes4(jnp.where(q_tok >= NSA_CMP_LEN - 1, 1.0, 0.0))

    def compressed(nr):
        c_id = lax.broadcasted_iota(jnp.int32, (nr, QB), 0)
        c_q = lax.broadcasted_iota(jnp.int32, (nr, QB), 1)
        ok_c = lanes4(jnp.where(c_id * NSA_CMP_STRIDE + (NSA_CMP_LEN - 1) <= s0 + c_q, 1.0, 0.0))
        for g in groups:
            s = jnp.where(ok_c > 0.5, sc_ref[g, 0:nr, :], NSA_NEG)
            e = jnp.exp2(s - jnp.max(s, axis=0, keepdims=True))
            p = e * (any_ok / jnp.sum(e, axis=0, keepdims=True))
            oc_ref[g] = gate[g][0:1, :] * jnp.dot(vc_ref[0, g, :, 0:nr], p.astype(BF16),
                                                  preferred_element_type=F32)
            ps = p[:, 0:QB]
            for hg in range(1, HG):
                ps = ps + p[:, hg * QB:(hg + 1) * QB]
            ps_hi, ps_lo = _split_bf16(ps)
            imp_ref[g] = (jnp.dot(agg_ref[:, 0:nr], ps_hi, preferred_element_type=F32)
                          + jnp.dot(agg_ref[:, 0:nr], ps_lo, preferred_element_type=F32))

    last_half_tile = (nc // 2 - QB // NSA_CMP_STRIDE + 1) // (QB // NSA_CMP_STRIDE)

    @pl.when(qi <= last_half_tile)
    def _():
        compressed(nc // 2)

    @pl.when(qi > last_half_tile)
    def _():
        compressed(nc)

    out = [oc_ref[g] for g in groups]
    score = []
    for g in groups:
        sco = jnp.where(forced > 0.5, 1e9, jnp.where(blk * NSA_SEL_LEN <= t_tok, imp_ref[g], -1e9))
        score.append(jnp.where(blk < n_sel, sco, -jnp.inf))
    for _ in range(sel_k):
        for g in groups:
            mx = jnp.max(score[g], axis=0, keepdims=True)
            first = jnp.min(jnp.where(score[g] == mx, blkf, float(LANES)), axis=0, keepdims=True)
            score[g] = jnp.where(blkf == first, -jnp.inf, score[g])
    q2 = []
    for g in groups:
        picked = jnp.where(score[g] == -jnp.inf, 0.0, NSA_NEG)
        if n_sel < LANES:
            picked = jnp.where(blk < n_sel, picked, NSA_NEG)
        sel_q = jnp.transpose(picked)
        q2.append(jnp.concatenate([q[g], jnp.concatenate([sel_q] * HG, axis=0).astype(BF16)], axis=1))

    SEL, WIN = 0, 1

    def reset():
        m_ref[...] = jnp.full_like(m_ref, NSA_NEG)
        l_ref[...] = jnp.zeros_like(l_ref)
        acc_ref[...] = jnp.zeros_like(acc_ref)

    def tile_step(br, g, k, s, v_t):
        m_old = m_ref[br, g, k]
        m_new = jnp.maximum(m_old, jnp.max(s, axis=0, keepdims=True))
        alpha = jnp.exp2(m_old - m_new)
        p = jnp.exp2(s - m_new)
        l_ref[br, g, k] = l_ref[br, g, k] * alpha + jnp.sum(p, axis=0, keepdims=True)
        acc_ref[br, g, k] = (acc_ref[br, g, k] * alpha
                             + jnp.dot(v_t, p.astype(BF16), preferred_element_type=F32))
        m_ref[br, g, k] = m_new

    def finish(br, g):
        m = jnp.maximum(m_ref[br, g, 0], m_ref[br, g, 1])
        w0, w1 = jnp.exp2(m_ref[br, g, 0] - m), jnp.exp2(m_ref[br, g, 1] - m)
        return ((acc_ref[br, g, 0] * w0 + acc_ref[br, g, 1] * w1)
                * (1.0 / (l_ref[br, g, 0] * w0 + l_ref[br, g, 1] * w1)))

    def emit():
        heads_out = []
        for g in groups:
            o_t = out[g] + gate[g][1:2, :] * finish(SEL, g) + gate[g][2:3, :] * finish(WIN, g)
            heads_out += [jnp.transpose(o_t[:, hg * QB:(hg + 1) * QB]) for hg in range(HG)]
        o_ref[0] = jnp.concatenate(heads_out, axis=1).astype(o_ref.dtype)

    key_r = lax.broadcasted_iota(jnp.int32, (NSA_KT, QB), 0)
    key_q = lax.broadcasted_iota(jnp.int32, (NSA_KT, QB), 1)
    w_r = lax.broadcasted_iota(jnp.int32, (LANES, QB), 0)
    w_q = lax.broadcasted_iota(jnp.int32, (LANES, QB), 1)
    in_window = lanes4(jnp.where(w_r > w_q, 0.0, NSA_NEG))
    causal_w = lanes4(jnp.where(w_r <= w_q, 0.0, NSA_NEG))

    def sel_at(k0, n):
        k0 = pl.multiple_of(k0, LANES)
        return [(_dot_nt(ks_ref[0, g, pl.ds(k0, n), :], q2[g]), vs_ref[0, g, :, pl.ds(k0, n)])
                for g in groups]

    def win_at(k0, n):
        k0 = pl.multiple_of(k0, LANES)
        return [(_dot_nt(kw_ref[0, g, pl.ds(k0, n), :], q[g]), vw_ref[0, g, :, pl.ds(k0, n)])
                for g in groups]

    def delta_at(k0, n):
        return delta_t(kp_ref[pl.ds(pl.multiple_of(k0, LANES), n), :])

    def pair_scores(p, buf):
        for t in range(2):
            sc_t = sel_at((2 * p + t) * NSA_KT, NSA_KT)
            for g in groups:
                buf[g, t] = sc_t[g][0]

    def pair_softmax(p, buf):
        for t in range(2):
            k0 = pl.multiple_of((2 * p + t) * NSA_KT, NSA_KT)
            for g in groups:
                tile_step(SEL, g, t, buf[g, t], vs_ref[0, g, :, pl.ds(k0, NSA_KT)])

    def run_pairs(n_pairs):
        last = jnp.maximum(n_pairs - 1, 0)
        pair_scores(0, sa_ref)

        def two_pairs(j, carry):
            pair_scores(2 * j + 1, sb_ref)
            pair_softmax(2 * j, sa_ref)
            pair_scores(jnp.minimum(2 * j + 2, last), sa_ref)
            pair_softmax(2 * j + 1, sb_ref)
            return carry

        lax.fori_loop(0, n_pairs // 2, two_pairs, 0)

        @pl.when(n_pairs % 2 == 1)
        def _():
            pair_softmax(n_pairs - 1, sa_ref)

    @pl.when(std[qi] != 0)
    def _():
        reset()
        nb = (qi - 1) // 2
        run_pairs(nb // 2)

        @pl.when(nb % 2 == 1)
        def _():
            a = sel_at((nb - 1) * NSA_KT, NSA_KT)
            for g in groups:
                tile_step(SEL, g, 0, *a[g])

        @pl.when(qi % 2 == 0)
        def _():
            a = sel_at(nb * NSA_KT, LANES)
            for g in groups:
                tile_step(SEL, g, 1, *a[g])

        zeros = jnp.zeros((LANES, HG * QB), F32)
        sel_n = sel_at(s0 - LANES, 2 * LANES)
        win_p = win_at(s0 - NSA_WINDOW, NSA_WINDOW)
        win_d = win_at(s0, LANES)
        d_prev, d_diag = delta_at(s0 - LANES, LANES), delta_at(s0, LANES)
        for g in groups:
            bias = jnp.concatenate([d_prev[g], d_diag[g] + causal_w], axis=0)
            tile_step(SEL, g, 0, sel_n[g][0] + bias, sel_n[g][1])
        for g in groups:
            bias = jnp.concatenate([in_window] + [zeros] * (NSA_WTILES - 3) + [d_prev[g]], axis=0)
            tile_step(WIN, g, 0, win_p[g][0] + bias, win_p[g][1])
        for g in groups:
            tile_step(WIN, g, 1, win_d[g][0] + d_diag[g] + causal_w, win_d[g][1])
        emit()

    @pl.when(std[qi] == 0)
    def _():
        reset()
        n_past = (s0 + QB - 1) // NSA_KT
        n_pairs = nfast[qi] // 2
        run_pairs(n_pairs)

        def slow_body(kt, carry):
            a = sel_at(kt * NSA_KT, NSA_KT)
            d = delta_at(kt * NSA_KT, NSA_KT)
            for g in groups:
                tile_step(SEL, g, 1, a[g][0] + d[g], a[g][1])
            return carry

        lax.fori_loop(2 * n_pairs, n_past, slow_body, 0)
        k0 = n_past * NSA_KT
        a = sel_at(k0, NSA_KT)
        d = delta_at(k0, NSA_KT)
        causal = lanes4(jnp.where(k0 + key_r <= s0 + key_q, 0.0, NSA_NEG))
        for g in groups:
            tile_step(SEL, g, 0, a[g][0] + d[g] + causal, a[g][1])

        for j in range(NSA_WTILES):
            start = s0 - NSA_WINDOW + j * LANES

            @pl.when(start >= 0)
            def _():
                sw = win_at(start, LANES)
                d = maybe_delta(near_w[qi * NSA_WTILES + j],
                                kp_ref[pl.ds(pl.multiple_of(start, LANES), LANES), :])
                for g in groups:
                    s = sw[g][0] + d[g]
                    if j == 0:
                        s = s + in_window
                    elif j == NSA_WTILES - 1:
                        s = s + causal_w
                    tile_step(WIN, g, j % 2, s, sw[g][1])

        emit()


def _pos_digits(p, key_side):
    d0, d1, d2 = (p & 127).astype(F32), ((p >> 7) & 127).astype(F32), (p >> 14).astype(F32)
    one = jnp.ones_like(d0)
    if key_side:
        cols = [one, one, one, -d2, -d1, -d0]
    else:
        cols = [16384.0 * d2, 128.0 * d1, d0, 16384.0 * one, 128.0 * one, one]
    out = jnp.stack(cols, axis=-1)
    return jnp.pad(out, ((0, 0), (0, LANES - out.shape[-1]))).astype(BF16)


def nsa_mixer(proj, k_c, v_c, k_s, v_s, k_w, v_w, gate_logits, positions,
              ck_pe, ck_w1, ck_w2, cv_pe, cv_w1, cv_w2, rel_bias):
    k_c, v_c, k_s, v_s, k_w, v_w = lax.optimization_barrier((k_c, v_c, k_s, v_s, k_w, v_w))
    bs, s, _ = k_c.shape
    G, HG, DH, QB = NSA_KV_GROUPS, NSA_HPG, NSA_DH, NSA_QBLOCK
    nqt = s // QB
    nc = s // NSA_CMP_STRIDE
    n_cmp = (s - NSA_CMP_LEN) // NSA_CMP_STRIDE + 1
    n_sel = s // NSA_SEL_LEN
    sel_k = min(NSA_SEL_TOPK, n_sel)
    n_kt = s // NSA_KT
    n_ct = nc // LANES
    assert s % (LANES * NSA_CMP_STRIDE) == 0 and n_sel <= LANES and NSA_KT == 2 * QB

    def by_group(t):
        return t.reshape(bs, s, G, DH).transpose(0, 2, 1, 3).astype(F32)

    def by_group_t(t):
        return t.reshape(bs, s, G, DH).transpose(0, 2, 3, 1).astype(BF16)

    ones2 = jnp.zeros((LANES - DH,), F32).at[:2].set(1.0)

    def keys_aug(t, blocks=False):
        parts = [t, jnp.broadcast_to(ones2, t.shape[:-1] + (LANES - DH,))]
        if blocks:
            onehot = (jnp.arange(s)[:, None] // NSA_SEL_LEN == jnp.arange(LANES)[None, :]).astype(F32)
            parts.append(jnp.broadcast_to(onehot, t.shape[:-2] + (s, LANES)))
        return jnp.concatenate(parts, axis=-1).astype(BF16)

    chunks = jnp.stack([by_group(k_c), by_group(v_c)]).reshape(2, bs, G, nc, NSA_CMP_STRIDE * DH)
    pe = jnp.stack([ck_pe, cv_pe]).reshape(2, 1, NSA_CMP_LEN * DH)
    cmp = nsa_compress(chunks.astype(BF16), jnp.broadcast_to(pe, (2, 8, NSA_CMP_LEN * DH)).astype(BF16),
                       jnp.stack([ck_w1, cv_w1]).astype(BF16), jnp.stack([ck_w2, cv_w2]).astype(BF16))
    kc, vc_t = keys_aug(cmp[0]), cmp[1].transpose(0, 1, 3, 2).astype(BF16)

    rel_bias = rel_bias.astype(F32) * math.log2(math.e)
    far = rel_bias[REL_BUCKETS - 1]
    far_hi = far.astype(BF16).astype(F32)
    extra = jnp.zeros((NSA_HEADS, LANES - DH), F32).at[:, 0].set(far_hi).at[:, 1].set(far - far_hi)
    extra = jnp.broadcast_to(extra.reshape(G, HG, 1, LANES - DH), (G, HG, QB, LANES - DH))
    extra = extra.reshape(G, HG * QB, LANES - DH)
    gl = gate_logits.astype(F32).reshape(bs, nqt, QB, G, HG, 3).transpose(0, 3, 1, 5, 4, 2)
    gl = jnp.pad(gl.reshape(bs, G, nqt, 3, HG * QB), ((0, 0),) * 3 + ((0, 5), (0, 0)))

    td = (rel_bias[_bucket_table()] - rel_bias[REL_BUCKETS - 1][None, :]).T.astype(F32)
    cmp_end = jnp.minimum(jnp.arange(nc) * NSA_CMP_STRIDE + NSA_CMP_LEN - 1, s - 1)
    cpos = positions[cmp_end]
    qmin = positions.reshape(nqt, QB).min(axis=1)
    kmax = positions.reshape(nqt, QB).max(axis=1)
    near_s = (qmin[:, None] - positions.reshape(n_kt, NSA_KT).max(axis=1)[None, :]) < REL_MAX_DIST
    n_past = (jnp.arange(nqt) * QB + QB - 1) // NSA_KT
    n_fast = jnp.minimum(jnp.argmax(jnp.concatenate([near_s, jnp.ones((nqt, 1), bool)], axis=1), axis=1),
                         n_past)
    wt = jnp.clip(jnp.arange(nqt)[:, None] - (NSA_WTILES - 1) + jnp.arange(NSA_WTILES)[None, :], 0, nqt - 1)
    near_w = (qmin[:, None] - kmax[wt]) < REL_MAX_DIST
    near_c = (qmin[:, None] - cpos.reshape(n_ct, LANES).max(axis=1)[None, :]) < REL_MAX_DIST
    before = lax.cummax(kmax)[jnp.maximum(jnp.arange(nqt) - 2, 0)]
    std = (jnp.arange(nqt) >= NSA_WTILES - 1) & (qmin - before >= REL_MAX_DIST)

    cmp_start = np.arange(nc) * NSA_CMP_STRIDE
    sel_start = np.arange(LANES) * NSA_SEL_LEN
    agg_t = ((cmp_start[None, :] <= sel_start[:, None] + NSA_SEL_LEN - 1)
             & (cmp_start[None, :] + NSA_CMP_LEN - 1 >= sel_start[:, None])
             & (np.arange(nc)[None, :] < n_cmp) & (np.arange(LANES)[:, None] < n_sel))

    cols = HG * QB
    full = lambda shape: pl.BlockSpec(shape, lambda b, i, *_: (0,) * len(shape))
    per_b = lambda n, w: pl.BlockSpec((1, G, n, w), lambda b, i, *_: (b, 0, 0, 0))
    per_tile = lambda n, w: pl.BlockSpec((1, G, 1, n, w), lambda b, i, *_: (b, 0, i, 0, 0))
    grid_spec = pltpu.PrefetchScalarGridSpec(
        num_scalar_prefetch=4,
        grid=(bs, nqt),
        in_specs=[pl.BlockSpec((QB, NSA_Q), lambda b, i, *_: (b * nqt + i, OFF_NQ // NSA_Q)),
                  full((G, cols, LANES - DH)), per_tile(8, cols),
                  pl.BlockSpec((QB, LANES), lambda b, i, *_: (i, 0)),
                  full((s, LANES)), full((nc, LANES)),
                  per_b(s, 2 * LANES), per_b(DH, s), per_b(s, LANES), per_b(DH, s),
                  per_b(nc, LANES), per_b(DH, nc),
                  full((LANES, nc)), full((NSA_HEADS, LANES))],
        out_specs=pl.BlockSpec((1, QB, NSA_Q), lambda b, i, *_: (b, i, 0)),
        scratch_shapes=[pltpu.VMEM((2, G, 2, 1, cols), F32), pltpu.VMEM((2, G, 2, 1, cols), F32),
                        pltpu.VMEM((2, G, 2, DH, cols), F32)]
        + [pltpu.VMEM((G, 2, NSA_KT, cols), F32)] * 2 + [pltpu.VMEM((G, nc, cols), F32)]
        + [pltpu.VMEM((G, DH, cols), F32), pltpu.VMEM((G, LANES, QB), F32)],
    )
    return pl.pallas_call(
        functools.partial(_nsa_body, n_sel=n_sel, sel_k=sel_k),
        grid_spec=grid_spec,
        out_shape=jax.ShapeDtypeStruct((bs, s, NSA_Q), BF16),
        compiler_params=_params("parallel", "arbitrary"),
        name="nsa_attention",
    )(n_fast.astype(jnp.int32), std.astype(jnp.int32), near_w.reshape(-1).astype(jnp.int32),
      near_c.reshape(-1).astype(jnp.int32),
      proj, extra, gl, _pos_digits(positions, False), _pos_digits(positions, True), _pos_digits(cpos, True),
      keys_aug(by_group(k_s), blocks=True), by_group_t(v_s), keys_aug(by_group(k_w)), by_group_t(v_w),
      kc, vc_t, jnp.asarray(agg_t, BF16), td)


SGU_TC = 512


def _sgu_body(uv_ref, lg_ref, lb_ref, w_ref, b_ref, o_ref):
    uv = jax.nn.gelu(uv_ref[...].astype(F32))
    u, v = uv[:, :SGU_WIDTH], uv[:, SGU_WIDTH:]
    mu = jnp.mean(v, axis=-1, keepdims=True)
    vc = v - mu
    var = jnp.mean(vc * vc, axis=-1, keepdims=True)
    vn = (vc * lax.rsqrt(var + RMS_EPS) * lg_ref[...] + lb_ref[...]).astype(BF16)
    gw = SGU_WIDTH // SGU_GROUPS
    for c in range(SGU_TC // SGU_CHUNK):
        rows = slice(c * SGU_CHUNK, (c + 1) * SGU_CHUNK)
        for g in range(SGU_GROUPS):
            cols = slice(g * gw, (g + 1) * gw)
            mix = jnp.dot(w_ref[g], vn[rows, cols], preferred_element_type=F32) + b_ref[:, g:g + 1]
            o_ref[rows, cols] = (u[rows, cols] * mix).astype(o_ref.dtype)


def sgu_mixer(proj, ln_g, ln_b, w_s, b_s):
    t = proj.shape[0]
    assert t % SGU_TC == 0 and OFF_SGU % (2 * SGU_WIDTH) == 0
    fixed = lambda i: (0, 0)
    return pl.pallas_call(
        _sgu_body,
        grid=(t // SGU_TC,),
        in_specs=[pl.BlockSpec((SGU_TC, 2 * SGU_WIDTH), lambda i: (i, OFF_SGU // (2 * SGU_WIDTH))),
                  pl.BlockSpec((1, SGU_WIDTH), fixed), pl.BlockSpec((1, SGU_WIDTH), fixed),
                  pl.BlockSpec((SGU_GROUPS, SGU_CHUNK, SGU_CHUNK), lambda i: (0, 0, 0)),
                  pl.BlockSpec((SGU_CHUNK, SGU_GROUPS), fixed)],
        out_specs=pl.BlockSpec((SGU_TC, SGU_WIDTH), lambda i: (i, 0)),
        out_shape=jax.ShapeDtypeStruct((t, SGU_WIDTH), BF16),
        compiler_params=_params("parallel"),
        name="sgu_gate",
    )(proj, ln_g.reshape(1, SGU_WIDTH), ln_b.reshape(1, SGU_WIDTH), jnp.tril(w_s).astype(BF16), b_s.T)


def _permute_w_in(w):
    sizes = ([SSM_WIDTH, GDN_QKV, GDN_HEADS, GDN_HEADS, GDN_HEADS * GDN_DV, NSA_Q]
             + [NSA_KV] * 6 + [3 * NSA_HEADS, 2 * SGU_WIDTH, N_BRANCH * D_MODEL])
    cuts = [int(c) for c in np.cumsum(sizes)[:-1]]
    (w_ssm, w_qkv, w_a, w_b, w_z, w_nq, w_kc, w_vc, w_ks, w_vs, w_kw, w_vw,
     w_ng, w_sgu, w_gate) = jnp.split(w, cuts, axis=-1)
    pad = jnp.zeros((w.shape[0], PROJ_COLS - OFF_SMALL - SMALL_USED), w.dtype)
    return jnp.concatenate([w_qkv, w_ssm, w_z, w_nq, w_sgu, w_gate, w_kc, w_vc, w_ks, w_vs,
                            w_kw, w_vw, w_a, w_b, w_ng, pad], axis=-1).astype(BF16)


def kernel(x, positions, norm_mix, norm_ffn, norm_final, w_in, ssm_a_re, ssm_a_im, ssm_log_dt, ssm_b_re, ssm_b_im, ssm_c_re, ssm_c_im, ssm_d, ssm_glu_w, gdn_conv_w, gdn_a_log, gdn_dt_bias, gdn_norm, nsa_cmp_k_pe, nsa_cmp_k_w1, nsa_cmp_k_w2, nsa_cmp_v_pe, nsa_cmp_v_w1, nsa_cmp_v_w2, rel_bias, sgu_ln_g, sgu_ln_b, sgu_w, sgu_b, w_br_ssm, w_br_gdn, w_br_nsa, w_br_sgu, w_out, ffn_w_gate, ffn_w_up, ffn_w_down, moe_router, moe_w_gate, moe_w_up, moe_w_down):
    bs, s, d = x.shape
    t = bs * s
    assert DEPTH == 2
    xf = x.reshape(t, d)
    for layer in range(DEPTH):
        proj = in_proj(xf, norm_mix[layer], _permute_w_in(w_in[layer]))
        p3 = proj.reshape(bs, s, PROJ_COLS)

        def seg(off, width):
            return p3[..., off:off + width]

        y_ssm = ssm_mixer(proj, bs, ssm_a_re[layer], ssm_a_im[layer], ssm_log_dt[layer],
                          ssm_b_re[layer], ssm_b_im[layer], ssm_c_re[layer], ssm_c_im[layer],
                          ssm_d[layer], ssm_glu_w[layer])
        y_gdn = gdn_mixer(proj, bs, gdn_conv_w[layer], gdn_a_log[layer], gdn_dt_bias[layer],
                          gdn_norm[layer])
        kvs = [seg(OFF_KV + i * NSA_KV, NSA_KV) for i in range(6)]
        y_nsa = nsa_mixer(proj, *kvs, seg(OFF_SMALL + 2 * GDN_HEADS, 3 * NSA_HEADS),
                          positions, nsa_cmp_k_pe[layer], nsa_cmp_k_w1[layer], nsa_cmp_k_w2[layer],
                          nsa_cmp_v_pe[layer], nsa_cmp_v_w1[layer], nsa_cmp_v_w2[layer], rel_bias)
        y_sgu = sgu_mixer(proj, sgu_ln_g[layer], sgu_ln_b[layer], sgu_w[layer], sgu_b[layer])
        ys = [y_ssm, y_gdn, y_nsa.reshape(t, -1), y_sgu]
        ws = [w[layer].astype(BF16) for w in (w_br_ssm, w_br_gdn, w_br_nsa, w_br_sgu)]
        xf = merge(xf, proj, ys, ws, w_out[layer].astype(BF16))
        i = layer // 2
        if layer % 2 == 0:
            xf = dense_ffn(xf, norm_ffn[layer], ffn_w_gate[i], ffn_w_up[i], ffn_w_down[i])
        else:
            xf = moe_layer(xf, norm_ffn[layer], moe_router[i], moe_w_gate[i], moe_w_up[i],
                           moe_w_down[i], norm_final)
    return xf.reshape(bs, s, d)
```
